```python
import jax, jax.numpy as jnp
from jax import lax
import numpy as np

D_MODEL = 1024
BATCH = 4
SEQ = 4096
DEPTH = 1

D_MIX = D_MODEL
MLSTM_WIDTH = D_MIX // 2
MLSTM_HEADS = 4
MLSTM_HEAD_DIM = MLSTM_WIDTH // MLSTM_HEADS
MLSTM_CHUNK = 64
CONV_WIDTH = 4
POOL_WIDTH = D_MIX - MLSTM_WIDTH
POOL_WINDOWS = (2, 4, 8, 16)
POOL_GROUPS = len(POOL_WINDOWS)
POOL_GROUP_DIM = POOL_WIDTH // POOL_GROUPS
N_IN = 4 * MLSTM_WIDTH + 2 * MLSTM_HEADS + POOL_WIDTH
N_EXPERTS = 32
TOP_K = 4
D_FF = D_MODEL
SWIGLU_LIMIT = 7.0
SWIGLU_ALPHA = 1.702
EXPERT_BLOCK = 128
EPS = 1e-5

kernel_name = "hybrid_mlstm_pool_moe_block"


def rmsnorm(x, g):
    xf = x.astype(jnp.float32)
    y = xf * lax.rsqrt(jnp.mean(xf * xf, axis=-1, keepdims=True) + EPS)
    return (y * g.astype(jnp.float32)).astype(x.dtype)


def causal_depthwise_conv(u, w):
    k = w.shape[0]
    return lax.conv_general_dilated(
        u, w[:, None, :].astype(u.dtype), window_strides=(1,), padding=[(k - 1, 0)],
        dimension_numbers=("NWC", "WIO", "NWC"), feature_group_count=u.shape[-1])


def mlstm_chunkwise(q, k, v, ig, lf):
    B, H, S, Dh = q.shape
    L = MLSTM_CHUNK
    NC = S // L
    q = q.reshape(B, H, NC, L, Dh)
    k = k.reshape(B, H, NC, L, Dh)
    v = v.reshape(B, H, NC, L, Dh)
    ig = ig.reshape(B, H, NC, L)
    lf = lf.reshape(B, H, NC, L)
    b = jnp.cumsum(lf, axis=-1)
    b_tot = b[..., -1]

    g = b_tot[..., None] - b + ig
    m_loc = jnp.max(g, axis=-1)
    w_loc = jnp.exp(g - m_loc[..., None])
    C_loc = jnp.einsum("bhclk,bhclv->bhckv", w_loc[..., None] * k, v)
    n_loc = jnp.einsum("bhcl,bhclk->bhck", w_loc, k)

    def step(carry, inp):
        C, n, m = carry
        C_l, n_l, m_l, bt = inp
        m_new = jnp.maximum(bt + m, m_l)
        s_old = jnp.exp(bt + m - m_new)
        s_loc = jnp.exp(m_l - m_new)
        C_new = s_old[..., None, None] * C + s_loc[..., None, None] * C_l
        n_new = s_old[..., None] * n + s_loc[..., None] * n_l
        return (C_new, n_new, m_new), (C, n, m)

    init = (jnp.zeros((B, H, Dh, Dh), jnp.float32),
            jnp.zeros((B, H, Dh), jnp.float32),
            jnp.zeros((B, H), jnp.float32))
    xs = (jnp.moveaxis(C_loc, 2, 0), jnp.moveaxis(n_loc, 2, 0),
          jnp.moveaxis(m_loc, 2, 0), jnp.moveaxis(b_tot, 2, 0))
    _, (C_in, n_in, m_in) = lax.scan(step, init, xs)
    C_in = jnp.moveaxis(C_in, 0, 2)
    n_in = jnp.moveaxis(n_in, 0, 2)
    m_in = jnp.moveaxis(m_in, 0, 2)

    causal = jnp.tril(jnp.ones((L, L), dtype=bool))
    log_d = b[..., :, None] - b[..., None, :] + ig[..., None, :]
    log_d = jnp.where(causal, log_d, -jnp.inf)
    a = b + m_in[..., None]
    m_out = jnp.maximum(a, jnp.max(log_d, axis=-1))
    s = jnp.einsum("bhcld,bhcrd->bhclr", q, k) * jnp.exp(log_d - m_out[..., None])
    inter = jnp.exp(a - m_out)
    num = (jnp.einsum("bhclr,bhcrv->bhclv", s, v)
           + inter[..., None] * jnp.einsum("bhcld,bhcdv->bhclv", q, C_in))
    den = jnp.sum(s, axis=-1) + inter * jnp.einsum("bhcld,bhcd->bhcl", q, n_in)
    h = num / jnp.maximum(jnp.abs(den), jnp.exp(-m_out))[..., None]
    return h.reshape(B, H, S, Dh)


def multiscale_pool(u, pool_w, pool_scale):
    B, S, _ = u.shape
    uf = u.astype(jnp.float32).reshape(B, S, POOL_GROUPS, POOL_GROUP_DIM)
    cs = jnp.concatenate([jnp.zeros((B, 1, POOL_GROUPS, POOL_GROUP_DIM), jnp.float32),
                          jnp.cumsum(uf, axis=1)], axis=1)
    t = jnp.arange(S)
    pooled = []
    for gi, w in enumerate(POOL_WINDOWS):
        c = cs[:, :, gi]
        c_lo = jnp.concatenate([jnp.zeros((B, w - 1, POOL_GROUP_DIM), jnp.float32),
                                c[:, :S - w + 1]], axis=1)
        cnt = jnp.minimum(t + 1, w).astype(jnp.float32)
        pooled.append((c[:, 1:] - c_lo) / cnt[None, :, None])
    pooled = jnp.stack(pooled, axis=2) - uf
    mixed = jnp.einsum("bsgc,gcd->bsgd", pooled, pool_w.astype(jnp.float32))
    mixed = mixed.reshape(B, S, POOL_WIDTH) * pool_scale.astype(jnp.float32)
    return mixed.astype(u.dtype)


def hybrid_mixer(h, w_in, ig_b, fg_b, conv_w, head_norm_g, pool_w, pool_scale, w_out):
    B, S, _ = h.shape
    W, H, Dh = MLSTM_WIDTH, MLSTM_HEADS, MLSTM_HEAD_DIM
    p = h @ w_in
    qk = p[..., :2 * W]
    v = p[..., 2 * W:3 * W]
    o = p[..., 3 * W:4 * W]
    gates = p[..., 4 * W:4 * W + 2 * H].astype(jnp.float32)
    u = p[..., 4 * W + 2 * H:]

    qk = jax.nn.silu(causal_depthwise_conv(qk, conv_w))

    def to_heads(t):
        return t.reshape(B, S, H, Dh).transpose(0, 2, 1, 3).astype(jnp.float32)

    q = to_heads(qk[..., :W])
    k = to_heads(qk[..., W:]) * (Dh ** -0.5)
    vh = to_heads(v)
    ig = (gates[..., :H] + ig_b.astype(jnp.float32)).transpose(0, 2, 1)
    lf = jax.nn.log_sigmoid(gates[..., H:] + fg_b.astype(jnp.float32)).transpose(0, 2, 1)
    hm = mlstm_chunkwise(q, k, vh, ig, lf)
    mu = jnp.mean(hm, axis=-1, keepdims=True)
    var = jnp.mean(jnp.square(hm - mu), axis=-1, keepdims=True)
    hm = (hm - mu) * lax.rsqrt(var + EPS)
    hm = hm.transpose(0, 2, 1, 3).reshape(B, S, W) * head_norm_g.astype(jnp.float32)
    y_m = (jax.nn.sigmoid(o.astype(jnp.float32)) * hm).astype(h.dtype)

    y_p = multiscale_pool(u, pool_w, pool_scale)

    return jnp.concatenate([y_m, y_p], axis=-1) @ w_out


def moe_ffn(h, w_router, b_router, w_gate, b_gate, w_up, b_up, w_down, b_down):
    B, S, D = h.shape
    T = B * S
    xt = h.reshape(T, D)
    logits = (xt @ w_router + b_router).astype(jnp.float32)
    top_vals, top_idx = lax.top_k(logits, TOP_K)
    top_gates = jax.nn.softmax(top_vals, axis=-1)

    A = T * TOP_K
    flat_expert = top_idx.reshape(A)
    flat_token = jnp.repeat(jnp.arange(T, dtype=jnp.int32), TOP_K)
    flat_gate = top_gates.reshape(A)
    order = jnp.argsort(flat_expert)
    s_expert = flat_expert[order]
    s_token = flat_token[order]
    s_gate = flat_gate[order]

    counts = jax.ops.segment_sum(jnp.ones((A,), jnp.int32), flat_expert, num_segments=N_EXPERTS)
    group_start = jnp.cumsum(counts) - counts
    padded = ((counts + EXPERT_BLOCK - 1) // EXPERT_BLOCK) * EXPERT_BLOCK
    padded_end = jnp.cumsum(padded)
    padded_start = padded_end - padded
    rank = jnp.arange(A, dtype=jnp.int32) - group_start[s_expert]
    dest = padded_start[s_expert] + rank

    n_blocks = -(-A // EXPERT_BLOCK) + N_EXPERTS
    P = n_blocks * EXPERT_BLOCK
    tok_buf = jnp.full((P,), T, jnp.int32).at[dest].set(s_token)
    gate_buf = jnp.zeros((P,), jnp.float32).at[dest].set(s_gate)
    block_start = jnp.arange(n_blocks, dtype=jnp.int32) * EXPERT_BLOCK
    block_expert = jnp.minimum(jnp.searchsorted(padded_end, block_start, side="right"),
                               N_EXPERTS - 1).astype(jnp.int32)

    x_pad = jnp.concatenate([xt, jnp.zeros((1, D), xt.dtype)], axis=0)
    x_buf = x_pad[tok_buf].reshape(n_blocks, EXPERT_BLOCK, D)

    def expert_block(args):
        xb, e = args
        gate = xb @ w_gate[e] + b_gate[e]
        up = xb @ w_up[e] + b_up[e]
        gate = jnp.minimum(gate, SWIGLU_LIMIT)
        up = jnp.clip(up, -SWIGLU_LIMIT, SWIGLU_LIMIT)
        glu = gate * jax.nn.sigmoid(SWIGLU_ALPHA * gate)
        return (glu * (up + 1.0)) @ w_down[e] + b_down[e]

    y_buf = lax.map(expert_block, (x_buf, block_expert)).reshape(P, D)
    y_buf = y_buf * gate_buf[:, None].astype(y_buf.dtype)
    out = jax.ops.segment_sum(y_buf, tok_buf, num_segments=T + 1)[:T]
    return out.reshape(B, S, D).astype(h.dtype)


def setup_inputs(seed: int = 0) -> dict:
    key = jax.random.key(seed)
    ks = jax.random.split(key, 20)
    f32 = jnp.float32
    L_, D, E, F, H = DEPTH, D_MODEL, N_EXPERTS, D_FF, MLSTM_HEADS
    nrm = lambda k, shape, s: jax.random.normal(k, shape, f32) * s
    x = jax.random.normal(ks[0], (BATCH, SEQ, D), f32)
    norm1_g = 1.0 + nrm(ks[1], (L_, D), 0.02)
    w_in = nrm(ks[2], (L_, D, N_IN), D ** -0.5)
    ig_b = nrm(ks[3], (L_, H), 0.1)
    fg_b = jnp.broadcast_to(jnp.linspace(3.0, 6.0, H, dtype=f32), (L_, H)) + nrm(ks[4], (L_, H), 0.1)
    conv_w = nrm(ks[5], (L_, CONV_WIDTH, 2 * MLSTM_WIDTH), CONV_WIDTH ** -0.5)
    head_norm_g = 1.0 + nrm(ks[6], (L_, MLSTM_WIDTH), 0.02)
    pool_w = nrm(ks[7], (L_, POOL_GROUPS, POOL_GROUP_DIM, POOL_GROUP_DIM), POOL_GROUP_DIM ** -0.5)
    pool_scale = 1.0 + nrm(ks[8], (L_, POOL_WIDTH), 0.02)
    w_out = nrm(ks[9], (L_, D_MIX, D), D_MIX ** -0.5)
    norm2_g = 1.0 + nrm(ks[10], (L_, D), 0.02)
    w_router = nrm(ks[11], (L_, D, E), D ** -0.5)
    b_router = nrm(ks[12], (L_, E), 0.01)
    w_gate = nrm(ks[13], (L_, E, D, F), D ** -0.5)
    b_gate = nrm(ks[14], (L_, E, F), 0.01)
    w_up = nrm(ks[15], (L_, E, D, F), D ** -0.5)
    b_up = nrm(ks[16], (L_, E, F), 0.01)
    w_down = nrm(ks[17], (L_, E, F, D), F ** -0.5)
    b_down = nrm(ks[18], (L_, E, D), 0.01)
    normf_g = 1.0 + nrm(ks[19], (D,), 0.02)
    return {"x": x, "norm1_g": norm1_g, "w_in": w_in, "ig_b": ig_b, "fg_b": fg_b,
            "conv_w": conv_w, "head_norm_g": head_norm_g, "pool_w": pool_w,
            "pool_scale": pool_scale, "w_out": w_out, "norm2_g": norm2_g,
            "w_router": w_router, "b_router": b_router, "w_gate": w_gate,
            "b_gate": b_gate, "w_up": w_up, "b_up": b_up, "w_down": w_down,
            "b_down": b_down, "normf_g": normf_g}


def reference(x, norm1_g, w_in, ig_b, fg_b, conv_w, head_norm_g, pool_w, pool_scale,
              w_out, norm2_g, w_router, b_router, w_gate, b_gate, w_up, b_up,
              w_down, b_down, normf_g):
    for l in range(DEPTH):
        h = rmsnorm(x, norm1_g[l])
        x = x + hybrid_mixer(h, w_in[l], ig_b[l], fg_b[l], conv_w[l], head_norm_g[l],
                             pool_w[l], pool_scale[l], w_out[l])
        h = rmsnorm(x, norm2_g[l])
        x = x + moe_ffn(h, w_router[l], b_router[l], w_gate[l], b_gate[l], w_up[l],
                        b_up[l], w_down[l], b_down[l])
    return rmsnorm(x, normf_g)
```

```python
import functools

import jax
import jax.numpy as jnp
from jax import lax
from jax.experimental import pallas as pl
from jax.experimental.pallas import tpu as pltpu

D_MODEL = 1024
MLSTM_WIDTH = 512
MLSTM_HEADS = 4
HEAD_DIM = 128
CONV_WIDTH = 4
POOL_WIDTH = 512
POOL_WINDOWS = (2, 4, 8, 16)
POOL_GROUP_DIM = 128
N_EXPERTS = 32
TOP_K = 4
D_FF = 1024
SWIGLU_LIMIT = 7.0
SWIGLU_ALPHA = 1.702
EPS = 1e-5

N_MAIN = 4 * MLSTM_WIDTH + POOL_WIDTH
N_GATES = 2 * MLSTM_HEADS

LANES = 128
BF16_SUBLANES = 16
VMEM_LIMIT = 56 * 1024 * 1024

TM_PROJ = 512
CHUNK = 256
HALO = 16
TM_EXPERT = 256

NT_DIMS = (((1,), (1,)), ((), ()))
TN_DIMS = (((0,), (0,)), ((), ()))


def _sigmoid(x):
    return 1.0 / (1.0 + jnp.exp(-x))


def _in_proj_kernel(x_ref, g_ref, w_ref, wgt_ref, p_ref, gt_ref):
    x = x_ref[...]
    h = x * lax.rsqrt(jnp.mean(x * x, axis=-1, keepdims=True) + EPS) * g_ref[...]
    hb = h.astype(jnp.bfloat16)
    p_ref[...] = jnp.dot(hb, w_ref[...], preferred_element_type=jnp.float32).astype(p_ref.dtype)
    gt = lax.dot_general(wgt_ref[...], hb, NT_DIMS, preferred_element_type=jnp.float32)
    gt_ref[...] = gt[:N_GATES]


def _in_proj(x2, g1, w_main, wg_t):
    T = x2.shape[0]
    return pl.pallas_call(
        _in_proj_kernel,
        grid=(T // TM_PROJ,),
        in_specs=[
            pl.BlockSpec((TM_PROJ, D_MODEL), lambda i: (i, 0)),
            pl.BlockSpec((1, D_MODEL), lambda i: (0, 0)),
            pl.BlockSpec((D_MODEL, N_MAIN), lambda i: (0, 0)),
            pl.BlockSpec((BF16_SUBLANES, D_MODEL), lambda i: (0, 0)),
        ],
        out_specs=[
            pl.BlockSpec((TM_PROJ, N_MAIN), lambda i: (i, 0)),
            pl.BlockSpec((N_GATES, TM_PROJ), lambda i: (0, i)),
        ],
        out_shape=[
            jax.ShapeDtypeStruct((T, N_MAIN), jnp.bfloat16),
            jax.ShapeDtypeStruct((N_GATES, T), jnp.float32),
        ],
        compiler_params=pltpu.CompilerParams(
            dimension_semantics=("parallel",), vmem_limit_bytes=VMEM_LIMIT),
        name="in_proj",
    )(x2, g1, w_main, wg_t)


def _mlstm_kernel(qk_ref, qkp_ref, v_ref, o_ref, gt_ref, convw_ref, gb_ref, hng_ref,
                  tri_ref, y_ref, ext_ref, cn_ref, m_ref):
    L = CHUNK
    c = pl.program_id(1)

    @pl.when(c == 0)
    def _():
        cn_ref[...] = jnp.zeros_like(cn_ref)
        m_ref[...] = jnp.zeros_like(m_ref)

    halo = qkp_ref[...].astype(jnp.float32)
    ext_ref[0:HALO, :] = jnp.where(c > 0, halo, 0.0)
    ext_ref[HALO:HALO + L, :] = qk_ref[...].astype(jnp.float32)
    acc = None
    for j in range(CONV_WIDTH):
        off = HALO - (CONV_WIDTH - 1) + j
        term = convw_ref[j:j + 1, :] * ext_ref[off:off + L, :]
        acc = term if acc is None else acc + term
    qk = acc * _sigmoid(acc)

    gt = gt_ref[...] + gb_ref[...]
    f = gt[MLSTM_HEADS:]
    lf = jnp.minimum(f, 0.0) - jnp.log(1.0 + jnp.exp(-jnp.abs(f)))
    ig = gt[:MLSTM_HEADS]
    rows = jnp.concatenate([lf, ig], axis=0)
    tri = tri_ref[...]
    cols = lax.dot_general(tri, rows, NT_DIMS, precision=lax.Precision.HIGHEST,
                           preferred_element_type=jnp.float32)
    b_cols = cols[:L, :MLSTM_HEADS]
    ig_cols = cols[L:, MLSTM_HEADS:]
    b_rows = lax.dot_general(lf, tri[:L], NT_DIMS, precision=lax.Precision.HIGHEST,
                             preferred_element_type=jnp.float32)
    c_rows = ig - b_rows

    row_id = lax.broadcasted_iota(jnp.int32, (L, L), 0)
    col_id = lax.broadcasted_iota(jnp.int32, (L, L), 1)
    causal = col_id <= row_id
    ones_col = (lax.broadcasted_iota(jnp.int32, (L, HEAD_DIM), 1) == 0).astype(jnp.bfloat16)

    for h in range(MLSTM_HEADS):
        lo = h * HEAD_DIM
        q = qk[:, lo:lo + HEAD_DIM].astype(jnp.bfloat16)
        k_f = qk[:, MLSTM_WIDTH + lo:MLSTM_WIDTH + lo + HEAD_DIM] * (HEAD_DIM ** -0.5)
        v_ext = jnp.concatenate([v_ref[:, lo:lo + HEAD_DIM], ones_col], axis=1)
        b_col = b_cols[:, h:h + 1]
        ig_col = ig_cols[:, h:h + 1]
        c_row = c_rows[h:h + 1, :]
        b_tot = b_rows[h:h + 1, L - 1:L]
        m_in = m_ref[h][0:1, 0:1]
        cn = cn_ref[h]

        s_qk = lax.dot_general(q, k_f.astype(jnp.bfloat16), NT_DIMS,
                               preferred_element_type=jnp.float32)
        log_d = jnp.where(causal, b_col + c_row, -jnp.inf)
        a_col = b_col + m_in
        m_out = jnp.maximum(a_col, jnp.max(log_d, axis=-1, keepdims=True))
        s = (s_qk * jnp.exp(log_d - m_out)).astype(jnp.bfloat16)
        inter = jnp.exp(a_col - m_out)
        num = (jnp.dot(s, v_ext, preferred_element_type=jnp.float32)
               + inter * jnp.dot(q, cn.astype(jnp.bfloat16), preferred_element_type=jnp.float32))
        den = num[:, HEAD_DIM:HEAD_DIM + 1]
        hh = num[:, :HEAD_DIM] / jnp.maximum(jnp.abs(den), jnp.exp(-m_out))

        mu = jnp.mean(hh, axis=-1, keepdims=True)
        d = hh - mu
        var = jnp.mean(d * d, axis=-1, keepdims=True)
        hn = d * lax.rsqrt(var + EPS) * hng_ref[:, lo:lo + HEAD_DIM]
        og = _sigmoid(o_ref[:, lo:lo + HEAD_DIM].astype(jnp.float32))
        y_ref[:, lo:lo + HEAD_DIM] = (og * hn).astype(y_ref.dtype)

        g_col = b_tot - b_col + ig_col
        m_loc = jnp.max(g_col, axis=0, keepdims=True)
        kw = (k_f * jnp.exp(g_col - m_loc)).astype(jnp.bfloat16)
        c_loc = lax.dot_general(kw, v_ext, TN_DIMS, preferred_element_type=jnp.float32)
        m_new = jnp.maximum(b_tot + m_in, m_loc)
        s_old = jnp.exp(b_tot + m_in - m_new)
        s_loc = jnp.exp(m_loc - m_new)
        cn_ref[h] = s_old * cn + s_loc * c_loc
        m_ref[h] = jnp.broadcast_to(m_new, m_ref.shape[1:])


def _mlstm(p, gates_t, conv_w, gate_b, hn_g, tri, batch, seq):
    T = batch * seq
    L = CHUNK
    nc = seq // L
    halo_per_chunk = L // HALO

    def cur(bi, ci):
        return bi * nc + ci

    return pl.pallas_call(
        _mlstm_kernel,
        grid=(batch, nc),
        in_specs=[
            pl.BlockSpec((L, 2 * MLSTM_WIDTH), lambda bi, ci: (cur(bi, ci), 0)),
            pl.BlockSpec((HALO, 2 * MLSTM_WIDTH),
                         lambda bi, ci: (jnp.maximum(cur(bi, ci) * halo_per_chunk - 1, 0), 0)),
            pl.BlockSpec((L, MLSTM_WIDTH), lambda bi, ci: (cur(bi, ci), 2)),
            pl.BlockSpec((L, MLSTM_WIDTH), lambda bi, ci: (cur(bi, ci), 3)),
            pl.BlockSpec((N_GATES, L), lambda bi, ci: (0, cur(bi, ci))),
            pl.BlockSpec((CONV_WIDTH, 2 * MLSTM_WIDTH), lambda bi, ci: (0, 0)),
            pl.BlockSpec((N_GATES, 1), lambda bi, ci: (0, 0)),
            pl.BlockSpec((1, MLSTM_WIDTH), lambda bi, ci: (0, 0)),
            pl.BlockSpec((2 * L, L), lambda bi, ci: (0, 0)),
        ],
        out_specs=pl.BlockSpec((L, MLSTM_WIDTH), lambda bi, ci: (cur(bi, ci), 0)),
        out_shape=jax.ShapeDtypeStruct((T, MLSTM_WIDTH), jnp.bfloat16),
        scratch_shapes=[
            pltpu.VMEM((HALO + L, 2 * MLSTM_WIDTH), jnp.float32),
            pltpu.VMEM((MLSTM_HEADS, HEAD_DIM, 2 * HEAD_DIM), jnp.float32),
            pltpu.VMEM((MLSTM_HEADS, 8, LANES), jnp.float32),
        ],
        compiler_params=pltpu.CompilerParams(
            dimension_semantics=("parallel", "arbitrary"), vmem_limit_bytes=VMEM_LIMIT),
        name="mlstm",
    )(p, p, p, p, gates_t, conv_w, gate_b, hn_g, tri)


def _out_route_kernel(seq, x_ref, ym_ref, u_ref, up_ref, pw_ref, ps_ref, wo_ref, g2_ref,
                      wrt_ref, br_ref, x1_ref, h2_ref, idx_ref, gate_ref, rank_ref, cnt_ref,
                      ubuf_ref, carry_ref):
    TM = TM_PROJ
    i = pl.program_id(0)

    @pl.when(i == 0)
    def _():
        carry_ref[...] = jnp.zeros_like(carry_ref)

    pos0 = (i * TM) % seq
    ubuf_ref[0:HALO, :] = jnp.where(pos0 > 0, up_ref[...].astype(jnp.float32), 0.0)
    ubuf_ref[HALO:HALO + TM, :] = u_ref[...].astype(jnp.float32)
    pos = (pos0 + lax.broadcasted_iota(jnp.int32, (TM, 1), 0) + 1).astype(jnp.float32)
    mixed = []
    for gi, w in enumerate(POOL_WINDOWS):
        lo = gi * POOL_GROUP_DIM
        tok = ubuf_ref[HALO:HALO + TM, lo:lo + POOL_GROUP_DIM]
        acc = tok
        for j in range(1, w):
            acc = acc + ubuf_ref[HALO - j:HALO - j + TM, lo:lo + POOL_GROUP_DIM]
        pooled = acc / jnp.minimum(pos, float(w)) - tok
        mg = jnp.dot(pooled.astype(jnp.bfloat16), pw_ref[gi], preferred_element_type=jnp.float32)
        mixed.append((mg * ps_ref[:, lo:lo + POOL_GROUP_DIM]).astype(jnp.bfloat16))
    y_cat = jnp.concatenate([ym_ref[...]] + mixed, axis=1)

    x1 = x_ref[...] + jnp.dot(y_cat, wo_ref[...], preferred_element_type=jnp.float32)
    x1_ref[...] = x1
    h2 = x1 * lax.rsqrt(jnp.mean(x1 * x1, axis=-1, keepdims=True) + EPS) * g2_ref[...]
    h2b = h2.astype(jnp.bfloat16)
    h2_ref[...] = h2b

    logits = lax.dot_general(wrt_ref[...], h2b, NT_DIMS,
                             preferred_element_type=jnp.float32) + br_ref[...]
    e_id = lax.broadcasted_iota(jnp.int32, (N_EXPERTS, TM), 0).astype(jnp.float32)
    work = logits
    vals, ids, hots = [], [], []
    for _ in range(TOP_K):
        mk = jnp.max(work, axis=0, keepdims=True)
        ik = jnp.min(jnp.where(work == mk, e_id, float(N_EXPERTS)), axis=0, keepdims=True)
        hot = e_id == ik
        work = jnp.where(hot, -jnp.inf, work)
        vals.append(mk)
        ids.append(ik)
        hots.append(hot)
    ex = [jnp.exp(vk - vals[0]) for vk in vals]
    denom = ex[0] + ex[1] + ex[2] + ex[3]
    gate_ref[...] = jnp.concatenate([e / denom for e in ex], axis=0)
    idx_ref[...] = jnp.concatenate(ids, axis=0).astype(jnp.int32)

    sel_f = sum(jnp.where(hot, 1.0, 0.0) for hot in hots)
    t_row = lax.broadcasted_iota(jnp.int32, (TM, TM), 0)
    t_col = lax.broadcasted_iota(jnp.int32, (TM, TM), 1)
    before = jnp.where(t_row < t_col, 1.0, 0.0).astype(jnp.bfloat16)
    prefix = jnp.dot(sel_f.astype(jnp.bfloat16), before, preferred_element_type=jnp.float32)
    carry = carry_ref[...]
    rank_e = carry[:, 0:1] + prefix
    ranks = [jnp.sum(jnp.where(hot, rank_e, 0.0), axis=0, keepdims=True) for hot in hots]
    rank_ref[...] = jnp.concatenate(ranks, axis=0).astype(jnp.int32)
    carry_new = carry + jnp.sum(sel_f, axis=1, keepdims=True)
    carry_ref[...] = carry_new
    cnt_ref[...] = carry_new.astype(jnp.int32)


def _out_route(x2, ym, p, pool_w, pool_s, w_out, g2, wr_t, br, seq):
    T = x2.shape[0]
    TM = TM_PROJ
    nt = T // TM
    u_blk = N_MAIN // POOL_WIDTH - 1
    halo_per_tile = TM // HALO
    tok_spec = pl.BlockSpec((TOP_K, TM), lambda i: (0, i))
    return pl.pallas_call(
        functools.partial(_out_route_kernel, seq),
        grid=(nt,),
        in_specs=[
            pl.BlockSpec((TM, D_MODEL), lambda i: (i, 0)),
            pl.BlockSpec((TM, MLSTM_WIDTH), lambda i: (i, 0)),
            pl.BlockSpec((TM, POOL_WIDTH), lambda i: (i, u_blk)),
            pl.BlockSpec((HALO, POOL_WIDTH),
                         lambda i: (jnp.maximum(i * halo_per_tile - 1, 0), u_blk)),
            pl.BlockSpec((len(POOL_WINDOWS), POOL_GROUP_DIM, POOL_GROUP_DIM), lambda i: (0, 0, 0)),
            pl.BlockSpec((1, POOL_WIDTH), lambda i: (0, 0)),
            pl.BlockSpec((D_MODEL, D_MODEL), lambda i: (0, 0)),
            pl.BlockSpec((1, D_MODEL), lambda i: (0, 0)),
            pl.BlockSpec((N_EXPERTS, D_MODEL), lambda i: (0, 0)),
            pl.BlockSpec((N_EXPERTS, 1), lambda i: (0, 0)),
        ],
        out_specs=[
            pl.BlockSpec((TM, D_MODEL), lambda i: (i, 0)),
            pl.BlockSpec((TM, D_MODEL), lambda i: (i, 0)),
            tok_spec, tok_spec, tok_spec,
            pl.BlockSpec((N_EXPERTS, LANES), lambda i: (0, 0)),
        ],
        out_shape=[
            jax.ShapeDtypeStruct((T, D_MODEL), jnp.float32),
            jax.ShapeDtypeStruct((T, D_MODEL), jnp.bfloat16),
            jax.ShapeDtypeStruct((TOP_K, T), jnp.int32),
            jax.ShapeDtypeStruct((TOP_K, T), jnp.float32),
            jax.ShapeDtypeStruct((TOP_K, T), jnp.int32),
            jax.ShapeDtypeStruct((N_EXPERTS, LANES), jnp.int32),
        ],
        scratch_shapes=[
            pltpu.VMEM((HALO + TM, POOL_WIDTH), jnp.float32),
            pltpu.VMEM((N_EXPERTS, LANES), jnp.float32),
        ],
        compiler_params=pltpu.CompilerParams(
            dimension_semantics=("arbitrary",), vmem_limit_bytes=VMEM_LIMIT),
        name="out_route",
    )(x2, ym, p, p, pool_w, pool_s, w_out, g2, wr_t, br)


def _expert_kernel(be_ref, nb_ref, x_ref, gt_ref, wg_ref, bg_ref, wu_ref, bu_ref, wd_ref, bd_ref,
                   eye_ref, y_ref, wgb_ref, wub_ref, wdb_ref):
    b = pl.program_id(0)
    prev = be_ref[jnp.maximum(b - 1, 0)]
    new_expert = jnp.logical_or(b == 0, be_ref[b] != prev)

    @pl.when(new_expert)
    def _():
        wgb_ref[...] = wg_ref[0].astype(jnp.bfloat16)
        wub_ref[...] = wu_ref[0].astype(jnp.bfloat16)
        wdb_ref[...] = wd_ref[0].astype(jnp.bfloat16)

    @pl.when(b < nb_ref[0])
    def _():
        x = x_ref[...]
        gate = jnp.dot(x, wgb_ref[...], preferred_element_type=jnp.float32) + bg_ref[0]
        up = jnp.dot(x, wub_ref[...], preferred_element_type=jnp.float32) + bu_ref[0]
        gate = jnp.minimum(gate, SWIGLU_LIMIT)
        up = jnp.clip(up, -SWIGLU_LIMIT, SWIGLU_LIMIT)
        glu = gate * _sigmoid(SWIGLU_ALPHA * gate)
        act = (glu * (up + 1.0)).astype(jnp.bfloat16)
        y = jnp.dot(act, wdb_ref[...], preferred_element_type=jnp.float32) + bd_ref[0]
        g_col = lax.dot_general(eye_ref[...], gt_ref[0], NT_DIMS, precision=lax.Precision.HIGHEST,
                                preferred_element_type=jnp.float32)[:, 0:1]
        y_ref[...] = y * g_col

    @pl.when(b >= nb_ref[0])
    def _():
        y_ref[...] = jnp.zeros_like(y_ref)


def _experts(block_expert, n_used, x_buf, gate_rows, w_gate, b_gate, w_up, b_up, w_down, b_down, eye):
    n_blocks = block_expert.shape[0]
    TM = TM_EXPERT
    w_spec = pl.BlockSpec((1, D_MODEL, D_FF), lambda b, be, nb: (be[b], 0, 0))
    bias_spec = pl.BlockSpec((1, 1, D_FF), lambda b, be, nb: (be[b], 0, 0))
    grid_spec = pltpu.PrefetchScalarGridSpec(
        num_scalar_prefetch=2,
        grid=(n_blocks,),
        in_specs=[
            pl.BlockSpec((TM, D_MODEL), lambda b, be, nb: (b, 0)),
            pl.BlockSpec((1, 8, TM), lambda b, be, nb: (b, 0, 0)),
            w_spec, bias_spec, w_spec, bias_spec, w_spec, bias_spec,
            pl.BlockSpec((TM, TM), lambda b, be, nb: (0, 0)),
        ],
        out_specs=pl.BlockSpec((TM, D_MODEL), lambda b, be, nb: (b, 0)),
        scratch_shapes=[
            pltpu.VMEM((D_MODEL, D_FF), jnp.bfloat16),
            pltpu.VMEM((D_MODEL, D_FF), jnp.bfloat16),
            pltpu.VMEM((D_FF, D_MODEL), jnp.bfloat16),
        ],
    )
    return pl.pallas_call(
        _expert_kernel,
        grid_spec=grid_spec,
        out_shape=jax.ShapeDtypeStruct((n_blocks * TM, D_MODEL), jnp.float32),
        compiler_params=pltpu.CompilerParams(
            dimension_semantics=("arbitrary",), vmem_limit_bytes=VMEM_LIMIT),
        name="experts",
    )(block_expert, n_used, x_buf, gate_rows, w_gate, b_gate, w_up, b_up, w_down, b_down, eye)


def _final_kernel(x1_ref, yk_ref, g_ref, o_ref):
    x = x1_ref[...]
    for k in range(TOP_K):
        x = x + yk_ref[k]
    o_ref[...] = x * lax.rsqrt(jnp.mean(x * x, axis=-1, keepdims=True) + EPS) * g_ref[...]


def _final(x1, yk, gf):
    T = x1.shape[0]
    TM = TM_PROJ
    return pl.pallas_call(
        _final_kernel,
        grid=(T // TM,),
        in_specs=[
            pl.BlockSpec((TM, D_MODEL), lambda i: (i, 0)),
            pl.BlockSpec((TOP_K, TM, D_MODEL), lambda i: (0, i, 0)),
            pl.BlockSpec((1, D_MODEL), lambda i: (0, 0)),
        ],
        out_specs=pl.BlockSpec((TM, D_MODEL), lambda i: (i, 0)),
        out_shape=jax.ShapeDtypeStruct((T, D_MODEL), jnp.float32),
        compiler_params=pltpu.CompilerParams(
            dimension_semantics=("parallel",), vmem_limit_bytes=VMEM_LIMIT),
        name="final",
    )(x1, yk, gf)


def kernel(x, norm1_g, w_in, ig_b, fg_b, conv_w, head_norm_g, pool_w, pool_scale, w_out, norm2_g,
           w_router, b_router, w_gate, b_gate, w_up, b_up, w_down, b_down, normf_g):
    B, S, D = x.shape
    T = B * S
    depth = norm1_g.shape[0]
    W = MLSTM_WIDTH
    f32, bf16 = jnp.float32, jnp.bfloat16

    L = CHUNK
    t_l = lax.broadcasted_iota(jnp.int32, (L, L), 0)
    t_r = lax.broadcasted_iota(jnp.int32, (L, L), 1)
    tri = jnp.concatenate([(t_r <= t_l).astype(f32), (t_r == t_l).astype(f32)], axis=0)
    eye = jnp.eye(TM_EXPERT, dtype=f32)

    n_blocks = -(-(T * TOP_K) // TM_EXPERT) + N_EXPERTS
    x2 = x.reshape(T, D)
    for l in range(depth):
        w = w_in[l]
        w_main = jnp.concatenate([w[:, :4 * W], w[:, 4 * W + N_GATES:]], axis=1).astype(bf16)
        wg_t = jnp.zeros((BF16_SUBLANES, D), bf16).at[:N_GATES].set(
            w[:, 4 * W:4 * W + N_GATES].T.astype(bf16))
        p, gates_t = _in_proj(x2, norm1_g[l][None, :], w_main, wg_t)

        gate_b = jnp.concatenate([ig_b[l], fg_b[l]])[:, None].astype(f32)
        ym = _mlstm(p, gates_t, conv_w[l].astype(f32), gate_b, head_norm_g[l][None, :], tri, B, S)

        x1, h2, idx_t, gate_t, rank_t, cnt = _out_route(
            x2, ym, p, pool_w[l].astype(bf16), pool_scale[l][None, :], w_out[l].astype(bf16),
            norm2_g[l][None, :], w_router[l].T.astype(bf16), b_router[l][:, None], S)

        counts = cnt[:, 0]
        padded = ((counts + TM_EXPERT - 1) // TM_EXPERT) * TM_EXPERT
        padded_end = jnp.cumsum(padded)
        padded_start = padded_end - padded
        dest = padded_start[idx_t] + rank_t
        n_used = (padded_end[-1] // TM_EXPERT).astype(jnp.int32)[None]
        block_start = jnp.arange(n_blocks, dtype=jnp.int32) * TM_EXPERT
        block_expert = jnp.searchsorted(padded_end, block_start, side="right").astype(jnp.int32)
        last_expert = block_expert[jnp.maximum(n_used[0] - 1, 0)]
        block_expert = jnp.where(block_start < padded_end[-1], block_expert, last_expert)
        block_expert = jnp.minimum(block_expert, N_EXPERTS - 1)

        P = n_blocks * TM_EXPERT
        tok_ids = jnp.broadcast_to(jnp.arange(T, dtype=jnp.int32)[None, :], (TOP_K, T))
        tok_buf = jnp.zeros((P,), jnp.int32).at[dest.reshape(-1)].set(tok_ids.reshape(-1))
        gate_buf = jnp.zeros((P,), f32).at[dest.reshape(-1)].set(gate_t.reshape(-1))
        gate_rows = jnp.broadcast_to(gate_buf.reshape(n_blocks, 1, TM_EXPERT),
                                     (n_blocks, 8, TM_EXPERT))
        x_buf = h2[tok_buf]

        y_buf = _experts(block_expert, n_used, x_buf, gate_rows, w_gate[l], b_gate[l][:, None, :],
                         w_up[l], b_up[l][:, None, :], w_down[l], b_down[l][:, None, :], eye)
        yk = y_buf[dest]
        if l + 1 < depth:
            x2 = x1 + yk.sum(0)
    out = _final(x1, yk, normf_g[None, :])
    return out.reshape(B, S, D)
```

```python
import functools

import jax
import jax.numpy as jnp
from jax import lax
from jax.experimental import pallas as pl
from jax.experimental.pallas import tpu as pltpu

D_MODEL = 1024
MLSTM_WIDTH = 512
MLSTM_HEADS = 4
HEAD_DIM = 128
CONV_WIDTH = 4
POOL_WIDTH = 512
POOL_WINDOWS = (2, 4, 8, 16)
POOL_GROUP_DIM = 128
N_EXPERTS = 32
TOP_K = 4
D_FF = 1024
SWIGLU_LIMIT = 7.0
SWIGLU_ALPHA = 1.702
EPS = 1e-5

N_MAIN = 4 * MLSTM_WIDTH + POOL_WIDTH
N_GATES = 2 * MLSTM_HEADS

LANES = 128
BF16_SUBLANES = 16
VMEM_LIMIT = 56 * 1024 * 1024

TM_PROJ = 512
CHUNK = 256
HALO = 16
TM_EXPERT = 256
SLAB = D_MODEL // LANES
PLAN_UNROLL = 16

NT_DIMS = (((1,), (1,)), ((), ()))
TN_DIMS = (((0,), (0,)), ((), ()))


def _sigmoid(x):
    return 1.0 / (1.0 + jnp.exp(-x))


def _in_proj_kernel(x_ref, g_ref, w_ref, wgt_ref, p_ref, gt_ref):
    x = x_ref[...]
    h = x * lax.rsqrt(jnp.mean(x * x, axis=-1, keepdims=True) + EPS) * g_ref[...]
    hb = h.astype(jnp.bfloat16)
    p_ref[...] = jnp.dot(hb, w_ref[...], preferred_element_type=jnp.float32).astype(p_ref.dtype)
    gt = lax.dot_general(wgt_ref[...], hb, NT_DIMS, preferred_element_type=jnp.float32)
    gt_ref[...] = gt[:N_GATES]


def _in_proj(x2, g1, w_main, wg_t):
    T = x2.shape[0]
    return pl.pallas_call(
        _in_proj_kernel,
        grid=(T // TM_PROJ,),
        in_specs=[
            pl.BlockSpec((TM_PROJ, D_MODEL), lambda i: (i, 0)),
            pl.BlockSpec((1, D_MODEL), lambda i: (0, 0)),
            pl.BlockSpec((D_MODEL, N_MAIN), lambda i: (0, 0)),
            pl.BlockSpec((BF16_SUBLANES, D_MODEL), lambda i: (0, 0)),
        ],
        out_specs=[
            pl.BlockSpec((TM_PROJ, N_MAIN), lambda i: (i, 0)),
            pl.BlockSpec((N_GATES, TM_PROJ), lambda i: (0, i)),
        ],
        out_shape=[
            jax.ShapeDtypeStruct((T, N_MAIN), jnp.bfloat16),
            jax.ShapeDtypeStruct((N_GATES, T), jnp.float32),
        ],
        compiler_params=pltpu.CompilerParams(
            dimension_semantics=("parallel",), vmem_limit_bytes=VMEM_LIMIT),
        name="in_proj",
    )(x2, g1, w_main, wg_t)


def _mlstm_kernel(qk_ref, qkp_ref, v_ref, o_ref, gt_ref, convw_ref, gb_ref, hng_ref,
                  tri_ref, y_ref, ext_ref, cn_ref, m_ref):
    L = CHUNK
    c = pl.program_id(1)

    @pl.when(c == 0)
    def _():
        cn_ref[...] = jnp.zeros_like(cn_ref)
        m_ref[...] = jnp.zeros_like(m_ref)

    halo = qkp_ref[...].astype(jnp.float32)
    ext_ref[0:HALO, :] = jnp.where(c > 0, halo, 0.0)
    ext_ref[HALO:HALO + L, :] = qk_ref[...].astype(jnp.float32)
    acc = None
    for j in range(CONV_WIDTH):
        off = HALO - (CONV_WIDTH - 1) + j
        term = convw_ref[j:j + 1, :] * ext_ref[off:off + L, :]
        acc = term if acc is None else acc + term
    qk = acc * _sigmoid(acc)

    gt = gt_ref[...] + gb_ref[...]
    f = gt[MLSTM_HEADS:]
    lf = jnp.minimum(f, 0.0) - jnp.log(1.0 + jnp.exp(-jnp.abs(f)))
    ig = gt[:MLSTM_HEADS]
    rows = jnp.concatenate([lf, ig], axis=0)
    tri = tri_ref[...]
    cols = lax.dot_general(tri, rows, NT_DIMS, precision=lax.Precision.HIGHEST,
                           preferred_element_type=jnp.float32)
    b_cols = cols[:L, :MLSTM_HEADS]
    ig_cols = cols[L:, MLSTM_HEADS:]
    b_rows = lax.dot_general(lf, tri[:L], NT_DIMS, precision=lax.Precision.HIGHEST,
                             preferred_element_type=jnp.float32)
    c_rows = ig - b_rows

    row_id = lax.broadcasted_iota(jnp.int32, (L, L), 0)
    col_id = lax.broadcasted_iota(jnp.int32, (L, L), 1)
    causal = col_id <= row_id
    ones_col = (lax.broadcasted_iota(jnp.int32, (L, HEAD_DIM), 1) == 0).astype(jnp.bfloat16)

    for h in range(MLSTM_HEADS):
        lo = h * HEAD_DIM
        q = qk[:, lo:lo + HEAD_DIM].astype(jnp.bfloat16)
        k_f = qk[:, MLSTM_WIDTH + lo:MLSTM_WIDTH + lo + HEAD_DIM] * (HEAD_DIM ** -0.5)
        v_ext = jnp.concatenate([v_ref[:, lo:lo + HEAD_DIM], ones_col], axis=1)
        b_col = b_cols[:, h:h + 1]
        ig_col = ig_cols[:, h:h + 1]
        c_row = c_rows[h:h + 1, :]
        b_tot = b_rows[h:h + 1, L - 1:L]
        m_in = m_ref[h][0:1, 0:1]
        cn = cn_ref[h]

        s_qk = lax.dot_general(q, k_f.astype(jnp.bfloat16), NT_DIMS,
                               preferred_element_type=jnp.float32)
        log_d = jnp.where(causal, b_col + c_row, -jnp.inf)
        a_col = b_col + m_in
        m_out = jnp.maximum(a_col, jnp.max(log_d, axis=-1, keepdims=True))
        s = (s_qk * jnp.exp(log_d - m_out)).astype(jnp.bfloat16)
        inter = jnp.exp(a_col - m_out)
        num = (jnp.dot(s, v_ext, preferred_element_type=jnp.float32)
               + inter * jnp.dot(q, cn.astype(jnp.bfloat16), preferred_element_type=jnp.float32))
        den = num[:, HEAD_DIM:HEAD_DIM + 1]
        hh = num[:, :HEAD_DIM] / jnp.maximum(jnp.abs(den), jnp.exp(-m_out))

        mu = jnp.mean(hh, axis=-1, keepdims=True)
        d = hh - mu
        var = jnp.mean(d * d, axis=-1, keepdims=True)
        hn = d * lax.rsqrt(var + EPS) * hng_ref[:, lo:lo + HEAD_DIM]
        og = _sigmoid(o_ref[:, lo:lo + HEAD_DIM].astype(jnp.float32))
        y_ref[:, lo:lo + HEAD_DIM] = (og * hn).astype(y_ref.dtype)

        g_col = b_tot - b_col + ig_col
        m_loc = jnp.max(g_col, axis=0, keepdims=True)
        kw = (k_f * jnp.exp(g_col - m_loc)).astype(jnp.bfloat16)
        c_loc = lax.dot_general(kw, v_ext, TN_DIMS, preferred_element_type=jnp.float32)
        m_new = jnp.maximum(b_tot + m_in, m_loc)
        s_old = jnp.exp(b_tot + m_in - m_new)
        s_loc = jnp.exp(m_loc - m_new)
        cn_ref[h] = s_old * cn + s_loc * c_loc
        m_ref[h] = jnp.broadcast_to(m_new, m_ref.shape[1:])


def _mlstm(p, gates_t, conv_w, gate_b, hn_g, tri, batch, seq):
    T = batch * seq
    L = CHUNK
    nc = seq // L
    halo_per_chunk = L // HALO

    def cur(bi, ci):
        return bi * nc + ci

    return pl.pallas_call(
        _mlstm_kernel,
        grid=(batch, nc),
        in_specs=[
            pl.BlockSpec((L, 2 * MLSTM_WIDTH), lambda bi, ci: (cur(bi, ci), 0)),
            pl.BlockSpec((HALO, 2 * MLSTM_WIDTH),
                         lambda bi, ci: (jnp.maximum(cur(bi, ci) * halo_per_chunk - 1, 0), 0)),
            pl.BlockSpec((L, MLSTM_WIDTH), lambda bi, ci: (cur(bi, ci), 2)),
            pl.BlockSpec((L, MLSTM_WIDTH), lambda bi, ci: (cur(bi, ci), 3)),
            pl.BlockSpec((N_GATES, L), lambda bi, ci: (0, cur(bi, ci))),
            pl.BlockSpec((CONV_WIDTH, 2 * MLSTM_WIDTH), lambda bi, ci: (0, 0)),
            pl.BlockSpec((N_GATES, 1), lambda bi, ci: (0, 0)),
            pl.BlockSpec((1, MLSTM_WIDTH), lambda bi, ci: (0, 0)),
            pl.BlockSpec((2 * L, L), lambda bi, ci: (0, 0)),
        ],
        out_specs=pl.BlockSpec((L, MLSTM_WIDTH), lambda bi, ci: (cur(bi, ci), 0)),
        out_shape=jax.ShapeDtypeStruct((T, MLSTM_WIDTH), jnp.bfloat16),
        scratch_shapes=[
            pltpu.VMEM((HALO + L, 2 * MLSTM_WIDTH), jnp.float32),
            pltpu.VMEM((MLSTM_HEADS, HEAD_DIM, 2 * HEAD_DIM), jnp.float32),
            pltpu.VMEM((MLSTM_HEADS, 8, LANES), jnp.float32),
        ],
        compiler_params=pltpu.CompilerParams(
            dimension_semantics=("parallel", "arbitrary"), vmem_limit_bytes=VMEM_LIMIT),
        name="mlstm",
    )(p, p, p, p, gates_t, conv_w, gate_b, hn_g, tri)


def _out_route_kernel(seq, x_ref, ym_ref, u_ref, up_ref, pw_ref, ps_ref, wo_ref, g2_ref,
                      wrt_ref, br_ref, x1_ref, h2_ref, idx_ref, gate_ref, rank_ref, cnt_ref,
                      ubuf_ref, carry_ref):
    TM = TM_PROJ
    i = pl.program_id(0)

    @pl.when(i == 0)
    def _():
        carry_ref[...] = jnp.zeros_like(carry_ref)

    pos0 = (i * TM) % seq
    ubuf_ref[0:HALO, :] = jnp.where(pos0 > 0, up_ref[...].astype(jnp.float32), 0.0)
    ubuf_ref[HALO:HALO + TM, :] = u_ref[...].astype(jnp.float32)
    pos = (pos0 + lax.broadcasted_iota(jnp.int32, (TM, 1), 0) + 1).astype(jnp.float32)
    mixed = []
    for gi, w in enumerate(POOL_WINDOWS):
        lo = gi * POOL_GROUP_DIM
        tok = ubuf_ref[HALO:HALO + TM, lo:lo + POOL_GROUP_DIM]
        acc = tok
        for j in range(1, w):
            acc = acc + ubuf_ref[HALO - j:HALO - j + TM, lo:lo + POOL_GROUP_DIM]
        pooled = acc / jnp.minimum(pos, float(w)) - tok
        mg = jnp.dot(pooled.astype(jnp.bfloat16), pw_ref[gi], preferred_element_type=jnp.float32)
        mixed.append((mg * ps_ref[:, lo:lo + POOL_GROUP_DIM]).astype(jnp.bfloat16))
    y_cat = jnp.concatenate([ym_ref[...]] + mixed, axis=1)

    x1 = x_ref[...] + jnp.dot(y_cat, wo_ref[...], preferred_element_type=jnp.float32)
    x1_ref[...] = x1
    h2 = x1 * lax.rsqrt(jnp.mean(x1 * x1, axis=-1, keepdims=True) + EPS) * g2_ref[...]
    h2b = h2.astype(jnp.bfloat16)
    for s in range(SLAB):
        h2_ref[pl.ds(s, TM, stride=SLAB), :] = h2[:, s * LANES:(s + 1) * LANES]

    logits = lax.dot_general(wrt_ref[...], h2b, NT_DIMS,
                             preferred_element_type=jnp.float32) + br_ref[...]
    e_id = lax.broadcasted_iota(jnp.int32, (N_EXPERTS, TM), 0).astype(jnp.float32)
    work = logits
    vals, ids, hots = [], [], []
    for _ in range(TOP_K):
        mk = jnp.max(work, axis=0, keepdims=True)
        ik = jnp.min(jnp.where(work == mk, e_id, float(N_EXPERTS)), axis=0, keepdims=True)
        hot = e_id == ik
        work = jnp.where(hot, -jnp.inf, work)
        vals.append(mk)
        ids.append(ik)
        hots.append(hot)
    ex = [jnp.exp(vk - vals[0]) for vk in vals]
    denom = ex[0] + ex[1] + ex[2] + ex[3]
    gate_ref[...] = jnp.concatenate([e / denom for e in ex], axis=0)
    idx_ref[...] = jnp.concatenate(ids, axis=0).astype(jnp.int32)

    sel_f = sum(jnp.where(hot, 1.0, 0.0) for hot in hots)
    t_row = lax.broadcasted_iota(jnp.int32, (TM, TM), 0)
    t_col = lax.broadcasted_iota(jnp.int32, (TM, TM), 1)
    before = jnp.where(t_row < t_col, 1.0, 0.0).astype(jnp.bfloat16)
    prefix = jnp.dot(sel_f.astype(jnp.bfloat16), before, preferred_element_type=jnp.float32)
    carry = carry_ref[...]
    rank_e = carry[:, 0:1] + prefix
    ranks = [jnp.sum(jnp.where(hot, rank_e, 0.0), axis=0, keepdims=True) for hot in hots]
    rank_ref[...] = jnp.concatenate(ranks, axis=0).astype(jnp.int32)
    carry_new = carry + jnp.sum(sel_f, axis=1, keepdims=True)
    carry_ref[...] = carry_new
    cnt_ref[...] = carry_new.astype(jnp.int32)


def _out_route(x2, ym, p, pool_w, pool_s, w_out, g2, wr_t, br, seq):
    T = x2.shape[0]
    TM = TM_PROJ
    nt = T // TM
    u_blk = N_MAIN // POOL_WIDTH - 1
    halo_per_tile = TM // HALO
    tok_spec = pl.BlockSpec((TOP_K, TM), lambda i: (0, i))
    return pl.pallas_call(
        functools.partial(_out_route_kernel, seq),
        grid=(nt,),
        in_specs=[
            pl.BlockSpec((TM, D_MODEL), lambda i: (i, 0)),
            pl.BlockSpec((TM, MLSTM_WIDTH), lambda i: (i, 0)),
            pl.BlockSpec((TM, POOL_WIDTH), lambda i: (i, u_blk)),
            pl.BlockSpec((HALO, POOL_WIDTH),
                         lambda i: (jnp.maximum(i * halo_per_tile - 1, 0), u_blk)),
            pl.BlockSpec((len(POOL_WINDOWS), POOL_GROUP_DIM, POOL_GROUP_DIM), lambda i: (0, 0, 0)),
            pl.BlockSpec((1, POOL_WIDTH), lambda i: (0, 0)),
            pl.BlockSpec((D_MODEL, D_MODEL), lambda i: (0, 0)),
            pl.BlockSpec((1, D_MODEL), lambda i: (0, 0)),
            pl.BlockSpec((N_EXPERTS, D_MODEL), lambda i: (0, 0)),
            pl.BlockSpec((N_EXPERTS, 1), lambda i: (0, 0)),
        ],
        out_specs=[
            pl.BlockSpec((TM, D_MODEL), lambda i: (i, 0)),
            pl.BlockSpec((TM * SLAB, LANES), lambda i: (i, 0)),
            tok_spec, tok_spec, tok_spec,
            pl.BlockSpec((N_EXPERTS, LANES), lambda i: (0, 0)),
        ],
        out_shape=[
            jax.ShapeDtypeStruct((T, D_MODEL), jnp.float32),
            jax.ShapeDtypeStruct((T * SLAB, LANES), jnp.float32),
            jax.ShapeDtypeStruct((TOP_K, T), jnp.int32),
            jax.ShapeDtypeStruct((TOP_K, T), jnp.float32),
            jax.ShapeDtypeStruct((TOP_K, T), jnp.int32),
            jax.ShapeDtypeStruct((N_EXPERTS, LANES), jnp.int32),
        ],
        scratch_shapes=[
            pltpu.VMEM((HALO + TM, POOL_WIDTH), jnp.float32),
            pltpu.VMEM((N_EXPERTS, LANES), jnp.float32),
        ],
        compiler_params=pltpu.CompilerParams(
            dimension_semantics=("arbitrary",), vmem_limit_bytes=VMEM_LIMIT),
        name="out_route",
    )(x2, ym, p, p, pool_w, pool_s, w_out, g2, wr_t, br)


def _plan_kernel(n_assign, dest_ref, fill_ref, slot_ref, sem):
    cp = pltpu.make_async_copy(fill_ref, slot_ref, sem)
    cp.start()
    cp.wait()

    def body(i, carry):
        base = i * PLAN_UNROLL
        for j in range(PLAN_UNROLL):
            slot_ref[dest_ref[base + j]] = base + j
        return carry

    lax.fori_loop(0, n_assign // PLAN_UNROLL, body, 0)


def _plan(dest_flat, fill):
    n_assign = dest_flat.shape[0]
    return pl.pallas_call(
        functools.partial(_plan_kernel, n_assign),
        in_specs=[
            pl.BlockSpec(memory_space=pltpu.SMEM),
            pl.BlockSpec(memory_space=pl.ANY),
        ],
        out_specs=pl.BlockSpec(memory_space=pltpu.SMEM),
        out_shape=jax.ShapeDtypeStruct(fill.shape, jnp.int32),
        scratch_shapes=[pltpu.SemaphoreType.DMA(())],
        name="plan",
    )(dest_flat, fill)


def _expert_kernel(n_tok, be_ref, nb_ref, slot_ref, h2_ref, wg_ref, bg_ref, wu_ref, bu_ref,
                   wd_ref, bd_ref, yt_ref, xg_ref, ys_ref, wgb_ref, wub_ref, wdb_ref, gsem, ssem):
    TM = TM_EXPERT
    ROWS = TM * SLAB
    b = pl.program_id(0)
    nb = nb_ref[0]
    cur = b % 2
    nxt = 1 - cur

    def token_of(a):
        return a & (n_tok - 1) if n_tok & (n_tok - 1) == 0 else lax.rem(a, n_tok)

    def start_gather(blk, buf):
        for r in range(TM):
            t = token_of(slot_ref[blk * TM + r])
            pltpu.make_async_copy(h2_ref.at[pl.ds(pl.multiple_of(t * SLAB, SLAB), SLAB), :],
                                  xg_ref.at[buf, pl.ds(r * SLAB, SLAB), :], gsem.at[buf]).start()

    def wait_gather(buf):
        pltpu.make_async_copy(h2_ref.at[pl.ds(0, ROWS), :], xg_ref.at[buf], gsem.at[buf]).wait()

    def start_scatter(blk, buf):
        for r in range(TM):
            a = slot_ref[blk * TM + r]
            pltpu.make_async_copy(ys_ref.at[buf, pl.ds(r * SLAB, SLAB), :],
                                  yt_ref.at[pl.ds(pl.multiple_of(a * SLAB, SLAB), SLAB), :],
                                  ssem.at[buf]).start()

    def wait_scatter(buf):
        pltpu.make_async_copy(ys_ref.at[buf], yt_ref.at[pl.ds(0, ROWS), :], ssem.at[buf]).wait()

    @pl.when(b == 0)
    def _():
        start_gather(0, 0)
        ys_ref[...] = jnp.zeros_like(ys_ref)
        for buf in range(2):
            dump = yt_ref.at[pl.ds((n_tok * TOP_K + buf * TM) * SLAB, ROWS), :]
            cp = pltpu.make_async_copy(ys_ref.at[buf], dump, ssem.at[buf])
            cp.start()
            cp.wait()

    prev = be_ref[jnp.maximum(b - 1, 0)]
    new_expert = jnp.logical_or(b == 0, be_ref[b] != prev)

    @pl.when(jnp.logical_and(new_expert, b < nb))
    def _():
        wgb_ref[...] = wg_ref[0].astype(jnp.bfloat16)
        wub_ref[...] = wu_ref[0].astype(jnp.bfloat16)
        wdb_ref[...] = wd_ref[0].astype(jnp.bfloat16)

    @pl.when(b < nb)
    def _():
        wait_gather(cur)

        @pl.when(b >= 2)
        def _():
            wait_scatter(cur)

        start_gather(b + 1, nxt)
        x = jnp.concatenate(
            [xg_ref[cur, pl.ds(s, TM, stride=SLAB), :].astype(jnp.bfloat16) for s in range(SLAB)],
            axis=1)
        gate = jnp.dot(x, wgb_ref[...], preferred_element_type=jnp.float32) + bg_ref[0]
        up = jnp.dot(x, wub_ref[...], preferred_element_type=jnp.float32) + bu_ref[0]
        gate = jnp.minimum(gate, SWIGLU_LIMIT)
        up = jnp.clip(up, -SWIGLU_LIMIT, SWIGLU_LIMIT)
        glu = gate * _sigmoid(SWIGLU_ALPHA * gate)
        act = (glu * (up + 1.0)).astype(jnp.bfloat16)
        y = jnp.dot(act, wdb_ref[...], preferred_element_type=jnp.float32) + bd_ref[0]
        for s in range(SLAB):
            ys_ref[cur, pl.ds(s, TM, stride=SLAB), :] = y[:, s * LANES:(s + 1) * LANES]
        start_scatter(b, cur)

    @pl.when(b == nb)
    def _():
        wait_gather(cur)
        wait_scatter(nxt)

        @pl.when(b >= 2)
        def _():
            wait_scatter(cur)


def _experts(block_expert, n_used, slot_buf, h2_slab, w_gate, b_gate, w_up, b_up, w_down, b_down,
             n_tok):
    n_blocks = block_expert.shape[0]
    TM = TM_EXPERT
    n_assign = n_tok * TOP_K
    w_spec = pl.BlockSpec((1, D_MODEL, D_FF), lambda b, be, nb, sl: (be[b], 0, 0))
    bias_spec = pl.BlockSpec((1, 1, D_FF), lambda b, be, nb, sl: (be[b], 0, 0))
    grid_spec = pltpu.PrefetchScalarGridSpec(
        num_scalar_prefetch=3,
        grid=(n_blocks,),
        in_specs=[
            pl.BlockSpec(memory_space=pl.ANY),
            w_spec, bias_spec, w_spec, bias_spec, w_spec, bias_spec,
        ],
        out_specs=pl.BlockSpec(memory_space=pl.ANY),
        scratch_shapes=[
            pltpu.VMEM((2, TM * SLAB, LANES), jnp.float32),
            pltpu.VMEM((2, TM * SLAB, LANES), jnp.float32),
            pltpu.VMEM((D_MODEL, D_FF), jnp.bfloat16),
            pltpu.VMEM((D_MODEL, D_FF), jnp.bfloat16),
            pltpu.VMEM((D_FF, D_MODEL), jnp.bfloat16),
            pltpu.SemaphoreType.DMA((2,)),
            pltpu.SemaphoreType.DMA((2,)),
        ],
    )
    return pl.pallas_call(
        functools.partial(_expert_kernel, n_tok),
        grid_spec=grid_spec,
        out_shape=jax.ShapeDtypeStruct(((n_assign + 2 * TM) * SLAB, LANES), jnp.float32),
        compiler_params=pltpu.CompilerParams(
            dimension_semantics=("arbitrary",), vmem_limit_bytes=VMEM_LIMIT),
        name="experts",
    )(block_expert, n_used, slot_buf, h2_slab, w_gate, b_gate, w_up, b_up, w_down, b_down)


def _combine_kernel(normalize, x1_ref, y0_ref, y1_ref, y2_ref, y3_ref, gate_ref, g_ref, o_ref):
    TM = TM_PROJ
    gates = jnp.concatenate([gate_ref[...], jnp.zeros((8 - TOP_K, TM), jnp.float32)], axis=0)
    g_cols = jnp.transpose(gates)
    g_bc = [jnp.broadcast_to(g_cols[:, k:k + 1], (TM, LANES)) for k in range(TOP_K)]
    parts = []
    ssq = jnp.zeros((TM, LANES), jnp.float32)
    for s in range(SLAB):
        acc = x1_ref[:, s * LANES:(s + 1) * LANES]
        for k, y_ref in enumerate((y0_ref, y1_ref, y2_ref, y3_ref)):
            acc = acc + g_bc[k] * y_ref[pl.ds(s, TM, stride=SLAB), :]
        parts.append(acc)
        ssq = ssq + acc * acc
    if normalize:
        inv = lax.rsqrt(jnp.sum(ssq, axis=-1, keepdims=True) * (1.0 / D_MODEL) + EPS)
        for s in range(SLAB):
            o_ref[:, s * LANES:(s + 1) * LANES] = parts[s] * inv * g_ref[:, s * LANES:(s + 1) * LANES]
    else:
        for s in range(SLAB):
            o_ref[:, s * LANES:(s + 1) * LANES] = parts[s]


def _combine(x1, y_tok, gate_t, gf, normalize):
    T = x1.shape[0]
    TM = TM_PROJ
    nt = T // TM

    def y_spec(k):
        return pl.BlockSpec((TM * SLAB, LANES), lambda i: (k * nt + i, 0))

    return pl.pallas_call(
        functools.partial(_combine_kernel, normalize),
        grid=(nt,),
        in_specs=[
            pl.BlockSpec((TM, D_MODEL), lambda i: (i, 0)),
            y_spec(0), y_spec(1), y_spec(2), y_spec(3),
            pl.BlockSpec((TOP_K, TM), lambda i: (0, i)),
            pl.BlockSpec((1, D_MODEL), lambda i: (0, 0)),
        ],
        out_specs=pl.BlockSpec((TM, D_MODEL), lambda i: (i, 0)),
        out_shape=jax.ShapeDtypeStruct((T, D_MODEL), jnp.float32),
        compiler_params=pltpu.CompilerParams(
            dimension_semantics=("parallel",), vmem_limit_bytes=VMEM_LIMIT),
        name="combine",
    )(x1, y_tok, y_tok, y_tok, y_tok, gate_t, gf)


def kernel(x, norm1_g, w_in, ig_b, fg_b, conv_w, head_norm_g, pool_w, pool_scale, w_out, norm2_g,
           w_router, b_router, w_gate, b_gate, w_up, b_up, w_down, b_down, normf_g):
    B, S, D = x.shape
    T = B * S
    depth = norm1_g.shape[0]
    W = MLSTM_WIDTH
    f32, bf16 = jnp.float32, jnp.bfloat16

    L = CHUNK
    t_l = lax.broadcasted_iota(jnp.int32, (L, L), 0)
    t_r = lax.broadcasted_iota(jnp.int32, (L, L), 1)
    tri = jnp.concatenate([(t_r <= t_l).astype(f32), (t_r == t_l).astype(f32)], axis=0)

    n_assign = T * TOP_K
    n_blocks = -(-n_assign // TM_EXPERT) + N_EXPERTS
    n_rows = n_blocks * TM_EXPERT
    fill = n_assign + jnp.arange(n_rows, dtype=jnp.int32) % (2 * TM_EXPERT)
    x2 = x.reshape(T, D)
    for l in range(depth):
        w = w_in[l]
        w_main = jnp.concatenate([w[:, :4 * W], w[:, 4 * W + N_GATES:]], axis=1).astype(bf16)
        wg_t = jnp.zeros((BF16_SUBLANES, D), bf16).at[:N_GATES].set(
            w[:, 4 * W:4 * W + N_GATES].T.astype(bf16))
        p, gates_t = _in_proj(x2, norm1_g[l][None, :], w_main, wg_t)

        gate_b = jnp.concatenate([ig_b[l], fg_b[l]])[:, None].astype(f32)
        ym = _mlstm(p, gates_t, conv_w[l].astype(f32), gate_b, head_norm_g[l][None, :], tri, B, S)

        x1, h2, idx_t, gate_t, rank_t, cnt = _out_route(
            x2, ym, p, pool_w[l].astype(bf16), pool_scale[l][None, :], w_out[l].astype(bf16),
            norm2_g[l][None, :], w_router[l].T.astype(bf16), b_router[l][:, None], S)

        counts = cnt[:, 0]
        padded = ((counts + TM_EXPERT - 1) // TM_EXPERT) * TM_EXPERT
        padded_end = jnp.cumsum(padded)
        padded_start = padded_end - padded
        expert_ids = jnp.arange(N_EXPERTS, dtype=jnp.int32)[:, None, None]
        start_of = jnp.sum(jnp.where(idx_t[None] == expert_ids, padded_start[:, None, None], 0), axis=0)
        dest = start_of + rank_t
        n_used = (padded_end[-1] // TM_EXPERT).astype(jnp.int32)[None]
        block_start = jnp.arange(n_blocks, dtype=jnp.int32) * TM_EXPERT
        used_start = jnp.minimum(block_start, padded_end[-1] - TM_EXPERT)
        block_expert = jnp.sum(used_start[:, None] >= padded_end[None, :], axis=1).astype(jnp.int32)
        block_expert = jnp.minimum(block_expert, N_EXPERTS - 1)

        slot_buf = _plan(dest.reshape(-1), fill)
        y_tok = _experts(block_expert, n_used, slot_buf, h2, w_gate[l], b_gate[l][:, None, :],
                         w_up[l], b_up[l][:, None, :], w_down[l], b_down[l][:, None, :], T)
        last = l + 1 == depth
        x2 = _combine(x1, y_tok, gate_t, normf_g[None, :], last)
    return x2.reshape(B, S, D)
```

```python
import functools

import jax
import jax.numpy as jnp
from jax import lax
from jax.experimental import pallas as pl
from jax.experimental.pallas import tpu as pltpu

D_MODEL = 1024
MLSTM_WIDTH = 512
MLSTM_HEADS = 4
HEAD_DIM = 128
CONV_WIDTH = 4
POOL_WIDTH = 512
POOL_WINDOWS = (2, 4, 8, 16)
POOL_GROUP_DIM = 128
N_EXPERTS = 32
TOP_K = 4
D_FF = 1024
SWIGLU_LIMIT = 7.0
SWIGLU_ALPHA = 1.702
EPS = 1e-5

N_MAIN = 4 * MLSTM_WIDTH + POOL_WIDTH
N_GATES = 2 * MLSTM_HEADS

LANES = 128
BF16_SUBLANES = 16
VMEM_LIMIT = 56 * 1024 * 1024

TM_PROJ = 512
CHUNK = 256
HALO = 16
TM_EXPERT = 256
SLAB = D_MODEL // LANES
PLAN_UNROLL = 16

NT_DIMS = (((1,), (1,)), ((), ()))
TN_DIMS = (((0,), (0,)), ((), ()))


def _sigmoid(x):
    return 1.0 / (1.0 + jnp.exp(-x))


def _in_proj_kernel(x_ref, g_ref, w_ref, wgt_ref, p_ref, gt_ref):
    x = x_ref[...]
    h = x * lax.rsqrt(jnp.mean(x * x, axis=-1, keepdims=True) + EPS) * g_ref[...]
    hb = h.astype(jnp.bfloat16)
    p_ref[...] = jnp.dot(hb, w_ref[...], preferred_element_type=jnp.float32).astype(p_ref.dtype)
    gt = lax.dot_general(wgt_ref[...], hb, NT_DIMS, preferred_element_type=jnp.float32)
    gt_ref[...] = gt[:N_GATES]


def _in_proj(x2, g1, w_main, wg_t):
    T = x2.shape[0]
    return pl.pallas_call(
        _in_proj_kernel,
        grid=(T // TM_PROJ,),
        in_specs=[
            pl.BlockSpec((TM_PROJ, D_MODEL), lambda i: (i, 0)),
            pl.BlockSpec((1, D_MODEL), lambda i: (0, 0)),
            pl.BlockSpec((D_MODEL, N_MAIN), lambda i: (0, 0)),
            pl.BlockSpec((BF16_SUBLANES, D_MODEL), lambda i: (0, 0)),
        ],
        out_specs=[
            pl.BlockSpec((TM_PROJ, N_MAIN), lambda i: (i, 0)),
            pl.BlockSpec((N_GATES, TM_PROJ), lambda i: (0, i)),
        ],
        out_shape=[
            jax.ShapeDtypeStruct((T, N_MAIN), jnp.bfloat16),
            jax.ShapeDtypeStruct((N_GATES, T), jnp.float32),
        ],
        compiler_params=pltpu.CompilerParams(
            dimension_semantics=("parallel",), vmem_limit_bytes=VMEM_LIMIT),
        name="in_proj",
    )(x2, g1, w_main, wg_t)


def _mlstm_kernel(qk_ref, qkp_ref, v_ref, o_ref, gt_ref, convw_ref, gb_ref, hng_ref,
                  tri_ref, y_ref, ext_ref, cn_ref, m_ref):
    L = CHUNK
    c = pl.program_id(1)

    @pl.when(c == 0)
    def _():
        cn_ref[...] = jnp.zeros_like(cn_ref)
        m_ref[...] = jnp.zeros_like(m_ref)

    halo = qkp_ref[...].astype(jnp.float32)
    ext_ref[0:HALO, :] = jnp.where(c > 0, halo, 0.0)
    ext_ref[HALO:HALO + L, :] = qk_ref[...].astype(jnp.float32)
    acc = None
    for j in range(CONV_WIDTH):
        off = HALO - (CONV_WIDTH - 1) + j
        term = convw_ref[j:j + 1, :] * ext_ref[off:off + L, :]
        acc = term if acc is None else acc + term
    qk = acc * _sigmoid(acc)

    gt = gt_ref[...] + gb_ref[...]
    f = gt[MLSTM_HEADS:]
    lf = jnp.minimum(f, 0.0) - jnp.log(1.0 + jnp.exp(-jnp.abs(f)))
    ig = gt[:MLSTM_HEADS]
    rows = jnp.concatenate([lf, ig], axis=0)
    tri = tri_ref[...]
    cols = lax.dot_general(tri, rows, NT_DIMS, precision=lax.Precision.HIGHEST,
                           preferred_element_type=jnp.float32)
    b_cols = cols[:L, :MLSTM_HEADS]
    ig_cols = cols[L:, MLSTM_HEADS:]
    b_rows = lax.dot_general(lf, tri[:L], NT_DIMS, precision=lax.Precision.HIGHEST,
                             preferred_element_type=jnp.float32)
    c_rows = ig - b_rows

    row_id = lax.broadcasted_iota(jnp.int32, (L, L), 0)
    col_id = lax.broadcasted_iota(jnp.int32, (L, L), 1)
    causal = col_id <= row_id
    ones_col = (lax.broadcasted_iota(jnp.int32, (L, HEAD_DIM), 1) == 0).astype(jnp.bfloat16)

    for h in range(MLSTM_HEADS):
        lo = h * HEAD_DIM
        q = qk[:, lo:lo + HEAD_DIM].astype(jnp.bfloat16)
        k_f = qk[:, MLSTM_WIDTH + lo:MLSTM_WIDTH + lo + HEAD_DIM] * (HEAD_DIM ** -0.5)
        v_ext = jnp.concatenate([v_ref[:, lo:lo + HEAD_DIM], ones_col], axis=1)
        b_col = b_cols[:, h:h + 1]
        ig_col = ig_cols[:, h:h + 1]
        c_row = c_rows[h:h + 1, :]
        b_tot = b_rows[h:h + 1, L - 1:L]
        m_in = m_ref[h][0:1, 0:1]
        cn = cn_ref[h]

        s_qk = lax.dot_general(q, k_f.astype(jnp.bfloat16), NT_DIMS,
                               preferred_element_type=jnp.float32)
        log_d = jnp.where(causal, b_col + c_row, -jnp.inf)
        a_col = b_col + m_in
        m_out = jnp.maximum(a_col, jnp.max(log_d, axis=-1, keepdims=True))
        s = (s_qk * jnp.exp(log_d - m_out)).astype(jnp.bfloat16)
        inter = jnp.exp(a_col - m_out)
        num = (jnp.dot(s, v_ext, preferred_element_type=jnp.float32)
               + inter * jnp.dot(q, cn.astype(jnp.bfloat16), preferred_element_type=jnp.float32))
        den = num[:, HEAD_DIM:HEAD_DIM + 1]
        hh = num[:, :HEAD_DIM] / jnp.maximum(jnp.abs(den), jnp.exp(-m_out))

        mu = jnp.mean(hh, axis=-1, keepdims=True)
        d = hh - mu
        var = jnp.mean(d * d, axis=-1, keepdims=True)
        hn = d * lax.rsqrt(var + EPS) * hng_ref[:, lo:lo + HEAD_DIM]
        og = _sigmoid(o_ref[:, lo:lo + HEAD_DIM].astype(jnp.float32))
        y_ref[:, lo:lo + HEAD_DIM] = (og * hn).astype(y_ref.dtype)

        g_col = b_tot - b_col + ig_col
        m_loc = jnp.max(g_col, axis=0, keepdims=True)
        kw = (k_f * jnp.exp(g_col - m_loc)).astype(jnp.bfloat16)
        c_loc = lax.dot_general(kw, v_ext, TN_DIMS, preferred_element_type=jnp.float32)
        m_new = jnp.maximum(b_tot + m_in, m_loc)
        s_old = jnp.exp(b_tot + m_in - m_new)
        s_loc = jnp.exp(m_loc - m_new)
        cn_ref[h] = s_old * cn + s_loc * c_loc
        m_ref[h] = jnp.broadcast_to(m_new, m_ref.shape[1:])


def _mlstm(p, gates_t, conv_w, gate_b, hn_g, tri, batch, seq):
    T = batch * seq
    L = CHUNK
    nc = seq // L
    halo_per_chunk = L // HALO

    def cur(bi, ci):
        return bi * nc + ci

    return pl.pallas_call(
        _mlstm_kernel,
        grid=(batch, nc),
        in_specs=[
            pl.BlockSpec((L, 2 * MLSTM_WIDTH), lambda bi, ci: (cur(bi, ci), 0)),
            pl.BlockSpec((HALO, 2 * MLSTM_WIDTH),
                         lambda bi, ci: (jnp.maximum(cur(bi, ci) * halo_per_chunk - 1, 0), 0)),
            pl.BlockSpec((L, MLSTM_WIDTH), lambda bi, ci: (cur(bi, ci), 2)),
            pl.BlockSpec((L, MLSTM_WIDTH), lambda bi, ci: (cur(bi, ci), 3)),
            pl.BlockSpec((N_GATES, L), lambda bi, ci: (0, cur(bi, ci))),
            pl.BlockSpec((CONV_WIDTH, 2 * MLSTM_WIDTH), lambda bi, ci: (0, 0)),
            pl.BlockSpec((N_GATES, 1), lambda bi, ci: (0, 0)),
            pl.BlockSpec((1, MLSTM_WIDTH), lambda bi, ci: (0, 0)),
            pl.BlockSpec((2 * L, L), lambda bi, ci: (0, 0)),
        ],
        out_specs=pl.BlockSpec((L, MLSTM_WIDTH), lambda bi, ci: (cur(bi, ci), 0)),
        out_shape=jax.ShapeDtypeStruct((T, MLSTM_WIDTH), jnp.bfloat16),
        scratch_shapes=[
            pltpu.VMEM((HALO + L, 2 * MLSTM_WIDTH), jnp.float32),
            pltpu.VMEM((MLSTM_HEADS, HEAD_DIM, 2 * HEAD_DIM), jnp.float32),
            pltpu.VMEM((MLSTM_HEADS, 8, LANES), jnp.float32),
        ],
        compiler_params=pltpu.CompilerParams(
            dimension_semantics=("parallel", "arbitrary"), vmem_limit_bytes=VMEM_LIMIT),
        name="mlstm",
    )(p, p, p, p, gates_t, conv_w, gate_b, hn_g, tri)


def _out_route_kernel(seq, x_ref, ym_ref, u_ref, up_ref, pw_ref, ps_ref, wo_ref, g2_ref,
                      wrt_ref, br_ref, x1_ref, h2_ref, idx_ref, gate_ref, rank_ref, cnt_ref,
                      ubuf_ref, carry_ref):
    TM = TM_PROJ
    i = pl.program_id(0)

    @pl.when(i == 0)
    def _():
        carry_ref[...] = jnp.zeros_like(carry_ref)

    pos0 = (i * TM) % seq
    ubuf_ref[0:HALO, :] = jnp.where(pos0 > 0, up_ref[...].astype(jnp.float32), 0.0)
    ubuf_ref[HALO:HALO + TM, :] = u_ref[...].astype(jnp.float32)
    pos = (pos0 + lax.broadcasted_iota(jnp.int32, (TM, 1), 0) + 1).astype(jnp.float32)
    mixed = []
    for gi, w in enumerate(POOL_WINDOWS):
        lo = gi * POOL_GROUP_DIM
        tok = ubuf_ref[HALO:HALO + TM, lo:lo + POOL_GROUP_DIM]
        acc = tok
        for j in range(1, w):
            acc = acc + ubuf_ref[HALO - j:HALO - j + TM, lo:lo + POOL_GROUP_DIM]
        pooled = acc / jnp.minimum(pos, float(w)) - tok
        mg = jnp.dot(pooled.astype(jnp.bfloat16), pw_ref[gi], preferred_element_type=jnp.float32)
        mixed.append((mg * ps_ref[:, lo:lo + POOL_GROUP_DIM]).astype(jnp.bfloat16))
    y_cat = jnp.concatenate([ym_ref[...]] + mixed, axis=1)

    x1 = x_ref[...] + jnp.dot(y_cat, wo_ref[...], preferred_element_type=jnp.float32)
    x1_ref[...] = x1
    h2 = x1 * lax.rsqrt(jnp.mean(x1 * x1, axis=-1, keepdims=True) + EPS) * g2_ref[...]
    h2b = h2.astype(jnp.bfloat16)
    for s in range(SLAB):
        h2_ref[pl.ds(s, TM, stride=SLAB), :] = h2[:, s * LANES:(s + 1) * LANES]

    logits = lax.dot_general(wrt_ref[...], h2b, NT_DIMS,
                             preferred_element_type=jnp.float32) + br_ref[...]
    e_id = lax.broadcasted_iota(jnp.int32, (N_EXPERTS, TM), 0).astype(jnp.float32)
    work = logits
    vals, ids, hots = [], [], []
    for _ in range(TOP_K):
        mk = jnp.max(work, axis=0, keepdims=True)
        ik = jnp.min(jnp.where(work == mk, e_id, float(N_EXPERTS)), axis=0, keepdims=True)
        hot = e_id == ik
        work = jnp.where(hot, -jnp.inf, work)
        vals.append(mk)
        ids.append(ik)
        hots.append(hot)
    ex = [jnp.exp(vk - vals[0]) for vk in vals]
    denom = ex[0] + ex[1] + ex[2] + ex[3]
    gate_ref[...] = jnp.concatenate([e / denom for e in ex], axis=0)
    idx_ref[...] = jnp.concatenate(ids, axis=0).astype(jnp.int32)

    sel_f = sum(jnp.where(hot, 1.0, 0.0) for hot in hots)
    t_row = lax.broadcasted_iota(jnp.int32, (TM, TM), 0)
    t_col = lax.broadcasted_iota(jnp.int32, (TM, TM), 1)
    before = jnp.where(t_row < t_col, 1.0, 0.0).astype(jnp.bfloat16)
    prefix = jnp.dot(sel_f.astype(jnp.bfloat16), before, preferred_element_type=jnp.float32)
    carry = carry_ref[...]
    rank_e = carry[:, 0:1] + prefix
    ranks = [jnp.sum(jnp.where(hot, rank_e, 0.0), axis=0, keepdims=True) for hot in hots]
    rank_ref[...] = jnp.concatenate(ranks, axis=0).astype(jnp.int32)
    carry_new = carry + jnp.sum(sel_f, axis=1, keepdims=True)
    carry_ref[...] = carry_new
    cnt_ref[...] = carry_new.astype(jnp.int32)


def _out_route(x2, ym, p, pool_w, pool_s, w_out, g2, wr_t, br, seq):
    T = x2.shape[0]
    TM = TM_PROJ
    nt = T // TM
    u_blk = N_MAIN // POOL_WIDTH - 1
    halo_per_tile = TM // HALO
    tok_spec = pl.BlockSpec((TOP_K, TM), lambda i: (0, i))
    return pl.pallas_call(
        functools.partial(_out_route_kernel, seq),
        grid=(nt,),
        in_specs=[
            pl.BlockSpec((TM, D_MODEL), lambda i: (i, 0)),
            pl.BlockSpec((TM, MLSTM_WIDTH), lambda i: (i, 0)),
            pl.BlockSpec((TM, POOL_WIDTH), lambda i: (i, u_blk)),
            pl.BlockSpec((HALO, POOL_WIDTH),
                         lambda i: (jnp.maximum(i * halo_per_tile - 1, 0), u_blk)),
            pl.BlockSpec((len(POOL_WINDOWS), POOL_GROUP_DIM, POOL_GROUP_DIM), lambda i: (0, 0, 0)),
            pl.BlockSpec((1, POOL_WIDTH), lambda i: (0, 0)),
            pl.BlockSpec((D_MODEL, D_MODEL), lambda i: (0, 0)),
            pl.BlockSpec((1, D_MODEL), lambda i: (0, 0)),
            pl.BlockSpec((N_EXPERTS, D_MODEL), lambda i: (0, 0)),
            pl.BlockSpec((N_EXPERTS, 1), lambda i: (0, 0)),
        ],
        out_specs=[
            pl.BlockSpec((TM, D_MODEL), lambda i: (i, 0)),
            pl.BlockSpec((TM * SLAB, LANES), lambda i: (i, 0)),
            tok_spec, tok_spec, tok_spec,
            pl.BlockSpec((N_EXPERTS, LANES), lambda i: (0, 0)),
        ],
        out_shape=[
            jax.ShapeDtypeStruct((T, D_MODEL), jnp.float32),
            jax.ShapeDtypeStruct((T * SLAB, LANES), jnp.float32),
            jax.ShapeDtypeStruct((TOP_K, T), jnp.int32),
            jax.ShapeDtypeStruct((TOP_K, T), jnp.float32),
            jax.ShapeDtypeStruct((TOP_K, T), jnp.int32),
            jax.ShapeDtypeStruct((N_EXPERTS, LANES), jnp.int32),
        ],
        scratch_shapes=[
            pltpu.VMEM((HALO + TM, POOL_WIDTH), jnp.float32),
            pltpu.VMEM((N_EXPERTS, LANES), jnp.float32),
        ],
        compiler_params=pltpu.CompilerParams(
            dimension_semantics=("arbitrary",), vmem_limit_bytes=VMEM_LIMIT),
        name="out_route",
    )(x2, ym, p, p, pool_w, pool_s, w_out, g2, wr_t, br)


def _plan_kernel(n_assign, dest_ref, fill_ref, slot_ref, sem):
    cp = pltpu.make_async_copy(fill_ref, slot_ref, sem)
    cp.start()
    cp.wait()

    def body(i, carry):
        base = i * PLAN_UNROLL
        for j in range(PLAN_UNROLL):
            slot_ref[dest_ref[base + j]] = base + j
        return carry

    lax.fori_loop(0, n_assign // PLAN_UNROLL, body, 0)


def _plan(dest_flat, fill):
    n_assign = dest_flat.shape[0]
    return pl.pallas_call(
        functools.partial(_plan_kernel, n_assign),
        in_specs=[
            pl.BlockSpec(memory_space=pltpu.SMEM),
            pl.BlockSpec(memory_space=pl.ANY),
        ],
        out_specs=pl.BlockSpec(memory_space=pltpu.SMEM),
        out_shape=jax.ShapeDtypeStruct(fill.shape, jnp.int32),
        scratch_shapes=[pltpu.SemaphoreType.DMA(())],
        name="plan",
    )(dest_flat, fill)


def _expert_kernel(n_tok, be_ref, nb_ref, slot_ref, h2_ref, wg_ref, bg_ref, wu_ref, bu_ref,
                   wd_ref, bd_ref, yt_ref, xg0_ref, xg1_ref, ys0_ref, ys1_ref,
                   wgb_ref, wub_ref, wdb_ref, gsem, ssem):
    TM = TM_EXPERT
    ROWS = TM * SLAB
    b = pl.program_id(0)
    nb = nb_ref[0]
    xg = (xg0_ref, xg1_ref)
    ys = (ys0_ref, ys1_ref)

    def token_of(a):
        return a & (n_tok - 1) if n_tok & (n_tok - 1) == 0 else lax.rem(a, n_tok)

    def start_gather(blk, par, r0=0, r1=TM_EXPERT):
        base = (blk + 1) * TM
        for r in range(r0, r1):
            t = token_of(slot_ref[base + r])
            pltpu.make_async_copy(h2_ref.at[pl.ds(pl.multiple_of(t * SLAB, SLAB), SLAB), :],
                                  xg[par].at[pl.ds(r * SLAB, SLAB), :], gsem.at[par]).start()

    def wait_gather(par):
        pltpu.make_async_copy(h2_ref.at[pl.ds(0, ROWS), :], xg[par], gsem.at[par]).wait()

    def start_scatter(blk, par, r0=0, r1=TM_EXPERT):
        base = (blk + 1) * TM
        for r in range(r0, r1):
            a = slot_ref[base + r]
            pltpu.make_async_copy(ys[par].at[pl.ds(r * SLAB, SLAB), :],
                                  yt_ref.at[pl.ds(pl.multiple_of(a * SLAB, SLAB), SLAB), :],
                                  ssem.at[par]).start()

    def wait_scatter(par):
        pltpu.make_async_copy(ys[par], yt_ref.at[pl.ds(0, ROWS), :], ssem.at[par]).wait()

    @pl.when(b == 0)
    def _():
        start_gather(0, 0)
        for par in range(2):
            ys[par][...] = jnp.zeros_like(ys[par])
            dump = yt_ref.at[pl.ds((n_tok * TOP_K + par * TM) * SLAB, ROWS), :]
            cp = pltpu.make_async_copy(ys[par], dump, ssem.at[par])
            cp.start()
            cp.wait()

    prev = be_ref[jnp.maximum(b - 1, 0)]
    new_expert = jnp.logical_or(b == 0, be_ref[b] != prev)

    @pl.when(jnp.logical_and(new_expert, b < nb))
    def _():
        wgb_ref[...] = wg_ref[0].astype(jnp.bfloat16)
        wub_ref[...] = wu_ref[0].astype(jnp.bfloat16)
        wdb_ref[...] = wd_ref[0].astype(jnp.bfloat16)

    def block_step(par):
        oth = 1 - par
        wait_gather(par)

        @pl.when(b >= 1)
        def _():
            wait_scatter(par)

        for r0 in range(0, TM, 8):
            start_gather(b + 1, oth, r0, r0 + 8)
            start_scatter(b - 1, oth, r0, r0 + 8)
        x = jnp.concatenate(
            [xg[par][pl.ds(s, TM, stride=SLAB), :].astype(jnp.bfloat16) for s in range(SLAB)],
            axis=1)
        gate = jnp.dot(x, wgb_ref[...], preferred_element_type=jnp.float32) + bg_ref[0]
        up = jnp.dot(x, wub_ref[...], preferred_element_type=jnp.float32) + bu_ref[0]
        gate = jnp.minimum(gate, SWIGLU_LIMIT)
        up = jnp.clip(up, -SWIGLU_LIMIT, SWIGLU_LIMIT)
        glu = gate * _sigmoid(SWIGLU_ALPHA * gate)
        act = (glu * (up + 1.0)).astype(jnp.bfloat16)
        y = jnp.dot(act, wdb_ref[...], preferred_element_type=jnp.float32) + bd_ref[0]
        for s in range(SLAB):
            ys[par][pl.ds(s, TM, stride=SLAB), :] = y[:, s * LANES:(s + 1) * LANES]

    def drain_step(par):
        oth = 1 - par
        wait_gather(par)
        wait_scatter(par)
        start_scatter(b - 1, oth)
        wait_scatter(oth)

    for par in range(2):
        is_par = b % 2 == par
        pl.when(jnp.logical_and(b < nb, is_par))(functools.partial(block_step, par))
        pl.when(jnp.logical_and(b == nb, is_par))(functools.partial(drain_step, par))


def _experts(block_expert, n_used, slot_buf, h2_slab, w_gate, b_gate, w_up, b_up, w_down, b_down,
             n_tok):
    n_blocks = block_expert.shape[0]
    TM = TM_EXPERT
    n_assign = n_tok * TOP_K
    w_spec = pl.BlockSpec((1, D_MODEL, D_FF), lambda b, be, nb, sl: (be[b], 0, 0))
    bias_spec = pl.BlockSpec((1, 1, D_FF), lambda b, be, nb, sl: (be[b], 0, 0))
    grid_spec = pltpu.PrefetchScalarGridSpec(
        num_scalar_prefetch=3,
        grid=(n_blocks,),
        in_specs=[
            pl.BlockSpec(memory_space=pl.ANY),
            w_spec, bias_spec, w_spec, bias_spec, w_spec, bias_spec,
        ],
        out_specs=pl.BlockSpec(memory_space=pl.ANY),
        scratch_shapes=[
            pltpu.VMEM((TM * SLAB, LANES), jnp.float32),
            pltpu.VMEM((TM * SLAB, LANES), jnp.float32),
            pltpu.VMEM((TM * SLAB, LANES), jnp.float32),
            pltpu.VMEM((TM * SLAB, LANES), jnp.float32),
            pltpu.VMEM((D_MODEL, D_FF), jnp.bfloat16),
            pltpu.VMEM((D_MODEL, D_FF), jnp.bfloat16),
            pltpu.VMEM((D_FF, D_MODEL), jnp.bfloat16),
            pltpu.SemaphoreType.DMA((2,)),
            pltpu.SemaphoreType.DMA((2,)),
        ],
    )
    return pl.pallas_call(
        functools.partial(_expert_kernel, n_tok),
        grid_spec=grid_spec,
        out_shape=jax.ShapeDtypeStruct(((n_assign + 2 * TM) * SLAB, LANES), jnp.float32),
        compiler_params=pltpu.CompilerParams(
            dimension_semantics=("arbitrary",), vmem_limit_bytes=VMEM_LIMIT),
        name="experts",
    )(block_expert, n_used, slot_buf, h2_slab, w_gate, b_gate, w_up, b_up, w_down, b_down)


def _combine_kernel(normalize, x1_ref, y0_ref, y1_ref, y2_ref, y3_ref, gate_ref, g_ref, o_ref):
    TM = TM_PROJ
    gates = jnp.concatenate([gate_ref[...], jnp.zeros((8 - TOP_K, TM), jnp.float32)], axis=0)
    g_cols = jnp.transpose(gates)
    g_bc = [jnp.broadcast_to(g_cols[:, k:k + 1], (TM, LANES)) for k in range(TOP_K)]
    parts = []
    ssq = jnp.zeros((TM, LANES), jnp.float32)
    for s in range(SLAB):
        acc = x1_ref[:, s * LANES:(s + 1) * LANES]
        for k, y_ref in enumerate((y0_ref, y1_ref, y2_ref, y3_ref)):
            acc = acc + g_bc[k] * y_ref[pl.ds(s, TM, stride=SLAB), :]
        parts.append(acc)
        ssq = ssq + acc * acc
    if normalize:
        inv = lax.rsqrt(jnp.sum(ssq, axis=-1, keepdims=True) * (1.0 / D_MODEL) + EPS)
        for s in range(SLAB):
            o_ref[:, s * LANES:(s + 1) * LANES] = parts[s] * inv * g_ref[:, s * LANES:(s + 1) * LANES]
    else:
        for s in range(SLAB):
            o_ref[:, s * LANES:(s + 1) * LANES] = parts[s]


def _combine(x1, y_tok, gate_t, gf, normalize):
    T = x1.shape[0]
    TM = TM_PROJ
    nt = T // TM

    def y_spec(k):
        return pl.BlockSpec((TM * SLAB, LANES), lambda i: (k * nt + i, 0))

    return pl.pallas_call(
        functools.partial(_combine_kernel, normalize),
        grid=(nt,),
        in_specs=[
            pl.BlockSpec((TM, D_MODEL), lambda i: (i, 0)),
            y_spec(0), y_spec(1), y_spec(2), y_spec(3),
            pl.BlockSpec((TOP_K, TM), lambda i: (0, i)),
            pl.BlockSpec((1, D_MODEL), lambda i: (0, 0)),
        ],
        out_specs=pl.BlockSpec((TM, D_MODEL), lambda i: (i, 0)),
        out_shape=jax.ShapeDtypeStruct((T, D_MODEL), jnp.float32),
        compiler_params=pltpu.CompilerParams(
            dimension_semantics=("parallel",), vmem_limit_bytes=VMEM_LIMIT),
        name="combine",
    )(x1, y_tok, y_tok, y_tok, y_tok, gate_t, gf)


def kernel(x, norm1_g, w_in, ig_b, fg_b, conv_w, head_norm_g, pool_w, pool_scale, w_out, norm2_g,
           w_router, b_router, w_gate, b_gate, w_up, b_up, w_down, b_down, normf_g):
    B, S, D = x.shape
    T = B * S
    depth = norm1_g.shape[0]
    W = MLSTM_WIDTH
    f32, bf16 = jnp.float32, jnp.bfloat16

    L = CHUNK
    t_l = lax.broadcasted_iota(jnp.int32, (L, L), 0)
    t_r = lax.broadcasted_iota(jnp.int32, (L, L), 1)
    tri = jnp.concatenate([(t_r <= t_l).astype(f32), (t_r == t_l).astype(f32)], axis=0)

    n_assign = T * TOP_K
    n_blocks = -(-n_assign // TM_EXPERT) + N_EXPERTS
    n_rows = n_blocks * TM_EXPERT
    fill = n_assign + (jnp.arange(n_rows + TM_EXPERT, dtype=jnp.int32) + TM_EXPERT) % (2 * TM_EXPERT)
    x2 = x.reshape(T, D)
    for l in range(depth):
        w = w_in[l]
        w_main = jnp.concatenate([w[:, :4 * W], w[:, 4 * W + N_GATES:]], axis=1).astype(bf16)
        wg_t = jnp.zeros((BF16_SUBLANES, D), bf16).at[:N_GATES].set(
            w[:, 4 * W:4 * W + N_GATES].T.astype(bf16))
        p, gates_t = _in_proj(x2, norm1_g[l][None, :], w_main, wg_t)

        gate_b = jnp.concatenate([ig_b[l], fg_b[l]])[:, None].astype(f32)
        ym = _mlstm(p, gates_t, conv_w[l].astype(f32), gate_b, head_norm_g[l][None, :], tri, B, S)

        x1, h2, idx_t, gate_t, rank_t, cnt = _out_route(
            x2, ym, p, pool_w[l].astype(bf16), pool_scale[l][None, :], w_out[l].astype(bf16),
            norm2_g[l][None, :], w_router[l].T.astype(bf16), b_router[l][:, None], S)

        counts = cnt[:, 0]
        padded = ((counts + TM_EXPERT - 1) // TM_EXPERT) * TM_EXPERT
        padded_end = jnp.cumsum(padded)
        padded_start = padded_end - padded
        expert_ids = jnp.arange(N_EXPERTS, dtype=jnp.int32)[:, None, None]
        start_of = jnp.sum(jnp.where(idx_t[None] == expert_ids, padded_start[:, None, None], 0), axis=0)
        dest = start_of + rank_t
        n_used = (padded_end[-1] // TM_EXPERT).astype(jnp.int32)[None]
        block_start = jnp.arange(n_blocks, dtype=jnp.int32) * TM_EXPERT
        used_start = jnp.minimum(block_start, padded_end[-1] - TM_EXPERT)
        block_expert = jnp.sum(used_start[:, None] >= padded_end[None, :], axis=1).astype(jnp.int32)
        block_expert = jnp.minimum(block_expert, N_EXPERTS - 1)

        slot_buf = _plan(dest.reshape(-1) + TM_EXPERT, fill)
        y_tok = _experts(block_expert, n_used, slot_buf, h2, w_gate[l], b_gate[l][:, None, :],
                         w_up[l], b_up[l][:, None, :], w_down[l], b_down[l][:, None, :], T)
        last = l + 1 == depth
        x2 = _combine(x1, y_tok, gate_t, normf_g[None, :], last)
    return x2.reshape(B, S, D)
```

```python
import functools

import jax
import jax.numpy as jnp
from jax import lax
from jax.experimental import pallas as pl
from jax.experimental.pallas import tpu as pltpu

D_MODEL = 1024
MLSTM_WIDTH = 512
MLSTM_HEADS = 4
HEAD_DIM = 128
CONV_WIDTH = 4
POOL_WIDTH = 512
POOL_WINDOWS = (2, 4, 8, 16)
POOL_GROUP_DIM = 128
N_EXPERTS = 32
TOP_K = 4
D_FF = 1024
SWIGLU_LIMIT = 7.0
SWIGLU_ALPHA = 1.702
EPS = 1e-5

N_MAIN = 4 * MLSTM_WIDTH + POOL_WIDTH
N_GATES = 2 * MLSTM_HEADS

LANES = 128
BF16_SUBLANES = 16
VMEM_LIMIT = 56 * 1024 * 1024

TM_PROJ = 512
CHUNK = 256
HALO = 16
TM_EXPERT = 512
SLAB = D_MODEL // LANES
PLAN_UNROLL = 16

NT_DIMS = (((1,), (1,)), ((), ()))
TN_DIMS = (((0,), (0,)), ((), ()))


def _sigmoid(x):
    return 1.0 / (1.0 + jnp.exp(-x))


def _in_proj_kernel(x_ref, g_ref, w_ref, wgt_ref, p_ref, gt_ref):
    x = x_ref[...]
    h = x * lax.rsqrt(jnp.mean(x * x, axis=-1, keepdims=True) + EPS) * g_ref[...]
    hb = h.astype(jnp.bfloat16)
    p_ref[...] = jnp.dot(hb, w_ref[...], preferred_element_type=jnp.float32).astype(p_ref.dtype)
    gt = lax.dot_general(wgt_ref[...], hb, NT_DIMS, preferred_element_type=jnp.float32)
    gt_ref[...] = gt[:N_GATES]


def _in_proj(x2, g1, w_main, wg_t):
    T = x2.shape[0]
    return pl.pallas_call(
        _in_proj_kernel,
        grid=(T // TM_PROJ,),
        in_specs=[
            pl.BlockSpec((TM_PROJ, D_MODEL), lambda i: (i, 0)),
            pl.BlockSpec((1, D_MODEL), lambda i: (0, 0)),
            pl.BlockSpec((D_MODEL, N_MAIN), lambda i: (0, 0)),
            pl.BlockSpec((BF16_SUBLANES, D_MODEL), lambda i: (0, 0)),
        ],
        out_specs=[
            pl.BlockSpec((TM_PROJ, N_MAIN), lambda i: (i, 0)),
            pl.BlockSpec((N_GATES, TM_PROJ), lambda i: (0, i)),
        ],
        out_shape=[
            jax.ShapeDtypeStruct((T, N_MAIN), jnp.bfloat16),
            jax.ShapeDtypeStruct((N_GATES, T), jnp.float32),
        ],
        compiler_params=pltpu.CompilerParams(
            dimension_semantics=("parallel",), vmem_limit_bytes=VMEM_LIMIT),
        name="in_proj",
    )(x2, g1, w_main, wg_t)


def _mlstm_kernel(qk_ref, qkp_ref, v_ref, o_ref, gt_ref, convw_ref, gb_ref, hng_ref,
                  tri_ref, y_ref, ext_ref, cn_ref, m_ref):
    L = CHUNK
    c = pl.program_id(1)

    @pl.when(c == 0)
    def _():
        cn_ref[...] = jnp.zeros_like(cn_ref)
        m_ref[...] = jnp.zeros_like(m_ref)

    halo = qkp_ref[...].astype(jnp.float32)
    ext_ref[0:HALO, :] = jnp.where(c > 0, halo, 0.0)
    ext_ref[HALO:HALO + L, :] = qk_ref[...].astype(jnp.float32)
    acc = None
    for j in range(CONV_WIDTH):
        off = HALO - (CONV_WIDTH - 1) + j
        term = convw_ref[j:j + 1, :] * ext_ref[off:off + L, :]
        acc = term if acc is None else acc + term
    qk = acc * _sigmoid(acc)

    gt = gt_ref[...] + gb_ref[...]
    f = gt[MLSTM_HEADS:]
    lf = jnp.minimum(f, 0.0) - jnp.log(1.0 + jnp.exp(-jnp.abs(f)))
    ig = gt[:MLSTM_HEADS]
    rows = jnp.concatenate([lf, ig], axis=0)
    tri = tri_ref[...]
    cols = lax.dot_general(tri, rows, NT_DIMS, precision=lax.Precision.HIGHEST,
                           preferred_element_type=jnp.float32)
    b_cols = cols[:L, :MLSTM_HEADS]
    ig_cols = cols[L:, MLSTM_HEADS:]
    b_rows = lax.dot_general(lf, tri[:L], NT_DIMS, precision=lax.Precision.HIGHEST,
                             preferred_element_type=jnp.float32)
    c_rows = ig - b_rows

    row_id = lax.broadcasted_iota(jnp.int32, (L, L), 0)
    col_id = lax.broadcasted_iota(jnp.int32, (L, L), 1)
    causal = col_id <= row_id
    ones_col = (lax.broadcasted_iota(jnp.int32, (L, HEAD_DIM), 1) == 0).astype(jnp.bfloat16)

    for h in range(MLSTM_HEADS):
        lo = h * HEAD_DIM
        q = qk[:, lo:lo + HEAD_DIM].astype(jnp.bfloat16)
        k_f = qk[:, MLSTM_WIDTH + lo:MLSTM_WIDTH + lo + HEAD_DIM] * (HEAD_DIM ** -0.5)
        v_ext = jnp.concatenate([v_ref[:, lo:lo + HEAD_DIM], ones_col], axis=1)
        b_col = b_cols[:, h:h + 1]
        ig_col = ig_cols[:, h:h + 1]
        c_row = c_rows[h:h + 1, :]
        b_tot = b_rows[h:h + 1, L - 1:L]
        m_in = m_ref[h][0:1, 0:1]
        cn = cn_ref[h]

        s_qk = lax.dot_general(q, k_f.astype(jnp.bfloat16), NT_DIMS,
                               preferred_element_type=jnp.float32)
        log_d = jnp.where(causal, b_col + c_row, -jnp.inf)
        a_col = b_col + m_in
        m_out = jnp.maximum(a_col, jnp.max(log_d, axis=-1, keepdims=True))
        s = (s_qk * jnp.exp(log_d - m_out)).astype(jnp.bfloat16)
        inter = jnp.exp(a_col - m_out)
        num = (jnp.dot(s, v_ext, preferred_element_type=jnp.float32)
               + inter * jnp.dot(q, cn.astype(jnp.bfloat16), preferred_element_type=jnp.float32))
        den = num[:, HEAD_DIM:HEAD_DIM + 1]
        hh = num[:, :HEAD_DIM] / jnp.maximum(jnp.abs(den), jnp.exp(-m_out))

        mu = jnp.mean(hh, axis=-1, keepdims=True)
        d = hh - mu
        var = jnp.mean(d * d, axis=-1, keepdims=True)
        hn = d * lax.rsqrt(var + EPS) * hng_ref[:, lo:lo + HEAD_DIM]
        og = _sigmoid(o_ref[:, lo:lo + HEAD_DIM].astype(jnp.float32))
        y_ref[:, lo:lo + HEAD_DIM] = (og * hn).astype(y_ref.dtype)

        g_col = b_tot - b_col + ig_col
        m_loc = jnp.max(g_col, axis=0, keepdims=True)
        kw = (k_f * jnp.exp(g_col - m_loc)).astype(jnp.bfloat16)
        c_loc = lax.dot_general(kw, v_ext, TN_DIMS, preferred_element_type=jnp.float32)
        m_new = jnp.maximum(b_tot + m_in, m_loc)
        s_old = jnp.exp(b_tot + m_in - m_new)
        s_loc = jnp.exp(m_loc - m_new)
        cn_ref[h] = s_old * cn + s_loc * c_loc
        m_ref[h] = jnp.broadcast_to(m_new, m_ref.shape[1:])


def _mlstm(p, gates_t, conv_w, gate_b, hn_g, tri, batch, seq):
    T = batch * seq
    L = CHUNK
    nc = seq // L
    halo_per_chunk = L // HALO

    def cur(bi, ci):
        return bi * nc + ci

    return pl.pallas_call(
        _mlstm_kernel,
        grid=(batch, nc),
        in_specs=[
            pl.BlockSpec((L, 2 * MLSTM_WIDTH), lambda bi, ci: (cur(bi, ci), 0)),
            pl.BlockSpec((HALO, 2 * MLSTM_WIDTH),
                         lambda bi, ci: (jnp.maximum(cur(bi, ci) * halo_per_chunk - 1, 0), 0)),
            pl.BlockSpec((L, MLSTM_WIDTH), lambda bi, ci: (cur(bi, ci), 2)),
            pl.BlockSpec((L, MLSTM_WIDTH), lambda bi, ci: (cur(bi, ci), 3)),
            pl.BlockSpec((N_GATES, L), lambda bi, ci: (0, cur(bi, ci))),
            pl.BlockSpec((CONV_WIDTH, 2 * MLSTM_WIDTH), lambda bi, ci: (0, 0)),
            pl.BlockSpec((N_GATES, 1), lambda bi, ci: (0, 0)),
            pl.BlockSpec((1, MLSTM_WIDTH), lambda bi, ci: (0, 0)),
            pl.BlockSpec((2 * L, L), lambda bi, ci: (0, 0)),
        ],
        out_specs=pl.BlockSpec((L, MLSTM_WIDTH), lambda bi, ci: (cur(bi, ci), 0)),
        out_shape=jax.ShapeDtypeStruct((T, MLSTM_WIDTH), jnp.bfloat16),
        scratch_shapes=[
            pltpu.VMEM((HALO + L, 2 * MLSTM_WIDTH), jnp.float32),
            pltpu.VMEM((MLSTM_HEADS, HEAD_DIM, 2 * HEAD_DIM), jnp.float32),
            pltpu.VMEM((MLSTM_HEADS, 8, LANES), jnp.float32),
        ],
        compiler_params=pltpu.CompilerParams(
            dimension_semantics=("parallel", "arbitrary"), vmem_limit_bytes=VMEM_LIMIT),
        name="mlstm",
    )(p, p, p, p, gates_t, conv_w, gate_b, hn_g, tri)


def _out_route_kernel(seq, x_ref, ym_ref, u_ref, up_ref, pw_ref, ps_ref, wo_ref, g2_ref,
                      wrt_ref, br_ref, x1_ref, h2_ref, idx_ref, gate_ref, rank_ref, cnt_ref,
                      ubuf_ref, carry_ref):
    TM = TM_PROJ
    i = pl.program_id(0)

    @pl.when(i == 0)
    def _():
        carry_ref[...] = jnp.zeros_like(carry_ref)

    pos0 = (i * TM) % seq
    ubuf_ref[0:HALO, :] = jnp.where(pos0 > 0, up_ref[...].astype(jnp.float32), 0.0)
    ubuf_ref[HALO:HALO + TM, :] = u_ref[...].astype(jnp.float32)
    pos = (pos0 + lax.broadcasted_iota(jnp.int32, (TM, 1), 0) + 1).astype(jnp.float32)
    mixed = []
    for gi, w in enumerate(POOL_WINDOWS):
        lo = gi * POOL_GROUP_DIM
        tok = ubuf_ref[HALO:HALO + TM, lo:lo + POOL_GROUP_DIM]
        acc = tok
        for j in range(1, w):
            acc = acc + ubuf_ref[HALO - j:HALO - j + TM, lo:lo + POOL_GROUP_DIM]
        pooled = acc / jnp.minimum(pos, float(w)) - tok
        mg = jnp.dot(pooled.astype(jnp.bfloat16), pw_ref[gi], preferred_element_type=jnp.float32)
        mixed.append((mg * ps_ref[:, lo:lo + POOL_GROUP_DIM]).astype(jnp.bfloat16))
    y_cat = jnp.concatenate([ym_ref[...]] + mixed, axis=1)

    x1 = x_ref[...] + jnp.dot(y_cat, wo_ref[...], preferred_element_type=jnp.float32)
    x1_ref[...] = x1
    h2 = x1 * lax.rsqrt(jnp.mean(x1 * x1, axis=-1, keepdims=True) + EPS) * g2_ref[...]
    h2b = h2.astype(jnp.bfloat16)
    for s in range(SLAB):
        h2_ref[pl.ds(s, TM, stride=SLAB), :] = h2[:, s * LANES:(s + 1) * LANES]

    logits = lax.dot_general(wrt_ref[...], h2b, NT_DIMS,
                             preferred_element_type=jnp.float32) + br_ref[...]
    e_id = lax.broadcasted_iota(jnp.int32, (N_EXPERTS, TM), 0).astype(jnp.float32)
    work = logits
    vals, ids, hots = [], [], []
    for _ in range(TOP_K):
        mk = jnp.max(work, axis=0, keepdims=True)
        ik = jnp.min(jnp.where(work == mk, e_id, float(N_EXPERTS)), axis=0, keepdims=True)
        hot = e_id == ik
        work = jnp.where(hot, -jnp.inf, work)
        vals.append(mk)
        ids.append(ik)
        hots.append(hot)
    ex = [jnp.exp(vk - vals[0]) for vk in vals]
    denom = ex[0] + ex[1] + ex[2] + ex[3]
    gate_ref[...] = jnp.concatenate([e / denom for e in ex], axis=0)
    idx_ref[...] = jnp.concatenate(ids, axis=0).astype(jnp.int32)

    sel_f = sum(jnp.where(hot, 1.0, 0.0) for hot in hots)
    t_row = lax.broadcasted_iota(jnp.int32, (TM, TM), 0)
    t_col = lax.broadcasted_iota(jnp.int32, (TM, TM), 1)
    before = jnp.where(t_row < t_col, 1.0, 0.0).astype(jnp.bfloat16)
    prefix = jnp.dot(sel_f.astype(jnp.bfloat16), before, preferred_element_type=jnp.float32)
    carry = carry_ref[...]
    rank_e = carry[:, 0:1] + prefix
    ranks = [jnp.sum(jnp.where(hot, rank_e, 0.0), axis=0, keepdims=True) for hot in hots]
    rank_ref[...] = jnp.concatenate(ranks, axis=0).astype(jnp.int32)
    carry_new = carry + jnp.sum(sel_f, axis=1, keepdims=True)
    carry_ref[...] = carry_new
    cnt_ref[...] = carry_new.astype(jnp.int32)


def _out_route(x2, ym, p, pool_w, pool_s, w_out, g2, wr_t, br, seq):
    T = x2.shape[0]
    TM = TM_PROJ
    nt = T // TM
    u_blk = N_MAIN // POOL_WIDTH - 1
    halo_per_tile = TM // HALO
    tok_spec = pl.BlockSpec((TOP_K, TM), lambda i: (0, i))
    return pl.pallas_call(
        functools.partial(_out_route_kernel, seq),
        grid=(nt,),
        in_specs=[
            pl.BlockSpec((TM, D_MODEL), lambda i: (i, 0)),
            pl.BlockSpec((TM, MLSTM_WIDTH), lambda i: (i, 0)),
            pl.BlockSpec((TM, POOL_WIDTH), lambda i: (i, u_blk)),
            pl.BlockSpec((HALO, POOL_WIDTH),
                         lambda i: (jnp.maximum(i * halo_per_tile - 1, 0), u_blk)),
            pl.BlockSpec((len(POOL_WINDOWS), POOL_GROUP_DIM, POOL_GROUP_DIM), lambda i: (0, 0, 0)),
            pl.BlockSpec((1, POOL_WIDTH), lambda i: (0, 0)),
            pl.BlockSpec((D_MODEL, D_MODEL), lambda i: (0, 0)),
            pl.BlockSpec((1, D_MODEL), lambda i: (0, 0)),
            pl.BlockSpec((N_EXPERTS, D_MODEL), lambda i: (0, 0)),
            pl.BlockSpec((N_EXPERTS, 1), lambda i: (0, 0)),
        ],
        out_specs=[
            pl.BlockSpec((TM, D_MODEL), lambda i: (i, 0)),
            pl.BlockSpec((TM * SLAB, LANES), lambda i: (i, 0)),
            tok_spec, tok_spec, tok_spec,
            pl.BlockSpec((N_EXPERTS, LANES), lambda i: (0, 0)),
        ],
        out_shape=[
            jax.ShapeDtypeStruct((T, D_MODEL), jnp.float32),
            jax.ShapeDtypeStruct((T * SLAB, LANES), jnp.float32),
            jax.ShapeDtypeStruct((TOP_K, T), jnp.int32),
            jax.ShapeDtypeStruct((TOP_K, T), jnp.float32),
            jax.ShapeDtypeStruct((TOP_K, T), jnp.int32),
            jax.ShapeDtypeStruct((N_EXPERTS, LANES), jnp.int32),
        ],
        scratch_shapes=[
            pltpu.VMEM((HALO + TM, POOL_WIDTH), jnp.float32),
            pltpu.VMEM((N_EXPERTS, LANES), jnp.float32),
        ],
        compiler_params=pltpu.CompilerParams(
            dimension_semantics=("arbitrary",), vmem_limit_bytes=VMEM_LIMIT),
        name="out_route",
    )(x2, ym, p, p, pool_w, pool_s, w_out, g2, wr_t, br)


def _plan_kernel(n_assign, dest_ref, fill_ref, slot_ref, sem):
    cp = pltpu.make_async_copy(fill_ref, slot_ref, sem)
    cp.start()
    cp.wait()

    def body(i, carry):
        base = i * PLAN_UNROLL
        for j in range(PLAN_UNROLL):
            slot_ref[dest_ref[base + j]] = base + j
        return carry

    lax.fori_loop(0, n_assign // PLAN_UNROLL, body, 0)


def _plan(dest_flat, fill):
    n_assign = dest_flat.shape[0]
    return pl.pallas_call(
        functools.partial(_plan_kernel, n_assign),
        in_specs=[
            pl.BlockSpec(memory_space=pltpu.SMEM),
            pl.BlockSpec(memory_space=pl.ANY),
        ],
        out_specs=pl.BlockSpec(memory_space=pltpu.SMEM),
        out_shape=jax.ShapeDtypeStruct(fill.shape, jnp.int32),
        scratch_shapes=[pltpu.SemaphoreType.DMA(())],
        name="plan",
    )(dest_flat, fill)


def _expert_kernel(n_tok, be_ref, nb_ref, slot_ref, h2_ref, wg_ref, bg_ref, wu_ref, bu_ref,
                   wd_ref, bd_ref, yt_ref, xg0_ref, xg1_ref, ys0_ref, ys1_ref,
                   wgb_ref, wub_ref, wdb_ref, gsem, ssem):
    TM = TM_EXPERT
    ROWS = TM * SLAB
    b = pl.program_id(0)
    nb = nb_ref[0]
    xg = (xg0_ref, xg1_ref)
    ys = (ys0_ref, ys1_ref)

    def token_of(a):
        return a & (n_tok - 1) if n_tok & (n_tok - 1) == 0 else lax.rem(a, n_tok)

    def start_gather(blk, par, r0=0, r1=TM_EXPERT):
        base = (blk + 1) * TM
        for r in range(r0, r1):
            t = token_of(slot_ref[base + r])
            pltpu.make_async_copy(h2_ref.at[pl.ds(pl.multiple_of(t * SLAB, SLAB), SLAB), :],
                                  xg[par].at[pl.ds(r * SLAB, SLAB), :], gsem.at[par]).start()

    def wait_gather(par):
        pltpu.make_async_copy(h2_ref.at[pl.ds(0, ROWS), :], xg[par], gsem.at[par]).wait()

    def start_scatter(blk, par, r0=0, r1=TM_EXPERT):
        base = (blk + 1) * TM
        for r in range(r0, r1):
            a = slot_ref[base + r]
            pltpu.make_async_copy(ys[par].at[pl.ds(r * SLAB, SLAB), :],
                                  yt_ref.at[pl.ds(pl.multiple_of(a * SLAB, SLAB), SLAB), :],
                                  ssem.at[par]).start()

    def wait_scatter(par):
        pltpu.make_async_copy(ys[par], yt_ref.at[pl.ds(0, ROWS), :], ssem.at[par]).wait()

    @pl.when(b == 0)
    def _():
        start_gather(0, 0)
        for par in range(2):
            ys[par][...] = jnp.zeros_like(ys[par])
            dump = yt_ref.at[pl.ds((n_tok * TOP_K + par * TM) * SLAB, ROWS), :]
            cp = pltpu.make_async_copy(ys[par], dump, ssem.at[par])
            cp.start()
            cp.wait()

    prev = be_ref[jnp.maximum(b - 1, 0)]
    new_expert = jnp.logical_or(b == 0, be_ref[b] != prev)

    @pl.when(jnp.logical_and(new_expert, b < nb))
    def _():
        wgb_ref[...] = wg_ref[0].astype(jnp.bfloat16)
        wub_ref[...] = wu_ref[0].astype(jnp.bfloat16)
        wdb_ref[...] = wd_ref[0].astype(jnp.bfloat16)

    def block_step(par):
        oth = 1 - par
        wait_gather(par)

        @pl.when(b >= 1)
        def _():
            wait_scatter(par)

        for r0 in range(0, TM, 8):
            start_gather(b + 1, oth, r0, r0 + 8)
            start_scatter(b - 1, oth, r0, r0 + 8)
        x = jnp.concatenate(
            [xg[par][pl.ds(s, TM, stride=SLAB), :].astype(jnp.bfloat16) for s in range(SLAB)],
            axis=1)
        gate = jnp.dot(x, wgb_ref[...], preferred_element_type=jnp.float32) + bg_ref[0]
        up = jnp.dot(x, wub_ref[...], preferred_element_type=jnp.float32) + bu_ref[0]
        gate = jnp.minimum(gate, SWIGLU_LIMIT)
        up = jnp.clip(up, -SWIGLU_LIMIT, SWIGLU_LIMIT)
        glu = gate * _sigmoid(SWIGLU_ALPHA * gate)
        act = (glu * (up + 1.0)).astype(jnp.bfloat16)
        y = jnp.dot(act, wdb_ref[...], preferred_element_type=jnp.float32) + bd_ref[0]
        for s in range(SLAB):
            ys[par][pl.ds(s, TM, stride=SLAB), :] = y[:, s * LANES:(s + 1) * LANES]

    def drain_step(par):
        oth = 1 - par
        wait_gather(par)
        wait_scatter(par)
        start_scatter(b - 1, oth)
        wait_scatter(oth)

    for par in range(2):
        is_par = b % 2 == par
        pl.when(jnp.logical_and(b < nb, is_par))(functools.partial(block_step, par))
        pl.when(jnp.logical_and(b == nb, is_par))(functools.partial(drain_step, par))


def _experts(block_expert, n_used, slot_buf, h2_slab, w_gate, b_gate, w_up, b_up, w_down, b_down,
             n_tok):
    n_blocks = block_expert.shape[0]
    TM = TM_EXPERT
    n_assign = n_tok * TOP_K
    w_spec = pl.BlockSpec((1, D_MODEL, D_FF), lambda b, be, nb, sl: (be[b], 0, 0))
    bias_spec = pl.BlockSpec((1, 1, D_FF), lambda b, be, nb, sl: (be[b], 0, 0))
    grid_spec = pltpu.PrefetchScalarGridSpec(
        num_scalar_prefetch=3,
        grid=(n_blocks,),
        in_specs=[
            pl.BlockSpec(memory_space=pl.ANY),
            w_spec, bias_spec, w_spec, bias_spec, w_spec, bias_spec,
        ],
        out_specs=pl.BlockSpec(memory_space=pl.ANY),
        scratch_shapes=[
            pltpu.VMEM((TM * SLAB, LANES), jnp.float32),
            pltpu.VMEM((TM * SLAB, LANES), jnp.float32),
            pltpu.VMEM((TM * SLAB, LANES), jnp.float32),
            pltpu.VMEM((TM * SLAB, LANES), jnp.float32),
            pltpu.VMEM((D_MODEL, D_FF), jnp.bfloat16),
            pltpu.VMEM((D_MODEL, D_FF), jnp.bfloat16),
            pltpu.VMEM((D_FF, D_MODEL), jnp.bfloat16),
            pltpu.SemaphoreType.DMA((2,)),
            pltpu.SemaphoreType.DMA((2,)),
        ],
    )
    return pl.pallas_call(
        functools.partial(_expert_kernel, n_tok),
        grid_spec=grid_spec,
        out_shape=jax.ShapeDtypeStruct(((n_assign + 2 * TM) * SLAB, LANES), jnp.float32),
        compiler_params=pltpu.CompilerParams(
            dimension_semantics=("arbitrary",), vmem_limit_bytes=VMEM_LIMIT),
        name="experts",
    )(block_expert, n_used, slot_buf, h2_slab, w_gate, b_gate, w_up, b_up, w_down, b_down)


def _combine_kernel(normalize, x1_ref, y0_ref, y1_ref, y2_ref, y3_ref, gate_ref, g_ref, o_ref):
    TM = TM_PROJ
    gates = jnp.concatenate([gate_ref[...], jnp.zeros((8 - TOP_K, TM), jnp.float32)], axis=0)
    g_cols = jnp.transpose(gates)
    g_bc = [jnp.broadcast_to(g_cols[:, k:k + 1], (TM, LANES)) for k in range(TOP_K)]
    parts = []
    ssq = jnp.zeros((TM, LANES), jnp.float32)
    for s in range(SLAB):
        acc = x1_ref[:, s * LANES:(s + 1) * LANES]
        for k, y_ref in enumerate((y0_ref, y1_ref, y2_ref, y3_ref)):
            acc = acc + g_bc[k] * y_ref[pl.ds(s, TM, stride=SLAB), :]
        parts.append(acc)
        ssq = ssq + acc * acc
    if normalize:
        inv = lax.rsqrt(jnp.sum(ssq, axis=-1, keepdims=True) * (1.0 / D_MODEL) + EPS)
        for s in range(SLAB):
            o_ref[:, s * LANES:(s + 1) * LANES] = parts[s] * inv * g_ref[:, s * LANES:(s + 1) * LANES]
    else:
        for s in range(SLAB):
            o_ref[:, s * LANES:(s + 1) * LANES] = parts[s]


def _combine(x1, y_tok, gate_t, gf, normalize):
    T = x1.shape[0]
    TM = TM_PROJ
    nt = T // TM

    def y_spec(k):
        return pl.BlockSpec((TM * SLAB, LANES), lambda i: (k * nt + i, 0))

    return pl.pallas_call(
        functools.partial(_combine_kernel, normalize),
        grid=(nt,),
        in_specs=[
            pl.BlockSpec((TM, D_MODEL), lambda i: (i, 0)),
            y_spec(0), y_spec(1), y_spec(2), y_spec(3),
            pl.BlockSpec((TOP_K, TM), lambda i: (0, i)),
            pl.BlockSpec((1, D_MODEL), lambda i: (0, 0)),
        ],
        out_specs=pl.BlockSpec((TM, D_MODEL), lambda i: (i, 0)),
        out_shape=jax.ShapeDtypeStruct((T, D_MODEL), jnp.float32),
        compiler_params=pltpu.CompilerParams(
            dimension_semantics=("parallel",), vmem_limit_bytes=VMEM_LIMIT),
        name="combine",
    )(x1, y_tok, y_tok, y_tok, y_tok, gate_t, gf)


def kernel(x, norm1_g, w_in, ig_b, fg_b, conv_w, head_norm_g, pool_w, pool_scale, w_out, norm2_g,
           w_router, b_router, w_gate, b_gate, w_up, b_up, w_down, b_down, normf_g):
    B, S, D = x.shape
    T = B * S
    depth = norm1_g.shape[0]
    W = MLSTM_WIDTH
    f32, bf16 = jnp.float32, jnp.bfloat16

    L = CHUNK
    t_l = lax.broadcasted_iota(jnp.int32, (L, L), 0)
    t_r = lax.broadcasted_iota(jnp.int32, (L, L), 1)
    tri = jnp.concatenate([(t_r <= t_l).astype(f32), (t_r == t_l).astype(f32)], axis=0)

    n_assign = T * TOP_K
    n_blocks = -(-n_assign // TM_EXPERT) + N_EXPERTS
    n_rows = n_blocks * TM_EXPERT
    fill = n_assign + (jnp.arange(n_rows + TM_EXPERT, dtype=jnp.int32) + TM_EXPERT) % (2 * TM_EXPERT)
    x2 = x.reshape(T, D)
    for l in range(depth):
        w = w_in[l]
        w_main = jnp.concatenate([w[:, :4 * W], w[:, 4 * W + N_GATES:]], axis=1).astype(bf16)
        wg_t = jnp.zeros((BF16_SUBLANES, D), bf16).at[:N_GATES].set(
            w[:, 4 * W:4 * W + N_GATES].T.astype(bf16))
        p, gates_t = _in_proj(x2, norm1_g[l][None, :], w_main, wg_t)

        gate_b = jnp.concatenate([ig_b[l], fg_b[l]])[:, None].astype(f32)
        ym = _mlstm(p, gates_t, conv_w[l].astype(f32), gate_b, head_norm_g[l][None, :], tri, B, S)

        x1, h2, idx_t, gate_t, rank_t, cnt = _out_route(
            x2, ym, p, pool_w[l].astype(bf16), pool_scale[l][None, :], w_out[l].astype(bf16),
            norm2_g[l][None, :], w_router[l].T.astype(bf16), b_router[l][:, None], S)

        counts = cnt[:, 0]
        padded = ((counts + TM_EXPERT - 1) // TM_EXPERT) * TM_EXPERT
        padded_end = jnp.cumsum(padded)
        padded_start = padded_end - padded
        expert_ids = jnp.arange(N_EXPERTS, dtype=jnp.int32)[:, None, None]
        start_of = jnp.sum(jnp.where(idx_t[None] == expert_ids, padded_start[:, None, None], 0), axis=0)
        dest = start_of + rank_t
        n_used = (padded_end[-1] // TM_EXPERT).astype(jnp.int32)[None]
        block_start = jnp.arange(n_blocks, dtype=jnp.int32) * TM_EXPERT
        used_start = jnp.minimum(block_start, padded_end[-1] - TM_EXPERT)
        block_expert = jnp.sum(used_start[:, None] >= padded_end[None, :], axis=1).astype(jnp.int32)
        block_expert = jnp.minimum(block_expert, N_EXPERTS - 1)

        slot_buf = _plan(dest.reshape(-1) + TM_EXPERT, fill)
        y_tok = _experts(block_expert, n_used, slot_buf, h2, w_gate[l], b_gate[l][:, None, :],
                         w_up[l], b_up[l][:, None, :], w_down[l], b_down[l][:, None, :], T)
        last = l + 1 == depth
        x2 = _combine(x1, y_tok, gate_t, normf_g[None, :], last)
    return x2.reshape(B, S, D)
```

```python
import functools

import jax
import jax.numpy as jnp
from jax import lax
from jax.experimental import pallas as pl
from jax.experimental.pallas import tpu as pltpu

D_MODEL = 1024
MLSTM_WIDTH = 512
MLSTM_HEADS = 4
HEAD_DIM = 128
CONV_WIDTH = 4
POOL_WIDTH = 512
POOL_WINDOWS = (2, 4, 8, 16)
POOL_GROUP_DIM = 128
N_EXPERTS = 32
TOP_K = 4
D_FF = 1024
SWIGLU_LIMIT = 7.0
SWIGLU_ALPHA = 1.702
EPS = 1e-5

N_MAIN = 4 * MLSTM_WIDTH + POOL_WIDTH
N_GATES = 2 * MLSTM_HEADS

LANES = 128
BF16_SUBLANES = 16
VMEM_LIMIT = 56 * 1024 * 1024

TM_PROJ = 512
CHUNK = 256
HALO = 16
TM_EXPERT = 256
NBUF = 3
SLAB = D_MODEL // LANES
PLAN_UNROLL = 16

NT_DIMS = (((1,), (1,)), ((), ()))
TN_DIMS = (((0,), (0,)), ((), ()))


def _sigmoid(x):
    return 1.0 / (1.0 + jnp.exp(-x))


def _in_proj_kernel(x_ref, g_ref, w_ref, wgt_ref, p_ref, gt_ref):
    x = x_ref[...]
    h = x * lax.rsqrt(jnp.mean(x * x, axis=-1, keepdims=True) + EPS) * g_ref[...]
    hb = h.astype(jnp.bfloat16)
    p_ref[...] = jnp.dot(hb, w_ref[...], preferred_element_type=jnp.float32).astype(p_ref.dtype)
    gt = lax.dot_general(wgt_ref[...], hb, NT_DIMS, preferred_element_type=jnp.float32)
    gt_ref[...] = gt[:N_GATES]


def _in_proj(x2, g1, w_main, wg_t):
    T = x2.shape[0]
    return pl.pallas_call(
        _in_proj_kernel,
        grid=(T // TM_PROJ,),
        in_specs=[
            pl.BlockSpec((TM_PROJ, D_MODEL), lambda i: (i, 0)),
            pl.BlockSpec((1, D_MODEL), lambda i: (0, 0)),
            pl.BlockSpec((D_MODEL, N_MAIN), lambda i: (0, 0)),
            pl.BlockSpec((BF16_SUBLANES, D_MODEL), lambda i: (0, 0)),
        ],
        out_specs=[
            pl.BlockSpec((TM_PROJ, N_MAIN), lambda i: (i, 0)),
            pl.BlockSpec((N_GATES, TM_PROJ), lambda i: (0, i)),
        ],
        out_shape=[
            jax.ShapeDtypeStruct((T, N_MAIN), jnp.bfloat16),
            jax.ShapeDtypeStruct((N_GATES, T), jnp.float32),
        ],
        compiler_params=pltpu.CompilerParams(
            dimension_semantics=("parallel",), vmem_limit_bytes=VMEM_LIMIT),
        name="in_proj",
    )(x2, g1, w_main, wg_t)


def _mlstm_kernel(qk_ref, qkp_ref, v_ref, o_ref, gt_ref, convw_ref, gb_ref, hng_ref,
                  tri_ref, y_ref, ext_ref, cn_ref, m_ref):
    L = CHUNK
    c = pl.program_id(1)

    @pl.when(c == 0)
    def _():
        cn_ref[...] = jnp.zeros_like(cn_ref)
        m_ref[...] = jnp.zeros_like(m_ref)

    halo = qkp_ref[...].astype(jnp.float32)
    ext_ref[0:HALO, :] = jnp.where(c > 0, halo, 0.0)
    ext_ref[HALO:HALO + L, :] = qk_ref[...].astype(jnp.float32)
    acc = None
    for j in range(CONV_WIDTH):
        off = HALO - (CONV_WIDTH - 1) + j
        term = convw_ref[j:j + 1, :] * ext_ref[off:off + L, :]
        acc = term if acc is None else acc + term
    qk = acc * _sigmoid(acc)

    gt = gt_ref[...] + gb_ref[...]
    f = gt[MLSTM_HEADS:]
    lf = jnp.minimum(f, 0.0) - jnp.log(1.0 + jnp.exp(-jnp.abs(f)))
    ig = gt[:MLSTM_HEADS]
    rows = jnp.concatenate([lf, ig], axis=0)
    tri = tri_ref[...]
    cols = lax.dot_general(tri, rows, NT_DIMS, precision=lax.Precision.HIGHEST,
                           preferred_element_type=jnp.float32)
    b_cols = cols[:L, :MLSTM_HEADS]
    ig_cols = cols[L:, MLSTM_HEADS:]
    b_rows = lax.dot_general(lf, tri[:L], NT_DIMS, precision=lax.Precision.HIGHEST,
                             preferred_element_type=jnp.float32)
    c_rows = ig - b_rows

    row_id = lax.broadcasted_iota(jnp.int32, (L, L), 0)
    col_id = lax.broadcasted_iota(jnp.int32, (L, L), 1)
    causal = col_id <= row_id
    ones_col = (lax.broadcasted_iota(jnp.int32, (L, HEAD_DIM), 1) == 0).astype(jnp.bfloat16)

    for h in range(MLSTM_HEADS):
        lo = h * HEAD_DIM
        q = qk[:, lo:lo + HEAD_DIM].astype(jnp.bfloat16)
        k_f = qk[:, MLSTM_WIDTH + lo:MLSTM_WIDTH + lo + HEAD_DIM] * (HEAD_DIM ** -0.5)
        v_ext = jnp.concatenate([v_ref[:, lo:lo + HEAD_DIM], ones_col], axis=1)
        b_col = b_cols[:, h:h + 1]
        ig_col = ig_cols[:, h:h + 1]
        c_row = c_rows[h:h + 1, :]
        b_tot = b_rows[h:h + 1, L - 1:L]
        m_in = m_ref[h][0:1, 0:1]
        cn = cn_ref[h]

        s_qk = lax.dot_general(q, k_f.astype(jnp.bfloat16), NT_DIMS,
                               preferred_element_type=jnp.float32)
        log_d = jnp.where(causal, b_col + c_row, -jnp.inf)
        a_col = b_col + m_in
        m_out = jnp.maximum(a_col, jnp.max(log_d, axis=-1, keepdims=True))
        s = (s_qk * jnp.exp(log_d - m_out)).astype(jnp.bfloat16)
        inter = jnp.exp(a_col - m_out)
        num = (jnp.dot(s, v_ext, preferred_element_type=jnp.float32)
               + inter * jnp.dot(q, cn.astype(jnp.bfloat16), preferred_element_type=jnp.float32))
        den = num[:, HEAD_DIM:HEAD_DIM + 1]
        hh = num[:, :HEAD_DIM] / jnp.maximum(jnp.abs(den), jnp.exp(-m_out))

        mu = jnp.mean(hh, axis=-1, keepdims=True)
        d = hh - mu
        var = jnp.mean(d * d, axis=-1, keepdims=True)
        hn = d * lax.rsqrt(var + EPS) * hng_ref[:, lo:lo + HEAD_DIM]
        og = _sigmoid(o_ref[:, lo:lo + HEAD_DIM].astype(jnp.float32))
        y_ref[:, lo:lo + HEAD_DIM] = (og * hn).astype(y_ref.dtype)

        g_col = b_tot - b_col + ig_col
        m_loc = jnp.max(g_col, axis=0, keepdims=True)
        kw = (k_f * jnp.exp(g_col - m_loc)).astype(jnp.bfloat16)
        c_loc = lax.dot_general(kw, v_ext, TN_DIMS, preferred_element_type=jnp.float32)
        m_new = jnp.maximum(b_tot + m_in, m_loc)
        s_old = jnp.exp(b_tot + m_in - m_new)
        s_loc = jnp.exp(m_loc - m_new)
        cn_ref[h] = s_old * cn + s_loc * c_loc
        m_ref[h] = jnp.broadcast_to(m_new, m_ref.shape[1:])


def _mlstm(p, gates_t, conv_w, gate_b, hn_g, tri, batch, seq):
    T = batch * seq
    L = CHUNK
    nc = seq // L
    halo_per_chunk = L // HALO

    def cur(bi, ci):
        return bi * nc + ci

    return pl.pallas_call(
        _mlstm_kernel,
        grid=(batch, nc),
        in_specs=[
            pl.BlockSpec((L, 2 * MLSTM_WIDTH), lambda bi, ci: (cur(bi, ci), 0)),
            pl.BlockSpec((HALO, 2 * MLSTM_WIDTH),
                         lambda bi, ci: (jnp.maximum(cur(bi, ci) * halo_per_chunk - 1, 0), 0)),
            pl.BlockSpec((L, MLSTM_WIDTH), lambda bi, ci: (cur(bi, ci), 2)),
            pl.BlockSpec((L, MLSTM_WIDTH), lambda bi, ci: (cur(bi, ci), 3)),
            pl.BlockSpec((N_GATES, L), lambda bi, ci: (0, cur(bi, ci))),
            pl.BlockSpec((CONV_WIDTH, 2 * MLSTM_WIDTH), lambda bi, ci: (0, 0)),
            pl.BlockSpec((N_GATES, 1), lambda bi, ci: (0, 0)),
            pl.BlockSpec((1, MLSTM_WIDTH), lambda bi, ci: (0, 0)),
            pl.BlockSpec((2 * L, L), lambda bi, ci: (0, 0)),
        ],
        out_specs=pl.BlockSpec((L, MLSTM_WIDTH), lambda bi, ci: (cur(bi, ci), 0)),
        out_shape=jax.ShapeDtypeStruct((T, MLSTM_WIDTH), jnp.bfloat16),
        scratch_shapes=[
            pltpu.VMEM((HALO + L, 2 * MLSTM_WIDTH), jnp.float32),
            pltpu.VMEM((MLSTM_HEADS, HEAD_DIM, 2 * HEAD_DIM), jnp.float32),
            pltpu.VMEM((MLSTM_HEADS, 8, LANES), jnp.float32),
        ],
        compiler_params=pltpu.CompilerParams(
            dimension_semantics=("parallel", "arbitrary"), vmem_limit_bytes=VMEM_LIMIT),
        name="mlstm",
    )(p, p, p, p, gates_t, conv_w, gate_b, hn_g, tri)


def _out_route_kernel(seq, x_ref, ym_ref, u_ref, up_ref, pw_ref, ps_ref, wo_ref, g2_ref,
                      wrt_ref, br_ref, x1_ref, h2_ref, idx_ref, gate_ref, rank_ref, cnt_ref,
                      ubuf_ref, carry_ref):
    TM = TM_PROJ
    i = pl.program_id(0)

    @pl.when(i == 0)
    def _():
        carry_ref[...] = jnp.zeros_like(carry_ref)

    pos0 = (i * TM) % seq
    ubuf_ref[0:HALO, :] = jnp.where(pos0 > 0, up_ref[...].astype(jnp.float32), 0.0)
    ubuf_ref[HALO:HALO + TM, :] = u_ref[...].astype(jnp.float32)
    pos = (pos0 + lax.broadcasted_iota(jnp.int32, (TM, 1), 0) + 1).astype(jnp.float32)
    mixed = []
    for gi, w in enumerate(POOL_WINDOWS):
        lo = gi * POOL_GROUP_DIM
        tok = ubuf_ref[HALO:HALO + TM, lo:lo + POOL_GROUP_DIM]
        acc = tok
        for j in range(1, w):
            acc = acc + ubuf_ref[HALO - j:HALO - j + TM, lo:lo + POOL_GROUP_DIM]
        pooled = acc / jnp.minimum(pos, float(w)) - tok
        mg = jnp.dot(pooled.astype(jnp.bfloat16), pw_ref[gi], preferred_element_type=jnp.float32)
        mixed.append((mg * ps_ref[:, lo:lo + POOL_GROUP_DIM]).astype(jnp.bfloat16))
    y_cat = jnp.concatenate([ym_ref[...]] + mixed, axis=1)

    x1 = x_ref[...] + jnp.dot(y_cat, wo_ref[...], preferred_element_type=jnp.float32)
    x1_ref[...] = x1
    h2 = x1 * lax.rsqrt(jnp.mean(x1 * x1, axis=-1, keepdims=True) + EPS) * g2_ref[...]
    h2b = h2.astype(jnp.bfloat16)
    for s in range(SLAB):
        h2_ref[pl.ds(s, TM, stride=SLAB), :] = h2[:, s * LANES:(s + 1) * LANES]

    logits = lax.dot_general(wrt_ref[...], h2b, NT_DIMS,
                             preferred_element_type=jnp.float32) + br_ref[...]
    e_id = lax.broadcasted_iota(jnp.int32, (N_EXPERTS, TM), 0).astype(jnp.float32)
    work = logits
    vals, ids, hots = [], [], []
    for _ in range(TOP_K):
        mk = jnp.max(work, axis=0, keepdims=True)
        ik = jnp.min(jnp.where(work == mk, e_id, float(N_EXPERTS)), axis=0, keepdims=True)
        hot = e_id == ik
        work = jnp.where(hot, -jnp.inf, work)
        vals.append(mk)
        ids.append(ik)
        hots.append(hot)
    ex = [jnp.exp(vk - vals[0]) for vk in vals]
    denom = ex[0] + ex[1] + ex[2] + ex[3]
    gate_ref[...] = jnp.concatenate([e / denom for e in ex], axis=0)
    idx_ref[...] = jnp.concatenate(ids, axis=0).astype(jnp.int32)

    sel_f = sum(jnp.where(hot, 1.0, 0.0) for hot in hots)
    t_row = lax.broadcasted_iota(jnp.int32, (TM, TM), 0)
    t_col = lax.broadcasted_iota(jnp.int32, (TM, TM), 1)
    before = jnp.where(t_row < t_col, 1.0, 0.0).astype(jnp.bfloat16)
    prefix = jnp.dot(sel_f.astype(jnp.bfloat16), before, preferred_element_type=jnp.float32)
    carry = carry_ref[...]
    rank_e = carry[:, 0:1] + prefix
    ranks = [jnp.sum(jnp.where(hot, rank_e, 0.0), axis=0, keepdims=True) for hot in hots]
    rank_ref[...] = jnp.concatenate(ranks, axis=0).astype(jnp.int32)
    carry_new = carry + jnp.sum(sel_f, axis=1, keepdims=True)
    carry_ref[...] = carry_new
    cnt_ref[...] = carry_new.astype(jnp.int32)


def _out_route(x2, ym, p, pool_w, pool_s, w_out, g2, wr_t, br, seq):
    T = x2.shape[0]
    TM = TM_PROJ
    nt = T // TM
    u_blk = N_MAIN // POOL_WIDTH - 1
    halo_per_tile = TM // HALO
    tok_spec = pl.BlockSpec((TOP_K, TM), lambda i: (0, i))
    return pl.pallas_call(
        functools.partial(_out_route_kernel, seq),
        grid=(nt,),
        in_specs=[
            pl.BlockSpec((TM, D_MODEL), lambda i: (i, 0)),
            pl.BlockSpec((TM, MLSTM_WIDTH), lambda i: (i, 0)),
            pl.BlockSpec((TM, POOL_WIDTH), lambda i: (i, u_blk)),
            pl.BlockSpec((HALO, POOL_WIDTH),
                         lambda i: (jnp.maximum(i * halo_per_tile - 1, 0), u_blk)),
            pl.BlockSpec((len(POOL_WINDOWS), POOL_GROUP_DIM, POOL_GROUP_DIM), lambda i: (0, 0, 0)),
            pl.BlockSpec((1, POOL_WIDTH), lambda i: (0, 0)),
            pl.BlockSpec((D_MODEL, D_MODEL), lambda i: (0, 0)),
            pl.BlockSpec((1, D_MODEL), lambda i: (0, 0)),
            pl.BlockSpec((N_EXPERTS, D_MODEL), lambda i: (0, 0)),
            pl.BlockSpec((N_EXPERTS, 1), lambda i: (0, 0)),
        ],
        out_specs=[
            pl.BlockSpec((TM, D_MODEL), lambda i: (i, 0)),
            pl.BlockSpec((TM * SLAB, LANES), lambda i: (i, 0)),
            tok_spec, tok_spec, tok_spec,
            pl.BlockSpec((N_EXPERTS, LANES), lambda i: (0, 0)),
        ],
        out_shape=[
            jax.ShapeDtypeStruct((T, D_MODEL), jnp.float32),
            jax.ShapeDtypeStruct((T * SLAB, LANES), jnp.float32),
            jax.ShapeDtypeStruct((TOP_K, T), jnp.int32),
            jax.ShapeDtypeStruct((TOP_K, T), jnp.float32),
            jax.ShapeDtypeStruct((TOP_K, T), jnp.int32),
            jax.ShapeDtypeStruct((N_EXPERTS, LANES), jnp.int32),
        ],
        scratch_shapes=[
            pltpu.VMEM((HALO + TM, POOL_WIDTH), jnp.float32),
            pltpu.VMEM((N_EXPERTS, LANES), jnp.float32),
        ],
        compiler_params=pltpu.CompilerParams(
            dimension_semantics=("arbitrary",), vmem_limit_bytes=VMEM_LIMIT),
        name="out_route",
    )(x2, ym, p, p, pool_w, pool_s, w_out, g2, wr_t, br)


def _plan_kernel(n_assign, dest_ref, fill_ref, slot_ref, sem):
    cp = pltpu.make_async_copy(fill_ref, slot_ref, sem)
    cp.start()
    cp.wait()

    def body(i, carry):
        base = i * PLAN_UNROLL
        for j in range(PLAN_UNROLL):
            slot_ref[dest_ref[base + j]] = base + j
        return carry

    lax.fori_loop(0, n_assign // PLAN_UNROLL, body, 0)


def _plan(dest_flat, fill):
    n_assign = dest_flat.shape[0]
    return pl.pallas_call(
        functools.partial(_plan_kernel, n_assign),
        in_specs=[
            pl.BlockSpec(memory_space=pltpu.SMEM),
            pl.BlockSpec(memory_space=pl.ANY),
        ],
        out_specs=pl.BlockSpec(memory_space=pltpu.SMEM),
        out_shape=jax.ShapeDtypeStruct(fill.shape, jnp.int32),
        scratch_shapes=[pltpu.SemaphoreType.DMA(())],
        name="plan",
    )(dest_flat, fill)


def _expert_kernel(n_tok, bs_ref, slot_ref, h2_ref, wg_ref, bg_ref, wu_ref, bu_ref, wd_ref, bd_ref,
                   yt_ref, xg0_ref, xg1_ref, xg2_ref, ys0_ref, ys1_ref, ys2_ref,
                   wgb_ref, wub_ref, wdb_ref, gsem, ssem):
    TM = TM_EXPERT
    ROWS = TM * SLAB
    e = pl.program_id(0)
    n_total = bs_ref[N_EXPERTS]
    xg = (xg0_ref, xg1_ref, xg2_ref)
    ys = (ys0_ref, ys1_ref, ys2_ref)

    def token_of(a):
        return a & (n_tok - 1) if n_tok & (n_tok - 1) == 0 else lax.rem(a, n_tok)

    def start_gather(blk, par):
        base = (blk + 1) * TM
        for r in range(TM):
            t = token_of(slot_ref[base + r])
            pltpu.make_async_copy(h2_ref.at[pl.ds(pl.multiple_of(t * SLAB, SLAB), SLAB), :],
                                  xg[par].at[pl.ds(r * SLAB, SLAB), :], gsem.at[par]).start()

    def wait_gather(par):
        pltpu.make_async_copy(h2_ref.at[pl.ds(0, ROWS), :], xg[0], gsem.at[par]).wait()

    def start_scatter(blk, par):
        base = (blk + 1) * TM
        for r in range(TM):
            a = slot_ref[base + r]
            pltpu.make_async_copy(ys[par].at[pl.ds(r * SLAB, SLAB), :],
                                  yt_ref.at[pl.ds(pl.multiple_of(a * SLAB, SLAB), SLAB), :],
                                  ssem.at[par]).start()

    def wait_scatter(par):
        pltpu.make_async_copy(ys[0], yt_ref.at[pl.ds(0, ROWS), :], ssem.at[par]).wait()

    @pl.when(e == 0)
    def _():
        start_gather(0, 0)
        start_gather(1, 1)
        for par in range(NBUF):
            ys[par][...] = jnp.zeros_like(ys[par])
            dump = yt_ref.at[pl.ds((n_tok * TOP_K + par * TM) * SLAB, ROWS), :]
            cp = pltpu.make_async_copy(ys[par], dump, ssem.at[par])
            cp.start()
            cp.wait()

    wgb_ref[...] = wg_ref[0].astype(jnp.bfloat16)
    wub_ref[...] = wu_ref[0].astype(jnp.bfloat16)
    wdb_ref[...] = wd_ref[0].astype(jnp.bfloat16)

    def block_step(g, par):
        nxt2 = (par + 2) % NBUF
        wait_gather(par)

        @pl.when(g >= 2)
        def _():
            wait_scatter(par)

        start_gather(g + 2, nxt2)
        start_scatter(g - 1, nxt2)
        x = jnp.concatenate(
            [xg[par][pl.ds(s, TM, stride=SLAB), :].astype(jnp.bfloat16) for s in range(SLAB)],
            axis=1)
        gate = jnp.dot(x, wgb_ref[...], preferred_element_type=jnp.float32) + bg_ref[0]
        up = jnp.dot(x, wub_ref[...], preferred_element_type=jnp.float32) + bu_ref[0]
        gate = jnp.minimum(gate, SWIGLU_LIMIT)
        up = jnp.clip(up, -SWIGLU_LIMIT, SWIGLU_LIMIT)
        glu = gate * _sigmoid(SWIGLU_ALPHA * gate)
        act = (glu * (up + 1.0)).astype(jnp.bfloat16)
        y = jnp.dot(act, wdb_ref[...], preferred_element_type=jnp.float32) + bd_ref[0]
        for s in range(SLAB):
            ys[par][pl.ds(s, TM, stride=SLAB), :] = y[:, s * LANES:(s + 1) * LANES]

    def body(g, carry):
        for par in range(NBUF):
            pl.when(g % NBUF == par)(functools.partial(block_step, g, par))
        return carry

    lax.fori_loop(bs_ref[e], bs_ref[e + 1], body, 0)

    @pl.when(e == N_EXPERTS - 1)
    def _():
        g = n_total
        for par in range(NBUF):
            @pl.when((g - 1) % NBUF == par)
            def _():
                start_scatter(g - 1, par)
        wait_gather(g % NBUF)
        wait_gather((g + 1) % NBUF)
        wait_scatter((g - 1) % NBUF)
        wait_scatter((g + 1) % NBUF)

        @pl.when(g >= 2)
        def _():
            wait_scatter(g % NBUF)


def _experts(block_start, slot_buf, h2_slab, w_gate, b_gate, w_up, b_up, w_down, b_down, n_tok):
    TM = TM_EXPERT
    n_assign = n_tok * TOP_K
    w_spec = pl.BlockSpec((1, D_MODEL, D_FF), lambda e, bs, sl: (e, 0, 0))
    bias_spec = pl.BlockSpec((1, 1, D_FF), lambda e, bs, sl: (e, 0, 0))
    buf = pltpu.VMEM((TM * SLAB, LANES), jnp.float32)
    grid_spec = pltpu.PrefetchScalarGridSpec(
        num_scalar_prefetch=2,
        grid=(N_EXPERTS,),
        in_specs=[
            pl.BlockSpec(memory_space=pl.ANY),
            w_spec, bias_spec, w_spec, bias_spec, w_spec, bias_spec,
        ],
        out_specs=pl.BlockSpec(memory_space=pl.ANY),
        scratch_shapes=[
            buf, buf, buf, buf, buf, buf,
            pltpu.VMEM((D_MODEL, D_FF), jnp.bfloat16),
            pltpu.VMEM((D_MODEL, D_FF), jnp.bfloat16),
            pltpu.VMEM((D_FF, D_MODEL), jnp.bfloat16),
            pltpu.SemaphoreType.DMA((NBUF,)),
            pltpu.SemaphoreType.DMA((NBUF,)),
        ],
    )
    return pl.pallas_call(
        functools.partial(_expert_kernel, n_tok),
        grid_spec=grid_spec,
        out_shape=jax.ShapeDtypeStruct(((n_assign + NBUF * TM) * SLAB, LANES), jnp.float32),
        compiler_params=pltpu.CompilerParams(
            dimension_semantics=("arbitrary",), vmem_limit_bytes=VMEM_LIMIT),
        name="experts",
    )(block_start, slot_buf, h2_slab, w_gate, b_gate, w_up, b_up, w_down, b_down)


def _combine_kernel(normalize, x1_ref, y0_ref, y1_ref, y2_ref, y3_ref, gate_ref, g_ref, o_ref):
    TM = TM_PROJ
    gates = jnp.concatenate([gate_ref[...], jnp.zeros((8 - TOP_K, TM), jnp.float32)], axis=0)
    g_cols = jnp.transpose(gates)
    g_bc = [jnp.broadcast_to(g_cols[:, k:k + 1], (TM, LANES)) for k in range(TOP_K)]
    parts = []
    ssq = jnp.zeros((TM, LANES), jnp.float32)
    for s in range(SLAB):
        acc = x1_ref[:, s * LANES:(s + 1) * LANES]
        for k, y_ref in enumerate((y0_ref, y1_ref, y2_ref, y3_ref)):
            acc = acc + g_bc[k] * y_ref[pl.ds(s, TM, stride=SLAB), :]
        parts.append(acc)
        ssq = ssq + acc * acc
    if normalize:
        inv = lax.rsqrt(jnp.sum(ssq, axis=-1, keepdims=True) * (1.0 / D_MODEL) + EPS)
        for s in range(SLAB):
            o_ref[:, s * LANES:(s + 1) * LANES] = parts[s] * inv * g_ref[:, s * LANES:(s + 1) * LANES]
    else:
        for s in range(SLAB):
            o_ref[:, s * LANES:(s + 1) * LANES] = parts[s]


def _combine(x1, y_tok, gate_t, gf, normalize):
    T = x1.shape[0]
    TM = TM_PROJ
    nt = T // TM

    def y_spec(k):
        return pl.BlockSpec((TM * SLAB, LANES), lambda i: (k * nt + i, 0))

    return pl.pallas_call(
        functools.partial(_combine_kernel, normalize),
        grid=(nt,),
        in_specs=[
            pl.BlockSpec((TM, D_MODEL), lambda i: (i, 0)),
            y_spec(0), y_spec(1), y_spec(2), y_spec(3),
            pl.BlockSpec((TOP_K, TM), lambda i: (0, i)),
            pl.BlockSpec((1, D_MODEL), lambda i: (0, 0)),
        ],
        out_specs=pl.BlockSpec((TM, D_MODEL), lambda i: (i, 0)),
        out_shape=jax.ShapeDtypeStruct((T, D_MODEL), jnp.float32),
        compiler_params=pltpu.CompilerParams(
            dimension_semantics=("parallel",), vmem_limit_bytes=VMEM_LIMIT),
        name="combine",
    )(x1, y_tok, y_tok, y_tok, y_tok, gate_t, gf)


def kernel(x, norm1_g, w_in, ig_b, fg_b, conv_w, head_norm_g, pool_w, pool_scale, w_out, norm2_g,
           w_router, b_router, w_gate, b_gate, w_up, b_up, w_down, b_down, normf_g):
    B, S, D = x.shape
    T = B * S
    depth = norm1_g.shape[0]
    W = MLSTM_WIDTH
    f32, bf16 = jnp.float32, jnp.bfloat16

    L = CHUNK
    t_l = lax.broadcasted_iota(jnp.int32, (L, L), 0)
    t_r = lax.broadcasted_iota(jnp.int32, (L, L), 1)
    tri = jnp.concatenate([(t_r <= t_l).astype(f32), (t_r == t_l).astype(f32)], axis=0)

    n_assign = T * TOP_K
    n_blocks = -(-n_assign // TM_EXPERT) + N_EXPERTS
    n_rows = n_blocks * TM_EXPERT
    n_table = n_rows + 3 * TM_EXPERT
    fill = n_assign + ((jnp.arange(n_table, dtype=jnp.int32) + (NBUF - 1) * TM_EXPERT)
                       % (NBUF * TM_EXPERT))
    x2 = x.reshape(T, D)
    for l in range(depth):
        w = w_in[l]
        w_main = jnp.concatenate([w[:, :4 * W], w[:, 4 * W + N_GATES:]], axis=1).astype(bf16)
        wg_t = jnp.zeros((BF16_SUBLANES, D), bf16).at[:N_GATES].set(
            w[:, 4 * W:4 * W + N_GATES].T.astype(bf16))
        p, gates_t = _in_proj(x2, norm1_g[l][None, :], w_main, wg_t)

        gate_b = jnp.concatenate([ig_b[l], fg_b[l]])[:, None].astype(f32)
        ym = _mlstm(p, gates_t, conv_w[l].astype(f32), gate_b, head_norm_g[l][None, :], tri, B, S)

        x1, h2, idx_t, gate_t, rank_t, cnt = _out_route(
            x2, ym, p, pool_w[l].astype(bf16), pool_scale[l][None, :], w_out[l].astype(bf16),
            norm2_g[l][None, :], w_router[l].T.astype(bf16), b_router[l][:, None], S)

        counts = cnt[:, 0]
        padded = ((counts + TM_EXPERT - 1) // TM_EXPERT) * TM_EXPERT
        padded_end = jnp.cumsum(padded)
        padded_start = padded_end - padded
        expert_ids = jnp.arange(N_EXPERTS, dtype=jnp.int32)[:, None, None]
        start_of = jnp.sum(jnp.where(idx_t[None] == expert_ids, padded_start[:, None, None], 0), axis=0)
        dest = start_of + rank_t
        block_start = jnp.concatenate(
            [jnp.zeros((1,), jnp.int32), (padded_end // TM_EXPERT).astype(jnp.int32)])

        slot_buf = _plan(dest.reshape(-1) + TM_EXPERT, fill)
        y_tok = _experts(block_start, slot_buf, h2, w_gate[l], b_gate[l][:, None, :],
                         w_up[l], b_up[l][:, None, :], w_down[l], b_down[l][:, None, :], T)
        last = l + 1 == depth
        x2 = _combine(x1, y_tok, gate_t, normf_g[None, :], last)
    return x2.reshape(B, S, D)
```

```python
import functools

import jax
import jax.numpy as jnp
from jax import lax
from jax.experimental import pallas as pl
from jax.experimental.pallas import tpu as pltpu

D_MODEL = 1024
MLSTM_WIDTH = 512
MLSTM_HEADS = 4
HEAD_DIM = 128
CONV_WIDTH = 4
POOL_WIDTH = 512
POOL_WINDOWS = (2, 4, 8, 16)
POOL_GROUP_DIM = 128
N_EXPERTS = 32
TOP_K = 4
D_FF = 1024
SWIGLU_LIMIT = 7.0
SWIGLU_ALPHA = 1.702
EPS = 1e-5

N_MAIN = 4 * MLSTM_WIDTH + POOL_WIDTH
N_GATES = 2 * MLSTM_HEADS

LANES = 128
BF16_SUBLANES = 16
VMEM_LIMIT = 56 * 1024 * 1024

TM_PROJ = 512
CHUNK = 256
HALO = 16
TM_EXPERT = 256
NBUF = 3
SLAB = D_MODEL // LANES
PLAN_UNROLL = 16

NT_DIMS = (((1,), (1,)), ((), ()))
TN_DIMS = (((0,), (0,)), ((), ()))


def _sigmoid(x):
    return 1.0 / (1.0 + jnp.exp(-x))


def _in_proj_kernel(x_ref, g_ref, w_ref, wgt_ref, p_ref, gt_ref):
    x = x_ref[...]
    h = x * lax.rsqrt(jnp.mean(x * x, axis=-1, keepdims=True) + EPS) * g_ref[...]
    hb = h.astype(jnp.bfloat16)
    p_ref[...] = jnp.dot(hb, w_ref[...], preferred_element_type=jnp.float32).astype(p_ref.dtype)
    gt = lax.dot_general(wgt_ref[...], hb, NT_DIMS, preferred_element_type=jnp.float32)
    gt_ref[...] = gt[:N_GATES]


def _in_proj(x2, g1, w_main, wg_t):
    T = x2.shape[0]
    return pl.pallas_call(
        _in_proj_kernel,
        grid=(T // TM_PROJ,),
        in_specs=[
            pl.BlockSpec((TM_PROJ, D_MODEL), lambda i: (i, 0)),
            pl.BlockSpec((1, D_MODEL), lambda i: (0, 0)),
            pl.BlockSpec((D_MODEL, N_MAIN), lambda i: (0, 0)),
            pl.BlockSpec((BF16_SUBLANES, D_MODEL), lambda i: (0, 0)),
        ],
        out_specs=[
            pl.BlockSpec((TM_PROJ, N_MAIN), lambda i: (i, 0)),
            pl.BlockSpec((N_GATES, TM_PROJ), lambda i: (0, i)),
        ],
        out_shape=[
            jax.ShapeDtypeStruct((T, N_MAIN), jnp.bfloat16),
            jax.ShapeDtypeStruct((N_GATES, T), jnp.float32),
        ],
        compiler_params=pltpu.CompilerParams(
            dimension_semantics=("parallel",), vmem_limit_bytes=VMEM_LIMIT),
        name="in_proj",
    )(x2, g1, w_main, wg_t)


def _mlstm_kernel(qk_ref, qkp_ref, v_ref, o_ref, gt_ref, convw_ref, gb_ref, hng_ref,
                  tri_ref, shift_ref, hshift_ref, y_ref, cn_ref, m_ref):
    L = CHUNK
    c = pl.program_id(1)

    @pl.when(c == 0)
    def _():
        cn_ref[...] = jnp.zeros_like(cn_ref)
        m_ref[...] = jnp.zeros_like(m_ref)

    x_cur = qk_ref[...]
    x_prev = jnp.where(c > 0, qkp_ref[...], jnp.zeros_like(qkp_ref))
    acc = convw_ref[CONV_WIDTH - 1:CONV_WIDTH, :] * x_cur.astype(jnp.float32)
    for j in range(CONV_WIDTH - 1):
        sh = jnp.dot(shift_ref[j], x_cur, preferred_element_type=jnp.float32)
        top = sh[:8] + jnp.dot(hshift_ref[j], x_prev, preferred_element_type=jnp.float32)
        sh = jnp.concatenate([top, sh[8:]], axis=0)
        acc = acc + convw_ref[j:j + 1, :] * sh
    qk = acc * _sigmoid(acc)
    q_all = qk[:, :MLSTM_WIDTH].astype(jnp.bfloat16)
    k_t = jnp.transpose(qk[:, MLSTM_WIDTH:] * (HEAD_DIM ** -0.5))

    gt = gt_ref[...] + gb_ref[...]
    f = gt[MLSTM_HEADS:]
    lf = jnp.minimum(f, 0.0) - jnp.log(1.0 + jnp.exp(-jnp.abs(f)))
    ig = gt[:MLSTM_HEADS]
    b_rows = lax.dot_general(lf, tri_ref[...], NT_DIMS, precision=lax.Precision.HIGHEST,
                             preferred_element_type=jnp.float32)
    c_rows = ig - b_rows
    lane = lax.broadcasted_iota(jnp.int32, (MLSTM_HEADS, L), 1)
    cm_rows = c_rows
    d = 1
    while d < L:
        cm_rows = jnp.maximum(cm_rows, jnp.where(lane >= d, pltpu.roll(cm_rows, d, axis=1), -jnp.inf))
        d *= 2
    cols = jnp.transpose(jnp.concatenate([b_rows, cm_rows], axis=0))

    row_id = lax.broadcasted_iota(jnp.int32, (L, L), 0)
    col_id = lax.broadcasted_iota(jnp.int32, (L, L), 1)
    causal = col_id <= row_id
    ones_col = (lax.broadcasted_iota(jnp.int32, (L, HEAD_DIM), 1) == 0).astype(jnp.bfloat16)

    for h in range(MLSTM_HEADS):
        lo = h * HEAD_DIM
        q = q_all[:, lo:lo + HEAD_DIM]
        kt = k_t[lo:lo + HEAD_DIM, :]
        v_ext = jnp.concatenate([v_ref[:, lo:lo + HEAD_DIM], ones_col], axis=1)
        b_col = cols[:, h:h + 1]
        cm_col = cols[:, MLSTM_HEADS + h:MLSTM_HEADS + h + 1]
        c_row = c_rows[h:h + 1, :]
        b_tot = b_rows[h:h + 1, L - 1:L]
        cm_tot = cm_rows[h:h + 1, L - 1:L]
        m_in = m_ref[h][0:1, 0:1]
        cn = cn_ref[h]

        mx_col = jnp.maximum(cm_col, m_in)
        m_out = b_col + mx_col
        s_qk = jnp.dot(q, kt.astype(jnp.bfloat16), preferred_element_type=jnp.float32)
        s = (s_qk * jnp.exp(jnp.where(causal, c_row - mx_col, -jnp.inf))).astype(jnp.bfloat16)
        inter = jnp.exp(m_in - mx_col)
        num = (jnp.dot(s, v_ext, preferred_element_type=jnp.float32)
               + inter * jnp.dot(q, cn.astype(jnp.bfloat16), preferred_element_type=jnp.float32))
        den = num[:, HEAD_DIM:HEAD_DIM + 1]
        hh = num[:, :HEAD_DIM] / jnp.maximum(jnp.abs(den), jnp.exp(-m_out))

        mu = jnp.mean(hh, axis=-1, keepdims=True)
        dv = hh - mu
        var = jnp.mean(dv * dv, axis=-1, keepdims=True)
        hn = dv * lax.rsqrt(var + EPS) * hng_ref[:, lo:lo + HEAD_DIM]
        og = _sigmoid(o_ref[:, lo:lo + HEAD_DIM].astype(jnp.float32))
        y_ref[:, lo:lo + HEAD_DIM] = (og * hn).astype(y_ref.dtype)

        m_loc = b_tot + cm_tot
        kw_t = (kt * jnp.exp(c_row - cm_tot)).astype(jnp.bfloat16)
        c_loc = jnp.dot(kw_t, v_ext, preferred_element_type=jnp.float32)
        m_new = jnp.maximum(b_tot + m_in, m_loc)
        s_old = jnp.exp(b_tot + m_in - m_new)
        s_loc = jnp.exp(m_loc - m_new)
        cn_ref[h] = s_old * cn + s_loc * c_loc
        m_ref[h] = jnp.broadcast_to(m_new, m_ref.shape[1:])


def _mlstm(p, gates_t, conv_w, gate_b, hn_g, tri, shifts, halo_shifts, batch, seq):
    T = batch * seq
    L = CHUNK
    nc = seq // L
    halo_per_chunk = L // HALO

    def cur(bi, ci):
        return bi * nc + ci

    return pl.pallas_call(
        _mlstm_kernel,
        grid=(batch, nc),
        in_specs=[
            pl.BlockSpec((L, 2 * MLSTM_WIDTH), lambda bi, ci: (cur(bi, ci), 0)),
            pl.BlockSpec((HALO, 2 * MLSTM_WIDTH),
                         lambda bi, ci: (jnp.maximum(cur(bi, ci) * halo_per_chunk - 1, 0), 0)),
            pl.BlockSpec((L, MLSTM_WIDTH), lambda bi, ci: (cur(bi, ci), 2)),
            pl.BlockSpec((L, MLSTM_WIDTH), lambda bi, ci: (cur(bi, ci), 3)),
            pl.BlockSpec((N_GATES, L), lambda bi, ci: (0, cur(bi, ci))),
            pl.BlockSpec((CONV_WIDTH, 2 * MLSTM_WIDTH), lambda bi, ci: (0, 0)),
            pl.BlockSpec((N_GATES, 1), lambda bi, ci: (0, 0)),
            pl.BlockSpec((1, MLSTM_WIDTH), lambda bi, ci: (0, 0)),
            pl.BlockSpec((L, L), lambda bi, ci: (0, 0)),
            pl.BlockSpec((CONV_WIDTH - 1, L, L), lambda bi, ci: (0, 0, 0)),
            pl.BlockSpec((CONV_WIDTH - 1, 8, HALO), lambda bi, ci: (0, 0, 0)),
        ],
        out_specs=pl.BlockSpec((L, MLSTM_WIDTH), lambda bi, ci: (cur(bi, ci), 0)),
        out_shape=jax.ShapeDtypeStruct((T, MLSTM_WIDTH), jnp.bfloat16),
        scratch_shapes=[
            pltpu.VMEM((MLSTM_HEADS, HEAD_DIM, 2 * HEAD_DIM), jnp.float32),
            pltpu.VMEM((MLSTM_HEADS, 8, LANES), jnp.float32),
        ],
        compiler_params=pltpu.CompilerParams(
            dimension_semantics=("parallel", "arbitrary"), vmem_limit_bytes=VMEM_LIMIT),
        name="mlstm",
    )(p, p, p, p, gates_t, conv_w, gate_b, hn_g, tri, shifts, halo_shifts)


def _out_route_kernel(seq, x_ref, ym_ref, u_ref, up_ref, pw_ref, ps_ref, wo_ref, g2_ref,
                      wrt_ref, br_ref, x1_ref, h2_ref, idx_ref, gate_ref, rank_ref, cnt_ref,
                      ubuf_ref, carry_ref):
    TM = TM_PROJ
    i = pl.program_id(0)

    @pl.when(i == 0)
    def _():
        carry_ref[...] = jnp.zeros_like(carry_ref)

    pos0 = (i * TM) % seq
    ubuf_ref[0:HALO, :] = jnp.where(pos0 > 0, up_ref[...].astype(jnp.float32), 0.0)
    ubuf_ref[HALO:HALO + TM, :] = u_ref[...].astype(jnp.float32)
    pos = (pos0 + lax.broadcasted_iota(jnp.int32, (TM, 1), 0) + 1).astype(jnp.float32)
    mixed = []
    for gi, w in enumerate(POOL_WINDOWS):
        lo = gi * POOL_GROUP_DIM
        tok = ubuf_ref[HALO:HALO + TM, lo:lo + POOL_GROUP_DIM]
        acc = tok
        for j in range(1, w):
            acc = acc + ubuf_ref[HALO - j:HALO - j + TM, lo:lo + POOL_GROUP_DIM]
        pooled = acc / jnp.minimum(pos, float(w)) - tok
        mg = jnp.dot(pooled.astype(jnp.bfloat16), pw_ref[gi], preferred_element_type=jnp.float32)
        mixed.append((mg * ps_ref[:, lo:lo + POOL_GROUP_DIM]).astype(jnp.bfloat16))
    y_cat = jnp.concatenate([ym_ref[...]] + mixed, axis=1)

    x1 = x_ref[...] + jnp.dot(y_cat, wo_ref[...], preferred_element_type=jnp.float32)
    x1_ref[...] = x1
    h2 = x1 * lax.rsqrt(jnp.mean(x1 * x1, axis=-1, keepdims=True) + EPS) * g2_ref[...]
    h2b = h2.astype(jnp.bfloat16)
    for s in range(SLAB):
        h2_ref[pl.ds(s, TM, stride=SLAB), :] = h2[:, s * LANES:(s + 1) * LANES]

    logits = lax.dot_general(wrt_ref[...], h2b, NT_DIMS,
                             preferred_element_type=jnp.float32) + br_ref[...]
    e_id = lax.broadcasted_iota(jnp.int32, (N_EXPERTS, TM), 0).astype(jnp.float32)
    work = logits
    vals, ids, hots = [], [], []
    for _ in range(TOP_K):
        mk = jnp.max(work, axis=0, keepdims=True)
        ik = jnp.min(jnp.where(work == mk, e_id, float(N_EXPERTS)), axis=0, keepdims=True)
        hot = e_id == ik
        work = jnp.where(hot, -jnp.inf, work)
        vals.append(mk)
        ids.append(ik)
        hots.append(hot)
    ex = [jnp.exp(vk - vals[0]) for vk in vals]
    denom = ex[0] + ex[1] + ex[2] + ex[3]
    gate_ref[...] = jnp.concatenate([e / denom for e in ex], axis=0)
    idx_ref[...] = jnp.concatenate(ids, axis=0).astype(jnp.int32)

    sel_f = sum(jnp.where(hot, 1.0, 0.0) for hot in hots)
    t_row = lax.broadcasted_iota(jnp.int32, (TM, TM), 0)
    t_col = lax.broadcasted_iota(jnp.int32, (TM, TM), 1)
    before = jnp.where(t_row < t_col, 1.0, 0.0).astype(jnp.bfloat16)
    prefix = jnp.dot(sel_f.astype(jnp.bfloat16), before, preferred_element_type=jnp.float32)
    carry = carry_ref[...]
    rank_e = carry[:, 0:1] + prefix
    ranks = [jnp.sum(jnp.where(hot, rank_e, 0.0), axis=0, keepdims=True) for hot in hots]
    rank_ref[...] = jnp.concatenate(ranks, axis=0).astype(jnp.int32)
    carry_new = carry + jnp.sum(sel_f, axis=1, keepdims=True)
    carry_ref[...] = carry_new
    cnt_ref[...] = carry_new.astype(jnp.int32)


def _out_route(x2, ym, p, pool_w, pool_s, w_out, g2, wr_t, br, seq):
    T = x2.shape[0]
    TM = TM_PROJ
    nt = T // TM
    u_blk = N_MAIN // POOL_WIDTH - 1
    halo_per_tile = TM // HALO
    tok_spec = pl.BlockSpec((TOP_K, TM), lambda i: (0, i))
    return pl.pallas_call(
        functools.partial(_out_route_kernel, seq),
        grid=(nt,),
        in_specs=[
            pl.BlockSpec((TM, D_MODEL), lambda i: (i, 0)),
            pl.BlockSpec((TM, MLSTM_WIDTH), lambda i: (i, 0)),
            pl.BlockSpec((TM, POOL_WIDTH), lambda i: (i, u_blk)),
            pl.BlockSpec((HALO, POOL_WIDTH),
                         lambda i: (jnp.maximum(i * halo_per_tile - 1, 0), u_blk)),
            pl.BlockSpec((len(POOL_WINDOWS), POOL_GROUP_DIM, POOL_GROUP_DIM), lambda i: (0, 0, 0)),
            pl.BlockSpec((1, POOL_WIDTH), lambda i: (0, 0)),
            pl.BlockSpec((D_MODEL, D_MODEL), lambda i: (0, 0)),
            pl.BlockSpec((1, D_MODEL), lambda i: (0, 0)),
            pl.BlockSpec((N_EXPERTS, D_MODEL), lambda i: (0, 0)),
            pl.BlockSpec((N_EXPERTS, 1), lambda i: (0, 0)),
        ],
        out_specs=[
            pl.BlockSpec((TM, D_MODEL), lambda i: (i, 0)),
            pl.BlockSpec((TM * SLAB, LANES), lambda i: (i, 0)),
            tok_spec, tok_spec, tok_spec,
            pl.BlockSpec((N_EXPERTS, LANES), lambda i: (0, 0)),
        ],
        out_shape=[
            jax.ShapeDtypeStruct((T, D_MODEL), jnp.float32),
            jax.ShapeDtypeStruct((T * SLAB, LANES), jnp.float32),
            jax.ShapeDtypeStruct((TOP_K, T), jnp.int32),
            jax.ShapeDtypeStruct((TOP_K, T), jnp.float32),
            jax.ShapeDtypeStruct((TOP_K, T), jnp.int32),
            jax.ShapeDtypeStruct((N_EXPERTS, LANES), jnp.int32),
        ],
        scratch_shapes=[
            pltpu.VMEM((HALO + TM, POOL_WIDTH), jnp.float32),
            pltpu.VMEM((N_EXPERTS, LANES), jnp.float32),
        ],
        compiler_params=pltpu.CompilerParams(
            dimension_semantics=("arbitrary",), vmem_limit_bytes=VMEM_LIMIT),
        name="out_route",
    )(x2, ym, p, p, pool_w, pool_s, w_out, g2, wr_t, br)


def _plan_kernel(n_assign, dest_ref, fill_ref, slot_ref, sem):
    cp = pltpu.make_async_copy(fill_ref, slot_ref, sem)
    cp.start()
    cp.wait()

    def body(i, carry):
        base = i * PLAN_UNROLL
        for j in range(PLAN_UNROLL):
            slot_ref[dest_ref[base + j]] = base + j
        return carry

    lax.fori_loop(0, n_assign // PLAN_UNROLL, body, 0)


def _plan(dest_flat, fill):
    n_assign = dest_flat.shape[0]
    return pl.pallas_call(
        functools.partial(_plan_kernel, n_assign),
        in_specs=[
            pl.BlockSpec(memory_space=pltpu.SMEM),
            pl.BlockSpec(memory_space=pl.ANY),
        ],
        out_specs=pl.BlockSpec(memory_space=pltpu.SMEM),
        out_shape=jax.ShapeDtypeStruct(fill.shape, jnp.int32),
        scratch_shapes=[pltpu.SemaphoreType.DMA(())],
        name="plan",
    )(dest_flat, fill)


def _expert_kernel(n_tok, bs_ref, slot_ref, h2_ref, wg_ref, bg_ref, wu_ref, bu_ref, wd_ref, bd_ref,
                   yt_ref, xg0_ref, xg1_ref, xg2_ref, ys0_ref, ys1_ref, ys2_ref,
                   wgb_ref, wub_ref, wdb_ref, gsem, ssem):
    TM = TM_EXPERT
    ROWS = TM * SLAB
    e = pl.program_id(0)
    n_total = bs_ref[N_EXPERTS]
    xg = (xg0_ref, xg1_ref, xg2_ref)
    ys = (ys0_ref, ys1_ref, ys2_ref)

    def token_of(a):
        return a & (n_tok - 1) if n_tok & (n_tok - 1) == 0 else lax.rem(a, n_tok)

    def start_gather(blk, par):
        base = (blk + 1) * TM
        for r in range(TM):
            t = token_of(slot_ref[base + r])
            pltpu.make_async_copy(h2_ref.at[pl.ds(pl.multiple_of(t * SLAB, SLAB), SLAB), :],
                                  xg[par].at[pl.ds(r * SLAB, SLAB), :], gsem.at[par]).start()

    def wait_gather(par):
        pltpu.make_async_copy(h2_ref.at[pl.ds(0, ROWS), :], xg[0], gsem.at[par]).wait()

    def start_scatter(blk, par):
        base = (blk + 1) * TM
        for r in range(TM):
            a = slot_ref[base + r]
            pltpu.make_async_copy(ys[par].at[pl.ds(r * SLAB, SLAB), :],
                                  yt_ref.at[pl.ds(pl.multiple_of(a * SLAB, SLAB), SLAB), :],
                                  ssem.at[par]).start()

    def wait_scatter(par):
        pltpu.make_async_copy(ys[0], yt_ref.at[pl.ds(0, ROWS), :], ssem.at[par]).wait()

    @pl.when(e == 0)
    def _():
        start_gather(0, 0)
        start_gather(1, 1)
        for par in range(NBUF):
            ys[par][...] = jnp.zeros_like(ys[par])
            dump = yt_ref.at[pl.ds((n_tok * TOP_K + par * TM) * SLAB, ROWS), :]
            cp = pltpu.make_async_copy(ys[par], dump, ssem.at[par])
            cp.start()
            cp.wait()

    wgb_ref[...] = wg_ref[0].astype(jnp.bfloat16)
    wub_ref[...] = wu_ref[0].astype(jnp.bfloat16)
    wdb_ref[...] = wd_ref[0].astype(jnp.bfloat16)

    def block_step(g, par):
        nxt2 = (par + 2) % NBUF
        wait_gather(par)

        @pl.when(g >= 2)
        def _():
            wait_scatter(par)

        start_gather(g + 2, nxt2)
        start_scatter(g - 1, nxt2)
        x = jnp.concatenate(
            [xg[par][pl.ds(s, TM, stride=SLAB), :].astype(jnp.bfloat16) for s in range(SLAB)],
            axis=1)
        gate = jnp.dot(x, wgb_ref[...], preferred_element_type=jnp.float32) + bg_ref[0]
        up = jnp.dot(x, wub_ref[...], preferred_element_type=jnp.float32) + bu_ref[0]
        gate = jnp.minimum(gate, SWIGLU_LIMIT)
        up = jnp.clip(up, -SWIGLU_LIMIT, SWIGLU_LIMIT)
        glu = gate * _sigmoid(SWIGLU_ALPHA * gate)
        act = (glu * (up + 1.0)).astype(jnp.bfloat16)
        y = jnp.dot(act, wdb_ref[...], preferred_element_type=jnp.float32) + bd_ref[0]
        for s in range(SLAB):
            ys[par][pl.ds(s, TM, stride=SLAB), :] = y[:, s * LANES:(s + 1) * LANES]

    def body(g, carry):
        for par in range(NBUF):
            pl.when(g % NBUF == par)(functools.partial(block_step, g, par))
        return carry

    lax.fori_loop(bs_ref[e], bs_ref[e + 1], body, 0)

    @pl.when(e == N_EXPERTS - 1)
    def _():
        g = n_total
        for par in range(NBUF):
            @pl.when((g - 1) % NBUF == par)
            def _():
                start_scatter(g - 1, par)
        wait_gather(g % NBUF)
        wait_gather((g + 1) % NBUF)
        wait_scatter((g - 1) % NBUF)
        wait_scatter((g + 1) % NBUF)

        @pl.when(g >= 2)
        def _():
            wait_scatter(g % NBUF)


def _experts(block_start, slot_buf, h2_slab, w_gate, b_gate, w_up, b_up, w_down, b_down, n_tok):
    TM = TM_EXPERT
    n_assign = n_tok * TOP_K
    w_spec = pl.BlockSpec((1, D_MODEL, D_FF), lambda e, bs, sl: (e, 0, 0))
    bias_spec = pl.BlockSpec((1, 1, D_FF), lambda e, bs, sl: (e, 0, 0))
    buf = pltpu.VMEM((TM * SLAB, LANES), jnp.float32)
    grid_spec = pltpu.PrefetchScalarGridSpec(
        num_scalar_prefetch=2,
        grid=(N_EXPERTS,),
        in_specs=[
            pl.BlockSpec(memory_space=pl.ANY),
            w_spec, bias_spec, w_spec, bias_spec, w_spec, bias_spec,
        ],
        out_specs=pl.BlockSpec(memory_space=pl.ANY),
        scratch_shapes=[
            buf, buf, buf, buf, buf, buf,
            pltpu.VMEM((D_MODEL, D_FF), jnp.bfloat16),
            pltpu.VMEM((D_MODEL, D_FF), jnp.bfloat16),
            pltpu.VMEM((D_FF, D_MODEL), jnp.bfloat16),
            pltpu.SemaphoreType.DMA((NBUF,)),
            pltpu.SemaphoreType.DMA((NBUF,)),
        ],
    )
    return pl.pallas_call(
        functools.partial(_expert_kernel, n_tok),
        grid_spec=grid_spec,
        out_shape=jax.ShapeDtypeStruct(((n_assign + NBUF * TM) * SLAB, LANES), jnp.float32),
        compiler_params=pltpu.CompilerParams(
            dimension_semantics=("arbitrary",), vmem_limit_bytes=VMEM_LIMIT),
        name="experts",
    )(block_start, slot_buf, h2_slab, w_gate, b_gate, w_up, b_up, w_down, b_down)


def _combine_kernel(normalize, x1_ref, y0_ref, y1_ref, y2_ref, y3_ref, gate_ref, g_ref, o_ref):
    TM = TM_PROJ
    gates = jnp.concatenate([gate_ref[...], jnp.zeros((8 - TOP_K, TM), jnp.float32)], axis=0)
    g_cols = jnp.transpose(gates)
    g_bc = [jnp.broadcast_to(g_cols[:, k:k + 1], (TM, LANES)) for k in range(TOP_K)]
    parts = []
    ssq = jnp.zeros((TM, LANES), jnp.float32)
    for s in range(SLAB):
        acc = x1_ref[:, s * LANES:(s + 1) * LANES]
        for k, y_ref in enumerate((y0_ref, y1_ref, y2_ref, y3_ref)):
            acc = acc + g_bc[k] * y_ref[pl.ds(s, TM, stride=SLAB), :]
        parts.append(acc)
        ssq = ssq + acc * acc
    if normalize:
        inv = lax.rsqrt(jnp.sum(ssq, axis=-1, keepdims=True) * (1.0 / D_MODEL) + EPS)
        for s in range(SLAB):
            o_ref[:, s * LANES:(s + 1) * LANES] = parts[s] * inv * g_ref[:, s * LANES:(s + 1) * LANES]
    else:
        for s in range(SLAB):
            o_ref[:, s * LANES:(s + 1) * LANES] = parts[s]


def _combine(x1, y_tok, gate_t, gf, normalize):
    T = x1.shape[0]
    TM = TM_PROJ
    nt = T // TM

    def y_spec(k):
        return pl.BlockSpec((TM * SLAB, LANES), lambda i: (k * nt + i, 0))

    return pl.pallas_call(
        functools.partial(_combine_kernel, normalize),
        grid=(nt,),
        in_specs=[
            pl.BlockSpec((TM, D_MODEL), lambda i: (i, 0)),
            y_spec(0), y_spec(1), y_spec(2), y_spec(3),
            pl.BlockSpec((TOP_K, TM), lambda i: (0, i)),
            pl.BlockSpec((1, D_MODEL), lambda i: (0, 0)),
        ],
        out_specs=pl.BlockSpec((TM, D_MODEL), lambda i: (i, 0)),
        out_shape=jax.ShapeDtypeStruct((T, D_MODEL), jnp.float32),
        compiler_params=pltpu.CompilerParams(
            dimension_semantics=("parallel",), vmem_limit_bytes=VMEM_LIMIT),
        name="combine",
    )(x1, y_tok, y_tok, y_tok, y_tok, gate_t, gf)


def kernel(x, norm1_g, w_in, ig_b, fg_b, conv_w, head_norm_g, pool_w, pool_scale, w_out, norm2_g,
           w_router, b_router, w_gate, b_gate, w_up, b_up, w_down, b_down, normf_g):
    B, S, D = x.shape
    T = B * S
    depth = norm1_g.shape[0]
    W = MLSTM_WIDTH
    f32, bf16 = jnp.float32, jnp.bfloat16

    L = CHUNK
    t_l = lax.broadcasted_iota(jnp.int32, (L, L), 0)
    t_r = lax.broadcasted_iota(jnp.int32, (L, L), 1)
    tri = (t_r <= t_l).astype(f32)
    shifts = jnp.stack([(t_l - t_r == CONV_WIDTH - 1 - j).astype(bf16)
                        for j in range(CONV_WIDTH - 1)])
    h_t = lax.broadcasted_iota(jnp.int32, (8, HALO), 0)
    h_r = lax.broadcasted_iota(jnp.int32, (8, HALO), 1)
    halo_shifts = jnp.stack([(h_r - HALO - h_t == -(CONV_WIDTH - 1 - j)).astype(bf16)
                             for j in range(CONV_WIDTH - 1)])

    n_assign = T * TOP_K
    n_blocks = -(-n_assign // TM_EXPERT) + N_EXPERTS
    n_rows = n_blocks * TM_EXPERT
    n_table = n_rows + 3 * TM_EXPERT
    fill = n_assign + ((jnp.arange(n_table, dtype=jnp.int32) + (NBUF - 1) * TM_EXPERT)
                       % (NBUF * TM_EXPERT))
    x2 = x.reshape(T, D)
    for l in range(depth):
        w = w_in[l]
        w_main = jnp.concatenate([w[:, :4 * W], w[:, 4 * W + N_GATES:]], axis=1).astype(bf16)
        wg_t = jnp.zeros((BF16_SUBLANES, D), bf16).at[:N_GATES].set(
            w[:, 4 * W:4 * W + N_GATES].T.astype(bf16))
        p, gates_t = _in_proj(x2, norm1_g[l][None, :], w_main, wg_t)

        gate_b = jnp.concatenate([ig_b[l], fg_b[l]])[:, None].astype(f32)
        ym = _mlstm(p, gates_t, conv_w[l].astype(f32), gate_b, head_norm_g[l][None, :], tri,
                    shifts, halo_shifts, B, S)

        x1, h2, idx_t, gate_t, rank_t, cnt = _out_route(
            x2, ym, p, pool_w[l].astype(bf16), pool_scale[l][None, :], w_out[l].astype(bf16),
            norm2_g[l][None, :], w_router[l].T.astype(bf16), b_router[l][:, None], S)

        counts = cnt[:, 0]
        padded = ((counts + TM_EXPERT - 1) // TM_EXPERT) * TM_EXPERT
        padded_end = jnp.cumsum(padded)
        padded_start = padded_end - padded
        expert_ids = jnp.arange(N_EXPERTS, dtype=jnp.int32)[:, None, None]
        start_of = jnp.sum(jnp.where(idx_t[None] == expert_ids, padded_start[:, None, None], 0), axis=0)
        dest = start_of + rank_t
        block_start = jnp.concatenate(
            [jnp.zeros((1,), jnp.int32), (padded_end // TM_EXPERT).astype(jnp.int32)])

        slot_buf = _plan(dest.reshape(-1) + TM_EXPERT, fill)
        y_tok = _experts(block_start, slot_buf, h2, w_gate[l], b_gate[l][:, None, :],
                         w_up[l], b_up[l][:, None, :], w_down[l], b_down[l][:, None, :], T)
        last = l + 1 == depth
        x2 = _combine(x1, y_tok, gate_t, normf_g[None, :], last)
    return x2.reshape(B, S, D)
```

```python
import functools

import jax
import jax.numpy as jnp
from jax import lax
from jax.experimental import pallas as pl
from jax.experimental.pallas import tpu as pltpu

D_MODEL = 1024
MLSTM_WIDTH = 512
MLSTM_HEADS = 4
HEAD_DIM = 128
CONV_WIDTH = 4
POOL_WIDTH = 512
POOL_WINDOWS = (2, 4, 8, 16)
POOL_GROUP_DIM = 128
N_EXPERTS = 32
TOP_K = 4
D_FF = 1024
SWIGLU_LIMIT = 7.0
SWIGLU_ALPHA = 1.702
EPS = 1e-5

N_MAIN = 4 * MLSTM_WIDTH + POOL_WIDTH
N_GATES = 2 * MLSTM_HEADS

LANES = 128
BF16_SUBLANES = 16
VMEM_LIMIT = 56 * 1024 * 1024

TM_PROJ = 512
CHUNK = 256
HALO = 16
TM_EXPERT = 256
NBUF = 3
SLAB = D_MODEL // LANES
PLAN_UNROLL = 16

NT_DIMS = (((1,), (1,)), ((), ()))
TN_DIMS = (((0,), (0,)), ((), ()))


def _sigmoid(x):
    return 1.0 / (1.0 + jnp.exp(-x))


def _in_proj_kernel(x_ref, g_ref, w_ref, wgt_ref, p_ref, gt_ref):
    x = x_ref[...]
    h = x * lax.rsqrt(jnp.mean(x * x, axis=-1, keepdims=True) + EPS) * g_ref[...]
    hb = h.astype(jnp.bfloat16)
    p_ref[...] = jnp.dot(hb, w_ref[...], preferred_element_type=jnp.float32).astype(p_ref.dtype)
    gt = lax.dot_general(wgt_ref[...], hb, NT_DIMS, preferred_element_type=jnp.float32)
    gt_ref[...] = gt[:N_GATES]


def _in_proj(x2, g1, w_main, wg_t):
    T = x2.shape[0]
    return pl.pallas_call(
        _in_proj_kernel,
        grid=(T // TM_PROJ,),
        in_specs=[
            pl.BlockSpec((TM_PROJ, D_MODEL), lambda i: (i, 0)),
            pl.BlockSpec((1, D_MODEL), lambda i: (0, 0)),
            pl.BlockSpec((D_MODEL, N_MAIN), lambda i: (0, 0)),
            pl.BlockSpec((BF16_SUBLANES, D_MODEL), lambda i: (0, 0)),
        ],
        out_specs=[
            pl.BlockSpec((TM_PROJ, N_MAIN), lambda i: (i, 0)),
            pl.BlockSpec((N_GATES, TM_PROJ), lambda i: (0, i)),
        ],
        out_shape=[
            jax.ShapeDtypeStruct((T, N_MAIN), jnp.bfloat16),
            jax.ShapeDtypeStruct((N_GATES, T), jnp.float32),
        ],
        compiler_params=pltpu.CompilerParams(
            dimension_semantics=("parallel",), vmem_limit_bytes=VMEM_LIMIT),
        name="in_proj",
    )(x2, g1, w_main, wg_t)


def _mlstm_kernel(qk_ref, qkp_ref, v_ref, o_ref, gt_ref, convw_ref, gb_ref, hng_ref,
                  tri_ref, shift_ref, hshift_ref, y_ref, cn_ref, m_ref):
    L = CHUNK
    c = pl.program_id(1)

    @pl.when(c == 0)
    def _():
        cn_ref[...] = jnp.zeros_like(cn_ref)
        m_ref[...] = jnp.zeros_like(m_ref)

    x_cur = qk_ref[...]
    x_prev = jnp.where(c > 0, qkp_ref[...], jnp.zeros_like(qkp_ref))
    acc = convw_ref[CONV_WIDTH - 1:CONV_WIDTH, :] * x_cur.astype(jnp.float32)
    for j in range(CONV_WIDTH - 1):
        sh = jnp.dot(shift_ref[j], x_cur, preferred_element_type=jnp.float32)
        top = sh[:8] + jnp.dot(hshift_ref[j], x_prev, preferred_element_type=jnp.float32)
        sh = jnp.concatenate([top, sh[8:]], axis=0)
        acc = acc + convw_ref[j:j + 1, :] * sh
    qk = acc * _sigmoid(acc)
    q_all = qk[:, :MLSTM_WIDTH].astype(jnp.bfloat16)
    k_t = jnp.transpose(qk[:, MLSTM_WIDTH:] * (HEAD_DIM ** -0.5))

    gt = gt_ref[...] + gb_ref[...]
    f = gt[MLSTM_HEADS:]
    lf = jnp.minimum(f, 0.0) - jnp.log(1.0 + jnp.exp(-jnp.abs(f)))
    ig = gt[:MLSTM_HEADS]
    b_rows = lax.dot_general(lf, tri_ref[...], NT_DIMS, precision=lax.Precision.HIGHEST,
                             preferred_element_type=jnp.float32)
    c_rows = ig - b_rows
    lane = lax.broadcasted_iota(jnp.int32, (MLSTM_HEADS, L), 1)
    cm_rows = c_rows
    d = 1
    while d < L:
        cm_rows = jnp.maximum(cm_rows, jnp.where(lane >= d, pltpu.roll(cm_rows, d, axis=1), -jnp.inf))
        d *= 2
    cols = jnp.transpose(jnp.concatenate([b_rows, cm_rows], axis=0))

    row_id = lax.broadcasted_iota(jnp.int32, (L, L), 0)
    col_id = lax.broadcasted_iota(jnp.int32, (L, L), 1)
    causal = col_id <= row_id
    ones_col = (lax.broadcasted_iota(jnp.int32, (L, HEAD_DIM), 1) == 0).astype(jnp.bfloat16)

    for h in range(MLSTM_HEADS):
        lo = h * HEAD_DIM
        q = q_all[:, lo:lo + HEAD_DIM]
        kt = k_t[lo:lo + HEAD_DIM, :]
        v_ext = jnp.concatenate([v_ref[:, lo:lo + HEAD_DIM], ones_col], axis=1)
        b_col = cols[:, h:h + 1]
        cm_col = cols[:, MLSTM_HEADS + h:MLSTM_HEADS + h + 1]
        c_row = c_rows[h:h + 1, :]
        b_tot = b_rows[h:h + 1, L - 1:L]
        cm_tot = cm_rows[h:h + 1, L - 1:L]
        m_in = m_ref[h][0:1, 0:1]
        cn = cn_ref[h]

        mx_col = jnp.maximum(cm_col, m_in)
        m_out = b_col + mx_col
        s_qk = jnp.dot(q, kt.astype(jnp.bfloat16), preferred_element_type=jnp.float32)
        s = (s_qk * jnp.exp(jnp.where(causal, c_row - mx_col, -jnp.inf))).astype(jnp.bfloat16)
        inter = jnp.exp(m_in - mx_col)
        num = (jnp.dot(s, v_ext, preferred_element_type=jnp.float32)
               + inter * jnp.dot(q, cn.astype(jnp.bfloat16), preferred_element_type=jnp.float32))
        den = num[:, HEAD_DIM:HEAD_DIM + 1]
        hh = num[:, :HEAD_DIM] / jnp.maximum(jnp.abs(den), jnp.exp(-m_out))

        mu = jnp.mean(hh, axis=-1, keepdims=True)
        dv = hh - mu
        var = jnp.mean(dv * dv, axis=-1, keepdims=True)
        hn = dv * lax.rsqrt(var + EPS) * hng_ref[:, lo:lo + HEAD_DIM]
        og = _sigmoid(o_ref[:, lo:lo + HEAD_DIM].astype(jnp.float32))
        y_ref[:, lo:lo + HEAD_DIM] = (og * hn).astype(y_ref.dtype)

        m_loc = b_tot + cm_tot
        kw_t = (kt * jnp.exp(c_row - cm_tot)).astype(jnp.bfloat16)
        c_loc = jnp.dot(kw_t, v_ext, preferred_element_type=jnp.float32)
        m_new = jnp.maximum(b_tot + m_in, m_loc)
        s_old = jnp.exp(b_tot + m_in - m_new)
        s_loc = jnp.exp(m_loc - m_new)
        cn_ref[h] = s_old * cn + s_loc * c_loc
        m_ref[h] = jnp.broadcast_to(m_new, m_ref.shape[1:])


def _mlstm(p, gates_t, conv_w, gate_b, hn_g, tri, shifts, halo_shifts, batch, seq):
    T = batch * seq
    L = CHUNK
    nc = seq // L
    halo_per_chunk = L // HALO

    def cur(bi, ci):
        return bi * nc + ci

    return pl.pallas_call(
        _mlstm_kernel,
        grid=(batch, nc),
        in_specs=[
            pl.BlockSpec((L, 2 * MLSTM_WIDTH), lambda bi, ci: (cur(bi, ci), 0)),
            pl.BlockSpec((HALO, 2 * MLSTM_WIDTH),
                         lambda bi, ci: (jnp.maximum(cur(bi, ci) * halo_per_chunk - 1, 0), 0)),
            pl.BlockSpec((L, MLSTM_WIDTH), lambda bi, ci: (cur(bi, ci), 2)),
            pl.BlockSpec((L, MLSTM_WIDTH), lambda bi, ci: (cur(bi, ci), 3)),
            pl.BlockSpec((N_GATES, L), lambda bi, ci: (0, cur(bi, ci))),
            pl.BlockSpec((CONV_WIDTH, 2 * MLSTM_WIDTH), lambda bi, ci: (0, 0)),
            pl.BlockSpec((N_GATES, 1), lambda bi, ci: (0, 0)),
            pl.BlockSpec((1, MLSTM_WIDTH), lambda bi, ci: (0, 0)),
            pl.BlockSpec((L, L), lambda bi, ci: (0, 0)),
            pl.BlockSpec((CONV_WIDTH - 1, L, L), lambda bi, ci: (0, 0, 0)),
            pl.BlockSpec((CONV_WIDTH - 1, 8, HALO), lambda bi, ci: (0, 0, 0)),
        ],
        out_specs=pl.BlockSpec((L, MLSTM_WIDTH), lambda bi, ci: (cur(bi, ci), 0)),
        out_shape=jax.ShapeDtypeStruct((T, MLSTM_WIDTH), jnp.bfloat16),
        scratch_shapes=[
            pltpu.VMEM((MLSTM_HEADS, HEAD_DIM, 2 * HEAD_DIM), jnp.float32),
            pltpu.VMEM((MLSTM_HEADS, 8, LANES), jnp.float32),
        ],
        compiler_params=pltpu.CompilerParams(
            dimension_semantics=("parallel", "arbitrary"), vmem_limit_bytes=VMEM_LIMIT),
        name="mlstm",
    )(p, p, p, p, gates_t, conv_w, gate_b, hn_g, tri, shifts, halo_shifts)


def _out_route_kernel(seq, x_ref, ym_ref, u_ref, up_ref, pw_ref, ps_ref, wo_ref, g2_ref,
                      wrt_ref, br_ref, x1_ref, h2_ref, idx_ref, gate_ref, rank_ref, cnt_ref,
                      ubuf_ref, carry_ref):
    TM = TM_PROJ
    i = pl.program_id(0)

    @pl.when(i == 0)
    def _():
        carry_ref[...] = jnp.zeros_like(carry_ref)

    pos0 = (i * TM) % seq
    ubuf_ref[0:HALO, :] = jnp.where(pos0 > 0, up_ref[...].astype(jnp.float32), 0.0)
    ubuf_ref[HALO:HALO + TM, :] = u_ref[...].astype(jnp.float32)
    pos = (pos0 + lax.broadcasted_iota(jnp.int32, (TM, 1), 0) + 1).astype(jnp.float32)
    mixed = []
    for gi, w in enumerate(POOL_WINDOWS):
        lo = gi * POOL_GROUP_DIM
        tok = ubuf_ref[HALO:HALO + TM, lo:lo + POOL_GROUP_DIM]
        acc = tok
        for j in range(1, w):
            acc = acc + ubuf_ref[HALO - j:HALO - j + TM, lo:lo + POOL_GROUP_DIM]
        pooled = acc / jnp.minimum(pos, float(w)) - tok
        mg = jnp.dot(pooled.astype(jnp.bfloat16), pw_ref[gi], preferred_element_type=jnp.float32)
        mixed.append((mg * ps_ref[:, lo:lo + POOL_GROUP_DIM]).astype(jnp.bfloat16))
    y_cat = jnp.concatenate([ym_ref[...]] + mixed, axis=1)

    x1 = x_ref[...] + jnp.dot(y_cat, wo_ref[...], preferred_element_type=jnp.float32)
    x1_ref[...] = x1
    h2 = x1 * lax.rsqrt(jnp.mean(x1 * x1, axis=-1, keepdims=True) + EPS) * g2_ref[...]
    h2b = h2.astype(jnp.bfloat16)
    for s in range(SLAB):
        h2_ref[pl.ds(s, TM, stride=SLAB), :] = h2[:, s * LANES:(s + 1) * LANES]

    logits = lax.dot_general(wrt_ref[...], h2b, NT_DIMS,
                             preferred_element_type=jnp.float32) + br_ref[...]
    e_id = lax.broadcasted_iota(jnp.int32, (N_EXPERTS, TM), 0).astype(jnp.float32)
    work = logits
    vals, ids, hots = [], [], []
    for _ in range(TOP_K):
        mk = jnp.max(work, axis=0, keepdims=True)
        ik = jnp.min(jnp.where(work == mk, e_id, float(N_EXPERTS)), axis=0, keepdims=True)
        hot = e_id == ik
        work = jnp.where(hot, -jnp.inf, work)
        vals.append(mk)
        ids.append(ik)
        hots.append(hot)
    ex = [jnp.exp(vk - vals[0]) for vk in vals]
    denom = ex[0] + ex[1] + ex[2] + ex[3]
    gate_ref[...] = jnp.concatenate([e / denom for e in ex], axis=0)
    idx_ref[...] = jnp.concatenate(ids, axis=0).astype(jnp.int32)

    sel_f = sum(jnp.where(hot, 1.0, 0.0) for hot in hots)
    t_row = lax.broadcasted_iota(jnp.int32, (TM, TM), 0)
    t_col = lax.broadcasted_iota(jnp.int32, (TM, TM), 1)
    before = jnp.where(t_row < t_col, 1.0, 0.0).astype(jnp.bfloat16)
    prefix = jnp.dot(sel_f.astype(jnp.bfloat16), before, preferred_element_type=jnp.float32)
    carry = carry_ref[...]
    rank_e = carry[:, 0:1] + prefix
    ranks = [jnp.sum(jnp.where(hot, rank_e, 0.0), axis=0, keepdims=True) for hot in hots]
    rank_ref[...] = jnp.concatenate(ranks, axis=0).astype(jnp.int32)
    carry_new = carry + jnp.sum(sel_f, axis=1, keepdims=True)
    carry_ref[...] = carry_new
    cnt_ref[...] = carry_new.astype(jnp.int32)


def _out_route(x2, ym, p, pool_w, pool_s, w_out, g2, wr_t, br, seq):
    T = x2.shape[0]
    TM = TM_PROJ
    nt = T // TM
    u_blk = N_MAIN // POOL_WIDTH - 1
    halo_per_tile = TM // HALO
    tok_spec = pl.BlockSpec((TOP_K, TM), lambda i: (0, i))
    return pl.pallas_call(
        functools.partial(_out_route_kernel, seq),
        grid=(nt,),
        in_specs=[
            pl.BlockSpec((TM, D_MODEL), lambda i: (i, 0)),
            pl.BlockSpec((TM, MLSTM_WIDTH), lambda i: (i, 0)),
            pl.BlockSpec((TM, POOL_WIDTH), lambda i: (i, u_blk)),
            pl.BlockSpec((HALO, POOL_WIDTH),
                         lambda i: (jnp.maximum(i * halo_per_tile - 1, 0), u_blk)),
            pl.BlockSpec((len(POOL_WINDOWS), POOL_GROUP_DIM, POOL_GROUP_DIM), lambda i: (0, 0, 0)),
            pl.BlockSpec((1, POOL_WIDTH), lambda i: (0, 0)),
            pl.BlockSpec((D_MODEL, D_MODEL), lambda i: (0, 0)),
            pl.BlockSpec((1, D_MODEL), lambda i: (0, 0)),
            pl.BlockSpec((N_EXPERTS, D_MODEL), lambda i: (0, 0)),
            pl.BlockSpec((N_EXPERTS, 1), lambda i: (0, 0)),
        ],
        out_specs=[
            pl.BlockSpec((TM, D_MODEL), lambda i: (i, 0)),
            pl.BlockSpec((TM * SLAB, LANES), lambda i: (i, 0)),
            tok_spec, tok_spec, tok_spec,
            pl.BlockSpec((N_EXPERTS, LANES), lambda i: (0, 0)),
        ],
        out_shape=[
            jax.ShapeDtypeStruct((T, D_MODEL), jnp.float32),
            jax.ShapeDtypeStruct((T * SLAB, LANES), jnp.float32),
            jax.ShapeDtypeStruct((TOP_K, T), jnp.int32),
            jax.ShapeDtypeStruct((TOP_K, T), jnp.float32),
            jax.ShapeDtypeStruct((TOP_K, T), jnp.int32),
            jax.ShapeDtypeStruct((N_EXPERTS, LANES), jnp.int32),
        ],
        scratch_shapes=[
            pltpu.VMEM((HALO + TM, POOL_WIDTH), jnp.float32),
            pltpu.VMEM((N_EXPERTS, LANES), jnp.float32),
        ],
        compiler_params=pltpu.CompilerParams(
            dimension_semantics=("arbitrary",), vmem_limit_bytes=VMEM_LIMIT),
        name="out_route",
    )(x2, ym, p, p, pool_w, pool_s, w_out, g2, wr_t, br)


def _plan_kernel(n_assign, dest_ref, fill_ref, slot_ref, sem):
    cp = pltpu.make_async_copy(fill_ref, slot_ref, sem)
    cp.start()
    cp.wait()

    def body(i, carry):
        base = i * PLAN_UNROLL
        for j in range(PLAN_UNROLL):
            slot_ref[dest_ref[base + j]] = base + j
        return carry

    lax.fori_loop(0, n_assign // PLAN_UNROLL, body, 0)


def _plan(dest_flat, fill):
    n_assign = dest_flat.shape[0]
    return pl.pallas_call(
        functools.partial(_plan_kernel, n_assign),
        in_specs=[
            pl.BlockSpec(memory_space=pltpu.SMEM),
            pl.BlockSpec(memory_space=pl.ANY),
        ],
        out_specs=pl.BlockSpec(memory_space=pltpu.SMEM),
        out_shape=jax.ShapeDtypeStruct(fill.shape, jnp.int32),
        scratch_shapes=[pltpu.SemaphoreType.DMA(())],
        name="plan",
    )(dest_flat, fill)


def _expert_kernel(n_tok, bs_ref, slot_ref, h2_ref, wg_ref, bg_ref, wu_ref, bu_ref, wd_ref, bd_ref,
                   yt_ref, xg0_ref, xg1_ref, xg2_ref, ys0_ref, ys1_ref, ys2_ref,
                   wgb_ref, wub_ref, wdb_ref, gsem, ssem):
    TM = TM_EXPERT
    ROWS = TM * SLAB
    e = pl.program_id(0)
    n_total = bs_ref[N_EXPERTS]
    xg = (xg0_ref, xg1_ref, xg2_ref)
    ys = (ys0_ref, ys1_ref, ys2_ref)

    def token_of(a):
        return a & (n_tok - 1) if n_tok & (n_tok - 1) == 0 else lax.rem(a, n_tok)

    def start_gather(blk, par):
        base = (blk + 1) * TM
        for r in range(TM):
            t = token_of(slot_ref[base + r])
            pltpu.make_async_copy(h2_ref.at[pl.ds(pl.multiple_of(t * SLAB, SLAB), SLAB), :],
                                  xg[par].at[pl.ds(r * SLAB, SLAB), :], gsem.at[par]
                                  ).start(priority=r % 2)

    def wait_gather(par):
        pltpu.make_async_copy(h2_ref.at[pl.ds(0, ROWS), :], xg[0], gsem.at[par]).wait()

    def start_scatter(blk, par):
        base = (blk + 1) * TM
        for r in range(TM):
            a = slot_ref[base + r]
            pltpu.make_async_copy(ys[par].at[pl.ds(r * SLAB, SLAB), :],
                                  yt_ref.at[pl.ds(pl.multiple_of(a * SLAB, SLAB), SLAB), :],
                                  ssem.at[par]).start(priority=r % 2)

    def wait_scatter(par):
        pltpu.make_async_copy(ys[0], yt_ref.at[pl.ds(0, ROWS), :], ssem.at[par]).wait()

    @pl.when(e == 0)
    def _():
        start_gather(0, 0)
        start_gather(1, 1)
        for par in range(NBUF):
            ys[par][...] = jnp.zeros_like(ys[par])
            dump = yt_ref.at[pl.ds((n_tok * TOP_K + par * TM) * SLAB, ROWS), :]
            cp = pltpu.make_async_copy(ys[par], dump, ssem.at[par])
            cp.start()
            cp.wait()

    wgb_ref[...] = wg_ref[0].astype(jnp.bfloat16)
    wub_ref[...] = wu_ref[0].astype(jnp.bfloat16)
    wdb_ref[...] = wd_ref[0].astype(jnp.bfloat16)

    def block_step(g, par):
        nxt2 = (par + 2) % NBUF
        wait_gather(par)

        @pl.when(g >= 2)
        def _():
            wait_scatter(par)

        start_gather(g + 2, nxt2)
        start_scatter(g - 1, nxt2)
        x = jnp.concatenate(
            [xg[par][pl.ds(s, TM, stride=SLAB), :].astype(jnp.bfloat16) for s in range(SLAB)],
            axis=1)
        gate = jnp.dot(x, wgb_ref[...], preferred_element_type=jnp.float32) + bg_ref[0]
        up = jnp.dot(x, wub_ref[...], preferred_element_type=jnp.float32) + bu_ref[0]
        gate = jnp.minimum(gate, SWIGLU_LIMIT)
        up = jnp.clip(up, -SWIGLU_LIMIT, SWIGLU_LIMIT)
        glu = gate * _sigmoid(SWIGLU_ALPHA * gate)
        act = (glu * (up + 1.0)).astype(jnp.bfloat16)
        y = jnp.dot(act, wdb_ref[...], preferred_element_type=jnp.float32) + bd_ref[0]
        for s in range(SLAB):
            ys[par][pl.ds(s, TM, stride=SLAB), :] = y[:, s * LANES:(s + 1) * LANES]

    def body(g, carry):
        for par in range(NBUF):
            pl.when(g % NBUF == par)(functools.partial(block_step, g, par))
        return carry

    lax.fori_loop(bs_ref[e], bs_ref[e + 1], body, 0)

    @pl.when(e == N_EXPERTS - 1)
    def _():
        g = n_total
        for par in range(NBUF):
            @pl.when((g - 1) % NBUF == par)
            def _():
                start_scatter(g - 1, par)
        wait_gather(g % NBUF)
        wait_gather((g + 1) % NBUF)
        wait_scatter((g - 1) % NBUF)
        wait_scatter((g + 1) % NBUF)

        @pl.when(g >= 2)
        def _():
            wait_scatter(g % NBUF)


def _experts(block_start, slot_buf, h2_slab, w_gate, b_gate, w_up, b_up, w_down, b_down, n_tok):
    TM = TM_EXPERT
    n_assign = n_tok * TOP_K
    w_spec = pl.BlockSpec((1, D_MODEL, D_FF), lambda e, bs, sl: (e, 0, 0))
    bias_spec = pl.BlockSpec((1, 1, D_FF), lambda e, bs, sl: (e, 0, 0))
    buf = pltpu.VMEM((TM * SLAB, LANES), jnp.float32)
    grid_spec = pltpu.PrefetchScalarGridSpec(
        num_scalar_prefetch=2,
        grid=(N_EXPERTS,),
        in_specs=[
            pl.BlockSpec(memory_space=pl.ANY),
            w_spec, bias_spec, w_spec, bias_spec, w_spec, bias_spec,
        ],
        out_specs=pl.BlockSpec(memory_space=pl.ANY),
        scratch_shapes=[
            buf, buf, buf, buf, buf, buf,
            pltpu.VMEM((D_MODEL, D_FF), jnp.bfloat16),
            pltpu.VMEM((D_MODEL, D_FF), jnp.bfloat16),
            pltpu.VMEM((D_FF, D_MODEL), jnp.bfloat16),
            pltpu.SemaphoreType.DMA((NBUF,)),
            pltpu.SemaphoreType.DMA((NBUF,)),
        ],
    )
    return pl.pallas_call(
        functools.partial(_expert_kernel, n_tok),
        grid_spec=grid_spec,
        out_shape=jax.ShapeDtypeStruct(((n_assign + NBUF * TM) * SLAB, LANES), jnp.float32),
        compiler_params=pltpu.CompilerParams(
            dimension_semantics=("arbitrary",), vmem_limit_bytes=VMEM_LIMIT),
        name="experts",
    )(block_start, slot_buf, h2_slab, w_gate, b_gate, w_up, b_up, w_down, b_down)


def _combine_kernel(normalize, x1_ref, y0_ref, y1_ref, y2_ref, y3_ref, gate_ref, g_ref, o_ref):
    TM = TM_PROJ
    gates = jnp.concatenate([gate_ref[...], jnp.zeros((8 - TOP_K, TM), jnp.float32)], axis=0)
    g_cols = jnp.transpose(gates)
    g_bc = [jnp.broadcast_to(g_cols[:, k:k + 1], (TM, LANES)) for k in range(TOP_K)]
    parts = []
    ssq = jnp.zeros((TM, LANES), jnp.float32)
    for s in range(SLAB):
        acc = x1_ref[:, s * LANES:(s + 1) * LANES]
        for k, y_ref in enumerate((y0_ref, y1_ref, y2_ref, y3_ref)):
            acc = acc + g_bc[k] * y_ref[pl.ds(s, TM, stride=SLAB), :]
        parts.append(acc)
        ssq = ssq + acc * acc
    if normalize:
        inv = lax.rsqrt(jnp.sum(ssq, axis=-1, keepdims=True) * (1.0 / D_MODEL) + EPS)
        for s in range(SLAB):
            o_ref[:, s * LANES:(s + 1) * LANES] = parts[s] * inv * g_ref[:, s * LANES:(s + 1) * LANES]
    else:
        for s in range(SLAB):
            o_ref[:, s * LANES:(s + 1) * LANES] = parts[s]


def _combine(x1, y_tok, gate_t, gf, normalize):
    T = x1.shape[0]
    TM = TM_PROJ
    nt = T // TM

    def y_spec(k):
        return pl.BlockSpec((TM * SLAB, LANES), lambda i: (k * nt + i, 0))

    return pl.pallas_call(
        functools.partial(_combine_kernel, normalize),
        grid=(nt,),
        in_specs=[
            pl.BlockSpec((TM, D_MODEL), lambda i: (i, 0)),
            y_spec(0), y_spec(1), y_spec(2), y_spec(3),
            pl.BlockSpec((TOP_K, TM), lambda i: (0, i)),
            pl.BlockSpec((1, D_MODEL), lambda i: (0, 0)),
        ],
        out_specs=pl.BlockSpec((TM, D_MODEL), lambda i: (i, 0)),
        out_shape=jax.ShapeDtypeStruct((T, D_MODEL), jnp.float32),
        compiler_params=pltpu.CompilerParams(
            dimension_semantics=("parallel",), vmem_limit_bytes=VMEM_LIMIT),
        name="combine",
    )(x1, y_tok, y_tok, y_tok, y_tok, gate_t, gf)


def kernel(x, norm1_g, w_in, ig_b, fg_b, conv_w, head_norm_g, pool_w, pool_scale, w_out, norm2_g,
           w_router, b_router, w_gate, b_gate, w_up, b_up, w_down, b_down, normf_g):
    B, S, D = x.shape
    T = B * S
    depth = norm1_g.shape[0]
    W = MLSTM_WIDTH
    f32, bf16 = jnp.float32, jnp.bfloat16

    L = CHUNK
    t_l = lax.broadcasted_iota(jnp.int32, (L, L), 0)
    t_r = lax.broadcasted_iota(jnp.int32, (L, L), 1)
    tri = (t_r <= t_l).astype(f32)
    shifts = jnp.stack([(t_l - t_r == CONV_WIDTH - 1 - j).astype(bf16)
                        for j in range(CONV_WIDTH - 1)])
    h_t = lax.broadcasted_iota(jnp.int32, (8, HALO), 0)
    h_r = lax.broadcasted_iota(jnp.int32, (8, HALO), 1)
    halo_shifts = jnp.stack([(h_r - HALO - h_t == -(CONV_WIDTH - 1 - j)).astype(bf16)
                             for j in range(CONV_WIDTH - 1)])

    n_assign = T * TOP_K
    n_blocks = -(-n_assign // TM_EXPERT) + N_EXPERTS
    n_rows = n_blocks * TM_EXPERT
    n_table = n_rows + 3 * TM_EXPERT
    fill = n_assign + ((jnp.arange(n_table, dtype=jnp.int32) + (NBUF - 1) * TM_EXPERT)
                       % (NBUF * TM_EXPERT))
    x2 = x.reshape(T, D)
    for l in range(depth):
        w = w_in[l]
        w_main = jnp.concatenate([w[:, :4 * W], w[:, 4 * W + N_GATES:]], axis=1).astype(bf16)
        wg_t = jnp.zeros((BF16_SUBLANES, D), bf16).at[:N_GATES].set(
            w[:, 4 * W:4 * W + N_GATES].T.astype(bf16))
        p, gates_t = _in_proj(x2, norm1_g[l][None, :], w_main, wg_t)

        gate_b = jnp.concatenate([ig_b[l], fg_b[l]])[:, None].astype(f32)
        ym = _mlstm(p, gates_t, conv_w[l].astype(f32), gate_b, head_norm_g[l][None, :], tri,
                    shifts, halo_shifts, B, S)

        x1, h2, idx_t, gate_t, rank_t, cnt = _out_route(
            x2, ym, p, pool_w[l].astype(bf16), pool_scale[l][None, :], w_out[l].astype(bf16),
            norm2_g[l][None, :], w_router[l].T.astype(bf16), b_router[l][:, None], S)

        counts = cnt[:, 0]
        padded = ((counts + TM_EXPERT - 1) // TM_EXPERT) * TM_EXPERT
        padded_end = jnp.cumsum(padded)
        padded_start = padded_end - padded
        expert_ids = jnp.arange(N_EXPERTS, dtype=jnp.int32)[:, None, None]
        start_of = jnp.sum(jnp.where(idx_t[None] == expert_ids, padded_start[:, None, None], 0), axis=0)
        dest = start_of + rank_t
        block_start = jnp.concatenate(
            [jnp.zeros((1,), jnp.int32), (padded_end // TM_EXPERT).astype(jnp.int32)])

        slot_buf = _plan(dest.reshape(-1) + TM_EXPERT, fill)
        y_tok = _experts(block_start, slot_buf, h2, w_gate[l], b_gate[l][:, None, :],
                         w_up[l], b_up[l][:, None, :], w_down[l], b_down[l][:, None, :], T)
        last = l + 1 == depth
        x2 = _combine(x1, y_tok, gate_t, normf_g[None, :], last)
    return x2.reshape(B, S, D)
```

```python
import functools

import jax
import jax.numpy as jnp
from jax import lax
from jax.experimental import pallas as pl
from jax.experimental.pallas import tpu as pltpu

D_MODEL = 1024
MLSTM_WIDTH = 512
MLSTM_HEADS = 4
HEAD_DIM = 128
CONV_WIDTH = 4
POOL_WIDTH = 512
POOL_WINDOWS = (2, 4, 8, 16)
POOL_GROUP_DIM = 128
N_EXPERTS = 32
TOP_K = 4
D_FF = 1024
SWIGLU_LIMIT = 7.0
SWIGLU_ALPHA = 1.702
EPS = 1e-5

N_MAIN = 4 * MLSTM_WIDTH + POOL_WIDTH
N_GATES = 2 * MLSTM_HEADS

LANES = 128
BF16_SUBLANES = 16
VMEM_LIMIT = 56 * 1024 * 1024

TM_PROJ = 512
CHUNK = 256
MLSTM_BATCH = 2
HALO = 16
TM_EXPERT = 256
NBUF = 3
SLAB = D_MODEL // LANES
PLAN_UNROLL = 16

NT_DIMS = (((1,), (1,)), ((), ()))


def _sigmoid(x):
    return 1.0 / (1.0 + jnp.exp(-x))


def _in_proj_kernel(x_ref, g_ref, w_ref, wgt_ref, p_ref, gt_ref):
    x = x_ref[...]
    h = x * lax.rsqrt(jnp.mean(x * x, axis=-1, keepdims=True) + EPS) * g_ref[...]
    hb = h.astype(jnp.bfloat16)
    p_ref[...] = jnp.dot(hb, w_ref[...], preferred_element_type=jnp.float32).astype(p_ref.dtype)
    gt = lax.dot_general(wgt_ref[...], hb, NT_DIMS, preferred_element_type=jnp.float32)
    gt_ref[...] = gt[:N_GATES]


def _in_proj(x2, g1, w_main, wg_t):
    T = x2.shape[0]
    return pl.pallas_call(
        _in_proj_kernel,
        grid=(T // TM_PROJ,),
        in_specs=[
            pl.BlockSpec((TM_PROJ, D_MODEL), lambda i: (i, 0)),
            pl.BlockSpec((1, D_MODEL), lambda i: (0, 0)),
            pl.BlockSpec((D_MODEL, N_MAIN), lambda i: (0, 0)),
            pl.BlockSpec((BF16_SUBLANES, D_MODEL), lambda i: (0, 0)),
        ],
        out_specs=[
            pl.BlockSpec((TM_PROJ, N_MAIN), lambda i: (i, 0)),
            pl.BlockSpec((N_GATES, TM_PROJ), lambda i: (0, i)),
        ],
        out_shape=[
            jax.ShapeDtypeStruct((T, N_MAIN), jnp.bfloat16),
            jax.ShapeDtypeStruct((N_GATES, T), jnp.float32),
        ],
        compiler_params=pltpu.CompilerParams(
            dimension_semantics=("parallel",), vmem_limit_bytes=VMEM_LIMIT),
        name="in_proj",
    )(x2, g1, w_main, wg_t)


def _mlstm_kernel(qk_ref, qkp_ref, v_ref, o_ref, gt_ref, convw_ref, gb_ref, hng_ref,
                  tri_ref, shift_ref, hshift_ref, y_ref, cn_ref, m_ref):
    L = CHUNK
    c = pl.program_id(1)

    @pl.when(c == 0)
    def _():
        cn_ref[...] = jnp.zeros_like(cn_ref)
        m_ref[...] = jnp.zeros_like(m_ref)

    row_id = lax.broadcasted_iota(jnp.int32, (L, L), 0)
    col_id = lax.broadcasted_iota(jnp.int32, (L, L), 1)
    causal = col_id <= row_id
    ones_blk = jnp.ones((L, HEAD_DIM), jnp.bfloat16)
    lane = lax.broadcasted_iota(jnp.int32, (MLSTM_HEADS, L), 1)

    gate_terms = []
    for bb in range(MLSTM_BATCH):
        gt = gt_ref[bb] + gb_ref[...]
        f = gt[MLSTM_HEADS:]
        lf = jnp.minimum(f, 0.0) - jnp.log(1.0 + jnp.exp(-jnp.abs(f)))
        ig = gt[:MLSTM_HEADS]
        b_rows = lax.dot_general(lf, tri_ref[...], NT_DIMS, precision=lax.Precision.HIGHEST,
                                 preferred_element_type=jnp.float32)
        c_rows = ig - b_rows
        cm_rows = c_rows
        d = 1
        while d < L:
            cm_rows = jnp.maximum(
                cm_rows, jnp.where(lane >= d, pltpu.roll(cm_rows, d, axis=1), -jnp.inf))
            d *= 2
        gate_terms.append((b_rows, c_rows, cm_rows))

    conv_terms = []
    for bb in range(MLSTM_BATCH):
        x_cur = qk_ref[bb]
        x_prev = jnp.where(c > 0, qkp_ref[bb], jnp.zeros((HALO, 2 * MLSTM_WIDTH), jnp.bfloat16))
        acc = convw_ref[CONV_WIDTH - 1:CONV_WIDTH, :] * x_cur.astype(jnp.float32)
        for j in range(CONV_WIDTH - 1):
            sh = jnp.dot(shift_ref[j], x_cur, preferred_element_type=jnp.float32)
            top = sh[:8] + jnp.dot(hshift_ref[j], x_prev, preferred_element_type=jnp.float32)
            sh = jnp.concatenate([top, sh[8:]], axis=0)
            acc = acc + convw_ref[j:j + 1, :] * sh
        qk = acc * _sigmoid(acc)
        q_all = qk[:, :MLSTM_WIDTH].astype(jnp.bfloat16)
        k_t = jnp.transpose(qk[:, MLSTM_WIDTH:] * (HEAD_DIM ** -0.5))
        conv_terms.append((q_all, k_t))

    for bb in range(MLSTM_BATCH):
        b_rows, c_rows, cm_rows = gate_terms[bb]
        q_all, k_t = conv_terms[bb]
        m_in4 = jnp.concatenate(
            [m_ref[bb * MLSTM_HEADS + h][0:1, 0:1] for h in range(MLSTM_HEADS)], axis=0)
        mx_rows = jnp.maximum(cm_rows, m_in4)
        inter_rows = jnp.exp(m_in4 - mx_rows)
        einv_rows = jnp.exp(-(b_rows + mx_rows))
        fac_t = jnp.transpose(jnp.concatenate(
            [mx_rows, inter_rows, einv_rows, jnp.zeros_like(mx_rows)], axis=0))

        for h in range(MLSTM_HEADS):
            lo = h * HEAD_DIM
            st = bb * MLSTM_HEADS + h
            q = q_all[:, lo:lo + HEAD_DIM]
            kt = k_t[lo:lo + HEAD_DIM, :]
            v_ext = jnp.concatenate([v_ref[bb, :, lo:lo + HEAD_DIM], ones_blk], axis=1)
            mx_col = fac_t[:, h:h + 1]
            inter_col = fac_t[:, MLSTM_HEADS + h:MLSTM_HEADS + h + 1]
            einv_col = fac_t[:, 2 * MLSTM_HEADS + h:2 * MLSTM_HEADS + h + 1]
            c_row = c_rows[h:h + 1, :]
            b_tot = b_rows[h:h + 1, L - 1:L]
            cm_tot = cm_rows[h:h + 1, L - 1:L]
            m_in = m_ref[st][0:1, 0:1]
            cn = cn_ref[st]

            s_qk = jnp.dot(q, kt.astype(jnp.bfloat16), preferred_element_type=jnp.float32)
            s = (s_qk * jnp.exp(jnp.where(causal, c_row - mx_col, -jnp.inf))).astype(jnp.bfloat16)
            num = (jnp.dot(s, v_ext, preferred_element_type=jnp.float32)
                   + inter_col * jnp.dot(q, cn.astype(jnp.bfloat16),
                                         preferred_element_type=jnp.float32))
            den = num[:, HEAD_DIM:]
            hh = num[:, :HEAD_DIM] / jnp.maximum(jnp.abs(den), einv_col)

            mu = jnp.mean(hh, axis=-1, keepdims=True)
            dv = hh - mu
            var = jnp.mean(dv * dv, axis=-1, keepdims=True)
            hn = dv * lax.rsqrt(var + EPS) * hng_ref[:, lo:lo + HEAD_DIM]
            og = _sigmoid(o_ref[bb, :, lo:lo + HEAD_DIM].astype(jnp.float32))
            y_ref[bb, :, lo:lo + HEAD_DIM] = (og * hn).astype(y_ref.dtype)

            m_loc = b_tot + cm_tot
            kw_t = (kt * jnp.exp(c_row - cm_tot)).astype(jnp.bfloat16)
            c_loc = jnp.dot(kw_t, v_ext, preferred_element_type=jnp.float32)
            m_new = jnp.maximum(b_tot + m_in, m_loc)
            s_old = jnp.exp(b_tot + m_in - m_new)
            s_loc = jnp.exp(m_loc - m_new)
            cn_ref[st] = s_old * cn + s_loc * c_loc
            m_ref[st] = jnp.broadcast_to(m_new, m_ref.shape[1:])


def _mlstm(p3, gates_b, conv_w, gate_b, hn_g, tri, shifts, halo_shifts):
    batch, seq, _ = p3.shape
    L = CHUNK
    BB = MLSTM_BATCH
    halo_per_chunk = L // HALO
    return pl.pallas_call(
        _mlstm_kernel,
        grid=(batch // BB, seq // L),
        in_specs=[
            pl.BlockSpec((BB, L, 2 * MLSTM_WIDTH), lambda bi, ci: (bi, ci, 0)),
            pl.BlockSpec((BB, HALO, 2 * MLSTM_WIDTH),
                         lambda bi, ci: (bi, jnp.maximum(ci * halo_per_chunk - 1, 0), 0)),
            pl.BlockSpec((BB, L, MLSTM_WIDTH), lambda bi, ci: (bi, ci, 2)),
            pl.BlockSpec((BB, L, MLSTM_WIDTH), lambda bi, ci: (bi, ci, 3)),
            pl.BlockSpec((BB, N_GATES, L), lambda bi, ci: (bi, 0, ci)),
            pl.BlockSpec((CONV_WIDTH, 2 * MLSTM_WIDTH), lambda bi, ci: (0, 0)),
            pl.BlockSpec((N_GATES, 1), lambda bi, ci: (0, 0)),
            pl.BlockSpec((1, MLSTM_WIDTH), lambda bi, ci: (0, 0)),
            pl.BlockSpec((L, L), lambda bi, ci: (0, 0)),
            pl.BlockSpec((CONV_WIDTH - 1, L, L), lambda bi, ci: (0, 0, 0)),
            pl.BlockSpec((CONV_WIDTH - 1, 8, HALO), lambda bi, ci: (0, 0, 0)),
        ],
        out_specs=pl.BlockSpec((BB, L, MLSTM_WIDTH), lambda bi, ci: (bi, ci, 0)),
        out_shape=jax.ShapeDtypeStruct((batch, seq, MLSTM_WIDTH), jnp.bfloat16),
        scratch_shapes=[
            pltpu.VMEM((BB * MLSTM_HEADS, HEAD_DIM, 2 * HEAD_DIM), jnp.float32),
            pltpu.VMEM((BB * MLSTM_HEADS, 8, LANES), jnp.float32),
        ],
        compiler_params=pltpu.CompilerParams(
            dimension_semantics=("parallel", "arbitrary"), vmem_limit_bytes=VMEM_LIMIT),
        name="mlstm",
    )(p3, p3, p3, p3, gates_b, conv_w, gate_b, hn_g, tri, shifts, halo_shifts)


def _out_route_kernel(seq, x_ref, ym_ref, u_ref, up_ref, pw_ref, ps_ref, wo_ref, g2_ref,
                      wrt_ref, br_ref, x1_ref, h2_ref, idx_ref, gate_ref, rank_ref, cnt_ref,
                      ubuf_ref, carry_ref):
    TM = TM_PROJ
    i = pl.program_id(0)

    @pl.when(i == 0)
    def _():
        carry_ref[...] = jnp.zeros_like(carry_ref)

    pos0 = (i * TM) % seq
    ubuf_ref[0:HALO, :] = jnp.where(pos0 > 0, up_ref[...].astype(jnp.float32), 0.0)
    ubuf_ref[HALO:HALO + TM, :] = u_ref[...].astype(jnp.float32)
    pos = (pos0 + lax.broadcasted_iota(jnp.int32, (TM, 1), 0) + 1).astype(jnp.float32)
    mixed = []
    for gi, w in enumerate(POOL_WINDOWS):
        lo = gi * POOL_GROUP_DIM
        tok = ubuf_ref[HALO:HALO + TM, lo:lo + POOL_GROUP_DIM]
        acc = tok
        for j in range(1, w):
            acc = acc + ubuf_ref[HALO - j:HALO - j + TM, lo:lo + POOL_GROUP_DIM]
        pooled = acc / jnp.minimum(pos, float(w)) - tok
        mg = jnp.dot(pooled.astype(jnp.bfloat16), pw_ref[gi], preferred_element_type=jnp.float32)
        mixed.append((mg * ps_ref[:, lo:lo + POOL_GROUP_DIM]).astype(jnp.bfloat16))
    y_cat = jnp.concatenate([ym_ref[...]] + mixed, axis=1)

    x1 = x_ref[...] + jnp.dot(y_cat, wo_ref[...], preferred_element_type=jnp.float32)
    x1_ref[...] = x1
    h2 = x1 * lax.rsqrt(jnp.mean(x1 * x1, axis=-1, keepdims=True) + EPS) * g2_ref[...]
    h2b = h2.astype(jnp.bfloat16)
    for s in range(SLAB):
        h2_ref[pl.ds(s, TM, stride=SLAB), :] = h2[:, s * LANES:(s + 1) * LANES]

    logits = lax.dot_general(wrt_ref[...], h2b, NT_DIMS,
                             preferred_element_type=jnp.float32) + br_ref[...]
    e_id = lax.broadcasted_iota(jnp.int32, (N_EXPERTS, TM), 0).astype(jnp.float32)
    work = logits
    vals, ids, hots = [], [], []
    for _ in range(TOP_K):
        mk = jnp.max(work, axis=0, keepdims=True)
        ik = jnp.min(jnp.where(work == mk, e_id, float(N_EXPERTS)), axis=0, keepdims=True)
        hot = e_id == ik
        work = jnp.where(hot, -jnp.inf, work)
        vals.append(mk)
        ids.append(ik)
        hots.append(hot)
    ex = [jnp.exp(vk - vals[0]) for vk in vals]
    denom = ex[0] + ex[1] + ex[2] + ex[3]
    gate_ref[...] = jnp.concatenate([e / denom for e in ex], axis=0)
    idx_ref[...] = jnp.concatenate(ids, axis=0).astype(jnp.int32)

    sel_f = sum(jnp.where(hot, 1.0, 0.0) for hot in hots)
    t_row = lax.broadcasted_iota(jnp.int32, (TM, TM), 0)
    t_col = lax.broadcasted_iota(jnp.int32, (TM, TM), 1)
    before = jnp.where(t_row < t_col, 1.0, 0.0).astype(jnp.bfloat16)
    prefix = jnp.dot(sel_f.astype(jnp.bfloat16), before, preferred_element_type=jnp.float32)
    carry = carry_ref[...]
    rank_e = carry[:, 0:1] + prefix
    ranks = [jnp.sum(jnp.where(hot, rank_e, 0.0), axis=0, keepdims=True) for hot in hots]
    rank_ref[...] = jnp.concatenate(ranks, axis=0).astype(jnp.int32)
    carry_new = carry + jnp.sum(sel_f, axis=1, keepdims=True)
    carry_ref[...] = carry_new
    cnt_ref[...] = carry_new.astype(jnp.int32)


def _out_route(x2, ym, p, pool_w, pool_s, w_out, g2, wr_t, br, seq):
    T = x2.shape[0]
    TM = TM_PROJ
    nt = T // TM
    u_blk = N_MAIN // POOL_WIDTH - 1
    halo_per_tile = TM // HALO
    tok_spec = pl.BlockSpec((TOP_K, TM), lambda i: (0, i))
    return pl.pallas_call(
        functools.partial(_out_route_kernel, seq),
        grid=(nt,),
        in_specs=[
            pl.BlockSpec((TM, D_MODEL), lambda i: (i, 0)),
            pl.BlockSpec((TM, MLSTM_WIDTH), lambda i: (i, 0)),
            pl.BlockSpec((TM, POOL_WIDTH), lambda i: (i, u_blk)),
            pl.BlockSpec((HALO, POOL_WIDTH),
                         lambda i: (jnp.maximum(i * halo_per_tile - 1, 0), u_blk)),
            pl.BlockSpec((len(POOL_WINDOWS), POOL_GROUP_DIM, POOL_GROUP_DIM), lambda i: (0, 0, 0)),
            pl.BlockSpec((1, POOL_WIDTH), lambda i: (0, 0)),
            pl.BlockSpec((D_MODEL, D_MODEL), lambda i: (0, 0)),
            pl.BlockSpec((1, D_MODEL), lambda i: (0, 0)),
            pl.BlockSpec((N_EXPERTS, D_MODEL), lambda i: (0, 0)),
            pl.BlockSpec((N_EXPERTS, 1), lambda i: (0, 0)),
        ],
        out_specs=[
            pl.BlockSpec((TM, D_MODEL), lambda i: (i, 0)),
            pl.BlockSpec((TM * SLAB, LANES), lambda i: (i, 0)),
            tok_spec, tok_spec, tok_spec,
            pl.BlockSpec((N_EXPERTS, LANES), lambda i: (0, 0)),
        ],
        out_shape=[
            jax.ShapeDtypeStruct((T, D_MODEL), jnp.float32),
            jax.ShapeDtypeStruct((T * SLAB, LANES), jnp.float32),
            jax.ShapeDtypeStruct((TOP_K, T), jnp.int32),
            jax.ShapeDtypeStruct((TOP_K, T), jnp.float32),
            jax.ShapeDtypeStruct((TOP_K, T), jnp.int32),
            jax.ShapeDtypeStruct((N_EXPERTS, LANES), jnp.int32),
        ],
        scratch_shapes=[
            pltpu.VMEM((HALO + TM, POOL_WIDTH), jnp.float32),
            pltpu.VMEM((N_EXPERTS, LANES), jnp.float32),
        ],
        compiler_params=pltpu.CompilerParams(
            dimension_semantics=("arbitrary",), vmem_limit_bytes=VMEM_LIMIT),
        name="out_route",
    )(x2, ym, p, p, pool_w, pool_s, w_out, g2, wr_t, br)


def _plan_kernel(n_assign, dest_ref, fill_ref, slot_ref, sem):
    cp = pltpu.make_async_copy(fill_ref, slot_ref, sem)
    cp.start()
    cp.wait()

    def body(i, carry):
        base = i * PLAN_UNROLL
        for j in range(PLAN_UNROLL):
            slot_ref[dest_ref[base + j]] = base + j
        return carry

    lax.fori_loop(0, n_assign // PLAN_UNROLL, body, 0)


def _plan(dest_flat, fill):
    n_assign = dest_flat.shape[0]
    return pl.pallas_call(
        functools.partial(_plan_kernel, n_assign),
        in_specs=[
            pl.BlockSpec(memory_space=pltpu.SMEM),
            pl.BlockSpec(memory_space=pl.ANY),
        ],
        out_specs=pl.BlockSpec(memory_space=pltpu.SMEM),
        out_shape=jax.ShapeDtypeStruct(fill.shape, jnp.int32),
        scratch_shapes=[pltpu.SemaphoreType.DMA(())],
        name="plan",
    )(dest_flat, fill)


def _expert_kernel(n_tok, bs_ref, slot_ref, h2_ref, wg_ref, bg_ref, wu_ref, bu_ref, wd_ref, bd_ref,
                   yt_ref, xg0_ref, xg1_ref, xg2_ref, ys0_ref, ys1_ref, ys2_ref,
                   wgb_ref, wub_ref, wdb_ref, gsem, ssem):
    TM = TM_EXPERT
    ROWS = TM * SLAB
    e = pl.program_id(0)
    n_total = bs_ref[N_EXPERTS]
    xg = (xg0_ref, xg1_ref, xg2_ref)
    ys = (ys0_ref, ys1_ref, ys2_ref)

    def token_of(a):
        return a & (n_tok - 1) if n_tok & (n_tok - 1) == 0 else lax.rem(a, n_tok)

    def start_gather(blk, par):
        base = (blk + 1) * TM
        for r in range(TM):
            t = token_of(slot_ref[base + r])
            pltpu.make_async_copy(h2_ref.at[pl.ds(pl.multiple_of(t * SLAB, SLAB), SLAB), :],
                                  xg[par].at[pl.ds(r * SLAB, SLAB), :], gsem.at[par]).start()

    def wait_gather(par):
        pltpu.make_async_copy(h2_ref.at[pl.ds(0, ROWS), :], xg[0], gsem.at[par]).wait()

    def start_scatter(blk, par):
        base = (blk + 1) * TM
        for r in range(TM):
            a = slot_ref[base + r]
            pltpu.make_async_copy(ys[par].at[pl.ds(r * SLAB, SLAB), :],
                                  yt_ref.at[pl.ds(pl.multiple_of(a * SLAB, SLAB), SLAB), :],
                                  ssem.at[par]).start()

    def wait_scatter(par):
        pltpu.make_async_copy(ys[0], yt_ref.at[pl.ds(0, ROWS), :], ssem.at[par]).wait()

    @pl.when(e == 0)
    def _():
        start_gather(0, 0)
        start_gather(1, 1)
        for par in range(NBUF):
            ys[par][...] = jnp.zeros_like(ys[par])
            dump = yt_ref.at[pl.ds((n_tok * TOP_K + par * TM) * SLAB, ROWS), :]
            cp = pltpu.make_async_copy(ys[par], dump, ssem.at[par])
            cp.start()
            cp.wait()

    wgb_ref[...] = wg_ref[0].astype(jnp.bfloat16)
    wub_ref[...] = wu_ref[0].astype(jnp.bfloat16)
    wdb_ref[...] = wd_ref[0].astype(jnp.bfloat16)

    def block_step(g, par):
        nxt2 = (par + 2) % NBUF
        wait_gather(par)

        @pl.when(g >= 2)
        def _():
            wait_scatter(par)

        start_gather(g + 2, nxt2)
        start_scatter(g - 1, nxt2)
        x = jnp.concatenate(
            [xg[par][pl.ds(s, TM, stride=SLAB), :].astype(jnp.bfloat16) for s in range(SLAB)],
            axis=1)
        gate = jnp.dot(x, wgb_ref[...], preferred_element_type=jnp.float32) + bg_ref[0]
        up = jnp.dot(x, wub_ref[...], preferred_element_type=jnp.float32) + bu_ref[0]
        gate = jnp.minimum(gate, SWIGLU_LIMIT)
        up = jnp.clip(up, -SWIGLU_LIMIT, SWIGLU_LIMIT)
        glu = gate * _sigmoid(SWIGLU_ALPHA * gate)
        act = (glu * (up + 1.0)).astype(jnp.bfloat16)
        y = jnp.dot(act, wdb_ref[...], preferred_element_type=jnp.float32) + bd_ref[0]
        for s in range(SLAB):
            ys[par][pl.ds(s, TM, stride=SLAB), :] = y[:, s * LANES:(s + 1) * LANES]

    def body(g, carry):
        for par in range(NBUF):
            pl.when(g % NBUF == par)(functools.partial(block_step, g, par))
        return carry

    lax.fori_loop(bs_ref[e], bs_ref[e + 1], body, 0)

    @pl.when(e == N_EXPERTS - 1)
    def _():
        g = n_total
        for par in range(NBUF):
            @pl.when((g - 1) % NBUF == par)
            def _():
                start_scatter(g - 1, par)
        wait_gather(g % NBUF)
        wait_gather((g + 1) % NBUF)
        wait_scatter((g - 1) % NBUF)
        wait_scatter((g + 1) % NBUF)

        @pl.when(g >= 2)
        def _():
            wait_scatter(g % NBUF)


def _experts(block_start, slot_buf, h2_slab, w_gate, b_gate, w_up, b_up, w_down, b_down, n_tok):
    TM = TM_EXPERT
    n_assign = n_tok * TOP_K
    w_spec = pl.BlockSpec((1, D_MODEL, D_FF), lambda e, bs, sl: (e, 0, 0))
    bias_spec = pl.BlockSpec((1, 1, D_FF), lambda e, bs, sl: (e, 0, 0))
    buf = pltpu.VMEM((TM * SLAB, LANES), jnp.float32)
    grid_spec = pltpu.PrefetchScalarGridSpec(
        num_scalar_prefetch=2,
        grid=(N_EXPERTS,),
        in_specs=[
            pl.BlockSpec(memory_space=pl.ANY),
            w_spec, bias_spec, w_spec, bias_spec, w_spec, bias_spec,
        ],
        out_specs=pl.BlockSpec(memory_space=pl.ANY),
        scratch_shapes=[
            buf, buf, buf, buf, buf, buf,
            pltpu.VMEM((D_MODEL, D_FF), jnp.bfloat16),
            pltpu.VMEM((D_MODEL, D_FF), jnp.bfloat16),
            pltpu.VMEM((D_FF, D_MODEL), jnp.bfloat16),
            pltpu.SemaphoreType.DMA((NBUF,)),
            pltpu.SemaphoreType.DMA((NBUF,)),
        ],
    )
    return pl.pallas_call(
        functools.partial(_expert_kernel, n_tok),
        grid_spec=grid_spec,
        out_shape=jax.ShapeDtypeStruct(((n_assign + NBUF * TM) * SLAB, LANES), jnp.float32),
        compiler_params=pltpu.CompilerParams(
            dimension_semantics=("arbitrary",), vmem_limit_bytes=VMEM_LIMIT),
        name="experts",
    )(block_start, slot_buf, h2_slab, w_gate, b_gate, w_up, b_up, w_down, b_down)


def _combine_kernel(normalize, x1_ref, y0_ref, y1_ref, y2_ref, y3_ref, gate_ref, g_ref, o_ref):
    TM = TM_PROJ
    gates = jnp.concatenate([gate_ref[...], jnp.zeros((8 - TOP_K, TM), jnp.float32)], axis=0)
    g_cols = jnp.transpose(gates)
    g_bc = [jnp.broadcast_to(g_cols[:, k:k + 1], (TM, LANES)) for k in range(TOP_K)]
    parts = []
    ssq = jnp.zeros((TM, LANES), jnp.float32)
    for s in range(SLAB):
        acc = x1_ref[:, s * LANES:(s + 1) * LANES]
        for k, y_ref in enumerate((y0_ref, y1_ref, y2_ref, y3_ref)):
            acc = acc + g_bc[k] * y_ref[pl.ds(s, TM, stride=SLAB), :]
        parts.append(acc)
        ssq = ssq + acc * acc
    if normalize:
        inv = lax.rsqrt(jnp.sum(ssq, axis=-1, keepdims=True) * (1.0 / D_MODEL) + EPS)
        for s in range(SLAB):
            o_ref[:, s * LANES:(s + 1) * LANES] = parts[s] * inv * g_ref[:, s * LANES:(s + 1) * LANES]
    else:
        for s in range(SLAB):
            o_ref[:, s * LANES:(s + 1) * LANES] = parts[s]


def _combine(x1, y_tok, gate_t, gf, normalize):
    T = x1.shape[0]
    TM = TM_PROJ
    nt = T // TM

    def y_spec(k):
        return pl.BlockSpec((TM * SLAB, LANES), lambda i: (k * nt + i, 0))

    return pl.pallas_call(
        functools.partial(_combine_kernel, normalize),
        grid=(nt,),
        in_specs=[
            pl.BlockSpec((TM, D_MODEL), lambda i: (i, 0)),
            y_spec(0), y_spec(1), y_spec(2), y_spec(3),
            pl.BlockSpec((TOP_K, TM), lambda i: (0, i)),
            pl.BlockSpec((1, D_MODEL), lambda i: (0, 0)),
        ],
        out_specs=pl.BlockSpec((TM, D_MODEL), lambda i: (i, 0)),
        out_shape=jax.ShapeDtypeStruct((T, D_MODEL), jnp.float32),
        compiler_params=pltpu.CompilerParams(
            dimension_semantics=("parallel",), vmem_limit_bytes=VMEM_LIMIT),
        name="combine",
    )(x1, y_tok, y_tok, y_tok, y_tok, gate_t, gf)


def kernel(x, norm1_g, w_in, ig_b, fg_b, conv_w, head_norm_g, pool_w, pool_scale, w_out, norm2_g,
           w_router, b_router, w_gate, b_gate, w_up, b_up, w_down, b_down, normf_g):
    B, S, D = x.shape
    T = B * S
    depth = norm1_g.shape[0]
    W = MLSTM_WIDTH
    f32, bf16 = jnp.float32, jnp.bfloat16

    L = CHUNK
    t_l = lax.broadcasted_iota(jnp.int32, (L, L), 0)
    t_r = lax.broadcasted_iota(jnp.int32, (L, L), 1)
    tri = (t_r <= t_l).astype(f32)
    shifts = jnp.stack([(t_l - t_r == CONV_WIDTH - 1 - j).astype(bf16)
                        for j in range(CONV_WIDTH - 1)])
    h_t = lax.broadcasted_iota(jnp.int32, (8, HALO), 0)
    h_r = lax.broadcasted_iota(jnp.int32, (8, HALO), 1)
    halo_shifts = jnp.stack([(h_r - HALO - h_t == -(CONV_WIDTH - 1 - j)).astype(bf16)
                             for j in range(CONV_WIDTH - 1)])

    n_assign = T * TOP_K
    n_blocks = -(-n_assign // TM_EXPERT) + N_EXPERTS
    n_rows = n_blocks * TM_EXPERT
    n_table = n_rows + 3 * TM_EXPERT
    fill = n_assign + ((jnp.arange(n_table, dtype=jnp.int32) + (NBUF - 1) * TM_EXPERT)
                       % (NBUF * TM_EXPERT))
    x2 = x.reshape(T, D)
    for l in range(depth):
        w = w_in[l]
        w_main = jnp.concatenate([w[:, :4 * W], w[:, 4 * W + N_GATES:]], axis=1).astype(bf16)
        wg_t = jnp.zeros((BF16_SUBLANES, D), bf16).at[:N_GATES].set(
            w[:, 4 * W:4 * W + N_GATES].T.astype(bf16))
        p, gates_t = _in_proj(x2, norm1_g[l][None, :], w_main, wg_t)

        gate_b = jnp.concatenate([ig_b[l], fg_b[l]])[:, None].astype(f32)
        gates_b = gates_t.reshape(N_GATES, B, S).transpose(1, 0, 2)
        ym = _mlstm(p.reshape(B, S, N_MAIN), gates_b, conv_w[l].astype(f32), gate_b,
                    head_norm_g[l][None, :], tri, shifts, halo_shifts).reshape(T, W)

        x1, h2, idx_t, gate_t, rank_t, cnt = _out_route(
            x2, ym, p, pool_w[l].astype(bf16), pool_scale[l][None, :], w_out[l].astype(bf16),
            norm2_g[l][None, :], w_router[l].T.astype(bf16), b_router[l][:, None], S)

        counts = cnt[:, 0]
        padded = ((counts + TM_EXPERT - 1) // TM_EXPERT) * TM_EXPERT
        padded_end = jnp.cumsum(padded)
        padded_start = padded_end - padded
        expert_ids = jnp.arange(N_EXPERTS, dtype=jnp.int32)[:, None, None]
        start_of = jnp.sum(jnp.where(idx_t[None] == expert_ids, padded_start[:, None, None], 0), axis=0)
        dest = start_of + rank_t
        block_start = jnp.concatenate(
            [jnp.zeros((1,), jnp.int32), (padded_end // TM_EXPERT).astype(jnp.int32)])

        slot_buf = _plan(dest.reshape(-1) + TM_EXPERT, fill)
        y_tok = _experts(block_start, slot_buf, h2, w_gate[l], b_gate[l][:, None, :],
                         w_up[l], b_up[l][:, None, :], w_down[l], b_down[l][:, None, :], T)
        last = l + 1 == depth
        x2 = _combine(x1, y_tok, gate_t, normf_g[None, :], last)
    return x2.reshape(B, S, D)
```

```python
import functools

import jax
import jax.numpy as jnp
from jax import lax
from jax.experimental import pallas as pl
from jax.experimental.pallas import tpu as pltpu

D_MODEL = 1024
MLSTM_WIDTH = 512
MLSTM_HEADS = 4
HEAD_DIM = 128
CONV_WIDTH = 4
POOL_WIDTH = 512
POOL_WINDOWS = (2, 4, 8, 16)
POOL_GROUP_DIM = 128
N_EXPERTS = 32
TOP_K = 4
D_FF = 1024
SWIGLU_LIMIT = 7.0
SWIGLU_ALPHA = 1.702
EPS = 1e-5

N_MAIN = 4 * MLSTM_WIDTH + POOL_WIDTH
N_GATES = 2 * MLSTM_HEADS

LANES = 128
BF16_SUBLANES = 16
VMEM_LIMIT = 56 * 1024 * 1024

TM_PROJ = 512
CHUNK = 256
MLSTM_BATCH = 2
HALO = 16
TM_EXPERT = 256
NBUF = 3
SLAB = D_MODEL // LANES
YSLAB = SLAB // 2
PLAN_UNROLL = 16

NT_DIMS = (((1,), (1,)), ((), ()))


def _sigmoid(x):
    return 1.0 / (1.0 + jnp.exp(-x))


def _in_proj_kernel(x_ref, g_ref, w_ref, wgt_ref, p_ref, gt_ref):
    x = x_ref[...]
    h = x * lax.rsqrt(jnp.mean(x * x, axis=-1, keepdims=True) + EPS) * g_ref[...]
    hb = h.astype(jnp.bfloat16)
    p_ref[...] = jnp.dot(hb, w_ref[...], preferred_element_type=jnp.float32).astype(p_ref.dtype)
    gt = lax.dot_general(wgt_ref[...], hb, NT_DIMS, preferred_element_type=jnp.float32)
    gt_ref[...] = gt[:N_GATES]


def _in_proj(x2, g1, w_main, wg_t):
    T = x2.shape[0]
    return pl.pallas_call(
        _in_proj_kernel,
        grid=(T // TM_PROJ,),
        in_specs=[
            pl.BlockSpec((TM_PROJ, D_MODEL), lambda i: (i, 0)),
            pl.BlockSpec((1, D_MODEL), lambda i: (0, 0)),
            pl.BlockSpec((D_MODEL, N_MAIN), lambda i: (0, 0)),
            pl.BlockSpec((BF16_SUBLANES, D_MODEL), lambda i: (0, 0)),
        ],
        out_specs=[
            pl.BlockSpec((TM_PROJ, N_MAIN), lambda i: (i, 0)),
            pl.BlockSpec((N_GATES, TM_PROJ), lambda i: (0, i)),
        ],
        out_shape=[
            jax.ShapeDtypeStruct((T, N_MAIN), jnp.bfloat16),
            jax.ShapeDtypeStruct((N_GATES, T), jnp.float32),
        ],
        compiler_params=pltpu.CompilerParams(
            dimension_semantics=("parallel",), vmem_limit_bytes=VMEM_LIMIT),
        name="in_proj",
    )(x2, g1, w_main, wg_t)


def _mlstm_kernel(qk_ref, qkp_ref, v_ref, o_ref, gt_ref, convw_ref, gb_ref, hng_ref,
                  tri_ref, shift_ref, hshift_ref, y_ref, cn_ref, m_ref):
    L = CHUNK
    c = pl.program_id(1)

    @pl.when(c == 0)
    def _():
        cn_ref[...] = jnp.zeros_like(cn_ref)
        m_ref[...] = jnp.zeros_like(m_ref)

    row_id = lax.broadcasted_iota(jnp.int32, (L, L), 0)
    col_id = lax.broadcasted_iota(jnp.int32, (L, L), 1)
    causal = col_id <= row_id
    ones_blk = jnp.ones((L, HEAD_DIM), jnp.bfloat16)
    lane = lax.broadcasted_iota(jnp.int32, (MLSTM_HEADS, L), 1)

    gate_terms = []
    for bb in range(MLSTM_BATCH):
        gt = gt_ref[bb] + gb_ref[...]
        f = gt[MLSTM_HEADS:]
        lf = jnp.minimum(f, 0.0) - jnp.log(1.0 + jnp.exp(-jnp.abs(f)))
        ig = gt[:MLSTM_HEADS]
        b_rows = lax.dot_general(lf, tri_ref[...], NT_DIMS, precision=lax.Precision.HIGHEST,
                                 preferred_element_type=jnp.float32)
        c_rows = ig - b_rows
        cm_rows = c_rows
        d = 1
        while d < L:
            cm_rows = jnp.maximum(
                cm_rows, jnp.where(lane >= d, pltpu.roll(cm_rows, d, axis=1), -jnp.inf))
            d *= 2
        gate_terms.append((b_rows, c_rows, cm_rows))

    conv_terms = []
    for bb in range(MLSTM_BATCH):
        x_cur = qk_ref[bb]
        x_prev = jnp.where(c > 0, qkp_ref[bb], jnp.zeros((HALO, 2 * MLSTM_WIDTH), jnp.bfloat16))
        acc = convw_ref[CONV_WIDTH - 1:CONV_WIDTH, :] * x_cur.astype(jnp.float32)
        for j in range(CONV_WIDTH - 1):
            sh = jnp.dot(shift_ref[j], x_cur, preferred_element_type=jnp.float32)
            top = sh[:8] + jnp.dot(hshift_ref[j], x_prev, preferred_element_type=jnp.float32)
            sh = jnp.concatenate([top, sh[8:]], axis=0)
            acc = acc + convw_ref[j:j + 1, :] * sh
        qk = acc * _sigmoid(acc)
        q_all = qk[:, :MLSTM_WIDTH].astype(jnp.bfloat16)
        k_t = jnp.transpose(qk[:, MLSTM_WIDTH:] * (HEAD_DIM ** -0.5))
        conv_terms.append((q_all, k_t))

    for bb in range(MLSTM_BATCH):
        b_rows, c_rows, cm_rows = gate_terms[bb]
        q_all, k_t = conv_terms[bb]
        m_in4 = jnp.concatenate(
            [m_ref[bb * MLSTM_HEADS + h][0:1, 0:1] for h in range(MLSTM_HEADS)], axis=0)
        mx_rows = jnp.maximum(cm_rows, m_in4)
        inter_rows = jnp.exp(m_in4 - mx_rows)
        einv_rows = jnp.exp(-(b_rows + mx_rows))
        fac_t = jnp.transpose(jnp.concatenate(
            [mx_rows, inter_rows, einv_rows, jnp.zeros_like(mx_rows)], axis=0))

        for h in range(MLSTM_HEADS):
            lo = h * HEAD_DIM
            st = bb * MLSTM_HEADS + h
            q = q_all[:, lo:lo + HEAD_DIM]
            kt = k_t[lo:lo + HEAD_DIM, :]
            v_ext = jnp.concatenate([v_ref[bb, :, lo:lo + HEAD_DIM], ones_blk], axis=1)
            mx_col = fac_t[:, h:h + 1]
            inter_col = fac_t[:, MLSTM_HEADS + h:MLSTM_HEADS + h + 1]
            einv_col = fac_t[:, 2 * MLSTM_HEADS + h:2 * MLSTM_HEADS + h + 1]
            c_row = c_rows[h:h + 1, :]
            b_tot = b_rows[h:h + 1, L - 1:L]
            cm_tot = cm_rows[h:h + 1, L - 1:L]
            m_in = m_ref[st][0:1, 0:1]
            cn = cn_ref[st]

            s_qk = jnp.dot(q, kt.astype(jnp.bfloat16), preferred_element_type=jnp.float32)
            s = (s_qk * jnp.exp(jnp.where(causal, c_row - mx_col, -jnp.inf))).astype(jnp.bfloat16)
            num = (jnp.dot(s, v_ext, preferred_element_type=jnp.float32)
                   + inter_col * jnp.dot(q, cn.astype(jnp.bfloat16),
                                         preferred_element_type=jnp.float32))
            den = num[:, HEAD_DIM:]
            hh = num[:, :HEAD_DIM] / jnp.maximum(jnp.abs(den), einv_col)

            mu = jnp.mean(hh, axis=-1, keepdims=True)
            dv = hh - mu
            var = jnp.mean(dv * dv, axis=-1, keepdims=True)
            hn = dv * lax.rsqrt(var + EPS) * hng_ref[:, lo:lo + HEAD_DIM]
            og = _sigmoid(o_ref[bb, :, lo:lo + HEAD_DIM].astype(jnp.float32))
            y_ref[bb, :, lo:lo + HEAD_DIM] = (og * hn).astype(y_ref.dtype)

            m_loc = b_tot + cm_tot
            kw_t = (kt * jnp.exp(c_row - cm_tot)).astype(jnp.bfloat16)
            c_loc = jnp.dot(kw_t, v_ext, preferred_element_type=jnp.float32)
            m_new = jnp.maximum(b_tot + m_in, m_loc)
            s_old = jnp.exp(b_tot + m_in - m_new)
            s_loc = jnp.exp(m_loc - m_new)
            cn_ref[st] = s_old * cn + s_loc * c_loc
            m_ref[st] = jnp.broadcast_to(m_new, m_ref.shape[1:])


def _mlstm(p3, gates_b, conv_w, gate_b, hn_g, tri, shifts, halo_shifts):
    batch, seq, _ = p3.shape
    L = CHUNK
    BB = MLSTM_BATCH
    halo_per_chunk = L // HALO
    return pl.pallas_call(
        _mlstm_kernel,
        grid=(batch // BB, seq // L),
        in_specs=[
            pl.BlockSpec((BB, L, 2 * MLSTM_WIDTH), lambda bi, ci: (bi, ci, 0)),
            pl.BlockSpec((BB, HALO, 2 * MLSTM_WIDTH),
                         lambda bi, ci: (bi, jnp.maximum(ci * halo_per_chunk - 1, 0), 0)),
            pl.BlockSpec((BB, L, MLSTM_WIDTH), lambda bi, ci: (bi, ci, 2)),
            pl.BlockSpec((BB, L, MLSTM_WIDTH), lambda bi, ci: (bi, ci, 3)),
            pl.BlockSpec((BB, N_GATES, L), lambda bi, ci: (bi, 0, ci)),
            pl.BlockSpec((CONV_WIDTH, 2 * MLSTM_WIDTH), lambda bi, ci: (0, 0)),
            pl.BlockSpec((N_GATES, 1), lambda bi, ci: (0, 0)),
            pl.BlockSpec((1, MLSTM_WIDTH), lambda bi, ci: (0, 0)),
            pl.BlockSpec((L, L), lambda bi, ci: (0, 0)),
            pl.BlockSpec((CONV_WIDTH - 1, L, L), lambda bi, ci: (0, 0, 0)),
            pl.BlockSpec((CONV_WIDTH - 1, 8, HALO), lambda bi, ci: (0, 0, 0)),
        ],
        out_specs=pl.BlockSpec((BB, L, MLSTM_WIDTH), lambda bi, ci: (bi, ci, 0)),
        out_shape=jax.ShapeDtypeStruct((batch, seq, MLSTM_WIDTH), jnp.bfloat16),
        scratch_shapes=[
            pltpu.VMEM((BB * MLSTM_HEADS, HEAD_DIM, 2 * HEAD_DIM), jnp.float32),
            pltpu.VMEM((BB * MLSTM_HEADS, 8, LANES), jnp.float32),
        ],
        compiler_params=pltpu.CompilerParams(
            dimension_semantics=("parallel", "arbitrary"), vmem_limit_bytes=VMEM_LIMIT),
        name="mlstm",
    )(p3, p3, p3, p3, gates_b, conv_w, gate_b, hn_g, tri, shifts, halo_shifts)


def _out_route_kernel(seq, x_ref, ym_ref, u_ref, up_ref, pw_ref, ps_ref, wo_ref, g2_ref,
                      wrt_ref, br_ref, x1_ref, h2_ref, idx_ref, gate_ref, rank_ref, cnt_ref,
                      ubuf_ref, carry_ref):
    TM = TM_PROJ
    i = pl.program_id(0)

    @pl.when(i == 0)
    def _():
        carry_ref[...] = jnp.zeros_like(carry_ref)

    pos0 = (i * TM) % seq
    ubuf_ref[0:HALO, :] = jnp.where(pos0 > 0, up_ref[...].astype(jnp.float32), 0.0)
    ubuf_ref[HALO:HALO + TM, :] = u_ref[...].astype(jnp.float32)
    pos = (pos0 + lax.broadcasted_iota(jnp.int32, (TM, 1), 0) + 1).astype(jnp.float32)
    mixed = []
    for gi, w in enumerate(POOL_WINDOWS):
        lo = gi * POOL_GROUP_DIM
        tok = ubuf_ref[HALO:HALO + TM, lo:lo + POOL_GROUP_DIM]
        acc = tok
        for j in range(1, w):
            acc = acc + ubuf_ref[HALO - j:HALO - j + TM, lo:lo + POOL_GROUP_DIM]
        pooled = acc / jnp.minimum(pos, float(w)) - tok
        mg = jnp.dot(pooled.astype(jnp.bfloat16), pw_ref[gi], preferred_element_type=jnp.float32)
        mixed.append((mg * ps_ref[:, lo:lo + POOL_GROUP_DIM]).astype(jnp.bfloat16))
    y_cat = jnp.concatenate([ym_ref[...]] + mixed, axis=1)

    x1 = x_ref[...] + jnp.dot(y_cat, wo_ref[...], preferred_element_type=jnp.float32)
    x1_ref[...] = x1
    h2 = x1 * lax.rsqrt(jnp.mean(x1 * x1, axis=-1, keepdims=True) + EPS) * g2_ref[...]
    h2b = h2.astype(jnp.bfloat16)
    for s in range(SLAB):
        h2_ref[pl.ds(s, TM, stride=SLAB), :] = h2[:, s * LANES:(s + 1) * LANES]

    logits = lax.dot_general(wrt_ref[...], h2b, NT_DIMS,
                             preferred_element_type=jnp.float32) + br_ref[...]
    e_id = lax.broadcasted_iota(jnp.int32, (N_EXPERTS, TM), 0).astype(jnp.float32)
    work = logits
    vals, ids, hots = [], [], []
    for _ in range(TOP_K):
        mk = jnp.max(work, axis=0, keepdims=True)
        ik = jnp.min(jnp.where(work == mk, e_id, float(N_EXPERTS)), axis=0, keepdims=True)
        hot = e_id == ik
        work = jnp.where(hot, -jnp.inf, work)
        vals.append(mk)
        ids.append(ik)
        hots.append(hot)
    ex = [jnp.exp(vk - vals[0]) for vk in vals]
    denom = ex[0] + ex[1] + ex[2] + ex[3]
    gate_ref[...] = jnp.concatenate([e / denom for e in ex], axis=0)
    idx_ref[...] = jnp.concatenate(ids, axis=0).astype(jnp.int32)

    sel_f = sum(jnp.where(hot, 1.0, 0.0) for hot in hots)
    t_row = lax.broadcasted_iota(jnp.int32, (TM, TM), 0)
    t_col = lax.broadcasted_iota(jnp.int32, (TM, TM), 1)
    before = jnp.where(t_row < t_col, 1.0, 0.0).astype(jnp.bfloat16)
    prefix = jnp.dot(sel_f.astype(jnp.bfloat16), before, preferred_element_type=jnp.float32)
    carry = carry_ref[...]
    rank_e = carry[:, 0:1] + prefix
    ranks = [jnp.sum(jnp.where(hot, rank_e, 0.0), axis=0, keepdims=True) for hot in hots]
    rank_ref[...] = jnp.concatenate(ranks, axis=0).astype(jnp.int32)
    carry_new = carry + jnp.sum(sel_f, axis=1, keepdims=True)
    carry_ref[...] = carry_new
    cnt_ref[...] = carry_new.astype(jnp.int32)


def _out_route(x2, ym, p, pool_w, pool_s, w_out, g2, wr_t, br, seq):
    T = x2.shape[0]
    TM = TM_PROJ
    nt = T // TM
    u_blk = N_MAIN // POOL_WIDTH - 1
    halo_per_tile = TM // HALO
    tok_spec = pl.BlockSpec((TOP_K, TM), lambda i: (0, i))
    return pl.pallas_call(
        functools.partial(_out_route_kernel, seq),
        grid=(nt,),
        in_specs=[
            pl.BlockSpec((TM, D_MODEL), lambda i: (i, 0)),
            pl.BlockSpec((TM, MLSTM_WIDTH), lambda i: (i, 0)),
            pl.BlockSpec((TM, POOL_WIDTH), lambda i: (i, u_blk)),
            pl.BlockSpec((HALO, POOL_WIDTH),
                         lambda i: (jnp.maximum(i * halo_per_tile - 1, 0), u_blk)),
            pl.BlockSpec((len(POOL_WINDOWS), POOL_GROUP_DIM, POOL_GROUP_DIM), lambda i: (0, 0, 0)),
            pl.BlockSpec((1, POOL_WIDTH), lambda i: (0, 0)),
            pl.BlockSpec((D_MODEL, D_MODEL), lambda i: (0, 0)),
            pl.BlockSpec((1, D_MODEL), lambda i: (0, 0)),
            pl.BlockSpec((N_EXPERTS, D_MODEL), lambda i: (0, 0)),
            pl.BlockSpec((N_EXPERTS, 1), lambda i: (0, 0)),
        ],
        out_specs=[
            pl.BlockSpec((TM, D_MODEL), lambda i: (i, 0)),
            pl.BlockSpec((TM * SLAB, LANES), lambda i: (i, 0)),
            tok_spec, tok_spec, tok_spec,
            pl.BlockSpec((N_EXPERTS, LANES), lambda i: (0, 0)),
        ],
        out_shape=[
            jax.ShapeDtypeStruct((T, D_MODEL), jnp.float32),
            jax.ShapeDtypeStruct((T * SLAB, LANES), jnp.float32),
            jax.ShapeDtypeStruct((TOP_K, T), jnp.int32),
            jax.ShapeDtypeStruct((TOP_K, T), jnp.float32),
            jax.ShapeDtypeStruct((TOP_K, T), jnp.int32),
            jax.ShapeDtypeStruct((N_EXPERTS, LANES), jnp.int32),
        ],
        scratch_shapes=[
            pltpu.VMEM((HALO + TM, POOL_WIDTH), jnp.float32),
            pltpu.VMEM((N_EXPERTS, LANES), jnp.float32),
        ],
        compiler_params=pltpu.CompilerParams(
            dimension_semantics=("arbitrary",), vmem_limit_bytes=VMEM_LIMIT),
        name="out_route",
    )(x2, ym, p, p, pool_w, pool_s, w_out, g2, wr_t, br)


def _plan_kernel(n_assign, dest_ref, fill_ref, slot_ref, sem):
    cp = pltpu.make_async_copy(fill_ref, slot_ref, sem)
    cp.start()
    cp.wait()

    def body(i, carry):
        base = i * PLAN_UNROLL
        for j in range(PLAN_UNROLL):
            slot_ref[dest_ref[base + j]] = base + j
        return carry

    lax.fori_loop(0, n_assign // PLAN_UNROLL, body, 0)


def _plan(dest_flat, fill):
    n_assign = dest_flat.shape[0]
    return pl.pallas_call(
        functools.partial(_plan_kernel, n_assign),
        in_specs=[
            pl.BlockSpec(memory_space=pltpu.SMEM),
            pl.BlockSpec(memory_space=pl.ANY),
        ],
        out_specs=pl.BlockSpec(memory_space=pltpu.SMEM),
        out_shape=jax.ShapeDtypeStruct(fill.shape, jnp.int32),
        scratch_shapes=[pltpu.SemaphoreType.DMA(())],
        name="plan",
    )(dest_flat, fill)


def _expert_kernel(n_tok, bs_ref, slot_ref, h2_ref, wg_ref, bg_ref, wu_ref, bu_ref, wd_ref, bd_ref,
                   yt_ref, xg0_ref, xg1_ref, xg2_ref, ys0_ref, ys1_ref, ys2_ref,
                   wgb_ref, wub_ref, wdb_ref, gsem, ssem):
    TM = TM_EXPERT
    ROWS = TM * SLAB
    YROWS = TM * YSLAB
    e = pl.program_id(0)
    n_total = bs_ref[N_EXPERTS]
    xg = (xg0_ref, xg1_ref, xg2_ref)
    ys = (ys0_ref, ys1_ref, ys2_ref)

    def token_of(a):
        return a & (n_tok - 1) if n_tok & (n_tok - 1) == 0 else lax.rem(a, n_tok)

    def start_gather(blk, par):
        base = (blk + 1) * TM
        for r in range(TM):
            t = token_of(slot_ref[base + r])
            pltpu.make_async_copy(h2_ref.at[pl.ds(pl.multiple_of(t * SLAB, SLAB), SLAB), :],
                                  xg[par].at[pl.ds(r * SLAB, SLAB), :], gsem.at[par]).start()

    def wait_gather(par):
        pltpu.make_async_copy(h2_ref.at[pl.ds(0, ROWS), :], xg[0], gsem.at[par]).wait()

    def start_scatter(blk, par):
        base = (blk + 1) * TM
        for r in range(TM):
            a = slot_ref[base + r]
            pltpu.make_async_copy(ys[par].at[pl.ds(r * YSLAB, YSLAB), :],
                                  yt_ref.at[pl.ds(pl.multiple_of(a * YSLAB, YSLAB), YSLAB), :],
                                  ssem.at[par]).start()

    def wait_scatter(par):
        pltpu.make_async_copy(ys[0], yt_ref.at[pl.ds(0, YROWS), :], ssem.at[par]).wait()

    @pl.when(e == 0)
    def _():
        start_gather(0, 0)
        start_gather(1, 1)
        for par in range(NBUF):
            ys[par][...] = jnp.zeros_like(ys[par])
            dump = yt_ref.at[pl.ds((n_tok * TOP_K + par * TM) * YSLAB, YROWS), :]
            cp = pltpu.make_async_copy(ys[par], dump, ssem.at[par])
            cp.start()
            cp.wait()

    wgb_ref[...] = wg_ref[0].astype(jnp.bfloat16)
    wub_ref[...] = wu_ref[0].astype(jnp.bfloat16)
    wdb_ref[...] = wd_ref[0].astype(jnp.bfloat16)

    def block_step(g, par):
        nxt2 = (par + 2) % NBUF
        wait_gather(par)

        @pl.when(g >= 2)
        def _():
            wait_scatter(par)

        start_gather(g + 2, nxt2)
        start_scatter(g - 1, nxt2)
        x = jnp.concatenate(
            [xg[par][pl.ds(s, TM, stride=SLAB), :].astype(jnp.bfloat16) for s in range(SLAB)],
            axis=1)
        gate = jnp.dot(x, wgb_ref[...], preferred_element_type=jnp.float32) + bg_ref[0]
        up = jnp.dot(x, wub_ref[...], preferred_element_type=jnp.float32) + bu_ref[0]
        gate = jnp.minimum(gate, SWIGLU_LIMIT)
        up = jnp.clip(up, -SWIGLU_LIMIT, SWIGLU_LIMIT)
        glu = gate * _sigmoid(SWIGLU_ALPHA * gate)
        act = (glu * (up + 1.0)).astype(jnp.bfloat16)
        y = jnp.dot(act, wdb_ref[...], preferred_element_type=jnp.float32) + bd_ref[0]
        lo = pltpu.bitcast(y[:, :D_MODEL // 2].astype(jnp.bfloat16).astype(jnp.float32), jnp.uint32)
        hi = pltpu.bitcast(y[:, D_MODEL // 2:].astype(jnp.bfloat16).astype(jnp.float32), jnp.uint32)
        packed = (lo >> 16) | (hi & jnp.uint32(0xFFFF0000))
        for s in range(YSLAB):
            ys[par][pl.ds(s, TM, stride=YSLAB), :] = packed[:, s * LANES:(s + 1) * LANES]

    def body(g, carry):
        for par in range(NBUF):
            pl.when(g % NBUF == par)(functools.partial(block_step, g, par))
        return carry

    lax.fori_loop(bs_ref[e], bs_ref[e + 1], body, 0)

    @pl.when(e == N_EXPERTS - 1)
    def _():
        g = n_total
        for par in range(NBUF):
            @pl.when((g - 1) % NBUF == par)
            def _():
                start_scatter(g - 1, par)
        wait_gather(g % NBUF)
        wait_gather((g + 1) % NBUF)
        wait_scatter((g - 1) % NBUF)
        wait_scatter((g + 1) % NBUF)

        @pl.when(g >= 2)
        def _():
            wait_scatter(g % NBUF)


def _experts(block_start, slot_buf, h2_slab, w_gate, b_gate, w_up, b_up, w_down, b_down, n_tok):
    TM = TM_EXPERT
    n_assign = n_tok * TOP_K
    w_spec = pl.BlockSpec((1, D_MODEL, D_FF), lambda e, bs, sl: (e, 0, 0))
    bias_spec = pl.BlockSpec((1, 1, D_FF), lambda e, bs, sl: (e, 0, 0))
    buf = pltpu.VMEM((TM * SLAB, LANES), jnp.float32)
    ybuf = pltpu.VMEM((TM * YSLAB, LANES), jnp.uint32)
    grid_spec = pltpu.PrefetchScalarGridSpec(
        num_scalar_prefetch=2,
        grid=(N_EXPERTS,),
        in_specs=[
            pl.BlockSpec(memory_space=pl.ANY),
            w_spec, bias_spec, w_spec, bias_spec, w_spec, bias_spec,
        ],
        out_specs=pl.BlockSpec(memory_space=pl.ANY),
        scratch_shapes=[
            buf, buf, buf, ybuf, ybuf, ybuf,
            pltpu.VMEM((D_MODEL, D_FF), jnp.bfloat16),
            pltpu.VMEM((D_MODEL, D_FF), jnp.bfloat16),
            pltpu.VMEM((D_FF, D_MODEL), jnp.bfloat16),
            pltpu.SemaphoreType.DMA((NBUF,)),
            pltpu.SemaphoreType.DMA((NBUF,)),
        ],
    )
    return pl.pallas_call(
        functools.partial(_expert_kernel, n_tok),
        grid_spec=grid_spec,
        out_shape=jax.ShapeDtypeStruct(((n_assign + NBUF * TM) * YSLAB, LANES), jnp.uint32),
        compiler_params=pltpu.CompilerParams(
            dimension_semantics=("arbitrary",), vmem_limit_bytes=VMEM_LIMIT),
        name="experts",
    )(block_start, slot_buf, h2_slab, w_gate, b_gate, w_up, b_up, w_down, b_down)


def _combine_kernel(normalize, x1_ref, y0_ref, y1_ref, y2_ref, y3_ref, gate_ref, g_ref, o_ref):
    TM = TM_PROJ
    gates = jnp.concatenate([gate_ref[...], jnp.zeros((8 - TOP_K, TM), jnp.float32)], axis=0)
    g_cols = jnp.transpose(gates)
    g_bc = [jnp.broadcast_to(g_cols[:, k:k + 1], (TM, LANES)) for k in range(TOP_K)]
    ssq = jnp.zeros((TM, LANES), jnp.float32)
    parts = [x1_ref[:, s * LANES:(s + 1) * LANES] for s in range(SLAB)]
    for s in range(YSLAB):
        for k, y_ref in enumerate((y0_ref, y1_ref, y2_ref, y3_ref)):
            w = y_ref[pl.ds(s, TM, stride=YSLAB), :]
            lo = pltpu.bitcast(w << 16, jnp.float32)
            hi = pltpu.bitcast(w & jnp.uint32(0xFFFF0000), jnp.float32)
            parts[s] = parts[s] + g_bc[k] * lo
            parts[YSLAB + s] = parts[YSLAB + s] + g_bc[k] * hi
    for acc in parts:
        ssq = ssq + acc * acc
    if normalize:
        inv = lax.rsqrt(jnp.sum(ssq, axis=-1, keepdims=True) * (1.0 / D_MODEL) + EPS)
        for s in range(SLAB):
            o_ref[:, s * LANES:(s + 1) * LANES] = parts[s] * inv * g_ref[:, s * LANES:(s + 1) * LANES]
    else:
        for s in range(SLAB):
            o_ref[:, s * LANES:(s + 1) * LANES] = parts[s]


def _combine(x1, y_tok, gate_t, gf, normalize):
    T = x1.shape[0]
    TM = TM_PROJ
    nt = T // TM

    def y_spec(k):
        return pl.BlockSpec((TM * YSLAB, LANES), lambda i: (k * nt + i, 0))

    return pl.pallas_call(
        functools.partial(_combine_kernel, normalize),
        grid=(nt,),
        in_specs=[
            pl.BlockSpec((TM, D_MODEL), lambda i: (i, 0)),
            y_spec(0), y_spec(1), y_spec(2), y_spec(3),
            pl.BlockSpec((TOP_K, TM), lambda i: (0, i)),
            pl.BlockSpec((1, D_MODEL), lambda i: (0, 0)),
        ],
        out_specs=pl.BlockSpec((TM, D_MODEL), lambda i: (i, 0)),
        out_shape=jax.ShapeDtypeStruct((T, D_MODEL), jnp.float32),
        compiler_params=pltpu.CompilerParams(
            dimension_semantics=("parallel",), vmem_limit_bytes=VMEM_LIMIT),
        name="combine",
    )(x1, y_tok, y_tok, y_tok, y_tok, gate_t, gf)


def kernel(x, norm1_g, w_in, ig_b, fg_b, conv_w, head_norm_g, pool_w, pool_scale, w_out, norm2_g,
           w_router, b_router, w_gate, b_gate, w_up, b_up, w_down, b_down, normf_g):
    B, S, D = x.shape
    T = B * S
    depth = norm1_g.shape[0]
    W = MLSTM_WIDTH
    f32, bf16 = jnp.float32, jnp.bfloat16

    L = CHUNK
    t_l = lax.broadcasted_iota(jnp.int32, (L, L), 0)
    t_r = lax.broadcasted_iota(jnp.int32, (L, L), 1)
    tri = (t_r <= t_l).astype(f32)
    shifts = jnp.stack([(t_l - t_r == CONV_WIDTH - 1 - j).astype(bf16)
                        for j in range(CONV_WIDTH - 1)])
    h_t = lax.broadcasted_iota(jnp.int32, (8, HALO), 0)
    h_r = lax.broadcasted_iota(jnp.int32, (8, HALO), 1)
    halo_shifts = jnp.stack([(h_r - HALO - h_t == -(CONV_WIDTH - 1 - j)).astype(bf16)
                             for j in range(CONV_WIDTH - 1)])

    n_assign = T * TOP_K
    n_blocks = -(-n_assign // TM_EXPERT) + N_EXPERTS
    n_rows = n_blocks * TM_EXPERT
    n_table = n_rows + 3 * TM_EXPERT
    fill = n_assign + ((jnp.arange(n_table, dtype=jnp.int32) + (NBUF - 1) * TM_EXPERT)
                       % (NBUF * TM_EXPERT))
    x2 = x.reshape(T, D)
    for l in range(depth):
        w = w_in[l]
        w_main = jnp.concatenate([w[:, :4 * W], w[:, 4 * W + N_GATES:]], axis=1).astype(bf16)
        wg_t = jnp.zeros((BF16_SUBLANES, D), bf16).at[:N_GATES].set(
            w[:, 4 * W:4 * W + N_GATES].T.astype(bf16))
        p, gates_t = _in_proj(x2, norm1_g[l][None, :], w_main, wg_t)

        gate_b = jnp.concatenate([ig_b[l], fg_b[l]])[:, None].astype(f32)
        gates_b = gates_t.reshape(N_GATES, B, S).transpose(1, 0, 2)
        ym = _mlstm(p.reshape(B, S, N_MAIN), gates_b, conv_w[l].astype(f32), gate_b,
                    head_norm_g[l][None, :], tri, shifts, halo_shifts).reshape(T, W)

        x1, h2, idx_t, gate_t, rank_t, cnt = _out_route(
            x2, ym, p, pool_w[l].astype(bf16), pool_scale[l][None, :], w_out[l].astype(bf16),
            norm2_g[l][None, :], w_router[l].T.astype(bf16), b_router[l][:, None], S)

        counts = cnt[:, 0]
        padded = ((counts + TM_EXPERT - 1) // TM_EXPERT) * TM_EXPERT
        padded_end = jnp.cumsum(padded)
        padded_start = padded_end - padded
        expert_ids = jnp.arange(N_EXPERTS, dtype=jnp.int32)[:, None, None]
        start_of = jnp.sum(jnp.where(idx_t[None] == expert_ids, padded_start[:, None, None], 0), axis=0)
        dest = start_of + rank_t
        block_start = jnp.concatenate(
            [jnp.zeros((1,), jnp.int32), (padded_end // TM_EXPERT).astype(jnp.int32)])

        slot_buf = _plan(dest.reshape(-1) + TM_EXPERT, fill)
        y_tok = _experts(block_start, slot_buf, h2, w_gate[l], b_gate[l][:, None, :],
                         w_up[l], b_up[l][:, None, :], w_down[l], b_down[l][:, None, :], T)
        last = l + 1 == depth
        x2 = _combine(x1, y_tok, gate_t, normf_g[None, :], last)
    return x2.reshape(B, S, D)
```

```python
import functools

import jax
import jax.numpy as jnp
from jax import lax
from jax.experimental import pallas as pl
from jax.experimental.pallas import tpu as pltpu
from jax.experimental.pallas import tpu_sc as plsc

D_MODEL = 1024
MLSTM_WIDTH = 512
MLSTM_HEADS = 4
HEAD_DIM = 128
CONV_WIDTH = 4
POOL_WIDTH = 512
POOL_WINDOWS = (2, 4, 8, 16)
POOL_GROUP_DIM = 128
N_EXPERTS = 32
TOP_K = 4
D_FF = 1024
SWIGLU_LIMIT = 7.0
SWIGLU_ALPHA = 1.702
EPS = 1e-5

N_MAIN = 4 * MLSTM_WIDTH + POOL_WIDTH
N_GATES = 2 * MLSTM_HEADS

LANES = 128
BF16_SUBLANES = 16
VMEM_LIMIT = 56 * 1024 * 1024

TM_PROJ = 512
CHUNK = 256
MLSTM_BATCH = 2
HALO = 16
TM_EXPERT = 256
NBUF = 3
SLAB = D_MODEL // LANES
YSLAB = SLAB // 2
PLAN_CHUNK = 8192
SC_LANES = 16

NT_DIMS = (((1,), (1,)), ((), ()))


def _sigmoid(x):
    return 1.0 / (1.0 + jnp.exp(-x))


def _in_proj_kernel(x_ref, g_ref, w_ref, wgt_ref, p_ref, gt_ref):
    x = x_ref[...]
    h = x * lax.rsqrt(jnp.mean(x * x, axis=-1, keepdims=True) + EPS) * g_ref[...]
    hb = h.astype(jnp.bfloat16)
    p_ref[...] = jnp.dot(hb, w_ref[...], preferred_element_type=jnp.float32).astype(p_ref.dtype)
    gt = lax.dot_general(wgt_ref[...], hb, NT_DIMS, preferred_element_type=jnp.float32)
    gt_ref[...] = gt[:N_GATES]


def _in_proj(x2, g1, w_main, wg_t):
    T = x2.shape[0]
    return pl.pallas_call(
        _in_proj_kernel,
        grid=(T // TM_PROJ,),
        in_specs=[
            pl.BlockSpec((TM_PROJ, D_MODEL), lambda i: (i, 0)),
            pl.BlockSpec((1, D_MODEL), lambda i: (0, 0)),
            pl.BlockSpec((D_MODEL, N_MAIN), lambda i: (0, 0)),
            pl.BlockSpec((BF16_SUBLANES, D_MODEL), lambda i: (0, 0)),
        ],
        out_specs=[
            pl.BlockSpec((TM_PROJ, N_MAIN), lambda i: (i, 0)),
            pl.BlockSpec((N_GATES, TM_PROJ), lambda i: (0, i)),
        ],
        out_shape=[
            jax.ShapeDtypeStruct((T, N_MAIN), jnp.bfloat16),
            jax.ShapeDtypeStruct((N_GATES, T), jnp.float32),
        ],
        compiler_params=pltpu.CompilerParams(
            dimension_semantics=("parallel",), vmem_limit_bytes=VMEM_LIMIT),
        name="in_proj",
    )(x2, g1, w_main, wg_t)


def _mlstm_kernel(qk_ref, qkp_ref, v_ref, o_ref, gt_ref, convw_ref, gb_ref, hng_ref,
                  tri_ref, shift_ref, hshift_ref, y_ref, cn_ref, m_ref):
    L = CHUNK
    c = pl.program_id(1)

    @pl.when(c == 0)
    def _():
        cn_ref[...] = jnp.zeros_like(cn_ref)
        m_ref[...] = jnp.zeros_like(m_ref)

    row_id = lax.broadcasted_iota(jnp.int32, (L, L), 0)
    col_id = lax.broadcasted_iota(jnp.int32, (L, L), 1)
    causal = col_id <= row_id
    ones_blk = jnp.ones((L, HEAD_DIM), jnp.bfloat16)
    lane = lax.broadcasted_iota(jnp.int32, (MLSTM_HEADS, L), 1)

    gate_terms = []
    for bb in range(MLSTM_BATCH):
        gt = gt_ref[bb] + gb_ref[...]
        f = gt[MLSTM_HEADS:]
        lf = jnp.minimum(f, 0.0) - jnp.log(1.0 + jnp.exp(-jnp.abs(f)))
        ig = gt[:MLSTM_HEADS]
        b_rows = lax.dot_general(lf, tri_ref[...], NT_DIMS, precision=lax.Precision.HIGHEST,
                                 preferred_element_type=jnp.float32)
        c_rows = ig - b_rows
        cm_rows = c_rows
        d = 1
        while d < L:
            cm_rows = jnp.maximum(
                cm_rows, jnp.where(lane >= d, pltpu.roll(cm_rows, d, axis=1), -jnp.inf))
            d *= 2
        gate_terms.append((b_rows, c_rows, cm_rows))

    conv_terms = []
    for bb in range(MLSTM_BATCH):
        x_cur = qk_ref[bb]
        x_prev = jnp.where(c > 0, qkp_ref[bb], jnp.zeros((HALO, 2 * MLSTM_WIDTH), jnp.bfloat16))
        acc = convw_ref[CONV_WIDTH - 1:CONV_WIDTH, :] * x_cur.astype(jnp.float32)
        for j in range(CONV_WIDTH - 1):
            sh = jnp.dot(shift_ref[j], x_cur, preferred_element_type=jnp.float32)
            top = sh[:8] + jnp.dot(hshift_ref[j], x_prev, preferred_element_type=jnp.float32)
            sh = jnp.concatenate([top, sh[8:]], axis=0)
            acc = acc + convw_ref[j:j + 1, :] * sh
        qk = acc * _sigmoid(acc)
        q_all = qk[:, :MLSTM_WIDTH].astype(jnp.bfloat16)
        k_t = jnp.transpose(qk[:, MLSTM_WIDTH:] * (HEAD_DIM ** -0.5))
        conv_terms.append((q_all, k_t))

    for bb in range(MLSTM_BATCH):
        b_rows, c_rows, cm_rows = gate_terms[bb]
        q_all, k_t = conv_terms[bb]
        m_in4 = jnp.concatenate(
            [m_ref[bb * MLSTM_HEADS + h][0:1, 0:1] for h in range(MLSTM_HEADS)], axis=0)
        mx_rows = jnp.maximum(cm_rows, m_in4)
        inter_rows = jnp.exp(m_in4 - mx_rows)
        einv_rows = jnp.exp(-(b_rows + mx_rows))
        fac_t = jnp.transpose(jnp.concatenate(
            [mx_rows, inter_rows, einv_rows, jnp.zeros_like(mx_rows)], axis=0))

        for h in range(MLSTM_HEADS):
            lo = h * HEAD_DIM
            st = bb * MLSTM_HEADS + h
            q = q_all[:, lo:lo + HEAD_DIM]
            kt = k_t[lo:lo + HEAD_DIM, :]
            v_ext = jnp.concatenate([v_ref[bb, :, lo:lo + HEAD_DIM], ones_blk], axis=1)
            mx_col = fac_t[:, h:h + 1]
            inter_col = fac_t[:, MLSTM_HEADS + h:MLSTM_HEADS + h + 1]
            einv_col = fac_t[:, 2 * MLSTM_HEADS + h:2 * MLSTM_HEADS + h + 1]
            c_row = c_rows[h:h + 1, :]
            b_tot = b_rows[h:h + 1, L - 1:L]
            cm_tot = cm_rows[h:h + 1, L - 1:L]
            m_in = m_ref[st][0:1, 0:1]
            cn = cn_ref[st]

            s_qk = jnp.dot(q, kt.astype(jnp.bfloat16), preferred_element_type=jnp.float32)
            s = (s_qk * jnp.exp(jnp.where(causal, c_row - mx_col, -jnp.inf))).astype(jnp.bfloat16)
            num = (jnp.dot(s, v_ext, preferred_element_type=jnp.float32)
                   + inter_col * jnp.dot(q, cn.astype(jnp.bfloat16),
                                         preferred_element_type=jnp.float32))
            den = num[:, HEAD_DIM:]
            hh = num[:, :HEAD_DIM] / jnp.maximum(jnp.abs(den), einv_col)

            mu = jnp.mean(hh, axis=-1, keepdims=True)
            dv = hh - mu
            var = jnp.mean(dv * dv, axis=-1, keepdims=True)
            hn = dv * lax.rsqrt(var + EPS) * hng_ref[:, lo:lo + HEAD_DIM]
            og = _sigmoid(o_ref[bb, :, lo:lo + HEAD_DIM].astype(jnp.float32))
            y_ref[bb, :, lo:lo + HEAD_DIM] = (og * hn).astype(y_ref.dtype)

            m_loc = b_tot + cm_tot
            kw_t = (kt * jnp.exp(c_row - cm_tot)).astype(jnp.bfloat16)
            c_loc = jnp.dot(kw_t, v_ext, preferred_element_type=jnp.float32)
            m_new = jnp.maximum(b_tot + m_in, m_loc)
            s_old = jnp.exp(b_tot + m_in - m_new)
            s_loc = jnp.exp(m_loc - m_new)
            cn_ref[st] = s_old * cn + s_loc * c_loc
            m_ref[st] = jnp.broadcast_to(m_new, m_ref.shape[1:])


def _mlstm(p3, gates_b, conv_w, gate_b, hn_g, tri, shifts, halo_shifts):
    batch, seq, _ = p3.shape
    L = CHUNK
    BB = MLSTM_BATCH
    halo_per_chunk = L // HALO
    return pl.pallas_call(
        _mlstm_kernel,
        grid=(batch // BB, seq // L),
        in_specs=[
            pl.BlockSpec((BB, L, 2 * MLSTM_WIDTH), lambda bi, ci: (bi, ci, 0)),
            pl.BlockSpec((BB, HALO, 2 * MLSTM_WIDTH),
                         lambda bi, ci: (bi, jnp.maximum(ci * halo_per_chunk - 1, 0), 0)),
            pl.BlockSpec((BB, L, MLSTM_WIDTH), lambda bi, ci: (bi, ci, 2)),
            pl.BlockSpec((BB, L, MLSTM_WIDTH), lambda bi, ci: (bi, ci, 3)),
            pl.BlockSpec((BB, N_GATES, L), lambda bi, ci: (bi, 0, ci)),
            pl.BlockSpec((CONV_WIDTH, 2 * MLSTM_WIDTH), lambda bi, ci: (0, 0)),
            pl.BlockSpec((N_GATES, 1), lambda bi, ci: (0, 0)),
            pl.BlockSpec((1, MLSTM_WIDTH), lambda bi, ci: (0, 0)),
            pl.BlockSpec((L, L), lambda bi, ci: (0, 0)),
            pl.BlockSpec((CONV_WIDTH - 1, L, L), lambda bi, ci: (0, 0, 0)),
            pl.BlockSpec((CONV_WIDTH - 1, 8, HALO), lambda bi, ci: (0, 0, 0)),
        ],
        out_specs=pl.BlockSpec((BB, L, MLSTM_WIDTH), lambda bi, ci: (bi, ci, 0)),
        out_shape=jax.ShapeDtypeStruct((batch, seq, MLSTM_WIDTH), jnp.bfloat16),
        scratch_shapes=[
            pltpu.VMEM((BB * MLSTM_HEADS, HEAD_DIM, 2 * HEAD_DIM), jnp.float32),
            pltpu.VMEM((BB * MLSTM_HEADS, 8, LANES), jnp.float32),
        ],
        compiler_params=pltpu.CompilerParams(
            dimension_semantics=("parallel", "arbitrary"), vmem_limit_bytes=VMEM_LIMIT),
        name="mlstm",
    )(p3, p3, p3, p3, gates_b, conv_w, gate_b, hn_g, tri, shifts, halo_shifts)


def _out_route_kernel(seq, x_ref, ym_ref, u_ref, up_ref, pw_ref, ps_ref, wo_ref, g2_ref,
                      wrt_ref, br_ref, x1_ref, h2_ref, idx_ref, gate_ref, rank_ref, cnt_ref,
                      ubuf_ref, carry_ref):
    TM = TM_PROJ
    i = pl.program_id(0)

    @pl.when(i == 0)
    def _():
        carry_ref[...] = jnp.zeros_like(carry_ref)

    pos0 = (i * TM) % seq
    ubuf_ref[0:HALO, :] = jnp.where(pos0 > 0, up_ref[...].astype(jnp.float32), 0.0)
    ubuf_ref[HALO:HALO + TM, :] = u_ref[...].astype(jnp.float32)
    pos = (pos0 + lax.broadcasted_iota(jnp.int32, (TM, 1), 0) + 1).astype(jnp.float32)
    mixed = []
    for gi, w in enumerate(POOL_WINDOWS):
        lo = gi * POOL_GROUP_DIM
        tok = ubuf_ref[HALO:HALO + TM, lo:lo + POOL_GROUP_DIM]
        acc = tok
        for j in range(1, w):
            acc = acc + ubuf_ref[HALO - j:HALO - j + TM, lo:lo + POOL_GROUP_DIM]
        pooled = acc / jnp.minimum(pos, float(w)) - tok
        mg = jnp.dot(pooled.astype(jnp.bfloat16), pw_ref[gi], preferred_element_type=jnp.float32)
        mixed.append((mg * ps_ref[:, lo:lo + POOL_GROUP_DIM]).astype(jnp.bfloat16))
    y_cat = jnp.concatenate([ym_ref[...]] + mixed, axis=1)

    x1 = x_ref[...] + jnp.dot(y_cat, wo_ref[...], preferred_element_type=jnp.float32)
    x1_ref[...] = x1
    h2 = x1 * lax.rsqrt(jnp.mean(x1 * x1, axis=-1, keepdims=True) + EPS) * g2_ref[...]
    h2b = h2.astype(jnp.bfloat16)
    for s in range(SLAB):
        h2_ref[pl.ds(s, TM, stride=SLAB), :] = h2[:, s * LANES:(s + 1) * LANES]

    logits = lax.dot_general(wrt_ref[...], h2b, NT_DIMS,
                             preferred_element_type=jnp.float32) + br_ref[...]
    e_id = lax.broadcasted_iota(jnp.int32, (N_EXPERTS, TM), 0).astype(jnp.float32)
    work = logits
    vals, ids, hots = [], [], []
    for _ in range(TOP_K):
        mk = jnp.max(work, axis=0, keepdims=True)
        ik = jnp.min(jnp.where(work == mk, e_id, float(N_EXPERTS)), axis=0, keepdims=True)
        hot = e_id == ik
        work = jnp.where(hot, -jnp.inf, work)
        vals.append(mk)
        ids.append(ik)
        hots.append(hot)
    ex = [jnp.exp(vk - vals[0]) for vk in vals]
    denom = ex[0] + ex[1] + ex[2] + ex[3]
    gate_ref[...] = jnp.concatenate([e / denom for e in ex], axis=0)
    idx_ref[...] = jnp.concatenate(ids, axis=0).astype(jnp.int32)

    sel_f = sum(jnp.where(hot, 1.0, 0.0) for hot in hots)
    t_row = lax.broadcasted_iota(jnp.int32, (TM, TM), 0)
    t_col = lax.broadcasted_iota(jnp.int32, (TM, TM), 1)
    before = jnp.where(t_row < t_col, 1.0, 0.0).astype(jnp.bfloat16)
    prefix = jnp.dot(sel_f.astype(jnp.bfloat16), before, preferred_element_type=jnp.float32)
    carry = carry_ref[...]
    rank_e = carry[:, 0:1] + prefix
    ranks = [jnp.sum(jnp.where(hot, rank_e, 0.0), axis=0, keepdims=True) for hot in hots]
    rank_ref[...] = jnp.concatenate(ranks, axis=0).astype(jnp.int32)
    carry_new = carry + jnp.sum(sel_f, axis=1, keepdims=True)
    carry_ref[...] = carry_new
    cnt_ref[...] = carry_new.astype(jnp.int32)


def _out_route(x2, ym, p, pool_w, pool_s, w_out, g2, wr_t, br, seq):
    T = x2.shape[0]
    TM = TM_PROJ
    nt = T // TM
    u_blk = N_MAIN // POOL_WIDTH - 1
    halo_per_tile = TM // HALO
    tok_spec = pl.BlockSpec((TOP_K, TM), lambda i: (0, i))
    return pl.pallas_call(
        functools.partial(_out_route_kernel, seq),
        grid=(nt,),
        in_specs=[
            pl.BlockSpec((TM, D_MODEL), lambda i: (i, 0)),
            pl.BlockSpec((TM, MLSTM_WIDTH), lambda i: (i, 0)),
            pl.BlockSpec((TM, POOL_WIDTH), lambda i: (i, u_blk)),
            pl.BlockSpec((HALO, POOL_WIDTH),
                         lambda i: (jnp.maximum(i * halo_per_tile - 1, 0), u_blk)),
            pl.BlockSpec((len(POOL_WINDOWS), POOL_GROUP_DIM, POOL_GROUP_DIM), lambda i: (0, 0, 0)),
            pl.BlockSpec((1, POOL_WIDTH), lambda i: (0, 0)),
            pl.BlockSpec((D_MODEL, D_MODEL), lambda i: (0, 0)),
            pl.BlockSpec((1, D_MODEL), lambda i: (0, 0)),
            pl.BlockSpec((N_EXPERTS, D_MODEL), lambda i: (0, 0)),
            pl.BlockSpec((N_EXPERTS, 1), lambda i: (0, 0)),
        ],
        out_specs=[
            pl.BlockSpec((TM, D_MODEL), lambda i: (i, 0)),
            pl.BlockSpec((TM * SLAB, LANES), lambda i: (i, 0)),
            tok_spec, tok_spec, tok_spec,
            pl.BlockSpec((N_EXPERTS, LANES), lambda i: (0, 0)),
        ],
        out_shape=[
            jax.ShapeDtypeStruct((T, D_MODEL), jnp.float32),
            jax.ShapeDtypeStruct((T * SLAB, LANES), jnp.float32),
            jax.ShapeDtypeStruct((TOP_K, T), jnp.int32),
            jax.ShapeDtypeStruct((TOP_K, T), jnp.float32),
            jax.ShapeDtypeStruct((TOP_K, T), jnp.int32),
            jax.ShapeDtypeStruct((N_EXPERTS, LANES), jnp.int32),
        ],
        scratch_shapes=[
            pltpu.VMEM((HALO + TM, POOL_WIDTH), jnp.float32),
            pltpu.VMEM((N_EXPERTS, LANES), jnp.float32),
        ],
        compiler_params=pltpu.CompilerParams(
            dimension_semantics=("arbitrary",), vmem_limit_bytes=VMEM_LIMIT),
        name="out_route",
    )(x2, ym, p, p, pool_w, pool_s, w_out, g2, wr_t, br)


def _plan(dest_flat, fill):
    n_assign = dest_flat.shape[0]
    n_table = fill.shape[0]
    mesh = plsc.VectorSubcoreMesh(core_axis_name="c", subcore_axis_name="s")

    @pl.kernel(out_type=jax.ShapeDtypeStruct((n_table,), jnp.int32), mesh=mesh,
               scratch_types=[pltpu.VMEM((n_table,), jnp.int32),
                              pltpu.VMEM((PLAN_CHUNK,), jnp.int32)],
               compiler_params=pltpu.CompilerParams(needs_layout_passes=False))
    def plan_kernel(dest_hbm, fill_hbm, out_hbm, table, chunk):
        first = jnp.logical_and(lax.axis_index("c") == 0, lax.axis_index("s") == 0)

        @pl.when(first)
        def _():
            pltpu.sync_copy(fill_hbm, table)

            @pl.loop(0, n_assign // PLAN_CHUNK)
            def _(ci):
                pltpu.sync_copy(dest_hbm.at[pl.ds(ci * PLAN_CHUNK, PLAN_CHUNK)], chunk)

                @pl.loop(0, PLAN_CHUNK // SC_LANES)
                def _(i):
                    idx = chunk[pl.ds(i * SC_LANES, SC_LANES)]
                    vals = (ci * PLAN_CHUNK + i * SC_LANES
                            + lax.broadcasted_iota(jnp.int32, (SC_LANES,), 0))
                    plsc.store_scatter(table, [idx], vals)

            pltpu.sync_copy(table, out_hbm)

    return plan_kernel(dest_flat, fill)


def _expert_kernel(n_tok, bs_ref, slot_ref, h2_ref, wg_ref, bg_ref, wu_ref, bu_ref, wd_ref, bd_ref,
                   yt_ref, xg0_ref, xg1_ref, xg2_ref, ys0_ref, ys1_ref, ys2_ref,
                   wgb_ref, wub_ref, wdb_ref, gsem, ssem):
    TM = TM_EXPERT
    ROWS = TM * SLAB
    YROWS = TM * YSLAB
    e = pl.program_id(0)
    n_total = bs_ref[N_EXPERTS]
    xg = (xg0_ref, xg1_ref, xg2_ref)
    ys = (ys0_ref, ys1_ref, ys2_ref)

    def token_of(a):
        return a & (n_tok - 1) if n_tok & (n_tok - 1) == 0 else lax.rem(a, n_tok)

    def start_gather(blk, par):
        base = (blk + 1) * TM
        for r in range(TM):
            t = token_of(slot_ref[base + r])
            pltpu.make_async_copy(h2_ref.at[pl.ds(pl.multiple_of(t * SLAB, SLAB), SLAB), :],
                                  xg[par].at[pl.ds(r * SLAB, SLAB), :], gsem.at[par]).start()

    def wait_gather(par):
        pltpu.make_async_copy(h2_ref.at[pl.ds(0, ROWS), :], xg[0], gsem.at[par]).wait()

    def start_scatter(blk, par):
        base = (blk + 1) * TM
        for r in range(TM):
            a = slot_ref[base + r]
            pltpu.make_async_copy(ys[par].at[pl.ds(r * YSLAB, YSLAB), :],
                                  yt_ref.at[pl.ds(pl.multiple_of(a * YSLAB, YSLAB), YSLAB), :],
                                  ssem.at[par]).start()

    def wait_scatter(par):
        pltpu.make_async_copy(ys[0], yt_ref.at[pl.ds(0, YROWS), :], ssem.at[par]).wait()

    @pl.when(e == 0)
    def _():
        start_gather(0, 0)
        start_gather(1, 1)
        for par in range(NBUF):
            ys[par][...] = jnp.zeros_like(ys[par])
            dump = yt_ref.at[pl.ds((n_tok * TOP_K + par * TM) * YSLAB, YROWS), :]
            cp = pltpu.make_async_copy(ys[par], dump, ssem.at[par])
            cp.start()
            cp.wait()

    wgb_ref[...] = wg_ref[0].astype(jnp.bfloat16)
    wub_ref[...] = wu_ref[0].astype(jnp.bfloat16)
    wdb_ref[...] = wd_ref[0].astype(jnp.bfloat16)

    def block_step(g, par):
        nxt2 = (par + 2) % NBUF
        wait_gather(par)

        @pl.when(g >= 2)
        def _():
            wait_scatter(par)

        start_gather(g + 2, nxt2)
        start_scatter(g - 1, nxt2)
        x = jnp.concatenate(
            [xg[par][pl.ds(s, TM, stride=SLAB), :].astype(jnp.bfloat16) for s in range(SLAB)],
            axis=1)
        gate = jnp.dot(x, wgb_ref[...], preferred_element_type=jnp.float32) + bg_ref[0]
        up = jnp.dot(x, wub_ref[...], preferred_element_type=jnp.float32) + bu_ref[0]
        gate = jnp.minimum(gate, SWIGLU_LIMIT)
        up = jnp.clip(up, -SWIGLU_LIMIT, SWIGLU_LIMIT)
        glu = gate * _sigmoid(SWIGLU_ALPHA * gate)
        act = (glu * (up + 1.0)).astype(jnp.bfloat16)
        y = jnp.dot(act, wdb_ref[...], preferred_element_type=jnp.float32) + bd_ref[0]
        lo = pltpu.bitcast(y[:, :D_MODEL // 2].astype(jnp.bfloat16).astype(jnp.float32), jnp.uint32)
        hi = pltpu.bitcast(y[:, D_MODEL // 2:].astype(jnp.bfloat16).astype(jnp.float32), jnp.uint32)
        packed = (lo >> 16) | (hi & jnp.uint32(0xFFFF0000))
        for s in range(YSLAB):
            ys[par][pl.ds(s, TM, stride=YSLAB), :] = packed[:, s * LANES:(s + 1) * LANES]

    def body(g, carry):
        for par in range(NBUF):
            pl.when(g % NBUF == par)(functools.partial(block_step, g, par))
        return carry

    lax.fori_loop(bs_ref[e], bs_ref[e + 1], body, 0)

    @pl.when(e == N_EXPERTS - 1)
    def _():
        g = n_total
        for par in range(NBUF):
            @pl.when((g - 1) % NBUF == par)
            def _():
                start_scatter(g - 1, par)
        wait_gather(g % NBUF)
        wait_gather((g + 1) % NBUF)
        wait_scatter((g - 1) % NBUF)
        wait_scatter((g + 1) % NBUF)

        @pl.when(g >= 2)
        def _():
            wait_scatter(g % NBUF)


def _experts(block_start, slot_buf, h2_slab, w_gate, b_gate, w_up, b_up, w_down, b_down, n_tok):
    TM = TM_EXPERT
    n_assign = n_tok * TOP_K
    w_spec = pl.BlockSpec((1, D_MODEL, D_FF), lambda e, bs, sl: (e, 0, 0))
    bias_spec = pl.BlockSpec((1, 1, D_FF), lambda e, bs, sl: (e, 0, 0))
    buf = pltpu.VMEM((TM * SLAB, LANES), jnp.float32)
    ybuf = pltpu.VMEM((TM * YSLAB, LANES), jnp.uint32)
    grid_spec = pltpu.PrefetchScalarGridSpec(
        num_scalar_prefetch=2,
        grid=(N_EXPERTS,),
        in_specs=[
            pl.BlockSpec(memory_space=pl.ANY),
            w_spec, bias_spec, w_spec, bias_spec, w_spec, bias_spec,
        ],
        out_specs=pl.BlockSpec(memory_space=pl.ANY),
        scratch_shapes=[
            buf, buf, buf, ybuf, ybuf, ybuf,
            pltpu.VMEM((D_MODEL, D_FF), jnp.bfloat16),
            pltpu.VMEM((D_MODEL, D_FF), jnp.bfloat16),
            pltpu.VMEM((D_FF, D_MODEL), jnp.bfloat16),
            pltpu.SemaphoreType.DMA((NBUF,)),
            pltpu.SemaphoreType.DMA((NBUF,)),
        ],
    )
    return pl.pallas_call(
        functools.partial(_expert_kernel, n_tok),
        grid_spec=grid_spec,
        out_shape=jax.ShapeDtypeStruct(((n_assign + NBUF * TM) * YSLAB, LANES), jnp.uint32),
        compiler_params=pltpu.CompilerParams(
            dimension_semantics=("arbitrary",), vmem_limit_bytes=VMEM_LIMIT),
        name="experts",
    )(block_start, slot_buf, h2_slab, w_gate, b_gate, w_up, b_up, w_down, b_down)


def _combine_kernel(normalize, x1_ref, y0_ref, y1_ref, y2_ref, y3_ref, gate_ref, g_ref, o_ref):
    TM = TM_PROJ
    gates = jnp.concatenate([gate_ref[...], jnp.zeros((8 - TOP_K, TM), jnp.float32)], axis=0)
    g_cols = jnp.transpose(gates)
    g_bc = [jnp.broadcast_to(g_cols[:, k:k + 1], (TM, LANES)) for k in range(TOP_K)]
    ssq = jnp.zeros((TM, LANES), jnp.float32)
    parts = [x1_ref[:, s * LANES:(s + 1) * LANES] for s in range(SLAB)]
    for s in range(YSLAB):
        for k, y_ref in enumerate((y0_ref, y1_ref, y2_ref, y3_ref)):
            w = y_ref[pl.ds(s, TM, stride=YSLAB), :]
            lo = pltpu.bitcast(w << 16, jnp.float32)
            hi = pltpu.bitcast(w & jnp.uint32(0xFFFF0000), jnp.float32)
            parts[s] = parts[s] + g_bc[k] * lo
            parts[YSLAB + s] = parts[YSLAB + s] + g_bc[k] * hi
    for acc in parts:
        ssq = ssq + acc * acc
    if normalize:
        inv = lax.rsqrt(jnp.sum(ssq, axis=-1, keepdims=True) * (1.0 / D_MODEL) + EPS)
        for s in range(SLAB):
            o_ref[:, s * LANES:(s + 1) * LANES] = parts[s] * inv * g_ref[:, s * LANES:(s + 1) * LANES]
    else:
        for s in range(SLAB):
            o_ref[:, s * LANES:(s + 1) * LANES] = parts[s]


def _combine(x1, y_tok, gate_t, gf, normalize):
    T = x1.shape[0]
    TM = TM_PROJ
    nt = T // TM

    def y_spec(k):
        return pl.BlockSpec((TM * YSLAB, LANES), lambda i: (k * nt + i, 0))

    return pl.pallas_call(
        functools.partial(_combine_kernel, normalize),
        grid=(nt,),
        in_specs=[
            pl.BlockSpec((TM, D_MODEL), lambda i: (i, 0)),
            y_spec(0), y_spec(1), y_spec(2), y_spec(3),
            pl.BlockSpec((TOP_K, TM), lambda i: (0, i)),
            pl.BlockSpec((1, D_MODEL), lambda i: (0, 0)),
        ],
        out_specs=pl.BlockSpec((TM, D_MODEL), lambda i: (i, 0)),
        out_shape=jax.ShapeDtypeStruct((T, D_MODEL), jnp.float32),
        compiler_params=pltpu.CompilerParams(
            dimension_semantics=("parallel",), vmem_limit_bytes=VMEM_LIMIT),
        name="combine",
    )(x1, y_tok, y_tok, y_tok, y_tok, gate_t, gf)


def kernel(x, norm1_g, w_in, ig_b, fg_b, conv_w, head_norm_g, pool_w, pool_scale, w_out, norm2_g,
           w_router, b_router, w_gate, b_gate, w_up, b_up, w_down, b_down, normf_g):
    B, S, D = x.shape
    T = B * S
    depth = norm1_g.shape[0]
    W = MLSTM_WIDTH
    f32, bf16 = jnp.float32, jnp.bfloat16

    L = CHUNK
    t_l = lax.broadcasted_iota(jnp.int32, (L, L), 0)
    t_r = lax.broadcasted_iota(jnp.int32, (L, L), 1)
    tri = (t_r <= t_l).astype(f32)
    shifts = jnp.stack([(t_l - t_r == CONV_WIDTH - 1 - j).astype(bf16)
                        for j in range(CONV_WIDTH - 1)])
    h_t = lax.broadcasted_iota(jnp.int32, (8, HALO), 0)
    h_r = lax.broadcasted_iota(jnp.int32, (8, HALO), 1)
    halo_shifts = jnp.stack([(h_r - HALO - h_t == -(CONV_WIDTH - 1 - j)).astype(bf16)
                             for j in range(CONV_WIDTH - 1)])

    n_assign = T * TOP_K
    n_blocks = -(-n_assign // TM_EXPERT) + N_EXPERTS
    n_rows = n_blocks * TM_EXPERT
    n_table = n_rows + 3 * TM_EXPERT
    fill = n_assign + ((jnp.arange(n_table, dtype=jnp.int32) + (NBUF - 1) * TM_EXPERT)
                       % (NBUF * TM_EXPERT))
    x2 = x.reshape(T, D)
    for l in range(depth):
        w = w_in[l]
        w_main = jnp.concatenate([w[:, :4 * W], w[:, 4 * W + N_GATES:]], axis=1).astype(bf16)
        wg_t = jnp.zeros((BF16_SUBLANES, D), bf16).at[:N_GATES].set(
            w[:, 4 * W:4 * W + N_GATES].T.astype(bf16))
        p, gates_t = _in_proj(x2, norm1_g[l][None, :], w_main, wg_t)

        gate_b = jnp.concatenate([ig_b[l], fg_b[l]])[:, None].astype(f32)
        gates_b = gates_t.reshape(N_GATES, B, S).transpose(1, 0, 2)
        ym = _mlstm(p.reshape(B, S, N_MAIN), gates_b, conv_w[l].astype(f32), gate_b,
                    head_norm_g[l][None, :], tri, shifts, halo_shifts).reshape(T, W)

        x1, h2, idx_t, gate_t, rank_t, cnt = _out_route(
            x2, ym, p, pool_w[l].astype(bf16), pool_scale[l][None, :], w_out[l].astype(bf16),
            norm2_g[l][None, :], w_router[l].T.astype(bf16), b_router[l][:, None], S)

        counts = cnt[:, 0]
        padded = ((counts + TM_EXPERT - 1) // TM_EXPERT) * TM_EXPERT
        padded_end = jnp.cumsum(padded)
        padded_start = padded_end - padded
        expert_ids = jnp.arange(N_EXPERTS, dtype=jnp.int32)[:, None, None]
        start_of = jnp.sum(jnp.where(idx_t[None] == expert_ids, padded_start[:, None, None], 0), axis=0)
        dest = start_of + rank_t
        block_start = jnp.concatenate(
            [jnp.zeros((1,), jnp.int32), (padded_end // TM_EXPERT).astype(jnp.int32)])

        slot_buf = _plan(dest.reshape(-1) + TM_EXPERT, fill)
        y_tok = _experts(block_start, slot_buf, h2, w_gate[l], b_gate[l][:, None, :],
                         w_up[l], b_up[l][:, None, :], w_down[l], b_down[l][:, None, :], T)
        last = l + 1 == depth
        x2 = _combine(x1, y_tok, gate_t, normf_g[None, :], last)
    return x2.reshape(B, S, D)
```

```python
import functools

import jax
import jax.numpy as jnp
from jax import lax
from jax.experimental import pallas as pl
from jax.experimental.pallas import tpu as pltpu
from jax.experimental.pallas import tpu_sc as plsc

D_MODEL = 1024
MLSTM_WIDTH = 512
MLSTM_HEADS = 4
HEAD_DIM = 128
CONV_WIDTH = 4
POOL_WIDTH = 512
POOL_WINDOWS = (2, 4, 8, 16)
POOL_GROUP_DIM = 128
N_EXPERTS = 32
TOP_K = 4
D_FF = 1024
SWIGLU_LIMIT = 7.0
SWIGLU_ALPHA = 1.702
EPS = 1e-5

N_MAIN = 4 * MLSTM_WIDTH + POOL_WIDTH
N_GATES = 2 * MLSTM_HEADS

LANES = 128
BF16_SUBLANES = 16
VMEM_LIMIT = 56 * 1024 * 1024

TM_PROJ = 512
ROUTE_SUB = 2
CHUNK = 256
MLSTM_BATCH = 2
HALO = 16
TM_EXPERT = 256
NBUF = 3
SLAB = D_MODEL // LANES
YSLAB = SLAB // 2
PLAN_CHUNK = 8192
SC_LANES = 16
PLAN_UNROLL = 8

NT_DIMS = (((1,), (1,)), ((), ()))


def _sigmoid(x):
    return 1.0 / (1.0 + jnp.exp(-x))


def _in_proj_kernel(x_ref, g_ref, w_ref, wgt_ref, p_ref, gt_ref):
    x = x_ref[...]
    h = x * lax.rsqrt(jnp.mean(x * x, axis=-1, keepdims=True) + EPS) * g_ref[...]
    hb = h.astype(jnp.bfloat16)
    p_ref[...] = jnp.dot(hb, w_ref[...], preferred_element_type=jnp.float32).astype(p_ref.dtype)
    gt = lax.dot_general(wgt_ref[...], hb, NT_DIMS, preferred_element_type=jnp.float32)
    gt_ref[...] = gt[:N_GATES]


def _in_proj(x2, g1, w_main, wg_t):
    T = x2.shape[0]
    return pl.pallas_call(
        _in_proj_kernel,
        grid=(T // TM_PROJ,),
        in_specs=[
            pl.BlockSpec((TM_PROJ, D_MODEL), lambda i: (i, 0)),
            pl.BlockSpec((1, D_MODEL), lambda i: (0, 0)),
            pl.BlockSpec((D_MODEL, N_MAIN), lambda i: (0, 0)),
            pl.BlockSpec((BF16_SUBLANES, D_MODEL), lambda i: (0, 0)),
        ],
        out_specs=[
            pl.BlockSpec((TM_PROJ, N_MAIN), lambda i: (i, 0)),
            pl.BlockSpec((N_GATES, TM_PROJ), lambda i: (0, i)),
        ],
        out_shape=[
            jax.ShapeDtypeStruct((T, N_MAIN), jnp.bfloat16),
            jax.ShapeDtypeStruct((N_GATES, T), jnp.float32),
        ],
        compiler_params=pltpu.CompilerParams(
            dimension_semantics=("parallel",), vmem_limit_bytes=VMEM_LIMIT),
        name="in_proj",
    )(x2, g1, w_main, wg_t)


def _mlstm_kernel(qk_ref, qkp_ref, v_ref, o_ref, gt_ref, convw_ref, gb_ref, hng_ref,
                  tri_ref, shift_ref, hshift_ref, y_ref, cn_ref, m_ref):
    L = CHUNK
    c = pl.program_id(1)

    @pl.when(c == 0)
    def _():
        cn_ref[...] = jnp.zeros_like(cn_ref)
        m_ref[...] = jnp.zeros_like(m_ref)

    row_id = lax.broadcasted_iota(jnp.int32, (L, L), 0)
    col_id = lax.broadcasted_iota(jnp.int32, (L, L), 1)
    causal = col_id <= row_id
    ones_blk = jnp.ones((L, HEAD_DIM), jnp.bfloat16)
    lane = lax.broadcasted_iota(jnp.int32, (MLSTM_HEADS, L), 1)

    gate_terms = []
    for bb in range(MLSTM_BATCH):
        gt = gt_ref[bb] + gb_ref[...]
        f = gt[MLSTM_HEADS:]
        lf = jnp.minimum(f, 0.0) - jnp.log(1.0 + jnp.exp(-jnp.abs(f)))
        ig = gt[:MLSTM_HEADS]
        b_rows = lax.dot_general(lf, tri_ref[...], NT_DIMS, precision=lax.Precision.HIGHEST,
                                 preferred_element_type=jnp.float32)
        c_rows = ig - b_rows
        cm_rows = c_rows
        d = 1
        while d < L:
            cm_rows = jnp.maximum(
                cm_rows, jnp.where(lane >= d, pltpu.roll(cm_rows, d, axis=1), -jnp.inf))
            d *= 2
        gate_terms.append((b_rows, c_rows, cm_rows))

    conv_terms = []
    for bb in range(MLSTM_BATCH):
        x_cur = qk_ref[bb]
        x_prev = jnp.where(c > 0, qkp_ref[bb], jnp.zeros((HALO, 2 * MLSTM_WIDTH), jnp.bfloat16))
        acc = convw_ref[CONV_WIDTH - 1:CONV_WIDTH, :] * x_cur.astype(jnp.float32)
        for j in range(CONV_WIDTH - 1):
            sh = jnp.dot(shift_ref[j], x_cur, preferred_element_type=jnp.float32)
            top = sh[:8] + jnp.dot(hshift_ref[j], x_prev, preferred_element_type=jnp.float32)
            sh = jnp.concatenate([top, sh[8:]], axis=0)
            acc = acc + convw_ref[j:j + 1, :] * sh
        qk = acc * _sigmoid(acc)
        q_all = qk[:, :MLSTM_WIDTH].astype(jnp.bfloat16)
        k_t = jnp.transpose(qk[:, MLSTM_WIDTH:] * (HEAD_DIM ** -0.5))
        conv_terms.append((q_all, k_t))

    for bb in range(MLSTM_BATCH):
        b_rows, c_rows, cm_rows = gate_terms[bb]
        q_all, k_t = conv_terms[bb]
        m_in4 = jnp.concatenate(
            [m_ref[bb * MLSTM_HEADS + h][0:1, 0:1] for h in range(MLSTM_HEADS)], axis=0)
        mx_rows = jnp.maximum(cm_rows, m_in4)
        inter_rows = jnp.exp(m_in4 - mx_rows)
        einv_rows = jnp.exp(-(b_rows + mx_rows))
        fac_t = jnp.transpose(jnp.concatenate(
            [mx_rows, inter_rows, einv_rows, jnp.zeros_like(mx_rows)], axis=0))

        for h in range(MLSTM_HEADS):
            lo = h * HEAD_DIM
            st = bb * MLSTM_HEADS + h
            q = q_all[:, lo:lo + HEAD_DIM]
            kt = k_t[lo:lo + HEAD_DIM, :]
            v_ext = jnp.concatenate([v_ref[bb, :, lo:lo + HEAD_DIM], ones_blk], axis=1)
            mx_col = fac_t[:, h:h + 1]
            inter_col = fac_t[:, MLSTM_HEADS + h:MLSTM_HEADS + h + 1]
            einv_col = fac_t[:, 2 * MLSTM_HEADS + h:2 * MLSTM_HEADS + h + 1]
            c_row = c_rows[h:h + 1, :]
            b_tot = b_rows[h:h + 1, L - 1:L]
            cm_tot = cm_rows[h:h + 1, L - 1:L]
            m_in = m_ref[st][0:1, 0:1]
            cn = cn_ref[st]

            s_qk = jnp.dot(q, kt.astype(jnp.bfloat16), preferred_element_type=jnp.float32)
            s = (s_qk * jnp.exp(jnp.where(causal, c_row - mx_col, -jnp.inf))).astype(jnp.bfloat16)
            num = (jnp.dot(s, v_ext, preferred_element_type=jnp.float32)
                   + inter_col * jnp.dot(q, cn.astype(jnp.bfloat16),
                                         preferred_element_type=jnp.float32))
            den = num[:, HEAD_DIM:]
            hh = num[:, :HEAD_DIM] / jnp.maximum(jnp.abs(den), einv_col)

            mu = jnp.mean(hh, axis=-1, keepdims=True)
            dv = hh - mu
            var = jnp.mean(dv * dv, axis=-1, keepdims=True)
            hn = dv * lax.rsqrt(var + EPS) * hng_ref[:, lo:lo + HEAD_DIM]
            og = _sigmoid(o_ref[bb, :, lo:lo + HEAD_DIM].astype(jnp.float32))
            y_ref[bb, :, lo:lo + HEAD_DIM] = (og * hn).astype(y_ref.dtype)

            m_loc = b_tot + cm_tot
            kw_t = (kt * jnp.exp(c_row - cm_tot)).astype(jnp.bfloat16)
            c_loc = jnp.dot(kw_t, v_ext, preferred_element_type=jnp.float32)
            m_new = jnp.maximum(b_tot + m_in, m_loc)
            s_old = jnp.exp(b_tot + m_in - m_new)
            s_loc = jnp.exp(m_loc - m_new)
            cn_ref[st] = s_old * cn + s_loc * c_loc
            m_ref[st] = jnp.broadcast_to(m_new, m_ref.shape[1:])


def _mlstm(p3, gates_b, conv_w, gate_b, hn_g, tri, shifts, halo_shifts):
    batch, seq, _ = p3.shape
    L = CHUNK
    BB = MLSTM_BATCH
    halo_per_chunk = L // HALO
    return pl.pallas_call(
        _mlstm_kernel,
        grid=(batch // BB, seq // L),
        in_specs=[
            pl.BlockSpec((BB, L, 2 * MLSTM_WIDTH), lambda bi, ci: (bi, ci, 0)),
            pl.BlockSpec((BB, HALO, 2 * MLSTM_WIDTH),
                         lambda bi, ci: (bi, jnp.maximum(ci * halo_per_chunk - 1, 0), 0)),
            pl.BlockSpec((BB, L, MLSTM_WIDTH), lambda bi, ci: (bi, ci, 2)),
            pl.BlockSpec((BB, L, MLSTM_WIDTH), lambda bi, ci: (bi, ci, 3)),
            pl.BlockSpec((BB, N_GATES, L), lambda bi, ci: (bi, 0, ci)),
            pl.BlockSpec((CONV_WIDTH, 2 * MLSTM_WIDTH), lambda bi, ci: (0, 0)),
            pl.BlockSpec((N_GATES, 1), lambda bi, ci: (0, 0)),
            pl.BlockSpec((1, MLSTM_WIDTH), lambda bi, ci: (0, 0)),
            pl.BlockSpec((L, L), lambda bi, ci: (0, 0)),
            pl.BlockSpec((CONV_WIDTH - 1, L, L), lambda bi, ci: (0, 0, 0)),
            pl.BlockSpec((CONV_WIDTH - 1, 8, HALO), lambda bi, ci: (0, 0, 0)),
        ],
        out_specs=pl.BlockSpec((BB, L, MLSTM_WIDTH), lambda bi, ci: (bi, ci, 0)),
        out_shape=jax.ShapeDtypeStruct((batch, seq, MLSTM_WIDTH), jnp.bfloat16),
        scratch_shapes=[
            pltpu.VMEM((BB * MLSTM_HEADS, HEAD_DIM, 2 * HEAD_DIM), jnp.float32),
            pltpu.VMEM((BB * MLSTM_HEADS, 8, LANES), jnp.float32),
        ],
        compiler_params=pltpu.CompilerParams(
            dimension_semantics=("parallel", "arbitrary"), vmem_limit_bytes=VMEM_LIMIT),
        name="mlstm",
    )(p3, p3, p3, p3, gates_b, conv_w, gate_b, hn_g, tri, shifts, halo_shifts)


def _out_route_kernel(seq, x_ref, ym_ref, u_ref, up_ref, pw_ref, ps_ref, wo_ref, g2_ref,
                      wrt_ref, br_ref, x1_ref, h2_ref, idx_ref, gate_ref, rank_ref, cnt_ref,
                      carry_ref):
    TM = TM_PROJ
    R = ROUTE_SUB * TM
    i = pl.program_id(0)

    @pl.when(i == 0)
    def _():
        carry_ref[...] = jnp.zeros_like(carry_ref)

    pos0 = (i * R) % seq
    e_id = lax.broadcasted_iota(jnp.int32, (N_EXPERTS, TM), 0).astype(jnp.float32)
    t_row = lax.broadcasted_iota(jnp.int32, (TM, TM), 0)
    t_col = lax.broadcasted_iota(jnp.int32, (TM, TM), 1)
    before = jnp.where(t_row < t_col, 1.0, 0.0).astype(jnp.bfloat16)
    carry = carry_ref[...]
    subs = [slice(sub * TM, (sub + 1) * TM) for sub in range(ROUTE_SUB)]

    halo = jnp.where(pos0 > 0, up_ref[...].astype(jnp.float32), 0.0)
    u_ext = jnp.concatenate([halo, u_ref[...].astype(jnp.float32)], axis=0)
    win_sums = []
    for gi, w in enumerate(POOL_WINDOWS):
        sw = u_ext[:, gi * POOL_GROUP_DIM:(gi + 1) * POOL_GROUP_DIM]
        span = 1
        while span < w:
            sw = sw + pltpu.roll(sw, span, axis=0)
            span *= 2
        win_sums.append(sw)
    y_cats = []
    for sub, rows in enumerate(subs):
        r0 = sub * TM
        pos = (pos0 + r0 + lax.broadcasted_iota(jnp.int32, (TM, 1), 0) + 1).astype(jnp.float32)
        mixed = []
        for gi, w in enumerate(POOL_WINDOWS):
            lo = gi * POOL_GROUP_DIM
            tok = u_ext[HALO + r0:HALO + r0 + TM, lo:lo + POOL_GROUP_DIM]
            pooled = win_sums[gi][HALO + r0:HALO + r0 + TM] / jnp.minimum(pos, float(w)) - tok
            mg = jnp.dot(pooled.astype(jnp.bfloat16), pw_ref[gi],
                         preferred_element_type=jnp.float32)
            mixed.append((mg * ps_ref[:, lo:lo + POOL_GROUP_DIM]).astype(jnp.bfloat16))
        y_cats.append(jnp.concatenate([ym_ref[rows, :]] + mixed, axis=1))

    all_logits = []
    for sub, rows in enumerate(subs):
        r0 = sub * TM
        x1 = x_ref[rows, :] + jnp.dot(y_cats[sub], wo_ref[...], preferred_element_type=jnp.float32)
        x1_ref[rows, :] = x1
        h2 = x1 * lax.rsqrt(jnp.mean(x1 * x1, axis=-1, keepdims=True) + EPS) * g2_ref[...]
        h2b = h2.astype(jnp.bfloat16)
        for s in range(SLAB):
            h2_ref[pl.ds(r0 * SLAB + s, TM, stride=SLAB), :] = h2[:, s * LANES:(s + 1) * LANES]
        all_logits.append(lax.dot_general(wrt_ref[...], h2b, NT_DIMS,
                                          preferred_element_type=jnp.float32) + br_ref[...])

    for sub, rows in enumerate(subs):
        work = all_logits[sub]
        vals, ids, hots = [], [], []
        for _ in range(TOP_K):
            mk = jnp.max(work, axis=0, keepdims=True)
            ik = jnp.min(jnp.where(work == mk, e_id, float(N_EXPERTS)), axis=0, keepdims=True)
            hot = e_id == ik
            work = jnp.where(hot, -jnp.inf, work)
            vals.append(mk)
            ids.append(ik)
            hots.append(hot)
        ex = [jnp.exp(vk - vals[0]) for vk in vals]
        denom = ex[0] + ex[1] + ex[2] + ex[3]
        gate_ref[:, rows] = jnp.concatenate([e / denom for e in ex], axis=0)
        idx_ref[:, rows] = jnp.concatenate(ids, axis=0).astype(jnp.int32)

        sel_f = sum(jnp.where(hot, 1.0, 0.0) for hot in hots)
        prefix = jnp.dot(sel_f.astype(jnp.bfloat16), before, preferred_element_type=jnp.float32)
        rank_e = carry[:, 0:1] + prefix
        ranks = [jnp.sum(jnp.where(hot, rank_e, 0.0), axis=0, keepdims=True) for hot in hots]
        rank_ref[:, rows] = jnp.concatenate(ranks, axis=0).astype(jnp.int32)
        carry = carry + jnp.sum(sel_f, axis=1, keepdims=True)
    carry_ref[...] = carry
    cnt_ref[...] = carry.astype(jnp.int32)


def _out_route(x2, ym, p, pool_w, pool_s, w_out, g2, wr_t, br, seq):
    T = x2.shape[0]
    TM = ROUTE_SUB * TM_PROJ
    nt = T // TM
    u_blk = N_MAIN // POOL_WIDTH - 1
    halo_per_tile = TM // HALO
    tok_spec = pl.BlockSpec((TOP_K, TM), lambda i: (0, i))
    return pl.pallas_call(
        functools.partial(_out_route_kernel, seq),
        grid=(nt,),
        in_specs=[
            pl.BlockSpec((TM, D_MODEL), lambda i: (i, 0)),
            pl.BlockSpec((TM, MLSTM_WIDTH), lambda i: (i, 0)),
            pl.BlockSpec((TM, POOL_WIDTH), lambda i: (i, u_blk)),
            pl.BlockSpec((HALO, POOL_WIDTH),
                         lambda i: (jnp.maximum(i * halo_per_tile - 1, 0), u_blk)),
            pl.BlockSpec((len(POOL_WINDOWS), POOL_GROUP_DIM, POOL_GROUP_DIM), lambda i: (0, 0, 0)),
            pl.BlockSpec((1, POOL_WIDTH), lambda i: (0, 0)),
            pl.BlockSpec((D_MODEL, D_MODEL), lambda i: (0, 0)),
            pl.BlockSpec((1, D_MODEL), lambda i: (0, 0)),
            pl.BlockSpec((N_EXPERTS, D_MODEL), lambda i: (0, 0)),
            pl.BlockSpec((N_EXPERTS, 1), lambda i: (0, 0)),
        ],
        out_specs=[
            pl.BlockSpec((TM, D_MODEL), lambda i: (i, 0)),
            pl.BlockSpec((TM * SLAB, LANES), lambda i: (i, 0)),
            tok_spec, tok_spec, tok_spec,
            pl.BlockSpec((N_EXPERTS, LANES), lambda i: (0, 0)),
        ],
        out_shape=[
            jax.ShapeDtypeStruct((T, D_MODEL), jnp.float32),
            jax.ShapeDtypeStruct((T * SLAB, LANES), jnp.float32),
            jax.ShapeDtypeStruct((TOP_K, T), jnp.int32),
            jax.ShapeDtypeStruct((TOP_K, T), jnp.float32),
            jax.ShapeDtypeStruct((TOP_K, T), jnp.int32),
            jax.ShapeDtypeStruct((N_EXPERTS, LANES), jnp.int32),
        ],
        scratch_shapes=[
            pltpu.VMEM((N_EXPERTS, LANES), jnp.float32),
        ],
        compiler_params=pltpu.CompilerParams(
            dimension_semantics=("arbitrary",), vmem_limit_bytes=VMEM_LIMIT),
        name="out_route",
    )(x2, ym, p, p, pool_w, pool_s, w_out, g2, wr_t, br)


def _plan(dest_flat, fill):
    n_assign = dest_flat.shape[0]
    n_table = fill.shape[0]
    mesh = plsc.VectorSubcoreMesh(core_axis_name="c", subcore_axis_name="s")

    @pl.kernel(out_type=jax.ShapeDtypeStruct((n_table,), jnp.int32), mesh=mesh,
               scratch_types=[pltpu.VMEM((n_table,), jnp.int32),
                              pltpu.VMEM((PLAN_CHUNK,), jnp.int32)],
               compiler_params=pltpu.CompilerParams(needs_layout_passes=False))
    def plan_kernel(dest_hbm, fill_hbm, out_hbm, table, chunk):
        first = jnp.logical_and(lax.axis_index("c") == 0, lax.axis_index("s") == 0)

        @pl.when(first)
        def _():
            pltpu.sync_copy(fill_hbm, table)

            @pl.loop(0, n_assign // PLAN_CHUNK)
            def _(ci):
                pltpu.sync_copy(dest_hbm.at[pl.ds(ci * PLAN_CHUNK, PLAN_CHUNK)], chunk)

                @pl.loop(0, PLAN_CHUNK // (SC_LANES * PLAN_UNROLL))
                def _(i):
                    for j in range(PLAN_UNROLL):
                        off = (i * PLAN_UNROLL + j) * SC_LANES
                        idx = chunk[pl.ds(off, SC_LANES)]
                        vals = (ci * PLAN_CHUNK + off
                                + lax.broadcasted_iota(jnp.int32, (SC_LANES,), 0))
                        plsc.store_scatter(table, [idx], vals)

            pltpu.sync_copy(table, out_hbm)

    return plan_kernel(dest_flat, fill)


def _expert_kernel(n_tok, bs_ref, slot_ref, h2_ref, wg_ref, bg_ref, wu_ref, bu_ref, wd_ref, bd_ref,
                   yt_ref, xg0_ref, xg1_ref, xg2_ref, ys0_ref, ys1_ref, ys2_ref,
                   wgb_ref, wub_ref, wdb_ref, gsem, ssem):
    TM = TM_EXPERT
    ROWS = TM * SLAB
    YROWS = TM * YSLAB
    e = pl.program_id(0)
    n_total = bs_ref[N_EXPERTS]
    xg = (xg0_ref, xg1_ref, xg2_ref)
    ys = (ys0_ref, ys1_ref, ys2_ref)

    def token_of(a):
        return a & (n_tok - 1) if n_tok & (n_tok - 1) == 0 else lax.rem(a, n_tok)

    def start_gather(blk, par):
        base = (blk + 1) * TM
        for r in range(TM):
            t = token_of(slot_ref[base + r])
            pltpu.make_async_copy(h2_ref.at[pl.ds(pl.multiple_of(t * SLAB, SLAB), SLAB), :],
                                  xg[par].at[pl.ds(r * SLAB, SLAB), :], gsem.at[par]).start()

    def wait_gather(par):
        pltpu.make_async_copy(h2_ref.at[pl.ds(0, ROWS), :], xg[0], gsem.at[par]).wait()

    def start_scatter(blk, par):
        base = (blk + 1) * TM
        for r in range(TM):
            a = slot_ref[base + r]
            pltpu.make_async_copy(ys[par].at[pl.ds(r * YSLAB, YSLAB), :],
                                  yt_ref.at[pl.ds(pl.multiple_of(a * YSLAB, YSLAB), YSLAB), :],
                                  ssem.at[par]).start()

    def wait_scatter(par):
        pltpu.make_async_copy(ys[0], yt_ref.at[pl.ds(0, YROWS), :], ssem.at[par]).wait()

    @pl.when(e == 0)
    def _():
        start_gather(0, 0)
        start_gather(1, 1)
        for par in range(NBUF):
            ys[par][...] = jnp.zeros_like(ys[par])
            dump = yt_ref.at[pl.ds((n_tok * TOP_K + par * TM) * YSLAB, YROWS), :]
            cp = pltpu.make_async_copy(ys[par], dump, ssem.at[par])
            cp.start()
            cp.wait()

    wgb_ref[...] = wg_ref[0].astype(jnp.bfloat16)
    wub_ref[...] = wu_ref[0].astype(jnp.bfloat16)
    wdb_ref[...] = wd_ref[0].astype(jnp.bfloat16)

    def block_step(g, par):
        nxt2 = (par + 2) % NBUF
        wait_gather(par)

        @pl.when(g >= 2)
        def _():
            wait_scatter(par)

        start_gather(g + 2, nxt2)
        start_scatter(g - 1, nxt2)
        x = jnp.concatenate(
            [xg[par][pl.ds(s, TM, stride=SLAB), :].astype(jnp.bfloat16) for s in range(SLAB)],
            axis=1)
        gate = jnp.dot(x, wgb_ref[...], preferred_element_type=jnp.float32) + bg_ref[0]
        up = jnp.dot(x, wub_ref[...], preferred_element_type=jnp.float32) + bu_ref[0]
        gate = jnp.minimum(gate, SWIGLU_LIMIT)
        up = jnp.clip(up, -SWIGLU_LIMIT, SWIGLU_LIMIT)
        glu = gate * _sigmoid(SWIGLU_ALPHA * gate)
        act = (glu * (up + 1.0)).astype(jnp.bfloat16)
        y = jnp.dot(act, wdb_ref[...], preferred_element_type=jnp.float32) + bd_ref[0]
        lo = pltpu.bitcast(y[:, :D_MODEL // 2].astype(jnp.bfloat16).astype(jnp.float32), jnp.uint32)
        hi = pltpu.bitcast(y[:, D_MODEL // 2:].astype(jnp.bfloat16).astype(jnp.float32), jnp.uint32)
        packed = (lo >> 16) | (hi & jnp.uint32(0xFFFF0000))
        for s in range(YSLAB):
            ys[par][pl.ds(s, TM, stride=YSLAB), :] = packed[:, s * LANES:(s + 1) * LANES]

    def body(g, carry):
        for par in range(NBUF):
            pl.when(g % NBUF == par)(functools.partial(block_step, g, par))
        return carry

    lax.fori_loop(bs_ref[e], bs_ref[e + 1], body, 0)

    @pl.when(e == N_EXPERTS - 1)
    def _():
        g = n_total
        for par in range(NBUF):
            @pl.when((g - 1) % NBUF == par)
            def _():
                start_scatter(g - 1, par)
        wait_gather(g % NBUF)
        wait_gather((g + 1) % NBUF)
        wait_scatter((g - 1) % NBUF)
        wait_scatter((g + 1) % NBUF)

        @pl.when(g >= 2)
        def _():
            wait_scatter(g % NBUF)


def _experts(block_start, slot_buf, h2_slab, w_gate, b_gate, w_up, b_up, w_down, b_down, n_tok):
    TM = TM_EXPERT
    n_assign = n_tok * TOP_K
    w_spec = pl.BlockSpec((1, D_MODEL, D_FF), lambda e, bs, sl: (e, 0, 0))
    bias_spec = pl.BlockSpec((1, 1, D_FF), lambda e, bs, sl: (e, 0, 0))
    buf = pltpu.VMEM((TM * SLAB, LANES), jnp.float32)
    ybuf = pltpu.VMEM((TM * YSLAB, LANES), jnp.uint32)
    grid_spec = pltpu.PrefetchScalarGridSpec(
        num_scalar_prefetch=2,
        grid=(N_EXPERTS,),
        in_specs=[
            pl.BlockSpec(memory_space=pl.ANY),
            w_spec, bias_spec, w_spec, bias_spec, w_spec, bias_spec,
        ],
        out_specs=pl.BlockSpec(memory_space=pl.ANY),
        scratch_shapes=[
            buf, buf, buf, ybuf, ybuf, ybuf,
            pltpu.VMEM((D_MODEL, D_FF), jnp.bfloat16),
            pltpu.VMEM((D_MODEL, D_FF), jnp.bfloat16),
            pltpu.VMEM((D_FF, D_MODEL), jnp.bfloat16),
            pltpu.SemaphoreType.DMA((NBUF,)),
            pltpu.SemaphoreType.DMA((NBUF,)),
        ],
    )
    return pl.pallas_call(
        functools.partial(_expert_kernel, n_tok),
        grid_spec=grid_spec,
        out_shape=jax.ShapeDtypeStruct(((n_assign + NBUF * TM) * YSLAB, LANES), jnp.uint32),
        compiler_params=pltpu.CompilerParams(
            dimension_semantics=("arbitrary",), vmem_limit_bytes=VMEM_LIMIT),
        name="experts",
    )(block_start, slot_buf, h2_slab, w_gate, b_gate, w_up, b_up, w_down, b_down)


def _combine_kernel(normalize, x1_ref, y0_ref, y1_ref, y2_ref, y3_ref, gate_ref, g_ref, o_ref):
    TM = TM_PROJ
    gates = jnp.concatenate([gate_ref[...], jnp.zeros((8 - TOP_K, TM), jnp.float32)], axis=0)
    g_cols = jnp.transpose(gates)
    g_bc = [jnp.broadcast_to(g_cols[:, k:k + 1], (TM, LANES)) for k in range(TOP_K)]
    ssq = jnp.zeros((TM, LANES), jnp.float32)
    parts = [x1_ref[:, s * LANES:(s + 1) * LANES] for s in range(SLAB)]
    for s in range(YSLAB):
        for k, y_ref in enumerate((y0_ref, y1_ref, y2_ref, y3_ref)):
            w = y_ref[pl.ds(s, TM, stride=YSLAB), :]
            lo = pltpu.bitcast(w << 16, jnp.float32)
            hi = pltpu.bitcast(w & jnp.uint32(0xFFFF0000), jnp.float32)
            parts[s] = parts[s] + g_bc[k] * lo
            parts[YSLAB + s] = parts[YSLAB + s] + g_bc[k] * hi
    for acc in parts:
        ssq = ssq + acc * acc
    if normalize:
        inv = lax.rsqrt(jnp.sum(ssq, axis=-1, keepdims=True) * (1.0 / D_MODEL) + EPS)
        for s in range(SLAB):
            o_ref[:, s * LANES:(s + 1) * LANES] = parts[s] * inv * g_ref[:, s * LANES:(s + 1) * LANES]
    else:
        for s in range(SLAB):
            o_ref[:, s * LANES:(s + 1) * LANES] = parts[s]


def _combine(x1, y_tok, gate_t, gf, normalize):
    T = x1.shape[0]
    TM = TM_PROJ
    nt = T // TM

    def y_spec(k):
        return pl.BlockSpec((TM * YSLAB, LANES), lambda i: (k * nt + i, 0))

    return pl.pallas_call(
        functools.partial(_combine_kernel, normalize),
        grid=(nt,),
        in_specs=[
            pl.BlockSpec((TM, D_MODEL), lambda i: (i, 0)),
            y_spec(0), y_spec(1), y_spec(2), y_spec(3),
            pl.BlockSpec((TOP_K, TM), lambda i: (0, i)),
            pl.BlockSpec((1, D_MODEL), lambda i: (0, 0)),
        ],
        out_specs=pl.BlockSpec((TM, D_MODEL), lambda i: (i, 0)),
        out_shape=jax.ShapeDtypeStruct((T, D_MODEL), jnp.float32),
        compiler_params=pltpu.CompilerParams(
            dimension_semantics=("parallel",), vmem_limit_bytes=VMEM_LIMIT),
        name="combine",
    )(x1, y_tok, y_tok, y_tok, y_tok, gate_t, gf)


def kernel(x, norm1_g, w_in, ig_b, fg_b, conv_w, head_norm_g, pool_w, pool_scale, w_out, norm2_g,
           w_router, b_router, w_gate, b_gate, w_up, b_up, w_down, b_down, normf_g):
    B, S, D = x.shape
    T = B * S
    depth = norm1_g.shape[0]
    W = MLSTM_WIDTH
    f32, bf16 = jnp.float32, jnp.bfloat16

    L = CHUNK
    t_l = lax.broadcasted_iota(jnp.int32, (L, L), 0)
    t_r = lax.broadcasted_iota(jnp.int32, (L, L), 1)
    tri = (t_r <= t_l).astype(f32)
    shifts = jnp.stack([(t_l - t_r == CONV_WIDTH - 1 - j).astype(bf16)
                        for j in range(CONV_WIDTH - 1)])
    h_t = lax.broadcasted_iota(jnp.int32, (8, HALO), 0)
    h_r = lax.broadcasted_iota(jnp.int32, (8, HALO), 1)
    halo_shifts = jnp.stack([(h_r - HALO - h_t == -(CONV_WIDTH - 1 - j)).astype(bf16)
                             for j in range(CONV_WIDTH - 1)])

    n_assign = T * TOP_K
    n_blocks = -(-n_assign // TM_EXPERT) + N_EXPERTS
    n_rows = n_blocks * TM_EXPERT
    n_table = n_rows + 3 * TM_EXPERT
    fill = n_assign + ((jnp.arange(n_table, dtype=jnp.int32) + (NBUF - 1) * TM_EXPERT)
                       % (NBUF * TM_EXPERT))
    x2 = x.reshape(T, D)
    for l in range(depth):
        w = w_in[l]
        w_main = jnp.concatenate([w[:, :4 * W], w[:, 4 * W + N_GATES:]], axis=1).astype(bf16)
        wg_t = jnp.zeros((BF16_SUBLANES, D), bf16).at[:N_GATES].set(
            w[:, 4 * W:4 * W + N_GATES].T.astype(bf16))
        p, gates_t = _in_proj(x2, norm1_g[l][None, :], w_main, wg_t)

        gate_b = jnp.concatenate([ig_b[l], fg_b[l]])[:, None].astype(f32)
        gates_b = gates_t.reshape(N_GATES, B, S).transpose(1, 0, 2)
        ym = _mlstm(p.reshape(B, S, N_MAIN), gates_b, conv_w[l].astype(f32), gate_b,
                    head_norm_g[l][None, :], tri, shifts, halo_shifts).reshape(T, W)

        x1, h2, idx_t, gate_t, rank_t, cnt = _out_route(
            x2, ym, p, pool_w[l].astype(bf16), pool_scale[l][None, :], w_out[l].astype(bf16),
            norm2_g[l][None, :], w_router[l].T.astype(bf16), b_router[l][:, None], S)

        counts = cnt[:, 0]
        padded = ((counts + TM_EXPERT - 1) // TM_EXPERT) * TM_EXPERT
        padded_end = jnp.cumsum(padded)
        padded_start = padded_end - padded
        expert_ids = jnp.arange(N_EXPERTS, dtype=jnp.int32)[:, None, None]
        start_of = jnp.sum(jnp.where(idx_t[None] == expert_ids, padded_start[:, None, None], 0), axis=0)
        dest = start_of + rank_t
        block_start = jnp.concatenate(
            [jnp.zeros((1,), jnp.int32), (padded_end // TM_EXPERT).astype(jnp.int32)])

        slot_buf = _plan(dest.reshape(-1) + TM_EXPERT, fill)
        y_tok = _experts(block_start, slot_buf, h2, w_gate[l], b_gate[l][:, None, :],
                         w_up[l], b_up[l][:, None, :], w_down[l], b_down[l][:, None, :], T)
        last = l + 1 == depth
        x2 = _combine(x1, y_tok, gate_t, normf_g[None, :], last)
    return x2.reshape(B, S, D)
```

```python
import functools

import jax
import jax.numpy as jnp
from jax import lax
from jax.experimental import pallas as pl
from jax.experimental.pallas import tpu as pltpu
from jax.experimental.pallas import tpu_sc as plsc

D_MODEL = 1024
MLSTM_WIDTH = 512
MLSTM_HEADS = 4
HEAD_DIM = 128
CONV_WIDTH = 4
POOL_WIDTH = 512
POOL_WINDOWS = (2, 4, 8, 16)
POOL_GROUP_DIM = 128
N_EXPERTS = 32
TOP_K = 4
D_FF = 1024
SWIGLU_LIMIT = 7.0
SWIGLU_ALPHA = 1.702
EPS = 1e-5

N_MAIN = 4 * MLSTM_WIDTH + POOL_WIDTH
N_GATES = 2 * MLSTM_HEADS

LANES = 128
BF16_SUBLANES = 16
VMEM_LIMIT = 56 * 1024 * 1024

TM_PROJ = 512
ROUTE_SUB = 2
CHUNK = 256
MLSTM_BATCH = 2
HALO = 16
TM_EXPERT = 256
NBUF = 3
ROW_DMA_PRIORITY = 1
SLAB = D_MODEL // LANES
YSLAB = SLAB // 2
PLAN_CHUNK = 8192
SC_LANES = 16
PLAN_UNROLL = 8

NT_DIMS = (((1,), (1,)), ((), ()))


def _sigmoid(x):
    return 1.0 / (1.0 + jnp.exp(-x))


def _in_proj_kernel(x_ref, g_ref, w_ref, wgt_ref, p_ref, gt_ref):
    x = x_ref[...]
    h = x * lax.rsqrt(jnp.mean(x * x, axis=-1, keepdims=True) + EPS) * g_ref[...]
    hb = h.astype(jnp.bfloat16)
    p_ref[...] = jnp.dot(hb, w_ref[...], preferred_element_type=jnp.float32).astype(p_ref.dtype)
    gt = lax.dot_general(wgt_ref[...], hb, NT_DIMS, preferred_element_type=jnp.float32)
    gt_ref[...] = gt[:N_GATES]


def _in_proj(x2, g1, w_main, wg_t):
    T = x2.shape[0]
    return pl.pallas_call(
        _in_proj_kernel,
        grid=(T // TM_PROJ,),
        in_specs=[
            pl.BlockSpec((TM_PROJ, D_MODEL), lambda i: (i, 0)),
            pl.BlockSpec((1, D_MODEL), lambda i: (0, 0)),
            pl.BlockSpec((D_MODEL, N_MAIN), lambda i: (0, 0)),
            pl.BlockSpec((BF16_SUBLANES, D_MODEL), lambda i: (0, 0)),
        ],
        out_specs=[
            pl.BlockSpec((TM_PROJ, N_MAIN), lambda i: (i, 0)),
            pl.BlockSpec((N_GATES, TM_PROJ), lambda i: (0, i)),
        ],
        out_shape=[
            jax.ShapeDtypeStruct((T, N_MAIN), jnp.bfloat16),
            jax.ShapeDtypeStruct((N_GATES, T), jnp.float32),
        ],
        compiler_params=pltpu.CompilerParams(
            dimension_semantics=("parallel",), vmem_limit_bytes=VMEM_LIMIT),
        name="in_proj",
    )(x2, g1, w_main, wg_t)


def _mlstm_kernel(qk_ref, qkp_ref, v_ref, o_ref, gt_ref, convw_ref, gb_ref, hng_ref,
                  tri_ref, shift_ref, hshift_ref, y_ref, cn_ref, m_ref):
    L = CHUNK
    c = pl.program_id(1)

    @pl.when(c == 0)
    def _():
        cn_ref[...] = jnp.zeros_like(cn_ref)
        m_ref[...] = jnp.zeros_like(m_ref)

    row_id = lax.broadcasted_iota(jnp.int32, (L, L), 0)
    col_id = lax.broadcasted_iota(jnp.int32, (L, L), 1)
    causal = col_id <= row_id
    ones_blk = jnp.ones((L, HEAD_DIM), jnp.bfloat16)
    lane = lax.broadcasted_iota(jnp.int32, (MLSTM_HEADS, L), 1)

    gate_terms = []
    for bb in range(MLSTM_BATCH):
        gt = gt_ref[bb] + gb_ref[...]
        f = gt[MLSTM_HEADS:]
        lf = jnp.minimum(f, 0.0) - jnp.log(1.0 + jnp.exp(-jnp.abs(f)))
        ig = gt[:MLSTM_HEADS]
        b_rows = lax.dot_general(lf, tri_ref[...], NT_DIMS, precision=lax.Precision.HIGHEST,
                                 preferred_element_type=jnp.float32)
        c_rows = ig - b_rows
        cm_rows = c_rows
        d = 1
        while d < L:
            cm_rows = jnp.maximum(
                cm_rows, jnp.where(lane >= d, pltpu.roll(cm_rows, d, axis=1), -jnp.inf))
            d *= 2
        gate_terms.append((b_rows, c_rows, cm_rows))

    conv_terms = []
    for bb in range(MLSTM_BATCH):
        x_cur = qk_ref[bb]
        x_prev = jnp.where(c > 0, qkp_ref[bb], jnp.zeros((HALO, 2 * MLSTM_WIDTH), jnp.bfloat16))
        acc = convw_ref[CONV_WIDTH - 1:CONV_WIDTH, :] * x_cur.astype(jnp.float32)
        for j in range(CONV_WIDTH - 1):
            sh = jnp.dot(shift_ref[j], x_cur, preferred_element_type=jnp.float32)
            top = sh[:8] + jnp.dot(hshift_ref[j], x_prev, preferred_element_type=jnp.float32)
            sh = jnp.concatenate([top, sh[8:]], axis=0)
            acc = acc + convw_ref[j:j + 1, :] * sh
        qk = acc * _sigmoid(acc)
        q_all = qk[:, :MLSTM_WIDTH].astype(jnp.bfloat16)
        k_t = jnp.transpose(qk[:, MLSTM_WIDTH:] * (HEAD_DIM ** -0.5))
        conv_terms.append((q_all, k_t))

    for bb in range(MLSTM_BATCH):
        b_rows, c_rows, cm_rows = gate_terms[bb]
        q_all, k_t = conv_terms[bb]
        m_in4 = jnp.concatenate(
            [m_ref[bb * MLSTM_HEADS + h][0:1, 0:1] for h in range(MLSTM_HEADS)], axis=0)
        mx_rows = jnp.maximum(cm_rows, m_in4)
        inter_rows = jnp.exp(m_in4 - mx_rows)
        einv_rows = jnp.exp(-(b_rows + mx_rows))
        fac_t = jnp.transpose(jnp.concatenate(
            [mx_rows, inter_rows, einv_rows, jnp.zeros_like(mx_rows)], axis=0))

        for h in range(MLSTM_HEADS):
            lo = h * HEAD_DIM
            st = bb * MLSTM_HEADS + h
            q = q_all[:, lo:lo + HEAD_DIM]
            kt = k_t[lo:lo + HEAD_DIM, :]
            v_ext = jnp.concatenate([v_ref[bb, :, lo:lo + HEAD_DIM], ones_blk], axis=1)
            mx_col = fac_t[:, h:h + 1]
            inter_col = fac_t[:, MLSTM_HEADS + h:MLSTM_HEADS + h + 1]
            einv_col = fac_t[:, 2 * MLSTM_HEADS + h:2 * MLSTM_HEADS + h + 1]
            c_row = c_rows[h:h + 1, :]
            b_tot = b_rows[h:h + 1, L - 1:L]
            cm_tot = cm_rows[h:h + 1, L - 1:L]
            m_in = m_ref[st][0:1, 0:1]
            cn = cn_ref[st]

            s_qk = jnp.dot(q, kt.astype(jnp.bfloat16), preferred_element_type=jnp.float32)
            s = (s_qk * jnp.exp(jnp.where(causal, c_row - mx_col, -jnp.inf))).astype(jnp.bfloat16)
            num = (jnp.dot(s, v_ext, preferred_element_type=jnp.float32)
                   + inter_col * jnp.dot(q, cn.astype(jnp.bfloat16),
                                         preferred_element_type=jnp.float32))
            den = num[:, HEAD_DIM:]
            hh = num[:, :HEAD_DIM] / jnp.maximum(jnp.abs(den), einv_col)

            mu = jnp.mean(hh, axis=-1, keepdims=True)
            dv = hh - mu
            var = jnp.mean(dv * dv, axis=-1, keepdims=True)
            hn = dv * lax.rsqrt(var + EPS) * hng_ref[:, lo:lo + HEAD_DIM]
            og = _sigmoid(o_ref[bb, :, lo:lo + HEAD_DIM].astype(jnp.float32))
            y_ref[bb, :, lo:lo + HEAD_DIM] = (og * hn).astype(y_ref.dtype)

            m_loc = b_tot + cm_tot
            kw_t = (kt * jnp.exp(c_row - cm_tot)).astype(jnp.bfloat16)
            c_loc = jnp.dot(kw_t, v_ext, preferred_element_type=jnp.float32)
            m_new = jnp.maximum(b_tot + m_in, m_loc)
            s_old = jnp.exp(b_tot + m_in - m_new)
            s_loc = jnp.exp(m_loc - m_new)
            cn_ref[st] = s_old * cn + s_loc * c_loc
            m_ref[st] = jnp.broadcast_to(m_new, m_ref.shape[1:])


def _mlstm(p3, gates_b, conv_w, gate_b, hn_g, tri, shifts, halo_shifts):
    batch, seq, _ = p3.shape
    L = CHUNK
    BB = MLSTM_BATCH
    halo_per_chunk = L // HALO
    return pl.pallas_call(
        _mlstm_kernel,
        grid=(batch // BB, seq // L),
        in_specs=[
            pl.BlockSpec((BB, L, 2 * MLSTM_WIDTH), lambda bi, ci: (bi, ci, 0)),
            pl.BlockSpec((BB, HALO, 2 * MLSTM_WIDTH),
                         lambda bi, ci: (bi, jnp.maximum(ci * halo_per_chunk - 1, 0), 0)),
            pl.BlockSpec((BB, L, MLSTM_WIDTH), lambda bi, ci: (bi, ci, 2)),
            pl.BlockSpec((BB, L, MLSTM_WIDTH), lambda bi, ci: (bi, ci, 3)),
            pl.BlockSpec((BB, N_GATES, L), lambda bi, ci: (bi, 0, ci)),
            pl.BlockSpec((CONV_WIDTH, 2 * MLSTM_WIDTH), lambda bi, ci: (0, 0)),
            pl.BlockSpec((N_GATES, 1), lambda bi, ci: (0, 0)),
            pl.BlockSpec((1, MLSTM_WIDTH), lambda bi, ci: (0, 0)),
            pl.BlockSpec((L, L), lambda bi, ci: (0, 0)),
            pl.BlockSpec((CONV_WIDTH - 1, L, L), lambda bi, ci: (0, 0, 0)),
            pl.BlockSpec((CONV_WIDTH - 1, 8, HALO), lambda bi, ci: (0, 0, 0)),
        ],
        out_specs=pl.BlockSpec((BB, L, MLSTM_WIDTH), lambda bi, ci: (bi, ci, 0)),
        out_shape=jax.ShapeDtypeStruct((batch, seq, MLSTM_WIDTH), jnp.bfloat16),
        scratch_shapes=[
            pltpu.VMEM((BB * MLSTM_HEADS, HEAD_DIM, 2 * HEAD_DIM), jnp.float32),
            pltpu.VMEM((BB * MLSTM_HEADS, 8, LANES), jnp.float32),
        ],
        compiler_params=pltpu.CompilerParams(
            dimension_semantics=("parallel", "arbitrary"), vmem_limit_bytes=VMEM_LIMIT),
        name="mlstm",
    )(p3, p3, p3, p3, gates_b, conv_w, gate_b, hn_g, tri, shifts, halo_shifts)


def _out_route_kernel(seq, x_ref, ym_ref, u_ref, up_ref, pw_ref, ps_ref, wo_ref, g2_ref,
                      wrt_ref, br_ref, x1_ref, h2_ref, idx_ref, gate_ref, rank_ref, cnt_ref,
                      carry_ref):
    TM = TM_PROJ
    R = ROUTE_SUB * TM
    i = pl.program_id(0)

    @pl.when(i == 0)
    def _():
        carry_ref[...] = jnp.zeros_like(carry_ref)

    pos0 = (i * R) % seq
    e_id = lax.broadcasted_iota(jnp.int32, (N_EXPERTS, TM), 0).astype(jnp.float32)
    t_row = lax.broadcasted_iota(jnp.int32, (TM, TM), 0)
    t_col = lax.broadcasted_iota(jnp.int32, (TM, TM), 1)
    before = jnp.where(t_row < t_col, 1.0, 0.0).astype(jnp.bfloat16)
    carry = carry_ref[...]
    subs = [slice(sub * TM, (sub + 1) * TM) for sub in range(ROUTE_SUB)]

    halo = jnp.where(pos0 > 0, up_ref[...].astype(jnp.float32), 0.0)
    u_ext = jnp.concatenate([halo, u_ref[...].astype(jnp.float32)], axis=0)
    win_sums = []
    for gi, w in enumerate(POOL_WINDOWS):
        sw = u_ext[:, gi * POOL_GROUP_DIM:(gi + 1) * POOL_GROUP_DIM]
        span = 1
        while span < w:
            sw = sw + pltpu.roll(sw, span, axis=0)
            span *= 2
        win_sums.append(sw)
    y_cats = []
    for sub, rows in enumerate(subs):
        r0 = sub * TM
        pos = (pos0 + r0 + lax.broadcasted_iota(jnp.int32, (TM, 1), 0) + 1).astype(jnp.float32)
        mixed = []
        for gi, w in enumerate(POOL_WINDOWS):
            lo = gi * POOL_GROUP_DIM
            tok = u_ext[HALO + r0:HALO + r0 + TM, lo:lo + POOL_GROUP_DIM]
            pooled = win_sums[gi][HALO + r0:HALO + r0 + TM] / jnp.minimum(pos, float(w)) - tok
            mg = jnp.dot(pooled.astype(jnp.bfloat16), pw_ref[gi],
                         preferred_element_type=jnp.float32)
            mixed.append((mg * ps_ref[:, lo:lo + POOL_GROUP_DIM]).astype(jnp.bfloat16))
        y_cats.append(jnp.concatenate([ym_ref[rows, :]] + mixed, axis=1))

    all_logits = []
    for sub, rows in enumerate(subs):
        r0 = sub * TM
        x1 = x_ref[rows, :] + jnp.dot(y_cats[sub], wo_ref[...], preferred_element_type=jnp.float32)
        x1_ref[rows, :] = x1
        h2 = x1 * lax.rsqrt(jnp.mean(x1 * x1, axis=-1, keepdims=True) + EPS) * g2_ref[...]
        h2b = h2.astype(jnp.bfloat16)
        for s in range(SLAB):
            h2_ref[pl.ds(r0 * SLAB + s, TM, stride=SLAB), :] = h2[:, s * LANES:(s + 1) * LANES]
        all_logits.append(lax.dot_general(wrt_ref[...], h2b, NT_DIMS,
                                          preferred_element_type=jnp.float32) + br_ref[...])

    for sub, rows in enumerate(subs):
        work = all_logits[sub]
        vals, ids, hots = [], [], []
        for _ in range(TOP_K):
            mk = jnp.max(work, axis=0, keepdims=True)
            ik = jnp.min(jnp.where(work == mk, e_id, float(N_EXPERTS)), axis=0, keepdims=True)
            hot = e_id == ik
            work = jnp.where(hot, -jnp.inf, work)
            vals.append(mk)
            ids.append(ik)
            hots.append(hot)
        ex = [jnp.exp(vk - vals[0]) for vk in vals]
        denom = ex[0] + ex[1] + ex[2] + ex[3]
        gate_ref[:, rows] = jnp.concatenate([e / denom for e in ex], axis=0)
        idx_ref[:, rows] = jnp.concatenate(ids, axis=0).astype(jnp.int32)

        sel_f = sum(jnp.where(hot, 1.0, 0.0) for hot in hots)
        prefix = jnp.dot(sel_f.astype(jnp.bfloat16), before, preferred_element_type=jnp.float32)
        rank_e = carry[:, 0:1] + prefix
        ranks = [jnp.sum(jnp.where(hot, rank_e, 0.0), axis=0, keepdims=True) for hot in hots]
        rank_ref[:, rows] = jnp.concatenate(ranks, axis=0).astype(jnp.int32)
        carry = carry + jnp.sum(sel_f, axis=1, keepdims=True)
    carry_ref[...] = carry
    cnt_ref[...] = carry.astype(jnp.int32)


def _out_route(x2, ym, p, pool_w, pool_s, w_out, g2, wr_t, br, seq):
    T = x2.shape[0]
    TM = ROUTE_SUB * TM_PROJ
    nt = T // TM
    u_blk = N_MAIN // POOL_WIDTH - 1
    halo_per_tile = TM // HALO
    tok_spec = pl.BlockSpec((TOP_K, TM), lambda i: (0, i))
    return pl.pallas_call(
        functools.partial(_out_route_kernel, seq),
        grid=(nt,),
        in_specs=[
            pl.BlockSpec((TM, D_MODEL), lambda i: (i, 0)),
            pl.BlockSpec((TM, MLSTM_WIDTH), lambda i: (i, 0)),
            pl.BlockSpec((TM, POOL_WIDTH), lambda i: (i, u_blk)),
            pl.BlockSpec((HALO, POOL_WIDTH),
                         lambda i: (jnp.maximum(i * halo_per_tile - 1, 0), u_blk)),
            pl.BlockSpec((len(POOL_WINDOWS), POOL_GROUP_DIM, POOL_GROUP_DIM), lambda i: (0, 0, 0)),
            pl.BlockSpec((1, POOL_WIDTH), lambda i: (0, 0)),
            pl.BlockSpec((D_MODEL, D_MODEL), lambda i: (0, 0)),
            pl.BlockSpec((1, D_MODEL), lambda i: (0, 0)),
            pl.BlockSpec((N_EXPERTS, D_MODEL), lambda i: (0, 0)),
            pl.BlockSpec((N_EXPERTS, 1), lambda i: (0, 0)),
        ],
        out_specs=[
            pl.BlockSpec((TM, D_MODEL), lambda i: (i, 0)),
            pl.BlockSpec((TM * SLAB, LANES), lambda i: (i, 0)),
            tok_spec, tok_spec, tok_spec,
            pl.BlockSpec((N_EXPERTS, LANES), lambda i: (0, 0)),
        ],
        out_shape=[
            jax.ShapeDtypeStruct((T, D_MODEL), jnp.float32),
            jax.ShapeDtypeStruct((T * SLAB, LANES), jnp.float32),
            jax.ShapeDtypeStruct((TOP_K, T), jnp.int32),
            jax.ShapeDtypeStruct((TOP_K, T), jnp.float32),
            jax.ShapeDtypeStruct((TOP_K, T), jnp.int32),
            jax.ShapeDtypeStruct((N_EXPERTS, LANES), jnp.int32),
        ],
        scratch_shapes=[
            pltpu.VMEM((N_EXPERTS, LANES), jnp.float32),
        ],
        compiler_params=pltpu.CompilerParams(
            dimension_semantics=("arbitrary",), vmem_limit_bytes=VMEM_LIMIT),
        name="out_route",
    )(x2, ym, p, p, pool_w, pool_s, w_out, g2, wr_t, br)


def _plan(dest_flat, fill):
    n_assign = dest_flat.shape[0]
    n_table = fill.shape[0]
    mesh = plsc.VectorSubcoreMesh(core_axis_name="c", subcore_axis_name="s")

    @pl.kernel(out_type=jax.ShapeDtypeStruct((n_table,), jnp.int32), mesh=mesh,
               scratch_types=[pltpu.VMEM((n_table,), jnp.int32),
                              pltpu.VMEM((PLAN_CHUNK,), jnp.int32)],
               compiler_params=pltpu.CompilerParams(needs_layout_passes=False))
    def plan_kernel(dest_hbm, fill_hbm, out_hbm, table, chunk):
        first = jnp.logical_and(lax.axis_index("c") == 0, lax.axis_index("s") == 0)

        @pl.when(first)
        def _():
            pltpu.sync_copy(fill_hbm, table)

            @pl.loop(0, n_assign // PLAN_CHUNK)
            def _(ci):
                pltpu.sync_copy(dest_hbm.at[pl.ds(ci * PLAN_CHUNK, PLAN_CHUNK)], chunk)

                @pl.loop(0, PLAN_CHUNK // (SC_LANES * PLAN_UNROLL))
                def _(i):
                    for j in range(PLAN_UNROLL):
                        off = (i * PLAN_UNROLL + j) * SC_LANES
                        idx = chunk[pl.ds(off, SC_LANES)]
                        vals = (ci * PLAN_CHUNK + off
                                + lax.broadcasted_iota(jnp.int32, (SC_LANES,), 0))
                        plsc.store_scatter(table, [idx], vals)

            pltpu.sync_copy(table, out_hbm)

    return plan_kernel(dest_flat, fill)


def _expert_kernel(n_tok, bs_ref, slot_ref, h2_ref, wg_ref, bg_ref, wu_ref, bu_ref, wd_ref, bd_ref,
                   yt_ref, xg0_ref, xg1_ref, xg2_ref, ys0_ref, ys1_ref, ys2_ref,
                   wgb_ref, wub_ref, wdb_ref, gsem, ssem):
    TM = TM_EXPERT
    ROWS = TM * SLAB
    YROWS = TM * YSLAB
    e = pl.program_id(0)
    n_total = bs_ref[N_EXPERTS]
    xg = (xg0_ref, xg1_ref, xg2_ref)
    ys = (ys0_ref, ys1_ref, ys2_ref)

    def token_of(a):
        return a & (n_tok - 1) if n_tok & (n_tok - 1) == 0 else lax.rem(a, n_tok)

    def start_gather(blk, par):
        base = (blk + 1) * TM
        for r in range(TM):
            t = token_of(slot_ref[base + r])
            pltpu.make_async_copy(h2_ref.at[pl.ds(pl.multiple_of(t * SLAB, SLAB), SLAB), :],
                                  xg[par].at[pl.ds(r * SLAB, SLAB), :], gsem.at[par]
                                  ).start(priority=ROW_DMA_PRIORITY)

    def wait_gather(par):
        pltpu.make_async_copy(h2_ref.at[pl.ds(0, ROWS), :], xg[0], gsem.at[par]).wait()

    def start_scatter(blk, par):
        base = (blk + 1) * TM
        for r in range(TM):
            a = slot_ref[base + r]
            pltpu.make_async_copy(ys[par].at[pl.ds(r * YSLAB, YSLAB), :],
                                  yt_ref.at[pl.ds(pl.multiple_of(a * YSLAB, YSLAB), YSLAB), :],
                                  ssem.at[par]).start(priority=ROW_DMA_PRIORITY)

    def wait_scatter(par):
        pltpu.make_async_copy(ys[0], yt_ref.at[pl.ds(0, YROWS), :], ssem.at[par]).wait()

    @pl.when(e == 0)
    def _():
        start_gather(0, 0)
        start_gather(1, 1)
        for par in range(NBUF):
            ys[par][...] = jnp.zeros_like(ys[par])
            dump = yt_ref.at[pl.ds((n_tok * TOP_K + par * TM) * YSLAB, YROWS), :]
            cp = pltpu.make_async_copy(ys[par], dump, ssem.at[par])
            cp.start()
            cp.wait()

    wgb_ref[...] = wg_ref[0].astype(jnp.bfloat16)
    wub_ref[...] = wu_ref[0].astype(jnp.bfloat16)
    wdb_ref[...] = wd_ref[0].astype(jnp.bfloat16)

    def block_step(g, par):
        nxt2 = (par + 2) % NBUF
        wait_gather(par)

        @pl.when(g >= 2)
        def _():
            wait_scatter(par)

        start_gather(g + 2, nxt2)
        start_scatter(g - 1, nxt2)
        x = jnp.concatenate(
            [xg[par][pl.ds(s, TM, stride=SLAB), :].astype(jnp.bfloat16) for s in range(SLAB)],
            axis=1)
        gate = jnp.dot(x, wgb_ref[...], preferred_element_type=jnp.float32) + bg_ref[0]
        up = jnp.dot(x, wub_ref[...], preferred_element_type=jnp.float32) + bu_ref[0]
        gate = jnp.minimum(gate, SWIGLU_LIMIT)
        up = jnp.clip(up, -SWIGLU_LIMIT, SWIGLU_LIMIT)
        glu = gate * _sigmoid(SWIGLU_ALPHA * gate)
        act = (glu * (up + 1.0)).astype(jnp.bfloat16)
        y = jnp.dot(act, wdb_ref[...], preferred_element_type=jnp.float32) + bd_ref[0]
        lo = pltpu.bitcast(y[:, :D_MODEL // 2].astype(jnp.bfloat16).astype(jnp.float32), jnp.uint32)
        hi = pltpu.bitcast(y[:, D_MODEL // 2:].astype(jnp.bfloat16).astype(jnp.float32), jnp.uint32)
        packed = (lo >> 16) | (hi & jnp.uint32(0xFFFF0000))
        for s in range(YSLAB):
            ys[par][pl.ds(s, TM, stride=YSLAB), :] = packed[:, s * LANES:(s + 1) * LANES]

    def body(g, carry):
        for par in range(NBUF):
            pl.when(g % NBUF == par)(functools.partial(block_step, g, par))
        return carry

    lax.fori_loop(bs_ref[e], bs_ref[e + 1], body, 0)

    @pl.when(e == N_EXPERTS - 1)
    def _():
        g = n_total
        for par in range(NBUF):
            @pl.when((g - 1) % NBUF == par)
            def _():
                start_scatter(g - 1, par)
        wait_gather(g % NBUF)
        wait_gather((g + 1) % NBUF)
        wait_scatter((g - 1) % NBUF)
        wait_scatter((g + 1) % NBUF)

        @pl.when(g >= 2)
        def _():
            wait_scatter(g % NBUF)


def _experts(block_start, slot_buf, h2_slab, w_gate, b_gate, w_up, b_up, w_down, b_down, n_tok):
    TM = TM_EXPERT
    n_assign = n_tok * TOP_K
    w_spec = pl.BlockSpec((1, D_MODEL, D_FF), lambda e, bs, sl: (e, 0, 0))
    bias_spec = pl.BlockSpec((1, 1, D_FF), lambda e, bs, sl: (e, 0, 0))
    buf = pltpu.VMEM((TM * SLAB, LANES), jnp.float32)
    ybuf = pltpu.VMEM((TM * YSLAB, LANES), jnp.uint32)
    grid_spec = pltpu.PrefetchScalarGridSpec(
        num_scalar_prefetch=2,
        grid=(N_EXPERTS,),
        in_specs=[
            pl.BlockSpec(memory_space=pl.ANY),
            w_spec, bias_spec, w_spec, bias_spec, w_spec, bias_spec,
        ],
        out_specs=pl.BlockSpec(memory_space=pl.ANY),
        scratch_shapes=[
            buf, buf, buf, ybuf, ybuf, ybuf,
            pltpu.VMEM((D_MODEL, D_FF), jnp.bfloat16),
            pltpu.VMEM((D_MODEL, D_FF), jnp.bfloat16),
            pltpu.VMEM((D_FF, D_MODEL), jnp.bfloat16),
            pltpu.SemaphoreType.DMA((NBUF,)),
            pltpu.SemaphoreType.DMA((NBUF,)),
        ],
    )
    return pl.pallas_call(
        functools.partial(_expert_kernel, n_tok),
        grid_spec=grid_spec,
        out_shape=jax.ShapeDtypeStruct(((n_assign + NBUF * TM) * YSLAB, LANES), jnp.uint32),
        compiler_params=pltpu.CompilerParams(
            dimension_semantics=("arbitrary",), vmem_limit_bytes=VMEM_LIMIT),
        name="experts",
    )(block_start, slot_buf, h2_slab, w_gate, b_gate, w_up, b_up, w_down, b_down)


def _combine_kernel(normalize, x1_ref, y0_ref, y1_ref, y2_ref, y3_ref, gate_ref, g_ref, o_ref):
    TM = TM_PROJ
    gates = jnp.concatenate([gate_ref[...], jnp.zeros((8 - TOP_K, TM), jnp.float32)], axis=0)
    g_cols = jnp.transpose(gates)
    g_bc = [jnp.broadcast_to(g_cols[:, k:k + 1], (TM, LANES)) for k in range(TOP_K)]
    ssq = jnp.zeros((TM, LANES), jnp.float32)
    parts = [x1_ref[:, s * LANES:(s + 1) * LANES] for s in range(SLAB)]
    for s in range(YSLAB):
        for k, y_ref in enumerate((y0_ref, y1_ref, y2_ref, y3_ref)):
            w = y_ref[pl.ds(s, TM, stride=YSLAB), :]
            lo = pltpu.bitcast(w << 16, jnp.float32)
            hi = pltpu.bitcast(w & jnp.uint32(0xFFFF0000), jnp.float32)
            parts[s] = parts[s] + g_bc[k] * lo
            parts[YSLAB + s] = parts[YSLAB + s] + g_bc[k] * hi
    for acc in parts:
        ssq = ssq + acc * acc
    if normalize:
        inv = lax.rsqrt(jnp.sum(ssq, axis=-1, keepdims=True) * (1.0 / D_MODEL) + EPS)
        for s in range(SLAB):
            o_ref[:, s * LANES:(s + 1) * LANES] = parts[s] * inv * g_ref[:, s * LANES:(s + 1) * LANES]
    else:
        for s in range(SLAB):
            o_ref[:, s * LANES:(s + 1) * LANES] = parts[s]


def _combine(x1, y_tok, gate_t, gf, normalize):
    T = x1.shape[0]
    TM = TM_PROJ
    nt = T // TM

    def y_spec(k):
        return pl.BlockSpec((TM * YSLAB, LANES), lambda i: (k * nt + i, 0))

    return pl.pallas_call(
        functools.partial(_combine_kernel, normalize),
        grid=(nt,),
        in_specs=[
            pl.BlockSpec((TM, D_MODEL), lambda i: (i, 0)),
            y_spec(0), y_spec(1), y_spec(2), y_spec(3),
            pl.BlockSpec((TOP_K, TM), lambda i: (0, i)),
            pl.BlockSpec((1, D_MODEL), lambda i: (0, 0)),
        ],
        out_specs=pl.BlockSpec((TM, D_MODEL), lambda i: (i, 0)),
        out_shape=jax.ShapeDtypeStruct((T, D_MODEL), jnp.float32),
        compiler_params=pltpu.CompilerParams(
            dimension_semantics=("parallel",), vmem_limit_bytes=VMEM_LIMIT),
        name="combine",
    )(x1, y_tok, y_tok, y_tok, y_tok, gate_t, gf)


def kernel(x, norm1_g, w_in, ig_b, fg_b, conv_w, head_norm_g, pool_w, pool_scale, w_out, norm2_g,
           w_router, b_router, w_gate, b_gate, w_up, b_up, w_down, b_down, normf_g):
    B, S, D = x.shape
    T = B * S
    depth = norm1_g.shape[0]
    W = MLSTM_WIDTH
    f32, bf16 = jnp.float32, jnp.bfloat16

    L = CHUNK
    t_l = lax.broadcasted_iota(jnp.int32, (L, L), 0)
    t_r = lax.broadcasted_iota(jnp.int32, (L, L), 1)
    tri = (t_r <= t_l).astype(f32)
    shifts = jnp.stack([(t_l - t_r == CONV_WIDTH - 1 - j).astype(bf16)
                        for j in range(CONV_WIDTH - 1)])
    h_t = lax.broadcasted_iota(jnp.int32, (8, HALO), 0)
    h_r = lax.broadcasted_iota(jnp.int32, (8, HALO), 1)
    halo_shifts = jnp.stack([(h_r - HALO - h_t == -(CONV_WIDTH - 1 - j)).astype(bf16)
                             for j in range(CONV_WIDTH - 1)])

    n_assign = T * TOP_K
    n_blocks = -(-n_assign // TM_EXPERT) + N_EXPERTS
    n_rows = n_blocks * TM_EXPERT
    n_table = n_rows + 3 * TM_EXPERT
    fill = n_assign + ((jnp.arange(n_table, dtype=jnp.int32) + (NBUF - 1) * TM_EXPERT)
                       % (NBUF * TM_EXPERT))
    x2 = x.reshape(T, D)
    for l in range(depth):
        w = w_in[l]
        w_main = jnp.concatenate([w[:, :4 * W], w[:, 4 * W + N_GATES:]], axis=1).astype(bf16)
        wg_t = jnp.zeros((BF16_SUBLANES, D), bf16).at[:N_GATES].set(
            w[:, 4 * W:4 * W + N_GATES].T.astype(bf16))
        p, gates_t = _in_proj(x2, norm1_g[l][None, :], w_main, wg_t)

        gate_b = jnp.concatenate([ig_b[l], fg_b[l]])[:, None].astype(f32)
        gates_b = gates_t.reshape(N_GATES, B, S).transpose(1, 0, 2)
        ym = _mlstm(p.reshape(B, S, N_MAIN), gates_b, conv_w[l].astype(f32), gate_b,
                    head_norm_g[l][None, :], tri, shifts, halo_shifts).reshape(T, W)

        x1, h2, idx_t, gate_t, rank_t, cnt = _out_route(
            x2, ym, p, pool_w[l].astype(bf16), pool_scale[l][None, :], w_out[l].astype(bf16),
            norm2_g[l][None, :], w_router[l].T.astype(bf16), b_router[l][:, None], S)

        counts = cnt[:, 0]
        padded = ((counts + TM_EXPERT - 1) // TM_EXPERT) * TM_EXPERT
        padded_end = jnp.cumsum(padded)
        padded_start = padded_end - padded
        expert_ids = jnp.arange(N_EXPERTS, dtype=jnp.int32)[:, None, None]
        start_of = jnp.sum(jnp.where(idx_t[None] == expert_ids, padded_start[:, None, None], 0), axis=0)
        dest = start_of + rank_t
        block_start = jnp.concatenate(
            [jnp.zeros((1,), jnp.int32), (padded_end // TM_EXPERT).astype(jnp.int32)])

        slot_buf = _plan(dest.reshape(-1) + TM_EXPERT, fill)
        y_tok = _experts(block_start, slot_buf, h2, w_gate[l], b_gate[l][:, None, :],
                         w_up[l], b_up[l][:, None, :], w_down[l], b_down[l][:, None, :], T)
        last = l + 1 == depth
        x2 = _combine(x1, y_tok, gate_t, normf_g[None, :], last)
    return x2.reshape(B, S, D)
```

```python
import functools

import jax
import jax.numpy as jnp
from jax import lax
from jax.experimental import pallas as pl
from jax.experimental.pallas import tpu as pltpu
from jax.experimental.pallas import tpu_sc as plsc

D_MODEL = 1024
MLSTM_WIDTH = 512
MLSTM_HEADS = 4
HEAD_DIM = 128
CONV_WIDTH = 4
POOL_WIDTH = 512
POOL_WINDOWS = (2, 4, 8, 16)
POOL_GROUP_DIM = 128
N_EXPERTS = 32
TOP_K = 4
D_FF = 1024
SWIGLU_LIMIT = 7.0
SWIGLU_ALPHA = 1.702
EPS = 1e-5

N_MAIN = 4 * MLSTM_WIDTH + POOL_WIDTH
N_GATES = 2 * MLSTM_HEADS

LANES = 128
BF16_SUBLANES = 16
VMEM_LIMIT = 56 * 1024 * 1024

TM_PROJ = 512
ROUTE_SUB = 2
CHUNK = 256
MLSTM_BATCH = 2
HALO = 16
TM_EXPERT = 256
NBUF = 4
ROW_DMA_PRIORITY = 1
SLAB = D_MODEL // LANES
YSLAB = SLAB // 2
PLAN_CHUNK = 8192
SC_LANES = 16
PLAN_UNROLL = 8

NT_DIMS = (((1,), (1,)), ((), ()))


def _sigmoid(x):
    return 1.0 / (1.0 + jnp.exp(-x))


def _in_proj_kernel(x_ref, g_ref, w_ref, wgt_ref, p_ref, gt_ref):
    x = x_ref[...]
    h = x * lax.rsqrt(jnp.mean(x * x, axis=-1, keepdims=True) + EPS) * g_ref[...]
    hb = h.astype(jnp.bfloat16)
    p_ref[...] = jnp.dot(hb, w_ref[...], preferred_element_type=jnp.float32).astype(p_ref.dtype)
    gt = lax.dot_general(wgt_ref[...], hb, NT_DIMS, preferred_element_type=jnp.float32)
    gt_ref[...] = gt[:N_GATES]


def _in_proj(x2, g1, w_main, wg_t):
    T = x2.shape[0]
    return pl.pallas_call(
        _in_proj_kernel,
        grid=(T // TM_PROJ,),
        in_specs=[
            pl.BlockSpec((TM_PROJ, D_MODEL), lambda i: (i, 0)),
            pl.BlockSpec((1, D_MODEL), lambda i: (0, 0)),
            pl.BlockSpec((D_MODEL, N_MAIN), lambda i: (0, 0)),
            pl.BlockSpec((BF16_SUBLANES, D_MODEL), lambda i: (0, 0)),
        ],
        out_specs=[
            pl.BlockSpec((TM_PROJ, N_MAIN), lambda i: (i, 0)),
            pl.BlockSpec((N_GATES, TM_PROJ), lambda i: (0, i)),
        ],
        out_shape=[
            jax.ShapeDtypeStruct((T, N_MAIN), jnp.bfloat16),
            jax.ShapeDtypeStruct((N_GATES, T), jnp.float32),
        ],
        compiler_params=pltpu.CompilerParams(
            dimension_semantics=("parallel",), vmem_limit_bytes=VMEM_LIMIT),
        name="in_proj",
    )(x2, g1, w_main, wg_t)


def _mlstm_kernel(qk_ref, qkp_ref, v_ref, o_ref, gt_ref, convw_ref, gb_ref, hng_ref,
                  tri_ref, shift_ref, hshift_ref, y_ref, cn_ref, m_ref):
    L = CHUNK
    c = pl.program_id(1)

    @pl.when(c == 0)
    def _():
        cn_ref[...] = jnp.zeros_like(cn_ref)
        m_ref[...] = jnp.zeros_like(m_ref)

    row_id = lax.broadcasted_iota(jnp.int32, (L, L), 0)
    col_id = lax.broadcasted_iota(jnp.int32, (L, L), 1)
    causal = col_id <= row_id
    ones_blk = jnp.ones((L, HEAD_DIM), jnp.bfloat16)
    lane = lax.broadcasted_iota(jnp.int32, (MLSTM_HEADS, L), 1)

    gate_terms = []
    for bb in range(MLSTM_BATCH):
        gt = gt_ref[bb] + gb_ref[...]
        f = gt[MLSTM_HEADS:]
        lf = jnp.minimum(f, 0.0) - jnp.log(1.0 + jnp.exp(-jnp.abs(f)))
        ig = gt[:MLSTM_HEADS]
        b_rows = lax.dot_general(lf, tri_ref[...], NT_DIMS, precision=lax.Precision.HIGHEST,
                                 preferred_element_type=jnp.float32)
        c_rows = ig - b_rows
        cm_rows = c_rows
        d = 1
        while d < L:
            cm_rows = jnp.maximum(
                cm_rows, jnp.where(lane >= d, pltpu.roll(cm_rows, d, axis=1), -jnp.inf))
            d *= 2
        gate_terms.append((b_rows, c_rows, cm_rows))

    conv_terms = []
    for bb in range(MLSTM_BATCH):
        x_cur = qk_ref[bb]
        x_prev = jnp.where(c > 0, qkp_ref[bb], jnp.zeros((HALO, 2 * MLSTM_WIDTH), jnp.bfloat16))
        acc = convw_ref[CONV_WIDTH - 1:CONV_WIDTH, :] * x_cur.astype(jnp.float32)
        for j in range(CONV_WIDTH - 1):
            sh = jnp.dot(shift_ref[j], x_cur, preferred_element_type=jnp.float32)
            top = sh[:8] + jnp.dot(hshift_ref[j], x_prev, preferred_element_type=jnp.float32)
            sh = jnp.concatenate([top, sh[8:]], axis=0)
            acc = acc + convw_ref[j:j + 1, :] * sh
        qk = acc * _sigmoid(acc)
        q_all = qk[:, :MLSTM_WIDTH].astype(jnp.bfloat16)
        k_t = jnp.transpose(qk[:, MLSTM_WIDTH:] * (HEAD_DIM ** -0.5))
        conv_terms.append((q_all, k_t))

    for bb in range(MLSTM_BATCH):
        b_rows, c_rows, cm_rows = gate_terms[bb]
        q_all, k_t = conv_terms[bb]
        m_in4 = jnp.concatenate(
            [m_ref[bb * MLSTM_HEADS + h][0:1, 0:1] for h in range(MLSTM_HEADS)], axis=0)
        mx_rows = jnp.maximum(cm_rows, m_in4)
        inter_rows = jnp.exp(m_in4 - mx_rows)
        einv_rows = jnp.exp(-(b_rows + mx_rows))
        fac_t = jnp.transpose(jnp.concatenate(
            [mx_rows, inter_rows, einv_rows, jnp.zeros_like(mx_rows)], axis=0))

        for h in range(MLSTM_HEADS):
            lo = h * HEAD_DIM
            st = bb * MLSTM_HEADS + h
            q = q_all[:, lo:lo + HEAD_DIM]
            kt = k_t[lo:lo + HEAD_DIM, :]
            v_ext = jnp.concatenate([v_ref[bb, :, lo:lo + HEAD_DIM], ones_blk], axis=1)
            mx_col = fac_t[:, h:h + 1]
            inter_col = fac_t[:, MLSTM_HEADS + h:MLSTM_HEADS + h + 1]
            einv_col = fac_t[:, 2 * MLSTM_HEADS + h:2 * MLSTM_HEADS + h + 1]
            c_row = c_rows[h:h + 1, :]
            b_tot = b_rows[h:h + 1, L - 1:L]
            cm_tot = cm_rows[h:h + 1, L - 1:L]
            m_in = m_ref[st][0:1, 0:1]
            cn = cn_ref[st]

            s_qk = jnp.dot(q, kt.astype(jnp.bfloat16), preferred_element_type=jnp.float32)
            s = (s_qk * jnp.exp(jnp.where(causal, c_row - mx_col, -jnp.inf))).astype(jnp.bfloat16)
            num = (jnp.dot(s, v_ext, preferred_element_type=jnp.float32)
                   + inter_col * jnp.dot(q, cn.astype(jnp.bfloat16),
                                         preferred_element_type=jnp.float32))
            den = num[:, HEAD_DIM:]
            hh = num[:, :HEAD_DIM] / jnp.maximum(jnp.abs(den), einv_col)

            mu = jnp.mean(hh, axis=-1, keepdims=True)
            dv = hh - mu
            var = jnp.mean(dv * dv, axis=-1, keepdims=True)
            hn = dv * lax.rsqrt(var + EPS) * hng_ref[:, lo:lo + HEAD_DIM]
            og = _sigmoid(o_ref[bb, :, lo:lo + HEAD_DIM].astype(jnp.float32))
            y_ref[bb, :, lo:lo + HEAD_DIM] = (og * hn).astype(y_ref.dtype)

            m_loc = b_tot + cm_tot
            kw_t = (kt * jnp.exp(c_row - cm_tot)).astype(jnp.bfloat16)
            c_loc = jnp.dot(kw_t, v_ext, preferred_element_type=jnp.float32)
            m_new = jnp.maximum(b_tot + m_in, m_loc)
            s_old = jnp.exp(b_tot + m_in - m_new)
            s_loc = jnp.exp(m_loc - m_new)
            cn_ref[st] = s_old * cn + s_loc * c_loc
            m_ref[st] = jnp.broadcast_to(m_new, m_ref.shape[1:])


def _mlstm(p3, gates_b, conv_w, gate_b, hn_g, tri, shifts, halo_shifts):
    batch, seq, _ = p3.shape
    L = CHUNK
    BB = MLSTM_BATCH
    halo_per_chunk = L // HALO
    return pl.pallas_call(
        _mlstm_kernel,
        grid=(batch // BB, seq // L),
        in_specs=[
            pl.BlockSpec((BB, L, 2 * MLSTM_WIDTH), lambda bi, ci: (bi, ci, 0)),
            pl.BlockSpec((BB, HALO, 2 * MLSTM_WIDTH),
                         lambda bi, ci: (bi, jnp.maximum(ci * halo_per_chunk - 1, 0), 0)),
            pl.BlockSpec((BB, L, MLSTM_WIDTH), lambda bi, ci: (bi, ci, 2)),
            pl.BlockSpec((BB, L, MLSTM_WIDTH), lambda bi, ci: (bi, ci, 3)),
            pl.BlockSpec((BB, N_GATES, L), lambda bi, ci: (bi, 0, ci)),
            pl.BlockSpec((CONV_WIDTH, 2 * MLSTM_WIDTH), lambda bi, ci: (0, 0)),
            pl.BlockSpec((N_GATES, 1), lambda bi, ci: (0, 0)),
            pl.BlockSpec((1, MLSTM_WIDTH), lambda bi, ci: (0, 0)),
            pl.BlockSpec((L, L), lambda bi, ci: (0, 0)),
            pl.BlockSpec((CONV_WIDTH - 1, L, L), lambda bi, ci: (0, 0, 0)),
            pl.BlockSpec((CONV_WIDTH - 1, 8, HALO), lambda bi, ci: (0, 0, 0)),
        ],
        out_specs=pl.BlockSpec((BB, L, MLSTM_WIDTH), lambda bi, ci: (bi, ci, 0)),
        out_shape=jax.ShapeDtypeStruct((batch, seq, MLSTM_WIDTH), jnp.bfloat16),
        scratch_shapes=[
            pltpu.VMEM((BB * MLSTM_HEADS, HEAD_DIM, 2 * HEAD_DIM), jnp.float32),
            pltpu.VMEM((BB * MLSTM_HEADS, 8, LANES), jnp.float32),
        ],
        compiler_params=pltpu.CompilerParams(
            dimension_semantics=("parallel", "arbitrary"), vmem_limit_bytes=VMEM_LIMIT),
        name="mlstm",
    )(p3, p3, p3, p3, gates_b, conv_w, gate_b, hn_g, tri, shifts, halo_shifts)


def _out_route_kernel(seq, x_ref, ym_ref, u_ref, up_ref, pw_ref, ps_ref, wo_ref, g2_ref,
                      wrt_ref, br_ref, x1_ref, h2_ref, idx_ref, gate_ref, rank_ref, cnt_ref,
                      carry_ref):
    TM = TM_PROJ
    R = ROUTE_SUB * TM
    i = pl.program_id(0)

    @pl.when(i == 0)
    def _():
        carry_ref[...] = jnp.zeros_like(carry_ref)

    pos0 = (i * R) % seq
    e_id = lax.broadcasted_iota(jnp.int32, (N_EXPERTS, TM), 0).astype(jnp.float32)
    t_row = lax.broadcasted_iota(jnp.int32, (TM, TM), 0)
    t_col = lax.broadcasted_iota(jnp.int32, (TM, TM), 1)
    before = jnp.where(t_row < t_col, 1.0, 0.0).astype(jnp.bfloat16)
    carry = carry_ref[...]
    subs = [slice(sub * TM, (sub + 1) * TM) for sub in range(ROUTE_SUB)]

    halo = jnp.where(pos0 > 0, up_ref[...].astype(jnp.float32), 0.0)
    u_ext = jnp.concatenate([halo, u_ref[...].astype(jnp.float32)], axis=0)
    win_sums = []
    for gi, w in enumerate(POOL_WINDOWS):
        sw = u_ext[:, gi * POOL_GROUP_DIM:(gi + 1) * POOL_GROUP_DIM]
        span = 1
        while span < w:
            sw = sw + pltpu.roll(sw, span, axis=0)
            span *= 2
        win_sums.append(sw)
    y_cats = []
    for sub, rows in enumerate(subs):
        r0 = sub * TM
        pos = (pos0 + r0 + lax.broadcasted_iota(jnp.int32, (TM, 1), 0) + 1).astype(jnp.float32)
        mixed = []
        for gi, w in enumerate(POOL_WINDOWS):
            lo = gi * POOL_GROUP_DIM
            tok = u_ext[HALO + r0:HALO + r0 + TM, lo:lo + POOL_GROUP_DIM]
            pooled = win_sums[gi][HALO + r0:HALO + r0 + TM] / jnp.minimum(pos, float(w)) - tok
            mg = jnp.dot(pooled.astype(jnp.bfloat16), pw_ref[gi],
                         preferred_element_type=jnp.float32)
            mixed.append((mg * ps_ref[:, lo:lo + POOL_GROUP_DIM]).astype(jnp.bfloat16))
        y_cats.append(jnp.concatenate([ym_ref[rows, :]] + mixed, axis=1))

    all_logits = []
    for sub, rows in enumerate(subs):
        r0 = sub * TM
        x1 = x_ref[rows, :] + jnp.dot(y_cats[sub], wo_ref[...], preferred_element_type=jnp.float32)
        x1_ref[rows, :] = x1
        h2 = x1 * lax.rsqrt(jnp.mean(x1 * x1, axis=-1, keepdims=True) + EPS) * g2_ref[...]
        h2b = h2.astype(jnp.bfloat16)
        for s in range(SLAB):
            h2_ref[pl.ds(r0 * SLAB + s, TM, stride=SLAB), :] = h2[:, s * LANES:(s + 1) * LANES]
        all_logits.append(lax.dot_general(wrt_ref[...], h2b, NT_DIMS,
                                          preferred_element_type=jnp.float32) + br_ref[...])

    for sub, rows in enumerate(subs):
        work = all_logits[sub]
        vals, ids, hots = [], [], []
        for _ in range(TOP_K):
            mk = jnp.max(work, axis=0, keepdims=True)
            ik = jnp.min(jnp.where(work == mk, e_id, float(N_EXPERTS)), axis=0, keepdims=True)
            hot = e_id == ik
            work = jnp.where(hot, -jnp.inf, work)
            vals.append(mk)
            ids.append(ik)
            hots.append(hot)
        ex = [jnp.exp(vk - vals[0]) for vk in vals]
        denom = ex[0] + ex[1] + ex[2] + ex[3]
        gate_ref[:, rows] = jnp.concatenate([e / denom for e in ex], axis=0)
        idx_ref[:, rows] = jnp.concatenate(ids, axis=0).astype(jnp.int32)

        sel_f = sum(jnp.where(hot, 1.0, 0.0) for hot in hots)
        prefix = jnp.dot(sel_f.astype(jnp.bfloat16), before, preferred_element_type=jnp.float32)
        rank_e = carry[:, 0:1] + prefix
        ranks = [jnp.sum(jnp.where(hot, rank_e, 0.0), axis=0, keepdims=True) for hot in hots]
        rank_ref[:, rows] = jnp.concatenate(ranks, axis=0).astype(jnp.int32)
        carry = carry + jnp.sum(sel_f, axis=1, keepdims=True)
    carry_ref[...] = carry
    cnt_ref[...] = carry.astype(jnp.int32)


def _out_route(x2, ym, p, pool_w, pool_s, w_out, g2, wr_t, br, seq):
    T = x2.shape[0]
    TM = ROUTE_SUB * TM_PROJ
    nt = T // TM
    u_blk = N_MAIN // POOL_WIDTH - 1
    halo_per_tile = TM // HALO
    tok_spec = pl.BlockSpec((TOP_K, TM), lambda i: (0, i))
    return pl.pallas_call(
        functools.partial(_out_route_kernel, seq),
        grid=(nt,),
        in_specs=[
            pl.BlockSpec((TM, D_MODEL), lambda i: (i, 0)),
            pl.BlockSpec((TM, MLSTM_WIDTH), lambda i: (i, 0)),
            pl.BlockSpec((TM, POOL_WIDTH), lambda i: (i, u_blk)),
            pl.BlockSpec((HALO, POOL_WIDTH),
                         lambda i: (jnp.maximum(i * halo_per_tile - 1, 0), u_blk)),
            pl.BlockSpec((len(POOL_WINDOWS), POOL_GROUP_DIM, POOL_GROUP_DIM), lambda i: (0, 0, 0)),
            pl.BlockSpec((1, POOL_WIDTH), lambda i: (0, 0)),
            pl.BlockSpec((D_MODEL, D_MODEL), lambda i: (0, 0)),
            pl.BlockSpec((1, D_MODEL), lambda i: (0, 0)),
            pl.BlockSpec((N_EXPERTS, D_MODEL), lambda i: (0, 0)),
            pl.BlockSpec((N_EXPERTS, 1), lambda i: (0, 0)),
        ],
        out_specs=[
            pl.BlockSpec((TM, D_MODEL), lambda i: (i, 0)),
            pl.BlockSpec((TM * SLAB, LANES), lambda i: (i, 0)),
            tok_spec, tok_spec, tok_spec,
            pl.BlockSpec((N_EXPERTS, LANES), lambda i: (0, 0)),
        ],
        out_shape=[
            jax.ShapeDtypeStruct((T, D_MODEL), jnp.float32),
            jax.ShapeDtypeStruct((T * SLAB, LANES), jnp.float32),
            jax.ShapeDtypeStruct((TOP_K, T), jnp.int32),
            jax.ShapeDtypeStruct((TOP_K, T), jnp.float32),
            jax.ShapeDtypeStruct((TOP_K, T), jnp.int32),
            jax.ShapeDtypeStruct((N_EXPERTS, LANES), jnp.int32),
        ],
        scratch_shapes=[
            pltpu.VMEM((N_EXPERTS, LANES), jnp.float32),
        ],
        compiler_params=pltpu.CompilerParams(
            dimension_semantics=("arbitrary",), vmem_limit_bytes=VMEM_LIMIT),
        name="out_route",
    )(x2, ym, p, p, pool_w, pool_s, w_out, g2, wr_t, br)


def _plan(dest_flat, fill):
    n_assign = dest_flat.shape[0]
    n_table = fill.shape[0]
    mesh = plsc.VectorSubcoreMesh(core_axis_name="c", subcore_axis_name="s")

    @pl.kernel(out_type=jax.ShapeDtypeStruct((n_table,), jnp.int32), mesh=mesh,
               scratch_types=[pltpu.VMEM((n_table,), jnp.int32),
                              pltpu.VMEM((PLAN_CHUNK,), jnp.int32)],
               compiler_params=pltpu.CompilerParams(needs_layout_passes=False))
    def plan_kernel(dest_hbm, fill_hbm, out_hbm, table, chunk):
        first = jnp.logical_and(lax.axis_index("c") == 0, lax.axis_index("s") == 0)

        @pl.when(first)
        def _():
            pltpu.sync_copy(fill_hbm, table)

            @pl.loop(0, n_assign // PLAN_CHUNK)
            def _(ci):
                pltpu.sync_copy(dest_hbm.at[pl.ds(ci * PLAN_CHUNK, PLAN_CHUNK)], chunk)

                @pl.loop(0, PLAN_CHUNK // (SC_LANES * PLAN_UNROLL))
                def _(i):
                    for j in range(PLAN_UNROLL):
                        off = (i * PLAN_UNROLL + j) * SC_LANES
                        idx = chunk[pl.ds(off, SC_LANES)]
                        vals = (ci * PLAN_CHUNK + off
                                + lax.broadcasted_iota(jnp.int32, (SC_LANES,), 0))
                        plsc.store_scatter(table, [idx], vals)

            pltpu.sync_copy(table, out_hbm)

    return plan_kernel(dest_flat, fill)


def _expert_kernel(n_tok, bs_ref, slot_ref, h2_ref, wg_ref, bg_ref, wu_ref, bu_ref, wd_ref, bd_ref,
                   yt_ref, *scratch):
    TM = TM_EXPERT
    ROWS = TM * SLAB
    YROWS = TM * YSLAB
    e = pl.program_id(0)
    n_total = bs_ref[N_EXPERTS]
    xg = scratch[:NBUF]
    ys = scratch[NBUF:2 * NBUF]
    wgb_ref, wub_ref, wdb_ref, gsem, ssem = scratch[2 * NBUF:]

    def token_of(a):
        return a & (n_tok - 1) if n_tok & (n_tok - 1) == 0 else lax.rem(a, n_tok)

    def start_gather(blk, par):
        base = (blk + 1) * TM
        for r in range(TM):
            t = token_of(slot_ref[base + r])
            pltpu.make_async_copy(h2_ref.at[pl.ds(pl.multiple_of(t * SLAB, SLAB), SLAB), :],
                                  xg[par].at[pl.ds(r * SLAB, SLAB), :], gsem.at[par]
                                  ).start(priority=ROW_DMA_PRIORITY)

    def wait_gather(par):
        pltpu.make_async_copy(h2_ref.at[pl.ds(0, ROWS), :], xg[0], gsem.at[par]).wait()

    def start_scatter(blk, par):
        base = (blk + 1) * TM
        for r in range(TM):
            a = slot_ref[base + r]
            pltpu.make_async_copy(ys[par].at[pl.ds(r * YSLAB, YSLAB), :],
                                  yt_ref.at[pl.ds(pl.multiple_of(a * YSLAB, YSLAB), YSLAB), :],
                                  ssem.at[par]).start(priority=ROW_DMA_PRIORITY)

    def wait_scatter(par):
        pltpu.make_async_copy(ys[0], yt_ref.at[pl.ds(0, YROWS), :], ssem.at[par]).wait()

    @pl.when(e == 0)
    def _():
        for blk in range(NBUF - 1):
            start_gather(blk, blk)
        for par in range(NBUF):
            ys[par][...] = jnp.zeros_like(ys[par])
            dump = yt_ref.at[pl.ds((n_tok * TOP_K + par * TM) * YSLAB, YROWS), :]
            cp = pltpu.make_async_copy(ys[par], dump, ssem.at[par])
            cp.start()
            cp.wait()

    wgb_ref[...] = wg_ref[0].astype(jnp.bfloat16)
    wub_ref[...] = wu_ref[0].astype(jnp.bfloat16)
    wdb_ref[...] = wd_ref[0].astype(jnp.bfloat16)

    def block_step(g, par):
        prv = (par + NBUF - 1) % NBUF
        wait_gather(par)

        @pl.when(g >= NBUF - 1)
        def _():
            wait_scatter(par)

        start_gather(g + NBUF - 1, prv)
        start_scatter(g - 1, prv)
        x = jnp.concatenate(
            [xg[par][pl.ds(s, TM, stride=SLAB), :].astype(jnp.bfloat16) for s in range(SLAB)],
            axis=1)
        gate = jnp.dot(x, wgb_ref[...], preferred_element_type=jnp.float32) + bg_ref[0]
        up = jnp.dot(x, wub_ref[...], preferred_element_type=jnp.float32) + bu_ref[0]
        gate = jnp.minimum(gate, SWIGLU_LIMIT)
        up = jnp.clip(up, -SWIGLU_LIMIT, SWIGLU_LIMIT)
        glu = gate * _sigmoid(SWIGLU_ALPHA * gate)
        act = (glu * (up + 1.0)).astype(jnp.bfloat16)
        y = jnp.dot(act, wdb_ref[...], preferred_element_type=jnp.float32) + bd_ref[0]
        lo = pltpu.bitcast(y[:, :D_MODEL // 2].astype(jnp.bfloat16).astype(jnp.float32), jnp.uint32)
        hi = pltpu.bitcast(y[:, D_MODEL // 2:].astype(jnp.bfloat16).astype(jnp.float32), jnp.uint32)
        packed = (lo >> 16) | (hi & jnp.uint32(0xFFFF0000))
        for s in range(YSLAB):
            ys[par][pl.ds(s, TM, stride=YSLAB), :] = packed[:, s * LANES:(s + 1) * LANES]

    def body(g, carry):
        for par in range(NBUF):
            pl.when(g % NBUF == par)(functools.partial(block_step, g, par))
        return carry

    lax.fori_loop(bs_ref[e], bs_ref[e + 1], body, 0)

    @pl.when(e == N_EXPERTS - 1)
    def _():
        g = n_total
        for par in range(NBUF):
            @pl.when((g - 1) % NBUF == par)
            def _():
                start_scatter(g - 1, par)
        for j in range(NBUF - 1):
            wait_gather((g + j) % NBUF)
        wait_scatter((g - 1) % NBUF)
        for j in range(2, NBUF + 1):
            @pl.when(g >= j - 1)
            def _():
                wait_scatter((g + NBUF - j) % NBUF)


def _experts(block_start, slot_buf, h2_slab, w_gate, b_gate, w_up, b_up, w_down, b_down, n_tok):
    TM = TM_EXPERT
    n_assign = n_tok * TOP_K
    w_spec = pl.BlockSpec((1, D_MODEL, D_FF), lambda e, bs, sl: (e, 0, 0))
    bias_spec = pl.BlockSpec((1, 1, D_FF), lambda e, bs, sl: (e, 0, 0))
    buf = pltpu.VMEM((TM * SLAB, LANES), jnp.float32)
    ybuf = pltpu.VMEM((TM * YSLAB, LANES), jnp.uint32)
    grid_spec = pltpu.PrefetchScalarGridSpec(
        num_scalar_prefetch=2,
        grid=(N_EXPERTS,),
        in_specs=[
            pl.BlockSpec(memory_space=pl.ANY),
            w_spec, bias_spec, w_spec, bias_spec, w_spec, bias_spec,
        ],
        out_specs=pl.BlockSpec(memory_space=pl.ANY),
        scratch_shapes=[
            *([buf] * NBUF), *([ybuf] * NBUF),
            pltpu.VMEM((D_MODEL, D_FF), jnp.bfloat16),
            pltpu.VMEM((D_MODEL, D_FF), jnp.bfloat16),
            pltpu.VMEM((D_FF, D_MODEL), jnp.bfloat16),
            pltpu.SemaphoreType.DMA((NBUF,)),
            pltpu.SemaphoreType.DMA((NBUF,)),
        ],
    )
    return pl.pallas_call(
        functools.partial(_expert_kernel, n_tok),
        grid_spec=grid_spec,
        out_shape=jax.ShapeDtypeStruct(((n_assign + NBUF * TM) * YSLAB, LANES), jnp.uint32),
        compiler_params=pltpu.CompilerParams(
            dimension_semantics=("arbitrary",), vmem_limit_bytes=VMEM_LIMIT),
        name="experts",
    )(block_start, slot_buf, h2_slab, w_gate, b_gate, w_up, b_up, w_down, b_down)


def _combine_kernel(normalize, x1_ref, y0_ref, y1_ref, y2_ref, y3_ref, gate_ref, g_ref, o_ref):
    TM = TM_PROJ
    gates = jnp.concatenate([gate_ref[...], jnp.zeros((8 - TOP_K, TM), jnp.float32)], axis=0)
    g_cols = jnp.transpose(gates)
    g_bc = [jnp.broadcast_to(g_cols[:, k:k + 1], (TM, LANES)) for k in range(TOP_K)]
    ssq = jnp.zeros((TM, LANES), jnp.float32)
    parts = [x1_ref[:, s * LANES:(s + 1) * LANES] for s in range(SLAB)]
    for s in range(YSLAB):
        for k, y_ref in enumerate((y0_ref, y1_ref, y2_ref, y3_ref)):
            w = y_ref[pl.ds(s, TM, stride=YSLAB), :]
            lo = pltpu.bitcast(w << 16, jnp.float32)
            hi = pltpu.bitcast(w & jnp.uint32(0xFFFF0000), jnp.float32)
            parts[s] = parts[s] + g_bc[k] * lo
            parts[YSLAB + s] = parts[YSLAB + s] + g_bc[k] * hi
    for acc in parts:
        ssq = ssq + acc * acc
    if normalize:
        inv = lax.rsqrt(jnp.sum(ssq, axis=-1, keepdims=True) * (1.0 / D_MODEL) + EPS)
        for s in range(SLAB):
            o_ref[:, s * LANES:(s + 1) * LANES] = parts[s] * inv * g_ref[:, s * LANES:(s + 1) * LANES]
    else:
        for s in range(SLAB):
            o_ref[:, s * LANES:(s + 1) * LANES] = parts[s]


def _combine(x1, y_tok, gate_t, gf, normalize):
    T = x1.shape[0]
    TM = TM_PROJ
    nt = T // TM

    def y_spec(k):
        return pl.BlockSpec((TM * YSLAB, LANES), lambda i: (k * nt + i, 0))

    return pl.pallas_call(
        functools.partial(_combine_kernel, normalize),
        grid=(nt,),
        in_specs=[
            pl.BlockSpec((TM, D_MODEL), lambda i: (i, 0)),
            y_spec(0), y_spec(1), y_spec(2), y_spec(3),
            pl.BlockSpec((TOP_K, TM), lambda i: (0, i)),
            pl.BlockSpec((1, D_MODEL), lambda i: (0, 0)),
        ],
        out_specs=pl.BlockSpec((TM, D_MODEL), lambda i: (i, 0)),
        out_shape=jax.ShapeDtypeStruct((T, D_MODEL), jnp.float32),
        compiler_params=pltpu.CompilerParams(
            dimension_semantics=("parallel",), vmem_limit_bytes=VMEM_LIMIT),
        name="combine",
    )(x1, y_tok, y_tok, y_tok, y_tok, gate_t, gf)


def kernel(x, norm1_g, w_in, ig_b, fg_b, conv_w, head_norm_g, pool_w, pool_scale, w_out, norm2_g,
           w_router, b_router, w_gate, b_gate, w_up, b_up, w_down, b_down, normf_g):
    B, S, D = x.shape
    T = B * S
    depth = norm1_g.shape[0]
    W = MLSTM_WIDTH
    f32, bf16 = jnp.float32, jnp.bfloat16

    L = CHUNK
    t_l = lax.broadcasted_iota(jnp.int32, (L, L), 0)
    t_r = lax.broadcasted_iota(jnp.int32, (L, L), 1)
    tri = (t_r <= t_l).astype(f32)
    shifts = jnp.stack([(t_l - t_r == CONV_WIDTH - 1 - j).astype(bf16)
                        for j in range(CONV_WIDTH - 1)])
    h_t = lax.broadcasted_iota(jnp.int32, (8, HALO), 0)
    h_r = lax.broadcasted_iota(jnp.int32, (8, HALO), 1)
    halo_shifts = jnp.stack([(h_r - HALO - h_t == -(CONV_WIDTH - 1 - j)).astype(bf16)
                             for j in range(CONV_WIDTH - 1)])

    n_assign = T * TOP_K
    n_blocks = -(-n_assign // TM_EXPERT) + N_EXPERTS
    n_rows = n_blocks * TM_EXPERT
    n_table = n_rows + NBUF * TM_EXPERT
    fill = n_assign + ((jnp.arange(n_table, dtype=jnp.int32) + (NBUF - 1) * TM_EXPERT)
                       % (NBUF * TM_EXPERT))
    x2 = x.reshape(T, D)
    for l in range(depth):
        w = w_in[l]
        w_main = jnp.concatenate([w[:, :4 * W], w[:, 4 * W + N_GATES:]], axis=1).astype(bf16)
        wg_t = jnp.zeros((BF16_SUBLANES, D), bf16).at[:N_GATES].set(
            w[:, 4 * W:4 * W + N_GATES].T.astype(bf16))
        p, gates_t = _in_proj(x2, norm1_g[l][None, :], w_main, wg_t)

        gate_b = jnp.concatenate([ig_b[l], fg_b[l]])[:, None].astype(f32)
        gates_b = gates_t.reshape(N_GATES, B, S).transpose(1, 0, 2)
        ym = _mlstm(p.reshape(B, S, N_MAIN), gates_b, conv_w[l].astype(f32), gate_b,
                    head_norm_g[l][None, :], tri, shifts, halo_shifts).reshape(T, W)

        x1, h2, idx_t, gate_t, rank_t, cnt = _out_route(
            x2, ym, p, pool_w[l].astype(bf16), pool_scale[l][None, :], w_out[l].astype(bf16),
            norm2_g[l][None, :], w_router[l].T.astype(bf16), b_router[l][:, None], S)

        counts = cnt[:, 0]
        padded = ((counts + TM_EXPERT - 1) // TM_EXPERT) * TM_EXPERT
        padded_end = jnp.cumsum(padded)
        padded_start = padded_end - padded
        expert_ids = jnp.arange(N_EXPERTS, dtype=jnp.int32)[:, None, None]
        start_of = jnp.sum(jnp.where(idx_t[None] == expert_ids, padded_start[:, None, None], 0), axis=0)
        dest = start_of + rank_t
        block_start = jnp.concatenate(
            [jnp.zeros((1,), jnp.int32), (padded_end // TM_EXPERT).astype(jnp.int32)])

        slot_buf = _plan(dest.reshape(-1) + TM_EXPERT, fill)
        y_tok = _experts(block_start, slot_buf, h2, w_gate[l], b_gate[l][:, None, :],
                         w_up[l], b_up[l][:, None, :], w_down[l], b_down[l][:, None, :], T)
        last = l + 1 == depth
        x2 = _combine(x1, y_tok, gate_t, normf_g[None, :], last)
    return x2.reshape(B, S, D)
```

```python
import functools

import jax
import jax.numpy as jnp
from jax import lax
from jax.experimental import pallas as pl
from jax.experimental.pallas import tpu as pltpu
from jax.experimental.pallas import tpu_sc as plsc

D_MODEL = 1024
MLSTM_WIDTH = 512
MLSTM_HEADS = 4
HEAD_DIM = 128
CONV_WIDTH = 4
POOL_WIDTH = 512
POOL_WINDOWS = (2, 4, 8, 16)
POOL_GROUP_DIM = 128
N_EXPERTS = 32
TOP_K = 4
D_FF = 1024
SWIGLU_LIMIT = 7.0
SWIGLU_ALPHA = 1.702
EPS = 1e-5

N_MAIN = 4 * MLSTM_WIDTH + POOL_WIDTH
N_GATES = 2 * MLSTM_HEADS

LANES = 128
BF16_SUBLANES = 16
VMEM_LIMIT = 56 * 1024 * 1024

TM_PROJ = 512
ROUTE_SUB = 2
CHUNK = 256
MLSTM_BATCH = 2
HALO = 16
TM_EXPERT = 256
NBUF = 6
ROW_DMA_PRIORITY = 1
SLAB = D_MODEL // LANES
YSLAB = SLAB // 2
PLAN_CHUNK = 8192
SC_LANES = 16
PLAN_UNROLL = 8

NT_DIMS = (((1,), (1,)), ((), ()))


def _sigmoid(x):
    return 1.0 / (1.0 + jnp.exp(-x))


def _in_proj_kernel(x_ref, g_ref, w_ref, wgt_ref, p_ref, gt_ref):
    x = x_ref[...]
    h = x * lax.rsqrt(jnp.mean(x * x, axis=-1, keepdims=True) + EPS) * g_ref[...]
    hb = h.astype(jnp.bfloat16)
    p_ref[...] = jnp.dot(hb, w_ref[...], preferred_element_type=jnp.float32).astype(p_ref.dtype)
    gt = lax.dot_general(wgt_ref[...], hb, NT_DIMS, preferred_element_type=jnp.float32)
    gt_ref[...] = gt[:N_GATES]


def _in_proj(x2, g1, w_main, wg_t):
    T = x2.shape[0]
    return pl.pallas_call(
        _in_proj_kernel,
        grid=(T // TM_PROJ,),
        in_specs=[
            pl.BlockSpec((TM_PROJ, D_MODEL), lambda i: (i, 0)),
            pl.BlockSpec((1, D_MODEL), lambda i: (0, 0)),
            pl.BlockSpec((D_MODEL, N_MAIN), lambda i: (0, 0)),
            pl.BlockSpec((BF16_SUBLANES, D_MODEL), lambda i: (0, 0)),
        ],
        out_specs=[
            pl.BlockSpec((TM_PROJ, N_MAIN), lambda i: (i, 0)),
            pl.BlockSpec((N_GATES, TM_PROJ), lambda i: (0, i)),
        ],
        out_shape=[
            jax.ShapeDtypeStruct((T, N_MAIN), jnp.bfloat16),
            jax.ShapeDtypeStruct((N_GATES, T), jnp.float32),
        ],
        compiler_params=pltpu.CompilerParams(
            dimension_semantics=("parallel",), vmem_limit_bytes=VMEM_LIMIT),
        name="in_proj",
    )(x2, g1, w_main, wg_t)


def _mlstm_kernel(qk_ref, qkp_ref, v_ref, o_ref, gt_ref, convw_ref, gb_ref, hng_ref,
                  tri_ref, shift_ref, hshift_ref, y_ref, cn_ref, m_ref):
    L = CHUNK
    c = pl.program_id(1)

    @pl.when(c == 0)
    def _():
        cn_ref[...] = jnp.zeros_like(cn_ref)
        m_ref[...] = jnp.zeros_like(m_ref)

    row_id = lax.broadcasted_iota(jnp.int32, (L, L), 0)
    col_id = lax.broadcasted_iota(jnp.int32, (L, L), 1)
    causal = col_id <= row_id
    ones_blk = jnp.ones((L, HEAD_DIM), jnp.bfloat16)
    lane = lax.broadcasted_iota(jnp.int32, (MLSTM_HEADS, L), 1)

    gate_terms = []
    for bb in range(MLSTM_BATCH):
        gt = gt_ref[bb] + gb_ref[...]
        f = gt[MLSTM_HEADS:]
        lf = jnp.minimum(f, 0.0) - jnp.log(1.0 + jnp.exp(-jnp.abs(f)))
        ig = gt[:MLSTM_HEADS]
        b_rows = lax.dot_general(lf, tri_ref[...], NT_DIMS, precision=lax.Precision.HIGHEST,
                                 preferred_element_type=jnp.float32)
        c_rows = ig - b_rows
        cm_rows = c_rows
        d = 1
        while d < L:
            cm_rows = jnp.maximum(
                cm_rows, jnp.where(lane >= d, pltpu.roll(cm_rows, d, axis=1), -jnp.inf))
            d *= 2
        gate_terms.append((b_rows, c_rows, cm_rows))

    conv_terms = []
    for bb in range(MLSTM_BATCH):
        x_cur = qk_ref[bb]
        x_prev = jnp.where(c > 0, qkp_ref[bb], jnp.zeros((HALO, 2 * MLSTM_WIDTH), jnp.bfloat16))
        acc = convw_ref[CONV_WIDTH - 1:CONV_WIDTH, :] * x_cur.astype(jnp.float32)
        for j in range(CONV_WIDTH - 1):
            sh = jnp.dot(shift_ref[j], x_cur, preferred_element_type=jnp.float32)
            top = sh[:8] + jnp.dot(hshift_ref[j], x_prev, preferred_element_type=jnp.float32)
            sh = jnp.concatenate([top, sh[8:]], axis=0)
            acc = acc + convw_ref[j:j + 1, :] * sh
        qk = acc * _sigmoid(acc)
        q_all = qk[:, :MLSTM_WIDTH].astype(jnp.bfloat16)
        k_t = jnp.transpose(qk[:, MLSTM_WIDTH:] * (HEAD_DIM ** -0.5))
        conv_terms.append((q_all, k_t))

    for bb in range(MLSTM_BATCH):
        b_rows, c_rows, cm_rows = gate_terms[bb]
        q_all, k_t = conv_terms[bb]
        m_in4 = jnp.concatenate(
            [m_ref[bb * MLSTM_HEADS + h][0:1, 0:1] for h in range(MLSTM_HEADS)], axis=0)
        mx_rows = jnp.maximum(cm_rows, m_in4)
        inter_rows = jnp.exp(m_in4 - mx_rows)
        einv_rows = jnp.exp(-(b_rows + mx_rows))
        fac_t = jnp.transpose(jnp.concatenate(
            [mx_rows, inter_rows, einv_rows, jnp.zeros_like(mx_rows)], axis=0))

        for h in range(MLSTM_HEADS):
            lo = h * HEAD_DIM
            st = bb * MLSTM_HEADS + h
            q = q_all[:, lo:lo + HEAD_DIM]
            kt = k_t[lo:lo + HEAD_DIM, :]
            v_ext = jnp.concatenate([v_ref[bb, :, lo:lo + HEAD_DIM], ones_blk], axis=1)
            mx_col = fac_t[:, h:h + 1]
            inter_col = fac_t[:, MLSTM_HEADS + h:MLSTM_HEADS + h + 1]
            einv_col = fac_t[:, 2 * MLSTM_HEADS + h:2 * MLSTM_HEADS + h + 1]
            c_row = c_rows[h:h + 1, :]
            b_tot = b_rows[h:h + 1, L - 1:L]
            cm_tot = cm_rows[h:h + 1, L - 1:L]
            m_in = m_ref[st][0:1, 0:1]
            cn = cn_ref[st]

            s_qk = jnp.dot(q, kt.astype(jnp.bfloat16), preferred_element_type=jnp.float32)
            s = (s_qk * jnp.exp(jnp.where(causal, c_row - mx_col, -jnp.inf))).astype(jnp.bfloat16)
            num = (jnp.dot(s, v_ext, preferred_element_type=jnp.float32)
                   + inter_col * jnp.dot(q, cn.astype(jnp.bfloat16),
                                         preferred_element_type=jnp.float32))
            den = num[:, HEAD_DIM:]
            hh = num[:, :HEAD_DIM] / jnp.maximum(jnp.abs(den), einv_col)

            mu = jnp.mean(hh, axis=-1, keepdims=True)
            dv = hh - mu
            var = jnp.mean(dv * dv, axis=-1, keepdims=True)
            hn = dv * lax.rsqrt(var + EPS) * hng_ref[:, lo:lo + HEAD_DIM]
            og = _sigmoid(o_ref[bb, :, lo:lo + HEAD_DIM].astype(jnp.float32))
            y_ref[bb, :, lo:lo + HEAD_DIM] = (og * hn).astype(y_ref.dtype)

            m_loc = b_tot + cm_tot
            kw_t = (kt * jnp.exp(c_row - cm_tot)).astype(jnp.bfloat16)
            c_loc = jnp.dot(kw_t, v_ext, preferred_element_type=jnp.float32)
            m_new = jnp.maximum(b_tot + m_in, m_loc)
            s_old = jnp.exp(b_tot + m_in - m_new)
            s_loc = jnp.exp(m_loc - m_new)
            cn_ref[st] = s_old * cn + s_loc * c_loc
            m_ref[st] = jnp.broadcast_to(m_new, m_ref.shape[1:])


def _mlstm(p3, gates_b, conv_w, gate_b, hn_g, tri, shifts, halo_shifts):
    batch, seq, _ = p3.shape
    L = CHUNK
    BB = MLSTM_BATCH
    halo_per_chunk = L // HALO
    return pl.pallas_call(
        _mlstm_kernel,
        grid=(batch // BB, seq // L),
        in_specs=[
            pl.BlockSpec((BB, L, 2 * MLSTM_WIDTH), lambda bi, ci: (bi, ci, 0)),
            pl.BlockSpec((BB, HALO, 2 * MLSTM_WIDTH),
                         lambda bi, ci: (bi, jnp.maximum(ci * halo_per_chunk - 1, 0), 0)),
            pl.BlockSpec((BB, L, MLSTM_WIDTH), lambda bi, ci: (bi, ci, 2)),
            pl.BlockSpec((BB, L, MLSTM_WIDTH), lambda bi, ci: (bi, ci, 3)),
            pl.BlockSpec((BB, N_GATES, L), lambda bi, ci: (bi, 0, ci)),
            pl.BlockSpec((CONV_WIDTH, 2 * MLSTM_WIDTH), lambda bi, ci: (0, 0)),
            pl.BlockSpec((N_GATES, 1), lambda bi, ci: (0, 0)),
            pl.BlockSpec((1, MLSTM_WIDTH), lambda bi, ci: (0, 0)),
            pl.BlockSpec((L, L), lambda bi, ci: (0, 0)),
            pl.BlockSpec((CONV_WIDTH - 1, L, L), lambda bi, ci: (0, 0, 0)),
            pl.BlockSpec((CONV_WIDTH - 1, 8, HALO), lambda bi, ci: (0, 0, 0)),
        ],
        out_specs=pl.BlockSpec((BB, L, MLSTM_WIDTH), lambda bi, ci: (bi, ci, 0)),
        out_shape=jax.ShapeDtypeStruct((batch, seq, MLSTM_WIDTH), jnp.bfloat16),
        scratch_shapes=[
            pltpu.VMEM((BB * MLSTM_HEADS, HEAD_DIM, 2 * HEAD_DIM), jnp.float32),
            pltpu.VMEM((BB * MLSTM_HEADS, 8, LANES), jnp.float32),
        ],
        compiler_params=pltpu.CompilerParams(
            dimension_semantics=("parallel", "arbitrary"), vmem_limit_bytes=VMEM_LIMIT),
        name="mlstm",
    )(p3, p3, p3, p3, gates_b, conv_w, gate_b, hn_g, tri, shifts, halo_shifts)


def _out_route_kernel(seq, x_ref, ym_ref, u_ref, up_ref, pw_ref, ps_ref, wo_ref, g2_ref,
                      wrt_ref, br_ref, x1_ref, h2_ref, idx_ref, gate_ref, rank_ref, cnt_ref,
                      carry_ref):
    TM = TM_PROJ
    R = ROUTE_SUB * TM
    i = pl.program_id(0)

    @pl.when(i == 0)
    def _():
        carry_ref[...] = jnp.zeros_like(carry_ref)

    pos0 = (i * R) % seq
    e_id = lax.broadcasted_iota(jnp.int32, (N_EXPERTS, TM), 0).astype(jnp.float32)
    t_row = lax.broadcasted_iota(jnp.int32, (TM, TM), 0)
    t_col = lax.broadcasted_iota(jnp.int32, (TM, TM), 1)
    before = jnp.where(t_row < t_col, 1.0, 0.0).astype(jnp.bfloat16)
    carry = carry_ref[...]
    subs = [slice(sub * TM, (sub + 1) * TM) for sub in range(ROUTE_SUB)]

    halo = jnp.where(pos0 > 0, up_ref[...].astype(jnp.float32), 0.0)
    u_ext = jnp.concatenate([halo, u_ref[...].astype(jnp.float32)], axis=0)
    win_sums = []
    for gi, w in enumerate(POOL_WINDOWS):
        sw = u_ext[:, gi * POOL_GROUP_DIM:(gi + 1) * POOL_GROUP_DIM]
        span = 1
        while span < w:
            sw = sw + pltpu.roll(sw, span, axis=0)
            span *= 2
        win_sums.append(sw)
    y_cats = []
    for sub, rows in enumerate(subs):
        r0 = sub * TM
        pos = (pos0 + r0 + lax.broadcasted_iota(jnp.int32, (TM, 1), 0) + 1).astype(jnp.float32)
        mixed = []
        for gi, w in enumerate(POOL_WINDOWS):
            lo = gi * POOL_GROUP_DIM
            tok = u_ext[HALO + r0:HALO + r0 + TM, lo:lo + POOL_GROUP_DIM]
            pooled = win_sums[gi][HALO + r0:HALO + r0 + TM] / jnp.minimum(pos, float(w)) - tok
            mg = jnp.dot(pooled.astype(jnp.bfloat16), pw_ref[gi],
                         preferred_element_type=jnp.float32)
            mixed.append((mg * ps_ref[:, lo:lo + POOL_GROUP_DIM]).astype(jnp.bfloat16))
        y_cats.append(jnp.concatenate([ym_ref[rows, :]] + mixed, axis=1))

    all_logits = []
    for sub, rows in enumerate(subs):
        r0 = sub * TM
        x1 = x_ref[rows, :] + jnp.dot(y_cats[sub], wo_ref[...], preferred_element_type=jnp.float32)
        x1_ref[rows, :] = x1
        h2 = x1 * lax.rsqrt(jnp.mean(x1 * x1, axis=-1, keepdims=True) + EPS) * g2_ref[...]
        h2b = h2.astype(jnp.bfloat16)
        for s in range(SLAB):
            h2_ref[pl.ds(r0 * SLAB + s, TM, stride=SLAB), :] = h2[:, s * LANES:(s + 1) * LANES]
        all_logits.append(lax.dot_general(wrt_ref[...], h2b, NT_DIMS,
                                          preferred_element_type=jnp.float32) + br_ref[...])

    for sub, rows in enumerate(subs):
        work = all_logits[sub]
        vals, ids, hots = [], [], []
        for _ in range(TOP_K):
            mk = jnp.max(work, axis=0, keepdims=True)
            ik = jnp.min(jnp.where(work == mk, e_id, float(N_EXPERTS)), axis=0, keepdims=True)
            hot = e_id == ik
            work = jnp.where(hot, -jnp.inf, work)
            vals.append(mk)
            ids.append(ik)
            hots.append(hot)
        ex = [jnp.exp(vk - vals[0]) for vk in vals]
        denom = ex[0] + ex[1] + ex[2] + ex[3]
        gate_ref[:, rows] = jnp.concatenate([e / denom for e in ex], axis=0)
        idx_ref[:, rows] = jnp.concatenate(ids, axis=0).astype(jnp.int32)

        sel_f = sum(jnp.where(hot, 1.0, 0.0) for hot in hots)
        prefix = jnp.dot(sel_f.astype(jnp.bfloat16), before, preferred_element_type=jnp.float32)
        rank_e = carry[:, 0:1] + prefix
        ranks = [jnp.sum(jnp.where(hot, rank_e, 0.0), axis=0, keepdims=True) for hot in hots]
        rank_ref[:, rows] = jnp.concatenate(ranks, axis=0).astype(jnp.int32)
        carry = carry + jnp.sum(sel_f, axis=1, keepdims=True)
    carry_ref[...] = carry
    cnt_ref[...] = carry.astype(jnp.int32)


def _out_route(x2, ym, p, pool_w, pool_s, w_out, g2, wr_t, br, seq):
    T = x2.shape[0]
    TM = ROUTE_SUB * TM_PROJ
    nt = T // TM
    u_blk = N_MAIN // POOL_WIDTH - 1
    halo_per_tile = TM // HALO
    tok_spec = pl.BlockSpec((TOP_K, TM), lambda i: (0, i))
    return pl.pallas_call(
        functools.partial(_out_route_kernel, seq),
        grid=(nt,),
        in_specs=[
            pl.BlockSpec((TM, D_MODEL), lambda i: (i, 0)),
            pl.BlockSpec((TM, MLSTM_WIDTH), lambda i: (i, 0)),
            pl.BlockSpec((TM, POOL_WIDTH), lambda i: (i, u_blk)),
            pl.BlockSpec((HALO, POOL_WIDTH),
                         lambda i: (jnp.maximum(i * halo_per_tile - 1, 0), u_blk)),
            pl.BlockSpec((len(POOL_WINDOWS), POOL_GROUP_DIM, POOL_GROUP_DIM), lambda i: (0, 0, 0)),
            pl.BlockSpec((1, POOL_WIDTH), lambda i: (0, 0)),
            pl.BlockSpec((D_MODEL, D_MODEL), lambda i: (0, 0)),
            pl.BlockSpec((1, D_MODEL), lambda i: (0, 0)),
            pl.BlockSpec((N_EXPERTS, D_MODEL), lambda i: (0, 0)),
            pl.BlockSpec((N_EXPERTS, 1), lambda i: (0, 0)),
        ],
        out_specs=[
            pl.BlockSpec((TM, D_MODEL), lambda i: (i, 0)),
            pl.BlockSpec((TM * SLAB, LANES), lambda i: (i, 0)),
            tok_spec, tok_spec, tok_spec,
            pl.BlockSpec((N_EXPERTS, LANES), lambda i: (0, 0)),
        ],
        out_shape=[
            jax.ShapeDtypeStruct((T, D_MODEL), jnp.float32),
            jax.ShapeDtypeStruct((T * SLAB, LANES), jnp.float32),
            jax.ShapeDtypeStruct((TOP_K, T), jnp.int32),
            jax.ShapeDtypeStruct((TOP_K, T), jnp.float32),
            jax.ShapeDtypeStruct((TOP_K, T), jnp.int32),
            jax.ShapeDtypeStruct((N_EXPERTS, LANES), jnp.int32),
        ],
        scratch_shapes=[
            pltpu.VMEM((N_EXPERTS, LANES), jnp.float32),
        ],
        compiler_params=pltpu.CompilerParams(
            dimension_semantics=("arbitrary",), vmem_limit_bytes=VMEM_LIMIT),
        name="out_route",
    )(x2, ym, p, p, pool_w, pool_s, w_out, g2, wr_t, br)


def _plan(dest_flat, fill):
    n_assign = dest_flat.shape[0]
    n_table = fill.shape[0]
    mesh = plsc.VectorSubcoreMesh(core_axis_name="c", subcore_axis_name="s")

    @pl.kernel(out_type=jax.ShapeDtypeStruct((n_table,), jnp.int32), mesh=mesh,
               scratch_types=[pltpu.VMEM((n_table,), jnp.int32),
                              pltpu.VMEM((PLAN_CHUNK,), jnp.int32)],
               compiler_params=pltpu.CompilerParams(needs_layout_passes=False))
    def plan_kernel(dest_hbm, fill_hbm, out_hbm, table, chunk):
        first = jnp.logical_and(lax.axis_index("c") == 0, lax.axis_index("s") == 0)

        @pl.when(first)
        def _():
            pltpu.sync_copy(fill_hbm, table)

            @pl.loop(0, n_assign // PLAN_CHUNK)
            def _(ci):
                pltpu.sync_copy(dest_hbm.at[pl.ds(ci * PLAN_CHUNK, PLAN_CHUNK)], chunk)

                @pl.loop(0, PLAN_CHUNK // (SC_LANES * PLAN_UNROLL))
                def _(i):
                    for j in range(PLAN_UNROLL):
                        off = (i * PLAN_UNROLL + j) * SC_LANES
                        idx = chunk[pl.ds(off, SC_LANES)]
                        vals = (ci * PLAN_CHUNK + off
                                + lax.broadcasted_iota(jnp.int32, (SC_LANES,), 0))
                        plsc.store_scatter(table, [idx], vals)

            pltpu.sync_copy(table, out_hbm)

    return plan_kernel(dest_flat, fill)


def _expert_kernel(n_tok, bs_ref, slot_ref, h2_ref, wg_ref, bg_ref, wu_ref, bu_ref, wd_ref, bd_ref,
                   yt_ref, *scratch):
    TM = TM_EXPERT
    ROWS = TM * SLAB
    YROWS = TM * YSLAB
    e = pl.program_id(0)
    n_total = bs_ref[N_EXPERTS]
    xg = scratch[:NBUF]
    ys = scratch[NBUF:2 * NBUF]
    wgb_ref, wub_ref, wdb_ref, gsem, ssem = scratch[2 * NBUF:]

    def token_of(a):
        return a & (n_tok - 1) if n_tok & (n_tok - 1) == 0 else lax.rem(a, n_tok)

    def start_gather(blk, par):
        base = (blk + 1) * TM
        for r in range(TM):
            t = token_of(slot_ref[base + r])
            pltpu.make_async_copy(h2_ref.at[pl.ds(pl.multiple_of(t * SLAB, SLAB), SLAB), :],
                                  xg[par].at[pl.ds(r * SLAB, SLAB), :], gsem.at[par]
                                  ).start(priority=ROW_DMA_PRIORITY)

    def wait_gather(par):
        pltpu.make_async_copy(h2_ref.at[pl.ds(0, ROWS), :], xg[0], gsem.at[par]).wait()

    def start_scatter(blk, par):
        base = (blk + 1) * TM
        for r in range(TM):
            a = slot_ref[base + r]
            pltpu.make_async_copy(ys[par].at[pl.ds(r * YSLAB, YSLAB), :],
                                  yt_ref.at[pl.ds(pl.multiple_of(a * YSLAB, YSLAB), YSLAB), :],
                                  ssem.at[par]).start(priority=ROW_DMA_PRIORITY)

    def wait_scatter(par):
        pltpu.make_async_copy(ys[0], yt_ref.at[pl.ds(0, YROWS), :], ssem.at[par]).wait()

    @pl.when(e == 0)
    def _():
        for blk in range(NBUF - 1):
            start_gather(blk, blk)
        for par in range(NBUF):
            ys[par][...] = jnp.zeros_like(ys[par])
            dump = yt_ref.at[pl.ds((n_tok * TOP_K + par * TM) * YSLAB, YROWS), :]
            cp = pltpu.make_async_copy(ys[par], dump, ssem.at[par])
            cp.start()
            cp.wait()

    wgb_ref[...] = wg_ref[0].astype(jnp.bfloat16)
    wub_ref[...] = wu_ref[0].astype(jnp.bfloat16)
    wdb_ref[...] = wd_ref[0].astype(jnp.bfloat16)

    def block_step(g, par):
        prv = (par + NBUF - 1) % NBUF
        wait_gather(par)

        @pl.when(g >= NBUF - 1)
        def _():
            wait_scatter(par)

        start_gather(g + NBUF - 1, prv)
        start_scatter(g - 1, prv)
        x = jnp.concatenate(
            [xg[par][pl.ds(s, TM, stride=SLAB), :].astype(jnp.bfloat16) for s in range(SLAB)],
            axis=1)
        gate = jnp.dot(x, wgb_ref[...], preferred_element_type=jnp.float32) + bg_ref[0]
        up = jnp.dot(x, wub_ref[...], preferred_element_type=jnp.float32) + bu_ref[0]
        gate = jnp.minimum(gate, SWIGLU_LIMIT)
        up = jnp.clip(up, -SWIGLU_LIMIT, SWIGLU_LIMIT)
        glu = gate * _sigmoid(SWIGLU_ALPHA * gate)
        act = (glu * (up + 1.0)).astype(jnp.bfloat16)
        y = jnp.dot(act, wdb_ref[...], preferred_element_type=jnp.float32) + bd_ref[0]
        lo = pltpu.bitcast(y[:, :D_MODEL // 2].astype(jnp.bfloat16).astype(jnp.float32), jnp.uint32)
        hi = pltpu.bitcast(y[:, D_MODEL // 2:].astype(jnp.bfloat16).astype(jnp.float32), jnp.uint32)
        packed = (lo >> 16) | (hi & jnp.uint32(0xFFFF0000))
        for s in range(YSLAB):
            ys[par][pl.ds(s, TM, stride=YSLAB), :] = packed[:, s * LANES:(s + 1) * LANES]

    def body(g, carry):
        for par in range(NBUF):
            pl.when(g % NBUF == par)(functools.partial(block_step, g, par))
        return carry

    lax.fori_loop(bs_ref[e], bs_ref[e + 1], body, 0)

    @pl.when(e == N_EXPERTS - 1)
    def _():
        g = n_total
        for par in range(NBUF):
            @pl.when((g - 1) % NBUF == par)
            def _():
                start_scatter(g - 1, par)
        for j in range(NBUF - 1):
            wait_gather((g + j) % NBUF)
        wait_scatter((g - 1) % NBUF)
        for j in range(2, NBUF + 1):
            @pl.when(g >= j - 1)
            def _():
                wait_scatter((g + NBUF - j) % NBUF)


def _experts(block_start, slot_buf, h2_slab, w_gate, b_gate, w_up, b_up, w_down, b_down, n_tok):
    TM = TM_EXPERT
    n_assign = n_tok * TOP_K
    w_spec = pl.BlockSpec((1, D_MODEL, D_FF), lambda e, bs, sl: (e, 0, 0))
    bias_spec = pl.BlockSpec((1, 1, D_FF), lambda e, bs, sl: (e, 0, 0))
    buf = pltpu.VMEM((TM * SLAB, LANES), jnp.float32)
    ybuf = pltpu.VMEM((TM * YSLAB, LANES), jnp.uint32)
    grid_spec = pltpu.PrefetchScalarGridSpec(
        num_scalar_prefetch=2,
        grid=(N_EXPERTS,),
        in_specs=[
            pl.BlockSpec(memory_space=pl.ANY),
            w_spec, bias_spec, w_spec, bias_spec, w_spec, bias_spec,
        ],
        out_specs=pl.BlockSpec(memory_space=pl.ANY),
        scratch_shapes=[
            *([buf] * NBUF), *([ybuf] * NBUF),
            pltpu.VMEM((D_MODEL, D_FF), jnp.bfloat16),
            pltpu.VMEM((D_MODEL, D_FF), jnp.bfloat16),
            pltpu.VMEM((D_FF, D_MODEL), jnp.bfloat16),
            pltpu.SemaphoreType.DMA((NBUF,)),
            pltpu.SemaphoreType.DMA((NBUF,)),
        ],
    )
    return pl.pallas_call(
        functools.partial(_expert_kernel, n_tok),
        grid_spec=grid_spec,
        out_shape=jax.ShapeDtypeStruct(((n_assign + NBUF * TM) * YSLAB, LANES), jnp.uint32),
        compiler_params=pltpu.CompilerParams(
            dimension_semantics=("arbitrary",), vmem_limit_bytes=VMEM_LIMIT),
        name="experts",
    )(block_start, slot_buf, h2_slab, w_gate, b_gate, w_up, b_up, w_down, b_down)


def _combine_kernel(normalize, x1_ref, y0_ref, y1_ref, y2_ref, y3_ref, gate_ref, g_ref, o_ref):
    TM = TM_PROJ
    gates = jnp.concatenate([gate_ref[...], jnp.zeros((8 - TOP_K, TM), jnp.float32)], axis=0)
    g_cols = jnp.transpose(gates)
    g_bc = [jnp.broadcast_to(g_cols[:, k:k + 1], (TM, LANES)) for k in range(TOP_K)]
    ssq = jnp.zeros((TM, LANES), jnp.float32)
    parts = [x1_ref[:, s * LANES:(s + 1) * LANES] for s in range(SLAB)]
    for s in range(YSLAB):
        for k, y_ref in enumerate((y0_ref, y1_ref, y2_ref, y3_ref)):
            w = y_ref[pl.ds(s, TM, stride=YSLAB), :]
            lo = pltpu.bitcast(w << 16, jnp.float32)
            hi = pltpu.bitcast(w & jnp.uint32(0xFFFF0000), jnp.float32)
            parts[s] = parts[s] + g_bc[k] * lo
            parts[YSLAB + s] = parts[YSLAB + s] + g_bc[k] * hi
    for acc in parts:
        ssq = ssq + acc * acc
    if normalize:
        inv = lax.rsqrt(jnp.sum(ssq, axis=-1, keepdims=True) * (1.0 / D_MODEL) + EPS)
        for s in range(SLAB):
            o_ref[:, s * LANES:(s + 1) * LANES] = parts[s] * inv * g_ref[:, s * LANES:(s + 1) * LANES]
    else:
        for s in range(SLAB):
            o_ref[:, s * LANES:(s + 1) * LANES] = parts[s]


def _combine(x1, y_tok, gate_t, gf, normalize):
    T = x1.shape[0]
    TM = TM_PROJ
    nt = T // TM

    def y_spec(k):
        return pl.BlockSpec((TM * YSLAB, LANES), lambda i: (k * nt + i, 0))

    return pl.pallas_call(
        functools.partial(_combine_kernel, normalize),
        grid=(nt,),
        in_specs=[
            pl.BlockSpec((TM, D_MODEL), lambda i: (i, 0)),
            y_spec(0), y_spec(1), y_spec(2), y_spec(3),
            pl.BlockSpec((TOP_K, TM), lambda i: (0, i)),
            pl.BlockSpec((1, D_MODEL), lambda i: (0, 0)),
        ],
        out_specs=pl.BlockSpec((TM, D_MODEL), lambda i: (i, 0)),
        out_shape=jax.ShapeDtypeStruct((T, D_MODEL), jnp.float32),
        compiler_params=pltpu.CompilerParams(
            dimension_semantics=("parallel",), vmem_limit_bytes=VMEM_LIMIT),
        name="combine",
    )(x1, y_tok, y_tok, y_tok, y_tok, gate_t, gf)


def kernel(x, norm1_g, w_in, ig_b, fg_b, conv_w, head_norm_g, pool_w, pool_scale, w_out, norm2_g,
           w_router, b_router, w_gate, b_gate, w_up, b_up, w_down, b_down, normf_g):
    B, S, D = x.shape
    T = B * S
    depth = norm1_g.shape[0]
    W = MLSTM_WIDTH
    f32, bf16 = jnp.float32, jnp.bfloat16

    L = CHUNK
    t_l = lax.broadcasted_iota(jnp.int32, (L, L), 0)
    t_r = lax.broadcasted_iota(jnp.int32, (L, L), 1)
    tri = (t_r <= t_l).astype(f32)
    shifts = jnp.stack([(t_l - t_r == CONV_WIDTH - 1 - j).astype(bf16)
                        for j in range(CONV_WIDTH - 1)])
    h_t = lax.broadcasted_iota(jnp.int32, (8, HALO), 0)
    h_r = lax.broadcasted_iota(jnp.int32, (8, HALO), 1)
    halo_shifts = jnp.stack([(h_r - HALO - h_t == -(CONV_WIDTH - 1 - j)).astype(bf16)
                             for j in range(CONV_WIDTH - 1)])

    n_assign = T * TOP_K
    n_blocks = -(-n_assign // TM_EXPERT) + N_EXPERTS
    n_rows = n_blocks * TM_EXPERT
    n_table = n_rows + NBUF * TM_EXPERT
    fill = n_assign + ((jnp.arange(n_table, dtype=jnp.int32) + (NBUF - 1) * TM_EXPERT)
                       % (NBUF * TM_EXPERT))
    x2 = x.reshape(T, D)
    for l in range(depth):
        w = w_in[l]
        w_main = jnp.concatenate([w[:, :4 * W], w[:, 4 * W + N_GATES:]], axis=1).astype(bf16)
        wg_t = jnp.zeros((BF16_SUBLANES, D), bf16).at[:N_GATES].set(
            w[:, 4 * W:4 * W + N_GATES].T.astype(bf16))
        p, gates_t = _in_proj(x2, norm1_g[l][None, :], w_main, wg_t)

        gate_b = jnp.concatenate([ig_b[l], fg_b[l]])[:, None].astype(f32)
        gates_b = gates_t.reshape(N_GATES, B, S).transpose(1, 0, 2)
        ym = _mlstm(p.reshape(B, S, N_MAIN), gates_b, conv_w[l].astype(f32), gate_b,
                    head_norm_g[l][None, :], tri, shifts, halo_shifts).reshape(T, W)

        x1, h2, idx_t, gate_t, rank_t, cnt = _out_route(
            x2, ym, p, pool_w[l].astype(bf16), pool_scale[l][None, :], w_out[l].astype(bf16),
            norm2_g[l][None, :], w_router[l].T.astype(bf16), b_router[l][:, None], S)

        counts = cnt[:, 0]
        padded = ((counts + TM_EXPERT - 1) // TM_EXPERT) * TM_EXPERT
        padded_end = jnp.cumsum(padded)
        padded_start = padded_end - padded
        expert_ids = jnp.arange(N_EXPERTS, dtype=jnp.int32)[:, None, None]
        start_of = jnp.sum(jnp.where(idx_t[None] == expert_ids, padded_start[:, None, None], 0), axis=0)
        dest = start_of + rank_t
        block_start = jnp.concatenate(
            [jnp.zeros((1,), jnp.int32), (padded_end // TM_EXPERT).astype(jnp.int32)])

        slot_buf = _plan(dest.reshape(-1) + TM_EXPERT, fill)
        y_tok = _experts(block_start, slot_buf, h2, w_gate[l], b_gate[l][:, None, :],
                         w_up[l], b_up[l][:, None, :], w_down[l], b_down[l][:, None, :], T)
        last = l + 1 == depth
        x2 = _combine(x1, y_tok, gate_t, normf_g[None, :], last)
    return x2.reshape(B, S, D)
```

```python
import functools

import jax
import jax.numpy as jnp
from jax import lax
from jax.experimental import pallas as pl
from jax.experimental.pallas import tpu as pltpu
from jax.experimental.pallas import tpu_sc as plsc

D_MODEL = 1024
MLSTM_WIDTH = 512
MLSTM_HEADS = 4
HEAD_DIM = 128
CONV_WIDTH = 4
POOL_WIDTH = 512
POOL_WINDOWS = (2, 4, 8, 16)
POOL_GROUP_DIM = 128
N_EXPERTS = 32
TOP_K = 4
D_FF = 1024
SWIGLU_LIMIT = 7.0
SWIGLU_ALPHA = 1.702
EPS = 1e-5

N_MAIN = 4 * MLSTM_WIDTH + POOL_WIDTH
N_GATES = 2 * MLSTM_HEADS

LANES = 128
BF16_SUBLANES = 16
VMEM_LIMIT = 56 * 1024 * 1024

TM_PROJ = 512
ROUTE_SUB = 2
CHUNK = 256
MLSTM_BATCH = 2
HALO = 16
TM_EXPERT = 256
NBUF = 4
ROW_DMA_PRIORITY = 1
SLAB = D_MODEL // LANES
PSLAB = SLAB // 2
PLAN_CHUNK = 8192
SC_LANES = 16
PLAN_UNROLL = 8

NT_DIMS = (((1,), (1,)), ((), ()))


def _sigmoid(x):
    return 1.0 / (1.0 + jnp.exp(-x))


def _pack_bf16_pairs(v):
    half = v.shape[1] // 2
    lo = pltpu.bitcast(v[:, :half].astype(jnp.bfloat16).astype(jnp.float32), jnp.uint32)
    hi = pltpu.bitcast(v[:, half:].astype(jnp.bfloat16).astype(jnp.float32), jnp.uint32)
    return (lo >> 16) | (hi & jnp.uint32(0xFFFF0000))


def _unpack_lo(w):
    return pltpu.bitcast(w << 16, jnp.float32)


def _unpack_hi(w):
    return pltpu.bitcast(w & jnp.uint32(0xFFFF0000), jnp.float32)


def _in_proj_kernel(x_ref, g_ref, w_ref, wgt_ref, p_ref, gt_ref):
    x = x_ref[...]
    h = x * lax.rsqrt(jnp.mean(x * x, axis=-1, keepdims=True) + EPS) * g_ref[...]
    hb = h.astype(jnp.bfloat16)
    p_ref[...] = jnp.dot(hb, w_ref[...], preferred_element_type=jnp.float32).astype(p_ref.dtype)
    gt = lax.dot_general(wgt_ref[...], hb, NT_DIMS, preferred_element_type=jnp.float32)
    gt_ref[...] = gt[:N_GATES]


def _in_proj(x2, g1, w_main, wg_t):
    T = x2.shape[0]
    return pl.pallas_call(
        _in_proj_kernel,
        grid=(T // TM_PROJ,),
        in_specs=[
            pl.BlockSpec((TM_PROJ, D_MODEL), lambda i: (i, 0)),
            pl.BlockSpec((1, D_MODEL), lambda i: (0, 0)),
            pl.BlockSpec((D_MODEL, N_MAIN), lambda i: (0, 0)),
            pl.BlockSpec((BF16_SUBLANES, D_MODEL), lambda i: (0, 0)),
        ],
        out_specs=[
            pl.BlockSpec((TM_PROJ, N_MAIN), lambda i: (i, 0)),
            pl.BlockSpec((N_GATES, TM_PROJ), lambda i: (0, i)),
        ],
        out_shape=[
            jax.ShapeDtypeStruct((T, N_MAIN), jnp.bfloat16),
            jax.ShapeDtypeStruct((N_GATES, T), jnp.float32),
        ],
        compiler_params=pltpu.CompilerParams(
            dimension_semantics=("parallel",), vmem_limit_bytes=VMEM_LIMIT),
        name="in_proj",
    )(x2, g1, w_main, wg_t)


def _mlstm_kernel(qk_ref, qkp_ref, v_ref, o_ref, gt_ref, convw_ref, gb_ref, hng_ref,
                  tri_ref, shift_ref, hshift_ref, y_ref, cn_ref, m_ref):
    L = CHUNK
    c = pl.program_id(1)

    @pl.when(c == 0)
    def _():
        cn_ref[...] = jnp.zeros_like(cn_ref)
        m_ref[...] = jnp.zeros_like(m_ref)

    row_id = lax.broadcasted_iota(jnp.int32, (L, L), 0)
    col_id = lax.broadcasted_iota(jnp.int32, (L, L), 1)
    causal = col_id <= row_id
    ones_blk = jnp.ones((L, HEAD_DIM), jnp.bfloat16)
    lane = lax.broadcasted_iota(jnp.int32, (MLSTM_HEADS, L), 1)

    gate_terms = []
    for bb in range(MLSTM_BATCH):
        gt = gt_ref[bb] + gb_ref[...]
        f = gt[MLSTM_HEADS:]
        lf = jnp.minimum(f, 0.0) - jnp.log(1.0 + jnp.exp(-jnp.abs(f)))
        ig = gt[:MLSTM_HEADS]
        b_rows = lax.dot_general(lf, tri_ref[...], NT_DIMS, precision=lax.Precision.HIGHEST,
                                 preferred_element_type=jnp.float32)
        c_rows = ig - b_rows
        cm_rows = c_rows
        d = 1
        while d < L:
            cm_rows = jnp.maximum(
                cm_rows, jnp.where(lane >= d, pltpu.roll(cm_rows, d, axis=1), -jnp.inf))
            d *= 2
        gate_terms.append((b_rows, c_rows, cm_rows))

    conv_terms = []
    for bb in range(MLSTM_BATCH):
        x_cur = qk_ref[bb]
        x_prev = jnp.where(c > 0, qkp_ref[bb], jnp.zeros((HALO, 2 * MLSTM_WIDTH), jnp.bfloat16))
        acc = convw_ref[CONV_WIDTH - 1:CONV_WIDTH, :] * x_cur.astype(jnp.float32)
        for j in range(CONV_WIDTH - 1):
            sh = jnp.dot(shift_ref[j], x_cur, preferred_element_type=jnp.float32)
            top = sh[:8] + jnp.dot(hshift_ref[j], x_prev, preferred_element_type=jnp.float32)
            sh = jnp.concatenate([top, sh[8:]], axis=0)
            acc = acc + convw_ref[j:j + 1, :] * sh
        qk = acc * _sigmoid(acc)
        q_all = qk[:, :MLSTM_WIDTH].astype(jnp.bfloat16)
        k_t = jnp.transpose(qk[:, MLSTM_WIDTH:] * (HEAD_DIM ** -0.5))
        conv_terms.append((q_all, k_t))

    for bb in range(MLSTM_BATCH):
        b_rows, c_rows, cm_rows = gate_terms[bb]
        q_all, k_t = conv_terms[bb]
        m_in4 = jnp.concatenate(
            [m_ref[bb * MLSTM_HEADS + h][0:1, 0:1] for h in range(MLSTM_HEADS)], axis=0)
        mx_rows = jnp.maximum(cm_rows, m_in4)
        inter_rows = jnp.exp(m_in4 - mx_rows)
        einv_rows = jnp.exp(-(b_rows + mx_rows))
        fac_t = jnp.transpose(jnp.concatenate(
            [mx_rows, inter_rows, einv_rows, jnp.zeros_like(mx_rows)], axis=0))

        for h in range(MLSTM_HEADS):
            lo = h * HEAD_DIM
            st = bb * MLSTM_HEADS + h
            q = q_all[:, lo:lo + HEAD_DIM]
            kt = k_t[lo:lo + HEAD_DIM, :]
            v_ext = jnp.concatenate([v_ref[bb, :, lo:lo + HEAD_DIM], ones_blk], axis=1)
            mx_col = fac_t[:, h:h + 1]
            inter_col = fac_t[:, MLSTM_HEADS + h:MLSTM_HEADS + h + 1]
            einv_col = fac_t[:, 2 * MLSTM_HEADS + h:2 * MLSTM_HEADS + h + 1]
            c_row = c_rows[h:h + 1, :]
            b_tot = b_rows[h:h + 1, L - 1:L]
            cm_tot = cm_rows[h:h + 1, L - 1:L]
            m_in = m_ref[st][0:1, 0:1]
            cn = cn_ref[st]

            s_qk = jnp.dot(q, kt.astype(jnp.bfloat16), preferred_element_type=jnp.float32)
            s = (s_qk * jnp.exp(jnp.where(causal, c_row - mx_col, -jnp.inf))).astype(jnp.bfloat16)
            num = (jnp.dot(s, v_ext, preferred_element_type=jnp.float32)
                   + inter_col * jnp.dot(q, cn.astype(jnp.bfloat16),
                                         preferred_element_type=jnp.float32))
            den = num[:, HEAD_DIM:]
            hh = num[:, :HEAD_DIM] / jnp.maximum(jnp.abs(den), einv_col)

            mu = jnp.mean(hh, axis=-1, keepdims=True)
            dv = hh - mu
            var = jnp.mean(dv * dv, axis=-1, keepdims=True)
            hn = dv * lax.rsqrt(var + EPS) * hng_ref[:, lo:lo + HEAD_DIM]
            og = _sigmoid(o_ref[bb, :, lo:lo + HEAD_DIM].astype(jnp.float32))
            y_ref[bb, :, lo:lo + HEAD_DIM] = (og * hn).astype(y_ref.dtype)

            m_loc = b_tot + cm_tot
            kw_t = (kt * jnp.exp(c_row - cm_tot)).astype(jnp.bfloat16)
            c_loc = jnp.dot(kw_t, v_ext, preferred_element_type=jnp.float32)
            m_new = jnp.maximum(b_tot + m_in, m_loc)
            s_old = jnp.exp(b_tot + m_in - m_new)
            s_loc = jnp.exp(m_loc - m_new)
            cn_ref[st] = s_old * cn + s_loc * c_loc
            m_ref[st] = jnp.broadcast_to(m_new, m_ref.shape[1:])


def _mlstm(p3, gates_b, conv_w, gate_b, hn_g, tri, shifts, halo_shifts):
    batch, seq, _ = p3.shape
    L = CHUNK
    BB = MLSTM_BATCH
    halo_per_chunk = L // HALO
    return pl.pallas_call(
        _mlstm_kernel,
        grid=(batch // BB, seq // L),
        in_specs=[
            pl.BlockSpec((BB, L, 2 * MLSTM_WIDTH), lambda bi, ci: (bi, ci, 0)),
            pl.BlockSpec((BB, HALO, 2 * MLSTM_WIDTH),
                         lambda bi, ci: (bi, jnp.maximum(ci * halo_per_chunk - 1, 0), 0)),
            pl.BlockSpec((BB, L, MLSTM_WIDTH), lambda bi, ci: (bi, ci, 2)),
            pl.BlockSpec((BB, L, MLSTM_WIDTH), lambda bi, ci: (bi, ci, 3)),
            pl.BlockSpec((BB, N_GATES, L), lambda bi, ci: (bi, 0, ci)),
            pl.BlockSpec((CONV_WIDTH, 2 * MLSTM_WIDTH), lambda bi, ci: (0, 0)),
            pl.BlockSpec((N_GATES, 1), lambda bi, ci: (0, 0)),
            pl.BlockSpec((1, MLSTM_WIDTH), lambda bi, ci: (0, 0)),
            pl.BlockSpec((L, L), lambda bi, ci: (0, 0)),
            pl.BlockSpec((CONV_WIDTH - 1, L, L), lambda bi, ci: (0, 0, 0)),
            pl.BlockSpec((CONV_WIDTH - 1, 8, HALO), lambda bi, ci: (0, 0, 0)),
        ],
        out_specs=pl.BlockSpec((BB, L, MLSTM_WIDTH), lambda bi, ci: (bi, ci, 0)),
        out_shape=jax.ShapeDtypeStruct((batch, seq, MLSTM_WIDTH), jnp.bfloat16),
        scratch_shapes=[
            pltpu.VMEM((BB * MLSTM_HEADS, HEAD_DIM, 2 * HEAD_DIM), jnp.float32),
            pltpu.VMEM((BB * MLSTM_HEADS, 8, LANES), jnp.float32),
        ],
        compiler_params=pltpu.CompilerParams(
            dimension_semantics=("parallel", "arbitrary"), vmem_limit_bytes=VMEM_LIMIT),
        name="mlstm",
    )(p3, p3, p3, p3, gates_b, conv_w, gate_b, hn_g, tri, shifts, halo_shifts)


def _out_route_kernel(seq, x_ref, ym_ref, u_ref, up_ref, pw_ref, ps_ref, wo_ref, g2_ref,
                      wrt_ref, br_ref, x1_ref, h2_ref, idx_ref, gate_ref, rank_ref, cnt_ref,
                      carry_ref):
    TM = TM_PROJ
    R = ROUTE_SUB * TM
    i = pl.program_id(0)

    @pl.when(i == 0)
    def _():
        carry_ref[...] = jnp.zeros_like(carry_ref)

    pos0 = (i * R) % seq
    e_id = lax.broadcasted_iota(jnp.int32, (N_EXPERTS, TM), 0).astype(jnp.float32)
    t_row = lax.broadcasted_iota(jnp.int32, (TM, TM), 0)
    t_col = lax.broadcasted_iota(jnp.int32, (TM, TM), 1)
    before = jnp.where(t_row < t_col, 1.0, 0.0).astype(jnp.bfloat16)
    carry = carry_ref[...]
    subs = [slice(sub * TM, (sub + 1) * TM) for sub in range(ROUTE_SUB)]

    halo = jnp.where(pos0 > 0, up_ref[...].astype(jnp.float32), 0.0)
    u_ext = jnp.concatenate([halo, u_ref[...].astype(jnp.float32)], axis=0)
    win_sums = []
    for gi, w in enumerate(POOL_WINDOWS):
        sw = u_ext[:, gi * POOL_GROUP_DIM:(gi + 1) * POOL_GROUP_DIM]
        span = 1
        while span < w:
            sw = sw + pltpu.roll(sw, span, axis=0)
            span *= 2
        win_sums.append(sw)
    y_cats = []
    for sub, rows in enumerate(subs):
        r0 = sub * TM
        pos = (pos0 + r0 + lax.broadcasted_iota(jnp.int32, (TM, 1), 0) + 1).astype(jnp.float32)
        mixed = []
        for gi, w in enumerate(POOL_WINDOWS):
            lo = gi * POOL_GROUP_DIM
            tok = u_ext[HALO + r0:HALO + r0 + TM, lo:lo + POOL_GROUP_DIM]
            pooled = win_sums[gi][HALO + r0:HALO + r0 + TM] / jnp.minimum(pos, float(w)) - tok
            mg = jnp.dot(pooled.astype(jnp.bfloat16), pw_ref[gi],
                         preferred_element_type=jnp.float32)
            mixed.append((mg * ps_ref[:, lo:lo + POOL_GROUP_DIM]).astype(jnp.bfloat16))
        y_cats.append(jnp.concatenate([ym_ref[rows, :]] + mixed, axis=1))

    all_logits = []
    for sub, rows in enumerate(subs):
        r0 = sub * TM
        x1 = x_ref[rows, :] + jnp.dot(y_cats[sub], wo_ref[...], preferred_element_type=jnp.float32)
        x1_ref[rows, :] = x1
        h2 = x1 * lax.rsqrt(jnp.mean(x1 * x1, axis=-1, keepdims=True) + EPS) * g2_ref[...]
        h2b = h2.astype(jnp.bfloat16)
        h2w = _pack_bf16_pairs(h2)
        for s in range(PSLAB):
            h2_ref[pl.ds(r0 * PSLAB + s, TM, stride=PSLAB), :] = h2w[:, s * LANES:(s + 1) * LANES]
        all_logits.append(lax.dot_general(wrt_ref[...], h2b, NT_DIMS,
                                          preferred_element_type=jnp.float32) + br_ref[...])

    for sub, rows in enumerate(subs):
        work = all_logits[sub]
        vals, ids, hots = [], [], []
        for _ in range(TOP_K):
            mk = jnp.max(work, axis=0, keepdims=True)
            ik = jnp.min(jnp.where(work == mk, e_id, float(N_EXPERTS)), axis=0, keepdims=True)
            hot = e_id == ik
            work = jnp.where(hot, -jnp.inf, work)
            vals.append(mk)
            ids.append(ik)
            hots.append(hot)
        ex = [jnp.exp(vk - vals[0]) for vk in vals]
        denom = ex[0] + ex[1] + ex[2] + ex[3]
        gate_ref[:, rows] = jnp.concatenate([e / denom for e in ex], axis=0)
        idx_ref[:, rows] = jnp.concatenate(ids, axis=0).astype(jnp.int32)

        sel_f = sum(jnp.where(hot, 1.0, 0.0) for hot in hots)
        prefix = jnp.dot(sel_f.astype(jnp.bfloat16), before, preferred_element_type=jnp.float32)
        rank_e = carry[:, 0:1] + prefix
        ranks = [jnp.sum(jnp.where(hot, rank_e, 0.0), axis=0, keepdims=True) for hot in hots]
        rank_ref[:, rows] = jnp.concatenate(ranks, axis=0).astype(jnp.int32)
        carry = carry + jnp.sum(sel_f, axis=1, keepdims=True)
    carry_ref[...] = carry
    cnt_ref[...] = carry.astype(jnp.int32)


def _out_route(x2, ym, p, pool_w, pool_s, w_out, g2, wr_t, br, seq):
    T = x2.shape[0]
    TM = ROUTE_SUB * TM_PROJ
    nt = T // TM
    u_blk = N_MAIN // POOL_WIDTH - 1
    halo_per_tile = TM // HALO
    tok_spec = pl.BlockSpec((TOP_K, TM), lambda i: (0, i))
    return pl.pallas_call(
        functools.partial(_out_route_kernel, seq),
        grid=(nt,),
        in_specs=[
            pl.BlockSpec((TM, D_MODEL), lambda i: (i, 0)),
            pl.BlockSpec((TM, MLSTM_WIDTH), lambda i: (i, 0)),
            pl.BlockSpec((TM, POOL_WIDTH), lambda i: (i, u_blk)),
            pl.BlockSpec((HALO, POOL_WIDTH),
                         lambda i: (jnp.maximum(i * halo_per_tile - 1, 0), u_blk)),
            pl.BlockSpec((len(POOL_WINDOWS), POOL_GROUP_DIM, POOL_GROUP_DIM), lambda i: (0, 0, 0)),
            pl.BlockSpec((1, POOL_WIDTH), lambda i: (0, 0)),
            pl.BlockSpec((D_MODEL, D_MODEL), lambda i: (0, 0)),
            pl.BlockSpec((1, D_MODEL), lambda i: (0, 0)),
            pl.BlockSpec((N_EXPERTS, D_MODEL), lambda i: (0, 0)),
            pl.BlockSpec((N_EXPERTS, 1), lambda i: (0, 0)),
        ],
        out_specs=[
            pl.BlockSpec((TM, D_MODEL), lambda i: (i, 0)),
            pl.BlockSpec((TM * PSLAB, LANES), lambda i: (i, 0)),
            tok_spec, tok_spec, tok_spec,
            pl.BlockSpec((N_EXPERTS, LANES), lambda i: (0, 0)),
        ],
        out_shape=[
            jax.ShapeDtypeStruct((T, D_MODEL), jnp.float32),
            jax.ShapeDtypeStruct((T * PSLAB, LANES), jnp.uint32),
            jax.ShapeDtypeStruct((TOP_K, T), jnp.int32),
            jax.ShapeDtypeStruct((TOP_K, T), jnp.float32),
            jax.ShapeDtypeStruct((TOP_K, T), jnp.int32),
            jax.ShapeDtypeStruct((N_EXPERTS, LANES), jnp.int32),
        ],
        scratch_shapes=[
            pltpu.VMEM((N_EXPERTS, LANES), jnp.float32),
        ],
        compiler_params=pltpu.CompilerParams(
            dimension_semantics=("arbitrary",), vmem_limit_bytes=VMEM_LIMIT),
        name="out_route",
    )(x2, ym, p, p, pool_w, pool_s, w_out, g2, wr_t, br)


def _plan(dest_flat, fill):
    n_assign = dest_flat.shape[0]
    n_table = fill.shape[0]
    mesh = plsc.VectorSubcoreMesh(core_axis_name="c", subcore_axis_name="s")

    @pl.kernel(out_type=jax.ShapeDtypeStruct((n_table,), jnp.int32), mesh=mesh,
               scratch_types=[pltpu.VMEM((n_table,), jnp.int32),
                              pltpu.VMEM((PLAN_CHUNK,), jnp.int32)],
               compiler_params=pltpu.CompilerParams(needs_layout_passes=False))
    def plan_kernel(dest_hbm, fill_hbm, out_hbm, table, chunk):
        first = jnp.logical_and(lax.axis_index("c") == 0, lax.axis_index("s") == 0)

        @pl.when(first)
        def _():
            pltpu.sync_copy(fill_hbm, table)

            @pl.loop(0, n_assign // PLAN_CHUNK)
            def _(ci):
                pltpu.sync_copy(dest_hbm.at[pl.ds(ci * PLAN_CHUNK, PLAN_CHUNK)], chunk)

                @pl.loop(0, PLAN_CHUNK // (SC_LANES * PLAN_UNROLL))
                def _(i):
                    for j in range(PLAN_UNROLL):
                        off = (i * PLAN_UNROLL + j) * SC_LANES
                        idx = chunk[pl.ds(off, SC_LANES)]
                        vals = (ci * PLAN_CHUNK + off
                                + lax.broadcasted_iota(jnp.int32, (SC_LANES,), 0))
                        plsc.store_scatter(table, [idx], vals)

            pltpu.sync_copy(table, out_hbm)

    return plan_kernel(dest_flat, fill)


def _expert_kernel(n_tok, bs_ref, slot_ref, h2_ref, wg_ref, bg_ref, wu_ref, bu_ref, wd_ref, bd_ref,
                   yt_ref, *scratch):
    TM = TM_EXPERT
    ROWS = TM * PSLAB
    e = pl.program_id(0)
    n_total = bs_ref[N_EXPERTS]
    xg = scratch[:NBUF]
    ys = scratch[NBUF:2 * NBUF]
    wgb_ref, wub_ref, wdb_ref, gsem, ssem = scratch[2 * NBUF:]

    def token_of(a):
        return a & (n_tok - 1) if n_tok & (n_tok - 1) == 0 else lax.rem(a, n_tok)

    def start_gather(blk, par):
        base = (blk + 1) * TM
        for r in range(TM):
            t = token_of(slot_ref[base + r])
            pltpu.make_async_copy(h2_ref.at[pl.ds(pl.multiple_of(t * PSLAB, PSLAB), PSLAB), :],
                                  xg[par].at[pl.ds(r * PSLAB, PSLAB), :], gsem.at[par]
                                  ).start(priority=ROW_DMA_PRIORITY)

    def wait_gather(par):
        pltpu.make_async_copy(h2_ref.at[pl.ds(0, ROWS), :], xg[0], gsem.at[par]).wait()

    def start_scatter(blk, par):
        base = (blk + 1) * TM
        for r in range(TM):
            a = slot_ref[base + r]
            pltpu.make_async_copy(ys[par].at[pl.ds(r * PSLAB, PSLAB), :],
                                  yt_ref.at[pl.ds(pl.multiple_of(a * PSLAB, PSLAB), PSLAB), :],
                                  ssem.at[par]).start(priority=ROW_DMA_PRIORITY)

    def wait_scatter(par):
        pltpu.make_async_copy(ys[0], yt_ref.at[pl.ds(0, ROWS), :], ssem.at[par]).wait()

    @pl.when(e == 0)
    def _():
        for blk in range(NBUF - 1):
            start_gather(blk, blk)
        for par in range(NBUF):
            ys[par][...] = jnp.zeros_like(ys[par])
            dump = yt_ref.at[pl.ds((n_tok * TOP_K + par * TM) * PSLAB, ROWS), :]
            cp = pltpu.make_async_copy(ys[par], dump, ssem.at[par])
            cp.start()
            cp.wait()

    wgb_ref[...] = wg_ref[0].astype(jnp.bfloat16)
    wub_ref[...] = wu_ref[0].astype(jnp.bfloat16)
    wdb_ref[...] = wd_ref[0].astype(jnp.bfloat16)

    def block_step(g, par):
        prv = (par + NBUF - 1) % NBUF
        wait_gather(par)

        @pl.when(g >= NBUF - 1)
        def _():
            wait_scatter(par)

        start_gather(g + NBUF - 1, prv)
        start_scatter(g - 1, prv)
        words = [xg[par][pl.ds(s, TM, stride=PSLAB), :] for s in range(PSLAB)]
        x = jnp.concatenate([_unpack_lo(w).astype(jnp.bfloat16) for w in words]
                            + [_unpack_hi(w).astype(jnp.bfloat16) for w in words], axis=1)
        gate = jnp.dot(x, wgb_ref[...], preferred_element_type=jnp.float32) + bg_ref[0]
        up = jnp.dot(x, wub_ref[...], preferred_element_type=jnp.float32) + bu_ref[0]
        gate = jnp.minimum(gate, SWIGLU_LIMIT)
        up = jnp.clip(up, -SWIGLU_LIMIT, SWIGLU_LIMIT)
        glu = gate * _sigmoid(SWIGLU_ALPHA * gate)
        act = (glu * (up + 1.0)).astype(jnp.bfloat16)
        y = jnp.dot(act, wdb_ref[...], preferred_element_type=jnp.float32) + bd_ref[0]
        packed = _pack_bf16_pairs(y)
        for s in range(PSLAB):
            ys[par][pl.ds(s, TM, stride=PSLAB), :] = packed[:, s * LANES:(s + 1) * LANES]

    def body(g, carry):
        for par in range(NBUF):
            pl.when(g % NBUF == par)(functools.partial(block_step, g, par))
        return carry

    lax.fori_loop(bs_ref[e], bs_ref[e + 1], body, 0)

    @pl.when(e == N_EXPERTS - 1)
    def _():
        g = n_total
        for par in range(NBUF):
            @pl.when((g - 1) % NBUF == par)
            def _():
                start_scatter(g - 1, par)
        for j in range(NBUF - 1):
            wait_gather((g + j) % NBUF)
        wait_scatter((g - 1) % NBUF)
        for j in range(2, NBUF + 1):
            @pl.when(g >= j - 1)
            def _():
                wait_scatter((g + NBUF - j) % NBUF)


def _experts(block_start, slot_buf, h2_slab, w_gate, b_gate, w_up, b_up, w_down, b_down, n_tok):
    TM = TM_EXPERT
    n_assign = n_tok * TOP_K
    w_spec = pl.BlockSpec((1, D_MODEL, D_FF), lambda e, bs, sl: (e, 0, 0))
    bias_spec = pl.BlockSpec((1, 1, D_FF), lambda e, bs, sl: (e, 0, 0))
    buf = pltpu.VMEM((TM * PSLAB, LANES), jnp.uint32)
    grid_spec = pltpu.PrefetchScalarGridSpec(
        num_scalar_prefetch=2,
        grid=(N_EXPERTS,),
        in_specs=[
            pl.BlockSpec(memory_space=pl.ANY),
            w_spec, bias_spec, w_spec, bias_spec, w_spec, bias_spec,
        ],
        out_specs=pl.BlockSpec(memory_space=pl.ANY),
        scratch_shapes=[
            *([buf] * (2 * NBUF)),
            pltpu.VMEM((D_MODEL, D_FF), jnp.bfloat16),
            pltpu.VMEM((D_MODEL, D_FF), jnp.bfloat16),
            pltpu.VMEM((D_FF, D_MODEL), jnp.bfloat16),
            pltpu.SemaphoreType.DMA((NBUF,)),
            pltpu.SemaphoreType.DMA((NBUF,)),
        ],
    )
    return pl.pallas_call(
        functools.partial(_expert_kernel, n_tok),
        grid_spec=grid_spec,
        out_shape=jax.ShapeDtypeStruct(((n_assign + NBUF * TM) * PSLAB, LANES), jnp.uint32),
        compiler_params=pltpu.CompilerParams(
            dimension_semantics=("arbitrary",), vmem_limit_bytes=VMEM_LIMIT),
        name="experts",
    )(block_start, slot_buf, h2_slab, w_gate, b_gate, w_up, b_up, w_down, b_down)


def _combine_kernel(normalize, x1_ref, y0_ref, y1_ref, y2_ref, y3_ref, gate_ref, g_ref, o_ref):
    TM = TM_PROJ
    gates = jnp.concatenate([gate_ref[...], jnp.zeros((8 - TOP_K, TM), jnp.float32)], axis=0)
    g_cols = jnp.transpose(gates)
    g_bc = [jnp.broadcast_to(g_cols[:, k:k + 1], (TM, LANES)) for k in range(TOP_K)]
    ssq = jnp.zeros((TM, LANES), jnp.float32)
    parts = [x1_ref[:, s * LANES:(s + 1) * LANES] for s in range(SLAB)]
    for s in range(PSLAB):
        for k, y_ref in enumerate((y0_ref, y1_ref, y2_ref, y3_ref)):
            w = y_ref[pl.ds(s, TM, stride=PSLAB), :]
            parts[s] = parts[s] + g_bc[k] * _unpack_lo(w)
            parts[PSLAB + s] = parts[PSLAB + s] + g_bc[k] * _unpack_hi(w)
    for acc in parts:
        ssq = ssq + acc * acc
    if normalize:
        inv = lax.rsqrt(jnp.sum(ssq, axis=-1, keepdims=True) * (1.0 / D_MODEL) + EPS)
        for s in range(SLAB):
            o_ref[:, s * LANES:(s + 1) * LANES] = parts[s] * inv * g_ref[:, s * LANES:(s + 1) * LANES]
    else:
        for s in range(SLAB):
            o_ref[:, s * LANES:(s + 1) * LANES] = parts[s]


def _combine(x1, y_tok, gate_t, gf, normalize):
    T = x1.shape[0]
    TM = TM_PROJ
    nt = T // TM

    def y_spec(k):
        return pl.BlockSpec((TM * PSLAB, LANES), lambda i: (k * nt + i, 0))

    return pl.pallas_call(
        functools.partial(_combine_kernel, normalize),
        grid=(nt,),
        in_specs=[
            pl.BlockSpec((TM, D_MODEL), lambda i: (i, 0)),
            y_spec(0), y_spec(1), y_spec(2), y_spec(3),
            pl.BlockSpec((TOP_K, TM), lambda i: (0, i)),
            pl.BlockSpec((1, D_MODEL), lambda i: (0, 0)),
        ],
        out_specs=pl.BlockSpec((TM, D_MODEL), lambda i: (i, 0)),
        out_shape=jax.ShapeDtypeStruct((T, D_MODEL), jnp.float32),
        compiler_params=pltpu.CompilerParams(
            dimension_semantics=("parallel",), vmem_limit_bytes=VMEM_LIMIT),
        name="combine",
    )(x1, y_tok, y_tok, y_tok, y_tok, gate_t, gf)


def kernel(x, norm1_g, w_in, ig_b, fg_b, conv_w, head_norm_g, pool_w, pool_scale, w_out, norm2_g,
           w_router, b_router, w_gate, b_gate, w_up, b_up, w_down, b_down, normf_g):
    B, S, D = x.shape
    T = B * S
    depth = norm1_g.shape[0]
    W = MLSTM_WIDTH
    f32, bf16 = jnp.float32, jnp.bfloat16

    L = CHUNK
    t_l = lax.broadcasted_iota(jnp.int32, (L, L), 0)
    t_r = lax.broadcasted_iota(jnp.int32, (L, L), 1)
    tri = (t_r <= t_l).astype(f32)
    shifts = jnp.stack([(t_l - t_r == CONV_WIDTH - 1 - j).astype(bf16)
                        for j in range(CONV_WIDTH - 1)])
    h_t = lax.broadcasted_iota(jnp.int32, (8, HALO), 0)
    h_r = lax.broadcasted_iota(jnp.int32, (8, HALO), 1)
    halo_shifts = jnp.stack([(h_r - HALO - h_t == -(CONV_WIDTH - 1 - j)).astype(bf16)
                             for j in range(CONV_WIDTH - 1)])

    n_assign = T * TOP_K
    n_blocks = -(-n_assign // TM_EXPERT) + N_EXPERTS
    n_rows = n_blocks * TM_EXPERT
    n_table = n_rows + NBUF * TM_EXPERT
    fill = n_assign + ((jnp.arange(n_table, dtype=jnp.int32) + (NBUF - 1) * TM_EXPERT)
                       % (NBUF * TM_EXPERT))
    x2 = x.reshape(T, D)
    for l in range(depth):
        w = w_in[l]
        w_main = jnp.concatenate([w[:, :4 * W], w[:, 4 * W + N_GATES:]], axis=1).astype(bf16)
        wg_t = jnp.zeros((BF16_SUBLANES, D), bf16).at[:N_GATES].set(
            w[:, 4 * W:4 * W + N_GATES].T.astype(bf16))
        p, gates_t = _in_proj(x2, norm1_g[l][None, :], w_main, wg_t)

        gate_b = jnp.concatenate([ig_b[l], fg_b[l]])[:, None].astype(f32)
        gates_b = gates_t.reshape(N_GATES, B, S).transpose(1, 0, 2)
        ym = _mlstm(p.reshape(B, S, N_MAIN), gates_b, conv_w[l].astype(f32), gate_b,
                    head_norm_g[l][None, :], tri, shifts, halo_shifts).reshape(T, W)

        x1, h2, idx_t, gate_t, rank_t, cnt = _out_route(
            x2, ym, p, pool_w[l].astype(bf16), pool_scale[l][None, :], w_out[l].astype(bf16),
            norm2_g[l][None, :], w_router[l].T.astype(bf16), b_router[l][:, None], S)

        counts = cnt[:, 0]
        padded = ((counts + TM_EXPERT - 1) // TM_EXPERT) * TM_EXPERT
        padded_end = jnp.cumsum(padded)
        padded_start = padded_end - padded
        expert_ids = jnp.arange(N_EXPERTS, dtype=jnp.int32)[:, None, None]
        start_of = jnp.sum(jnp.where(idx_t[None] == expert_ids, padded_start[:, None, None], 0), axis=0)
        dest = start_of + rank_t
        block_start = jnp.concatenate(
            [jnp.zeros((1,), jnp.int32), (padded_end // TM_EXPERT).astype(jnp.int32)])

        slot_buf = _plan(dest.reshape(-1) + TM_EXPERT, fill)
        y_tok = _experts(block_start, slot_buf, h2, w_gate[l], b_gate[l][:, None, :],
                         w_up[l], b_up[l][:, None, :], w_down[l], b_down[l][:, None, :], T)
        last = l + 1 == depth
        x2 = _combine(x1, y_tok, gate_t, normf_g[None, :], last)
    return x2.reshape(B, S, D)
```

```python
import functools

import jax
import jax.numpy as jnp
from jax import lax
from jax.experimental import pallas as pl
from jax.experimental.pallas import tpu as pltpu
from jax.experimental.pallas import tpu_sc as plsc

D_MODEL = 1024
MLSTM_WIDTH = 512
MLSTM_HEADS = 4
HEAD_DIM = 128
CONV_WIDTH = 4
POOL_WIDTH = 512
POOL_WINDOWS = (2, 4, 8, 16)
POOL_GROUP_DIM = 128
N_EXPERTS = 32
TOP_K = 4
D_FF = 1024
SWIGLU_LIMIT = 7.0
SWIGLU_ALPHA = 1.702
EPS = 1e-5

N_MAIN = 4 * MLSTM_WIDTH + POOL_WIDTH
N_GATES = 2 * MLSTM_HEADS

LANES = 128
BF16_SUBLANES = 16
VMEM_LIMIT = 56 * 1024 * 1024

TM_PROJ = 512
ROUTE_SUB = 2
CHUNK = 256
MLSTM_BATCH = 2
HALO = 16
TM_EXPERT = 256
NBUF = 4
ROW_DMA_PRIORITY = 1
SLAB = D_MODEL // LANES
PSLAB = SLAB // 2
PLAN_CHUNK = 8192
SC_LANES = 16
PLAN_UNROLL = 8

NT_DIMS = (((1,), (1,)), ((), ()))


def _sigmoid(x):
    return 1.0 / (1.0 + jnp.exp(-x))


def _pack_bf16_pairs(v):
    half = v.shape[1] // 2
    lo = pltpu.bitcast(v[:, :half].astype(jnp.bfloat16).astype(jnp.float32), jnp.uint32)
    hi = pltpu.bitcast(v[:, half:].astype(jnp.bfloat16).astype(jnp.float32), jnp.uint32)
    return (lo >> 16) | (hi & jnp.uint32(0xFFFF0000))


def _unpack_lo(w):
    return pltpu.bitcast(w << 16, jnp.float32)


def _unpack_hi(w):
    return pltpu.bitcast(w & jnp.uint32(0xFFFF0000), jnp.float32)


def _in_proj_kernel(x_ref, g_ref, wa_ref, wu_ref, wgt_ref, p_ref, gt_ref):
    x = x_ref[...]
    h = x * lax.rsqrt(jnp.mean(x * x, axis=-1, keepdims=True) + EPS) * g_ref[...]
    hb = h.astype(jnp.bfloat16)
    n_a = wa_ref.shape[1]
    p_ref[:, :n_a] = jnp.dot(hb, wa_ref[...], preferred_element_type=jnp.float32).astype(p_ref.dtype)
    p_ref[:, n_a:] = jnp.dot(hb, wu_ref[...], preferred_element_type=jnp.float32).astype(p_ref.dtype)
    gt = lax.dot_general(wgt_ref[...], hb, NT_DIMS, preferred_element_type=jnp.float32)
    gt_ref[...] = gt[:N_GATES]


def _in_proj(x2, g1, w_a, w_u, wg_t):
    T = x2.shape[0]
    return pl.pallas_call(
        _in_proj_kernel,
        grid=(T // TM_PROJ,),
        in_specs=[
            pl.BlockSpec((TM_PROJ, D_MODEL), lambda i: (i, 0)),
            pl.BlockSpec((1, D_MODEL), lambda i: (0, 0)),
            pl.BlockSpec(w_a.shape, lambda i: (0, 0)),
            pl.BlockSpec(w_u.shape, lambda i: (0, 0)),
            pl.BlockSpec((BF16_SUBLANES, D_MODEL), lambda i: (0, 0)),
        ],
        out_specs=[
            pl.BlockSpec((TM_PROJ, N_MAIN), lambda i: (i, 0)),
            pl.BlockSpec((N_GATES, TM_PROJ), lambda i: (0, i)),
        ],
        out_shape=[
            jax.ShapeDtypeStruct((T, N_MAIN), jnp.bfloat16),
            jax.ShapeDtypeStruct((N_GATES, T), jnp.float32),
        ],
        compiler_params=pltpu.CompilerParams(
            dimension_semantics=("parallel",), vmem_limit_bytes=VMEM_LIMIT),
        name="in_proj",
    )(x2, g1, w_a, w_u, wg_t)


def _mlstm_kernel(qk_ref, qkp_ref, v_ref, o_ref, gt_ref, convw_ref, gb_ref, hng_ref,
                  tri_ref, shift_ref, hshift_ref, y_ref, cn_ref, m_ref):
    L = CHUNK
    c = pl.program_id(1)

    @pl.when(c == 0)
    def _():
        cn_ref[...] = jnp.zeros_like(cn_ref)
        m_ref[...] = jnp.zeros_like(m_ref)

    row_id = lax.broadcasted_iota(jnp.int32, (L, L), 0)
    col_id = lax.broadcasted_iota(jnp.int32, (L, L), 1)
    causal = col_id <= row_id
    ones_blk = jnp.ones((L, HEAD_DIM), jnp.bfloat16)
    lane = lax.broadcasted_iota(jnp.int32, (MLSTM_HEADS, L), 1)

    gate_terms = []
    for bb in range(MLSTM_BATCH):
        gt = gt_ref[bb] + gb_ref[...]
        f = gt[MLSTM_HEADS:]
        lf = jnp.minimum(f, 0.0) - jnp.log(1.0 + jnp.exp(-jnp.abs(f)))
        ig = gt[:MLSTM_HEADS]
        b_rows = lax.dot_general(lf, tri_ref[...], NT_DIMS, precision=lax.Precision.HIGHEST,
                                 preferred_element_type=jnp.float32)
        c_rows = ig - b_rows
        cm_rows = c_rows
        d = 1
        while d < L:
            cm_rows = jnp.maximum(
                cm_rows, jnp.where(lane >= d, pltpu.roll(cm_rows, d, axis=1), -jnp.inf))
            d *= 2
        gate_terms.append((b_rows, c_rows, cm_rows))

    conv_terms = []
    for bb in range(MLSTM_BATCH):
        x_cur = qk_ref[bb]
        x_prev = jnp.where(c > 0, qkp_ref[bb], jnp.zeros((HALO, 2 * MLSTM_WIDTH), jnp.bfloat16))
        acc = convw_ref[CONV_WIDTH - 1:CONV_WIDTH, :] * x_cur.astype(jnp.float32)
        for j in range(CONV_WIDTH - 1):
            sh = jnp.dot(shift_ref[j], x_cur, preferred_element_type=jnp.float32)
            top = sh[:8] + jnp.dot(hshift_ref[j], x_prev, preferred_element_type=jnp.float32)
            sh = jnp.concatenate([top, sh[8:]], axis=0)
            acc = acc + convw_ref[j:j + 1, :] * sh
        qk = acc * _sigmoid(acc)
        q_all = qk[:, :MLSTM_WIDTH].astype(jnp.bfloat16)
        k_t = jnp.transpose(qk[:, MLSTM_WIDTH:] * (HEAD_DIM ** -0.5))
        conv_terms.append((q_all, k_t))

    for bb in range(MLSTM_BATCH):
        b_rows, c_rows, cm_rows = gate_terms[bb]
        q_all, k_t = conv_terms[bb]
        m_in4 = jnp.concatenate(
            [m_ref[bb * MLSTM_HEADS + h][0:1, 0:1] for h in range(MLSTM_HEADS)], axis=0)
        mx_rows = jnp.maximum(cm_rows, m_in4)
        inter_rows = jnp.exp(m_in4 - mx_rows)
        einv_rows = jnp.exp(-(b_rows + mx_rows))
        fac_t = jnp.transpose(jnp.concatenate(
            [mx_rows, inter_rows, einv_rows, jnp.zeros_like(mx_rows)], axis=0))

        for h in range(MLSTM_HEADS):
            lo = h * HEAD_DIM
            st = bb * MLSTM_HEADS + h
            q = q_all[:, lo:lo + HEAD_DIM]
            kt = k_t[lo:lo + HEAD_DIM, :]
            v_ext = jnp.concatenate([v_ref[bb, :, lo:lo + HEAD_DIM], ones_blk], axis=1)
            mx_col = fac_t[:, h:h + 1]
            inter_col = fac_t[:, MLSTM_HEADS + h:MLSTM_HEADS + h + 1]
            einv_col = fac_t[:, 2 * MLSTM_HEADS + h:2 * MLSTM_HEADS + h + 1]
            c_row = c_rows[h:h + 1, :]
            b_tot = b_rows[h:h + 1, L - 1:L]
            cm_tot = cm_rows[h:h + 1, L - 1:L]
            m_in = m_ref[st][0:1, 0:1]
            cn = cn_ref[st]

            s_qk = jnp.dot(q, kt.astype(jnp.bfloat16), preferred_element_type=jnp.float32)
            s = (s_qk * jnp.exp(jnp.where(causal, c_row - mx_col, -jnp.inf))).astype(jnp.bfloat16)
            num = (jnp.dot(s, v_ext, preferred_element_type=jnp.float32)
                   + inter_col * jnp.dot(q, cn.astype(jnp.bfloat16),
                                         preferred_element_type=jnp.float32))
            den = num[:, HEAD_DIM:]
            hh = num[:, :HEAD_DIM] / jnp.maximum(jnp.abs(den), einv_col)

            mu = jnp.mean(hh, axis=-1, keepdims=True)
            dv = hh - mu
            var = jnp.mean(dv * dv, axis=-1, keepdims=True)
            hn = dv * lax.rsqrt(var + EPS) * hng_ref[:, lo:lo + HEAD_DIM]
            og = _sigmoid(o_ref[bb, :, lo:lo + HEAD_DIM].astype(jnp.float32))
            y_ref[bb, :, lo:lo + HEAD_DIM] = (og * hn).astype(y_ref.dtype)

            m_loc = b_tot + cm_tot
            kw_t = (kt * jnp.exp(c_row - cm_tot)).astype(jnp.bfloat16)
            c_loc = jnp.dot(kw_t, v_ext, preferred_element_type=jnp.float32)
            m_new = jnp.maximum(b_tot + m_in, m_loc)
            s_old = jnp.exp(b_tot + m_in - m_new)
            s_loc = jnp.exp(m_loc - m_new)
            cn_ref[st] = s_old * cn + s_loc * c_loc
            m_ref[st] = jnp.broadcast_to(m_new, m_ref.shape[1:])


def _mlstm(p3, gates_b, conv_w, gate_b, hn_g, tri, shifts, halo_shifts):
    batch, seq, _ = p3.shape
    L = CHUNK
    BB = MLSTM_BATCH
    halo_per_chunk = L // HALO
    return pl.pallas_call(
        _mlstm_kernel,
        grid=(batch // BB, seq // L),
        in_specs=[
            pl.BlockSpec((BB, L, 2 * MLSTM_WIDTH), lambda bi, ci: (bi, ci, 0)),
            pl.BlockSpec((BB, HALO, 2 * MLSTM_WIDTH),
                         lambda bi, ci: (bi, jnp.maximum(ci * halo_per_chunk - 1, 0), 0)),
            pl.BlockSpec((BB, L, MLSTM_WIDTH), lambda bi, ci: (bi, ci, 2)),
            pl.BlockSpec((BB, L, MLSTM_WIDTH), lambda bi, ci: (bi, ci, 3)),
            pl.BlockSpec((BB, N_GATES, L), lambda bi, ci: (bi, 0, ci)),
            pl.BlockSpec((CONV_WIDTH, 2 * MLSTM_WIDTH), lambda bi, ci: (0, 0)),
            pl.BlockSpec((N_GATES, 1), lambda bi, ci: (0, 0)),
            pl.BlockSpec((1, MLSTM_WIDTH), lambda bi, ci: (0, 0)),
            pl.BlockSpec((L, L), lambda bi, ci: (0, 0)),
            pl.BlockSpec((CONV_WIDTH - 1, L, L), lambda bi, ci: (0, 0, 0)),
            pl.BlockSpec((CONV_WIDTH - 1, 8, HALO), lambda bi, ci: (0, 0, 0)),
        ],
        out_specs=pl.BlockSpec((BB, L, MLSTM_WIDTH), lambda bi, ci: (bi, ci, 0)),
        out_shape=jax.ShapeDtypeStruct((batch, seq, MLSTM_WIDTH), jnp.bfloat16),
        scratch_shapes=[
            pltpu.VMEM((BB * MLSTM_HEADS, HEAD_DIM, 2 * HEAD_DIM), jnp.float32),
            pltpu.VMEM((BB * MLSTM_HEADS, 8, LANES), jnp.float32),
        ],
        compiler_params=pltpu.CompilerParams(
            dimension_semantics=("parallel", "arbitrary"), vmem_limit_bytes=VMEM_LIMIT),
        name="mlstm",
    )(p3, p3, p3, p3, gates_b, conv_w, gate_b, hn_g, tri, shifts, halo_shifts)


def _out_route_kernel(seq, x_ref, ym_ref, u_ref, up_ref, pw_ref, ps_ref, wo_ref, g2_ref,
                      wrt_ref, br_ref, x1_ref, h2_ref, idx_ref, gate_ref, rank_ref, cnt_ref,
                      carry_ref):
    TM = TM_PROJ
    R = ROUTE_SUB * TM
    i = pl.program_id(0)

    @pl.when(i == 0)
    def _():
        carry_ref[...] = jnp.zeros_like(carry_ref)

    pos0 = (i * R) % seq
    e_id = lax.broadcasted_iota(jnp.int32, (N_EXPERTS, TM), 0).astype(jnp.float32)
    t_row = lax.broadcasted_iota(jnp.int32, (TM, TM), 0)
    t_col = lax.broadcasted_iota(jnp.int32, (TM, TM), 1)
    before = jnp.where(t_row < t_col, 1.0, 0.0).astype(jnp.bfloat16)
    carry = carry_ref[...]
    subs = [slice(sub * TM, (sub + 1) * TM) for sub in range(ROUTE_SUB)]

    halo = jnp.where(pos0 > 0, up_ref[...].astype(jnp.float32), 0.0)
    u_ext = jnp.concatenate([halo, u_ref[...].astype(jnp.float32)], axis=0)
    win_sums = []
    for gi, w in enumerate(POOL_WINDOWS):
        sw = u_ext[:, gi * POOL_GROUP_DIM:(gi + 1) * POOL_GROUP_DIM]
        span = 1
        while span < w:
            sw = sw + pltpu.roll(sw, span, axis=0)
            span *= 2
        win_sums.append(sw)
    y_cats = []
    for sub, rows in enumerate(subs):
        r0 = sub * TM
        pos = (pos0 + r0 + lax.broadcasted_iota(jnp.int32, (TM, 1), 0) + 1).astype(jnp.float32)
        mixed = []
        for gi, w in enumerate(POOL_WINDOWS):
            lo = gi * POOL_GROUP_DIM
            tok = u_ext[HALO + r0:HALO + r0 + TM, lo:lo + POOL_GROUP_DIM]
            pooled = win_sums[gi][HALO + r0:HALO + r0 + TM] / jnp.minimum(pos, float(w)) - tok
            mg = jnp.dot(pooled.astype(jnp.bfloat16), pw_ref[gi],
                         preferred_element_type=jnp.float32)
            mixed.append((mg * ps_ref[:, lo:lo + POOL_GROUP_DIM]).astype(jnp.bfloat16))
        y_cats.append(jnp.concatenate([ym_ref[rows, :]] + mixed, axis=1))

    all_logits = []
    for sub, rows in enumerate(subs):
        r0 = sub * TM
        x1 = x_ref[rows, :] + jnp.dot(y_cats[sub], wo_ref[...], preferred_element_type=jnp.float32)
        x1_ref[rows, :] = x1
        h2 = x1 * lax.rsqrt(jnp.mean(x1 * x1, axis=-1, keepdims=True) + EPS) * g2_ref[...]
        h2b = h2.astype(jnp.bfloat16)
        h2w = _pack_bf16_pairs(h2)
        for s in range(PSLAB):
            h2_ref[pl.ds(r0 * PSLAB + s, TM, stride=PSLAB), :] = h2w[:, s * LANES:(s + 1) * LANES]
        all_logits.append(lax.dot_general(wrt_ref[...], h2b, NT_DIMS,
                                          preferred_element_type=jnp.float32) + br_ref[...])

    for sub, rows in enumerate(subs):
        work = all_logits[sub]
        vals, ids, hots = [], [], []
        for _ in range(TOP_K):
            mk = jnp.max(work, axis=0, keepdims=True)
            ik = jnp.min(jnp.where(work == mk, e_id, float(N_EXPERTS)), axis=0, keepdims=True)
            hot = e_id == ik
            work = jnp.where(hot, -jnp.inf, work)
            vals.append(mk)
            ids.append(ik)
            hots.append(hot)
        ex = [jnp.exp(vk - vals[0]) for vk in vals]
        denom = ex[0] + ex[1] + ex[2] + ex[3]
        gate_ref[:, rows] = jnp.concatenate([e / denom for e in ex], axis=0)
        idx_ref[:, rows] = jnp.concatenate(ids, axis=0).astype(jnp.int32)

        sel_f = sum(jnp.where(hot, 1.0, 0.0) for hot in hots)
        prefix = jnp.dot(sel_f.astype(jnp.bfloat16), before, preferred_element_type=jnp.float32)
        rank_e = carry[:, 0:1] + prefix
        ranks = [jnp.sum(jnp.where(hot, rank_e, 0.0), axis=0, keepdims=True) for hot in hots]
        rank_ref[:, rows] = jnp.concatenate(ranks, axis=0).astype(jnp.int32)
        carry = carry + jnp.sum(sel_f, axis=1, keepdims=True)
    carry_ref[...] = carry
    cnt_ref[...] = carry.astype(jnp.int32)


def _out_route(x2, ym, p, pool_w, pool_s, w_out, g2, wr_t, br, seq):
    T = x2.shape[0]
    TM = ROUTE_SUB * TM_PROJ
    nt = T // TM
    u_blk = N_MAIN // POOL_WIDTH - 1
    halo_per_tile = TM // HALO
    tok_spec = pl.BlockSpec((TOP_K, TM), lambda i: (0, i))
    return pl.pallas_call(
        functools.partial(_out_route_kernel, seq),
        grid=(nt,),
        in_specs=[
            pl.BlockSpec((TM, D_MODEL), lambda i: (i, 0)),
            pl.BlockSpec((TM, MLSTM_WIDTH), lambda i: (i, 0)),
            pl.BlockSpec((TM, POOL_WIDTH), lambda i: (i, u_blk)),
            pl.BlockSpec((HALO, POOL_WIDTH),
                         lambda i: (jnp.maximum(i * halo_per_tile - 1, 0), u_blk)),
            pl.BlockSpec((len(POOL_WINDOWS), POOL_GROUP_DIM, POOL_GROUP_DIM), lambda i: (0, 0, 0)),
            pl.BlockSpec((1, POOL_WIDTH), lambda i: (0, 0)),
            pl.BlockSpec((D_MODEL, D_MODEL), lambda i: (0, 0)),
            pl.BlockSpec((1, D_MODEL), lambda i: (0, 0)),
            pl.BlockSpec((N_EXPERTS, D_MODEL), lambda i: (0, 0)),
            pl.BlockSpec((N_EXPERTS, 1), lambda i: (0, 0)),
        ],
        out_specs=[
            pl.BlockSpec((TM, D_MODEL), lambda i: (i, 0)),
            pl.BlockSpec((TM * PSLAB, LANES), lambda i: (i, 0)),
            tok_spec, tok_spec, tok_spec,
            pl.BlockSpec((N_EXPERTS, LANES), lambda i: (0, 0)),
        ],
        out_shape=[
            jax.ShapeDtypeStruct((T, D_MODEL), jnp.float32),
            jax.ShapeDtypeStruct((T * PSLAB, LANES), jnp.uint32),
            jax.ShapeDtypeStruct((TOP_K, T), jnp.int32),
            jax.ShapeDtypeStruct((TOP_K, T), jnp.float32),
            jax.ShapeDtypeStruct((TOP_K, T), jnp.int32),
            jax.ShapeDtypeStruct((N_EXPERTS, LANES), jnp.int32),
        ],
        scratch_shapes=[
            pltpu.VMEM((N_EXPERTS, LANES), jnp.float32),
        ],
        compiler_params=pltpu.CompilerParams(
            dimension_semantics=("arbitrary",), vmem_limit_bytes=VMEM_LIMIT),
        name="out_route",
    )(x2, ym, p, p, pool_w, pool_s, w_out, g2, wr_t, br)


def _plan(dest_flat, fill):
    n_assign = dest_flat.shape[0]
    n_table = fill.shape[0]
    mesh = plsc.VectorSubcoreMesh(core_axis_name="c", subcore_axis_name="s")

    @pl.kernel(out_type=jax.ShapeDtypeStruct((n_table,), jnp.int32), mesh=mesh,
               scratch_types=[pltpu.VMEM((n_table,), jnp.int32),
                              pltpu.VMEM((PLAN_CHUNK,), jnp.int32)],
               compiler_params=pltpu.CompilerParams(needs_layout_passes=False))
    def plan_kernel(dest_hbm, fill_hbm, out_hbm, table, chunk):
        first = jnp.logical_and(lax.axis_index("c") == 0, lax.axis_index("s") == 0)

        @pl.when(first)
        def _():
            pltpu.sync_copy(fill_hbm, table)

            @pl.loop(0, n_assign // PLAN_CHUNK)
            def _(ci):
                pltpu.sync_copy(dest_hbm.at[pl.ds(ci * PLAN_CHUNK, PLAN_CHUNK)], chunk)

                @pl.loop(0, PLAN_CHUNK // (SC_LANES * PLAN_UNROLL))
                def _(i):
                    for j in range(PLAN_UNROLL):
                        off = (i * PLAN_UNROLL + j) * SC_LANES
                        idx = chunk[pl.ds(off, SC_LANES)]
                        vals = (ci * PLAN_CHUNK + off
                                + lax.broadcasted_iota(jnp.int32, (SC_LANES,), 0))
                        plsc.store_scatter(table, [idx], vals)

            pltpu.sync_copy(table, out_hbm)

    return plan_kernel(dest_flat, fill)


def _expert_kernel(n_tok, bs_ref, slot_ref, h2_ref, wg_ref, bg_ref, wu_ref, bu_ref, wd_ref, bd_ref,
                   yt_ref, *scratch):
    TM = TM_EXPERT
    ROWS = TM * PSLAB
    e = pl.program_id(0)
    n_total = bs_ref[N_EXPERTS]
    xg = scratch[:NBUF]
    ys = scratch[NBUF:2 * NBUF]
    wgb_ref, wub_ref, wdb_ref, gsem, ssem = scratch[2 * NBUF:]

    def token_of(a):
        return a & (n_tok - 1) if n_tok & (n_tok - 1) == 0 else lax.rem(a, n_tok)

    def start_gather(blk, par):
        base = (blk + 1) * TM
        for r in range(TM):
            t = token_of(slot_ref[base + r])
            pltpu.make_async_copy(h2_ref.at[pl.ds(pl.multiple_of(t * PSLAB, PSLAB), PSLAB), :],
                                  xg[par].at[pl.ds(r * PSLAB, PSLAB), :], gsem.at[par]
                                  ).start(priority=ROW_DMA_PRIORITY)

    def wait_gather(par):
        pltpu.make_async_copy(h2_ref.at[pl.ds(0, ROWS), :], xg[0], gsem.at[par]).wait()

    def start_scatter(blk, par):
        base = (blk + 1) * TM
        for r in range(TM):
            a = slot_ref[base + r]
            pltpu.make_async_copy(ys[par].at[pl.ds(r * PSLAB, PSLAB), :],
                                  yt_ref.at[pl.ds(pl.multiple_of(a * PSLAB, PSLAB), PSLAB), :],
                                  ssem.at[par]).start(priority=ROW_DMA_PRIORITY)

    def wait_scatter(par):
        pltpu.make_async_copy(ys[0], yt_ref.at[pl.ds(0, ROWS), :], ssem.at[par]).wait()

    @pl.when(e == 0)
    def _():
        for blk in range(NBUF - 1):
            start_gather(blk, blk)
        for par in range(NBUF):
            ys[par][...] = jnp.zeros_like(ys[par])
            dump = yt_ref.at[pl.ds((n_tok * TOP_K + par * TM) * PSLAB, ROWS), :]
            cp = pltpu.make_async_copy(ys[par], dump, ssem.at[par])
            cp.start()
            cp.wait()

    wgb_ref[...] = wg_ref[0].astype(jnp.bfloat16)
    wub_ref[...] = wu_ref[0].astype(jnp.bfloat16)
    wdb_ref[...] = wd_ref[0].astype(jnp.bfloat16)

    def block_step(g, par):
        prv = (par + NBUF - 1) % NBUF
        wait_gather(par)

        @pl.when(g >= NBUF - 1)
        def _():
            wait_scatter(par)

        start_gather(g + NBUF - 1, prv)
        start_scatter(g - 1, prv)
        words = [xg[par][pl.ds(s, TM, stride=PSLAB), :] for s in range(PSLAB)]
        x = jnp.concatenate([_unpack_lo(w).astype(jnp.bfloat16) for w in words]
                            + [_unpack_hi(w).astype(jnp.bfloat16) for w in words], axis=1)
        gate = jnp.dot(x, wgb_ref[...], preferred_element_type=jnp.float32) + bg_ref[0]
        up = jnp.dot(x, wub_ref[...], preferred_element_type=jnp.float32) + bu_ref[0]
        gate = jnp.minimum(gate, SWIGLU_LIMIT)
        up = jnp.clip(up, -SWIGLU_LIMIT, SWIGLU_LIMIT)
        glu = gate * _sigmoid(SWIGLU_ALPHA * gate)
        act = (glu * (up + 1.0)).astype(jnp.bfloat16)
        y = jnp.dot(act, wdb_ref[...], preferred_element_type=jnp.float32) + bd_ref[0]
        packed = _pack_bf16_pairs(y)
        for s in range(PSLAB):
            ys[par][pl.ds(s, TM, stride=PSLAB), :] = packed[:, s * LANES:(s + 1) * LANES]

    def body(g, carry):
        for par in range(NBUF):
            pl.when(g % NBUF == par)(functools.partial(block_step, g, par))
        return carry

    lax.fori_loop(bs_ref[e], bs_ref[e + 1], body, 0)

    @pl.when(e == N_EXPERTS - 1)
    def _():
        g = n_total
        for par in range(NBUF):
            @pl.when((g - 1) % NBUF == par)
            def _():
                start_scatter(g - 1, par)
        for j in range(NBUF - 1):
            wait_gather((g + j) % NBUF)
        wait_scatter((g - 1) % NBUF)
        for j in range(2, NBUF + 1):
            @pl.when(g >= j - 1)
            def _():
                wait_scatter((g + NBUF - j) % NBUF)


def _experts(block_start, slot_buf, h2_slab, w_gate, b_gate, w_up, b_up, w_down, b_down, n_tok):
    TM = TM_EXPERT
    n_assign = n_tok * TOP_K
    w_spec = pl.BlockSpec((1, D_MODEL, D_FF), lambda e, bs, sl: (e, 0, 0))
    bias_spec = pl.BlockSpec((1, 1, D_FF), lambda e, bs, sl: (e, 0, 0))
    buf = pltpu.VMEM((TM * PSLAB, LANES), jnp.uint32)
    grid_spec = pltpu.PrefetchScalarGridSpec(
        num_scalar_prefetch=2,
        grid=(N_EXPERTS,),
        in_specs=[
            pl.BlockSpec(memory_space=pl.ANY),
            w_spec, bias_spec, w_spec, bias_spec, w_spec, bias_spec,
        ],
        out_specs=pl.BlockSpec(memory_space=pl.ANY),
        scratch_shapes=[
            *([buf] * (2 * NBUF)),
            pltpu.VMEM((D_MODEL, D_FF), jnp.bfloat16),
            pltpu.VMEM((D_MODEL, D_FF), jnp.bfloat16),
            pltpu.VMEM((D_FF, D_MODEL), jnp.bfloat16),
            pltpu.SemaphoreType.DMA((NBUF,)),
            pltpu.SemaphoreType.DMA((NBUF,)),
        ],
    )
    return pl.pallas_call(
        functools.partial(_expert_kernel, n_tok),
        grid_spec=grid_spec,
        out_shape=jax.ShapeDtypeStruct(((n_assign + NBUF * TM) * PSLAB, LANES), jnp.uint32),
        compiler_params=pltpu.CompilerParams(
            dimension_semantics=("arbitrary",), vmem_limit_bytes=VMEM_LIMIT),
        name="experts",
    )(block_start, slot_buf, h2_slab, w_gate, b_gate, w_up, b_up, w_down, b_down)


def _combine_kernel(normalize, x1_ref, y0_ref, y1_ref, y2_ref, y3_ref, gate_ref, g_ref, o_ref):
    TM = TM_PROJ
    gates = jnp.concatenate([gate_ref[...], jnp.zeros((8 - TOP_K, TM), jnp.float32)], axis=0)
    g_cols = jnp.transpose(gates)
    g_bc = [jnp.broadcast_to(g_cols[:, k:k + 1], (TM, LANES)) for k in range(TOP_K)]
    ssq = jnp.zeros((TM, LANES), jnp.float32)
    parts = [x1_ref[:, s * LANES:(s + 1) * LANES] for s in range(SLAB)]
    for s in range(PSLAB):
        for k, y_ref in enumerate((y0_ref, y1_ref, y2_ref, y3_ref)):
            w = y_ref[pl.ds(s, TM, stride=PSLAB), :]
            parts[s] = parts[s] + g_bc[k] * _unpack_lo(w)
            parts[PSLAB + s] = parts[PSLAB + s] + g_bc[k] * _unpack_hi(w)
    for acc in parts:
        ssq = ssq + acc * acc
    if normalize:
        inv = lax.rsqrt(jnp.sum(ssq, axis=-1, keepdims=True) * (1.0 / D_MODEL) + EPS)
        for s in range(SLAB):
            o_ref[:, s * LANES:(s + 1) * LANES] = parts[s] * inv * g_ref[:, s * LANES:(s + 1) * LANES]
    else:
        for s in range(SLAB):
            o_ref[:, s * LANES:(s + 1) * LANES] = parts[s]


def _combine(x1, y_tok, gate_t, gf, normalize):
    T = x1.shape[0]
    TM = TM_PROJ
    nt = T // TM

    def y_spec(k):
        return pl.BlockSpec((TM * PSLAB, LANES), lambda i: (k * nt + i, 0))

    return pl.pallas_call(
        functools.partial(_combine_kernel, normalize),
        grid=(nt,),
        in_specs=[
            pl.BlockSpec((TM, D_MODEL), lambda i: (i, 0)),
            y_spec(0), y_spec(1), y_spec(2), y_spec(3),
            pl.BlockSpec((TOP_K, TM), lambda i: (0, i)),
            pl.BlockSpec((1, D_MODEL), lambda i: (0, 0)),
        ],
        out_specs=pl.BlockSpec((TM, D_MODEL), lambda i: (i, 0)),
        out_shape=jax.ShapeDtypeStruct((T, D_MODEL), jnp.float32),
        compiler_params=pltpu.CompilerParams(
            dimension_semantics=("parallel",), vmem_limit_bytes=VMEM_LIMIT),
        name="combine",
    )(x1, y_tok, y_tok, y_tok, y_tok, gate_t, gf)


def kernel(x, norm1_g, w_in, ig_b, fg_b, conv_w, head_norm_g, pool_w, pool_scale, w_out, norm2_g,
           w_router, b_router, w_gate, b_gate, w_up, b_up, w_down, b_down, normf_g):
    B, S, D = x.shape
    T = B * S
    depth = norm1_g.shape[0]
    W = MLSTM_WIDTH
    f32, bf16 = jnp.float32, jnp.bfloat16

    L = CHUNK
    t_l = lax.broadcasted_iota(jnp.int32, (L, L), 0)
    t_r = lax.broadcasted_iota(jnp.int32, (L, L), 1)
    tri = (t_r <= t_l).astype(f32)
    shifts = jnp.stack([(t_l - t_r == CONV_WIDTH - 1 - j).astype(bf16)
                        for j in range(CONV_WIDTH - 1)])
    h_t = lax.broadcasted_iota(jnp.int32, (8, HALO), 0)
    h_r = lax.broadcasted_iota(jnp.int32, (8, HALO), 1)
    halo_shifts = jnp.stack([(h_r - HALO - h_t == -(CONV_WIDTH - 1 - j)).astype(bf16)
                             for j in range(CONV_WIDTH - 1)])

    n_assign = T * TOP_K
    n_blocks = -(-n_assign // TM_EXPERT) + N_EXPERTS
    n_rows = n_blocks * TM_EXPERT
    n_table = n_rows + NBUF * TM_EXPERT
    fill = n_assign + ((jnp.arange(n_table, dtype=jnp.int32) + (NBUF - 1) * TM_EXPERT)
                       % (NBUF * TM_EXPERT))
    x2 = x.reshape(T, D)
    for l in range(depth):
        w = w_in[l]
        w_a = w[:, :4 * W].astype(bf16)
        w_u = w[:, 4 * W + N_GATES:].astype(bf16)
        wg_t = jnp.zeros((BF16_SUBLANES, D), bf16).at[:N_GATES].set(
            w[:, 4 * W:4 * W + N_GATES].T.astype(bf16))
        p, gates_t = _in_proj(x2, norm1_g[l][None, :], w_a, w_u, wg_t)

        gate_b = jnp.concatenate([ig_b[l], fg_b[l]])[:, None].astype(f32)
        gates_b = gates_t.reshape(N_GATES, B, S).transpose(1, 0, 2)
        ym = _mlstm(p.reshape(B, S, N_MAIN), gates_b, conv_w[l].astype(f32), gate_b,
                    head_norm_g[l][None, :], tri, shifts, halo_shifts).reshape(T, W)

        x1, h2, idx_t, gate_t, rank_t, cnt = _out_route(
            x2, ym, p, pool_w[l].astype(bf16), pool_scale[l][None, :], w_out[l].astype(bf16),
            norm2_g[l][None, :], w_router[l].T.astype(bf16), b_router[l][:, None], S)

        counts = cnt[:, 0]
        padded = ((counts + TM_EXPERT - 1) // TM_EXPERT) * TM_EXPERT
        padded_end = jnp.cumsum(padded)
        padded_start = padded_end - padded
        expert_ids = jnp.arange(N_EXPERTS, dtype=jnp.int32)[:, None, None]
        start_of = jnp.sum(jnp.where(idx_t[None] == expert_ids, padded_start[:, None, None], 0), axis=0)
        dest = start_of + rank_t
        block_start = jnp.concatenate(
            [jnp.zeros((1,), jnp.int32), (padded_end // TM_EXPERT).astype(jnp.int32)])

        slot_buf = _plan(dest.reshape(-1) + TM_EXPERT, fill)
        y_tok = _experts(block_start, slot_buf, h2, w_gate[l], b_gate[l][:, None, :],
                         w_up[l], b_up[l][:, None, :], w_down[l], b_down[l][:, None, :], T)
        last = l + 1 == depth
        x2 = _combine(x1, y_tok, gate_t, normf_g[None, :], last)
    return x2.reshape(B, S, D)
```

```python
import functools

import jax
import jax.numpy as jnp
from jax import lax
from jax.experimental import pallas as pl
from jax.experimental.pallas import tpu as pltpu
from jax.experimental.pallas import tpu_sc as plsc

D_MODEL = 1024
MLSTM_WIDTH = 512
MLSTM_HEADS = 4
HEAD_DIM = 128
CONV_WIDTH = 4
POOL_WIDTH = 512
POOL_WINDOWS = (2, 4, 8, 16)
POOL_GROUP_DIM = 128
N_EXPERTS = 32
TOP_K = 4
D_FF = 1024
SWIGLU_LIMIT = 7.0
SWIGLU_ALPHA = 1.702
EPS = 1e-5

N_MAIN = 4 * MLSTM_WIDTH + POOL_WIDTH
N_GATES = 2 * MLSTM_HEADS

LANES = 128
BF16_SUBLANES = 16
VMEM_LIMIT = 56 * 1024 * 1024

TM_PROJ = 512
ROUTE_SUB = 2
CHUNK = 256
MLSTM_BATCH = 2
HALO = 16
TM_EXPERT = 256
NBUF = 4
ROW_DMA_PRIORITY = 1
SLAB = D_MODEL // LANES
PSLAB = SLAB // 2
PLAN_CHUNK = 8192
SC_LANES = 16
PLAN_UNROLL = 8

NT_DIMS = (((1,), (1,)), ((), ()))


def _sigmoid(x):
    return 1.0 / (1.0 + jnp.exp(-x))


def _pack_bf16_pairs(v):
    half = v.shape[1] // 2
    lo = pltpu.bitcast(v[:, :half].astype(jnp.bfloat16).astype(jnp.float32), jnp.uint32)
    hi = pltpu.bitcast(v[:, half:].astype(jnp.bfloat16).astype(jnp.float32), jnp.uint32)
    return (lo >> 16) | (hi & jnp.uint32(0xFFFF0000))


def _unpack_lo(w):
    return pltpu.bitcast(w << 16, jnp.float32)


def _unpack_hi(w):
    return pltpu.bitcast(w & jnp.uint32(0xFFFF0000), jnp.float32)


def _in_proj_kernel(x_ref, g_ref, w_ref, wgt_ref, p_ref, gt_ref, wa_ref, wu_ref):
    n_a = wa_ref.shape[1]

    @pl.when(pl.program_id(0) == 0)
    def _():
        wa_ref[...] = w_ref[:, :n_a].astype(jnp.bfloat16)
        wu_ref[...] = w_ref[:, n_a + N_GATES:].astype(jnp.bfloat16)

    x = x_ref[...]
    h = x * lax.rsqrt(jnp.mean(x * x, axis=-1, keepdims=True) + EPS) * g_ref[...]
    hb = h.astype(jnp.bfloat16)
    p_ref[:, :n_a] = jnp.dot(hb, wa_ref[...], preferred_element_type=jnp.float32).astype(p_ref.dtype)
    p_ref[:, n_a:] = jnp.dot(hb, wu_ref[...], preferred_element_type=jnp.float32).astype(p_ref.dtype)
    gt = lax.dot_general(wgt_ref[...], hb, NT_DIMS, preferred_element_type=jnp.float32)
    gt_ref[...] = gt[:N_GATES]


def _in_proj(x2, g1, w, wg_t):
    T = x2.shape[0]
    return pl.pallas_call(
        _in_proj_kernel,
        grid=(T // TM_PROJ,),
        in_specs=[
            pl.BlockSpec((TM_PROJ, D_MODEL), lambda i: (i, 0)),
            pl.BlockSpec((1, D_MODEL), lambda i: (0, 0)),
            pl.BlockSpec(w.shape, lambda i: (0, 0)),
            pl.BlockSpec((BF16_SUBLANES, D_MODEL), lambda i: (0, 0)),
        ],
        out_specs=[
            pl.BlockSpec((TM_PROJ, N_MAIN), lambda i: (i, 0)),
            pl.BlockSpec((N_GATES, TM_PROJ), lambda i: (0, i)),
        ],
        out_shape=[
            jax.ShapeDtypeStruct((T, N_MAIN), jnp.bfloat16),
            jax.ShapeDtypeStruct((N_GATES, T), jnp.float32),
        ],
        scratch_shapes=[
            pltpu.VMEM((D_MODEL, 4 * MLSTM_WIDTH), jnp.bfloat16),
            pltpu.VMEM((D_MODEL, POOL_WIDTH), jnp.bfloat16),
        ],
        compiler_params=pltpu.CompilerParams(
            dimension_semantics=("arbitrary",), vmem_limit_bytes=VMEM_LIMIT),
        name="in_proj",
    )(x2, g1, w, wg_t)


def _mlstm_kernel(qk_ref, qkp_ref, v_ref, o_ref, gt_ref, convw_ref, gb_ref, hng_ref,
                  tri_ref, shift_ref, hshift_ref, y_ref, cn_ref, m_ref):
    L = CHUNK
    c = pl.program_id(1)

    @pl.when(c == 0)
    def _():
        cn_ref[...] = jnp.zeros_like(cn_ref)
        m_ref[...] = jnp.zeros_like(m_ref)

    row_id = lax.broadcasted_iota(jnp.int32, (L, L), 0)
    col_id = lax.broadcasted_iota(jnp.int32, (L, L), 1)
    causal = col_id <= row_id
    ones_blk = jnp.ones((L, HEAD_DIM), jnp.bfloat16)
    lane = lax.broadcasted_iota(jnp.int32, (MLSTM_HEADS, L), 1)

    gate_terms = []
    for bb in range(MLSTM_BATCH):
        gt = gt_ref[bb] + gb_ref[...]
        f = gt[MLSTM_HEADS:]
        lf = jnp.minimum(f, 0.0) - jnp.log(1.0 + jnp.exp(-jnp.abs(f)))
        ig = gt[:MLSTM_HEADS]
        b_rows = lax.dot_general(lf, tri_ref[...], NT_DIMS, precision=lax.Precision.HIGHEST,
                                 preferred_element_type=jnp.float32)
        c_rows = ig - b_rows
        cm_rows = c_rows
        d = 1
        while d < L:
            cm_rows = jnp.maximum(
                cm_rows, jnp.where(lane >= d, pltpu.roll(cm_rows, d, axis=1), -jnp.inf))
            d *= 2
        gate_terms.append((b_rows, c_rows, cm_rows))

    conv_terms = []
    for bb in range(MLSTM_BATCH):
        x_cur = qk_ref[bb]
        x_prev = jnp.where(c > 0, qkp_ref[bb], jnp.zeros((HALO, 2 * MLSTM_WIDTH), jnp.bfloat16))
        acc = convw_ref[CONV_WIDTH - 1:CONV_WIDTH, :] * x_cur.astype(jnp.float32)
        for j in range(CONV_WIDTH - 1):
            sh = jnp.dot(shift_ref[j], x_cur, preferred_element_type=jnp.float32)
            top = sh[:8] + jnp.dot(hshift_ref[j], x_prev, preferred_element_type=jnp.float32)
            sh = jnp.concatenate([top, sh[8:]], axis=0)
            acc = acc + convw_ref[j:j + 1, :] * sh
        qk = acc * _sigmoid(acc)
        q_all = qk[:, :MLSTM_WIDTH].astype(jnp.bfloat16)
        k_t = jnp.transpose(qk[:, MLSTM_WIDTH:] * (HEAD_DIM ** -0.5))
        conv_terms.append((q_all, k_t))

    for bb in range(MLSTM_BATCH):
        b_rows, c_rows, cm_rows = gate_terms[bb]
        q_all, k_t = conv_terms[bb]
        m_in4 = jnp.concatenate(
            [m_ref[bb * MLSTM_HEADS + h][0:1, 0:1] for h in range(MLSTM_HEADS)], axis=0)
        mx_rows = jnp.maximum(cm_rows, m_in4)
        inter_rows = jnp.exp(m_in4 - mx_rows)
        einv_rows = jnp.exp(-(b_rows + mx_rows))
        fac_t = jnp.transpose(jnp.concatenate(
            [mx_rows, inter_rows, einv_rows, jnp.zeros_like(mx_rows)], axis=0))

        for h in range(MLSTM_HEADS):
            lo = h * HEAD_DIM
            st = bb * MLSTM_HEADS + h
            q = q_all[:, lo:lo + HEAD_DIM]
            kt = k_t[lo:lo + HEAD_DIM, :]
            v_ext = jnp.concatenate([v_ref[bb, :, lo:lo + HEAD_DIM], ones_blk], axis=1)
            mx_col = fac_t[:, h:h + 1]
            inter_col = fac_t[:, MLSTM_HEADS + h:MLSTM_HEADS + h + 1]
            einv_col = fac_t[:, 2 * MLSTM_HEADS + h:2 * MLSTM_HEADS + h + 1]
            c_row = c_rows[h:h + 1, :]
            b_tot = b_rows[h:h + 1, L - 1:L]
            cm_tot = cm_rows[h:h + 1, L - 1:L]
            m_in = m_ref[st][0:1, 0:1]
            cn = cn_ref[st]

            s_qk = jnp.dot(q, kt.astype(jnp.bfloat16), preferred_element_type=jnp.float32)
            s = (s_qk * jnp.exp(jnp.where(causal, c_row - mx_col, -jnp.inf))).astype(jnp.bfloat16)
            num = (jnp.dot(s, v_ext, preferred_element_type=jnp.float32)
                   + inter_col * jnp.dot(q, cn.astype(jnp.bfloat16),
                                         preferred_element_type=jnp.float32))
            den = num[:, HEAD_DIM:]
            hh = num[:, :HEAD_DIM] / jnp.maximum(jnp.abs(den), einv_col)

            mu = jnp.mean(hh, axis=-1, keepdims=True)
            dv = hh - mu
            var = jnp.mean(dv * dv, axis=-1, keepdims=True)
            hn = dv * lax.rsqrt(var + EPS) * hng_ref[:, lo:lo + HEAD_DIM]
            og = _sigmoid(o_ref[bb, :, lo:lo + HEAD_DIM].astype(jnp.float32))
            y_ref[bb, :, lo:lo + HEAD_DIM] = (og * hn).astype(y_ref.dtype)

            m_loc = b_tot + cm_tot
            kw_t = (kt * jnp.exp(c_row - cm_tot)).astype(jnp.bfloat16)
            c_loc = jnp.dot(kw_t, v_ext, preferred_element_type=jnp.float32)
            m_new = jnp.maximum(b_tot + m_in, m_loc)
            s_old = jnp.exp(b_tot + m_in - m_new)
            s_loc = jnp.exp(m_loc - m_new)
            cn_ref[st] = s_old * cn + s_loc * c_loc
            m_ref[st] = jnp.broadcast_to(m_new, m_ref.shape[1:])


def _mlstm(p3, gates_b, conv_w, gate_b, hn_g, tri, shifts, halo_shifts):
    batch, seq, _ = p3.shape
    L = CHUNK
    BB = MLSTM_BATCH
    halo_per_chunk = L // HALO
    return pl.pallas_call(
        _mlstm_kernel,
        grid=(batch // BB, seq // L),
        in_specs=[
            pl.BlockSpec((BB, L, 2 * MLSTM_WIDTH), lambda bi, ci: (bi, ci, 0)),
            pl.BlockSpec((BB, HALO, 2 * MLSTM_WIDTH),
                         lambda bi, ci: (bi, jnp.maximum(ci * halo_per_chunk - 1, 0), 0)),
            pl.BlockSpec((BB, L, MLSTM_WIDTH), lambda bi, ci: (bi, ci, 2)),
            pl.BlockSpec((BB, L, MLSTM_WIDTH), lambda bi, ci: (bi, ci, 3)),
            pl.BlockSpec((BB, N_GATES, L), lambda bi, ci: (bi, 0, ci)),
            pl.BlockSpec((CONV_WIDTH, 2 * MLSTM_WIDTH), lambda bi, ci: (0, 0)),
            pl.BlockSpec((N_GATES, 1), lambda bi, ci: (0, 0)),
            pl.BlockSpec((1, MLSTM_WIDTH), lambda bi, ci: (0, 0)),
            pl.BlockSpec((L, L), lambda bi, ci: (0, 0)),
            pl.BlockSpec((CONV_WIDTH - 1, L, L), lambda bi, ci: (0, 0, 0)),
            pl.BlockSpec((CONV_WIDTH - 1, 8, HALO), lambda bi, ci: (0, 0, 0)),
        ],
        out_specs=pl.BlockSpec((BB, L, MLSTM_WIDTH), lambda bi, ci: (bi, ci, 0)),
        out_shape=jax.ShapeDtypeStruct((batch, seq, MLSTM_WIDTH), jnp.bfloat16),
        scratch_shapes=[
            pltpu.VMEM((BB * MLSTM_HEADS, HEAD_DIM, 2 * HEAD_DIM), jnp.float32),
            pltpu.VMEM((BB * MLSTM_HEADS, 8, LANES), jnp.float32),
        ],
        compiler_params=pltpu.CompilerParams(
            dimension_semantics=("parallel", "arbitrary"), vmem_limit_bytes=VMEM_LIMIT),
        name="mlstm",
    )(p3, p3, p3, p3, gates_b, conv_w, gate_b, hn_g, tri, shifts, halo_shifts)


def _out_route_kernel(seq, x_ref, ym_ref, u_ref, up_ref, pw_ref, ps_ref, wo_ref, g2_ref,
                      wrt_ref, br_ref, x1_ref, h2_ref, idx_ref, gate_ref, rank_ref, cnt_ref,
                      carry_ref, wob_ref):
    TM = TM_PROJ
    R = ROUTE_SUB * TM
    i = pl.program_id(0)

    @pl.when(i == 0)
    def _():
        carry_ref[...] = jnp.zeros_like(carry_ref)
        wob_ref[...] = wo_ref[...].astype(jnp.bfloat16)

    pos0 = (i * R) % seq
    e_id = lax.broadcasted_iota(jnp.int32, (N_EXPERTS, TM), 0).astype(jnp.float32)
    t_row = lax.broadcasted_iota(jnp.int32, (TM, TM), 0)
    t_col = lax.broadcasted_iota(jnp.int32, (TM, TM), 1)
    before = jnp.where(t_row < t_col, 1.0, 0.0).astype(jnp.bfloat16)
    carry = carry_ref[...]
    subs = [slice(sub * TM, (sub + 1) * TM) for sub in range(ROUTE_SUB)]

    halo = jnp.where(pos0 > 0, up_ref[...].astype(jnp.float32), 0.0)
    u_ext = jnp.concatenate([halo, u_ref[...].astype(jnp.float32)], axis=0)
    win_sums = []
    for gi, w in enumerate(POOL_WINDOWS):
        sw = u_ext[:, gi * POOL_GROUP_DIM:(gi + 1) * POOL_GROUP_DIM]
        span = 1
        while span < w:
            sw = sw + pltpu.roll(sw, span, axis=0)
            span *= 2
        win_sums.append(sw)
    y_cats = []
    for sub, rows in enumerate(subs):
        r0 = sub * TM
        pos = (pos0 + r0 + lax.broadcasted_iota(jnp.int32, (TM, 1), 0) + 1).astype(jnp.float32)
        mixed = []
        for gi, w in enumerate(POOL_WINDOWS):
            lo = gi * POOL_GROUP_DIM
            tok = u_ext[HALO + r0:HALO + r0 + TM, lo:lo + POOL_GROUP_DIM]
            pooled = win_sums[gi][HALO + r0:HALO + r0 + TM] / jnp.minimum(pos, float(w)) - tok
            mg = jnp.dot(pooled.astype(jnp.bfloat16), pw_ref[gi],
                         preferred_element_type=jnp.float32)
            mixed.append((mg * ps_ref[:, lo:lo + POOL_GROUP_DIM]).astype(jnp.bfloat16))
        y_cats.append(jnp.concatenate([ym_ref[rows, :]] + mixed, axis=1))

    all_logits = []
    for sub, rows in enumerate(subs):
        r0 = sub * TM
        x1 = x_ref[rows, :] + jnp.dot(y_cats[sub], wob_ref[...], preferred_element_type=jnp.float32)
        x1_ref[rows, :] = x1
        h2 = x1 * lax.rsqrt(jnp.mean(x1 * x1, axis=-1, keepdims=True) + EPS) * g2_ref[...]
        h2b = h2.astype(jnp.bfloat16)
        h2w = _pack_bf16_pairs(h2)
        for s in range(PSLAB):
            h2_ref[pl.ds(r0 * PSLAB + s, TM, stride=PSLAB), :] = h2w[:, s * LANES:(s + 1) * LANES]
        all_logits.append(lax.dot_general(wrt_ref[...], h2b, NT_DIMS,
                                          preferred_element_type=jnp.float32) + br_ref[...])

    for sub, rows in enumerate(subs):
        work = all_logits[sub]
        vals, ids, hots = [], [], []
        for _ in range(TOP_K):
            mk = jnp.max(work, axis=0, keepdims=True)
            ik = jnp.min(jnp.where(work == mk, e_id, float(N_EXPERTS)), axis=0, keepdims=True)
            hot = e_id == ik
            work = jnp.where(hot, -jnp.inf, work)
            vals.append(mk)
            ids.append(ik)
            hots.append(hot)
        ex = [jnp.exp(vk - vals[0]) for vk in vals]
        denom = ex[0] + ex[1] + ex[2] + ex[3]
        gate_ref[:, rows] = jnp.concatenate([e / denom for e in ex], axis=0)
        idx_ref[:, rows] = jnp.concatenate(ids, axis=0).astype(jnp.int32)

        sel_f = sum(jnp.where(hot, 1.0, 0.0) for hot in hots)
        prefix = jnp.dot(sel_f.astype(jnp.bfloat16), before, preferred_element_type=jnp.float32)
        rank_e = carry[:, 0:1] + prefix
        ranks = [jnp.sum(jnp.where(hot, rank_e, 0.0), axis=0, keepdims=True) for hot in hots]
        rank_ref[:, rows] = jnp.concatenate(ranks, axis=0).astype(jnp.int32)
        carry = carry + jnp.sum(sel_f, axis=1, keepdims=True)
    carry_ref[...] = carry
    cnt_ref[...] = carry.astype(jnp.int32)


def _out_route(x2, ym, p, pool_w, pool_s, w_out, g2, wr_t, br, seq):
    T = x2.shape[0]
    TM = ROUTE_SUB * TM_PROJ
    nt = T // TM
    u_blk = N_MAIN // POOL_WIDTH - 1
    halo_per_tile = TM // HALO
    tok_spec = pl.BlockSpec((TOP_K, TM), lambda i: (0, i))
    return pl.pallas_call(
        functools.partial(_out_route_kernel, seq),
        grid=(nt,),
        in_specs=[
            pl.BlockSpec((TM, D_MODEL), lambda i: (i, 0)),
            pl.BlockSpec((TM, MLSTM_WIDTH), lambda i: (i, 0)),
            pl.BlockSpec((TM, POOL_WIDTH), lambda i: (i, u_blk)),
            pl.BlockSpec((HALO, POOL_WIDTH),
                         lambda i: (jnp.maximum(i * halo_per_tile - 1, 0), u_blk)),
            pl.BlockSpec((len(POOL_WINDOWS), POOL_GROUP_DIM, POOL_GROUP_DIM), lambda i: (0, 0, 0)),
            pl.BlockSpec((1, POOL_WIDTH), lambda i: (0, 0)),
            pl.BlockSpec((D_MODEL, D_MODEL), lambda i: (0, 0)),
            pl.BlockSpec((1, D_MODEL), lambda i: (0, 0)),
            pl.BlockSpec((N_EXPERTS, D_MODEL), lambda i: (0, 0)),
            pl.BlockSpec((N_EXPERTS, 1), lambda i: (0, 0)),
        ],
        out_specs=[
            pl.BlockSpec((TM, D_MODEL), lambda i: (i, 0)),
            pl.BlockSpec((TM * PSLAB, LANES), lambda i: (i, 0)),
            tok_spec, tok_spec, tok_spec,
            pl.BlockSpec((N_EXPERTS, LANES), lambda i: (0, 0)),
        ],
        out_shape=[
            jax.ShapeDtypeStruct((T, D_MODEL), jnp.float32),
            jax.ShapeDtypeStruct((T * PSLAB, LANES), jnp.uint32),
            jax.ShapeDtypeStruct((TOP_K, T), jnp.int32),
            jax.ShapeDtypeStruct((TOP_K, T), jnp.float32),
            jax.ShapeDtypeStruct((TOP_K, T), jnp.int32),
            jax.ShapeDtypeStruct((N_EXPERTS, LANES), jnp.int32),
        ],
        scratch_shapes=[
            pltpu.VMEM((N_EXPERTS, LANES), jnp.float32),
            pltpu.VMEM((D_MODEL, D_MODEL), jnp.bfloat16),
        ],
        compiler_params=pltpu.CompilerParams(
            dimension_semantics=("arbitrary",), vmem_limit_bytes=VMEM_LIMIT),
        name="out_route",
    )(x2, ym, p, p, pool_w, pool_s, w_out, g2, wr_t, br)


def _plan(dest_flat, fill):
    n_assign = dest_flat.shape[0]
    n_table = fill.shape[0]
    mesh = plsc.VectorSubcoreMesh(core_axis_name="c", subcore_axis_name="s")

    @pl.kernel(out_type=jax.ShapeDtypeStruct((n_table,), jnp.int32), mesh=mesh,
               scratch_types=[pltpu.VMEM((n_table,), jnp.int32),
                              pltpu.VMEM((PLAN_CHUNK,), jnp.int32)],
               compiler_params=pltpu.CompilerParams(needs_layout_passes=False))
    def plan_kernel(dest_hbm, fill_hbm, out_hbm, table, chunk):
        first = jnp.logical_and(lax.axis_index("c") == 0, lax.axis_index("s") == 0)

        @pl.when(first)
        def _():
            pltpu.sync_copy(fill_hbm, table)

            @pl.loop(0, n_assign // PLAN_CHUNK)
            def _(ci):
                pltpu.sync_copy(dest_hbm.at[pl.ds(ci * PLAN_CHUNK, PLAN_CHUNK)], chunk)

                @pl.loop(0, PLAN_CHUNK // (SC_LANES * PLAN_UNROLL))
                def _(i):
                    for j in range(PLAN_UNROLL):
                        off = (i * PLAN_UNROLL + j) * SC_LANES
                        idx = chunk[pl.ds(off, SC_LANES)]
                        vals = (ci * PLAN_CHUNK + off
                                + lax.broadcasted_iota(jnp.int32, (SC_LANES,), 0))
                        plsc.store_scatter(table, [idx], vals)

            pltpu.sync_copy(table, out_hbm)

    return plan_kernel(dest_flat, fill)


def _expert_kernel(n_tok, bs_ref, slot_ref, h2_ref, wg_ref, bg_ref, wu_ref, bu_ref, wd_ref, bd_ref,
                   yt_ref, *scratch):
    TM = TM_EXPERT
    ROWS = TM * PSLAB
    e = pl.program_id(0)
    n_total = bs_ref[N_EXPERTS]
    xg = scratch[:NBUF]
    ys = scratch[NBUF:2 * NBUF]
    wgb_ref, wub_ref, wdb_ref, gsem, ssem = scratch[2 * NBUF:]

    def token_of(a):
        return a & (n_tok - 1) if n_tok & (n_tok - 1) == 0 else lax.rem(a, n_tok)

    def start_gather(blk, par):
        base = (blk + 1) * TM
        for r in range(TM):
            t = token_of(slot_ref[base + r])
            pltpu.make_async_copy(h2_ref.at[pl.ds(pl.multiple_of(t * PSLAB, PSLAB), PSLAB), :],
                                  xg[par].at[pl.ds(r * PSLAB, PSLAB), :], gsem.at[par]
                                  ).start(priority=ROW_DMA_PRIORITY)

    def wait_gather(par):
        pltpu.make_async_copy(h2_ref.at[pl.ds(0, ROWS), :], xg[0], gsem.at[par]).wait()

    def start_scatter(blk, par):
        base = (blk + 1) * TM
        for r in range(TM):
            a = slot_ref[base + r]
            pltpu.make_async_copy(ys[par].at[pl.ds(r * PSLAB, PSLAB), :],
                                  yt_ref.at[pl.ds(pl.multiple_of(a * PSLAB, PSLAB), PSLAB), :],
                                  ssem.at[par]).start(priority=ROW_DMA_PRIORITY)

    def wait_scatter(par):
        pltpu.make_async_copy(ys[0], yt_ref.at[pl.ds(0, ROWS), :], ssem.at[par]).wait()

    @pl.when(e == 0)
    def _():
        for blk in range(NBUF - 1):
            start_gather(blk, blk)
        for par in range(NBUF):
            ys[par][...] = jnp.zeros_like(ys[par])
            dump = yt_ref.at[pl.ds((n_tok * TOP_K + par * TM) * PSLAB, ROWS), :]
            cp = pltpu.make_async_copy(ys[par], dump, ssem.at[par])
            cp.start()
            cp.wait()

    wgb_ref[...] = wg_ref[0].astype(jnp.bfloat16)
    wub_ref[...] = wu_ref[0].astype(jnp.bfloat16)
    wdb_ref[...] = wd_ref[0].astype(jnp.bfloat16)

    def block_step(g, par):
        prv = (par + NBUF - 1) % NBUF
        wait_gather(par)

        @pl.when(g >= NBUF - 1)
        def _():
            wait_scatter(par)

        start_gather(g + NBUF - 1, prv)
        start_scatter(g - 1, prv)
        words = [xg[par][pl.ds(s, TM, stride=PSLAB), :] for s in range(PSLAB)]
        x = jnp.concatenate([_unpack_lo(w).astype(jnp.bfloat16) for w in words]
                            + [_unpack_hi(w).astype(jnp.bfloat16) for w in words], axis=1)
        gate = jnp.dot(x, wgb_ref[...], preferred_element_type=jnp.float32) + bg_ref[0]
        up = jnp.dot(x, wub_ref[...], preferred_element_type=jnp.float32) + bu_ref[0]
        gate = jnp.minimum(gate, SWIGLU_LIMIT)
        up = jnp.clip(up, -SWIGLU_LIMIT, SWIGLU_LIMIT)
        glu = gate * _sigmoid(SWIGLU_ALPHA * gate)
        act = (glu * (up + 1.0)).astype(jnp.bfloat16)
        y = jnp.dot(act, wdb_ref[...], preferred_element_type=jnp.float32) + bd_ref[0]
        packed = _pack_bf16_pairs(y)
        for s in range(PSLAB):
            ys[par][pl.ds(s, TM, stride=PSLAB), :] = packed[:, s * LANES:(s + 1) * LANES]

    def body(g, carry):
        for par in range(NBUF):
            pl.when(g % NBUF == par)(functools.partial(block_step, g, par))
        return carry

    lax.fori_loop(bs_ref[e], bs_ref[e + 1], body, 0)

    @pl.when(e == N_EXPERTS - 1)
    def _():
        g = n_total
        for par in range(NBUF):
            @pl.when((g - 1) % NBUF == par)
            def _():
                start_scatter(g - 1, par)
        for j in range(NBUF - 1):
            wait_gather((g + j) % NBUF)
        wait_scatter((g - 1) % NBUF)
        for j in range(2, NBUF + 1):
            @pl.when(g >= j - 1)
            def _():
                wait_scatter((g + NBUF - j) % NBUF)


def _experts(block_start, slot_buf, h2_slab, w_gate, b_gate, w_up, b_up, w_down, b_down, n_tok):
    TM = TM_EXPERT
    n_assign = n_tok * TOP_K
    w_spec = pl.BlockSpec((1, D_MODEL, D_FF), lambda e, bs, sl: (e, 0, 0))
    bias_spec = pl.BlockSpec((1, 1, D_FF), lambda e, bs, sl: (e, 0, 0))
    buf = pltpu.VMEM((TM * PSLAB, LANES), jnp.uint32)
    grid_spec = pltpu.PrefetchScalarGridSpec(
        num_scalar_prefetch=2,
        grid=(N_EXPERTS,),
        in_specs=[
            pl.BlockSpec(memory_space=pl.ANY),
            w_spec, bias_spec, w_spec, bias_spec, w_spec, bias_spec,
        ],
        out_specs=pl.BlockSpec(memory_space=pl.ANY),
        scratch_shapes=[
            *([buf] * (2 * NBUF)),
            pltpu.VMEM((D_MODEL, D_FF), jnp.bfloat16),
            pltpu.VMEM((D_MODEL, D_FF), jnp.bfloat16),
            pltpu.VMEM((D_FF, D_MODEL), jnp.bfloat16),
            pltpu.SemaphoreType.DMA((NBUF,)),
            pltpu.SemaphoreType.DMA((NBUF,)),
        ],
    )
    return pl.pallas_call(
        functools.partial(_expert_kernel, n_tok),
        grid_spec=grid_spec,
        out_shape=jax.ShapeDtypeStruct(((n_assign + NBUF * TM) * PSLAB, LANES), jnp.uint32),
        compiler_params=pltpu.CompilerParams(
            dimension_semantics=("arbitrary",), vmem_limit_bytes=VMEM_LIMIT),
        name="experts",
    )(block_start, slot_buf, h2_slab, w_gate, b_gate, w_up, b_up, w_down, b_down)


def _combine_kernel(normalize, x1_ref, y0_ref, y1_ref, y2_ref, y3_ref, gate_ref, g_ref, o_ref):
    TM = TM_PROJ
    gates = jnp.concatenate([gate_ref[...], jnp.zeros((8 - TOP_K, TM), jnp.float32)], axis=0)
    g_cols = jnp.transpose(gates)
    g_bc = [jnp.broadcast_to(g_cols[:, k:k + 1], (TM, LANES)) for k in range(TOP_K)]
    ssq = jnp.zeros((TM, LANES), jnp.float32)
    parts = [x1_ref[:, s * LANES:(s + 1) * LANES] for s in range(SLAB)]
    for s in range(PSLAB):
        for k, y_ref in enumerate((y0_ref, y1_ref, y2_ref, y3_ref)):
            w = y_ref[pl.ds(s, TM, stride=PSLAB), :]
            parts[s] = parts[s] + g_bc[k] * _unpack_lo(w)
            parts[PSLAB + s] = parts[PSLAB + s] + g_bc[k] * _unpack_hi(w)
    for acc in parts:
        ssq = ssq + acc * acc
    if normalize:
        inv = lax.rsqrt(jnp.sum(ssq, axis=-1, keepdims=True) * (1.0 / D_MODEL) + EPS)
        for s in range(SLAB):
            o_ref[:, s * LANES:(s + 1) * LANES] = parts[s] * inv * g_ref[:, s * LANES:(s + 1) * LANES]
    else:
        for s in range(SLAB):
            o_ref[:, s * LANES:(s + 1) * LANES] = parts[s]


def _combine(x1, y_tok, gate_t, gf, normalize):
    T = x1.shape[0]
    TM = TM_PROJ
    nt = T // TM

    def y_spec(k):
        return pl.BlockSpec((TM * PSLAB, LANES), lambda i: (k * nt + i, 0))

    return pl.pallas_call(
        functools.partial(_combine_kernel, normalize),
        grid=(nt,),
        in_specs=[
            pl.BlockSpec((TM, D_MODEL), lambda i: (i, 0)),
            y_spec(0), y_spec(1), y_spec(2), y_spec(3),
            pl.BlockSpec((TOP_K, TM), lambda i: (0, i)),
            pl.BlockSpec((1, D_MODEL), lambda i: (0, 0)),
        ],
        out_specs=pl.BlockSpec((TM, D_MODEL), lambda i: (i, 0)),
        out_shape=jax.ShapeDtypeStruct((T, D_MODEL), jnp.float32),
        compiler_params=pltpu.CompilerParams(
            dimension_semantics=("parallel",), vmem_limit_bytes=VMEM_LIMIT),
        name="combine",
    )(x1, y_tok, y_tok, y_tok, y_tok, gate_t, gf)


def kernel(x, norm1_g, w_in, ig_b, fg_b, conv_w, head_norm_g, pool_w, pool_scale, w_out, norm2_g,
           w_router, b_router, w_gate, b_gate, w_up, b_up, w_down, b_down, normf_g):
    B, S, D = x.shape
    T = B * S
    depth = norm1_g.shape[0]
    W = MLSTM_WIDTH
    f32, bf16 = jnp.float32, jnp.bfloat16

    L = CHUNK
    t_l = lax.broadcasted_iota(jnp.int32, (L, L), 0)
    t_r = lax.broadcasted_iota(jnp.int32, (L, L), 1)
    tri = (t_r <= t_l).astype(f32)
    shifts = jnp.stack([(t_l - t_r == CONV_WIDTH - 1 - j).astype(bf16)
                        for j in range(CONV_WIDTH - 1)])
    h_t = lax.broadcasted_iota(jnp.int32, (8, HALO), 0)
    h_r = lax.broadcasted_iota(jnp.int32, (8, HALO), 1)
    halo_shifts = jnp.stack([(h_r - HALO - h_t == -(CONV_WIDTH - 1 - j)).astype(bf16)
                             for j in range(CONV_WIDTH - 1)])

    n_assign = T * TOP_K
    n_blocks = -(-n_assign // TM_EXPERT) + N_EXPERTS
    n_rows = n_blocks * TM_EXPERT
    n_table = n_rows + NBUF * TM_EXPERT
    fill = n_assign + ((jnp.arange(n_table, dtype=jnp.int32) + (NBUF - 1) * TM_EXPERT)
                       % (NBUF * TM_EXPERT))
    x2 = x.reshape(T, D)
    for l in range(depth):
        w = w_in[l]
        wg_t = jnp.zeros((BF16_SUBLANES, D), bf16).at[:N_GATES].set(
            w[:, 4 * W:4 * W + N_GATES].T.astype(bf16))
        p, gates_t = _in_proj(x2, norm1_g[l][None, :], w, wg_t)

        gate_b = jnp.concatenate([ig_b[l], fg_b[l]])[:, None].astype(f32)
        gates_b = gates_t.reshape(N_GATES, B, S).transpose(1, 0, 2)
        ym = _mlstm(p.reshape(B, S, N_MAIN), gates_b, conv_w[l].astype(f32), gate_b,
                    head_norm_g[l][None, :], tri, shifts, halo_shifts).reshape(T, W)

        x1, h2, idx_t, gate_t, rank_t, cnt = _out_route(
            x2, ym, p, pool_w[l].astype(bf16), pool_scale[l][None, :], w_out[l],
            norm2_g[l][None, :], w_router[l].T.astype(bf16), b_router[l][:, None], S)

        counts = cnt[:, 0]
        padded = ((counts + TM_EXPERT - 1) // TM_EXPERT) * TM_EXPERT
        padded_end = jnp.cumsum(padded)
        padded_start = padded_end - padded
        expert_ids = jnp.arange(N_EXPERTS, dtype=jnp.int32)[:, None, None]
        start_of = jnp.sum(jnp.where(idx_t[None] == expert_ids, padded_start[:, None, None], 0), axis=0)
        dest = start_of + rank_t
        block_start = jnp.concatenate(
            [jnp.zeros((1,), jnp.int32), (padded_end // TM_EXPERT).astype(jnp.int32)])

        slot_buf = _plan(dest.reshape(-1) + TM_EXPERT, fill)
        y_tok = _experts(block_start, slot_buf, h2, w_gate[l], b_gate[l][:, None, :],
                         w_up[l], b_up[l][:, None, :], w_down[l], b_down[l][:, None, :], T)
        last = l + 1 == depth
        x2 = _combine(x1, y_tok, gate_t, normf_g[None, :], last)
    return x2.reshape(B, S, D)
```

```python
import functools

import jax
import jax.numpy as jnp
from jax import lax
from jax.experimental import pallas as pl
from jax.experimental.pallas import tpu as pltpu
from jax.experimental.pallas import tpu_sc as plsc

D_MODEL = 1024
MLSTM_WIDTH = 512
MLSTM_HEADS = 4
HEAD_DIM = 128
CONV_WIDTH = 4
POOL_WIDTH = 512
POOL_WINDOWS = (2, 4, 8, 16)
POOL_GROUP_DIM = 128
N_EXPERTS = 32
TOP_K = 4
D_FF = 1024
SWIGLU_LIMIT = 7.0
SWIGLU_ALPHA = 1.702
EPS = 1e-5

N_MAIN = 4 * MLSTM_WIDTH + POOL_WIDTH
N_GATES = 2 * MLSTM_HEADS

LANES = 128
BF16_SUBLANES = 16
VMEM_LIMIT = 56 * 1024 * 1024

TM_PROJ = 512
ROUTE_SUB = 2
CHUNK = 256
MLSTM_BATCH = 2
HALO = 16
TM_EXPERT = 256
NBUF = 4
ROW_DMA_PRIORITY = 1
SLAB = D_MODEL // LANES
PSLAB = SLAB // 2
PLAN_CHUNK = 8192
SC_LANES = 16
PLAN_UNROLL = 8

NT_DIMS = (((1,), (1,)), ((), ()))


def _sigmoid(x):
    return 1.0 / (1.0 + jnp.exp(-x))


def _pack_bf16_pairs(v):
    half = v.shape[1] // 2
    lo = pltpu.bitcast(v[:, :half].astype(jnp.bfloat16).astype(jnp.float32), jnp.uint32)
    hi = pltpu.bitcast(v[:, half:].astype(jnp.bfloat16).astype(jnp.float32), jnp.uint32)
    return (lo >> 16) | (hi & jnp.uint32(0xFFFF0000))


def _unpack_lo(w):
    return pltpu.bitcast(w << 16, jnp.float32)


def _unpack_hi(w):
    return pltpu.bitcast(w & jnp.uint32(0xFFFF0000), jnp.float32)


def _in_proj_kernel(x_ref, g_ref, wa_ref, wu_ref, wgt_ref, p_ref, gt_ref):
    x = x_ref[...]
    h = x * lax.rsqrt(jnp.mean(x * x, axis=-1, keepdims=True) + EPS) * g_ref[...]
    hb = h.astype(jnp.bfloat16)
    n_a = wa_ref.shape[1]
    p_ref[:, :n_a] = jnp.dot(hb, wa_ref[...], preferred_element_type=jnp.float32).astype(p_ref.dtype)
    p_ref[:, n_a:] = jnp.dot(hb, wu_ref[...], preferred_element_type=jnp.float32).astype(p_ref.dtype)
    gt = lax.dot_general(wgt_ref[...], hb, NT_DIMS, preferred_element_type=jnp.float32)
    gt_ref[...] = gt[:N_GATES]


def _in_proj(x2, g1, w_a, w_u, wg_t):
    T = x2.shape[0]
    return pl.pallas_call(
        _in_proj_kernel,
        grid=(T // TM_PROJ,),
        in_specs=[
            pl.BlockSpec((TM_PROJ, D_MODEL), lambda i: (i, 0)),
            pl.BlockSpec((1, D_MODEL), lambda i: (0, 0)),
            pl.BlockSpec(w_a.shape, lambda i: (0, 0)),
            pl.BlockSpec(w_u.shape, lambda i: (0, 0)),
            pl.BlockSpec((BF16_SUBLANES, D_MODEL), lambda i: (0, 0)),
        ],
        out_specs=[
            pl.BlockSpec((TM_PROJ, N_MAIN), lambda i: (i, 0)),
            pl.BlockSpec((N_GATES, TM_PROJ), lambda i: (0, i)),
        ],
        out_shape=[
            jax.ShapeDtypeStruct((T, N_MAIN), jnp.bfloat16),
            jax.ShapeDtypeStruct((N_GATES, T), jnp.float32),
        ],
        compiler_params=pltpu.CompilerParams(
            dimension_semantics=("parallel",), vmem_limit_bytes=VMEM_LIMIT),
        name="in_proj",
    )(x2, g1, w_a, w_u, wg_t)


def _mlstm_kernel(qk_ref, qkp_ref, v_ref, o_ref, gt_ref, convw_ref, gb_ref, hng_ref,
                  tri_ref, shift_ref, hshift_ref, y_ref, cn_ref, m_ref):
    L = CHUNK
    c = pl.program_id(1)

    @pl.when(c == 0)
    def _():
        cn_ref[...] = jnp.zeros_like(cn_ref)
        m_ref[...] = jnp.zeros_like(m_ref)

    row_id = lax.broadcasted_iota(jnp.int32, (L, L), 0)
    col_id = lax.broadcasted_iota(jnp.int32, (L, L), 1)
    causal = col_id <= row_id
    ones_blk = jnp.ones((L, HEAD_DIM), jnp.bfloat16)
    lane = lax.broadcasted_iota(jnp.int32, (MLSTM_HEADS, L), 1)

    gate_terms = []
    for bb in range(MLSTM_BATCH):
        gt = gt_ref[bb] + gb_ref[...]
        f = gt[MLSTM_HEADS:]
        lf = jnp.minimum(f, 0.0) - jnp.log(1.0 + jnp.exp(-jnp.abs(f)))
        ig = gt[:MLSTM_HEADS]
        b_rows = lax.dot_general(lf, tri_ref[...], NT_DIMS, precision=lax.Precision.HIGHEST,
                                 preferred_element_type=jnp.float32)
        c_rows = ig - b_rows
        cm_rows = c_rows
        d = 1
        while d < L:
            cm_rows = jnp.maximum(
                cm_rows, jnp.where(lane >= d, pltpu.roll(cm_rows, d, axis=1), -jnp.inf))
            d *= 2
        gate_terms.append((b_rows, c_rows, cm_rows))

    conv_terms = []
    for bb in range(MLSTM_BATCH):
        x_cur = qk_ref[bb]
        x_prev = jnp.where(c > 0, qkp_ref[bb], jnp.zeros((HALO, 2 * MLSTM_WIDTH), jnp.bfloat16))
        acc = convw_ref[CONV_WIDTH - 1:CONV_WIDTH, :] * x_cur.astype(jnp.float32)
        for j in range(CONV_WIDTH - 1):
            sh = jnp.dot(shift_ref[j], x_cur, preferred_element_type=jnp.float32)
            top = sh[:8] + jnp.dot(hshift_ref[j], x_prev, preferred_element_type=jnp.float32)
            sh = jnp.concatenate([top, sh[8:]], axis=0)
            acc = acc + convw_ref[j:j + 1, :] * sh
        qk = acc * _sigmoid(acc)
        q_all = qk[:, :MLSTM_WIDTH].astype(jnp.bfloat16)
        k_t = jnp.transpose(qk[:, MLSTM_WIDTH:] * (HEAD_DIM ** -0.5))
        conv_terms.append((q_all, k_t))

    for bb in range(MLSTM_BATCH):
        b_rows, c_rows, cm_rows = gate_terms[bb]
        q_all, k_t = conv_terms[bb]
        m_in4 = jnp.concatenate(
            [m_ref[bb * MLSTM_HEADS + h][0:1, 0:1] for h in range(MLSTM_HEADS)], axis=0)
        mx_rows = jnp.maximum(cm_rows, m_in4)
        inter_rows = jnp.exp(m_in4 - mx_rows)
        einv_rows = jnp.exp(-(b_rows + mx_rows))
        fac_t = jnp.transpose(jnp.concatenate(
            [mx_rows, inter_rows, einv_rows, jnp.zeros_like(mx_rows)], axis=0))

        for h in range(MLSTM_HEADS):
            lo = h * HEAD_DIM
            st = bb * MLSTM_HEADS + h
            q = q_all[:, lo:lo + HEAD_DIM]
            kt = k_t[lo:lo + HEAD_DIM, :]
            v_ext = jnp.concatenate([v_ref[bb, :, lo:lo + HEAD_DIM], ones_blk], axis=1)
            mx_col = fac_t[:, h:h + 1]
            inter_col = fac_t[:, MLSTM_HEADS + h:MLSTM_HEADS + h + 1]
            einv_col = fac_t[:, 2 * MLSTM_HEADS + h:2 * MLSTM_HEADS + h + 1]
            c_row = c_rows[h:h + 1, :]
            b_tot = b_rows[h:h + 1, L - 1:L]
            cm_tot = cm_rows[h:h + 1, L - 1:L]
            m_in = m_ref[st][0:1, 0:1]
            cn = cn_ref[st]

            s_qk = jnp.dot(q, kt.astype(jnp.bfloat16), preferred_element_type=jnp.float32)
            s = (s_qk * jnp.exp(jnp.where(causal, c_row - mx_col, -jnp.inf))).astype(jnp.bfloat16)
            num = (jnp.dot(s, v_ext, preferred_element_type=jnp.float32)
                   + inter_col * jnp.dot(q, cn.astype(jnp.bfloat16),
                                         preferred_element_type=jnp.float32))
            den = num[:, HEAD_DIM:]
            hh = num[:, :HEAD_DIM] / jnp.maximum(jnp.abs(den), einv_col)

            mu = jnp.mean(hh, axis=-1, keepdims=True)
            dv = hh - mu
            var = jnp.mean(dv * dv, axis=-1, keepdims=True)
            hn = dv * lax.rsqrt(var + EPS) * hng_ref[:, lo:lo + HEAD_DIM]
            og = _sigmoid(o_ref[bb, :, lo:lo + HEAD_DIM].astype(jnp.float32))
            y_ref[bb, :, lo:lo + HEAD_DIM] = (og * hn).astype(y_ref.dtype)

            m_loc = b_tot + cm_tot
            kw_t = (kt * jnp.exp(c_row - cm_tot)).astype(jnp.bfloat16)
            c_loc = jnp.dot(kw_t, v_ext, preferred_element_type=jnp.float32)
            m_new = jnp.maximum(b_tot + m_in, m_loc)
            s_old = jnp.exp(b_tot + m_in - m_new)
            s_loc = jnp.exp(m_loc - m_new)
            cn_ref[st] = s_old * cn + s_loc * c_loc
            m_ref[st] = jnp.broadcast_to(m_new, m_ref.shape[1:])


def _mlstm(p3, gates_b, conv_w, gate_b, hn_g, tri, shifts, halo_shifts):
    batch, seq, _ = p3.shape
    L = CHUNK
    BB = MLSTM_BATCH
    halo_per_chunk = L // HALO
    return pl.pallas_call(
        _mlstm_kernel,
        grid=(batch // BB, seq // L),
        in_specs=[
            pl.BlockSpec((BB, L, 2 * MLSTM_WIDTH), lambda bi, ci: (bi, ci, 0)),
            pl.BlockSpec((BB, HALO, 2 * MLSTM_WIDTH),
                         lambda bi, ci: (bi, jnp.maximum(ci * halo_per_chunk - 1, 0), 0)),
            pl.BlockSpec((BB, L, MLSTM_WIDTH), lambda bi, ci: (bi, ci, 2)),
            pl.BlockSpec((BB, L, MLSTM_WIDTH), lambda bi, ci: (bi, ci, 3)),
            pl.BlockSpec((BB, N_GATES, L), lambda bi, ci: (bi, 0, ci)),
            pl.BlockSpec((CONV_WIDTH, 2 * MLSTM_WIDTH), lambda bi, ci: (0, 0)),
            pl.BlockSpec((N_GATES, 1), lambda bi, ci: (0, 0)),
            pl.BlockSpec((1, MLSTM_WIDTH), lambda bi, ci: (0, 0)),
            pl.BlockSpec((L, L), lambda bi, ci: (0, 0)),
            pl.BlockSpec((CONV_WIDTH - 1, L, L), lambda bi, ci: (0, 0, 0)),
            pl.BlockSpec((CONV_WIDTH - 1, 8, HALO), lambda bi, ci: (0, 0, 0)),
        ],
        out_specs=pl.BlockSpec((BB, L, MLSTM_WIDTH), lambda bi, ci: (bi, ci, 0)),
        out_shape=jax.ShapeDtypeStruct((batch, seq, MLSTM_WIDTH), jnp.bfloat16),
        scratch_shapes=[
            pltpu.VMEM((BB * MLSTM_HEADS, HEAD_DIM, 2 * HEAD_DIM), jnp.float32),
            pltpu.VMEM((BB * MLSTM_HEADS, 8, LANES), jnp.float32),
        ],
        compiler_params=pltpu.CompilerParams(
            dimension_semantics=("parallel", "arbitrary"), vmem_limit_bytes=VMEM_LIMIT),
        name="mlstm",
    )(p3, p3, p3, p3, gates_b, conv_w, gate_b, hn_g, tri, shifts, halo_shifts)


def _out_route_kernel(seq, x_ref, ym_ref, u_ref, up_ref, pw_ref, ps_ref, wo_ref, g2_ref,
                      wrt_ref, br_ref, x1_ref, h2_ref, idx_ref, gate_ref, rank_ref, cnt_ref,
                      carry_ref, wob_ref):
    TM = TM_PROJ
    R = ROUTE_SUB * TM
    i = pl.program_id(0)

    @pl.when(i == 0)
    def _():
        carry_ref[...] = jnp.zeros_like(carry_ref)
        wob_ref[...] = wo_ref[...].astype(jnp.bfloat16)

    pos0 = (i * R) % seq
    e_id = lax.broadcasted_iota(jnp.int32, (N_EXPERTS, TM), 0).astype(jnp.float32)
    t_row = lax.broadcasted_iota(jnp.int32, (TM, TM), 0)
    t_col = lax.broadcasted_iota(jnp.int32, (TM, TM), 1)
    before = jnp.where(t_row < t_col, 1.0, 0.0).astype(jnp.bfloat16)
    carry = carry_ref[...]
    subs = [slice(sub * TM, (sub + 1) * TM) for sub in range(ROUTE_SUB)]

    halo = jnp.where(pos0 > 0, up_ref[...].astype(jnp.float32), 0.0)
    u_ext = jnp.concatenate([halo, u_ref[...].astype(jnp.float32)], axis=0)
    win_sums = []
    for gi, w in enumerate(POOL_WINDOWS):
        sw = u_ext[:, gi * POOL_GROUP_DIM:(gi + 1) * POOL_GROUP_DIM]
        span = 1
        while span < w:
            sw = sw + pltpu.roll(sw, span, axis=0)
            span *= 2
        win_sums.append(sw)
    y_cats = []
    for sub, rows in enumerate(subs):
        r0 = sub * TM
        pos = (pos0 + r0 + lax.broadcasted_iota(jnp.int32, (TM, 1), 0) + 1).astype(jnp.float32)
        mixed = []
        for gi, w in enumerate(POOL_WINDOWS):
            lo = gi * POOL_GROUP_DIM
            tok = u_ext[HALO + r0:HALO + r0 + TM, lo:lo + POOL_GROUP_DIM]
            pooled = win_sums[gi][HALO + r0:HALO + r0 + TM] / jnp.minimum(pos, float(w)) - tok
            mg = jnp.dot(pooled.astype(jnp.bfloat16), pw_ref[gi],
                         preferred_element_type=jnp.float32)
            mixed.append((mg * ps_ref[:, lo:lo + POOL_GROUP_DIM]).astype(jnp.bfloat16))
        y_cats.append(jnp.concatenate([ym_ref[rows, :]] + mixed, axis=1))

    all_logits = []
    for sub, rows in enumerate(subs):
        r0 = sub * TM
        x1 = x_ref[rows, :] + jnp.dot(y_cats[sub], wob_ref[...], preferred_element_type=jnp.float32)
        x1_ref[rows, :] = x1
        h2 = x1 * lax.rsqrt(jnp.mean(x1 * x1, axis=-1, keepdims=True) + EPS) * g2_ref[...]
        h2b = h2.astype(jnp.bfloat16)
        h2w = _pack_bf16_pairs(h2)
        for s in range(PSLAB):
            h2_ref[pl.ds(r0 * PSLAB + s, TM, stride=PSLAB), :] = h2w[:, s * LANES:(s + 1) * LANES]
        all_logits.append(lax.dot_general(wrt_ref[...], h2b, NT_DIMS,
                                          preferred_element_type=jnp.float32) + br_ref[...])

    for sub, rows in enumerate(subs):
        work = all_logits[sub]
        vals, ids, hots = [], [], []
        for _ in range(TOP_K):
            mk = jnp.max(work, axis=0, keepdims=True)
            ik = jnp.min(jnp.where(work == mk, e_id, float(N_EXPERTS)), axis=0, keepdims=True)
            hot = e_id == ik
            work = jnp.where(hot, -jnp.inf, work)
            vals.append(mk)
            ids.append(ik)
            hots.append(hot)
        ex = [jnp.exp(vk - vals[0]) for vk in vals]
        denom = ex[0] + ex[1] + ex[2] + ex[3]
        gate_ref[:, rows] = jnp.concatenate([e / denom for e in ex], axis=0)
        idx_ref[:, rows] = jnp.concatenate(ids, axis=0).astype(jnp.int32)

        sel_f = sum(jnp.where(hot, 1.0, 0.0) for hot in hots)
        prefix = jnp.dot(sel_f.astype(jnp.bfloat16), before, preferred_element_type=jnp.float32)
        rank_e = carry[:, 0:1] + prefix
        ranks = [jnp.sum(jnp.where(hot, rank_e, 0.0), axis=0, keepdims=True) for hot in hots]
        rank_ref[:, rows] = jnp.concatenate(ranks, axis=0).astype(jnp.int32)
        carry = carry + jnp.sum(sel_f, axis=1, keepdims=True)
    carry_ref[...] = carry
    cnt_ref[...] = carry.astype(jnp.int32)


def _out_route(x2, ym, p, pool_w, pool_s, w_out, g2, wr_t, br, seq):
    T = x2.shape[0]
    TM = ROUTE_SUB * TM_PROJ
    nt = T // TM
    u_blk = N_MAIN // POOL_WIDTH - 1
    halo_per_tile = TM // HALO
    tok_spec = pl.BlockSpec((TOP_K, TM), lambda i: (0, i))
    return pl.pallas_call(
        functools.partial(_out_route_kernel, seq),
        grid=(nt,),
        in_specs=[
            pl.BlockSpec((TM, D_MODEL), lambda i: (i, 0)),
            pl.BlockSpec((TM, MLSTM_WIDTH), lambda i: (i, 0)),
            pl.BlockSpec((TM, POOL_WIDTH), lambda i: (i, u_blk)),
            pl.BlockSpec((HALO, POOL_WIDTH),
                         lambda i: (jnp.maximum(i * halo_per_tile - 1, 0), u_blk)),
            pl.BlockSpec((len(POOL_WINDOWS), POOL_GROUP_DIM, POOL_GROUP_DIM), lambda i: (0, 0, 0)),
            pl.BlockSpec((1, POOL_WIDTH), lambda i: (0, 0)),
            pl.BlockSpec((D_MODEL, D_MODEL), lambda i: (0, 0)),
            pl.BlockSpec((1, D_MODEL), lambda i: (0, 0)),
            pl.BlockSpec((N_EXPERTS, D_MODEL), lambda i: (0, 0)),
            pl.BlockSpec((N_EXPERTS, 1), lambda i: (0, 0)),
        ],
        out_specs=[
            pl.BlockSpec((TM, D_MODEL), lambda i: (i, 0)),
            pl.BlockSpec((TM * PSLAB, LANES), lambda i: (i, 0)),
            tok_spec, tok_spec, tok_spec,
            pl.BlockSpec((N_EXPERTS, LANES), lambda i: (0, 0)),
        ],
        out_shape=[
            jax.ShapeDtypeStruct((T, D_MODEL), jnp.float32),
            jax.ShapeDtypeStruct((T * PSLAB, LANES), jnp.uint32),
            jax.ShapeDtypeStruct((TOP_K, T), jnp.int32),
            jax.ShapeDtypeStruct((TOP_K, T), jnp.float32),
            jax.ShapeDtypeStruct((TOP_K, T), jnp.int32),
            jax.ShapeDtypeStruct((N_EXPERTS, LANES), jnp.int32),
        ],
        scratch_shapes=[
            pltpu.VMEM((N_EXPERTS, LANES), jnp.float32),
            pltpu.VMEM((D_MODEL, D_MODEL), jnp.bfloat16),
        ],
        compiler_params=pltpu.CompilerParams(
            dimension_semantics=("arbitrary",), vmem_limit_bytes=VMEM_LIMIT),
        name="out_route",
    )(x2, ym, p, p, pool_w, pool_s, w_out, g2, wr_t, br)


def _plan(dest_flat, fill):
    n_assign = dest_flat.shape[0]
    n_table = fill.shape[0]
    mesh = plsc.VectorSubcoreMesh(core_axis_name="c", subcore_axis_name="s")

    @pl.kernel(out_type=jax.ShapeDtypeStruct((n_table,), jnp.int32), mesh=mesh,
               scratch_types=[pltpu.VMEM((n_table,), jnp.int32),
                              pltpu.VMEM((PLAN_CHUNK,), jnp.int32)],
               compiler_params=pltpu.CompilerParams(needs_layout_passes=False))
    def plan_kernel(dest_hbm, fill_hbm, out_hbm, table, chunk):
        first = jnp.logical_and(lax.axis_index("c") == 0, lax.axis_index("s") == 0)

        @pl.when(first)
        def _():
            pltpu.sync_copy(fill_hbm, table)

            @pl.loop(0, n_assign // PLAN_CHUNK)
            def _(ci):
                pltpu.sync_copy(dest_hbm.at[pl.ds(ci * PLAN_CHUNK, PLAN_CHUNK)], chunk)

                @pl.loop(0, PLAN_CHUNK // (SC_LANES * PLAN_UNROLL))
                def _(i):
                    for j in range(PLAN_UNROLL):
                        off = (i * PLAN_UNROLL + j) * SC_LANES
                        idx = chunk[pl.ds(off, SC_LANES)]
                        vals = (ci * PLAN_CHUNK + off
                                + lax.broadcasted_iota(jnp.int32, (SC_LANES,), 0))
                        plsc.store_scatter(table, [idx], vals)

            pltpu.sync_copy(table, out_hbm)

    return plan_kernel(dest_flat, fill)


def _expert_kernel(n_tok, bs_ref, slot_ref, h2_ref, wg_ref, bg_ref, wu_ref, bu_ref, wd_ref, bd_ref,
                   yt_ref, *scratch):
    TM = TM_EXPERT
    ROWS = TM * PSLAB
    e = pl.program_id(0)
    n_total = bs_ref[N_EXPERTS]
    xg = scratch[:NBUF]
    ys = scratch[NBUF:2 * NBUF]
    wgb_ref, wub_ref, wdb_ref, gsem, ssem = scratch[2 * NBUF:]

    def token_of(a):
        return a & (n_tok - 1) if n_tok & (n_tok - 1) == 0 else lax.rem(a, n_tok)

    def start_gather(blk, par):
        base = (blk + 1) * TM
        for r in range(TM):
            t = token_of(slot_ref[base + r])
            pltpu.make_async_copy(h2_ref.at[pl.ds(pl.multiple_of(t * PSLAB, PSLAB), PSLAB), :],
                                  xg[par].at[pl.ds(r * PSLAB, PSLAB), :], gsem.at[par]
                                  ).start(priority=ROW_DMA_PRIORITY)

    def wait_gather(par):
        pltpu.make_async_copy(h2_ref.at[pl.ds(0, ROWS), :], xg[0], gsem.at[par]).wait()

    def start_scatter(blk, par):
        base = (blk + 1) * TM
        for r in range(TM):
            a = slot_ref[base + r]
            pltpu.make_async_copy(ys[par].at[pl.ds(r * PSLAB, PSLAB), :],
                                  yt_ref.at[pl.ds(pl.multiple_of(a * PSLAB, PSLAB), PSLAB), :],
                                  ssem.at[par]).start(priority=ROW_DMA_PRIORITY)

    def wait_scatter(par):
        pltpu.make_async_copy(ys[0], yt_ref.at[pl.ds(0, ROWS), :], ssem.at[par]).wait()

    @pl.when(e == 0)
    def _():
        for blk in range(NBUF - 1):
            start_gather(blk, blk)
        for par in range(NBUF):
            ys[par][...] = jnp.zeros_like(ys[par])
            dump = yt_ref.at[pl.ds((n_tok * TOP_K + par * TM) * PSLAB, ROWS), :]
            cp = pltpu.make_async_copy(ys[par], dump, ssem.at[par])
            cp.start()
            cp.wait()

    wgb_ref[...] = wg_ref[0].astype(jnp.bfloat16)
    wub_ref[...] = wu_ref[0].astype(jnp.bfloat16)
    wdb_ref[...] = wd_ref[0].astype(jnp.bfloat16)

    def block_step(g, par):
        prv = (par + NBUF - 1) % NBUF
        wait_gather(par)

        @pl.when(g >= NBUF - 1)
        def _():
            wait_scatter(par)

        start_gather(g + NBUF - 1, prv)
        start_scatter(g - 1, prv)
        words = [xg[par][pl.ds(s, TM, stride=PSLAB), :] for s in range(PSLAB)]
        x = jnp.concatenate([_unpack_lo(w).astype(jnp.bfloat16) for w in words]
                            + [_unpack_hi(w).astype(jnp.bfloat16) for w in words], axis=1)
        gate = jnp.dot(x, wgb_ref[...], preferred_element_type=jnp.float32) + bg_ref[0]
        up = jnp.dot(x, wub_ref[...], preferred_element_type=jnp.float32) + bu_ref[0]
        gate = jnp.minimum(gate, SWIGLU_LIMIT)
        up = jnp.clip(up, -SWIGLU_LIMIT, SWIGLU_LIMIT)
        glu = gate * _sigmoid(SWIGLU_ALPHA * gate)
        act = (glu * (up + 1.0)).astype(jnp.bfloat16)
        y = jnp.dot(act, wdb_ref[...], preferred_element_type=jnp.float32) + bd_ref[0]
        packed = _pack_bf16_pairs(y)
        for s in range(PSLAB):
            ys[par][pl.ds(s, TM, stride=PSLAB), :] = packed[:, s * LANES:(s + 1) * LANES]

    def body(g, carry):
        for par in range(NBUF):
            pl.when(g % NBUF == par)(functools.partial(block_step, g, par))
        return carry

    lax.fori_loop(bs_ref[e], bs_ref[e + 1], body, 0)

    @pl.when(e == N_EXPERTS - 1)
    def _():
        g = n_total
        for par in range(NBUF):
            @pl.when((g - 1) % NBUF == par)
            def _():
                start_scatter(g - 1, par)
        for j in range(NBUF - 1):
            wait_gather((g + j) % NBUF)
        wait_scatter((g - 1) % NBUF)
        for j in range(2, NBUF + 1):
            @pl.when(g >= j - 1)
            def _():
                wait_scatter((g + NBUF - j) % NBUF)


def _experts(block_start, slot_buf, h2_slab, w_gate, b_gate, w_up, b_up, w_down, b_down, n_tok):
    TM = TM_EXPERT
    n_assign = n_tok * TOP_K
    w_spec = pl.BlockSpec((1, D_MODEL, D_FF), lambda e, bs, sl: (e, 0, 0))
    bias_spec = pl.BlockSpec((1, 1, D_FF), lambda e, bs, sl: (e, 0, 0))
    buf = pltpu.VMEM((TM * PSLAB, LANES), jnp.uint32)
    grid_spec = pltpu.PrefetchScalarGridSpec(
        num_scalar_prefetch=2,
        grid=(N_EXPERTS,),
        in_specs=[
            pl.BlockSpec(memory_space=pl.ANY),
            w_spec, bias_spec, w_spec, bias_spec, w_spec, bias_spec,
        ],
        out_specs=pl.BlockSpec(memory_space=pl.ANY),
        scratch_shapes=[
            *([buf] * (2 * NBUF)),
            pltpu.VMEM((D_MODEL, D_FF), jnp.bfloat16),
            pltpu.VMEM((D_MODEL, D_FF), jnp.bfloat16),
            pltpu.VMEM((D_FF, D_MODEL), jnp.bfloat16),
            pltpu.SemaphoreType.DMA((NBUF,)),
            pltpu.SemaphoreType.DMA((NBUF,)),
        ],
    )
    return pl.pallas_call(
        functools.partial(_expert_kernel, n_tok),
        grid_spec=grid_spec,
        out_shape=jax.ShapeDtypeStruct(((n_assign + NBUF * TM) * PSLAB, LANES), jnp.uint32),
        compiler_params=pltpu.CompilerParams(
            dimension_semantics=("arbitrary",), vmem_limit_bytes=VMEM_LIMIT),
        name="experts",
    )(block_start, slot_buf, h2_slab, w_gate, b_gate, w_up, b_up, w_down, b_down)


def _combine_kernel(normalize, x1_ref, y0_ref, y1_ref, y2_ref, y3_ref, gate_ref, g_ref, o_ref):
    TM = TM_PROJ
    gates = jnp.concatenate([gate_ref[...], jnp.zeros((8 - TOP_K, TM), jnp.float32)], axis=0)
    g_cols = jnp.transpose(gates)
    g_bc = [jnp.broadcast_to(g_cols[:, k:k + 1], (TM, LANES)) for k in range(TOP_K)]
    ssq = jnp.zeros((TM, LANES), jnp.float32)
    parts = [x1_ref[:, s * LANES:(s + 1) * LANES] for s in range(SLAB)]
    for s in range(PSLAB):
        for k, y_ref in enumerate((y0_ref, y1_ref, y2_ref, y3_ref)):
            w = y_ref[pl.ds(s, TM, stride=PSLAB), :]
            parts[s] = parts[s] + g_bc[k] * _unpack_lo(w)
            parts[PSLAB + s] = parts[PSLAB + s] + g_bc[k] * _unpack_hi(w)
    for acc in parts:
        ssq = ssq + acc * acc
    if normalize:
        inv = lax.rsqrt(jnp.sum(ssq, axis=-1, keepdims=True) * (1.0 / D_MODEL) + EPS)
        for s in range(SLAB):
            o_ref[:, s * LANES:(s + 1) * LANES] = parts[s] * inv * g_ref[:, s * LANES:(s + 1) * LANES]
    else:
        for s in range(SLAB):
            o_ref[:, s * LANES:(s + 1) * LANES] = parts[s]


def _combine(x1, y_tok, gate_t, gf, normalize):
    T = x1.shape[0]
    TM = TM_PROJ
    nt = T // TM

    def y_spec(k):
        return pl.BlockSpec((TM * PSLAB, LANES), lambda i: (k * nt + i, 0))

    return pl.pallas_call(
        functools.partial(_combine_kernel, normalize),
        grid=(nt,),
        in_specs=[
            pl.BlockSpec((TM, D_MODEL), lambda i: (i, 0)),
            y_spec(0), y_spec(1), y_spec(2), y_spec(3),
            pl.BlockSpec((TOP_K, TM), lambda i: (0, i)),
            pl.BlockSpec((1, D_MODEL), lambda i: (0, 0)),
        ],
        out_specs=pl.BlockSpec((TM, D_MODEL), lambda i: (i, 0)),
        out_shape=jax.ShapeDtypeStruct((T, D_MODEL), jnp.float32),
        compiler_params=pltpu.CompilerParams(
            dimension_semantics=("parallel",), vmem_limit_bytes=VMEM_LIMIT),
        name="combine",
    )(x1, y_tok, y_tok, y_tok, y_tok, gate_t, gf)


def kernel(x, norm1_g, w_in, ig_b, fg_b, conv_w, head_norm_g, pool_w, pool_scale, w_out, norm2_g,
           w_router, b_router, w_gate, b_gate, w_up, b_up, w_down, b_down, normf_g):
    B, S, D = x.shape
    T = B * S
    depth = norm1_g.shape[0]
    W = MLSTM_WIDTH
    f32, bf16 = jnp.float32, jnp.bfloat16

    L = CHUNK
    t_l = lax.broadcasted_iota(jnp.int32, (L, L), 0)
    t_r = lax.broadcasted_iota(jnp.int32, (L, L), 1)
    tri = (t_r <= t_l).astype(f32)
    shifts = jnp.stack([(t_l - t_r == CONV_WIDTH - 1 - j).astype(bf16)
                        for j in range(CONV_WIDTH - 1)])
    h_t = lax.broadcasted_iota(jnp.int32, (8, HALO), 0)
    h_r = lax.broadcasted_iota(jnp.int32, (8, HALO), 1)
    halo_shifts = jnp.stack([(h_r - HALO - h_t == -(CONV_WIDTH - 1 - j)).astype(bf16)
                             for j in range(CONV_WIDTH - 1)])

    n_assign = T * TOP_K
    n_blocks = -(-n_assign // TM_EXPERT) + N_EXPERTS
    n_rows = n_blocks * TM_EXPERT
    n_table = n_rows + NBUF * TM_EXPERT
    fill = n_assign + ((jnp.arange(n_table, dtype=jnp.int32) + (NBUF - 1) * TM_EXPERT)
                       % (NBUF * TM_EXPERT))
    x2 = x.reshape(T, D)
    for l in range(depth):
        w = w_in[l]
        wg_t = jnp.zeros((BF16_SUBLANES, D), bf16).at[:N_GATES].set(
            w[:, 4 * W:4 * W + N_GATES].T.astype(bf16))
        w_a = w[:, :4 * W].astype(bf16)
        w_u = w[:, 4 * W + N_GATES:].astype(bf16)
        p, gates_t = _in_proj(x2, norm1_g[l][None, :], w_a, w_u, wg_t)

        gate_b = jnp.concatenate([ig_b[l], fg_b[l]])[:, None].astype(f32)
        gates_b = gates_t.reshape(N_GATES, B, S).transpose(1, 0, 2)
        ym = _mlstm(p.reshape(B, S, N_MAIN), gates_b, conv_w[l].astype(f32), gate_b,
                    head_norm_g[l][None, :], tri, shifts, halo_shifts).reshape(T, W)

        x1, h2, idx_t, gate_t, rank_t, cnt = _out_route(
            x2, ym, p, pool_w[l].astype(bf16), pool_scale[l][None, :], w_out[l],
            norm2_g[l][None, :], w_router[l].T.astype(bf16), b_router[l][:, None], S)

        counts = cnt[:, 0]
        padded = ((counts + TM_EXPERT - 1) // TM_EXPERT) * TM_EXPERT
        padded_end = jnp.cumsum(padded)
        padded_start = padded_end - padded
        expert_ids = jnp.arange(N_EXPERTS, dtype=jnp.int32)[:, None, None]
        start_of = jnp.sum(jnp.where(idx_t[None] == expert_ids, padded_start[:, None, None], 0), axis=0)
        dest = start_of + rank_t
        block_start = jnp.concatenate(
            [jnp.zeros((1,), jnp.int32), (padded_end // TM_EXPERT).astype(jnp.int32)])

        slot_buf = _plan(dest.reshape(-1) + TM_EXPERT, fill)
        y_tok = _experts(block_start, slot_buf, h2, w_gate[l], b_gate[l][:, None, :],
                         w_up[l], b_up[l][:, None, :], w_down[l], b_down[l][:, None, :], T)
        last = l + 1 == depth
        x2 = _combine(x1, y_tok, gate_t, normf_g[None, :], last)
    return x2.reshape(B, S, D)
```

```python
import functools

import jax
import jax.numpy as jnp
from jax import lax
from jax.experimental import pallas as pl
from jax.experimental.pallas import tpu as pltpu
from jax.experimental.pallas import tpu_sc as plsc

D_MODEL = 1024
MLSTM_WIDTH = 512
MLSTM_HEADS = 4
HEAD_DIM = 128
CONV_WIDTH = 4
POOL_WIDTH = 512
POOL_WINDOWS = (2, 4, 8, 16)
POOL_GROUP_DIM = 128
N_EXPERTS = 32
TOP_K = 4
D_FF = 1024
SWIGLU_LIMIT = 7.0
SWIGLU_ALPHA = 1.702
EPS = 1e-5

N_MAIN = 4 * MLSTM_WIDTH + POOL_WIDTH
N_GATES = 2 * MLSTM_HEADS

LANES = 128
BF16_SUBLANES = 16
VMEM_LIMIT = 56 * 1024 * 1024

TM_PROJ = 512
PROJ_SUB = 2
ROUTE_SUB = 2
CHUNK = 256
MLSTM_BATCH = 2
HALO = 16
TM_EXPERT = 256
NBUF = 5
ROW_DMA_PRIORITY = 1
SLAB = D_MODEL // LANES
PSLAB = SLAB // 2
PLAN_CHUNK = 8192
SC_LANES = 16
PLAN_UNROLL = 8

NT_DIMS = (((1,), (1,)), ((), ()))


def _sigmoid(x):
    return 1.0 / (1.0 + jnp.exp(-x))


def _pack_bf16_pairs(v):
    half = v.shape[1] // 2
    lo = pltpu.bitcast(v[:, :half].astype(jnp.bfloat16).astype(jnp.float32), jnp.uint32)
    hi = pltpu.bitcast(v[:, half:].astype(jnp.bfloat16).astype(jnp.float32), jnp.uint32)
    return (lo >> 16) | (hi & jnp.uint32(0xFFFF0000))


def _unpack_lo(w):
    return pltpu.bitcast(w << 16, jnp.float32)


def _unpack_hi(w):
    return pltpu.bitcast(w & jnp.uint32(0xFFFF0000), jnp.float32)


def _in_proj_kernel(x_ref, g_ref, wa_ref, wu_ref, wgt_ref, p_ref, gt_ref):
    n_a = wa_ref.shape[1]
    for sub in range(PROJ_SUB):
        rows = slice(sub * TM_PROJ, (sub + 1) * TM_PROJ)
        x = x_ref[rows, :]
        h = x * lax.rsqrt(jnp.mean(x * x, axis=-1, keepdims=True) + EPS) * g_ref[...]
        hb = h.astype(jnp.bfloat16)
        p_ref[rows, :n_a] = jnp.dot(hb, wa_ref[...],
                                    preferred_element_type=jnp.float32).astype(p_ref.dtype)
        p_ref[rows, n_a:] = jnp.dot(hb, wu_ref[...],
                                    preferred_element_type=jnp.float32).astype(p_ref.dtype)
        gt = lax.dot_general(wgt_ref[...], hb, NT_DIMS, preferred_element_type=jnp.float32)
        gt_ref[:, rows] = gt[:N_GATES]


def _in_proj(x2, g1, w_a, w_u, wg_t):
    T = x2.shape[0]
    return pl.pallas_call(
        _in_proj_kernel,
        grid=(T // (PROJ_SUB * TM_PROJ),),
        in_specs=[
            pl.BlockSpec((PROJ_SUB * TM_PROJ, D_MODEL), lambda i: (i, 0)),
            pl.BlockSpec((1, D_MODEL), lambda i: (0, 0)),
            pl.BlockSpec(w_a.shape, lambda i: (0, 0)),
            pl.BlockSpec(w_u.shape, lambda i: (0, 0)),
            pl.BlockSpec((BF16_SUBLANES, D_MODEL), lambda i: (0, 0)),
        ],
        out_specs=[
            pl.BlockSpec((PROJ_SUB * TM_PROJ, N_MAIN), lambda i: (i, 0)),
            pl.BlockSpec((N_GATES, PROJ_SUB * TM_PROJ), lambda i: (0, i)),
        ],
        out_shape=[
            jax.ShapeDtypeStruct((T, N_MAIN), jnp.bfloat16),
            jax.ShapeDtypeStruct((N_GATES, T), jnp.float32),
        ],
        compiler_params=pltpu.CompilerParams(
            dimension_semantics=("parallel",), vmem_limit_bytes=VMEM_LIMIT),
        name="in_proj",
    )(x2, g1, w_a, w_u, wg_t)


def _mlstm_kernel(qk_ref, qkp_ref, v_ref, o_ref, gt_ref, convw_ref, gb_ref, hng_ref,
                  tri_ref, shift_ref, hshift_ref, y_ref, cn_ref, m_ref):
    L = CHUNK
    c = pl.program_id(1)

    @pl.when(c == 0)
    def _():
        cn_ref[...] = jnp.zeros_like(cn_ref)
        m_ref[...] = jnp.zeros_like(m_ref)

    row_id = lax.broadcasted_iota(jnp.int32, (L, L), 0)
    col_id = lax.broadcasted_iota(jnp.int32, (L, L), 1)
    causal = col_id <= row_id
    ones_blk = jnp.ones((L, HEAD_DIM), jnp.bfloat16)
    lane = lax.broadcasted_iota(jnp.int32, (MLSTM_HEADS, L), 1)

    gate_terms = []
    for bb in range(MLSTM_BATCH):
        gt = gt_ref[bb] + gb_ref[...]
        f = gt[MLSTM_HEADS:]
        lf = jnp.minimum(f, 0.0) - jnp.log(1.0 + jnp.exp(-jnp.abs(f)))
        ig = gt[:MLSTM_HEADS]
        b_rows = lax.dot_general(lf, tri_ref[...], NT_DIMS, precision=lax.Precision.HIGHEST,
                                 preferred_element_type=jnp.float32)
        c_rows = ig - b_rows
        cm_rows = c_rows
        d = 1
        while d < L:
            cm_rows = jnp.maximum(
                cm_rows, jnp.where(lane >= d, pltpu.roll(cm_rows, d, axis=1), -jnp.inf))
            d *= 2
        gate_terms.append((b_rows, c_rows, cm_rows))

    conv_terms = []
    for bb in range(MLSTM_BATCH):
        x_cur = qk_ref[bb]
        x_prev = jnp.where(c > 0, qkp_ref[bb], jnp.zeros((HALO, 2 * MLSTM_WIDTH), jnp.bfloat16))
        acc = convw_ref[CONV_WIDTH - 1:CONV_WIDTH, :] * x_cur.astype(jnp.float32)
        for j in range(CONV_WIDTH - 1):
            sh = jnp.dot(shift_ref[j], x_cur, preferred_element_type=jnp.float32)
            top = sh[:8] + jnp.dot(hshift_ref[j], x_prev, preferred_element_type=jnp.float32)
            sh = jnp.concatenate([top, sh[8:]], axis=0)
            acc = acc + convw_ref[j:j + 1, :] * sh
        qk = acc * _sigmoid(acc)
        q_all = qk[:, :MLSTM_WIDTH].astype(jnp.bfloat16)
        k_t = jnp.transpose(qk[:, MLSTM_WIDTH:] * (HEAD_DIM ** -0.5))
        conv_terms.append((q_all, k_t))

    for bb in range(MLSTM_BATCH):
        b_rows, c_rows, cm_rows = gate_terms[bb]
        q_all, k_t = conv_terms[bb]
        m_in4 = jnp.concatenate(
            [m_ref[bb * MLSTM_HEADS + h][0:1, 0:1] for h in range(MLSTM_HEADS)], axis=0)
        mx_rows = jnp.maximum(cm_rows, m_in4)
        inter_rows = jnp.exp(m_in4 - mx_rows)
        einv_rows = jnp.exp(-(b_rows + mx_rows))
        fac_t = jnp.transpose(jnp.concatenate(
            [mx_rows, inter_rows, einv_rows, jnp.zeros_like(mx_rows)], axis=0))

        for h in range(MLSTM_HEADS):
            lo = h * HEAD_DIM
            st = bb * MLSTM_HEADS + h
            q = q_all[:, lo:lo + HEAD_DIM]
            kt = k_t[lo:lo + HEAD_DIM, :]
            v_ext = jnp.concatenate([v_ref[bb, :, lo:lo + HEAD_DIM], ones_blk], axis=1)
            mx_col = fac_t[:, h:h + 1]
            inter_col = fac_t[:, MLSTM_HEADS + h:MLSTM_HEADS + h + 1]
            einv_col = fac_t[:, 2 * MLSTM_HEADS + h:2 * MLSTM_HEADS + h + 1]
            c_row = c_rows[h:h + 1, :]
            b_tot = b_rows[h:h + 1, L - 1:L]
            cm_tot = cm_rows[h:h + 1, L - 1:L]
            m_in = m_ref[st][0:1, 0:1]
            cn = cn_ref[st]

            s_qk = jnp.dot(q, kt.astype(jnp.bfloat16), preferred_element_type=jnp.float32)
            s = (s_qk * jnp.exp(jnp.where(causal, c_row - mx_col, -jnp.inf))).astype(jnp.bfloat16)
            num = (jnp.dot(s, v_ext, preferred_element_type=jnp.float32)
                   + inter_col * jnp.dot(q, cn.astype(jnp.bfloat16),
                                         preferred_element_type=jnp.float32))
            den = num[:, HEAD_DIM:]
            hh = num[:, :HEAD_DIM] / jnp.maximum(jnp.abs(den), einv_col)

            mu = jnp.mean(hh, axis=-1, keepdims=True)
            dv = hh - mu
            var = jnp.mean(dv * dv, axis=-1, keepdims=True)
            hn = dv * lax.rsqrt(var + EPS) * hng_ref[:, lo:lo + HEAD_DIM]
            og = _sigmoid(o_ref[bb, :, lo:lo + HEAD_DIM].astype(jnp.float32))
            y_ref[bb, :, lo:lo + HEAD_DIM] = (og * hn).astype(y_ref.dtype)

            m_loc = b_tot + cm_tot
            kw_t = (kt * jnp.exp(c_row - cm_tot)).astype(jnp.bfloat16)
            c_loc = jnp.dot(kw_t, v_ext, preferred_element_type=jnp.float32)
            m_new = jnp.maximum(b_tot + m_in, m_loc)
            s_old = jnp.exp(b_tot + m_in - m_new)
            s_loc = jnp.exp(m_loc - m_new)
            cn_ref[st] = s_old * cn + s_loc * c_loc
            m_ref[st] = jnp.broadcast_to(m_new, m_ref.shape[1:])


def _mlstm(p3, gates_b, conv_w, gate_b, hn_g, tri, shifts, halo_shifts):
    batch, seq, _ = p3.shape
    L = CHUNK
    BB = MLSTM_BATCH
    halo_per_chunk = L // HALO
    return pl.pallas_call(
        _mlstm_kernel,
        grid=(batch // BB, seq // L),
        in_specs=[
            pl.BlockSpec((BB, L, 2 * MLSTM_WIDTH), lambda bi, ci: (bi, ci, 0)),
            pl.BlockSpec((BB, HALO, 2 * MLSTM_WIDTH),
                         lambda bi, ci: (bi, jnp.maximum(ci * halo_per_chunk - 1, 0), 0)),
            pl.BlockSpec((BB, L, MLSTM_WIDTH), lambda bi, ci: (bi, ci, 2)),
            pl.BlockSpec((BB, L, MLSTM_WIDTH), lambda bi, ci: (bi, ci, 3)),
            pl.BlockSpec((BB, N_GATES, L), lambda bi, ci: (bi, 0, ci)),
            pl.BlockSpec((CONV_WIDTH, 2 * MLSTM_WIDTH), lambda bi, ci: (0, 0)),
            pl.BlockSpec((N_GATES, 1), lambda bi, ci: (0, 0)),
            pl.BlockSpec((1, MLSTM_WIDTH), lambda bi, ci: (0, 0)),
            pl.BlockSpec((L, L), lambda bi, ci: (0, 0)),
            pl.BlockSpec((CONV_WIDTH - 1, L, L), lambda bi, ci: (0, 0, 0)),
            pl.BlockSpec((CONV_WIDTH - 1, 8, HALO), lambda bi, ci: (0, 0, 0)),
        ],
        out_specs=pl.BlockSpec((BB, L, MLSTM_WIDTH), lambda bi, ci: (bi, ci, 0)),
        out_shape=jax.ShapeDtypeStruct((batch, seq, MLSTM_WIDTH), jnp.bfloat16),
        scratch_shapes=[
            pltpu.VMEM((BB * MLSTM_HEADS, HEAD_DIM, 2 * HEAD_DIM), jnp.float32),
            pltpu.VMEM((BB * MLSTM_HEADS, 8, LANES), jnp.float32),
        ],
        compiler_params=pltpu.CompilerParams(
            dimension_semantics=("parallel", "arbitrary"), vmem_limit_bytes=VMEM_LIMIT),
        name="mlstm",
    )(p3, p3, p3, p3, gates_b, conv_w, gate_b, hn_g, tri, shifts, halo_shifts)


def _out_route_kernel(seq, x_ref, ym_ref, u_ref, up_ref, pw_ref, ps_ref, wo_ref, g2_ref,
                      wrt_ref, br_ref, x1_ref, h2_ref, idx_ref, gate_ref, rank_ref, cnt_ref,
                      carry_ref):
    TM = TM_PROJ
    R = ROUTE_SUB * TM
    i = pl.program_id(0)

    @pl.when(i == 0)
    def _():
        carry_ref[...] = jnp.zeros_like(carry_ref)

    pos0 = (i * R) % seq
    e_id = lax.broadcasted_iota(jnp.int32, (N_EXPERTS, TM), 0).astype(jnp.float32)
    t_row = lax.broadcasted_iota(jnp.int32, (TM, TM), 0)
    t_col = lax.broadcasted_iota(jnp.int32, (TM, TM), 1)
    before = jnp.where(t_row < t_col, 1.0, 0.0).astype(jnp.bfloat16)
    carry = carry_ref[...]
    subs = [slice(sub * TM, (sub + 1) * TM) for sub in range(ROUTE_SUB)]

    halo = jnp.where(pos0 > 0, up_ref[...].astype(jnp.float32), 0.0)
    u_ext = jnp.concatenate([halo, u_ref[...].astype(jnp.float32)], axis=0)
    win_sums = []
    for gi, w in enumerate(POOL_WINDOWS):
        sw = u_ext[:, gi * POOL_GROUP_DIM:(gi + 1) * POOL_GROUP_DIM]
        span = 1
        while span < w:
            sw = sw + pltpu.roll(sw, span, axis=0)
            span *= 2
        win_sums.append(sw)
    y_cats = []
    for sub, rows in enumerate(subs):
        r0 = sub * TM
        pos = (pos0 + r0 + lax.broadcasted_iota(jnp.int32, (TM, 1), 0) + 1).astype(jnp.float32)
        mixed = []
        for gi, w in enumerate(POOL_WINDOWS):
            lo = gi * POOL_GROUP_DIM
            tok = u_ext[HALO + r0:HALO + r0 + TM, lo:lo + POOL_GROUP_DIM]
            pooled = win_sums[gi][HALO + r0:HALO + r0 + TM] / jnp.minimum(pos, float(w)) - tok
            mg = jnp.dot(pooled.astype(jnp.bfloat16), pw_ref[gi],
                         preferred_element_type=jnp.float32)
            mixed.append((mg * ps_ref[:, lo:lo + POOL_GROUP_DIM]).astype(jnp.bfloat16))
        y_cats.append(jnp.concatenate([ym_ref[rows, :]] + mixed, axis=1))

    all_logits = []
    for sub, rows in enumerate(subs):
        r0 = sub * TM
        x1 = x_ref[rows, :] + jnp.dot(y_cats[sub], wo_ref[...], preferred_element_type=jnp.float32)
        x1_ref[rows, :] = x1
        h2 = x1 * lax.rsqrt(jnp.mean(x1 * x1, axis=-1, keepdims=True) + EPS) * g2_ref[...]
        h2b = h2.astype(jnp.bfloat16)
        h2w = _pack_bf16_pairs(h2)
        for s in range(PSLAB):
            h2_ref[pl.ds(r0 * PSLAB + s, TM, stride=PSLAB), :] = h2w[:, s * LANES:(s + 1) * LANES]
        all_logits.append(lax.dot_general(wrt_ref[...], h2b, NT_DIMS,
                                          preferred_element_type=jnp.float32) + br_ref[...])

    for sub, rows in enumerate(subs):
        work = all_logits[sub]
        vals, ids, hots = [], [], []
        for _ in range(TOP_K):
            mk = jnp.max(work, axis=0, keepdims=True)
            ik = jnp.min(jnp.where(work == mk, e_id, float(N_EXPERTS)), axis=0, keepdims=True)
            hot = e_id == ik
            work = jnp.where(hot, -jnp.inf, work)
            vals.append(mk)
            ids.append(ik)
            hots.append(hot)
        ex = [jnp.exp(vk - vals[0]) for vk in vals]
        denom = ex[0] + ex[1] + ex[2] + ex[3]
        gate_ref[:, rows] = jnp.concatenate([e / denom for e in ex], axis=0)
        idx_ref[:, rows] = jnp.concatenate(ids, axis=0).astype(jnp.int32)

        sel_f = sum(jnp.where(hot, 1.0, 0.0) for hot in hots)
        prefix = jnp.dot(sel_f.astype(jnp.bfloat16), before, preferred_element_type=jnp.float32)
        rank_e = carry[:, 0:1] + prefix
        ranks = [jnp.sum(jnp.where(hot, rank_e, 0.0), axis=0, keepdims=True) for hot in hots]
        rank_ref[:, rows] = jnp.concatenate(ranks, axis=0).astype(jnp.int32)
        carry = carry + jnp.sum(sel_f, axis=1, keepdims=True)
    carry_ref[...] = carry
    cnt_ref[...] = carry.astype(jnp.int32)


def _out_route(x2, ym, p, pool_w, pool_s, w_out, g2, wr_t, br, seq):
    T = x2.shape[0]
    TM = ROUTE_SUB * TM_PROJ
    nt = T // TM
    u_blk = N_MAIN // POOL_WIDTH - 1
    halo_per_tile = TM // HALO
    tok_spec = pl.BlockSpec((TOP_K, TM), lambda i: (0, i))
    return pl.pallas_call(
        functools.partial(_out_route_kernel, seq),
        grid=(nt,),
        in_specs=[
            pl.BlockSpec((TM, D_MODEL), lambda i: (i, 0)),
            pl.BlockSpec((TM, MLSTM_WIDTH), lambda i: (i, 0)),
            pl.BlockSpec((TM, POOL_WIDTH), lambda i: (i, u_blk)),
            pl.BlockSpec((HALO, POOL_WIDTH),
                         lambda i: (jnp.maximum(i * halo_per_tile - 1, 0), u_blk)),
            pl.BlockSpec((len(POOL_WINDOWS), POOL_GROUP_DIM, POOL_GROUP_DIM), lambda i: (0, 0, 0)),
            pl.BlockSpec((1, POOL_WIDTH), lambda i: (0, 0)),
            pl.BlockSpec((D_MODEL, D_MODEL), lambda i: (0, 0)),
            pl.BlockSpec((1, D_MODEL), lambda i: (0, 0)),
            pl.BlockSpec((N_EXPERTS, D_MODEL), lambda i: (0, 0)),
            pl.BlockSpec((N_EXPERTS, 1), lambda i: (0, 0)),
        ],
        out_specs=[
            pl.BlockSpec((TM, D_MODEL), lambda i: (i, 0)),
            pl.BlockSpec((TM * PSLAB, LANES), lambda i: (i, 0)),
            tok_spec, tok_spec, tok_spec,
            pl.BlockSpec((N_EXPERTS, LANES), lambda i: (0, 0)),
        ],
        out_shape=[
            jax.ShapeDtypeStruct((T, D_MODEL), jnp.float32),
            jax.ShapeDtypeStruct((T * PSLAB, LANES), jnp.uint32),
            jax.ShapeDtypeStruct((TOP_K, T), jnp.int32),
            jax.ShapeDtypeStruct((TOP_K, T), jnp.float32),
            jax.ShapeDtypeStruct((TOP_K, T), jnp.int32),
            jax.ShapeDtypeStruct((N_EXPERTS, LANES), jnp.int32),
        ],
        scratch_shapes=[
            pltpu.VMEM((N_EXPERTS, LANES), jnp.float32),
        ],
        compiler_params=pltpu.CompilerParams(
            dimension_semantics=("arbitrary",), vmem_limit_bytes=VMEM_LIMIT),
        name="out_route",
    )(x2, ym, p, p, pool_w, pool_s, w_out, g2, wr_t, br)


def _plan(dest_flat, fill):
    n_assign = dest_flat.shape[0]
    n_table = fill.shape[0]
    mesh = plsc.VectorSubcoreMesh(core_axis_name="c", subcore_axis_name="s")

    @pl.kernel(out_type=jax.ShapeDtypeStruct((n_table,), jnp.int32), mesh=mesh,
               scratch_types=[pltpu.VMEM((n_table,), jnp.int32),
                              pltpu.VMEM((PLAN_CHUNK,), jnp.int32)],
               compiler_params=pltpu.CompilerParams(needs_layout_passes=False))
    def plan_kernel(dest_hbm, fill_hbm, out_hbm, table, chunk):
        first = jnp.logical_and(lax.axis_index("c") == 0, lax.axis_index("s") == 0)

        @pl.when(first)
        def _():
            pltpu.sync_copy(fill_hbm, table)

            @pl.loop(0, n_assign // PLAN_CHUNK)
            def _(ci):
                pltpu.sync_copy(dest_hbm.at[pl.ds(ci * PLAN_CHUNK, PLAN_CHUNK)], chunk)

                @pl.loop(0, PLAN_CHUNK // (SC_LANES * PLAN_UNROLL))
                def _(i):
                    for j in range(PLAN_UNROLL):
                        off = (i * PLAN_UNROLL + j) * SC_LANES
                        idx = chunk[pl.ds(off, SC_LANES)]
                        vals = (ci * PLAN_CHUNK + off
                                + lax.broadcasted_iota(jnp.int32, (SC_LANES,), 0))
                        plsc.store_scatter(table, [idx], vals)

            pltpu.sync_copy(table, out_hbm)

    return plan_kernel(dest_flat, fill)


def _expert_kernel(n_tok, bs_ref, slot_ref, h2_ref, wg_ref, bg_ref, wu_ref, bu_ref, wd_ref, bd_ref,
                   yt_ref, *scratch):
    TM = TM_EXPERT
    ROWS = TM * PSLAB
    e = pl.program_id(0)
    n_total = bs_ref[N_EXPERTS]
    xg = scratch[:NBUF]
    ys = scratch[NBUF:2 * NBUF]
    wgb_ref, wub_ref, wdb_ref, gsem, ssem = scratch[2 * NBUF:]

    def token_of(a):
        return a & (n_tok - 1) if n_tok & (n_tok - 1) == 0 else lax.rem(a, n_tok)

    def start_gather(blk, par):
        base = (blk + 1) * TM
        for r in range(TM):
            t = token_of(slot_ref[base + r])
            pltpu.make_async_copy(h2_ref.at[pl.ds(pl.multiple_of(t * PSLAB, PSLAB), PSLAB), :],
                                  xg[par].at[pl.ds(r * PSLAB, PSLAB), :], gsem.at[par]
                                  ).start(priority=ROW_DMA_PRIORITY)

    def wait_gather(par):
        pltpu.make_async_copy(h2_ref.at[pl.ds(0, ROWS), :], xg[0], gsem.at[par]).wait()

    def start_scatter(blk, par):
        base = (blk + 1) * TM
        for r in range(TM):
            a = slot_ref[base + r]
            pltpu.make_async_copy(ys[par].at[pl.ds(r * PSLAB, PSLAB), :],
                                  yt_ref.at[pl.ds(pl.multiple_of(a * PSLAB, PSLAB), PSLAB), :],
                                  ssem.at[par]).start(priority=ROW_DMA_PRIORITY)

    def wait_scatter(par):
        pltpu.make_async_copy(ys[0], yt_ref.at[pl.ds(0, ROWS), :], ssem.at[par]).wait()

    @pl.when(e == 0)
    def _():
        for blk in range(NBUF - 1):
            start_gather(blk, blk)
        for par in range(NBUF):
            ys[par][...] = jnp.zeros_like(ys[par])
            dump = yt_ref.at[pl.ds((n_tok * TOP_K + par * TM) * PSLAB, ROWS), :]
            cp = pltpu.make_async_copy(ys[par], dump, ssem.at[par])
            cp.start()
            cp.wait()

    wgb_ref[...] = wg_ref[0].astype(jnp.bfloat16)
    wub_ref[...] = wu_ref[0].astype(jnp.bfloat16)
    wdb_ref[...] = wd_ref[0].astype(jnp.bfloat16)

    def block_step(g, par):
        prv = (par + NBUF - 1) % NBUF
        wait_gather(par)

        @pl.when(g >= NBUF - 1)
        def _():
            wait_scatter(par)

        start_gather(g + NBUF - 1, prv)
        start_scatter(g - 1, prv)
        words = [xg[par][pl.ds(s, TM, stride=PSLAB), :] for s in range(PSLAB)]
        x = jnp.concatenate([_unpack_lo(w).astype(jnp.bfloat16) for w in words]
                            + [_unpack_hi(w).astype(jnp.bfloat16) for w in words], axis=1)
        gate = jnp.dot(x, wgb_ref[...], preferred_element_type=jnp.float32) + bg_ref[0]
        up = jnp.dot(x, wub_ref[...], preferred_element_type=jnp.float32) + bu_ref[0]
        gate = jnp.minimum(gate, SWIGLU_LIMIT)
        up = jnp.clip(up, -SWIGLU_LIMIT, SWIGLU_LIMIT)
        glu = gate * _sigmoid(SWIGLU_ALPHA * gate)
        act = (glu * (up + 1.0)).astype(jnp.bfloat16)
        y = jnp.dot(act, wdb_ref[...], preferred_element_type=jnp.float32) + bd_ref[0]
        packed = _pack_bf16_pairs(y)
        for s in range(PSLAB):
            ys[par][pl.ds(s, TM, stride=PSLAB), :] = packed[:, s * LANES:(s + 1) * LANES]

    def body(g, carry):
        for par in range(NBUF):
            pl.when(g % NBUF == par)(functools.partial(block_step, g, par))
        return carry

    lax.fori_loop(bs_ref[e], bs_ref[e + 1], body, 0)

    @pl.when(e == N_EXPERTS - 1)
    def _():
        g = n_total
        for par in range(NBUF):
            @pl.when((g - 1) % NBUF == par)
            def _():
                start_scatter(g - 1, par)
        for j in range(NBUF - 1):
            wait_gather((g + j) % NBUF)
        wait_scatter((g - 1) % NBUF)
        for j in range(2, NBUF + 1):
            @pl.when(g >= j - 1)
            def _():
                wait_scatter((g + NBUF - j) % NBUF)


def _experts(block_start, slot_buf, h2_slab, w_gate, b_gate, w_up, b_up, w_down, b_down, n_tok):
    TM = TM_EXPERT
    n_assign = n_tok * TOP_K
    w_spec = pl.BlockSpec((1, D_MODEL, D_FF), lambda e, bs, sl: (e, 0, 0))
    bias_spec = pl.BlockSpec((1, 1, D_FF), lambda e, bs, sl: (e, 0, 0))
    buf = pltpu.VMEM((TM * PSLAB, LANES), jnp.uint32)
    grid_spec = pltpu.PrefetchScalarGridSpec(
        num_scalar_prefetch=2,
        grid=(N_EXPERTS,),
        in_specs=[
            pl.BlockSpec(memory_space=pl.ANY),
            w_spec, bias_spec, w_spec, bias_spec, w_spec, bias_spec,
        ],
        out_specs=pl.BlockSpec(memory_space=pl.ANY),
        scratch_shapes=[
            *([buf] * (2 * NBUF)),
            pltpu.VMEM((D_MODEL, D_FF), jnp.bfloat16),
            pltpu.VMEM((D_MODEL, D_FF), jnp.bfloat16),
            pltpu.VMEM((D_FF, D_MODEL), jnp.bfloat16),
            pltpu.SemaphoreType.DMA((NBUF,)),
            pltpu.SemaphoreType.DMA((NBUF,)),
        ],
    )
    return pl.pallas_call(
        functools.partial(_expert_kernel, n_tok),
        grid_spec=grid_spec,
        out_shape=jax.ShapeDtypeStruct(((n_assign + NBUF * TM) * PSLAB, LANES), jnp.uint32),
        compiler_params=pltpu.CompilerParams(
            dimension_semantics=("arbitrary",), vmem_limit_bytes=VMEM_LIMIT),
        name="experts",
    )(block_start, slot_buf, h2_slab, w_gate, b_gate, w_up, b_up, w_down, b_down)


def _combine_kernel(normalize, x1_ref, y0_ref, y1_ref, y2_ref, y3_ref, gate_ref, g_ref, o_ref):
    TM = TM_PROJ
    gates = jnp.concatenate([gate_ref[...], jnp.zeros((8 - TOP_K, TM), jnp.float32)], axis=0)
    g_cols = jnp.transpose(gates)
    g_bc = [jnp.broadcast_to(g_cols[:, k:k + 1], (TM, LANES)) for k in range(TOP_K)]
    ssq = jnp.zeros((TM, LANES), jnp.float32)
    parts = [x1_ref[:, s * LANES:(s + 1) * LANES] for s in range(SLAB)]
    for s in range(PSLAB):
        for k, y_ref in enumerate((y0_ref, y1_ref, y2_ref, y3_ref)):
            w = y_ref[pl.ds(s, TM, stride=PSLAB), :]
            parts[s] = parts[s] + g_bc[k] * _unpack_lo(w)
            parts[PSLAB + s] = parts[PSLAB + s] + g_bc[k] * _unpack_hi(w)
    for acc in parts:
        ssq = ssq + acc * acc
    if normalize:
        inv = lax.rsqrt(jnp.sum(ssq, axis=-1, keepdims=True) * (1.0 / D_MODEL) + EPS)
        for s in range(SLAB):
            o_ref[:, s * LANES:(s + 1) * LANES] = parts[s] * inv * g_ref[:, s * LANES:(s + 1) * LANES]
    else:
        for s in range(SLAB):
            o_ref[:, s * LANES:(s + 1) * LANES] = parts[s]


def _combine(x1, y_tok, gate_t, gf, normalize):
    T = x1.shape[0]
    TM = TM_PROJ
    nt = T // TM

    def y_spec(k):
        return pl.BlockSpec((TM * PSLAB, LANES), lambda i: (k * nt + i, 0))

    return pl.pallas_call(
        functools.partial(_combine_kernel, normalize),
        grid=(nt,),
        in_specs=[
            pl.BlockSpec((TM, D_MODEL), lambda i: (i, 0)),
            y_spec(0), y_spec(1), y_spec(2), y_spec(3),
            pl.BlockSpec((TOP_K, TM), lambda i: (0, i)),
            pl.BlockSpec((1, D_MODEL), lambda i: (0, 0)),
        ],
        out_specs=pl.BlockSpec((TM, D_MODEL), lambda i: (i, 0)),
        out_shape=jax.ShapeDtypeStruct((T, D_MODEL), jnp.float32),
        compiler_params=pltpu.CompilerParams(
            dimension_semantics=("parallel",), vmem_limit_bytes=VMEM_LIMIT),
        name="combine",
    )(x1, y_tok, y_tok, y_tok, y_tok, gate_t, gf)


def kernel(x, norm1_g, w_in, ig_b, fg_b, conv_w, head_norm_g, pool_w, pool_scale, w_out, norm2_g,
           w_router, b_router, w_gate, b_gate, w_up, b_up, w_down, b_down, normf_g):
    B, S, D = x.shape
    T = B * S
    depth = norm1_g.shape[0]
    W = MLSTM_WIDTH
    f32, bf16 = jnp.float32, jnp.bfloat16

    L = CHUNK
    t_l = lax.broadcasted_iota(jnp.int32, (L, L), 0)
    t_r = lax.broadcasted_iota(jnp.int32, (L, L), 1)
    tri = (t_r <= t_l).astype(f32)
    shifts = jnp.stack([(t_l - t_r == CONV_WIDTH - 1 - j).astype(bf16)
                        for j in range(CONV_WIDTH - 1)])
    h_t = lax.broadcasted_iota(jnp.int32, (8, HALO), 0)
    h_r = lax.broadcasted_iota(jnp.int32, (8, HALO), 1)
    halo_shifts = jnp.stack([(h_r - HALO - h_t == -(CONV_WIDTH - 1 - j)).astype(bf16)
                             for j in range(CONV_WIDTH - 1)])

    n_assign = T * TOP_K
    n_blocks = -(-n_assign // TM_EXPERT) + N_EXPERTS
    n_rows = n_blocks * TM_EXPERT
    n_table = n_rows + NBUF * TM_EXPERT
    fill = n_assign + ((jnp.arange(n_table, dtype=jnp.int32) + (NBUF - 1) * TM_EXPERT)
                       % (NBUF * TM_EXPERT))
    x2 = x.reshape(T, D)
    for l in range(depth):
        w = w_in[l]
        w_a = w[:, :4 * W].astype(bf16)
        w_u = w[:, 4 * W + N_GATES:].astype(bf16)
        wg_t = jnp.zeros((BF16_SUBLANES, D), bf16).at[:N_GATES].set(
            w[:, 4 * W:4 * W + N_GATES].T.astype(bf16))
        p, gates_t = _in_proj(x2, norm1_g[l][None, :], w_a, w_u, wg_t)

        gate_b = jnp.concatenate([ig_b[l], fg_b[l]])[:, None].astype(f32)
        gates_b = gates_t.reshape(N_GATES, B, S).transpose(1, 0, 2)
        ym = _mlstm(p.reshape(B, S, N_MAIN), gates_b, conv_w[l].astype(f32), gate_b,
                    head_norm_g[l][None, :], tri, shifts, halo_shifts).reshape(T, W)

        x1, h2, idx_t, gate_t, rank_t, cnt = _out_route(
            x2, ym, p, pool_w[l].astype(bf16), pool_scale[l][None, :], w_out[l].astype(bf16),
            norm2_g[l][None, :], w_router[l].T.astype(bf16), b_router[l][:, None], S)

        counts = cnt[:, 0]
        padded = ((counts + TM_EXPERT - 1) // TM_EXPERT) * TM_EXPERT
        padded_end = jnp.cumsum(padded)
        padded_start = padded_end - padded
        expert_ids = jnp.arange(N_EXPERTS, dtype=jnp.int32)[:, None, None]
        start_of = jnp.sum(jnp.where(idx_t[None] == expert_ids, padded_start[:, None, None], 0), axis=0)
        dest = start_of + rank_t
        block_start = jnp.concatenate(
            [jnp.zeros((1,), jnp.int32), (padded_end // TM_EXPERT).astype(jnp.int32)])

        slot_buf = _plan(dest.reshape(-1) + TM_EXPERT, fill)
        y_tok = _experts(block_start, slot_buf, h2, w_gate[l], b_gate[l][:, None, :],
                         w_up[l], b_up[l][:, None, :], w_down[l], b_down[l][:, None, :], T)
        last = l + 1 == depth
        x2 = _combine(x1, y_tok, gate_t, normf_g[None, :], last)
    return x2.reshape(B, S, D)
```

```python
import functools

import jax
import jax.numpy as jnp
from jax import lax
from jax.experimental import pallas as pl
from jax.experimental.pallas import tpu as pltpu
from jax.experimental.pallas import tpu_sc as plsc

D_MODEL = 1024
MLSTM_WIDTH = 512
MLSTM_HEADS = 4
HEAD_DIM = 128
CONV_WIDTH = 4
POOL_WIDTH = 512
POOL_WINDOWS = (2, 4, 8, 16)
POOL_GROUP_DIM = 128
N_EXPERTS = 32
TOP_K = 4
D_FF = 1024
SWIGLU_LIMIT = 7.0
SWIGLU_ALPHA = 1.702
EPS = 1e-5

N_MAIN = 4 * MLSTM_WIDTH + POOL_WIDTH
N_GATES = 2 * MLSTM_HEADS

LANES = 128
BF16_SUBLANES = 16
VMEM_LIMIT = 56 * 1024 * 1024

TM_PROJ = 512
ROUTE_SUB = 2
CHUNK = 256
MLSTM_BATCH = 2
HALO = 16
TM_EXPERT = 256
NBUF = 4
GU_CHUNK = 256
ROW_DMA_PRIORITY = 1
SLAB = D_MODEL // LANES
PSLAB = SLAB // 2
PLAN_CHUNK = 8192
SC_LANES = 16
PLAN_UNROLL = 8

NT_DIMS = (((1,), (1,)), ((), ()))


def _sigmoid(x):
    return 1.0 / (1.0 + jnp.exp(-x))


def _pack_bf16_pairs(v):
    half = v.shape[1] // 2
    lo = pltpu.bitcast(v[:, :half].astype(jnp.bfloat16).astype(jnp.float32), jnp.uint32)
    hi = pltpu.bitcast(v[:, half:].astype(jnp.bfloat16).astype(jnp.float32), jnp.uint32)
    return (lo >> 16) | (hi & jnp.uint32(0xFFFF0000))


def _unpack_lo(w):
    return pltpu.bitcast(w << 16, jnp.float32)


def _unpack_hi(w):
    return pltpu.bitcast(w & jnp.uint32(0xFFFF0000), jnp.float32)


def _in_proj_kernel(x_ref, g_ref, wa_ref, wu_ref, wgt_ref, p_ref, gt_ref):
    x = x_ref[...]
    h = x * lax.rsqrt(jnp.mean(x * x, axis=-1, keepdims=True) + EPS) * g_ref[...]
    hb = h.astype(jnp.bfloat16)
    n_a = wa_ref.shape[1]
    p_ref[:, :n_a] = jnp.dot(hb, wa_ref[...], preferred_element_type=jnp.float32).astype(p_ref.dtype)
    p_ref[:, n_a:] = jnp.dot(hb, wu_ref[...], preferred_element_type=jnp.float32).astype(p_ref.dtype)
    gt = lax.dot_general(wgt_ref[...], hb, NT_DIMS, preferred_element_type=jnp.float32)
    gt_ref[...] = gt[:N_GATES]


def _in_proj(x2, g1, w_a, w_u, wg_t):
    T = x2.shape[0]
    return pl.pallas_call(
        _in_proj_kernel,
        grid=(T // TM_PROJ,),
        in_specs=[
            pl.BlockSpec((TM_PROJ, D_MODEL), lambda i: (i, 0)),
            pl.BlockSpec((1, D_MODEL), lambda i: (0, 0)),
            pl.BlockSpec(w_a.shape, lambda i: (0, 0)),
            pl.BlockSpec(w_u.shape, lambda i: (0, 0)),
            pl.BlockSpec((BF16_SUBLANES, D_MODEL), lambda i: (0, 0)),
        ],
        out_specs=[
            pl.BlockSpec((TM_PROJ, N_MAIN), lambda i: (i, 0)),
            pl.BlockSpec((N_GATES, TM_PROJ), lambda i: (0, i)),
        ],
        out_shape=[
            jax.ShapeDtypeStruct((T, N_MAIN), jnp.bfloat16),
            jax.ShapeDtypeStruct((N_GATES, T), jnp.float32),
        ],
        compiler_params=pltpu.CompilerParams(
            dimension_semantics=("parallel",), vmem_limit_bytes=VMEM_LIMIT),
        name="in_proj",
    )(x2, g1, w_a, w_u, wg_t)


def _mlstm_kernel(qk_ref, qkp_ref, v_ref, o_ref, gt_ref, convw_ref, gb_ref, hng_ref,
                  tri_ref, shift_ref, hshift_ref, y_ref, cn_ref, m_ref):
    L = CHUNK
    c = pl.program_id(1)

    @pl.when(c == 0)
    def _():
        cn_ref[...] = jnp.zeros_like(cn_ref)
        m_ref[...] = jnp.zeros_like(m_ref)

    row_id = lax.broadcasted_iota(jnp.int32, (L, L), 0)
    col_id = lax.broadcasted_iota(jnp.int32, (L, L), 1)
    causal = col_id <= row_id
    ones_blk = jnp.ones((L, HEAD_DIM), jnp.bfloat16)
    lane = lax.broadcasted_iota(jnp.int32, (MLSTM_HEADS, L), 1)

    gate_terms = []
    for bb in range(MLSTM_BATCH):
        gt = gt_ref[bb] + gb_ref[...]
        f = gt[MLSTM_HEADS:]
        lf = jnp.minimum(f, 0.0) - jnp.log(1.0 + jnp.exp(-jnp.abs(f)))
        ig = gt[:MLSTM_HEADS]
        b_rows = lax.dot_general(lf, tri_ref[...], NT_DIMS, precision=lax.Precision.HIGHEST,
                                 preferred_element_type=jnp.float32)
        c_rows = ig - b_rows
        cm_rows = c_rows
        d = 1
        while d < L:
            cm_rows = jnp.maximum(
                cm_rows, jnp.where(lane >= d, pltpu.roll(cm_rows, d, axis=1), -jnp.inf))
            d *= 2
        gate_terms.append((b_rows, c_rows, cm_rows))

    conv_terms = []
    for bb in range(MLSTM_BATCH):
        x_cur = qk_ref[bb]
        x_prev = jnp.where(c > 0, qkp_ref[bb], jnp.zeros((HALO, 2 * MLSTM_WIDTH), jnp.bfloat16))
        acc = convw_ref[CONV_WIDTH - 1:CONV_WIDTH, :] * x_cur.astype(jnp.float32)
        for j in range(CONV_WIDTH - 1):
            sh = jnp.dot(shift_ref[j], x_cur, preferred_element_type=jnp.float32)
            top = sh[:8] + jnp.dot(hshift_ref[j], x_prev, preferred_element_type=jnp.float32)
            sh = jnp.concatenate([top, sh[8:]], axis=0)
            acc = acc + convw_ref[j:j + 1, :] * sh
        qk = acc * _sigmoid(acc)
        q_all = qk[:, :MLSTM_WIDTH].astype(jnp.bfloat16)
        k_t = jnp.transpose(qk[:, MLSTM_WIDTH:] * (HEAD_DIM ** -0.5))
        conv_terms.append((q_all, k_t))

    for bb in range(MLSTM_BATCH):
        b_rows, c_rows, cm_rows = gate_terms[bb]
        q_all, k_t = conv_terms[bb]
        m_in4 = jnp.concatenate(
            [m_ref[bb * MLSTM_HEADS + h][0:1, 0:1] for h in range(MLSTM_HEADS)], axis=0)
        mx_rows = jnp.maximum(cm_rows, m_in4)
        inter_rows = jnp.exp(m_in4 - mx_rows)
        einv_rows = jnp.exp(-(b_rows + mx_rows))
        fac_t = jnp.transpose(jnp.concatenate(
            [mx_rows, inter_rows, einv_rows, jnp.zeros_like(mx_rows)], axis=0))

        for h in range(MLSTM_HEADS):
            lo = h * HEAD_DIM
            st = bb * MLSTM_HEADS + h
            q = q_all[:, lo:lo + HEAD_DIM]
            kt = k_t[lo:lo + HEAD_DIM, :]
            v_ext = jnp.concatenate([v_ref[bb, :, lo:lo + HEAD_DIM], ones_blk], axis=1)
            mx_col = fac_t[:, h:h + 1]
            inter_col = fac_t[:, MLSTM_HEADS + h:MLSTM_HEADS + h + 1]
            einv_col = fac_t[:, 2 * MLSTM_HEADS + h:2 * MLSTM_HEADS + h + 1]
            c_row = c_rows[h:h + 1, :]
            b_tot = b_rows[h:h + 1, L - 1:L]
            cm_tot = cm_rows[h:h + 1, L - 1:L]
            m_in = m_ref[st][0:1, 0:1]
            cn = cn_ref[st]

            s_qk = jnp.dot(q, kt.astype(jnp.bfloat16), preferred_element_type=jnp.float32)
            s = (s_qk * jnp.exp(jnp.where(causal, c_row - mx_col, -jnp.inf))).astype(jnp.bfloat16)
            num = (jnp.dot(s, v_ext, preferred_element_type=jnp.float32)
                   + inter_col * jnp.dot(q, cn.astype(jnp.bfloat16),
                                         preferred_element_type=jnp.float32))
            den = num[:, HEAD_DIM:]
            hh = num[:, :HEAD_DIM] / jnp.maximum(jnp.abs(den), einv_col)

            mu = jnp.mean(hh, axis=-1, keepdims=True)
            dv = hh - mu
            var = jnp.mean(dv * dv, axis=-1, keepdims=True)
            hn = dv * lax.rsqrt(var + EPS) * hng_ref[:, lo:lo + HEAD_DIM]
            og = _sigmoid(o_ref[bb, :, lo:lo + HEAD_DIM].astype(jnp.float32))
            y_ref[bb, :, lo:lo + HEAD_DIM] = (og * hn).astype(y_ref.dtype)

            m_loc = b_tot + cm_tot
            kw_t = (kt * jnp.exp(c_row - cm_tot)).astype(jnp.bfloat16)
            c_loc = jnp.dot(kw_t, v_ext, preferred_element_type=jnp.float32)
            m_new = jnp.maximum(b_tot + m_in, m_loc)
            s_old = jnp.exp(b_tot + m_in - m_new)
            s_loc = jnp.exp(m_loc - m_new)
            cn_ref[st] = s_old * cn + s_loc * c_loc
            m_ref[st] = jnp.broadcast_to(m_new, m_ref.shape[1:])


def _mlstm(p3, gates_b, conv_w, gate_b, hn_g, tri, shifts, halo_shifts):
    batch, seq, _ = p3.shape
    L = CHUNK
    BB = MLSTM_BATCH
    halo_per_chunk = L // HALO
    return pl.pallas_call(
        _mlstm_kernel,
        grid=(batch // BB, seq // L),
        in_specs=[
            pl.BlockSpec((BB, L, 2 * MLSTM_WIDTH), lambda bi, ci: (bi, ci, 0)),
            pl.BlockSpec((BB, HALO, 2 * MLSTM_WIDTH),
                         lambda bi, ci: (bi, jnp.maximum(ci * halo_per_chunk - 1, 0), 0)),
            pl.BlockSpec((BB, L, MLSTM_WIDTH), lambda bi, ci: (bi, ci, 2)),
            pl.BlockSpec((BB, L, MLSTM_WIDTH), lambda bi, ci: (bi, ci, 3)),
            pl.BlockSpec((BB, N_GATES, L), lambda bi, ci: (bi, 0, ci)),
            pl.BlockSpec((CONV_WIDTH, 2 * MLSTM_WIDTH), lambda bi, ci: (0, 0)),
            pl.BlockSpec((N_GATES, 1), lambda bi, ci: (0, 0)),
            pl.BlockSpec((1, MLSTM_WIDTH), lambda bi, ci: (0, 0)),
            pl.BlockSpec((L, L), lambda bi, ci: (0, 0)),
            pl.BlockSpec((CONV_WIDTH - 1, L, L), lambda bi, ci: (0, 0, 0)),
            pl.BlockSpec((CONV_WIDTH - 1, 8, HALO), lambda bi, ci: (0, 0, 0)),
        ],
        out_specs=pl.BlockSpec((BB, L, MLSTM_WIDTH), lambda bi, ci: (bi, ci, 0)),
        out_shape=jax.ShapeDtypeStruct((batch, seq, MLSTM_WIDTH), jnp.bfloat16),
        scratch_shapes=[
            pltpu.VMEM((BB * MLSTM_HEADS, HEAD_DIM, 2 * HEAD_DIM), jnp.float32),
            pltpu.VMEM((BB * MLSTM_HEADS, 8, LANES), jnp.float32),
        ],
        compiler_params=pltpu.CompilerParams(
            dimension_semantics=("parallel", "arbitrary"), vmem_limit_bytes=VMEM_LIMIT),
        name="mlstm",
    )(p3, p3, p3, p3, gates_b, conv_w, gate_b, hn_g, tri, shifts, halo_shifts)


def _out_route_kernel(seq, x_ref, ym_ref, u_ref, up_ref, pw_ref, ps_ref, wo_ref, g2_ref,
                      wrt_ref, br_ref, x1_ref, h2_ref, idx_ref, gate_ref, rank_ref, cnt_ref,
                      carry_ref):
    TM = TM_PROJ
    R = ROUTE_SUB * TM
    i = pl.program_id(0)

    @pl.when(i == 0)
    def _():
        carry_ref[...] = jnp.zeros_like(carry_ref)

    pos0 = (i * R) % seq
    e_id = lax.broadcasted_iota(jnp.int32, (N_EXPERTS, TM), 0).astype(jnp.float32)
    t_row = lax.broadcasted_iota(jnp.int32, (TM, TM), 0)
    t_col = lax.broadcasted_iota(jnp.int32, (TM, TM), 1)
    before = jnp.where(t_row < t_col, 1.0, 0.0).astype(jnp.bfloat16)
    carry = carry_ref[...]
    subs = [slice(sub * TM, (sub + 1) * TM) for sub in range(ROUTE_SUB)]

    halo = jnp.where(pos0 > 0, up_ref[...].astype(jnp.float32), 0.0)
    u_ext = jnp.concatenate([halo, u_ref[...].astype(jnp.float32)], axis=0)
    win_sums = []
    for gi, w in enumerate(POOL_WINDOWS):
        sw = u_ext[:, gi * POOL_GROUP_DIM:(gi + 1) * POOL_GROUP_DIM]
        span = 1
        while span < w:
            sw = sw + pltpu.roll(sw, span, axis=0)
            span *= 2
        win_sums.append(sw)
    y_cats = []
    for sub, rows in enumerate(subs):
        r0 = sub * TM
        pos = (pos0 + r0 + lax.broadcasted_iota(jnp.int32, (TM, 1), 0) + 1).astype(jnp.float32)
        mixed = []
        for gi, w in enumerate(POOL_WINDOWS):
            lo = gi * POOL_GROUP_DIM
            tok = u_ext[HALO + r0:HALO + r0 + TM, lo:lo + POOL_GROUP_DIM]
            pooled = win_sums[gi][HALO + r0:HALO + r0 + TM] / jnp.minimum(pos, float(w)) - tok
            mg = jnp.dot(pooled.astype(jnp.bfloat16), pw_ref[gi],
                         preferred_element_type=jnp.float32)
            mixed.append((mg * ps_ref[:, lo:lo + POOL_GROUP_DIM]).astype(jnp.bfloat16))
        y_cats.append(jnp.concatenate([ym_ref[rows, :]] + mixed, axis=1))

    all_logits = []
    for sub, rows in enumerate(subs):
        r0 = sub * TM
        x1 = x_ref[rows, :] + jnp.dot(y_cats[sub], wo_ref[...], preferred_element_type=jnp.float32)
        x1_ref[rows, :] = x1
        h2 = x1 * lax.rsqrt(jnp.mean(x1 * x1, axis=-1, keepdims=True) + EPS) * g2_ref[...]
        h2b = h2.astype(jnp.bfloat16)
        h2w = _pack_bf16_pairs(h2)
        for s in range(PSLAB):
            h2_ref[pl.ds(r0 * PSLAB + s, TM, stride=PSLAB), :] = h2w[:, s * LANES:(s + 1) * LANES]
        all_logits.append(lax.dot_general(wrt_ref[...], h2b, NT_DIMS,
                                          preferred_element_type=jnp.float32) + br_ref[...])

    for sub, rows in enumerate(subs):
        work = all_logits[sub]
        vals, ids, hots = [], [], []
        for _ in range(TOP_K):
            mk = jnp.max(work, axis=0, keepdims=True)
            ik = jnp.min(jnp.where(work == mk, e_id, float(N_EXPERTS)), axis=0, keepdims=True)
            hot = e_id == ik
            work = jnp.where(hot, -jnp.inf, work)
            vals.append(mk)
            ids.append(ik)
            hots.append(hot)
        ex = [jnp.exp(vk - vals[0]) for vk in vals]
        denom = ex[0] + ex[1] + ex[2] + ex[3]
        gate_ref[:, rows] = jnp.concatenate([e / denom for e in ex], axis=0)
        idx_ref[:, rows] = jnp.concatenate(ids, axis=0).astype(jnp.int32)

        sel_f = sum(jnp.where(hot, 1.0, 0.0) for hot in hots)
        prefix = jnp.dot(sel_f.astype(jnp.bfloat16), before, preferred_element_type=jnp.float32)
        rank_e = carry[:, 0:1] + prefix
        ranks = [jnp.sum(jnp.where(hot, rank_e, 0.0), axis=0, keepdims=True) for hot in hots]
        rank_ref[:, rows] = jnp.concatenate(ranks, axis=0).astype(jnp.int32)
        carry = carry + jnp.sum(sel_f, axis=1, keepdims=True)
    carry_ref[...] = carry
    cnt_ref[...] = carry.astype(jnp.int32)


def _out_route(x2, ym, p, pool_w, pool_s, w_out, g2, wr_t, br, seq):
    T = x2.shape[0]
    TM = ROUTE_SUB * TM_PROJ
    nt = T // TM
    u_blk = N_MAIN // POOL_WIDTH - 1
    halo_per_tile = TM // HALO
    tok_spec = pl.BlockSpec((TOP_K, TM), lambda i: (0, i))
    return pl.pallas_call(
        functools.partial(_out_route_kernel, seq),
        grid=(nt,),
        in_specs=[
            pl.BlockSpec((TM, D_MODEL), lambda i: (i, 0)),
            pl.BlockSpec((TM, MLSTM_WIDTH), lambda i: (i, 0)),
            pl.BlockSpec((TM, POOL_WIDTH), lambda i: (i, u_blk)),
            pl.BlockSpec((HALO, POOL_WIDTH),
                         lambda i: (jnp.maximum(i * halo_per_tile - 1, 0), u_blk)),
            pl.BlockSpec((len(POOL_WINDOWS), POOL_GROUP_DIM, POOL_GROUP_DIM), lambda i: (0, 0, 0)),
            pl.BlockSpec((1, POOL_WIDTH), lambda i: (0, 0)),
            pl.BlockSpec((D_MODEL, D_MODEL), lambda i: (0, 0)),
            pl.BlockSpec((1, D_MODEL), lambda i: (0, 0)),
            pl.BlockSpec((N_EXPERTS, D_MODEL), lambda i: (0, 0)),
            pl.BlockSpec((N_EXPERTS, 1), lambda i: (0, 0)),
        ],
        out_specs=[
            pl.BlockSpec((TM, D_MODEL), lambda i: (i, 0)),
            pl.BlockSpec((TM * PSLAB, LANES), lambda i: (i, 0)),
            tok_spec, tok_spec, tok_spec,
            pl.BlockSpec((N_EXPERTS, LANES), lambda i: (0, 0)),
        ],
        out_shape=[
            jax.ShapeDtypeStruct((T, D_MODEL), jnp.float32),
            jax.ShapeDtypeStruct((T * PSLAB, LANES), jnp.uint32),
            jax.ShapeDtypeStruct((TOP_K, T), jnp.int32),
            jax.ShapeDtypeStruct((TOP_K, T), jnp.float32),
            jax.ShapeDtypeStruct((TOP_K, T), jnp.int32),
            jax.ShapeDtypeStruct((N_EXPERTS, LANES), jnp.int32),
        ],
        scratch_shapes=[
            pltpu.VMEM((N_EXPERTS, LANES), jnp.float32),
        ],
        compiler_params=pltpu.CompilerParams(
            dimension_semantics=("arbitrary",), vmem_limit_bytes=VMEM_LIMIT),
        name="out_route",
    )(x2, ym, p, p, pool_w, pool_s, w_out, g2, wr_t, br)


def _plan(dest_flat, fill):
    n_assign = dest_flat.shape[0]
    n_table = fill.shape[0]
    mesh = plsc.VectorSubcoreMesh(core_axis_name="c", subcore_axis_name="s")

    @pl.kernel(out_type=jax.ShapeDtypeStruct((n_table,), jnp.int32), mesh=mesh,
               scratch_types=[pltpu.VMEM((n_table,), jnp.int32),
                              pltpu.VMEM((PLAN_CHUNK,), jnp.int32)],
               compiler_params=pltpu.CompilerParams(needs_layout_passes=False))
    def plan_kernel(dest_hbm, fill_hbm, out_hbm, table, chunk):
        first = jnp.logical_and(lax.axis_index("c") == 0, lax.axis_index("s") == 0)

        @pl.when(first)
        def _():
            pltpu.sync_copy(fill_hbm, table)

            @pl.loop(0, n_assign // PLAN_CHUNK)
            def _(ci):
                pltpu.sync_copy(dest_hbm.at[pl.ds(ci * PLAN_CHUNK, PLAN_CHUNK)], chunk)

                @pl.loop(0, PLAN_CHUNK // (SC_LANES * PLAN_UNROLL))
                def _(i):
                    for j in range(PLAN_UNROLL):
                        off = (i * PLAN_UNROLL + j) * SC_LANES
                        idx = chunk[pl.ds(off, SC_LANES)]
                        vals = (ci * PLAN_CHUNK + off
                                + lax.broadcasted_iota(jnp.int32, (SC_LANES,), 0))
                        plsc.store_scatter(table, [idx], vals)

            pltpu.sync_copy(table, out_hbm)

    return plan_kernel(dest_flat, fill)


def _expert_kernel(n_tok, bs_ref, slot_ref, h2_ref, wg_ref, bg_ref, wu_ref, bu_ref, wd_ref, bd_ref,
                   yt_ref, *scratch):
    TM = TM_EXPERT
    ROWS = TM * PSLAB
    e = pl.program_id(0)
    n_total = bs_ref[N_EXPERTS]
    xg = scratch[:NBUF]
    ys = scratch[NBUF:2 * NBUF]
    wgu_ref, wdb_ref, gsem, ssem = scratch[2 * NBUF:]

    def token_of(a):
        return a & (n_tok - 1) if n_tok & (n_tok - 1) == 0 else lax.rem(a, n_tok)

    def start_gather(blk, par):
        base = (blk + 1) * TM
        for r in range(TM):
            t = token_of(slot_ref[base + r])
            pltpu.make_async_copy(h2_ref.at[pl.ds(pl.multiple_of(t * PSLAB, PSLAB), PSLAB), :],
                                  xg[par].at[pl.ds(r * PSLAB, PSLAB), :], gsem.at[par]
                                  ).start(priority=ROW_DMA_PRIORITY)

    def wait_gather(par):
        pltpu.make_async_copy(h2_ref.at[pl.ds(0, ROWS), :], xg[0], gsem.at[par]).wait()

    def start_scatter(blk, par):
        base = (blk + 1) * TM
        for r in range(TM):
            a = slot_ref[base + r]
            pltpu.make_async_copy(ys[par].at[pl.ds(r * PSLAB, PSLAB), :],
                                  yt_ref.at[pl.ds(pl.multiple_of(a * PSLAB, PSLAB), PSLAB), :],
                                  ssem.at[par]).start(priority=ROW_DMA_PRIORITY)

    def wait_scatter(par):
        pltpu.make_async_copy(ys[0], yt_ref.at[pl.ds(0, ROWS), :], ssem.at[par]).wait()

    @pl.when(e == 0)
    def _():
        for blk in range(NBUF - 1):
            start_gather(blk, blk)
        for par in range(NBUF):
            ys[par][...] = jnp.zeros_like(ys[par])
            dump = yt_ref.at[pl.ds((n_tok * TOP_K + par * TM) * PSLAB, ROWS), :]
            cp = pltpu.make_async_copy(ys[par], dump, ssem.at[par])
            cp.start()
            cp.wait()

    for c in range(D_FF // GU_CHUNK):
        cols = slice(c * GU_CHUNK, (c + 1) * GU_CHUNK)
        wgu_ref[:, 2 * c * GU_CHUNK:(2 * c + 1) * GU_CHUNK] = wg_ref[0, :, cols].astype(jnp.bfloat16)
        wgu_ref[:, (2 * c + 1) * GU_CHUNK:(2 * c + 2) * GU_CHUNK] = (
            wu_ref[0, :, cols].astype(jnp.bfloat16))
    wdb_ref[...] = wd_ref[0].astype(jnp.bfloat16)

    def block_step(g, par):
        prv = (par + NBUF - 1) % NBUF
        wait_gather(par)

        @pl.when(g >= NBUF - 1)
        def _():
            wait_scatter(par)

        start_gather(g + NBUF - 1, prv)
        start_scatter(g - 1, prv)
        words = [xg[par][pl.ds(s, TM, stride=PSLAB), :] for s in range(PSLAB)]
        x = jnp.concatenate([_unpack_lo(w).astype(jnp.bfloat16) for w in words]
                            + [_unpack_hi(w).astype(jnp.bfloat16) for w in words], axis=1)
        gu = jnp.dot(x, wgu_ref[...], preferred_element_type=jnp.float32)
        acts = []
        for c in range(D_FF // GU_CHUNK):
            cols = slice(c * GU_CHUNK, (c + 1) * GU_CHUNK)
            gate = gu[:, 2 * c * GU_CHUNK:(2 * c + 1) * GU_CHUNK] + bg_ref[0][:, cols]
            up = gu[:, (2 * c + 1) * GU_CHUNK:(2 * c + 2) * GU_CHUNK] + bu_ref[0][:, cols]
            gate = jnp.minimum(gate, SWIGLU_LIMIT)
            up = jnp.clip(up, -SWIGLU_LIMIT, SWIGLU_LIMIT)
            glu = gate * _sigmoid(SWIGLU_ALPHA * gate)
            acts.append((glu * (up + 1.0)).astype(jnp.bfloat16))
        act = jnp.concatenate(acts, axis=1)
        y = jnp.dot(act, wdb_ref[...], preferred_element_type=jnp.float32) + bd_ref[0]
        packed = _pack_bf16_pairs(y)
        for s in range(PSLAB):
            ys[par][pl.ds(s, TM, stride=PSLAB), :] = packed[:, s * LANES:(s + 1) * LANES]

    def body(g, carry):
        for par in range(NBUF):
            pl.when(g % NBUF == par)(functools.partial(block_step, g, par))
        return carry

    lax.fori_loop(bs_ref[e], bs_ref[e + 1], body, 0)

    @pl.when(e == N_EXPERTS - 1)
    def _():
        g = n_total
        for par in range(NBUF):
            @pl.when((g - 1) % NBUF == par)
            def _():
                start_scatter(g - 1, par)
        for j in range(NBUF - 1):
            wait_gather((g + j) % NBUF)
        wait_scatter((g - 1) % NBUF)
        for j in range(2, NBUF + 1):
            @pl.when(g >= j - 1)
            def _():
                wait_scatter((g + NBUF - j) % NBUF)


def _experts(block_start, slot_buf, h2_slab, w_gate, b_gate, w_up, b_up, w_down, b_down, n_tok):
    TM = TM_EXPERT
    n_assign = n_tok * TOP_K
    w_spec = pl.BlockSpec((1, D_MODEL, D_FF), lambda e, bs, sl: (e, 0, 0))
    bias_spec = pl.BlockSpec((1, 1, D_FF), lambda e, bs, sl: (e, 0, 0))
    buf = pltpu.VMEM((TM * PSLAB, LANES), jnp.uint32)
    grid_spec = pltpu.PrefetchScalarGridSpec(
        num_scalar_prefetch=2,
        grid=(N_EXPERTS,),
        in_specs=[
            pl.BlockSpec(memory_space=pl.ANY),
            w_spec, bias_spec, w_spec, bias_spec, w_spec, bias_spec,
        ],
        out_specs=pl.BlockSpec(memory_space=pl.ANY),
        scratch_shapes=[
            *([buf] * (2 * NBUF)),
            pltpu.VMEM((D_MODEL, 2 * D_FF), jnp.bfloat16),
            pltpu.VMEM((D_FF, D_MODEL), jnp.bfloat16),
            pltpu.SemaphoreType.DMA((NBUF,)),
            pltpu.SemaphoreType.DMA((NBUF,)),
        ],
    )
    return pl.pallas_call(
        functools.partial(_expert_kernel, n_tok),
        grid_spec=grid_spec,
        out_shape=jax.ShapeDtypeStruct(((n_assign + NBUF * TM) * PSLAB, LANES), jnp.uint32),
        compiler_params=pltpu.CompilerParams(
            dimension_semantics=("arbitrary",), vmem_limit_bytes=VMEM_LIMIT),
        name="experts",
    )(block_start, slot_buf, h2_slab, w_gate, b_gate, w_up, b_up, w_down, b_down)


def _combine_kernel(normalize, x1_ref, y0_ref, y1_ref, y2_ref, y3_ref, gate_ref, g_ref, o_ref):
    TM = TM_PROJ
    gates = jnp.concatenate([gate_ref[...], jnp.zeros((8 - TOP_K, TM), jnp.float32)], axis=0)
    g_cols = jnp.transpose(gates)
    g_bc = [jnp.broadcast_to(g_cols[:, k:k + 1], (TM, LANES)) for k in range(TOP_K)]
    ssq = jnp.zeros((TM, LANES), jnp.float32)
    parts = [x1_ref[:, s * LANES:(s + 1) * LANES] for s in range(SLAB)]
    for s in range(PSLAB):
        for k, y_ref in enumerate((y0_ref, y1_ref, y2_ref, y3_ref)):
            w = y_ref[pl.ds(s, TM, stride=PSLAB), :]
            parts[s] = parts[s] + g_bc[k] * _unpack_lo(w)
            parts[PSLAB + s] = parts[PSLAB + s] + g_bc[k] * _unpack_hi(w)
    for acc in parts:
        ssq = ssq + acc * acc
    if normalize:
        inv = lax.rsqrt(jnp.sum(ssq, axis=-1, keepdims=True) * (1.0 / D_MODEL) + EPS)
        for s in range(SLAB):
            o_ref[:, s * LANES:(s + 1) * LANES] = parts[s] * inv * g_ref[:, s * LANES:(s + 1) * LANES]
    else:
        for s in range(SLAB):
            o_ref[:, s * LANES:(s + 1) * LANES] = parts[s]


def _combine(x1, y_tok, gate_t, gf, normalize):
    T = x1.shape[0]
    TM = TM_PROJ
    nt = T // TM

    def y_spec(k):
        return pl.BlockSpec((TM * PSLAB, LANES), lambda i: (k * nt + i, 0))

    return pl.pallas_call(
        functools.partial(_combine_kernel, normalize),
        grid=(nt,),
        in_specs=[
            pl.BlockSpec((TM, D_MODEL), lambda i: (i, 0)),
            y_spec(0), y_spec(1), y_spec(2), y_spec(3),
            pl.BlockSpec((TOP_K, TM), lambda i: (0, i)),
            pl.BlockSpec((1, D_MODEL), lambda i: (0, 0)),
        ],
        out_specs=pl.BlockSpec((TM, D_MODEL), lambda i: (i, 0)),
        out_shape=jax.ShapeDtypeStruct((T, D_MODEL), jnp.float32),
        compiler_params=pltpu.CompilerParams(
            dimension_semantics=("parallel",), vmem_limit_bytes=VMEM_LIMIT),
        name="combine",
    )(x1, y_tok, y_tok, y_tok, y_tok, gate_t, gf)


def kernel(x, norm1_g, w_in, ig_b, fg_b, conv_w, head_norm_g, pool_w, pool_scale, w_out, norm2_g,
           w_router, b_router, w_gate, b_gate, w_up, b_up, w_down, b_down, normf_g):
    B, S, D = x.shape
    T = B * S
    depth = norm1_g.shape[0]
    W = MLSTM_WIDTH
    f32, bf16 = jnp.float32, jnp.bfloat16

    L = CHUNK
    t_l = lax.broadcasted_iota(jnp.int32, (L, L), 0)
    t_r = lax.broadcasted_iota(jnp.int32, (L, L), 1)
    tri = (t_r <= t_l).astype(f32)
    shifts = jnp.stack([(t_l - t_r == CONV_WIDTH - 1 - j).astype(bf16)
                        for j in range(CONV_WIDTH - 1)])
    h_t = lax.broadcasted_iota(jnp.int32, (8, HALO), 0)
    h_r = lax.broadcasted_iota(jnp.int32, (8, HALO), 1)
    halo_shifts = jnp.stack([(h_r - HALO - h_t == -(CONV_WIDTH - 1 - j)).astype(bf16)
                             for j in range(CONV_WIDTH - 1)])

    n_assign = T * TOP_K
    n_blocks = -(-n_assign // TM_EXPERT) + N_EXPERTS
    n_rows = n_blocks * TM_EXPERT
    n_table = n_rows + NBUF * TM_EXPERT
    fill = n_assign + ((jnp.arange(n_table, dtype=jnp.int32) + (NBUF - 1) * TM_EXPERT)
                       % (NBUF * TM_EXPERT))
    x2 = x.reshape(T, D)
    for l in range(depth):
        w = w_in[l]
        w_a = w[:, :4 * W].astype(bf16)
        w_u = w[:, 4 * W + N_GATES:].astype(bf16)
        wg_t = jnp.zeros((BF16_SUBLANES, D), bf16).at[:N_GATES].set(
            w[:, 4 * W:4 * W + N_GATES].T.astype(bf16))
        p, gates_t = _in_proj(x2, norm1_g[l][None, :], w_a, w_u, wg_t)

        gate_b = jnp.concatenate([ig_b[l], fg_b[l]])[:, None].astype(f32)
        gates_b = gates_t.reshape(N_GATES, B, S).transpose(1, 0, 2)
        ym = _mlstm(p.reshape(B, S, N_MAIN), gates_b, conv_w[l].astype(f32), gate_b,
                    head_norm_g[l][None, :], tri, shifts, halo_shifts).reshape(T, W)

        x1, h2, idx_t, gate_t, rank_t, cnt = _out_route(
            x2, ym, p, pool_w[l].astype(bf16), pool_scale[l][None, :], w_out[l].astype(bf16),
            norm2_g[l][None, :], w_router[l].T.astype(bf16), b_router[l][:, None], S)

        counts = cnt[:, 0]
        padded = ((counts + TM_EXPERT - 1) // TM_EXPERT) * TM_EXPERT
        padded_end = jnp.cumsum(padded)
        padded_start = padded_end - padded
        expert_ids = jnp.arange(N_EXPERTS, dtype=jnp.int32)[:, None, None]
        start_of = jnp.sum(jnp.where(idx_t[None] == expert_ids, padded_start[:, None, None], 0), axis=0)
        dest = start_of + rank_t
        block_start = jnp.concatenate(
            [jnp.zeros((1,), jnp.int32), (padded_end // TM_EXPERT).astype(jnp.int32)])

        slot_buf = _plan(dest.reshape(-1) + TM_EXPERT, fill)
        y_tok = _experts(block_start, slot_buf, h2, w_gate[l], b_gate[l][:, None, :],
                         w_up[l], b_up[l][:, None, :], w_down[l], b_down[l][:, None, :], T)
        last = l + 1 == depth
        x2 = _combine(x1, y_tok, gate_t, normf_g[None, :], last)
    return x2.reshape(B, S, D)
```

```python
import functools

import jax
import jax.numpy as jnp
from jax import lax
from jax.experimental import pallas as pl
from jax.experimental.pallas import tpu as pltpu
from jax.experimental.pallas import tpu_sc as plsc

D_MODEL = 1024
MLSTM_WIDTH = 512
MLSTM_HEADS = 4
HEAD_DIM = 128
CONV_WIDTH = 4
POOL_WIDTH = 512
POOL_WINDOWS = (2, 4, 8, 16)
POOL_GROUP_DIM = 128
N_EXPERTS = 32
TOP_K = 4
D_FF = 1024
SWIGLU_LIMIT = 7.0
SWIGLU_ALPHA = 1.702
EPS = 1e-5

N_MAIN = 4 * MLSTM_WIDTH + POOL_WIDTH
N_GATES = 2 * MLSTM_HEADS

LANES = 128
BF16_SUBLANES = 16
VMEM_LIMIT = 56 * 1024 * 1024

TM_PROJ = 512
ROUTE_SUB = 2
CHUNK = 256
MLSTM_BATCH = 2
HALO = 16
TM_EXPERT = 256
NBUF = 4
DOT_COLS = 512
ROW_DMA_PRIORITY = 1
SLAB = D_MODEL // LANES
PSLAB = SLAB // 2
PLAN_CHUNK = 8192
SC_LANES = 16
PLAN_UNROLL = 8

NT_DIMS = (((1,), (1,)), ((), ()))


def _sigmoid(x):
    return 1.0 / (1.0 + jnp.exp(-x))


def _pack_bf16_pairs(v):
    half = v.shape[1] // 2
    lo = pltpu.bitcast(v[:, :half].astype(jnp.bfloat16).astype(jnp.float32), jnp.uint32)
    hi = pltpu.bitcast(v[:, half:].astype(jnp.bfloat16).astype(jnp.float32), jnp.uint32)
    return (lo >> 16) | (hi & jnp.uint32(0xFFFF0000))


def _unpack_lo(w):
    return pltpu.bitcast(w << 16, jnp.float32)


def _unpack_hi(w):
    return pltpu.bitcast(w & jnp.uint32(0xFFFF0000), jnp.float32)


def _in_proj_kernel(x_ref, g_ref, wa_ref, wu_ref, wgt_ref, p_ref, gt_ref):
    x = x_ref[...]
    h = x * lax.rsqrt(jnp.mean(x * x, axis=-1, keepdims=True) + EPS) * g_ref[...]
    hb = h.astype(jnp.bfloat16)
    n_a = wa_ref.shape[1]
    p_ref[:, :n_a] = jnp.dot(hb, wa_ref[...], preferred_element_type=jnp.float32).astype(p_ref.dtype)
    p_ref[:, n_a:] = jnp.dot(hb, wu_ref[...], preferred_element_type=jnp.float32).astype(p_ref.dtype)
    gt = lax.dot_general(wgt_ref[...], hb, NT_DIMS, preferred_element_type=jnp.float32)
    gt_ref[...] = gt[:N_GATES]


def _in_proj(x2, g1, w_a, w_u, wg_t):
    T = x2.shape[0]
    return pl.pallas_call(
        _in_proj_kernel,
        grid=(T // TM_PROJ,),
        in_specs=[
            pl.BlockSpec((TM_PROJ, D_MODEL), lambda i: (i, 0)),
            pl.BlockSpec((1, D_MODEL), lambda i: (0, 0)),
            pl.BlockSpec(w_a.shape, lambda i: (0, 0)),
            pl.BlockSpec(w_u.shape, lambda i: (0, 0)),
            pl.BlockSpec((BF16_SUBLANES, D_MODEL), lambda i: (0, 0)),
        ],
        out_specs=[
            pl.BlockSpec((TM_PROJ, N_MAIN), lambda i: (i, 0)),
            pl.BlockSpec((N_GATES, TM_PROJ), lambda i: (0, i)),
        ],
        out_shape=[
            jax.ShapeDtypeStruct((T, N_MAIN), jnp.bfloat16),
            jax.ShapeDtypeStruct((N_GATES, T), jnp.float32),
        ],
        compiler_params=pltpu.CompilerParams(
            dimension_semantics=("parallel",), vmem_limit_bytes=VMEM_LIMIT),
        name="in_proj",
    )(x2, g1, w_a, w_u, wg_t)


def _mlstm_kernel(qk_ref, qkp_ref, v_ref, o_ref, gt_ref, convw_ref, gb_ref, hng_ref,
                  tri_ref, shift_ref, hshift_ref, y_ref, cn_ref, m_ref):
    L = CHUNK
    c = pl.program_id(1)

    @pl.when(c == 0)
    def _():
        cn_ref[...] = jnp.zeros_like(cn_ref)
        m_ref[...] = jnp.zeros_like(m_ref)

    row_id = lax.broadcasted_iota(jnp.int32, (L, L), 0)
    col_id = lax.broadcasted_iota(jnp.int32, (L, L), 1)
    causal = col_id <= row_id
    ones_blk = jnp.ones((L, HEAD_DIM), jnp.bfloat16)
    lane = lax.broadcasted_iota(jnp.int32, (MLSTM_HEADS, L), 1)

    gate_terms = []
    for bb in range(MLSTM_BATCH):
        gt = gt_ref[bb] + gb_ref[...]
        f = gt[MLSTM_HEADS:]
        lf = jnp.minimum(f, 0.0) - jnp.log(1.0 + jnp.exp(-jnp.abs(f)))
        ig = gt[:MLSTM_HEADS]
        b_rows = lax.dot_general(lf, tri_ref[...], NT_DIMS, precision=lax.Precision.HIGHEST,
                                 preferred_element_type=jnp.float32)
        c_rows = ig - b_rows
        cm_rows = c_rows
        d = 1
        while d < L:
            cm_rows = jnp.maximum(
                cm_rows, jnp.where(lane >= d, pltpu.roll(cm_rows, d, axis=1), -jnp.inf))
            d *= 2
        gate_terms.append((b_rows, c_rows, cm_rows))

    conv_terms = []
    for bb in range(MLSTM_BATCH):
        x_cur = qk_ref[bb]
        x_prev = jnp.where(c > 0, qkp_ref[bb], jnp.zeros((HALO, 2 * MLSTM_WIDTH), jnp.bfloat16))
        acc = convw_ref[CONV_WIDTH - 1:CONV_WIDTH, :] * x_cur.astype(jnp.float32)
        for j in range(CONV_WIDTH - 1):
            sh = jnp.dot(shift_ref[j], x_cur, preferred_element_type=jnp.float32)
            top = sh[:8] + jnp.dot(hshift_ref[j], x_prev, preferred_element_type=jnp.float32)
            sh = jnp.concatenate([top, sh[8:]], axis=0)
            acc = acc + convw_ref[j:j + 1, :] * sh
        qk = acc * _sigmoid(acc)
        q_all = qk[:, :MLSTM_WIDTH].astype(jnp.bfloat16)
        k_t = jnp.transpose(qk[:, MLSTM_WIDTH:] * (HEAD_DIM ** -0.5))
        conv_terms.append((q_all, k_t))

    for bb in range(MLSTM_BATCH):
        b_rows, c_rows, cm_rows = gate_terms[bb]
        q_all, k_t = conv_terms[bb]
        m_in4 = jnp.concatenate(
            [m_ref[bb * MLSTM_HEADS + h][0:1, 0:1] for h in range(MLSTM_HEADS)], axis=0)
        mx_rows = jnp.maximum(cm_rows, m_in4)
        inter_rows = jnp.exp(m_in4 - mx_rows)
        einv_rows = jnp.exp(-(b_rows + mx_rows))
        fac_t = jnp.transpose(jnp.concatenate(
            [mx_rows, inter_rows, einv_rows, jnp.zeros_like(mx_rows)], axis=0))

        for h in range(MLSTM_HEADS):
            lo = h * HEAD_DIM
            st = bb * MLSTM_HEADS + h
            q = q_all[:, lo:lo + HEAD_DIM]
            kt = k_t[lo:lo + HEAD_DIM, :]
            v_ext = jnp.concatenate([v_ref[bb, :, lo:lo + HEAD_DIM], ones_blk], axis=1)
            mx_col = fac_t[:, h:h + 1]
            inter_col = fac_t[:, MLSTM_HEADS + h:MLSTM_HEADS + h + 1]
            einv_col = fac_t[:, 2 * MLSTM_HEADS + h:2 * MLSTM_HEADS + h + 1]
            c_row = c_rows[h:h + 1, :]
            b_tot = b_rows[h:h + 1, L - 1:L]
            cm_tot = cm_rows[h:h + 1, L - 1:L]
            m_in = m_ref[st][0:1, 0:1]
            cn = cn_ref[st]

            s_qk = jnp.dot(q, kt.astype(jnp.bfloat16), preferred_element_type=jnp.float32)
            s = (s_qk * jnp.exp(jnp.where(causal, c_row - mx_col, -jnp.inf))).astype(jnp.bfloat16)
            num = (jnp.dot(s, v_ext, preferred_element_type=jnp.float32)
                   + inter_col * jnp.dot(q, cn.astype(jnp.bfloat16),
                                         preferred_element_type=jnp.float32))
            den = num[:, HEAD_DIM:]
            hh = num[:, :HEAD_DIM] / jnp.maximum(jnp.abs(den), einv_col)

            mu = jnp.mean(hh, axis=-1, keepdims=True)
            dv = hh - mu
            var = jnp.mean(dv * dv, axis=-1, keepdims=True)
            hn = dv * lax.rsqrt(var + EPS) * hng_ref[:, lo:lo + HEAD_DIM]
            og = _sigmoid(o_ref[bb, :, lo:lo + HEAD_DIM].astype(jnp.float32))
            y_ref[bb, :, lo:lo + HEAD_DIM] = (og * hn).astype(y_ref.dtype)

            m_loc = b_tot + cm_tot
            kw_t = (kt * jnp.exp(c_row - cm_tot)).astype(jnp.bfloat16)
            c_loc = jnp.dot(kw_t, v_ext, preferred_element_type=jnp.float32)
            m_new = jnp.maximum(b_tot + m_in, m_loc)
            s_old = jnp.exp(b_tot + m_in - m_new)
            s_loc = jnp.exp(m_loc - m_new)
            cn_ref[st] = s_old * cn + s_loc * c_loc
            m_ref[st] = jnp.broadcast_to(m_new, m_ref.shape[1:])


def _mlstm(p3, gates_b, conv_w, gate_b, hn_g, tri, shifts, halo_shifts):
    batch, seq, _ = p3.shape
    L = CHUNK
    BB = MLSTM_BATCH
    halo_per_chunk = L // HALO
    return pl.pallas_call(
        _mlstm_kernel,
        grid=(batch // BB, seq // L),
        in_specs=[
            pl.BlockSpec((BB, L, 2 * MLSTM_WIDTH), lambda bi, ci: (bi, ci, 0)),
            pl.BlockSpec((BB, HALO, 2 * MLSTM_WIDTH),
                         lambda bi, ci: (bi, jnp.maximum(ci * halo_per_chunk - 1, 0), 0)),
            pl.BlockSpec((BB, L, MLSTM_WIDTH), lambda bi, ci: (bi, ci, 2)),
            pl.BlockSpec((BB, L, MLSTM_WIDTH), lambda bi, ci: (bi, ci, 3)),
            pl.BlockSpec((BB, N_GATES, L), lambda bi, ci: (bi, 0, ci)),
            pl.BlockSpec((CONV_WIDTH, 2 * MLSTM_WIDTH), lambda bi, ci: (0, 0)),
            pl.BlockSpec((N_GATES, 1), lambda bi, ci: (0, 0)),
            pl.BlockSpec((1, MLSTM_WIDTH), lambda bi, ci: (0, 0)),
            pl.BlockSpec((L, L), lambda bi, ci: (0, 0)),
            pl.BlockSpec((CONV_WIDTH - 1, L, L), lambda bi, ci: (0, 0, 0)),
            pl.BlockSpec((CONV_WIDTH - 1, 8, HALO), lambda bi, ci: (0, 0, 0)),
        ],
        out_specs=pl.BlockSpec((BB, L, MLSTM_WIDTH), lambda bi, ci: (bi, ci, 0)),
        out_shape=jax.ShapeDtypeStruct((batch, seq, MLSTM_WIDTH), jnp.bfloat16),
        scratch_shapes=[
            pltpu.VMEM((BB * MLSTM_HEADS, HEAD_DIM, 2 * HEAD_DIM), jnp.float32),
            pltpu.VMEM((BB * MLSTM_HEADS, 8, LANES), jnp.float32),
        ],
        compiler_params=pltpu.CompilerParams(
            dimension_semantics=("parallel", "arbitrary"), vmem_limit_bytes=VMEM_LIMIT),
        name="mlstm",
    )(p3, p3, p3, p3, gates_b, conv_w, gate_b, hn_g, tri, shifts, halo_shifts)


def _out_route_kernel(seq, x_ref, ym_ref, u_ref, up_ref, pw_ref, ps_ref, wo_ref, g2_ref,
                      wrt_ref, br_ref, x1_ref, h2_ref, idx_ref, gate_ref, rank_ref, cnt_ref,
                      carry_ref):
    TM = TM_PROJ
    R = ROUTE_SUB * TM
    i = pl.program_id(0)

    @pl.when(i == 0)
    def _():
        carry_ref[...] = jnp.zeros_like(carry_ref)

    pos0 = (i * R) % seq
    e_id = lax.broadcasted_iota(jnp.int32, (N_EXPERTS, TM), 0).astype(jnp.float32)
    t_row = lax.broadcasted_iota(jnp.int32, (TM, TM), 0)
    t_col = lax.broadcasted_iota(jnp.int32, (TM, TM), 1)
    before = jnp.where(t_row < t_col, 1.0, 0.0).astype(jnp.bfloat16)
    carry = carry_ref[...]
    subs = [slice(sub * TM, (sub + 1) * TM) for sub in range(ROUTE_SUB)]

    halo = jnp.where(pos0 > 0, up_ref[...].astype(jnp.float32), 0.0)
    u_ext = jnp.concatenate([halo, u_ref[...].astype(jnp.float32)], axis=0)
    win_sums = []
    for gi, w in enumerate(POOL_WINDOWS):
        sw = u_ext[:, gi * POOL_GROUP_DIM:(gi + 1) * POOL_GROUP_DIM]
        span = 1
        while span < w:
            sw = sw + pltpu.roll(sw, span, axis=0)
            span *= 2
        win_sums.append(sw)
    y_cats = []
    for sub, rows in enumerate(subs):
        r0 = sub * TM
        pos = (pos0 + r0 + lax.broadcasted_iota(jnp.int32, (TM, 1), 0) + 1).astype(jnp.float32)
        mixed = []
        for gi, w in enumerate(POOL_WINDOWS):
            lo = gi * POOL_GROUP_DIM
            tok = u_ext[HALO + r0:HALO + r0 + TM, lo:lo + POOL_GROUP_DIM]
            pooled = win_sums[gi][HALO + r0:HALO + r0 + TM] / jnp.minimum(pos, float(w)) - tok
            mg = jnp.dot(pooled.astype(jnp.bfloat16), pw_ref[gi],
                         preferred_element_type=jnp.float32)
            mixed.append((mg * ps_ref[:, lo:lo + POOL_GROUP_DIM]).astype(jnp.bfloat16))
        y_cats.append(jnp.concatenate([ym_ref[rows, :]] + mixed, axis=1))

    all_logits = []
    for sub, rows in enumerate(subs):
        r0 = sub * TM
        x1 = x_ref[rows, :] + jnp.dot(y_cats[sub], wo_ref[...], preferred_element_type=jnp.float32)
        x1_ref[rows, :] = x1
        h2 = x1 * lax.rsqrt(jnp.mean(x1 * x1, axis=-1, keepdims=True) + EPS) * g2_ref[...]
        h2b = h2.astype(jnp.bfloat16)
        h2w = _pack_bf16_pairs(h2)
        for s in range(PSLAB):
            h2_ref[pl.ds(r0 * PSLAB + s, TM, stride=PSLAB), :] = h2w[:, s * LANES:(s + 1) * LANES]
        all_logits.append(lax.dot_general(wrt_ref[...], h2b, NT_DIMS,
                                          preferred_element_type=jnp.float32) + br_ref[...])

    for sub, rows in enumerate(subs):
        work = all_logits[sub]
        vals, ids, hots = [], [], []
        for _ in range(TOP_K):
            mk = jnp.max(work, axis=0, keepdims=True)
            ik = jnp.min(jnp.where(work == mk, e_id, float(N_EXPERTS)), axis=0, keepdims=True)
            hot = e_id == ik
            work = jnp.where(hot, -jnp.inf, work)
            vals.append(mk)
            ids.append(ik)
            hots.append(hot)
        ex = [jnp.exp(vk - vals[0]) for vk in vals]
        denom = ex[0] + ex[1] + ex[2] + ex[3]
        gate_ref[:, rows] = jnp.concatenate([e / denom for e in ex], axis=0)
        idx_ref[:, rows] = jnp.concatenate(ids, axis=0).astype(jnp.int32)

        sel_f = sum(jnp.where(hot, 1.0, 0.0) for hot in hots)
        prefix = jnp.dot(sel_f.astype(jnp.bfloat16), before, preferred_element_type=jnp.float32)
        rank_e = carry[:, 0:1] + prefix
        ranks = [jnp.sum(jnp.where(hot, rank_e, 0.0), axis=0, keepdims=True) for hot in hots]
        rank_ref[:, rows] = jnp.concatenate(ranks, axis=0).astype(jnp.int32)
        carry = carry + jnp.sum(sel_f, axis=1, keepdims=True)
    carry_ref[...] = carry
    cnt_ref[...] = carry.astype(jnp.int32)


def _out_route(x2, ym, p, pool_w, pool_s, w_out, g2, wr_t, br, seq):
    T = x2.shape[0]
    TM = ROUTE_SUB * TM_PROJ
    nt = T // TM
    u_blk = N_MAIN // POOL_WIDTH - 1
    halo_per_tile = TM // HALO
    tok_spec = pl.BlockSpec((TOP_K, TM), lambda i: (0, i))
    return pl.pallas_call(
        functools.partial(_out_route_kernel, seq),
        grid=(nt,),
        in_specs=[
            pl.BlockSpec((TM, D_MODEL), lambda i: (i, 0)),
            pl.BlockSpec((TM, MLSTM_WIDTH), lambda i: (i, 0)),
            pl.BlockSpec((TM, POOL_WIDTH), lambda i: (i, u_blk)),
            pl.BlockSpec((HALO, POOL_WIDTH),
                         lambda i: (jnp.maximum(i * halo_per_tile - 1, 0), u_blk)),
            pl.BlockSpec((len(POOL_WINDOWS), POOL_GROUP_DIM, POOL_GROUP_DIM), lambda i: (0, 0, 0)),
            pl.BlockSpec((1, POOL_WIDTH), lambda i: (0, 0)),
            pl.BlockSpec((D_MODEL, D_MODEL), lambda i: (0, 0)),
            pl.BlockSpec((1, D_MODEL), lambda i: (0, 0)),
            pl.BlockSpec((N_EXPERTS, D_MODEL), lambda i: (0, 0)),
            pl.BlockSpec((N_EXPERTS, 1), lambda i: (0, 0)),
        ],
        out_specs=[
            pl.BlockSpec((TM, D_MODEL), lambda i: (i, 0)),
            pl.BlockSpec((TM * PSLAB, LANES), lambda i: (i, 0)),
            tok_spec, tok_spec, tok_spec,
            pl.BlockSpec((N_EXPERTS, LANES), lambda i: (0, 0)),
        ],
        out_shape=[
            jax.ShapeDtypeStruct((T, D_MODEL), jnp.float32),
            jax.ShapeDtypeStruct((T * PSLAB, LANES), jnp.uint32),
            jax.ShapeDtypeStruct((TOP_K, T), jnp.int32),
            jax.ShapeDtypeStruct((TOP_K, T), jnp.float32),
            jax.ShapeDtypeStruct((TOP_K, T), jnp.int32),
            jax.ShapeDtypeStruct((N_EXPERTS, LANES), jnp.int32),
        ],
        scratch_shapes=[
            pltpu.VMEM((N_EXPERTS, LANES), jnp.float32),
        ],
        compiler_params=pltpu.CompilerParams(
            dimension_semantics=("arbitrary",), vmem_limit_bytes=VMEM_LIMIT),
        name="out_route",
    )(x2, ym, p, p, pool_w, pool_s, w_out, g2, wr_t, br)


def _plan(dest_flat, fill):
    n_assign = dest_flat.shape[0]
    n_table = fill.shape[0]
    mesh = plsc.VectorSubcoreMesh(core_axis_name="c", subcore_axis_name="s")

    @pl.kernel(out_type=jax.ShapeDtypeStruct((n_table,), jnp.int32), mesh=mesh,
               scratch_types=[pltpu.VMEM((n_table,), jnp.int32),
                              pltpu.VMEM((PLAN_CHUNK,), jnp.int32)],
               compiler_params=pltpu.CompilerParams(needs_layout_passes=False))
    def plan_kernel(dest_hbm, fill_hbm, out_hbm, table, chunk):
        first = jnp.logical_and(lax.axis_index("c") == 0, lax.axis_index("s") == 0)

        @pl.when(first)
        def _():
            pltpu.sync_copy(fill_hbm, table)

            @pl.loop(0, n_assign // PLAN_CHUNK)
            def _(ci):
                pltpu.sync_copy(dest_hbm.at[pl.ds(ci * PLAN_CHUNK, PLAN_CHUNK)], chunk)

                @pl.loop(0, PLAN_CHUNK // (SC_LANES * PLAN_UNROLL))
                def _(i):
                    for j in range(PLAN_UNROLL):
                        off = (i * PLAN_UNROLL + j) * SC_LANES
                        idx = chunk[pl.ds(off, SC_LANES)]
                        vals = (ci * PLAN_CHUNK + off
                                + lax.broadcasted_iota(jnp.int32, (SC_LANES,), 0))
                        plsc.store_scatter(table, [idx], vals)

            pltpu.sync_copy(table, out_hbm)

    return plan_kernel(dest_flat, fill)


def _expert_kernel(n_tok, bs_ref, slot_ref, h2_ref, wg_ref, bg_ref, wu_ref, bu_ref, wd_ref, bd_ref,
                   yt_ref, *scratch):
    TM = TM_EXPERT
    ROWS = TM * PSLAB
    e = pl.program_id(0)
    n_total = bs_ref[N_EXPERTS]
    xg = scratch[:NBUF]
    ys = scratch[NBUF:2 * NBUF]
    wgb_ref, wub_ref, wdb_ref, gsem, ssem = scratch[2 * NBUF:]

    def token_of(a):
        return a & (n_tok - 1) if n_tok & (n_tok - 1) == 0 else lax.rem(a, n_tok)

    def start_gather(blk, par):
        base = (blk + 1) * TM
        for r in range(TM):
            t = token_of(slot_ref[base + r])
            pltpu.make_async_copy(h2_ref.at[pl.ds(pl.multiple_of(t * PSLAB, PSLAB), PSLAB), :],
                                  xg[par].at[pl.ds(r * PSLAB, PSLAB), :], gsem.at[par]
                                  ).start(priority=ROW_DMA_PRIORITY)

    def wait_gather(par):
        pltpu.make_async_copy(h2_ref.at[pl.ds(0, ROWS), :], xg[0], gsem.at[par]).wait()

    def start_scatter(blk, par):
        base = (blk + 1) * TM
        for r in range(TM):
            a = slot_ref[base + r]
            pltpu.make_async_copy(ys[par].at[pl.ds(r * PSLAB, PSLAB), :],
                                  yt_ref.at[pl.ds(pl.multiple_of(a * PSLAB, PSLAB), PSLAB), :],
                                  ssem.at[par]).start(priority=ROW_DMA_PRIORITY)

    def wait_scatter(par):
        pltpu.make_async_copy(ys[0], yt_ref.at[pl.ds(0, ROWS), :], ssem.at[par]).wait()

    @pl.when(e == 0)
    def _():
        for blk in range(NBUF - 1):
            start_gather(blk, blk)
        for par in range(NBUF):
            ys[par][...] = jnp.zeros_like(ys[par])
            dump = yt_ref.at[pl.ds((n_tok * TOP_K + par * TM) * PSLAB, ROWS), :]
            cp = pltpu.make_async_copy(ys[par], dump, ssem.at[par])
            cp.start()
            cp.wait()

    wgb_ref[...] = wg_ref[0].astype(jnp.bfloat16)
    wub_ref[...] = wu_ref[0].astype(jnp.bfloat16)
    wdb_ref[...] = wd_ref[0].astype(jnp.bfloat16)

    def block_step(g, par):
        prv = (par + NBUF - 1) % NBUF
        wait_gather(par)

        @pl.when(g >= NBUF - 1)
        def _():
            wait_scatter(par)

        start_gather(g + NBUF - 1, prv)
        start_scatter(g - 1, prv)
        words = [xg[par][pl.ds(s, TM, stride=PSLAB), :] for s in range(PSLAB)]
        x = jnp.concatenate([_unpack_lo(w).astype(jnp.bfloat16) for w in words]
                            + [_unpack_hi(w).astype(jnp.bfloat16) for w in words], axis=1)
        acts = []
        for c in range(D_FF // DOT_COLS):
            cols = slice(c * DOT_COLS, (c + 1) * DOT_COLS)
            gate = jnp.dot(x, wgb_ref[:, cols], preferred_element_type=jnp.float32) + bg_ref[0][:, cols]
            up = jnp.dot(x, wub_ref[:, cols], preferred_element_type=jnp.float32) + bu_ref[0][:, cols]
            gate = jnp.minimum(gate, SWIGLU_LIMIT)
            up = jnp.clip(up, -SWIGLU_LIMIT, SWIGLU_LIMIT)
            glu = gate * _sigmoid(SWIGLU_ALPHA * gate)
            acts.append((glu * (up + 1.0)).astype(jnp.bfloat16))
        act = jnp.concatenate(acts, axis=1)
        y_parts = []
        for c in range(D_MODEL // DOT_COLS):
            cols = slice(c * DOT_COLS, (c + 1) * DOT_COLS)
            y_parts.append(jnp.dot(act, wdb_ref[:, cols], preferred_element_type=jnp.float32)
                           + bd_ref[0][:, cols])
        y = jnp.concatenate(y_parts, axis=1)
        packed = _pack_bf16_pairs(y)
        for s in range(PSLAB):
            ys[par][pl.ds(s, TM, stride=PSLAB), :] = packed[:, s * LANES:(s + 1) * LANES]

    def body(g, carry):
        for par in range(NBUF):
            pl.when(g % NBUF == par)(functools.partial(block_step, g, par))
        return carry

    lax.fori_loop(bs_ref[e], bs_ref[e + 1], body, 0)

    @pl.when(e == N_EXPERTS - 1)
    def _():
        g = n_total
        for par in range(NBUF):
            @pl.when((g - 1) % NBUF == par)
            def _():
                start_scatter(g - 1, par)
        for j in range(NBUF - 1):
            wait_gather((g + j) % NBUF)
        wait_scatter((g - 1) % NBUF)
        for j in range(2, NBUF + 1):
            @pl.when(g >= j - 1)
            def _():
                wait_scatter((g + NBUF - j) % NBUF)


def _experts(block_start, slot_buf, h2_slab, w_gate, b_gate, w_up, b_up, w_down, b_down, n_tok):
    TM = TM_EXPERT
    n_assign = n_tok * TOP_K
    w_spec = pl.BlockSpec((1, D_MODEL, D_FF), lambda e, bs, sl: (e, 0, 0))
    bias_spec = pl.BlockSpec((1, 1, D_FF), lambda e, bs, sl: (e, 0, 0))
    buf = pltpu.VMEM((TM * PSLAB, LANES), jnp.uint32)
    grid_spec = pltpu.PrefetchScalarGridSpec(
        num_scalar_prefetch=2,
        grid=(N_EXPERTS,),
        in_specs=[
            pl.BlockSpec(memory_space=pl.ANY),
            w_spec, bias_spec, w_spec, bias_spec, w_spec, bias_spec,
        ],
        out_specs=pl.BlockSpec(memory_space=pl.ANY),
        scratch_shapes=[
            *([buf] * (2 * NBUF)),
            pltpu.VMEM((D_MODEL, D_FF), jnp.bfloat16),
            pltpu.VMEM((D_MODEL, D_FF), jnp.bfloat16),
            pltpu.VMEM((D_FF, D_MODEL), jnp.bfloat16),
            pltpu.SemaphoreType.DMA((NBUF,)),
            pltpu.SemaphoreType.DMA((NBUF,)),
        ],
    )
    return pl.pallas_call(
        functools.partial(_expert_kernel, n_tok),
        grid_spec=grid_spec,
        out_shape=jax.ShapeDtypeStruct(((n_assign + NBUF * TM) * PSLAB, LANES), jnp.uint32),
        compiler_params=pltpu.CompilerParams(
            dimension_semantics=("arbitrary",), vmem_limit_bytes=VMEM_LIMIT),
        name="experts",
    )(block_start, slot_buf, h2_slab, w_gate, b_gate, w_up, b_up, w_down, b_down)


def _combine_kernel(normalize, x1_ref, y0_ref, y1_ref, y2_ref, y3_ref, gate_ref, g_ref, o_ref):
    TM = TM_PROJ
    gates = jnp.concatenate([gate_ref[...], jnp.zeros((8 - TOP_K, TM), jnp.float32)], axis=0)
    g_cols = jnp.transpose(gates)
    g_bc = [jnp.broadcast_to(g_cols[:, k:k + 1], (TM, LANES)) for k in range(TOP_K)]
    ssq = jnp.zeros((TM, LANES), jnp.float32)
    parts = [x1_ref[:, s * LANES:(s + 1) * LANES] for s in range(SLAB)]
    for s in range(PSLAB):
        for k, y_ref in enumerate((y0_ref, y1_ref, y2_ref, y3_ref)):
            w = y_ref[pl.ds(s, TM, stride=PSLAB), :]
            parts[s] = parts[s] + g_bc[k] * _unpack_lo(w)
            parts[PSLAB + s] = parts[PSLAB + s] + g_bc[k] * _unpack_hi(w)
    for acc in parts:
        ssq = ssq + acc * acc
    if normalize:
        inv = lax.rsqrt(jnp.sum(ssq, axis=-1, keepdims=True) * (1.0 / D_MODEL) + EPS)
        for s in range(SLAB):
            o_ref[:, s * LANES:(s + 1) * LANES] = parts[s] * inv * g_ref[:, s * LANES:(s + 1) * LANES]
    else:
        for s in range(SLAB):
            o_ref[:, s * LANES:(s + 1) * LANES] = parts[s]


def _combine(x1, y_tok, gate_t, gf, normalize):
    T = x1.shape[0]
    TM = TM_PROJ
    nt = T // TM

    def y_spec(k):
        return pl.BlockSpec((TM * PSLAB, LANES), lambda i: (k * nt + i, 0))

    return pl.pallas_call(
        functools.partial(_combine_kernel, normalize),
        grid=(nt,),
        in_specs=[
            pl.BlockSpec((TM, D_MODEL), lambda i: (i, 0)),
            y_spec(0), y_spec(1), y_spec(2), y_spec(3),
            pl.BlockSpec((TOP_K, TM), lambda i: (0, i)),
            pl.BlockSpec((1, D_MODEL), lambda i: (0, 0)),
        ],
        out_specs=pl.BlockSpec((TM, D_MODEL), lambda i: (i, 0)),
        out_shape=jax.ShapeDtypeStruct((T, D_MODEL), jnp.float32),
        compiler_params=pltpu.CompilerParams(
            dimension_semantics=("parallel",), vmem_limit_bytes=VMEM_LIMIT),
        name="combine",
    )(x1, y_tok, y_tok, y_tok, y_tok, gate_t, gf)


def kernel(x, norm1_g, w_in, ig_b, fg_b, conv_w, head_norm_g, pool_w, pool_scale, w_out, norm2_g,
           w_router, b_router, w_gate, b_gate, w_up, b_up, w_down, b_down, normf_g):
    B, S, D = x.shape
    T = B * S
    depth = norm1_g.shape[0]
    W = MLSTM_WIDTH
    f32, bf16 = jnp.float32, jnp.bfloat16

    L = CHUNK
    t_l = lax.broadcasted_iota(jnp.int32, (L, L), 0)
    t_r = lax.broadcasted_iota(jnp.int32, (L, L), 1)
    tri = (t_r <= t_l).astype(f32)
    shifts = jnp.stack([(t_l - t_r == CONV_WIDTH - 1 - j).astype(bf16)
                        for j in range(CONV_WIDTH - 1)])
    h_t = lax.broadcasted_iota(jnp.int32, (8, HALO), 0)
    h_r = lax.broadcasted_iota(jnp.int32, (8, HALO), 1)
    halo_shifts = jnp.stack([(h_r - HALO - h_t == -(CONV_WIDTH - 1 - j)).astype(bf16)
                             for j in range(CONV_WIDTH - 1)])

    n_assign = T * TOP_K
    n_blocks = -(-n_assign // TM_EXPERT) + N_EXPERTS
    n_rows = n_blocks * TM_EXPERT
    n_table = n_rows + NBUF * TM_EXPERT
    fill = n_assign + ((jnp.arange(n_table, dtype=jnp.int32) + (NBUF - 1) * TM_EXPERT)
                       % (NBUF * TM_EXPERT))
    x2 = x.reshape(T, D)
    for l in range(depth):
        w = w_in[l]
        w_a = w[:, :4 * W].astype(bf16)
        w_u = w[:, 4 * W + N_GATES:].astype(bf16)
        wg_t = jnp.zeros((BF16_SUBLANES, D), bf16).at[:N_GATES].set(
            w[:, 4 * W:4 * W + N_GATES].T.astype(bf16))
        p, gates_t = _in_proj(x2, norm1_g[l][None, :], w_a, w_u, wg_t)

        gate_b = jnp.concatenate([ig_b[l], fg_b[l]])[:, None].astype(f32)
        gates_b = gates_t.reshape(N_GATES, B, S).transpose(1, 0, 2)
        ym = _mlstm(p.reshape(B, S, N_MAIN), gates_b, conv_w[l].astype(f32), gate_b,
                    head_norm_g[l][None, :], tri, shifts, halo_shifts).reshape(T, W)

        x1, h2, idx_t, gate_t, rank_t, cnt = _out_route(
            x2, ym, p, pool_w[l].astype(bf16), pool_scale[l][None, :], w_out[l].astype(bf16),
            norm2_g[l][None, :], w_router[l].T.astype(bf16), b_router[l][:, None], S)

        counts = cnt[:, 0]
        padded = ((counts + TM_EXPERT - 1) // TM_EXPERT) * TM_EXPERT
        padded_end = jnp.cumsum(padded)
        padded_start = padded_end - padded
        expert_ids = jnp.arange(N_EXPERTS, dtype=jnp.int32)[:, None, None]
        start_of = jnp.sum(jnp.where(idx_t[None] == expert_ids, padded_start[:, None, None], 0), axis=0)
        dest = start_of + rank_t
        block_start = jnp.concatenate(
            [jnp.zeros((1,), jnp.int32), (padded_end // TM_EXPERT).astype(jnp.int32)])

        slot_buf = _plan(dest.reshape(-1) + TM_EXPERT, fill)
        y_tok = _experts(block_start, slot_buf, h2, w_gate[l], b_gate[l][:, None, :],
                         w_up[l], b_up[l][:, None, :], w_down[l], b_down[l][:, None, :], T)
        last = l + 1 == depth
        x2 = _combine(x1, y_tok, gate_t, normf_g[None, :], last)
    return x2.reshape(B, S, D)
```

```python
import functools

import jax
import jax.numpy as jnp
from jax import lax
from jax.experimental import pallas as pl
from jax.experimental.pallas import tpu as pltpu
from jax.experimental.pallas import tpu_sc as plsc

D_MODEL = 1024
MLSTM_WIDTH = 512
MLSTM_HEADS = 4
HEAD_DIM = 128
CONV_WIDTH = 4
POOL_WIDTH = 512
POOL_WINDOWS = (2, 4, 8, 16)
POOL_GROUP_DIM = 128
N_EXPERTS = 32
TOP_K = 4
D_FF = 1024
SWIGLU_LIMIT = 7.0
SWIGLU_ALPHA = 1.702
EPS = 1e-5

N_MAIN = 4 * MLSTM_WIDTH + POOL_WIDTH
N_GATES = 2 * MLSTM_HEADS

LANES = 128
BF16_SUBLANES = 16
VMEM_LIMIT = 56 * 1024 * 1024

TM_PROJ = 512
ROUTE_SUB = 2
CHUNK = 256
MLSTM_BATCH = 2
HALO = 16
TM_EXPERT = 256
NBUF = 4
ROW_DMA_PRIORITY = 1
SLAB = D_MODEL // LANES
PSLAB = SLAB // 2
PLAN_CHUNK = 32768
SC_LANES = 16
PLAN_UNROLL = 8

NT_DIMS = (((1,), (1,)), ((), ()))


def _sigmoid(x):
    return 1.0 / (1.0 + jnp.exp(-x))


def _pack_bf16_pairs(v):
    half = v.shape[1] // 2
    lo = pltpu.bitcast(v[:, :half].astype(jnp.bfloat16).astype(jnp.float32), jnp.uint32)
    hi = pltpu.bitcast(v[:, half:].astype(jnp.bfloat16).astype(jnp.float32), jnp.uint32)
    return (lo >> 16) | (hi & jnp.uint32(0xFFFF0000))


def _unpack_lo(w):
    return pltpu.bitcast(w << 16, jnp.float32)


def _unpack_hi(w):
    return pltpu.bitcast(w & jnp.uint32(0xFFFF0000), jnp.float32)


def _in_proj_kernel(x_ref, g_ref, wa_ref, wu_ref, wgt_ref, p_ref, gt_ref):
    x = x_ref[...]
    h = x * lax.rsqrt(jnp.mean(x * x, axis=-1, keepdims=True) + EPS) * g_ref[...]
    hb = h.astype(jnp.bfloat16)
    n_a = wa_ref.shape[1]
    p_ref[:, :n_a] = jnp.dot(hb, wa_ref[...], preferred_element_type=jnp.float32).astype(p_ref.dtype)
    p_ref[:, n_a:] = jnp.dot(hb, wu_ref[...], preferred_element_type=jnp.float32).astype(p_ref.dtype)
    gt = lax.dot_general(wgt_ref[...], hb, NT_DIMS, preferred_element_type=jnp.float32)
    gt_ref[...] = gt[:N_GATES]


def _in_proj(x2, g1, w_a, w_u, wg_t):
    T = x2.shape[0]
    return pl.pallas_call(
        _in_proj_kernel,
        grid=(T // TM_PROJ,),
        in_specs=[
            pl.BlockSpec((TM_PROJ, D_MODEL), lambda i: (i, 0)),
            pl.BlockSpec((1, D_MODEL), lambda i: (0, 0)),
            pl.BlockSpec(w_a.shape, lambda i: (0, 0)),
            pl.BlockSpec(w_u.shape, lambda i: (0, 0)),
            pl.BlockSpec((BF16_SUBLANES, D_MODEL), lambda i: (0, 0)),
        ],
        out_specs=[
            pl.BlockSpec((TM_PROJ, N_MAIN), lambda i: (i, 0)),
            pl.BlockSpec((N_GATES, TM_PROJ), lambda i: (0, i)),
        ],
        out_shape=[
            jax.ShapeDtypeStruct((T, N_MAIN), jnp.bfloat16),
            jax.ShapeDtypeStruct((N_GATES, T), jnp.float32),
        ],
        compiler_params=pltpu.CompilerParams(
            dimension_semantics=("parallel",), vmem_limit_bytes=VMEM_LIMIT),
        name="in_proj",
    )(x2, g1, w_a, w_u, wg_t)


def _mlstm_kernel(qk_ref, qkp_ref, v_ref, o_ref, gt_ref, convw_ref, gb_ref, hng_ref,
                  tri_ref, shift_ref, hshift_ref, y_ref, cn_ref, m_ref):
    L = CHUNK
    c = pl.program_id(1)

    @pl.when(c == 0)
    def _():
        cn_ref[...] = jnp.zeros_like(cn_ref)
        m_ref[...] = jnp.zeros_like(m_ref)

    row_id = lax.broadcasted_iota(jnp.int32, (L, L), 0)
    col_id = lax.broadcasted_iota(jnp.int32, (L, L), 1)
    causal = col_id <= row_id
    ones_blk = jnp.ones((L, HEAD_DIM), jnp.bfloat16)
    lane = lax.broadcasted_iota(jnp.int32, (MLSTM_HEADS, L), 1)

    gate_terms = []
    for bb in range(MLSTM_BATCH):
        gt = gt_ref[bb] + gb_ref[...]
        f = gt[MLSTM_HEADS:]
        lf = jnp.minimum(f, 0.0) - jnp.log(1.0 + jnp.exp(-jnp.abs(f)))
        ig = gt[:MLSTM_HEADS]
        b_rows = lax.dot_general(lf, tri_ref[...], NT_DIMS, precision=lax.Precision.HIGHEST,
                                 preferred_element_type=jnp.float32)
        c_rows = ig - b_rows
        cm_rows = c_rows
        d = 1
        while d < L:
            cm_rows = jnp.maximum(
                cm_rows, jnp.where(lane >= d, pltpu.roll(cm_rows, d, axis=1), -jnp.inf))
            d *= 2
        gate_terms.append((b_rows, c_rows, cm_rows))

    conv_terms = []
    for bb in range(MLSTM_BATCH):
        x_cur = qk_ref[bb]
        x_prev = jnp.where(c > 0, qkp_ref[bb], jnp.zeros((HALO, 2 * MLSTM_WIDTH), jnp.bfloat16))
        acc = convw_ref[CONV_WIDTH - 1:CONV_WIDTH, :] * x_cur.astype(jnp.float32)
        for j in range(CONV_WIDTH - 1):
            sh = jnp.dot(shift_ref[j], x_cur, preferred_element_type=jnp.float32)
            top = sh[:8] + jnp.dot(hshift_ref[j], x_prev, preferred_element_type=jnp.float32)
            sh = jnp.concatenate([top, sh[8:]], axis=0)
            acc = acc + convw_ref[j:j + 1, :] * sh
        qk = acc * _sigmoid(acc)
        q_all = qk[:, :MLSTM_WIDTH].astype(jnp.bfloat16)
        k_t = jnp.transpose(qk[:, MLSTM_WIDTH:] * (HEAD_DIM ** -0.5))
        conv_terms.append((q_all, k_t))

    for bb in range(MLSTM_BATCH):
        b_rows, c_rows, cm_rows = gate_terms[bb]
        q_all, k_t = conv_terms[bb]
        m_in4 = jnp.concatenate(
            [m_ref[bb * MLSTM_HEADS + h][0:1, 0:1] for h in range(MLSTM_HEADS)], axis=0)
        mx_rows = jnp.maximum(cm_rows, m_in4)
        inter_rows = jnp.exp(m_in4 - mx_rows)
        einv_rows = jnp.exp(-(b_rows + mx_rows))
        fac_t = jnp.transpose(jnp.concatenate(
            [mx_rows, inter_rows, einv_rows, jnp.zeros_like(mx_rows)], axis=0))

        for h in range(MLSTM_HEADS):
            lo = h * HEAD_DIM
            st = bb * MLSTM_HEADS + h
            q = q_all[:, lo:lo + HEAD_DIM]
            kt = k_t[lo:lo + HEAD_DIM, :]
            v_ext = jnp.concatenate([v_ref[bb, :, lo:lo + HEAD_DIM], ones_blk], axis=1)
            mx_col = fac_t[:, h:h + 1]
            inter_col = fac_t[:, MLSTM_HEADS + h:MLSTM_HEADS + h + 1]
            einv_col = fac_t[:, 2 * MLSTM_HEADS + h:2 * MLSTM_HEADS + h + 1]
            c_row = c_rows[h:h + 1, :]
            b_tot = b_rows[h:h + 1, L - 1:L]
            cm_tot = cm_rows[h:h + 1, L - 1:L]
            m_in = m_ref[st][0:1, 0:1]
            cn = cn_ref[st]

            s_qk = jnp.dot(q, kt.astype(jnp.bfloat16), preferred_element_type=jnp.float32)
            s = (s_qk * jnp.exp(jnp.where(causal, c_row - mx_col, -jnp.inf))).astype(jnp.bfloat16)
            num = (jnp.dot(s, v_ext, preferred_element_type=jnp.float32)
                   + inter_col * jnp.dot(q, cn.astype(jnp.bfloat16),
                                         preferred_element_type=jnp.float32))
            den = num[:, HEAD_DIM:]
            hh = num[:, :HEAD_DIM] / jnp.maximum(jnp.abs(den), einv_col)

            mu = jnp.mean(hh, axis=-1, keepdims=True)
            dv = hh - mu
            var = jnp.mean(dv * dv, axis=-1, keepdims=True)
            hn = dv * lax.rsqrt(var + EPS) * hng_ref[:, lo:lo + HEAD_DIM]
            og = _sigmoid(o_ref[bb, :, lo:lo + HEAD_DIM].astype(jnp.float32))
            y_ref[bb, :, lo:lo + HEAD_DIM] = (og * hn).astype(y_ref.dtype)

            m_loc = b_tot + cm_tot
            kw_t = (kt * jnp.exp(c_row - cm_tot)).astype(jnp.bfloat16)
            c_loc = jnp.dot(kw_t, v_ext, preferred_element_type=jnp.float32)
            m_new = jnp.maximum(b_tot + m_in, m_loc)
            s_old = jnp.exp(b_tot + m_in - m_new)
            s_loc = jnp.exp(m_loc - m_new)
            cn_ref[st] = s_old * cn + s_loc * c_loc
            m_ref[st] = jnp.broadcast_to(m_new, m_ref.shape[1:])


def _mlstm(p3, gates_b, conv_w, gate_b, hn_g, tri, shifts, halo_shifts):
    batch, seq, _ = p3.shape
    L = CHUNK
    BB = MLSTM_BATCH
    halo_per_chunk = L // HALO
    return pl.pallas_call(
        _mlstm_kernel,
        grid=(batch // BB, seq // L),
        in_specs=[
            pl.BlockSpec((BB, L, 2 * MLSTM_WIDTH), lambda bi, ci: (bi, ci, 0)),
            pl.BlockSpec((BB, HALO, 2 * MLSTM_WIDTH),
                         lambda bi, ci: (bi, jnp.maximum(ci * halo_per_chunk - 1, 0), 0)),
            pl.BlockSpec((BB, L, MLSTM_WIDTH), lambda bi, ci: (bi, ci, 2)),
            pl.BlockSpec((BB, L, MLSTM_WIDTH), lambda bi, ci: (bi, ci, 3)),
            pl.BlockSpec((BB, N_GATES, L), lambda bi, ci: (bi, 0, ci)),
            pl.BlockSpec((CONV_WIDTH, 2 * MLSTM_WIDTH), lambda bi, ci: (0, 0)),
            pl.BlockSpec((N_GATES, 1), lambda bi, ci: (0, 0)),
            pl.BlockSpec((1, MLSTM_WIDTH), lambda bi, ci: (0, 0)),
            pl.BlockSpec((L, L), lambda bi, ci: (0, 0)),
            pl.BlockSpec((CONV_WIDTH - 1, L, L), lambda bi, ci: (0, 0, 0)),
            pl.BlockSpec((CONV_WIDTH - 1, 8, HALO), lambda bi, ci: (0, 0, 0)),
        ],
        out_specs=pl.BlockSpec((BB, L, MLSTM_WIDTH), lambda bi, ci: (bi, ci, 0)),
        out_shape=jax.ShapeDtypeStruct((batch, seq, MLSTM_WIDTH), jnp.bfloat16),
        scratch_shapes=[
            pltpu.VMEM((BB * MLSTM_HEADS, HEAD_DIM, 2 * HEAD_DIM), jnp.float32),
            pltpu.VMEM((BB * MLSTM_HEADS, 8, LANES), jnp.float32),
        ],
        compiler_params=pltpu.CompilerParams(
            dimension_semantics=("parallel", "arbitrary"), vmem_limit_bytes=VMEM_LIMIT),
        name="mlstm",
    )(p3, p3, p3, p3, gates_b, conv_w, gate_b, hn_g, tri, shifts, halo_shifts)


def _out_route_kernel(seq, x_ref, ym_ref, u_ref, up_ref, pw_ref, ps_ref, wo_ref, g2_ref,
                      wrt_ref, br_ref, x1_ref, h2_ref, idx_ref, gate_ref, rank_ref, cnt_ref,
                      carry_ref):
    TM = TM_PROJ
    R = ROUTE_SUB * TM
    i = pl.program_id(0)

    @pl.when(i == 0)
    def _():
        carry_ref[...] = jnp.zeros_like(carry_ref)

    pos0 = (i * R) % seq
    e_id = lax.broadcasted_iota(jnp.int32, (N_EXPERTS, TM), 0).astype(jnp.float32)
    t_row = lax.broadcasted_iota(jnp.int32, (TM, TM), 0)
    t_col = lax.broadcasted_iota(jnp.int32, (TM, TM), 1)
    before = jnp.where(t_row < t_col, 1.0, 0.0).astype(jnp.bfloat16)
    carry = carry_ref[...]
    subs = [slice(sub * TM, (sub + 1) * TM) for sub in range(ROUTE_SUB)]

    halo = jnp.where(pos0 > 0, up_ref[...].astype(jnp.float32), 0.0)
    u_ext = jnp.concatenate([halo, u_ref[...].astype(jnp.float32)], axis=0)
    win_sums = []
    for gi, w in enumerate(POOL_WINDOWS):
        sw = u_ext[:, gi * POOL_GROUP_DIM:(gi + 1) * POOL_GROUP_DIM]
        span = 1
        while span < w:
            sw = sw + pltpu.roll(sw, span, axis=0)
            span *= 2
        win_sums.append(sw)
    y_cats = []
    for sub, rows in enumerate(subs):
        r0 = sub * TM
        pos = (pos0 + r0 + lax.broadcasted_iota(jnp.int32, (TM, 1), 0) + 1).astype(jnp.float32)
        mixed = []
        for gi, w in enumerate(POOL_WINDOWS):
            lo = gi * POOL_GROUP_DIM
            tok = u_ext[HALO + r0:HALO + r0 + TM, lo:lo + POOL_GROUP_DIM]
            pooled = win_sums[gi][HALO + r0:HALO + r0 + TM] / jnp.minimum(pos, float(w)) - tok
            mg = jnp.dot(pooled.astype(jnp.bfloat16), pw_ref[gi],
                         preferred_element_type=jnp.float32)
            mixed.append((mg * ps_ref[:, lo:lo + POOL_GROUP_DIM]).astype(jnp.bfloat16))
        y_cats.append(jnp.concatenate([ym_ref[rows, :]] + mixed, axis=1))

    all_logits = []
    for sub, rows in enumerate(subs):
        r0 = sub * TM
        x1 = x_ref[rows, :] + jnp.dot(y_cats[sub], wo_ref[...], preferred_element_type=jnp.float32)
        x1_ref[rows, :] = x1
        h2 = x1 * lax.rsqrt(jnp.mean(x1 * x1, axis=-1, keepdims=True) + EPS) * g2_ref[...]
        h2b = h2.astype(jnp.bfloat16)
        h2w = _pack_bf16_pairs(h2)
        for s in range(PSLAB):
            h2_ref[pl.ds(r0 * PSLAB + s, TM, stride=PSLAB), :] = h2w[:, s * LANES:(s + 1) * LANES]
        all_logits.append(lax.dot_general(wrt_ref[...], h2b, NT_DIMS,
                                          preferred_element_type=jnp.float32) + br_ref[...])

    for sub, rows in enumerate(subs):
        work = all_logits[sub]
        vals, ids, hots = [], [], []
        for _ in range(TOP_K):
            mk = jnp.max(work, axis=0, keepdims=True)
            ik = jnp.min(jnp.where(work == mk, e_id, float(N_EXPERTS)), axis=0, keepdims=True)
            hot = e_id == ik
            work = jnp.where(hot, -jnp.inf, work)
            vals.append(mk)
            ids.append(ik)
            hots.append(hot)
        ex = [jnp.exp(vk - vals[0]) for vk in vals]
        denom = ex[0] + ex[1] + ex[2] + ex[3]
        gate_ref[:, rows] = jnp.concatenate([e / denom for e in ex], axis=0)
        idx_ref[:, rows] = jnp.concatenate(ids, axis=0).astype(jnp.int32)

        sel_f = sum(jnp.where(hot, 1.0, 0.0) for hot in hots)
        prefix = jnp.dot(sel_f.astype(jnp.bfloat16), before, preferred_element_type=jnp.float32)
        rank_e = carry[:, 0:1] + prefix
        ranks = [jnp.sum(jnp.where(hot, rank_e, 0.0), axis=0, keepdims=True) for hot in hots]
        rank_ref[:, rows] = jnp.concatenate(ranks, axis=0).astype(jnp.int32)
        carry = carry + jnp.sum(sel_f, axis=1, keepdims=True)
    carry_ref[...] = carry
    cnt_ref[...] = carry.astype(jnp.int32)


def _out_route(x2, ym, p, pool_w, pool_s, w_out, g2, wr_t, br, seq):
    T = x2.shape[0]
    TM = ROUTE_SUB * TM_PROJ
    nt = T // TM
    u_blk = N_MAIN // POOL_WIDTH - 1
    halo_per_tile = TM // HALO
    tok_spec = pl.BlockSpec((TOP_K, TM), lambda i: (0, i))
    return pl.pallas_call(
        functools.partial(_out_route_kernel, seq),
        grid=(nt,),
        in_specs=[
            pl.BlockSpec((TM, D_MODEL), lambda i: (i, 0)),
            pl.BlockSpec((TM, MLSTM_WIDTH), lambda i: (i, 0)),
            pl.BlockSpec((TM, POOL_WIDTH), lambda i: (i, u_blk)),
            pl.BlockSpec((HALO, POOL_WIDTH),
                         lambda i: (jnp.maximum(i * halo_per_tile - 1, 0), u_blk)),
            pl.BlockSpec((len(POOL_WINDOWS), POOL_GROUP_DIM, POOL_GROUP_DIM), lambda i: (0, 0, 0)),
            pl.BlockSpec((1, POOL_WIDTH), lambda i: (0, 0)),
            pl.BlockSpec((D_MODEL, D_MODEL), lambda i: (0, 0)),
            pl.BlockSpec((1, D_MODEL), lambda i: (0, 0)),
            pl.BlockSpec((N_EXPERTS, D_MODEL), lambda i: (0, 0)),
            pl.BlockSpec((N_EXPERTS, 1), lambda i: (0, 0)),
        ],
        out_specs=[
            pl.BlockSpec((TM, D_MODEL), lambda i: (i, 0)),
            pl.BlockSpec((TM * PSLAB, LANES), lambda i: (i, 0)),
            tok_spec, tok_spec, tok_spec,
            pl.BlockSpec((N_EXPERTS, LANES), lambda i: (0, 0)),
        ],
        out_shape=[
            jax.ShapeDtypeStruct((T, D_MODEL), jnp.float32),
            jax.ShapeDtypeStruct((T * PSLAB, LANES), jnp.uint32),
            jax.ShapeDtypeStruct((TOP_K, T), jnp.int32),
            jax.ShapeDtypeStruct((TOP_K, T), jnp.float32),
            jax.ShapeDtypeStruct((TOP_K, T), jnp.int32),
            jax.ShapeDtypeStruct((N_EXPERTS, LANES), jnp.int32),
        ],
        scratch_shapes=[
            pltpu.VMEM((N_EXPERTS, LANES), jnp.float32),
        ],
        compiler_params=pltpu.CompilerParams(
            dimension_semantics=("arbitrary",), vmem_limit_bytes=VMEM_LIMIT),
        name="out_route",
    )(x2, ym, p, p, pool_w, pool_s, w_out, g2, wr_t, br)


def _plan(dest_flat, n_table):
    n_assign = dest_flat.shape[0]
    period = NBUF * TM_EXPERT
    mesh = plsc.VectorSubcoreMesh(core_axis_name="c", subcore_axis_name="s")

    @pl.kernel(out_type=jax.ShapeDtypeStruct((n_table,), jnp.int32), mesh=mesh,
               scratch_types=[pltpu.VMEM((n_table,), jnp.int32),
                              pltpu.VMEM((PLAN_CHUNK,), jnp.int32)],
               compiler_params=pltpu.CompilerParams(needs_layout_passes=False))
    def plan_kernel(dest_hbm, out_hbm, table, chunk):
        first = jnp.logical_and(lax.axis_index("c") == 0, lax.axis_index("s") == 0)
        lanes = lax.broadcasted_iota(jnp.int32, (SC_LANES,), 0)

        @pl.when(first)
        def _():
            @pl.loop(0, n_table // SC_LANES)
            def _(i):
                q = i * SC_LANES + lanes + (NBUF - 1) * TM_EXPERT
                dump = q & (period - 1) if period & (period - 1) == 0 else lax.rem(q, period)
                table[pl.ds(i * SC_LANES, SC_LANES)] = n_assign + dump

            @pl.loop(0, n_assign // PLAN_CHUNK)
            def _(ci):
                pltpu.sync_copy(dest_hbm.at[pl.ds(ci * PLAN_CHUNK, PLAN_CHUNK)], chunk)

                @pl.loop(0, PLAN_CHUNK // (SC_LANES * PLAN_UNROLL))
                def _(i):
                    for j in range(PLAN_UNROLL):
                        off = (i * PLAN_UNROLL + j) * SC_LANES
                        idx = chunk[pl.ds(off, SC_LANES)]
                        vals = ci * PLAN_CHUNK + off + lanes
                        plsc.store_scatter(table, [idx], vals)

            pltpu.sync_copy(table, out_hbm)

    return plan_kernel(dest_flat)


def _expert_kernel(n_tok, bs_ref, slot_ref, h2_ref, wg_ref, bg_ref, wu_ref, bu_ref, wd_ref, bd_ref,
                   yt_ref, *scratch):
    TM = TM_EXPERT
    ROWS = TM * PSLAB
    e = pl.program_id(0)
    n_total = bs_ref[N_EXPERTS]
    xg = scratch[:NBUF]
    ys = scratch[NBUF:2 * NBUF]
    wgb_ref, wub_ref, wdb_ref, gsem, ssem = scratch[2 * NBUF:]

    def token_of(a):
        return a & (n_tok - 1) if n_tok & (n_tok - 1) == 0 else lax.rem(a, n_tok)

    def start_gather(blk, par):
        base = (blk + 1) * TM
        for r in range(TM):
            t = token_of(slot_ref[base + r])
            pltpu.make_async_copy(h2_ref.at[pl.ds(pl.multiple_of(t * PSLAB, PSLAB), PSLAB), :],
                                  xg[par].at[pl.ds(r * PSLAB, PSLAB), :], gsem.at[par]
                                  ).start(priority=ROW_DMA_PRIORITY)

    def wait_gather(par):
        pltpu.make_async_copy(h2_ref.at[pl.ds(0, ROWS), :], xg[0], gsem.at[par]).wait()

    def start_scatter(blk, par):
        base = (blk + 1) * TM
        for r in range(TM):
            a = slot_ref[base + r]
            pltpu.make_async_copy(ys[par].at[pl.ds(r * PSLAB, PSLAB), :],
                                  yt_ref.at[pl.ds(pl.multiple_of(a * PSLAB, PSLAB), PSLAB), :],
                                  ssem.at[par]).start(priority=ROW_DMA_PRIORITY)

    def wait_scatter(par):
        pltpu.make_async_copy(ys[0], yt_ref.at[pl.ds(0, ROWS), :], ssem.at[par]).wait()

    @pl.when(e == 0)
    def _():
        for blk in range(NBUF - 1):
            start_gather(blk, blk)
        for par in range(NBUF):
            ys[par][...] = jnp.zeros_like(ys[par])
            dump = yt_ref.at[pl.ds((n_tok * TOP_K + par * TM) * PSLAB, ROWS), :]
            cp = pltpu.make_async_copy(ys[par], dump, ssem.at[par])
            cp.start()
            cp.wait()

    wgb_ref[...] = wg_ref[0].astype(jnp.bfloat16)
    wub_ref[...] = wu_ref[0].astype(jnp.bfloat16)
    wdb_ref[...] = wd_ref[0].astype(jnp.bfloat16)

    def block_step(g, par):
        prv = (par + NBUF - 1) % NBUF
        wait_gather(par)

        @pl.when(g >= NBUF - 1)
        def _():
            wait_scatter(par)

        start_gather(g + NBUF - 1, prv)
        start_scatter(g - 1, prv)
        words = [xg[par][pl.ds(s, TM, stride=PSLAB), :] for s in range(PSLAB)]
        x = jnp.concatenate([_unpack_lo(w).astype(jnp.bfloat16) for w in words]
                            + [_unpack_hi(w).astype(jnp.bfloat16) for w in words], axis=1)
        gate = jnp.dot(x, wgb_ref[...], preferred_element_type=jnp.float32) + bg_ref[0]
        up = jnp.dot(x, wub_ref[...], preferred_element_type=jnp.float32) + bu_ref[0]
        gate = jnp.minimum(gate, SWIGLU_LIMIT)
        up = jnp.clip(up, -SWIGLU_LIMIT, SWIGLU_LIMIT)
        glu = gate * _sigmoid(SWIGLU_ALPHA * gate)
        act = (glu * (up + 1.0)).astype(jnp.bfloat16)
        y = jnp.dot(act, wdb_ref[...], preferred_element_type=jnp.float32) + bd_ref[0]
        packed = _pack_bf16_pairs(y)
        for s in range(PSLAB):
            ys[par][pl.ds(s, TM, stride=PSLAB), :] = packed[:, s * LANES:(s + 1) * LANES]

    def body(g, carry):
        for par in range(NBUF):
            pl.when(g % NBUF == par)(functools.partial(block_step, g, par))
        return carry

    lax.fori_loop(bs_ref[e], bs_ref[e + 1], body, 0)

    @pl.when(e == N_EXPERTS - 1)
    def _():
        g = n_total
        for par in range(NBUF):
            @pl.when((g - 1) % NBUF == par)
            def _():
                start_scatter(g - 1, par)
        for j in range(NBUF - 1):
            wait_gather((g + j) % NBUF)
        wait_scatter((g - 1) % NBUF)
        for j in range(2, NBUF + 1):
            @pl.when(g >= j - 1)
            def _():
                wait_scatter((g + NBUF - j) % NBUF)


def _experts(block_start, slot_buf, h2_slab, w_gate, b_gate, w_up, b_up, w_down, b_down, n_tok):
    TM = TM_EXPERT
    n_assign = n_tok * TOP_K
    w_spec = pl.BlockSpec((1, D_MODEL, D_FF), lambda e, bs, sl: (e, 0, 0))
    bias_spec = pl.BlockSpec((1, 1, D_FF), lambda e, bs, sl: (e, 0, 0))
    buf = pltpu.VMEM((TM * PSLAB, LANES), jnp.uint32)
    grid_spec = pltpu.PrefetchScalarGridSpec(
        num_scalar_prefetch=2,
        grid=(N_EXPERTS,),
        in_specs=[
            pl.BlockSpec(memory_space=pl.ANY),
            w_spec, bias_spec, w_spec, bias_spec, w_spec, bias_spec,
        ],
        out_specs=pl.BlockSpec(memory_space=pl.ANY),
        scratch_shapes=[
            *([buf] * (2 * NBUF)),
            pltpu.VMEM((D_MODEL, D_FF), jnp.bfloat16),
            pltpu.VMEM((D_MODEL, D_FF), jnp.bfloat16),
            pltpu.VMEM((D_FF, D_MODEL), jnp.bfloat16),
            pltpu.SemaphoreType.DMA((NBUF,)),
            pltpu.SemaphoreType.DMA((NBUF,)),
        ],
    )
    return pl.pallas_call(
        functools.partial(_expert_kernel, n_tok),
        grid_spec=grid_spec,
        out_shape=jax.ShapeDtypeStruct(((n_assign + NBUF * TM) * PSLAB, LANES), jnp.uint32),
        compiler_params=pltpu.CompilerParams(
            dimension_semantics=("arbitrary",), vmem_limit_bytes=VMEM_LIMIT),
        name="experts",
    )(block_start, slot_buf, h2_slab, w_gate, b_gate, w_up, b_up, w_down, b_down)


def _combine_kernel(normalize, x1_ref, y0_ref, y1_ref, y2_ref, y3_ref, gate_ref, g_ref, o_ref):
    TM = TM_PROJ
    gates = jnp.concatenate([gate_ref[...], jnp.zeros((8 - TOP_K, TM), jnp.float32)], axis=0)
    g_cols = jnp.transpose(gates)
    g_bc = [jnp.broadcast_to(g_cols[:, k:k + 1], (TM, LANES)) for k in range(TOP_K)]
    ssq = jnp.zeros((TM, LANES), jnp.float32)
    parts = [x1_ref[:, s * LANES:(s + 1) * LANES] for s in range(SLAB)]
    for s in range(PSLAB):
        for k, y_ref in enumerate((y0_ref, y1_ref, y2_ref, y3_ref)):
            w = y_ref[pl.ds(s, TM, stride=PSLAB), :]
            parts[s] = parts[s] + g_bc[k] * _unpack_lo(w)
            parts[PSLAB + s] = parts[PSLAB + s] + g_bc[k] * _unpack_hi(w)
    for acc in parts:
        ssq = ssq + acc * acc
    if normalize:
        inv = lax.rsqrt(jnp.sum(ssq, axis=-1, keepdims=True) * (1.0 / D_MODEL) + EPS)
        for s in range(SLAB):
            o_ref[:, s * LANES:(s + 1) * LANES] = parts[s] * inv * g_ref[:, s * LANES:(s + 1) * LANES]
    else:
        for s in range(SLAB):
            o_ref[:, s * LANES:(s + 1) * LANES] = parts[s]


def _combine(x1, y_tok, gate_t, gf, normalize):
    T = x1.shape[0]
    TM = TM_PROJ
    nt = T // TM

    def y_spec(k):
        return pl.BlockSpec((TM * PSLAB, LANES), lambda i: (k * nt + i, 0))

    return pl.pallas_call(
        functools.partial(_combine_kernel, normalize),
        grid=(nt,),
        in_specs=[
            pl.BlockSpec((TM, D_MODEL), lambda i: (i, 0)),
            y_spec(0), y_spec(1), y_spec(2), y_spec(3),
            pl.BlockSpec((TOP_K, TM), lambda i: (0, i)),
            pl.BlockSpec((1, D_MODEL), lambda i: (0, 0)),
        ],
        out_specs=pl.BlockSpec((TM, D_MODEL), lambda i: (i, 0)),
        out_shape=jax.ShapeDtypeStruct((T, D_MODEL), jnp.float32),
        compiler_params=pltpu.CompilerParams(
            dimension_semantics=("parallel",), vmem_limit_bytes=VMEM_LIMIT),
        name="combine",
    )(x1, y_tok, y_tok, y_tok, y_tok, gate_t, gf)


def kernel(x, norm1_g, w_in, ig_b, fg_b, conv_w, head_norm_g, pool_w, pool_scale, w_out, norm2_g,
           w_router, b_router, w_gate, b_gate, w_up, b_up, w_down, b_down, normf_g):
    B, S, D = x.shape
    T = B * S
    depth = norm1_g.shape[0]
    W = MLSTM_WIDTH
    f32, bf16 = jnp.float32, jnp.bfloat16

    L = CHUNK
    t_l = lax.broadcasted_iota(jnp.int32, (L, L), 0)
    t_r = lax.broadcasted_iota(jnp.int32, (L, L), 1)
    tri = (t_r <= t_l).astype(f32)
    shifts = jnp.stack([(t_l - t_r == CONV_WIDTH - 1 - j).astype(bf16)
                        for j in range(CONV_WIDTH - 1)])
    h_t = lax.broadcasted_iota(jnp.int32, (8, HALO), 0)
    h_r = lax.broadcasted_iota(jnp.int32, (8, HALO), 1)
    halo_shifts = jnp.stack([(h_r - HALO - h_t == -(CONV_WIDTH - 1 - j)).astype(bf16)
                             for j in range(CONV_WIDTH - 1)])

    n_assign = T * TOP_K
    n_blocks = -(-n_assign // TM_EXPERT) + N_EXPERTS
    n_rows = n_blocks * TM_EXPERT
    n_table = n_rows + NBUF * TM_EXPERT
    x2 = x.reshape(T, D)
    for l in range(depth):
        w = w_in[l]
        w_a = w[:, :4 * W].astype(bf16)
        w_u = w[:, 4 * W + N_GATES:].astype(bf16)
        wg_t = jnp.zeros((BF16_SUBLANES, D), bf16).at[:N_GATES].set(
            w[:, 4 * W:4 * W + N_GATES].T.astype(bf16))
        p, gates_t = _in_proj(x2, norm1_g[l][None, :], w_a, w_u, wg_t)

        gate_b = jnp.concatenate([ig_b[l], fg_b[l]])[:, None].astype(f32)
        gates_b = gates_t.reshape(N_GATES, B, S).transpose(1, 0, 2)
        ym = _mlstm(p.reshape(B, S, N_MAIN), gates_b, conv_w[l].astype(f32), gate_b,
                    head_norm_g[l][None, :], tri, shifts, halo_shifts).reshape(T, W)

        x1, h2, idx_t, gate_t, rank_t, cnt = _out_route(
            x2, ym, p, pool_w[l].astype(bf16), pool_scale[l][None, :], w_out[l].astype(bf16),
            norm2_g[l][None, :], w_router[l].T.astype(bf16), b_router[l][:, None], S)

        counts = cnt[:, 0]
        padded = ((counts + TM_EXPERT - 1) // TM_EXPERT) * TM_EXPERT
        padded_end = jnp.cumsum(padded)
        padded_start = padded_end - padded
        expert_ids = jnp.arange(N_EXPERTS, dtype=jnp.int32)[:, None, None]
        start_of = jnp.sum(jnp.where(idx_t[None] == expert_ids, padded_start[:, None, None], 0), axis=0)
        dest = start_of + rank_t
        block_start = jnp.concatenate(
            [jnp.zeros((1,), jnp.int32), (padded_end // TM_EXPERT).astype(jnp.int32)])

        slot_buf = _plan(dest.reshape(-1) + TM_EXPERT, n_table)
        y_tok = _experts(block_start, slot_buf, h2, w_gate[l], b_gate[l][:, None, :],
                         w_up[l], b_up[l][:, None, :], w_down[l], b_down[l][:, None, :], T)
        last = l + 1 == depth
        x2 = _combine(x1, y_tok, gate_t, normf_g[None, :], last)
    return x2.reshape(B, S, D)
```

```python
import functools

import jax
import jax.numpy as jnp
from jax import lax
from jax.experimental import pallas as pl
from jax.experimental.pallas import tpu as pltpu
from jax.experimental.pallas import tpu_sc as plsc

D_MODEL = 1024
MLSTM_WIDTH = 512
MLSTM_HEADS = 4
HEAD_DIM = 128
CONV_WIDTH = 4
POOL_WIDTH = 512
POOL_WINDOWS = (2, 4, 8, 16)
POOL_GROUP_DIM = 128
N_EXPERTS = 32
TOP_K = 4
D_FF = 1024
SWIGLU_LIMIT = 7.0
SWIGLU_ALPHA = 1.702
EPS = 1e-5

N_MAIN = 4 * MLSTM_WIDTH + POOL_WIDTH
N_GATES = 2 * MLSTM_HEADS

LANES = 128
BF16_SUBLANES = 16
VMEM_LIMIT = 56 * 1024 * 1024

TM_PROJ = 512
ROUTE_SUB = 2
CHUNK = 256
MLSTM_BATCH = 2
HALO = 16
TM_EXPERT = 256
NBUF = 4
ROW_DMA_PRIORITY = 1
SLAB = D_MODEL // LANES
PSLAB = SLAB // 2
PLAN_CHUNK = 8192
SC_LANES = 16
PLAN_UNROLL = 8
PLAN_TILES = 8

NT_DIMS = (((1,), (1,)), ((), ()))


def _sigmoid(x):
    return 1.0 / (1.0 + jnp.exp(-x))


def _pack_bf16_pairs(v):
    half = v.shape[1] // 2
    lo = pltpu.bitcast(v[:, :half].astype(jnp.bfloat16).astype(jnp.float32), jnp.uint32)
    hi = pltpu.bitcast(v[:, half:].astype(jnp.bfloat16).astype(jnp.float32), jnp.uint32)
    return (lo >> 16) | (hi & jnp.uint32(0xFFFF0000))


def _unpack_lo(w):
    return pltpu.bitcast(w << 16, jnp.float32)


def _unpack_hi(w):
    return pltpu.bitcast(w & jnp.uint32(0xFFFF0000), jnp.float32)


def _in_proj_kernel(x_ref, g_ref, wa_ref, wu_ref, wgt_ref, p_ref, gt_ref):
    x = x_ref[...]
    h = x * lax.rsqrt(jnp.mean(x * x, axis=-1, keepdims=True) + EPS) * g_ref[...]
    hb = h.astype(jnp.bfloat16)
    n_a = wa_ref.shape[1]
    p_ref[:, :n_a] = jnp.dot(hb, wa_ref[...], preferred_element_type=jnp.float32).astype(p_ref.dtype)
    p_ref[:, n_a:] = jnp.dot(hb, wu_ref[...], preferred_element_type=jnp.float32).astype(p_ref.dtype)
    gt = lax.dot_general(wgt_ref[...], hb, NT_DIMS, preferred_element_type=jnp.float32)
    gt_ref[...] = gt[:N_GATES]


def _in_proj(x2, g1, w_a, w_u, wg_t):
    T = x2.shape[0]
    return pl.pallas_call(
        _in_proj_kernel,
        grid=(T // TM_PROJ,),
        in_specs=[
            pl.BlockSpec((TM_PROJ, D_MODEL), lambda i: (i, 0)),
            pl.BlockSpec((1, D_MODEL), lambda i: (0, 0)),
            pl.BlockSpec(w_a.shape, lambda i: (0, 0)),
            pl.BlockSpec(w_u.shape, lambda i: (0, 0)),
            pl.BlockSpec((BF16_SUBLANES, D_MODEL), lambda i: (0, 0)),
        ],
        out_specs=[
            pl.BlockSpec((TM_PROJ, N_MAIN), lambda i: (i, 0)),
            pl.BlockSpec((N_GATES, TM_PROJ), lambda i: (0, i)),
        ],
        out_shape=[
            jax.ShapeDtypeStruct((T, N_MAIN), jnp.bfloat16),
            jax.ShapeDtypeStruct((N_GATES, T), jnp.float32),
        ],
        compiler_params=pltpu.CompilerParams(
            dimension_semantics=("parallel",), vmem_limit_bytes=VMEM_LIMIT),
        name="in_proj",
    )(x2, g1, w_a, w_u, wg_t)


def _mlstm_kernel(qk_ref, qkp_ref, v_ref, o_ref, gt_ref, convw_ref, gb_ref, hng_ref,
                  tri_ref, shift_ref, hshift_ref, y_ref, cn_ref, m_ref):
    L = CHUNK
    c = pl.program_id(1)

    @pl.when(c == 0)
    def _():
        cn_ref[...] = jnp.zeros_like(cn_ref)
        m_ref[...] = jnp.zeros_like(m_ref)

    row_id = lax.broadcasted_iota(jnp.int32, (L, L), 0)
    col_id = lax.broadcasted_iota(jnp.int32, (L, L), 1)
    causal = col_id <= row_id
    ones_blk = jnp.ones((L, HEAD_DIM), jnp.bfloat16)
    lane = lax.broadcasted_iota(jnp.int32, (MLSTM_HEADS, L), 1)

    gate_terms = []
    for bb in range(MLSTM_BATCH):
        gt = gt_ref[bb] + gb_ref[...]
        f = gt[MLSTM_HEADS:]
        lf = jnp.minimum(f, 0.0) - jnp.log(1.0 + jnp.exp(-jnp.abs(f)))
        ig = gt[:MLSTM_HEADS]
        b_rows = lax.dot_general(lf, tri_ref[...], NT_DIMS, precision=lax.Precision.HIGHEST,
                                 preferred_element_type=jnp.float32)
        c_rows = ig - b_rows
        cm_rows = c_rows
        d = 1
        while d < L:
            cm_rows = jnp.maximum(
                cm_rows, jnp.where(lane >= d, pltpu.roll(cm_rows, d, axis=1), -jnp.inf))
            d *= 2
        gate_terms.append((b_rows, c_rows, cm_rows))

    conv_terms = []
    for bb in range(MLSTM_BATCH):
        x_cur = qk_ref[bb]
        x_prev = jnp.where(c > 0, qkp_ref[bb], jnp.zeros((HALO, 2 * MLSTM_WIDTH), jnp.bfloat16))
        acc = convw_ref[CONV_WIDTH - 1:CONV_WIDTH, :] * x_cur.astype(jnp.float32)
        for j in range(CONV_WIDTH - 1):
            sh = jnp.dot(shift_ref[j], x_cur, preferred_element_type=jnp.float32)
            top = sh[:8] + jnp.dot(hshift_ref[j], x_prev, preferred_element_type=jnp.float32)
            sh = jnp.concatenate([top, sh[8:]], axis=0)
            acc = acc + convw_ref[j:j + 1, :] * sh
        qk = acc * _sigmoid(acc)
        q_all = qk[:, :MLSTM_WIDTH].astype(jnp.bfloat16)
        k_t = jnp.transpose(qk[:, MLSTM_WIDTH:] * (HEAD_DIM ** -0.5))
        conv_terms.append((q_all, k_t))

    for bb in range(MLSTM_BATCH):
        b_rows, c_rows, cm_rows = gate_terms[bb]
        q_all, k_t = conv_terms[bb]
        m_in4 = jnp.concatenate(
            [m_ref[bb * MLSTM_HEADS + h][0:1, 0:1] for h in range(MLSTM_HEADS)], axis=0)
        mx_rows = jnp.maximum(cm_rows, m_in4)
        inter_rows = jnp.exp(m_in4 - mx_rows)
        einv_rows = jnp.exp(-(b_rows + mx_rows))
        fac_t = jnp.transpose(jnp.concatenate(
            [mx_rows, inter_rows, einv_rows, jnp.zeros_like(mx_rows)], axis=0))

        for h in range(MLSTM_HEADS):
            lo = h * HEAD_DIM
            st = bb * MLSTM_HEADS + h
            q = q_all[:, lo:lo + HEAD_DIM]
            kt = k_t[lo:lo + HEAD_DIM, :]
            v_ext = jnp.concatenate([v_ref[bb, :, lo:lo + HEAD_DIM], ones_blk], axis=1)
            mx_col = fac_t[:, h:h + 1]
            inter_col = fac_t[:, MLSTM_HEADS + h:MLSTM_HEADS + h + 1]
            einv_col = fac_t[:, 2 * MLSTM_HEADS + h:2 * MLSTM_HEADS + h + 1]
            c_row = c_rows[h:h + 1, :]
            b_tot = b_rows[h:h + 1, L - 1:L]
            cm_tot = cm_rows[h:h + 1, L - 1:L]
            m_in = m_ref[st][0:1, 0:1]
            cn = cn_ref[st]

            s_qk = jnp.dot(q, kt.astype(jnp.bfloat16), preferred_element_type=jnp.float32)
            s = (s_qk * jnp.exp(jnp.where(causal, c_row - mx_col, -jnp.inf))).astype(jnp.bfloat16)
            num = (jnp.dot(s, v_ext, preferred_element_type=jnp.float32)
                   + inter_col * jnp.dot(q, cn.astype(jnp.bfloat16),
                                         preferred_element_type=jnp.float32))
            den = num[:, HEAD_DIM:]
            hh = num[:, :HEAD_DIM] / jnp.maximum(jnp.abs(den), einv_col)

            mu = jnp.mean(hh, axis=-1, keepdims=True)
            dv = hh - mu
            var = jnp.mean(dv * dv, axis=-1, keepdims=True)
            hn = dv * lax.rsqrt(var + EPS) * hng_ref[:, lo:lo + HEAD_DIM]
            og = _sigmoid(o_ref[bb, :, lo:lo + HEAD_DIM].astype(jnp.float32))
            y_ref[bb, :, lo:lo + HEAD_DIM] = (og * hn).astype(y_ref.dtype)

            m_loc = b_tot + cm_tot
            kw_t = (kt * jnp.exp(c_row - cm_tot)).astype(jnp.bfloat16)
            c_loc = jnp.dot(kw_t, v_ext, preferred_element_type=jnp.float32)
            m_new = jnp.maximum(b_tot + m_in, m_loc)
            s_old = jnp.exp(b_tot + m_in - m_new)
            s_loc = jnp.exp(m_loc - m_new)
            cn_ref[st] = s_old * cn + s_loc * c_loc
            m_ref[st] = jnp.broadcast_to(m_new, m_ref.shape[1:])


def _mlstm(p3, gates_b, conv_w, gate_b, hn_g, tri, shifts, halo_shifts):
    batch, seq, _ = p3.shape
    L = CHUNK
    BB = MLSTM_BATCH
    halo_per_chunk = L // HALO
    return pl.pallas_call(
        _mlstm_kernel,
        grid=(batch // BB, seq // L),
        in_specs=[
            pl.BlockSpec((BB, L, 2 * MLSTM_WIDTH), lambda bi, ci: (bi, ci, 0)),
            pl.BlockSpec((BB, HALO, 2 * MLSTM_WIDTH),
                         lambda bi, ci: (bi, jnp.maximum(ci * halo_per_chunk - 1, 0), 0)),
            pl.BlockSpec((BB, L, MLSTM_WIDTH), lambda bi, ci: (bi, ci, 2)),
            pl.BlockSpec((BB, L, MLSTM_WIDTH), lambda bi, ci: (bi, ci, 3)),
            pl.BlockSpec((BB, N_GATES, L), lambda bi, ci: (bi, 0, ci)),
            pl.BlockSpec((CONV_WIDTH, 2 * MLSTM_WIDTH), lambda bi, ci: (0, 0)),
            pl.BlockSpec((N_GATES, 1), lambda bi, ci: (0, 0)),
            pl.BlockSpec((1, MLSTM_WIDTH), lambda bi, ci: (0, 0)),
            pl.BlockSpec((L, L), lambda bi, ci: (0, 0)),
            pl.BlockSpec((CONV_WIDTH - 1, L, L), lambda bi, ci: (0, 0, 0)),
            pl.BlockSpec((CONV_WIDTH - 1, 8, HALO), lambda bi, ci: (0, 0, 0)),
        ],
        out_specs=pl.BlockSpec((BB, L, MLSTM_WIDTH), lambda bi, ci: (bi, ci, 0)),
        out_shape=jax.ShapeDtypeStruct((batch, seq, MLSTM_WIDTH), jnp.bfloat16),
        scratch_shapes=[
            pltpu.VMEM((BB * MLSTM_HEADS, HEAD_DIM, 2 * HEAD_DIM), jnp.float32),
            pltpu.VMEM((BB * MLSTM_HEADS, 8, LANES), jnp.float32),
        ],
        compiler_params=pltpu.CompilerParams(
            dimension_semantics=("parallel", "arbitrary"), vmem_limit_bytes=VMEM_LIMIT),
        name="mlstm",
    )(p3, p3, p3, p3, gates_b, conv_w, gate_b, hn_g, tri, shifts, halo_shifts)


def _out_route_kernel(seq, x_ref, ym_ref, u_ref, up_ref, pw_ref, ps_ref, wo_ref, g2_ref,
                      wrt_ref, br_ref, x1_ref, h2_ref, idx_ref, gate_ref, rank_ref, cnt_ref,
                      carry_ref):
    TM = TM_PROJ
    R = ROUTE_SUB * TM
    i = pl.program_id(0)

    @pl.when(i == 0)
    def _():
        carry_ref[...] = jnp.zeros_like(carry_ref)

    pos0 = (i * R) % seq
    e_id = lax.broadcasted_iota(jnp.int32, (N_EXPERTS, TM), 0).astype(jnp.float32)
    t_row = lax.broadcasted_iota(jnp.int32, (TM, TM), 0)
    t_col = lax.broadcasted_iota(jnp.int32, (TM, TM), 1)
    before = jnp.where(t_row < t_col, 1.0, 0.0).astype(jnp.bfloat16)
    carry = carry_ref[...]
    subs = [slice(sub * TM, (sub + 1) * TM) for sub in range(ROUTE_SUB)]

    halo = jnp.where(pos0 > 0, up_ref[...].astype(jnp.float32), 0.0)
    u_ext = jnp.concatenate([halo, u_ref[...].astype(jnp.float32)], axis=0)
    win_sums = []
    for gi, w in enumerate(POOL_WINDOWS):
        sw = u_ext[:, gi * POOL_GROUP_DIM:(gi + 1) * POOL_GROUP_DIM]
        span = 1
        while span < w:
            sw = sw + pltpu.roll(sw, span, axis=0)
            span *= 2
        win_sums.append(sw)
    y_cats = []
    for sub, rows in enumerate(subs):
        r0 = sub * TM
        pos = (pos0 + r0 + lax.broadcasted_iota(jnp.int32, (TM, 1), 0) + 1).astype(jnp.float32)
        mixed = []
        for gi, w in enumerate(POOL_WINDOWS):
            lo = gi * POOL_GROUP_DIM
            tok = u_ext[HALO + r0:HALO + r0 + TM, lo:lo + POOL_GROUP_DIM]
            pooled = win_sums[gi][HALO + r0:HALO + r0 + TM] / jnp.minimum(pos, float(w)) - tok
            mg = jnp.dot(pooled.astype(jnp.bfloat16), pw_ref[gi],
                         preferred_element_type=jnp.float32)
            mixed.append((mg * ps_ref[:, lo:lo + POOL_GROUP_DIM]).astype(jnp.bfloat16))
        y_cats.append(jnp.concatenate([ym_ref[rows, :]] + mixed, axis=1))

    all_logits = []
    for sub, rows in enumerate(subs):
        r0 = sub * TM
        x1 = x_ref[rows, :] + jnp.dot(y_cats[sub], wo_ref[...], preferred_element_type=jnp.float32)
        x1_ref[rows, :] = x1
        h2 = x1 * lax.rsqrt(jnp.mean(x1 * x1, axis=-1, keepdims=True) + EPS) * g2_ref[...]
        h2b = h2.astype(jnp.bfloat16)
        h2w = _pack_bf16_pairs(h2)
        for s in range(PSLAB):
            h2_ref[pl.ds(r0 * PSLAB + s, TM, stride=PSLAB), :] = h2w[:, s * LANES:(s + 1) * LANES]
        all_logits.append(lax.dot_general(wrt_ref[...], h2b, NT_DIMS,
                                          preferred_element_type=jnp.float32) + br_ref[...])

    for sub, rows in enumerate(subs):
        work = all_logits[sub]
        vals, ids, hots = [], [], []
        for _ in range(TOP_K):
            mk = jnp.max(work, axis=0, keepdims=True)
            ik = jnp.min(jnp.where(work == mk, e_id, float(N_EXPERTS)), axis=0, keepdims=True)
            hot = e_id == ik
            work = jnp.where(hot, -jnp.inf, work)
            vals.append(mk)
            ids.append(ik)
            hots.append(hot)
        ex = [jnp.exp(vk - vals[0]) for vk in vals]
        denom = ex[0] + ex[1] + ex[2] + ex[3]
        gate_ref[:, rows] = jnp.concatenate([e / denom for e in ex], axis=0)
        idx_ref[:, rows] = jnp.concatenate(ids, axis=0).astype(jnp.int32)

        sel_f = sum(jnp.where(hot, 1.0, 0.0) for hot in hots)
        prefix = jnp.dot(sel_f.astype(jnp.bfloat16), before, preferred_element_type=jnp.float32)
        rank_e = carry[:, 0:1] + prefix
        ranks = [jnp.sum(jnp.where(hot, rank_e, 0.0), axis=0, keepdims=True) for hot in hots]
        rank_ref[:, rows] = jnp.concatenate(ranks, axis=0).astype(jnp.int32)
        carry = carry + jnp.sum(sel_f, axis=1, keepdims=True)
    carry_ref[...] = carry
    cnt_ref[...] = carry.astype(jnp.int32)


def _out_route(x2, ym, p, pool_w, pool_s, w_out, g2, wr_t, br, seq):
    T = x2.shape[0]
    TM = ROUTE_SUB * TM_PROJ
    nt = T // TM
    u_blk = N_MAIN // POOL_WIDTH - 1
    halo_per_tile = TM // HALO
    tok_spec = pl.BlockSpec((TOP_K, TM), lambda i: (0, i))
    return pl.pallas_call(
        functools.partial(_out_route_kernel, seq),
        grid=(nt,),
        in_specs=[
            pl.BlockSpec((TM, D_MODEL), lambda i: (i, 0)),
            pl.BlockSpec((TM, MLSTM_WIDTH), lambda i: (i, 0)),
            pl.BlockSpec((TM, POOL_WIDTH), lambda i: (i, u_blk)),
            pl.BlockSpec((HALO, POOL_WIDTH),
                         lambda i: (jnp.maximum(i * halo_per_tile - 1, 0), u_blk)),
            pl.BlockSpec((len(POOL_WINDOWS), POOL_GROUP_DIM, POOL_GROUP_DIM), lambda i: (0, 0, 0)),
            pl.BlockSpec((1, POOL_WIDTH), lambda i: (0, 0)),
            pl.BlockSpec((D_MODEL, D_MODEL), lambda i: (0, 0)),
            pl.BlockSpec((1, D_MODEL), lambda i: (0, 0)),
            pl.BlockSpec((N_EXPERTS, D_MODEL), lambda i: (0, 0)),
            pl.BlockSpec((N_EXPERTS, 1), lambda i: (0, 0)),
        ],
        out_specs=[
            pl.BlockSpec((TM, D_MODEL), lambda i: (i, 0)),
            pl.BlockSpec((TM * PSLAB, LANES), lambda i: (i, 0)),
            tok_spec, tok_spec, tok_spec,
            pl.BlockSpec((N_EXPERTS, LANES), lambda i: (0, 0)),
        ],
        out_shape=[
            jax.ShapeDtypeStruct((T, D_MODEL), jnp.float32),
            jax.ShapeDtypeStruct((T * PSLAB, LANES), jnp.uint32),
            jax.ShapeDtypeStruct((TOP_K, T), jnp.int32),
            jax.ShapeDtypeStruct((TOP_K, T), jnp.float32),
            jax.ShapeDtypeStruct((TOP_K, T), jnp.int32),
            jax.ShapeDtypeStruct((N_EXPERTS, LANES), jnp.int32),
        ],
        scratch_shapes=[
            pltpu.VMEM((N_EXPERTS, LANES), jnp.float32),
        ],
        compiler_params=pltpu.CompilerParams(
            dimension_semantics=("arbitrary",), vmem_limit_bytes=VMEM_LIMIT),
        name="out_route",
    )(x2, ym, p, p, pool_w, pool_s, w_out, g2, wr_t, br)


def _plan(dest_flat, fill):
    n_assign = dest_flat.shape[0]
    n_table = fill.shape[0]
    rows = n_table // PLAN_TILES
    mesh = plsc.VectorSubcoreMesh(core_axis_name="c", subcore_axis_name="s")

    @pl.kernel(out_type=jax.ShapeDtypeStruct((n_table,), jnp.int32), mesh=mesh,
               scratch_types=[pltpu.VMEM((rows,), jnp.int32),
                              pltpu.VMEM((PLAN_CHUNK,), jnp.int32)],
               compiler_params=pltpu.CompilerParams(needs_layout_passes=False))
    def plan_kernel(dest_hbm, fill_hbm, out_hbm, table, chunk):
        sub = lax.axis_index("s")
        lanes = lax.broadcasted_iota(jnp.int32, (SC_LANES,), 0)

        @pl.when(jnp.logical_and(lax.axis_index("c") == 0, sub < PLAN_TILES))
        def _():
            base = sub * rows
            pltpu.sync_copy(fill_hbm.at[pl.ds(base, rows)], table)

            @pl.loop(0, n_assign // PLAN_CHUNK)
            def _(ci):
                pltpu.sync_copy(dest_hbm.at[pl.ds(ci * PLAN_CHUNK, PLAN_CHUNK)], chunk)

                @pl.loop(0, PLAN_CHUNK // (SC_LANES * PLAN_UNROLL))
                def _(i):
                    for j in range(PLAN_UNROLL):
                        off = (i * PLAN_UNROLL + j) * SC_LANES
                        idx = chunk[pl.ds(off, SC_LANES)] - base
                        mine = jnp.logical_and(idx >= 0, idx < rows)
                        vals = ci * PLAN_CHUNK + off + lanes
                        plsc.store_scatter(table, [jnp.where(mine, idx, 0)], vals, mask=mine)

            pltpu.sync_copy(table, out_hbm.at[pl.ds(base, rows)])

    return plan_kernel(dest_flat, fill)


def _expert_kernel(n_tok, bs_ref, slot_ref, h2_ref, wg_ref, bg_ref, wu_ref, bu_ref, wd_ref, bd_ref,
                   yt_ref, *scratch):
    TM = TM_EXPERT
    ROWS = TM * PSLAB
    e = pl.program_id(0)
    n_total = bs_ref[N_EXPERTS]
    xg = scratch[:NBUF]
    ys = scratch[NBUF:2 * NBUF]
    wgb_ref, wub_ref, wdb_ref, gsem, ssem = scratch[2 * NBUF:]

    def token_of(a):
        return a & (n_tok - 1) if n_tok & (n_tok - 1) == 0 else lax.rem(a, n_tok)

    def start_gather(blk, par):
        base = (blk + 1) * TM
        for r in range(TM):
            t = token_of(slot_ref[base + r])
            pltpu.make_async_copy(h2_ref.at[pl.ds(pl.multiple_of(t * PSLAB, PSLAB), PSLAB), :],
                                  xg[par].at[pl.ds(r * PSLAB, PSLAB), :], gsem.at[par]
                                  ).start(priority=ROW_DMA_PRIORITY)

    def wait_gather(par):
        pltpu.make_async_copy(h2_ref.at[pl.ds(0, ROWS), :], xg[0], gsem.at[par]).wait()

    def start_scatter(blk, par):
        base = (blk + 1) * TM
        for r in range(TM):
            a = slot_ref[base + r]
            pltpu.make_async_copy(ys[par].at[pl.ds(r * PSLAB, PSLAB), :],
                                  yt_ref.at[pl.ds(pl.multiple_of(a * PSLAB, PSLAB), PSLAB), :],
                                  ssem.at[par]).start(priority=ROW_DMA_PRIORITY)

    def wait_scatter(par):
        pltpu.make_async_copy(ys[0], yt_ref.at[pl.ds(0, ROWS), :], ssem.at[par]).wait()

    @pl.when(e == 0)
    def _():
        for blk in range(NBUF - 1):
            start_gather(blk, blk)
        for par in range(NBUF):
            ys[par][...] = jnp.zeros_like(ys[par])
            dump = yt_ref.at[pl.ds((n_tok * TOP_K + par * TM) * PSLAB, ROWS), :]
            cp = pltpu.make_async_copy(ys[par], dump, ssem.at[par])
            cp.start()
            cp.wait()

    wgb_ref[...] = wg_ref[0].astype(jnp.bfloat16)
    wub_ref[...] = wu_ref[0].astype(jnp.bfloat16)
    wdb_ref[...] = wd_ref[0].astype(jnp.bfloat16)

    def block_step(g, par):
        prv = (par + NBUF - 1) % NBUF
        wait_gather(par)

        @pl.when(g >= NBUF - 1)
        def _():
            wait_scatter(par)

        start_gather(g + NBUF - 1, prv)
        start_scatter(g - 1, prv)
        words = [xg[par][pl.ds(s, TM, stride=PSLAB), :] for s in range(PSLAB)]
        x = jnp.concatenate([_unpack_lo(w).astype(jnp.bfloat16) for w in words]
                            + [_unpack_hi(w).astype(jnp.bfloat16) for w in words], axis=1)
        gate = jnp.dot(x, wgb_ref[...], preferred_element_type=jnp.float32) + bg_ref[0]
        up = jnp.dot(x, wub_ref[...], preferred_element_type=jnp.float32) + bu_ref[0]
        gate = jnp.minimum(gate, SWIGLU_LIMIT)
        up = jnp.clip(up, -SWIGLU_LIMIT, SWIGLU_LIMIT)
        glu = gate * _sigmoid(SWIGLU_ALPHA * gate)
        act = (glu * (up + 1.0)).astype(jnp.bfloat16)
        y = jnp.dot(act, wdb_ref[...], preferred_element_type=jnp.float32) + bd_ref[0]
        packed = _pack_bf16_pairs(y)
        for s in range(PSLAB):
            ys[par][pl.ds(s, TM, stride=PSLAB), :] = packed[:, s * LANES:(s + 1) * LANES]

    def body(g, carry):
        for par in range(NBUF):
            pl.when(g % NBUF == par)(functools.partial(block_step, g, par))
        return carry

    lax.fori_loop(bs_ref[e], bs_ref[e + 1], body, 0)

    @pl.when(e == N_EXPERTS - 1)
    def _():
        g = n_total
        for par in range(NBUF):
            @pl.when((g - 1) % NBUF == par)
            def _():
                start_scatter(g - 1, par)
        for j in range(NBUF - 1):
            wait_gather((g + j) % NBUF)
        wait_scatter((g - 1) % NBUF)
        for j in range(2, NBUF + 1):
            @pl.when(g >= j - 1)
            def _():
                wait_scatter((g + NBUF - j) % NBUF)


def _experts(block_start, slot_buf, h2_slab, w_gate, b_gate, w_up, b_up, w_down, b_down, n_tok):
    TM = TM_EXPERT
    n_assign = n_tok * TOP_K
    w_spec = pl.BlockSpec((1, D_MODEL, D_FF), lambda e, bs, sl: (e, 0, 0))
    bias_spec = pl.BlockSpec((1, 1, D_FF), lambda e, bs, sl: (e, 0, 0))
    buf = pltpu.VMEM((TM * PSLAB, LANES), jnp.uint32)
    grid_spec = pltpu.PrefetchScalarGridSpec(
        num_scalar_prefetch=2,
        grid=(N_EXPERTS,),
        in_specs=[
            pl.BlockSpec(memory_space=pl.ANY),
            w_spec, bias_spec, w_spec, bias_spec, w_spec, bias_spec,
        ],
        out_specs=pl.BlockSpec(memory_space=pl.ANY),
        scratch_shapes=[
            *([buf] * (2 * NBUF)),
            pltpu.VMEM((D_MODEL, D_FF), jnp.bfloat16),
            pltpu.VMEM((D_MODEL, D_FF), jnp.bfloat16),
            pltpu.VMEM((D_FF, D_MODEL), jnp.bfloat16),
            pltpu.SemaphoreType.DMA((NBUF,)),
            pltpu.SemaphoreType.DMA((NBUF,)),
        ],
    )
    return pl.pallas_call(
        functools.partial(_expert_kernel, n_tok),
        grid_spec=grid_spec,
        out_shape=jax.ShapeDtypeStruct(((n_assign + NBUF * TM) * PSLAB, LANES), jnp.uint32),
        compiler_params=pltpu.CompilerParams(
            dimension_semantics=("arbitrary",), vmem_limit_bytes=VMEM_LIMIT),
        name="experts",
    )(block_start, slot_buf, h2_slab, w_gate, b_gate, w_up, b_up, w_down, b_down)


def _combine_kernel(normalize, x1_ref, y0_ref, y1_ref, y2_ref, y3_ref, gate_ref, g_ref, o_ref):
    TM = TM_PROJ
    gates = jnp.concatenate([gate_ref[...], jnp.zeros((8 - TOP_K, TM), jnp.float32)], axis=0)
    g_cols = jnp.transpose(gates)
    g_bc = [jnp.broadcast_to(g_cols[:, k:k + 1], (TM, LANES)) for k in range(TOP_K)]
    ssq = jnp.zeros((TM, LANES), jnp.float32)
    parts = [x1_ref[:, s * LANES:(s + 1) * LANES] for s in range(SLAB)]
    for s in range(PSLAB):
        for k, y_ref in enumerate((y0_ref, y1_ref, y2_ref, y3_ref)):
            w = y_ref[pl.ds(s, TM, stride=PSLAB), :]
            parts[s] = parts[s] + g_bc[k] * _unpack_lo(w)
            parts[PSLAB + s] = parts[PSLAB + s] + g_bc[k] * _unpack_hi(w)
    for acc in parts:
        ssq = ssq + acc * acc
    if normalize:
        inv = lax.rsqrt(jnp.sum(ssq, axis=-1, keepdims=True) * (1.0 / D_MODEL) + EPS)
        for s in range(SLAB):
            o_ref[:, s * LANES:(s + 1) * LANES] = parts[s] * inv * g_ref[:, s * LANES:(s + 1) * LANES]
    else:
        for s in range(SLAB):
            o_ref[:, s * LANES:(s + 1) * LANES] = parts[s]


def _combine(x1, y_tok, gate_t, gf, normalize):
    T = x1.shape[0]
    TM = TM_PROJ
    nt = T // TM

    def y_spec(k):
        return pl.BlockSpec((TM * PSLAB, LANES), lambda i: (k * nt + i, 0))

    return pl.pallas_call(
        functools.partial(_combine_kernel, normalize),
        grid=(nt,),
        in_specs=[
            pl.BlockSpec((TM, D_MODEL), lambda i: (i, 0)),
            y_spec(0), y_spec(1), y_spec(2), y_spec(3),
            pl.BlockSpec((TOP_K, TM), lambda i: (0, i)),
            pl.BlockSpec((1, D_MODEL), lambda i: (0, 0)),
        ],
        out_specs=pl.BlockSpec((TM, D_MODEL), lambda i: (i, 0)),
        out_shape=jax.ShapeDtypeStruct((T, D_MODEL), jnp.float32),
        compiler_params=pltpu.CompilerParams(
            dimension_semantics=("parallel",), vmem_limit_bytes=VMEM_LIMIT),
        name="combine",
    )(x1, y_tok, y_tok, y_tok, y_tok, gate_t, gf)


def kernel(x, norm1_g, w_in, ig_b, fg_b, conv_w, head_norm_g, pool_w, pool_scale, w_out, norm2_g,
           w_router, b_router, w_gate, b_gate, w_up, b_up, w_down, b_down, normf_g):
    B, S, D = x.shape
    T = B * S
    depth = norm1_g.shape[0]
    W = MLSTM_WIDTH
    f32, bf16 = jnp.float32, jnp.bfloat16

    L = CHUNK
    t_l = lax.broadcasted_iota(jnp.int32, (L, L), 0)
    t_r = lax.broadcasted_iota(jnp.int32, (L, L), 1)
    tri = (t_r <= t_l).astype(f32)
    shifts = jnp.stack([(t_l - t_r == CONV_WIDTH - 1 - j).astype(bf16)
                        for j in range(CONV_WIDTH - 1)])
    h_t = lax.broadcasted_iota(jnp.int32, (8, HALO), 0)
    h_r = lax.broadcasted_iota(jnp.int32, (8, HALO), 1)
    halo_shifts = jnp.stack([(h_r - HALO - h_t == -(CONV_WIDTH - 1 - j)).astype(bf16)
                             for j in range(CONV_WIDTH - 1)])

    n_assign = T * TOP_K
    n_blocks = -(-n_assign // TM_EXPERT) + N_EXPERTS
    n_rows = n_blocks * TM_EXPERT
    n_table = n_rows + NBUF * TM_EXPERT
    fill = n_assign + ((jnp.arange(n_table, dtype=jnp.int32) + (NBUF - 1) * TM_EXPERT)
                       % (NBUF * TM_EXPERT))
    x2 = x.reshape(T, D)
    for l in range(depth):
        w = w_in[l]
        w_a = w[:, :4 * W].astype(bf16)
        w_u = w[:, 4 * W + N_GATES:].astype(bf16)
        wg_t = jnp.zeros((BF16_SUBLANES, D), bf16).at[:N_GATES].set(
            w[:, 4 * W:4 * W + N_GATES].T.astype(bf16))
        p, gates_t = _in_proj(x2, norm1_g[l][None, :], w_a, w_u, wg_t)

        gate_b = jnp.concatenate([ig_b[l], fg_b[l]])[:, None].astype(f32)
        gates_b = gates_t.reshape(N_GATES, B, S).transpose(1, 0, 2)
        ym = _mlstm(p.reshape(B, S, N_MAIN), gates_b, conv_w[l].astype(f32), gate_b,
                    head_norm_g[l][None, :], tri, shifts, halo_shifts).reshape(T, W)

        x1, h2, idx_t, gate_t, rank_t, cnt = _out_route(
            x2, ym, p, pool_w[l].astype(bf16), pool_scale[l][None, :], w_out[l].astype(bf16),
            norm2_g[l][None, :], w_router[l].T.astype(bf16), b_router[l][:, None], S)

        counts = cnt[:, 0]
        padded = ((counts + TM_EXPERT - 1) // TM_EXPERT) * TM_EXPERT
        padded_end = jnp.cumsum(padded)
        padded_start = padded_end - padded
        expert_ids = jnp.arange(N_EXPERTS, dtype=jnp.int32)[:, None, None]
        start_of = jnp.sum(jnp.where(idx_t[None] == expert_ids, padded_start[:, None, None], 0), axis=0)
        dest = start_of + rank_t
        block_start = jnp.concatenate(
            [jnp.zeros((1,), jnp.int32), (padded_end // TM_EXPERT).astype(jnp.int32)])

        slot_buf = _plan(dest.reshape(-1) + TM_EXPERT, fill)
        y_tok = _experts(block_start, slot_buf, h2, w_gate[l], b_gate[l][:, None, :],
                         w_up[l], b_up[l][:, None, :], w_down[l], b_down[l][:, None, :], T)
        last = l + 1 == depth
        x2 = _combine(x1, y_tok, gate_t, normf_g[None, :], last)
    return x2.reshape(B, S, D)
```

```python
import functools

import jax
import jax.numpy as jnp
from jax import lax
from jax.experimental import pallas as pl
from jax.experimental.pallas import tpu as pltpu
from jax.experimental.pallas import tpu_sc as plsc

D_MODEL = 1024
MLSTM_WIDTH = 512
MLSTM_HEADS = 4
HEAD_DIM = 128
CONV_WIDTH = 4
POOL_WIDTH = 512
POOL_WINDOWS = (2, 4, 8, 16)
POOL_GROUP_DIM = 128
N_EXPERTS = 32
TOP_K = 4
D_FF = 1024
SWIGLU_LIMIT = 7.0
SWIGLU_ALPHA = 1.702
EPS = 1e-5

N_MAIN = 4 * MLSTM_WIDTH + POOL_WIDTH
N_GATES = 2 * MLSTM_HEADS

LANES = 128
BF16_SUBLANES = 16
VMEM_LIMIT = 56 * 1024 * 1024

TM_PROJ = 512
ROUTE_SUB = 2
CHUNK = 256
MLSTM_BATCH = 2
HALO = 16
TM_EXPERT = 512
NBUF = 2
ROW_DMA_PRIORITY = 1
SLAB = D_MODEL // LANES
PSLAB = SLAB // 2
PLAN_CHUNK = 8192
SC_LANES = 16
PLAN_UNROLL = 8

NT_DIMS = (((1,), (1,)), ((), ()))


def _sigmoid(x):
    return 1.0 / (1.0 + jnp.exp(-x))


def _pack_bf16_pairs(v):
    half = v.shape[1] // 2
    lo = pltpu.bitcast(v[:, :half].astype(jnp.bfloat16).astype(jnp.float32), jnp.uint32)
    hi = pltpu.bitcast(v[:, half:].astype(jnp.bfloat16).astype(jnp.float32), jnp.uint32)
    return (lo >> 16) | (hi & jnp.uint32(0xFFFF0000))


def _unpack_lo(w):
    return pltpu.bitcast(w << 16, jnp.float32)


def _unpack_hi(w):
    return pltpu.bitcast(w & jnp.uint32(0xFFFF0000), jnp.float32)


def _in_proj_kernel(x_ref, g_ref, wa_ref, wu_ref, wgt_ref, p_ref, gt_ref):
    x = x_ref[...]
    h = x * lax.rsqrt(jnp.mean(x * x, axis=-1, keepdims=True) + EPS) * g_ref[...]
    hb = h.astype(jnp.bfloat16)
    n_a = wa_ref.shape[1]
    p_ref[:, :n_a] = jnp.dot(hb, wa_ref[...], preferred_element_type=jnp.float32).astype(p_ref.dtype)
    p_ref[:, n_a:] = jnp.dot(hb, wu_ref[...], preferred_element_type=jnp.float32).astype(p_ref.dtype)
    gt = lax.dot_general(wgt_ref[...], hb, NT_DIMS, preferred_element_type=jnp.float32)
    gt_ref[...] = gt[:N_GATES]


def _in_proj(x2, g1, w_a, w_u, wg_t):
    T = x2.shape[0]
    return pl.pallas_call(
        _in_proj_kernel,
        grid=(T // TM_PROJ,),
        in_specs=[
            pl.BlockSpec((TM_PROJ, D_MODEL), lambda i: (i, 0)),
            pl.BlockSpec((1, D_MODEL), lambda i: (0, 0)),
            pl.BlockSpec(w_a.shape, lambda i: (0, 0)),
            pl.BlockSpec(w_u.shape, lambda i: (0, 0)),
            pl.BlockSpec((BF16_SUBLANES, D_MODEL), lambda i: (0, 0)),
        ],
        out_specs=[
            pl.BlockSpec((TM_PROJ, N_MAIN), lambda i: (i, 0)),
            pl.BlockSpec((N_GATES, TM_PROJ), lambda i: (0, i)),
        ],
        out_shape=[
            jax.ShapeDtypeStruct((T, N_MAIN), jnp.bfloat16),
            jax.ShapeDtypeStruct((N_GATES, T), jnp.float32),
        ],
        compiler_params=pltpu.CompilerParams(
            dimension_semantics=("parallel",), vmem_limit_bytes=VMEM_LIMIT),
        name="in_proj",
    )(x2, g1, w_a, w_u, wg_t)


def _mlstm_kernel(qk_ref, qkp_ref, v_ref, o_ref, gt_ref, convw_ref, gb_ref, hng_ref,
                  tri_ref, shift_ref, hshift_ref, y_ref, cn_ref, m_ref):
    L = CHUNK
    c = pl.program_id(1)

    @pl.when(c == 0)
    def _():
        cn_ref[...] = jnp.zeros_like(cn_ref)
        m_ref[...] = jnp.zeros_like(m_ref)

    row_id = lax.broadcasted_iota(jnp.int32, (L, L), 0)
    col_id = lax.broadcasted_iota(jnp.int32, (L, L), 1)
    causal = col_id <= row_id
    ones_blk = jnp.ones((L, HEAD_DIM), jnp.bfloat16)
    lane = lax.broadcasted_iota(jnp.int32, (MLSTM_HEADS, L), 1)

    gate_terms = []
    for bb in range(MLSTM_BATCH):
        gt = gt_ref[bb] + gb_ref[...]
        f = gt[MLSTM_HEADS:]
        lf = jnp.minimum(f, 0.0) - jnp.log(1.0 + jnp.exp(-jnp.abs(f)))
        ig = gt[:MLSTM_HEADS]
        b_rows = lax.dot_general(lf, tri_ref[...], NT_DIMS, precision=lax.Precision.HIGHEST,
                                 preferred_element_type=jnp.float32)
        c_rows = ig - b_rows
        cm_rows = c_rows
        d = 1
        while d < L:
            cm_rows = jnp.maximum(
                cm_rows, jnp.where(lane >= d, pltpu.roll(cm_rows, d, axis=1), -jnp.inf))
            d *= 2
        gate_terms.append((b_rows, c_rows, cm_rows))

    conv_terms = []
    for bb in range(MLSTM_BATCH):
        x_cur = qk_ref[bb]
        x_prev = jnp.where(c > 0, qkp_ref[bb], jnp.zeros((HALO, 2 * MLSTM_WIDTH), jnp.bfloat16))
        acc = convw_ref[CONV_WIDTH - 1:CONV_WIDTH, :] * x_cur.astype(jnp.float32)
        for j in range(CONV_WIDTH - 1):
            sh = jnp.dot(shift_ref[j], x_cur, preferred_element_type=jnp.float32)
            top = sh[:8] + jnp.dot(hshift_ref[j], x_prev, preferred_element_type=jnp.float32)
            sh = jnp.concatenate([top, sh[8:]], axis=0)
            acc = acc + convw_ref[j:j + 1, :] * sh
        qk = acc * _sigmoid(acc)
        q_all = qk[:, :MLSTM_WIDTH].astype(jnp.bfloat16)
        k_t = jnp.transpose(qk[:, MLSTM_WIDTH:] * (HEAD_DIM ** -0.5))
        conv_terms.append((q_all, k_t))

    for bb in range(MLSTM_BATCH):
        b_rows, c_rows, cm_rows = gate_terms[bb]
        q_all, k_t = conv_terms[bb]
        m_in4 = jnp.concatenate(
            [m_ref[bb * MLSTM_HEADS + h][0:1, 0:1] for h in range(MLSTM_HEADS)], axis=0)
        mx_rows = jnp.maximum(cm_rows, m_in4)
        inter_rows = jnp.exp(m_in4 - mx_rows)
        einv_rows = jnp.exp(-(b_rows + mx_rows))
        fac_t = jnp.transpose(jnp.concatenate(
            [mx_rows, inter_rows, einv_rows, jnp.zeros_like(mx_rows)], axis=0))

        for h in range(MLSTM_HEADS):
            lo = h * HEAD_DIM
            st = bb * MLSTM_HEADS + h
            q = q_all[:, lo:lo + HEAD_DIM]
            kt = k_t[lo:lo + HEAD_DIM, :]
            v_ext = jnp.concatenate([v_ref[bb, :, lo:lo + HEAD_DIM], ones_blk], axis=1)
            mx_col = fac_t[:, h:h + 1]
            inter_col = fac_t[:, MLSTM_HEADS + h:MLSTM_HEADS + h + 1]
            einv_col = fac_t[:, 2 * MLSTM_HEADS + h:2 * MLSTM_HEADS + h + 1]
            c_row = c_rows[h:h + 1, :]
            b_tot = b_rows[h:h + 1, L - 1:L]
            cm_tot = cm_rows[h:h + 1, L - 1:L]
            m_in = m_ref[st][0:1, 0:1]
            cn = cn_ref[st]

            s_qk = jnp.dot(q, kt.astype(jnp.bfloat16), preferred_element_type=jnp.float32)
            s = (s_qk * jnp.exp(jnp.where(causal, c_row - mx_col, -jnp.inf))).astype(jnp.bfloat16)
            num = (jnp.dot(s, v_ext, preferred_element_type=jnp.float32)
                   + inter_col * jnp.dot(q, cn.astype(jnp.bfloat16),
                                         preferred_element_type=jnp.float32))
            den = num[:, HEAD_DIM:]
            hh = num[:, :HEAD_DIM] / jnp.maximum(jnp.abs(den), einv_col)

            mu = jnp.mean(hh, axis=-1, keepdims=True)
            dv = hh - mu
            var = jnp.mean(dv * dv, axis=-1, keepdims=True)
            hn = dv * lax.rsqrt(var + EPS) * hng_ref[:, lo:lo + HEAD_DIM]
            og = _sigmoid(o_ref[bb, :, lo:lo + HEAD_DIM].astype(jnp.float32))
            y_ref[bb, :, lo:lo + HEAD_DIM] = (og * hn).astype(y_ref.dtype)

            m_loc = b_tot + cm_tot
            kw_t = (kt * jnp.exp(c_row - cm_tot)).astype(jnp.bfloat16)
            c_loc = jnp.dot(kw_t, v_ext, preferred_element_type=jnp.float32)
            m_new = jnp.maximum(b_tot + m_in, m_loc)
            s_old = jnp.exp(b_tot + m_in - m_new)
            s_loc = jnp.exp(m_loc - m_new)
            cn_ref[st] = s_old * cn + s_loc * c_loc
            m_ref[st] = jnp.broadcast_to(m_new, m_ref.shape[1:])


def _mlstm(p3, gates_b, conv_w, gate_b, hn_g, tri, shifts, halo_shifts):
    batch, seq, _ = p3.shape
    L = CHUNK
    BB = MLSTM_BATCH
    halo_per_chunk = L // HALO
    return pl.pallas_call(
        _mlstm_kernel,
        grid=(batch // BB, seq // L),
        in_specs=[
            pl.BlockSpec((BB, L, 2 * MLSTM_WIDTH), lambda bi, ci: (bi, ci, 0)),
            pl.BlockSpec((BB, HALO, 2 * MLSTM_WIDTH),
                         lambda bi, ci: (bi, jnp.maximum(ci * halo_per_chunk - 1, 0), 0)),
            pl.BlockSpec((BB, L, MLSTM_WIDTH), lambda bi, ci: (bi, ci, 2)),
            pl.BlockSpec((BB, L, MLSTM_WIDTH), lambda bi, ci: (bi, ci, 3)),
            pl.BlockSpec((BB, N_GATES, L), lambda bi, ci: (bi, 0, ci)),
            pl.BlockSpec((CONV_WIDTH, 2 * MLSTM_WIDTH), lambda bi, ci: (0, 0)),
            pl.BlockSpec((N_GATES, 1), lambda bi, ci: (0, 0)),
            pl.BlockSpec((1, MLSTM_WIDTH), lambda bi, ci: (0, 0)),
            pl.BlockSpec((L, L), lambda bi, ci: (0, 0)),
            pl.BlockSpec((CONV_WIDTH - 1, L, L), lambda bi, ci: (0, 0, 0)),
            pl.BlockSpec((CONV_WIDTH - 1, 8, HALO), lambda bi, ci: (0, 0, 0)),
        ],
        out_specs=pl.BlockSpec((BB, L, MLSTM_WIDTH), lambda bi, ci: (bi, ci, 0)),
        out_shape=jax.ShapeDtypeStruct((batch, seq, MLSTM_WIDTH), jnp.bfloat16),
        scratch_shapes=[
            pltpu.VMEM((BB * MLSTM_HEADS, HEAD_DIM, 2 * HEAD_DIM), jnp.float32),
            pltpu.VMEM((BB * MLSTM_HEADS, 8, LANES), jnp.float32),
        ],
        compiler_params=pltpu.CompilerParams(
            dimension_semantics=("parallel", "arbitrary"), vmem_limit_bytes=VMEM_LIMIT),
        name="mlstm",
    )(p3, p3, p3, p3, gates_b, conv_w, gate_b, hn_g, tri, shifts, halo_shifts)


def _out_route_kernel(seq, x_ref, ym_ref, u_ref, up_ref, pw_ref, ps_ref, wo_ref, g2_ref,
                      wrt_ref, br_ref, x1_ref, h2_ref, idx_ref, gate_ref, rank_ref, cnt_ref,
                      carry_ref):
    TM = TM_PROJ
    R = ROUTE_SUB * TM
    i = pl.program_id(0)

    @pl.when(i == 0)
    def _():
        carry_ref[...] = jnp.zeros_like(carry_ref)

    pos0 = (i * R) % seq
    e_id = lax.broadcasted_iota(jnp.int32, (N_EXPERTS, TM), 0).astype(jnp.float32)
    t_row = lax.broadcasted_iota(jnp.int32, (TM, TM), 0)
    t_col = lax.broadcasted_iota(jnp.int32, (TM, TM), 1)
    before = jnp.where(t_row < t_col, 1.0, 0.0).astype(jnp.bfloat16)
    carry = carry_ref[...]
    subs = [slice(sub * TM, (sub + 1) * TM) for sub in range(ROUTE_SUB)]

    halo = jnp.where(pos0 > 0, up_ref[...].astype(jnp.float32), 0.0)
    u_ext = jnp.concatenate([halo, u_ref[...].astype(jnp.float32)], axis=0)
    win_sums = []
    for gi, w in enumerate(POOL_WINDOWS):
        sw = u_ext[:, gi * POOL_GROUP_DIM:(gi + 1) * POOL_GROUP_DIM]
        span = 1
        while span < w:
            sw = sw + pltpu.roll(sw, span, axis=0)
            span *= 2
        win_sums.append(sw)
    y_cats = []
    for sub, rows in enumerate(subs):
        r0 = sub * TM
        pos = (pos0 + r0 + lax.broadcasted_iota(jnp.int32, (TM, 1), 0) + 1).astype(jnp.float32)
        mixed = []
        for gi, w in enumerate(POOL_WINDOWS):
            lo = gi * POOL_GROUP_DIM
            tok = u_ext[HALO + r0:HALO + r0 + TM, lo:lo + POOL_GROUP_DIM]
            pooled = win_sums[gi][HALO + r0:HALO + r0 + TM] / jnp.minimum(pos, float(w)) - tok
            mg = jnp.dot(pooled.astype(jnp.bfloat16), pw_ref[gi],
                         preferred_element_type=jnp.float32)
            mixed.append((mg * ps_ref[:, lo:lo + POOL_GROUP_DIM]).astype(jnp.bfloat16))
        y_cats.append(jnp.concatenate([ym_ref[rows, :]] + mixed, axis=1))

    all_logits = []
    for sub, rows in enumerate(subs):
        r0 = sub * TM
        x1 = x_ref[rows, :] + jnp.dot(y_cats[sub], wo_ref[...], preferred_element_type=jnp.float32)
        x1_ref[rows, :] = x1
        h2 = x1 * lax.rsqrt(jnp.mean(x1 * x1, axis=-1, keepdims=True) + EPS) * g2_ref[...]
        h2b = h2.astype(jnp.bfloat16)
        h2w = _pack_bf16_pairs(h2)
        for s in range(PSLAB):
            h2_ref[pl.ds(r0 * PSLAB + s, TM, stride=PSLAB), :] = h2w[:, s * LANES:(s + 1) * LANES]
        all_logits.append(lax.dot_general(wrt_ref[...], h2b, NT_DIMS,
                                          preferred_element_type=jnp.float32) + br_ref[...])

    for sub, rows in enumerate(subs):
        work = all_logits[sub]
        vals, ids, hots = [], [], []
        for _ in range(TOP_K):
            mk = jnp.max(work, axis=0, keepdims=True)
            ik = jnp.min(jnp.where(work == mk, e_id, float(N_EXPERTS)), axis=0, keepdims=True)
            hot = e_id == ik
            work = jnp.where(hot, -jnp.inf, work)
            vals.append(mk)
            ids.append(ik)
            hots.append(hot)
        ex = [jnp.exp(vk - vals[0]) for vk in vals]
        denom = ex[0] + ex[1] + ex[2] + ex[3]
        gate_ref[:, rows] = jnp.concatenate([e / denom for e in ex], axis=0)
        idx_ref[:, rows] = jnp.concatenate(ids, axis=0).astype(jnp.int32)

        sel_f = sum(jnp.where(hot, 1.0, 0.0) for hot in hots)
        prefix = jnp.dot(sel_f.astype(jnp.bfloat16), before, preferred_element_type=jnp.float32)
        rank_e = carry[:, 0:1] + prefix
        ranks = [jnp.sum(jnp.where(hot, rank_e, 0.0), axis=0, keepdims=True) for hot in hots]
        rank_ref[:, rows] = jnp.concatenate(ranks, axis=0).astype(jnp.int32)
        carry = carry + jnp.sum(sel_f, axis=1, keepdims=True)
    carry_ref[...] = carry
    cnt_ref[...] = carry.astype(jnp.int32)


def _out_route(x2, ym, p, pool_w, pool_s, w_out, g2, wr_t, br, seq):
    T = x2.shape[0]
    TM = ROUTE_SUB * TM_PROJ
    nt = T // TM
    u_blk = N_MAIN // POOL_WIDTH - 1
    halo_per_tile = TM // HALO
    tok_spec = pl.BlockSpec((TOP_K, TM), lambda i: (0, i))
    return pl.pallas_call(
        functools.partial(_out_route_kernel, seq),
        grid=(nt,),
        in_specs=[
            pl.BlockSpec((TM, D_MODEL), lambda i: (i, 0)),
            pl.BlockSpec((TM, MLSTM_WIDTH), lambda i: (i, 0)),
            pl.BlockSpec((TM, POOL_WIDTH), lambda i: (i, u_blk)),
            pl.BlockSpec((HALO, POOL_WIDTH),
                         lambda i: (jnp.maximum(i * halo_per_tile - 1, 0), u_blk)),
            pl.BlockSpec((len(POOL_WINDOWS), POOL_GROUP_DIM, POOL_GROUP_DIM), lambda i: (0, 0, 0)),
            pl.BlockSpec((1, POOL_WIDTH), lambda i: (0, 0)),
            pl.BlockSpec((D_MODEL, D_MODEL), lambda i: (0, 0)),
            pl.BlockSpec((1, D_MODEL), lambda i: (0, 0)),
            pl.BlockSpec((N_EXPERTS, D_MODEL), lambda i: (0, 0)),
            pl.BlockSpec((N_EXPERTS, 1), lambda i: (0, 0)),
        ],
        out_specs=[
            pl.BlockSpec((TM, D_MODEL), lambda i: (i, 0)),
            pl.BlockSpec((TM * PSLAB, LANES), lambda i: (i, 0)),
            tok_spec, tok_spec, tok_spec,
            pl.BlockSpec((N_EXPERTS, LANES), lambda i: (0, 0)),
        ],
        out_shape=[
            jax.ShapeDtypeStruct((T, D_MODEL), jnp.float32),
            jax.ShapeDtypeStruct((T * PSLAB, LANES), jnp.uint32),
            jax.ShapeDtypeStruct((TOP_K, T), jnp.int32),
            jax.ShapeDtypeStruct((TOP_K, T), jnp.float32),
            jax.ShapeDtypeStruct((TOP_K, T), jnp.int32),
            jax.ShapeDtypeStruct((N_EXPERTS, LANES), jnp.int32),
        ],
        scratch_shapes=[
            pltpu.VMEM((N_EXPERTS, LANES), jnp.float32),
        ],
        compiler_params=pltpu.CompilerParams(
            dimension_semantics=("arbitrary",), vmem_limit_bytes=VMEM_LIMIT),
        name="out_route",
    )(x2, ym, p, p, pool_w, pool_s, w_out, g2, wr_t, br)


def _plan(dest_flat, fill):
    n_assign = dest_flat.shape[0]
    n_table = fill.shape[0]
    mesh = plsc.VectorSubcoreMesh(core_axis_name="c", subcore_axis_name="s")

    @pl.kernel(out_type=jax.ShapeDtypeStruct((n_table,), jnp.int32), mesh=mesh,
               scratch_types=[pltpu.VMEM((n_table,), jnp.int32),
                              pltpu.VMEM((PLAN_CHUNK,), jnp.int32)],
               compiler_params=pltpu.CompilerParams(needs_layout_passes=False))
    def plan_kernel(dest_hbm, fill_hbm, out_hbm, table, chunk):
        first = jnp.logical_and(lax.axis_index("c") == 0, lax.axis_index("s") == 0)

        @pl.when(first)
        def _():
            pltpu.sync_copy(fill_hbm, table)

            @pl.loop(0, n_assign // PLAN_CHUNK)
            def _(ci):
                pltpu.sync_copy(dest_hbm.at[pl.ds(ci * PLAN_CHUNK, PLAN_CHUNK)], chunk)

                @pl.loop(0, PLAN_CHUNK // (SC_LANES * PLAN_UNROLL))
                def _(i):
                    for j in range(PLAN_UNROLL):
                        off = (i * PLAN_UNROLL + j) * SC_LANES
                        idx = chunk[pl.ds(off, SC_LANES)]
                        vals = (ci * PLAN_CHUNK + off
                                + lax.broadcasted_iota(jnp.int32, (SC_LANES,), 0))
                        plsc.store_scatter(table, [idx], vals)

            pltpu.sync_copy(table, out_hbm)

    return plan_kernel(dest_flat, fill)


def _expert_kernel(n_tok, bs_ref, slot_ref, h2_ref, wg_ref, bg_ref, wu_ref, bu_ref, wd_ref, bd_ref,
                   yt_ref, *scratch):
    TM = TM_EXPERT
    ROWS = TM * PSLAB
    e = pl.program_id(0)
    n_total = bs_ref[N_EXPERTS]
    xg = scratch[:NBUF]
    ys = scratch[NBUF:2 * NBUF]
    wgb_ref, wub_ref, wdb_ref, gsem, ssem = scratch[2 * NBUF:]

    def token_of(a):
        return a & (n_tok - 1) if n_tok & (n_tok - 1) == 0 else lax.rem(a, n_tok)

    def start_gather(blk, par):
        base = (blk + 1) * TM
        for r in range(TM):
            t = token_of(slot_ref[base + r])
            pltpu.make_async_copy(h2_ref.at[pl.ds(pl.multiple_of(t * PSLAB, PSLAB), PSLAB), :],
                                  xg[par].at[pl.ds(r * PSLAB, PSLAB), :], gsem.at[par]
                                  ).start(priority=ROW_DMA_PRIORITY)

    def wait_gather(par):
        pltpu.make_async_copy(h2_ref.at[pl.ds(0, ROWS), :], xg[0], gsem.at[par]).wait()

    def start_scatter(blk, par):
        base = (blk + 1) * TM
        for r in range(TM):
            a = slot_ref[base + r]
            pltpu.make_async_copy(ys[par].at[pl.ds(r * PSLAB, PSLAB), :],
                                  yt_ref.at[pl.ds(pl.multiple_of(a * PSLAB, PSLAB), PSLAB), :],
                                  ssem.at[par]).start(priority=ROW_DMA_PRIORITY)

    def wait_scatter(par):
        pltpu.make_async_copy(ys[0], yt_ref.at[pl.ds(0, ROWS), :], ssem.at[par]).wait()

    @pl.when(e == 0)
    def _():
        for blk in range(NBUF - 1):
            start_gather(blk, blk)
        for par in range(NBUF):
            ys[par][...] = jnp.zeros_like(ys[par])
            dump = yt_ref.at[pl.ds((n_tok * TOP_K + par * TM) * PSLAB, ROWS), :]
            cp = pltpu.make_async_copy(ys[par], dump, ssem.at[par])
            cp.start()
            cp.wait()

    wgb_ref[...] = wg_ref[0].astype(jnp.bfloat16)
    wub_ref[...] = wu_ref[0].astype(jnp.bfloat16)
    wdb_ref[...] = wd_ref[0].astype(jnp.bfloat16)

    def block_step(g, par):
        prv = (par + NBUF - 1) % NBUF
        wait_gather(par)

        @pl.when(g >= NBUF - 1)
        def _():
            wait_scatter(par)

        start_gather(g + NBUF - 1, prv)
        start_scatter(g - 1, prv)
        words = [xg[par][pl.ds(s, TM, stride=PSLAB), :] for s in range(PSLAB)]
        x = jnp.concatenate([_unpack_lo(w).astype(jnp.bfloat16) for w in words]
                            + [_unpack_hi(w).astype(jnp.bfloat16) for w in words], axis=1)
        gate = jnp.dot(x, wgb_ref[...], preferred_element_type=jnp.float32) + bg_ref[0]
        up = jnp.dot(x, wub_ref[...], preferred_element_type=jnp.float32) + bu_ref[0]
        gate = jnp.minimum(gate, SWIGLU_LIMIT)
        up = jnp.clip(up, -SWIGLU_LIMIT, SWIGLU_LIMIT)
        glu = gate * _sigmoid(SWIGLU_ALPHA * gate)
        act = (glu * (up + 1.0)).astype(jnp.bfloat16)
        y = jnp.dot(act, wdb_ref[...], preferred_element_type=jnp.float32) + bd_ref[0]
        packed = _pack_bf16_pairs(y)
        for s in range(PSLAB):
            ys[par][pl.ds(s, TM, stride=PSLAB), :] = packed[:, s * LANES:(s + 1) * LANES]

    def body(g, carry):
        for par in range(NBUF):
            pl.when(g % NBUF == par)(functools.partial(block_step, g, par))
        return carry

    lax.fori_loop(bs_ref[e], bs_ref[e + 1], body, 0)

    @pl.when(e == N_EXPERTS - 1)
    def _():
        g = n_total
        for par in range(NBUF):
            @pl.when((g - 1) % NBUF == par)
            def _():
                start_scatter(g - 1, par)
        for j in range(NBUF - 1):
            wait_gather((g + j) % NBUF)
        wait_scatter((g - 1) % NBUF)
        for j in range(2, NBUF + 1):
            @pl.when(g >= j - 1)
            def _():
                wait_scatter((g + NBUF - j) % NBUF)


def _experts(block_start, slot_buf, h2_slab, w_gate, b_gate, w_up, b_up, w_down, b_down, n_tok):
    TM = TM_EXPERT
    n_assign = n_tok * TOP_K
    w_spec = pl.BlockSpec((1, D_MODEL, D_FF), lambda e, bs, sl: (e, 0, 0))
    bias_spec = pl.BlockSpec((1, 1, D_FF), lambda e, bs, sl: (e, 0, 0))
    buf = pltpu.VMEM((TM * PSLAB, LANES), jnp.uint32)
    grid_spec = pltpu.PrefetchScalarGridSpec(
        num_scalar_prefetch=2,
        grid=(N_EXPERTS,),
        in_specs=[
            pl.BlockSpec(memory_space=pl.ANY),
            w_spec, bias_spec, w_spec, bias_spec, w_spec, bias_spec,
        ],
        out_specs=pl.BlockSpec(memory_space=pl.ANY),
        scratch_shapes=[
            *([buf] * (2 * NBUF)),
            pltpu.VMEM((D_MODEL, D_FF), jnp.bfloat16),
            pltpu.VMEM((D_MODEL, D_FF), jnp.bfloat16),
            pltpu.VMEM((D_FF, D_MODEL), jnp.bfloat16),
            pltpu.SemaphoreType.DMA((NBUF,)),
            pltpu.SemaphoreType.DMA((NBUF,)),
        ],
    )
    return pl.pallas_call(
        functools.partial(_expert_kernel, n_tok),
        grid_spec=grid_spec,
        out_shape=jax.ShapeDtypeStruct(((n_assign + NBUF * TM) * PSLAB, LANES), jnp.uint32),
        compiler_params=pltpu.CompilerParams(
            dimension_semantics=("arbitrary",), vmem_limit_bytes=VMEM_LIMIT),
        name="experts",
    )(block_start, slot_buf, h2_slab, w_gate, b_gate, w_up, b_up, w_down, b_down)


def _combine_kernel(normalize, x1_ref, y0_ref, y1_ref, y2_ref, y3_ref, gate_ref, g_ref, o_ref):
    TM = TM_PROJ
    gates = jnp.concatenate([gate_ref[...], jnp.zeros((8 - TOP_K, TM), jnp.float32)], axis=0)
    g_cols = jnp.transpose(gates)
    g_bc = [jnp.broadcast_to(g_cols[:, k:k + 1], (TM, LANES)) for k in range(TOP_K)]
    ssq = jnp.zeros((TM, LANES), jnp.float32)
    parts = [x1_ref[:, s * LANES:(s + 1) * LANES] for s in range(SLAB)]
    for s in range(PSLAB):
        for k, y_ref in enumerate((y0_ref, y1_ref, y2_ref, y3_ref)):
            w = y_ref[pl.ds(s, TM, stride=PSLAB), :]
            parts[s] = parts[s] + g_bc[k] * _unpack_lo(w)
            parts[PSLAB + s] = parts[PSLAB + s] + g_bc[k] * _unpack_hi(w)
    for acc in parts:
        ssq = ssq + acc * acc
    if normalize:
        inv = lax.rsqrt(jnp.sum(ssq, axis=-1, keepdims=True) * (1.0 / D_MODEL) + EPS)
        for s in range(SLAB):
            o_ref[:, s * LANES:(s + 1) * LANES] = parts[s] * inv * g_ref[:, s * LANES:(s + 1) * LANES]
    else:
        for s in range(SLAB):
            o_ref[:, s * LANES:(s + 1) * LANES] = parts[s]


def _combine(x1, y_tok, gate_t, gf, normalize):
    T = x1.shape[0]
    TM = TM_PROJ
    nt = T // TM

    def y_spec(k):
        return pl.BlockSpec((TM * PSLAB, LANES), lambda i: (k * nt + i, 0))

    return pl.pallas_call(
        functools.partial(_combine_kernel, normalize),
        grid=(nt,),
        in_specs=[
            pl.BlockSpec((TM, D_MODEL), lambda i: (i, 0)),
            y_spec(0), y_spec(1), y_spec(2), y_spec(3),
            pl.BlockSpec((TOP_K, TM), lambda i: (0, i)),
            pl.BlockSpec((1, D_MODEL), lambda i: (0, 0)),
        ],
        out_specs=pl.BlockSpec((TM, D_MODEL), lambda i: (i, 0)),
        out_shape=jax.ShapeDtypeStruct((T, D_MODEL), jnp.float32),
        compiler_params=pltpu.CompilerParams(
            dimension_semantics=("parallel",), vmem_limit_bytes=VMEM_LIMIT),
        name="combine",
    )(x1, y_tok, y_tok, y_tok, y_tok, gate_t, gf)


def kernel(x, norm1_g, w_in, ig_b, fg_b, conv_w, head_norm_g, pool_w, pool_scale, w_out, norm2_g,
           w_router, b_router, w_gate, b_gate, w_up, b_up, w_down, b_down, normf_g):
    B, S, D = x.shape
    T = B * S
    depth = norm1_g.shape[0]
    W = MLSTM_WIDTH
    f32, bf16 = jnp.float32, jnp.bfloat16

    L = CHUNK
    t_l = lax.broadcasted_iota(jnp.int32, (L, L), 0)
    t_r = lax.broadcasted_iota(jnp.int32, (L, L), 1)
    tri = (t_r <= t_l).astype(f32)
    shifts = jnp.stack([(t_l - t_r == CONV_WIDTH - 1 - j).astype(bf16)
                        for j in range(CONV_WIDTH - 1)])
    h_t = lax.broadcasted_iota(jnp.int32, (8, HALO), 0)
    h_r = lax.broadcasted_iota(jnp.int32, (8, HALO), 1)
    halo_shifts = jnp.stack([(h_r - HALO - h_t == -(CONV_WIDTH - 1 - j)).astype(bf16)
                             for j in range(CONV_WIDTH - 1)])

    n_assign = T * TOP_K
    n_blocks = -(-n_assign // TM_EXPERT) + N_EXPERTS
    n_rows = n_blocks * TM_EXPERT
    n_table = n_rows + NBUF * TM_EXPERT
    fill = n_assign + ((jnp.arange(n_table, dtype=jnp.int32) + (NBUF - 1) * TM_EXPERT)
                       % (NBUF * TM_EXPERT))
    x2 = x.reshape(T, D)
    for l in range(depth):
        w = w_in[l]
        w_a = w[:, :4 * W].astype(bf16)
        w_u = w[:, 4 * W + N_GATES:].astype(bf16)
        wg_t = jnp.zeros((BF16_SUBLANES, D), bf16).at[:N_GATES].set(
            w[:, 4 * W:4 * W + N_GATES].T.astype(bf16))
        p, gates_t = _in_proj(x2, norm1_g[l][None, :], w_a, w_u, wg_t)

        gate_b = jnp.concatenate([ig_b[l], fg_b[l]])[:, None].astype(f32)
        gates_b = gates_t.reshape(N_GATES, B, S).transpose(1, 0, 2)
        ym = _mlstm(p.reshape(B, S, N_MAIN), gates_b, conv_w[l].astype(f32), gate_b,
                    head_norm_g[l][None, :], tri, shifts, halo_shifts).reshape(T, W)

        x1, h2, idx_t, gate_t, rank_t, cnt = _out_route(
            x2, ym, p, pool_w[l].astype(bf16), pool_scale[l][None, :], w_out[l].astype(bf16),
            norm2_g[l][None, :], w_router[l].T.astype(bf16), b_router[l][:, None], S)

        counts = cnt[:, 0]
        padded = ((counts + TM_EXPERT - 1) // TM_EXPERT) * TM_EXPERT
        padded_end = jnp.cumsum(padded)
        padded_start = padded_end - padded
        expert_ids = jnp.arange(N_EXPERTS, dtype=jnp.int32)[:, None, None]
        start_of = jnp.sum(jnp.where(idx_t[None] == expert_ids, padded_start[:, None, None], 0), axis=0)
        dest = start_of + rank_t
        block_start = jnp.concatenate(
            [jnp.zeros((1,), jnp.int32), (padded_end // TM_EXPERT).astype(jnp.int32)])

        slot_buf = _plan(dest.reshape(-1) + TM_EXPERT, fill)
        y_tok = _experts(block_start, slot_buf, h2, w_gate[l], b_gate[l][:, None, :],
                         w_up[l], b_up[l][:, None, :], w_down[l], b_down[l][:, None, :], T)
        last = l + 1 == depth
        x2 = _combine(x1, y_tok, gate_t, normf_g[None, :], last)
    return x2.reshape(B, S, D)
```

```python
import functools

import jax
import jax.numpy as jnp
from jax import lax
from jax.experimental import pallas as pl
from jax.experimental.pallas import tpu as pltpu
from jax.experimental.pallas import tpu_sc as plsc

D_MODEL = 1024
MLSTM_WIDTH = 512
MLSTM_HEADS = 4
HEAD_DIM = 128
CONV_WIDTH = 4
POOL_WIDTH = 512
POOL_WINDOWS = (2, 4, 8, 16)
POOL_GROUP_DIM = 128
N_EXPERTS = 32
TOP_K = 4
D_FF = 1024
SWIGLU_LIMIT = 7.0
SWIGLU_ALPHA = 1.702
EPS = 1e-5

N_MAIN = 4 * MLSTM_WIDTH + POOL_WIDTH
N_GATES = 2 * MLSTM_HEADS

LANES = 128
BF16_SUBLANES = 16
VMEM_LIMIT = 56 * 1024 * 1024

TM_PROJ = 512
ROUTE_SUB = 2
CHUNK = 256
MLSTM_BATCH = 2
HALO = 16
TM_EXPERT = 512
NBUF = 4
ROW_DMA_PRIORITY = 1
SLAB = D_MODEL // LANES
PSLAB = SLAB // 2
PLAN_CHUNK = 8192
SC_LANES = 16
PLAN_UNROLL = 8

NT_DIMS = (((1,), (1,)), ((), ()))


def _sigmoid(x):
    return 1.0 / (1.0 + jnp.exp(-x))


def _pack_bf16_pairs(v):
    half = v.shape[1] // 2
    lo = pltpu.bitcast(v[:, :half].astype(jnp.bfloat16).astype(jnp.float32), jnp.uint32)
    hi = pltpu.bitcast(v[:, half:].astype(jnp.bfloat16).astype(jnp.float32), jnp.uint32)
    return (lo >> 16) | (hi & jnp.uint32(0xFFFF0000))


def _unpack_lo(w):
    return pltpu.bitcast(w << 16, jnp.float32)


def _unpack_hi(w):
    return pltpu.bitcast(w & jnp.uint32(0xFFFF0000), jnp.float32)


def _in_proj_kernel(x_ref, g_ref, wa_ref, wu_ref, wgt_ref, p_ref, gt_ref):
    x = x_ref[...]
    h = x * lax.rsqrt(jnp.mean(x * x, axis=-1, keepdims=True) + EPS) * g_ref[...]
    hb = h.astype(jnp.bfloat16)
    n_a = wa_ref.shape[1]
    p_ref[:, :n_a] = jnp.dot(hb, wa_ref[...], preferred_element_type=jnp.float32).astype(p_ref.dtype)
    p_ref[:, n_a:] = jnp.dot(hb, wu_ref[...], preferred_element_type=jnp.float32).astype(p_ref.dtype)
    gt = lax.dot_general(wgt_ref[...], hb, NT_DIMS, preferred_element_type=jnp.float32)
    gt_ref[...] = gt[:N_GATES]


def _in_proj(x2, g1, w_a, w_u, wg_t):
    T = x2.shape[0]
    return pl.pallas_call(
        _in_proj_kernel,
        grid=(T // TM_PROJ,),
        in_specs=[
            pl.BlockSpec((TM_PROJ, D_MODEL), lambda i: (i, 0)),
            pl.BlockSpec((1, D_MODEL), lambda i: (0, 0)),
            pl.BlockSpec(w_a.shape, lambda i: (0, 0)),
            pl.BlockSpec(w_u.shape, lambda i: (0, 0)),
            pl.BlockSpec((BF16_SUBLANES, D_MODEL), lambda i: (0, 0)),
        ],
        out_specs=[
            pl.BlockSpec((TM_PROJ, N_MAIN), lambda i: (i, 0)),
            pl.BlockSpec((N_GATES, TM_PROJ), lambda i: (0, i)),
        ],
        out_shape=[
            jax.ShapeDtypeStruct((T, N_MAIN), jnp.bfloat16),
            jax.ShapeDtypeStruct((N_GATES, T), jnp.float32),
        ],
        compiler_params=pltpu.CompilerParams(
            dimension_semantics=("parallel",), vmem_limit_bytes=VMEM_LIMIT),
        name="in_proj",
    )(x2, g1, w_a, w_u, wg_t)


def _mlstm_kernel(qk_ref, qkp_ref, v_ref, o_ref, gt_ref, convw_ref, gb_ref, hng_ref,
                  tri_ref, shift_ref, hshift_ref, y_ref, cn_ref, m_ref):
    L = CHUNK
    c = pl.program_id(1)

    @pl.when(c == 0)
    def _():
        cn_ref[...] = jnp.zeros_like(cn_ref)
        m_ref[...] = jnp.zeros_like(m_ref)

    row_id = lax.broadcasted_iota(jnp.int32, (L, L), 0)
    col_id = lax.broadcasted_iota(jnp.int32, (L, L), 1)
    causal = col_id <= row_id
    ones_blk = jnp.ones((L, HEAD_DIM), jnp.bfloat16)
    lane = lax.broadcasted_iota(jnp.int32, (MLSTM_HEADS, L), 1)

    gate_terms = []
    for bb in range(MLSTM_BATCH):
        gt = gt_ref[bb] + gb_ref[...]
        f = gt[MLSTM_HEADS:]
        lf = jnp.minimum(f, 0.0) - jnp.log(1.0 + jnp.exp(-jnp.abs(f)))
        ig = gt[:MLSTM_HEADS]
        b_rows = lax.dot_general(lf, tri_ref[...], NT_DIMS, precision=lax.Precision.HIGHEST,
                                 preferred_element_type=jnp.float32)
        c_rows = ig - b_rows
        cm_rows = c_rows
        d = 1
        while d < L:
            cm_rows = jnp.maximum(
                cm_rows, jnp.where(lane >= d, pltpu.roll(cm_rows, d, axis=1), -jnp.inf))
            d *= 2
        gate_terms.append((b_rows, c_rows, cm_rows))

    conv_terms = []
    for bb in range(MLSTM_BATCH):
        x_cur = qk_ref[bb]
        x_prev = jnp.where(c > 0, qkp_ref[bb], jnp.zeros((HALO, 2 * MLSTM_WIDTH), jnp.bfloat16))
        acc = convw_ref[CONV_WIDTH - 1:CONV_WIDTH, :] * x_cur.astype(jnp.float32)
        for j in range(CONV_WIDTH - 1):
            sh = jnp.dot(shift_ref[j], x_cur, preferred_element_type=jnp.float32)
            top = sh[:8] + jnp.dot(hshift_ref[j], x_prev, preferred_element_type=jnp.float32)
            sh = jnp.concatenate([top, sh[8:]], axis=0)
            acc = acc + convw_ref[j:j + 1, :] * sh
        qk = acc * _sigmoid(acc)
        q_all = qk[:, :MLSTM_WIDTH].astype(jnp.bfloat16)
        k_t = jnp.transpose(qk[:, MLSTM_WIDTH:] * (HEAD_DIM ** -0.5))
        conv_terms.append((q_all, k_t))

    for bb in range(MLSTM_BATCH):
        b_rows, c_rows, cm_rows = gate_terms[bb]
        q_all, k_t = conv_terms[bb]
        m_in4 = jnp.concatenate(
            [m_ref[bb * MLSTM_HEADS + h][0:1, 0:1] for h in range(MLSTM_HEADS)], axis=0)
        mx_rows = jnp.maximum(cm_rows, m_in4)
        inter_rows = jnp.exp(m_in4 - mx_rows)
        einv_rows = jnp.exp(-(b_rows + mx_rows))
        fac_t = jnp.transpose(jnp.concatenate(
            [mx_rows, inter_rows, einv_rows, jnp.zeros_like(mx_rows)], axis=0))

        for h in range(MLSTM_HEADS):
            lo = h * HEAD_DIM
            st = bb * MLSTM_HEADS + h
            q = q_all[:, lo:lo + HEAD_DIM]
            kt = k_t[lo:lo + HEAD_DIM, :]
            v_ext = jnp.concatenate([v_ref[bb, :, lo:lo + HEAD_DIM], ones_blk], axis=1)
            mx_col = fac_t[:, h:h + 1]
            inter_col = fac_t[:, MLSTM_HEADS + h:MLSTM_HEADS + h + 1]
            einv_col = fac_t[:, 2 * MLSTM_HEADS + h:2 * MLSTM_HEADS + h + 1]
            c_row = c_rows[h:h + 1, :]
            b_tot = b_rows[h:h + 1, L - 1:L]
            cm_tot = cm_rows[h:h + 1, L - 1:L]
            m_in = m_ref[st][0:1, 0:1]
            cn = cn_ref[st]

            s_qk = jnp.dot(q, kt.astype(jnp.bfloat16), preferred_element_type=jnp.float32)
            s = (s_qk * jnp.exp(jnp.where(causal, c_row - mx_col, -jnp.inf))).astype(jnp.bfloat16)
            num = (jnp.dot(s, v_ext, preferred_element_type=jnp.float32)
                   + inter_col * jnp.dot(q, cn.astype(jnp.bfloat16),
                                         preferred_element_type=jnp.float32))
            den = num[:, HEAD_DIM:]
            hh = num[:, :HEAD_DIM] / jnp.maximum(jnp.abs(den), einv_col)

            mu = jnp.mean(hh, axis=-1, keepdims=True)
            dv = hh - mu
            var = jnp.mean(dv * dv, axis=-1, keepdims=True)
            hn = dv * lax.rsqrt(var + EPS) * hng_ref[:, lo:lo + HEAD_DIM]
            og = _sigmoid(o_ref[bb, :, lo:lo + HEAD_DIM].astype(jnp.float32))
            y_ref[bb, :, lo:lo + HEAD_DIM] = (og * hn).astype(y_ref.dtype)

            m_loc = b_tot + cm_tot
            kw_t = (kt * jnp.exp(c_row - cm_tot)).astype(jnp.bfloat16)
            c_loc = jnp.dot(kw_t, v_ext, preferred_element_type=jnp.float32)
            m_new = jnp.maximum(b_tot + m_in, m_loc)
            s_old = jnp.exp(b_tot + m_in - m_new)
            s_loc = jnp.exp(m_loc - m_new)
            cn_ref[st] = s_old * cn + s_loc * c_loc
            m_ref[st] = jnp.broadcast_to(m_new, m_ref.shape[1:])


def _mlstm(p3, gates_b, conv_w, gate_b, hn_g, tri, shifts, halo_shifts):
    batch, seq, _ = p3.shape
    L = CHUNK
    BB = MLSTM_BATCH
    halo_per_chunk = L // HALO
    return pl.pallas_call(
        _mlstm_kernel,
        grid=(batch // BB, seq // L),
        in_specs=[
            pl.BlockSpec((BB, L, 2 * MLSTM_WIDTH), lambda bi, ci: (bi, ci, 0)),
            pl.BlockSpec((BB, HALO, 2 * MLSTM_WIDTH),
                         lambda bi, ci: (bi, jnp.maximum(ci * halo_per_chunk - 1, 0), 0)),
            pl.BlockSpec((BB, L, MLSTM_WIDTH), lambda bi, ci: (bi, ci, 2)),
            pl.BlockSpec((BB, L, MLSTM_WIDTH), lambda bi, ci: (bi, ci, 3)),
            pl.BlockSpec((BB, N_GATES, L), lambda bi, ci: (bi, 0, ci)),
            pl.BlockSpec((CONV_WIDTH, 2 * MLSTM_WIDTH), lambda bi, ci: (0, 0)),
            pl.BlockSpec((N_GATES, 1), lambda bi, ci: (0, 0)),
            pl.BlockSpec((1, MLSTM_WIDTH), lambda bi, ci: (0, 0)),
            pl.BlockSpec((L, L), lambda bi, ci: (0, 0)),
            pl.BlockSpec((CONV_WIDTH - 1, L, L), lambda bi, ci: (0, 0, 0)),
            pl.BlockSpec((CONV_WIDTH - 1, 8, HALO), lambda bi, ci: (0, 0, 0)),
        ],
        out_specs=pl.BlockSpec((BB, L, MLSTM_WIDTH), lambda bi, ci: (bi, ci, 0)),
        out_shape=jax.ShapeDtypeStruct((batch, seq, MLSTM_WIDTH), jnp.bfloat16),
        scratch_shapes=[
            pltpu.VMEM((BB * MLSTM_HEADS, HEAD_DIM, 2 * HEAD_DIM), jnp.float32),
            pltpu.VMEM((BB * MLSTM_HEADS, 8, LANES), jnp.float32),
        ],
        compiler_params=pltpu.CompilerParams(
            dimension_semantics=("parallel", "arbitrary"), vmem_limit_bytes=VMEM_LIMIT),
        name="mlstm",
    )(p3, p3, p3, p3, gates_b, conv_w, gate_b, hn_g, tri, shifts, halo_shifts)


def _out_route_kernel(seq, x_ref, ym_ref, u_ref, up_ref, pw_ref, ps_ref, wo_ref, g2_ref,
                      wrt_ref, br_ref, x1_ref, h2_ref, idx_ref, gate_ref, rank_ref, cnt_ref,
                      carry_ref):
    TM = TM_PROJ
    R = ROUTE_SUB * TM
    i = pl.program_id(0)

    @pl.when(i == 0)
    def _():
        carry_ref[...] = jnp.zeros_like(carry_ref)

    pos0 = (i * R) % seq
    e_id = lax.broadcasted_iota(jnp.int32, (N_EXPERTS, TM), 0).astype(jnp.float32)
    t_row = lax.broadcasted_iota(jnp.int32, (TM, TM), 0)
    t_col = lax.broadcasted_iota(jnp.int32, (TM, TM), 1)
    before = jnp.where(t_row < t_col, 1.0, 0.0).astype(jnp.bfloat16)
    carry = carry_ref[...]
    subs = [slice(sub * TM, (sub + 1) * TM) for sub in range(ROUTE_SUB)]

    halo = jnp.where(pos0 > 0, up_ref[...].astype(jnp.float32), 0.0)
    u_ext = jnp.concatenate([halo, u_ref[...].astype(jnp.float32)], axis=0)
    win_sums = []
    for gi, w in enumerate(POOL_WINDOWS):
        sw = u_ext[:, gi * POOL_GROUP_DIM:(gi + 1) * POOL_GROUP_DIM]
        span = 1
        while span < w:
            sw = sw + pltpu.roll(sw, span, axis=0)
            span *= 2
        win_sums.append(sw)
    y_cats = []
    for sub, rows in enumerate(subs):
        r0 = sub * TM
        pos = (pos0 + r0 + lax.broadcasted_iota(jnp.int32, (TM, 1), 0) + 1).astype(jnp.float32)
        mixed = []
        for gi, w in enumerate(POOL_WINDOWS):
            lo = gi * POOL_GROUP_DIM
            tok = u_ext[HALO + r0:HALO + r0 + TM, lo:lo + POOL_GROUP_DIM]
            pooled = win_sums[gi][HALO + r0:HALO + r0 + TM] / jnp.minimum(pos, float(w)) - tok
            mg = jnp.dot(pooled.astype(jnp.bfloat16), pw_ref[gi],
                         preferred_element_type=jnp.float32)
            mixed.append((mg * ps_ref[:, lo:lo + POOL_GROUP_DIM]).astype(jnp.bfloat16))
        y_cats.append(jnp.concatenate([ym_ref[rows, :]] + mixed, axis=1))

    all_logits = []
    for sub, rows in enumerate(subs):
        r0 = sub * TM
        x1 = x_ref[rows, :] + jnp.dot(y_cats[sub], wo_ref[...], preferred_element_type=jnp.float32)
        x1_ref[rows, :] = x1
        h2 = x1 * lax.rsqrt(jnp.mean(x1 * x1, axis=-1, keepdims=True) + EPS) * g2_ref[...]
        h2b = h2.astype(jnp.bfloat16)
        h2w = _pack_bf16_pairs(h2)
        for s in range(PSLAB):
            h2_ref[pl.ds(r0 * PSLAB + s, TM, stride=PSLAB), :] = h2w[:, s * LANES:(s + 1) * LANES]
        all_logits.append(lax.dot_general(wrt_ref[...], h2b, NT_DIMS,
                                          preferred_element_type=jnp.float32) + br_ref[...])

    for sub, rows in enumerate(subs):
        work = all_logits[sub]
        vals, ids, hots = [], [], []
        for _ in range(TOP_K):
            mk = jnp.max(work, axis=0, keepdims=True)
            ik = jnp.min(jnp.where(work == mk, e_id, float(N_EXPERTS)), axis=0, keepdims=True)
            hot = e_id == ik
            work = jnp.where(hot, -jnp.inf, work)
            vals.append(mk)
            ids.append(ik)
            hots.append(hot)
        ex = [jnp.exp(vk - vals[0]) for vk in vals]
        denom = ex[0] + ex[1] + ex[2] + ex[3]
        gate_ref[:, rows] = jnp.concatenate([e / denom for e in ex], axis=0)
        idx_ref[:, rows] = jnp.concatenate(ids, axis=0).astype(jnp.int32)

        sel_f = sum(jnp.where(hot, 1.0, 0.0) for hot in hots)
        prefix = jnp.dot(sel_f.astype(jnp.bfloat16), before, preferred_element_type=jnp.float32)
        rank_e = carry[:, 0:1] + prefix
        ranks = [jnp.sum(jnp.where(hot, rank_e, 0.0), axis=0, keepdims=True) for hot in hots]
        rank_ref[:, rows] = jnp.concatenate(ranks, axis=0).astype(jnp.int32)
        carry = carry + jnp.sum(sel_f, axis=1, keepdims=True)
    carry_ref[...] = carry
    cnt_ref[...] = carry.astype(jnp.int32)


def _out_route(x2, ym, p, pool_w, pool_s, w_out, g2, wr_t, br, seq):
    T = x2.shape[0]
    TM = ROUTE_SUB * TM_PROJ
    nt = T // TM
    u_blk = N_MAIN // POOL_WIDTH - 1
    halo_per_tile = TM // HALO
    tok_spec = pl.BlockSpec((TOP_K, TM), lambda i: (0, i))
    return pl.pallas_call(
        functools.partial(_out_route_kernel, seq),
        grid=(nt,),
        in_specs=[
            pl.BlockSpec((TM, D_MODEL), lambda i: (i, 0)),
            pl.BlockSpec((TM, MLSTM_WIDTH), lambda i: (i, 0)),
            pl.BlockSpec((TM, POOL_WIDTH), lambda i: (i, u_blk)),
            pl.BlockSpec((HALO, POOL_WIDTH),
                         lambda i: (jnp.maximum(i * halo_per_tile - 1, 0), u_blk)),
            pl.BlockSpec((len(POOL_WINDOWS), POOL_GROUP_DIM, POOL_GROUP_DIM), lambda i: (0, 0, 0)),
            pl.BlockSpec((1, POOL_WIDTH), lambda i: (0, 0)),
            pl.BlockSpec((D_MODEL, D_MODEL), lambda i: (0, 0)),
            pl.BlockSpec((1, D_MODEL), lambda i: (0, 0)),
            pl.BlockSpec((N_EXPERTS, D_MODEL), lambda i: (0, 0)),
            pl.BlockSpec((N_EXPERTS, 1), lambda i: (0, 0)),
        ],
        out_specs=[
            pl.BlockSpec((TM, D_MODEL), lambda i: (i, 0)),
            pl.BlockSpec((TM * PSLAB, LANES), lambda i: (i, 0)),
            tok_spec, tok_spec, tok_spec,
            pl.BlockSpec((N_EXPERTS, LANES), lambda i: (0, 0)),
        ],
        out_shape=[
            jax.ShapeDtypeStruct((T, D_MODEL), jnp.float32),
            jax.ShapeDtypeStruct((T * PSLAB, LANES), jnp.uint32),
            jax.ShapeDtypeStruct((TOP_K, T), jnp.int32),
            jax.ShapeDtypeStruct((TOP_K, T), jnp.float32),
            jax.ShapeDtypeStruct((TOP_K, T), jnp.int32),
            jax.ShapeDtypeStruct((N_EXPERTS, LANES), jnp.int32),
        ],
        scratch_shapes=[
            pltpu.VMEM((N_EXPERTS, LANES), jnp.float32),
        ],
        compiler_params=pltpu.CompilerParams(
            dimension_semantics=("arbitrary",), vmem_limit_bytes=VMEM_LIMIT),
        name="out_route",
    )(x2, ym, p, p, pool_w, pool_s, w_out, g2, wr_t, br)


def _plan(dest_flat, fill):
    n_assign = dest_flat.shape[0]
    n_table = fill.shape[0]
    mesh = plsc.VectorSubcoreMesh(core_axis_name="c", subcore_axis_name="s")

    @pl.kernel(out_type=jax.ShapeDtypeStruct((n_table,), jnp.int32), mesh=mesh,
               scratch_types=[pltpu.VMEM((n_table,), jnp.int32),
                              pltpu.VMEM((PLAN_CHUNK,), jnp.int32)],
               compiler_params=pltpu.CompilerParams(needs_layout_passes=False))
    def plan_kernel(dest_hbm, fill_hbm, out_hbm, table, chunk):
        first = jnp.logical_and(lax.axis_index("c") == 0, lax.axis_index("s") == 0)

        @pl.when(first)
        def _():
            pltpu.sync_copy(fill_hbm, table)

            @pl.loop(0, n_assign // PLAN_CHUNK)
            def _(ci):
                pltpu.sync_copy(dest_hbm.at[pl.ds(ci * PLAN_CHUNK, PLAN_CHUNK)], chunk)

                @pl.loop(0, PLAN_CHUNK // (SC_LANES * PLAN_UNROLL))
                def _(i):
                    for j in range(PLAN_UNROLL):
                        off = (i * PLAN_UNROLL + j) * SC_LANES
                        idx = chunk[pl.ds(off, SC_LANES)]
                        vals = (ci * PLAN_CHUNK + off
                                + lax.broadcasted_iota(jnp.int32, (SC_LANES,), 0))
                        plsc.store_scatter(table, [idx], vals)

            pltpu.sync_copy(table, out_hbm)

    return plan_kernel(dest_flat, fill)


def _expert_kernel(n_tok, bs_ref, slot_ref, h2_ref, wg_ref, bg_ref, wu_ref, bu_ref, wd_ref, bd_ref,
                   yt_ref, *scratch):
    TM = TM_EXPERT
    ROWS = TM * PSLAB
    e = pl.program_id(0)
    n_total = bs_ref[N_EXPERTS]
    xg = scratch[:NBUF]
    ys = scratch[NBUF:2 * NBUF]
    wgb_ref, wub_ref, wdb_ref, gsem, ssem = scratch[2 * NBUF:]

    def token_of(a):
        return a & (n_tok - 1) if n_tok & (n_tok - 1) == 0 else lax.rem(a, n_tok)

    def start_gather(blk, par):
        base = (blk + 1) * TM
        for r in range(TM):
            t = token_of(slot_ref[base + r])
            pltpu.make_async_copy(h2_ref.at[pl.ds(pl.multiple_of(t * PSLAB, PSLAB), PSLAB), :],
                                  xg[par].at[pl.ds(r * PSLAB, PSLAB), :], gsem.at[par]
                                  ).start(priority=ROW_DMA_PRIORITY)

    def wait_gather(par):
        pltpu.make_async_copy(h2_ref.at[pl.ds(0, ROWS), :], xg[0], gsem.at[par]).wait()

    def start_scatter(blk, par):
        base = (blk + 1) * TM
        for r in range(TM):
            a = slot_ref[base + r]
            pltpu.make_async_copy(ys[par].at[pl.ds(r * PSLAB, PSLAB), :],
                                  yt_ref.at[pl.ds(pl.multiple_of(a * PSLAB, PSLAB), PSLAB), :],
                                  ssem.at[par]).start(priority=ROW_DMA_PRIORITY)

    def wait_scatter(par):
        pltpu.make_async_copy(ys[0], yt_ref.at[pl.ds(0, ROWS), :], ssem.at[par]).wait()

    @pl.when(e == 0)
    def _():
        for blk in range(NBUF - 1):
            start_gather(blk, blk)
        for par in range(NBUF):
            ys[par][...] = jnp.zeros_like(ys[par])
            dump = yt_ref.at[pl.ds((n_tok * TOP_K + par * TM) * PSLAB, ROWS), :]
            cp = pltpu.make_async_copy(ys[par], dump, ssem.at[par])
            cp.start()
            cp.wait()

    wgb_ref[...] = wg_ref[0].astype(jnp.bfloat16)
    wub_ref[...] = wu_ref[0].astype(jnp.bfloat16)
    wdb_ref[...] = wd_ref[0].astype(jnp.bfloat16)

    def block_step(g, par):
        prv = (par + NBUF - 1) % NBUF
        wait_gather(par)

        @pl.when(g >= NBUF - 1)
        def _():
            wait_scatter(par)

        start_gather(g + NBUF - 1, prv)
        start_scatter(g - 1, prv)
        words = [xg[par][pl.ds(s, TM, stride=PSLAB), :] for s in range(PSLAB)]
        x = jnp.concatenate([_unpack_lo(w).astype(jnp.bfloat16) for w in words]
                            + [_unpack_hi(w).astype(jnp.bfloat16) for w in words], axis=1)
        gate = jnp.dot(x, wgb_ref[...], preferred_element_type=jnp.float32) + bg_ref[0]
        up = jnp.dot(x, wub_ref[...], preferred_element_type=jnp.float32) + bu_ref[0]
        gate = jnp.minimum(gate, SWIGLU_LIMIT)
        up = jnp.clip(up, -SWIGLU_LIMIT, SWIGLU_LIMIT)
        glu = gate * _sigmoid(SWIGLU_ALPHA * gate)
        act = (glu * (up + 1.0)).astype(jnp.bfloat16)
        y = jnp.dot(act, wdb_ref[...], preferred_element_type=jnp.float32) + bd_ref[0]
        packed = _pack_bf16_pairs(y)
        for s in range(PSLAB):
            ys[par][pl.ds(s, TM, stride=PSLAB), :] = packed[:, s * LANES:(s + 1) * LANES]

    def body(g, carry):
        for par in range(NBUF):
            pl.when(g % NBUF == par)(functools.partial(block_step, g, par))
        return carry

    lax.fori_loop(bs_ref[e], bs_ref[e + 1], body, 0)

    @pl.when(e == N_EXPERTS - 1)
    def _():
        g = n_total
        for par in range(NBUF):
            @pl.when((g - 1) % NBUF == par)
            def _():
                start_scatter(g - 1, par)
        for j in range(NBUF - 1):
            wait_gather((g + j) % NBUF)
        wait_scatter((g - 1) % NBUF)
        for j in range(2, NBUF + 1):
            @pl.when(g >= j - 1)
            def _():
                wait_scatter((g + NBUF - j) % NBUF)


def _experts(block_start, slot_buf, h2_slab, w_gate, b_gate, w_up, b_up, w_down, b_down, n_tok):
    TM = TM_EXPERT
    n_assign = n_tok * TOP_K
    w_spec = pl.BlockSpec((1, D_MODEL, D_FF), lambda e, bs, sl: (e, 0, 0))
    bias_spec = pl.BlockSpec((1, 1, D_FF), lambda e, bs, sl: (e, 0, 0))
    buf = pltpu.VMEM((TM * PSLAB, LANES), jnp.uint32)
    grid_spec = pltpu.PrefetchScalarGridSpec(
        num_scalar_prefetch=2,
        grid=(N_EXPERTS,),
        in_specs=[
            pl.BlockSpec(memory_space=pl.ANY),
            w_spec, bias_spec, w_spec, bias_spec, w_spec, bias_spec,
        ],
        out_specs=pl.BlockSpec(memory_space=pl.ANY),
        scratch_shapes=[
            *([buf] * (2 * NBUF)),
            pltpu.VMEM((D_MODEL, D_FF), jnp.bfloat16),
            pltpu.VMEM((D_MODEL, D_FF), jnp.bfloat16),
            pltpu.VMEM((D_FF, D_MODEL), jnp.bfloat16),
            pltpu.SemaphoreType.DMA((NBUF,)),
            pltpu.SemaphoreType.DMA((NBUF,)),
        ],
    )
    return pl.pallas_call(
        functools.partial(_expert_kernel, n_tok),
        grid_spec=grid_spec,
        out_shape=jax.ShapeDtypeStruct(((n_assign + NBUF * TM) * PSLAB, LANES), jnp.uint32),
        compiler_params=pltpu.CompilerParams(
            dimension_semantics=("arbitrary",), vmem_limit_bytes=VMEM_LIMIT),
        name="experts",
    )(block_start, slot_buf, h2_slab, w_gate, b_gate, w_up, b_up, w_down, b_down)


def _combine_kernel(normalize, x1_ref, y0_ref, y1_ref, y2_ref, y3_ref, gate_ref, g_ref, o_ref):
    TM = TM_PROJ
    gates = jnp.concatenate([gate_ref[...], jnp.zeros((8 - TOP_K, TM), jnp.float32)], axis=0)
    g_cols = jnp.transpose(gates)
    g_bc = [jnp.broadcast_to(g_cols[:, k:k + 1], (TM, LANES)) for k in range(TOP_K)]
    ssq = jnp.zeros((TM, LANES), jnp.float32)
    parts = [x1_ref[:, s * LANES:(s + 1) * LANES] for s in range(SLAB)]
    for s in range(PSLAB):
        for k, y_ref in enumerate((y0_ref, y1_ref, y2_ref, y3_ref)):
            w = y_ref[pl.ds(s, TM, stride=PSLAB), :]
            parts[s] = parts[s] + g_bc[k] * _unpack_lo(w)
            parts[PSLAB + s] = parts[PSLAB + s] + g_bc[k] * _unpack_hi(w)
    for acc in parts:
        ssq = ssq + acc * acc
    if normalize:
        inv = lax.rsqrt(jnp.sum(ssq, axis=-1, keepdims=True) * (1.0 / D_MODEL) + EPS)
        for s in range(SLAB):
            o_ref[:, s * LANES:(s + 1) * LANES] = parts[s] * inv * g_ref[:, s * LANES:(s + 1) * LANES]
    else:
        for s in range(SLAB):
            o_ref[:, s * LANES:(s + 1) * LANES] = parts[s]


def _combine(x1, y_tok, gate_t, gf, normalize):
    T = x1.shape[0]
    TM = TM_PROJ
    nt = T // TM

    def y_spec(k):
        return pl.BlockSpec((TM * PSLAB, LANES), lambda i: (k * nt + i, 0))

    return pl.pallas_call(
        functools.partial(_combine_kernel, normalize),
        grid=(nt,),
        in_specs=[
            pl.BlockSpec((TM, D_MODEL), lambda i: (i, 0)),
            y_spec(0), y_spec(1), y_spec(2), y_spec(3),
            pl.BlockSpec((TOP_K, TM), lambda i: (0, i)),
            pl.BlockSpec((1, D_MODEL), lambda i: (0, 0)),
        ],
        out_specs=pl.BlockSpec((TM, D_MODEL), lambda i: (i, 0)),
        out_shape=jax.ShapeDtypeStruct((T, D_MODEL), jnp.float32),
        compiler_params=pltpu.CompilerParams(
            dimension_semantics=("parallel",), vmem_limit_bytes=VMEM_LIMIT),
        name="combine",
    )(x1, y_tok, y_tok, y_tok, y_tok, gate_t, gf)


def kernel(x, norm1_g, w_in, ig_b, fg_b, conv_w, head_norm_g, pool_w, pool_scale, w_out, norm2_g,
           w_router, b_router, w_gate, b_gate, w_up, b_up, w_down, b_down, normf_g):
    B, S, D = x.shape
    T = B * S
    depth = norm1_g.shape[0]
    W = MLSTM_WIDTH
    f32, bf16 = jnp.float32, jnp.bfloat16

    L = CHUNK
    t_l = lax.broadcasted_iota(jnp.int32, (L, L), 0)
    t_r = lax.broadcasted_iota(jnp.int32, (L, L), 1)
    tri = (t_r <= t_l).astype(f32)
    shifts = jnp.stack([(t_l - t_r == CONV_WIDTH - 1 - j).astype(bf16)
                        for j in range(CONV_WIDTH - 1)])
    h_t = lax.broadcasted_iota(jnp.int32, (8, HALO), 0)
    h_r = lax.broadcasted_iota(jnp.int32, (8, HALO), 1)
    halo_shifts = jnp.stack([(h_r - HALO - h_t == -(CONV_WIDTH - 1 - j)).astype(bf16)
                             for j in range(CONV_WIDTH - 1)])

    n_assign = T * TOP_K
    n_blocks = -(-n_assign // TM_EXPERT) + N_EXPERTS
    n_rows = n_blocks * TM_EXPERT
    n_table = n_rows + NBUF * TM_EXPERT
    fill = n_assign + ((jnp.arange(n_table, dtype=jnp.int32) + (NBUF - 1) * TM_EXPERT)
                       % (NBUF * TM_EXPERT))
    x2 = x.reshape(T, D)
    for l in range(depth):
        w = w_in[l]
        w_a = w[:, :4 * W].astype(bf16)
        w_u = w[:, 4 * W + N_GATES:].astype(bf16)
        wg_t = jnp.zeros((BF16_SUBLANES, D), bf16).at[:N_GATES].set(
            w[:, 4 * W:4 * W + N_GATES].T.astype(bf16))
        p, gates_t = _in_proj(x2, norm1_g[l][None, :], w_a, w_u, wg_t)

        gate_b = jnp.concatenate([ig_b[l], fg_b[l]])[:, None].astype(f32)
        gates_b = gates_t.reshape(N_GATES, B, S).transpose(1, 0, 2)
        ym = _mlstm(p.reshape(B, S, N_MAIN), gates_b, conv_w[l].astype(f32), gate_b,
                    head_norm_g[l][None, :], tri, shifts, halo_shifts).reshape(T, W)

        x1, h2, idx_t, gate_t, rank_t, cnt = _out_route(
            x2, ym, p, pool_w[l].astype(bf16), pool_scale[l][None, :], w_out[l].astype(bf16),
            norm2_g[l][None, :], w_router[l].T.astype(bf16), b_router[l][:, None], S)

        counts = cnt[:, 0]
        padded = ((counts + TM_EXPERT - 1) // TM_EXPERT) * TM_EXPERT
        padded_end = jnp.cumsum(padded)
        padded_start = padded_end - padded
        expert_ids = jnp.arange(N_EXPERTS, dtype=jnp.int32)[:, None, None]
        start_of = jnp.sum(jnp.where(idx_t[None] == expert_ids, padded_start[:, None, None], 0), axis=0)
        dest = start_of + rank_t
        block_start = jnp.concatenate(
            [jnp.zeros((1,), jnp.int32), (padded_end // TM_EXPERT).astype(jnp.int32)])

        slot_buf = _plan(dest.reshape(-1) + TM_EXPERT, fill)
        y_tok = _experts(block_start, slot_buf, h2, w_gate[l], b_gate[l][:, None, :],
                         w_up[l], b_up[l][:, None, :], w_down[l], b_down[l][:, None, :], T)
        last = l + 1 == depth
        x2 = _combine(x1, y_tok, gate_t, normf_g[None, :], last)
    return x2.reshape(B, S, D)
```

```python
import functools

import jax
import jax.numpy as jnp
from jax import lax
from jax.experimental import pallas as pl
from jax.experimental.pallas import tpu as pltpu
from jax.experimental.pallas import tpu_sc as plsc

D_MODEL = 1024
MLSTM_WIDTH = 512
MLSTM_HEADS = 4
HEAD_DIM = 128
CONV_WIDTH = 4
POOL_WIDTH = 512
POOL_WINDOWS = (2, 4, 8, 16)
POOL_GROUP_DIM = 128
N_EXPERTS = 32
TOP_K = 4
D_FF = 1024
SWIGLU_LIMIT = 7.0
SWIGLU_ALPHA = 1.702
EPS = 1e-5

N_MAIN = 4 * MLSTM_WIDTH + POOL_WIDTH
N_GATES = 2 * MLSTM_HEADS

LANES = 128
BF16_SUBLANES = 16
VMEM_LIMIT = 56 * 1024 * 1024

TM_PROJ = 512
TM_COMBINE = 1024
ROUTE_SUB = 2
CHUNK = 256
MLSTM_BATCH = 4
HALO = 16
TM_EXPERT = 512
NBUF = 3
ROW_DMA_PRIORITY = 1
SLAB = D_MODEL // LANES
PSLAB = SLAB // 2
PLAN_CHUNK = 8192
SC_LANES = 16
PLAN_UNROLL = 8

NT_DIMS = (((1,), (1,)), ((), ()))


def _sigmoid(x):
    return 1.0 / (1.0 + jnp.exp(-x))


def _pack_bf16_pairs(v):
    half = v.shape[1] // 2
    lo = pltpu.bitcast(v[:, :half].astype(jnp.bfloat16).astype(jnp.float32), jnp.uint32)
    hi = pltpu.bitcast(v[:, half:].astype(jnp.bfloat16).astype(jnp.float32), jnp.uint32)
    return (lo >> 16) | (hi & jnp.uint32(0xFFFF0000))


def _unpack_lo(w):
    return pltpu.bitcast(w << 16, jnp.float32)


def _unpack_hi(w):
    return pltpu.bitcast(w & jnp.uint32(0xFFFF0000), jnp.float32)


def _in_proj_kernel(x_ref, g_ref, wa_ref, wu_ref, wgt_ref, p_ref, gt_ref):
    x = x_ref[...]
    h = x * lax.rsqrt(jnp.mean(x * x, axis=-1, keepdims=True) + EPS) * g_ref[...]
    hb = h.astype(jnp.bfloat16)
    n_a = wa_ref.shape[1]
    p_ref[:, :n_a] = jnp.dot(hb, wa_ref[...], preferred_element_type=jnp.float32).astype(p_ref.dtype)
    p_ref[:, n_a:] = jnp.dot(hb, wu_ref[...], preferred_element_type=jnp.float32).astype(p_ref.dtype)
    gt = lax.dot_general(wgt_ref[...], hb, NT_DIMS, preferred_element_type=jnp.float32)
    gt_ref[...] = gt[:N_GATES]


def _in_proj(x2, g1, w_a, w_u, wg_t):
    T = x2.shape[0]
    return pl.pallas_call(
        _in_proj_kernel,
        grid=(T // TM_PROJ,),
        in_specs=[
            pl.BlockSpec((TM_PROJ, D_MODEL), lambda i: (i, 0)),
            pl.BlockSpec((1, D_MODEL), lambda i: (0, 0)),
            pl.BlockSpec(w_a.shape, lambda i: (0, 0)),
            pl.BlockSpec(w_u.shape, lambda i: (0, 0)),
            pl.BlockSpec((BF16_SUBLANES, D_MODEL), lambda i: (0, 0)),
        ],
        out_specs=[
            pl.BlockSpec((TM_PROJ, N_MAIN), lambda i: (i, 0)),
            pl.BlockSpec((N_GATES, TM_PROJ), lambda i: (0, i)),
        ],
        out_shape=[
            jax.ShapeDtypeStruct((T, N_MAIN), jnp.bfloat16),
            jax.ShapeDtypeStruct((N_GATES, T), jnp.float32),
        ],
        compiler_params=pltpu.CompilerParams(
            dimension_semantics=("parallel",), vmem_limit_bytes=VMEM_LIMIT),
        name="in_proj",
    )(x2, g1, w_a, w_u, wg_t)


def _mlstm_kernel(qk_ref, qkp_ref, v_ref, o_ref, gt_ref, convw_ref, gb_ref, hng_ref,
                  tri_ref, shift_ref, hshift_ref, y_ref, cn_ref, m_ref):
    L = CHUNK
    c = pl.program_id(1)

    @pl.when(c == 0)
    def _():
        cn_ref[...] = jnp.zeros_like(cn_ref)
        m_ref[...] = jnp.zeros_like(m_ref)

    row_id = lax.broadcasted_iota(jnp.int32, (L, L), 0)
    col_id = lax.broadcasted_iota(jnp.int32, (L, L), 1)
    causal = col_id <= row_id
    ones_blk = jnp.ones((L, HEAD_DIM), jnp.bfloat16)
    lane = lax.broadcasted_iota(jnp.int32, (MLSTM_HEADS, L), 1)

    gate_terms = []
    for bb in range(MLSTM_BATCH):
        gt = gt_ref[bb] + gb_ref[...]
        f = gt[MLSTM_HEADS:]
        lf = jnp.minimum(f, 0.0) - jnp.log(1.0 + jnp.exp(-jnp.abs(f)))
        ig = gt[:MLSTM_HEADS]
        b_rows = lax.dot_general(lf, tri_ref[...], NT_DIMS, precision=lax.Precision.HIGHEST,
                                 preferred_element_type=jnp.float32)
        c_rows = ig - b_rows
        cm_rows = c_rows
        d = 1
        while d < L:
            cm_rows = jnp.maximum(
                cm_rows, jnp.where(lane >= d, pltpu.roll(cm_rows, d, axis=1), -jnp.inf))
            d *= 2
        gate_terms.append((b_rows, c_rows, cm_rows))

    conv_terms = []
    for bb in range(MLSTM_BATCH):
        x_cur = qk_ref[bb]
        x_prev = jnp.where(c > 0, qkp_ref[bb], jnp.zeros((HALO, 2 * MLSTM_WIDTH), jnp.bfloat16))
        acc = convw_ref[CONV_WIDTH - 1:CONV_WIDTH, :] * x_cur.astype(jnp.float32)
        for j in range(CONV_WIDTH - 1):
            sh = jnp.dot(shift_ref[j], x_cur, preferred_element_type=jnp.float32)
            top = sh[:8] + jnp.dot(hshift_ref[j], x_prev, preferred_element_type=jnp.float32)
            sh = jnp.concatenate([top, sh[8:]], axis=0)
            acc = acc + convw_ref[j:j + 1, :] * sh
        qk = acc * _sigmoid(acc)
        q_all = qk[:, :MLSTM_WIDTH].astype(jnp.bfloat16)
        k_t = jnp.transpose(qk[:, MLSTM_WIDTH:] * (HEAD_DIM ** -0.5))
        conv_terms.append((q_all, k_t))

    for bb in range(MLSTM_BATCH):
        b_rows, c_rows, cm_rows = gate_terms[bb]
        q_all, k_t = conv_terms[bb]
        m_in4 = jnp.concatenate(
            [m_ref[bb * MLSTM_HEADS + h][0:1, 0:1] for h in range(MLSTM_HEADS)], axis=0)
        mx_rows = jnp.maximum(cm_rows, m_in4)
        inter_rows = jnp.exp(m_in4 - mx_rows)
        einv_rows = jnp.exp(-(b_rows + mx_rows))
        fac_t = jnp.transpose(jnp.concatenate(
            [mx_rows, inter_rows, einv_rows, jnp.zeros_like(mx_rows)], axis=0))

        for h in range(MLSTM_HEADS):
            lo = h * HEAD_DIM
            st = bb * MLSTM_HEADS + h
            q = q_all[:, lo:lo + HEAD_DIM]
            kt = k_t[lo:lo + HEAD_DIM, :]
            v_ext = jnp.concatenate([v_ref[bb, :, lo:lo + HEAD_DIM], ones_blk], axis=1)
            mx_col = fac_t[:, h:h + 1]
            inter_col = fac_t[:, MLSTM_HEADS + h:MLSTM_HEADS + h + 1]
            einv_col = fac_t[:, 2 * MLSTM_HEADS + h:2 * MLSTM_HEADS + h + 1]
            c_row = c_rows[h:h + 1, :]
            b_tot = b_rows[h:h + 1, L - 1:L]
            cm_tot = cm_rows[h:h + 1, L - 1:L]
            m_in = m_ref[st][0:1, 0:1]
            cn = cn_ref[st]

            s_qk = jnp.dot(q, kt.astype(jnp.bfloat16), preferred_element_type=jnp.float32)
            s = (s_qk * jnp.exp(jnp.where(causal, c_row - mx_col, -jnp.inf))).astype(jnp.bfloat16)
            num = (jnp.dot(s, v_ext, preferred_element_type=jnp.float32)
                   + inter_col * jnp.dot(q, cn.astype(jnp.bfloat16),
                                         preferred_element_type=jnp.float32))
            den = num[:, HEAD_DIM:]
            hh = num[:, :HEAD_DIM] / jnp.maximum(jnp.abs(den), einv_col)

            mu = jnp.mean(hh, axis=-1, keepdims=True)
            dv = hh - mu
            var = jnp.mean(dv * dv, axis=-1, keepdims=True)
            hn = dv * lax.rsqrt(var + EPS) * hng_ref[:, lo:lo + HEAD_DIM]
            og = _sigmoid(o_ref[bb, :, lo:lo + HEAD_DIM].astype(jnp.float32))
            y_ref[bb, :, lo:lo + HEAD_DIM] = (og * hn).astype(y_ref.dtype)

            m_loc = b_tot + cm_tot
            kw_t = (kt * jnp.exp(c_row - cm_tot)).astype(jnp.bfloat16)
            c_loc = jnp.dot(kw_t, v_ext, preferred_element_type=jnp.float32)
            m_new = jnp.maximum(b_tot + m_in, m_loc)
            s_old = jnp.exp(b_tot + m_in - m_new)
            s_loc = jnp.exp(m_loc - m_new)
            cn_ref[st] = s_old * cn + s_loc * c_loc
            m_ref[st] = jnp.broadcast_to(m_new, m_ref.shape[1:])


def _mlstm(p3, gates_b, conv_w, gate_b, hn_g, tri, shifts, halo_shifts):
    batch, seq, _ = p3.shape
    L = CHUNK
    BB = MLSTM_BATCH
    halo_per_chunk = L // HALO
    return pl.pallas_call(
        _mlstm_kernel,
        grid=(batch // BB, seq // L),
        in_specs=[
            pl.BlockSpec((BB, L, 2 * MLSTM_WIDTH), lambda bi, ci: (bi, ci, 0)),
            pl.BlockSpec((BB, HALO, 2 * MLSTM_WIDTH),
                         lambda bi, ci: (bi, jnp.maximum(ci * halo_per_chunk - 1, 0), 0)),
            pl.BlockSpec((BB, L, MLSTM_WIDTH), lambda bi, ci: (bi, ci, 2)),
            pl.BlockSpec((BB, L, MLSTM_WIDTH), lambda bi, ci: (bi, ci, 3)),
            pl.BlockSpec((BB, N_GATES, L), lambda bi, ci: (bi, 0, ci)),
            pl.BlockSpec((CONV_WIDTH, 2 * MLSTM_WIDTH), lambda bi, ci: (0, 0)),
            pl.BlockSpec((N_GATES, 1), lambda bi, ci: (0, 0)),
            pl.BlockSpec((1, MLSTM_WIDTH), lambda bi, ci: (0, 0)),
            pl.BlockSpec((L, L), lambda bi, ci: (0, 0)),
            pl.BlockSpec((CONV_WIDTH - 1, L, L), lambda bi, ci: (0, 0, 0)),
            pl.BlockSpec((CONV_WIDTH - 1, 8, HALO), lambda bi, ci: (0, 0, 0)),
        ],
        out_specs=pl.BlockSpec((BB, L, MLSTM_WIDTH), lambda bi, ci: (bi, ci, 0)),
        out_shape=jax.ShapeDtypeStruct((batch, seq, MLSTM_WIDTH), jnp.bfloat16),
        scratch_shapes=[
            pltpu.VMEM((BB * MLSTM_HEADS, HEAD_DIM, 2 * HEAD_DIM), jnp.float32),
            pltpu.VMEM((BB * MLSTM_HEADS, 8, LANES), jnp.float32),
        ],
        compiler_params=pltpu.CompilerParams(
            dimension_semantics=("parallel", "arbitrary"), vmem_limit_bytes=VMEM_LIMIT),
        name="mlstm",
    )(p3, p3, p3, p3, gates_b, conv_w, gate_b, hn_g, tri, shifts, halo_shifts)


def _out_route_kernel(seq, x_ref, ym_ref, u_ref, up_ref, pw_ref, ps_ref, wo_ref, g2_ref,
                      wrt_ref, br_ref, x1_ref, h2_ref, idx_ref, gate_ref, rank_ref, cnt_ref,
                      carry_ref):
    TM = TM_PROJ
    R = ROUTE_SUB * TM
    i = pl.program_id(0)

    @pl.when(i == 0)
    def _():
        carry_ref[...] = jnp.zeros_like(carry_ref)

    pos0 = (i * R) % seq
    e_id = lax.broadcasted_iota(jnp.int32, (N_EXPERTS, TM), 0).astype(jnp.float32)
    t_row = lax.broadcasted_iota(jnp.int32, (TM, TM), 0)
    t_col = lax.broadcasted_iota(jnp.int32, (TM, TM), 1)
    before = jnp.where(t_row < t_col, 1.0, 0.0).astype(jnp.bfloat16)
    carry = carry_ref[...]
    subs = [slice(sub * TM, (sub + 1) * TM) for sub in range(ROUTE_SUB)]

    halo = jnp.where(pos0 > 0, up_ref[...].astype(jnp.float32), 0.0)
    u_ext = jnp.concatenate([halo, u_ref[...].astype(jnp.float32)], axis=0)
    win_sums = []
    for gi, w in enumerate(POOL_WINDOWS):
        sw = u_ext[:, gi * POOL_GROUP_DIM:(gi + 1) * POOL_GROUP_DIM]
        span = 1
        while span < w:
            sw = sw + pltpu.roll(sw, span, axis=0)
            span *= 2
        win_sums.append(sw)
    y_cats = []
    for sub, rows in enumerate(subs):
        r0 = sub * TM
        pos = (pos0 + r0 + lax.broadcasted_iota(jnp.int32, (TM, 1), 0) + 1).astype(jnp.float32)
        mixed = []
        for gi, w in enumerate(POOL_WINDOWS):
            lo = gi * POOL_GROUP_DIM
            tok = u_ext[HALO + r0:HALO + r0 + TM, lo:lo + POOL_GROUP_DIM]
            pooled = win_sums[gi][HALO + r0:HALO + r0 + TM] / jnp.minimum(pos, float(w)) - tok
            mg = jnp.dot(pooled.astype(jnp.bfloat16), pw_ref[gi],
                         preferred_element_type=jnp.float32)
            mixed.append((mg * ps_ref[:, lo:lo + POOL_GROUP_DIM]).astype(jnp.bfloat16))
        y_cats.append(jnp.concatenate([ym_ref[rows, :]] + mixed, axis=1))

    all_logits = []
    for sub, rows in enumerate(subs):
        r0 = sub * TM
        x1 = x_ref[rows, :] + jnp.dot(y_cats[sub], wo_ref[...], preferred_element_type=jnp.float32)
        x1_ref[rows, :] = x1
        h2 = x1 * lax.rsqrt(jnp.mean(x1 * x1, axis=-1, keepdims=True) + EPS) * g2_ref[...]
        h2b = h2.astype(jnp.bfloat16)
        h2w = _pack_bf16_pairs(h2)
        for s in range(PSLAB):
            h2_ref[pl.ds(r0 * PSLAB + s, TM, stride=PSLAB), :] = h2w[:, s * LANES:(s + 1) * LANES]
        all_logits.append(lax.dot_general(wrt_ref[...], h2b, NT_DIMS,
                                          preferred_element_type=jnp.float32) + br_ref[...])

    for sub, rows in enumerate(subs):
        work = all_logits[sub]
        vals, ids, hots = [], [], []
        for _ in range(TOP_K):
            mk = jnp.max(work, axis=0, keepdims=True)
            ik = jnp.min(jnp.where(work == mk, e_id, float(N_EXPERTS)), axis=0, keepdims=True)
            hot = e_id == ik
            work = jnp.where(hot, -jnp.inf, work)
            vals.append(mk)
            ids.append(ik)
            hots.append(hot)
        ex = [jnp.exp(vk - vals[0]) for vk in vals]
        denom = ex[0] + ex[1] + ex[2] + ex[3]
        gate_ref[:, rows] = jnp.concatenate([e / denom for e in ex], axis=0)
        idx_ref[:, rows] = jnp.concatenate(ids, axis=0).astype(jnp.int32)

        sel_f = sum(jnp.where(hot, 1.0, 0.0) for hot in hots)
        prefix = jnp.dot(sel_f.astype(jnp.bfloat16), before, preferred_element_type=jnp.float32)
        rank_e = carry[:, 0:1] + prefix
        ranks = [jnp.sum(jnp.where(hot, rank_e, 0.0), axis=0, keepdims=True) for hot in hots]
        rank_ref[:, rows] = jnp.concatenate(ranks, axis=0).astype(jnp.int32)
        carry = carry + jnp.sum(sel_f, axis=1, keepdims=True)
    carry_ref[...] = carry
    cnt_ref[...] = carry.astype(jnp.int32)


def _out_route(x2, ym, p, pool_w, pool_s, w_out, g2, wr_t, br, seq):
    T = x2.shape[0]
    TM = ROUTE_SUB * TM_PROJ
    nt = T // TM
    u_blk = N_MAIN // POOL_WIDTH - 1
    halo_per_tile = TM // HALO
    tok_spec = pl.BlockSpec((TOP_K, TM), lambda i: (0, i))
    return pl.pallas_call(
        functools.partial(_out_route_kernel, seq),
        grid=(nt,),
        in_specs=[
            pl.BlockSpec((TM, D_MODEL), lambda i: (i, 0)),
            pl.BlockSpec((TM, MLSTM_WIDTH), lambda i: (i, 0)),
            pl.BlockSpec((TM, POOL_WIDTH), lambda i: (i, u_blk)),
            pl.BlockSpec((HALO, POOL_WIDTH),
                         lambda i: (jnp.maximum(i * halo_per_tile - 1, 0), u_blk)),
            pl.BlockSpec((len(POOL_WINDOWS), POOL_GROUP_DIM, POOL_GROUP_DIM), lambda i: (0, 0, 0)),
            pl.BlockSpec((1, POOL_WIDTH), lambda i: (0, 0)),
            pl.BlockSpec((D_MODEL, D_MODEL), lambda i: (0, 0)),
            pl.BlockSpec((1, D_MODEL), lambda i: (0, 0)),
            pl.BlockSpec((N_EXPERTS, D_MODEL), lambda i: (0, 0)),
            pl.BlockSpec((N_EXPERTS, 1), lambda i: (0, 0)),
        ],
        out_specs=[
            pl.BlockSpec((TM, D_MODEL), lambda i: (i, 0)),
            pl.BlockSpec((TM * PSLAB, LANES), lambda i: (i, 0)),
            tok_spec, tok_spec, tok_spec,
            pl.BlockSpec((N_EXPERTS, LANES), lambda i: (0, 0)),
        ],
        out_shape=[
            jax.ShapeDtypeStruct((T, D_MODEL), jnp.float32),
            jax.ShapeDtypeStruct((T * PSLAB, LANES), jnp.uint32),
            jax.ShapeDtypeStruct((TOP_K, T), jnp.int32),
            jax.ShapeDtypeStruct((TOP_K, T), jnp.float32),
            jax.ShapeDtypeStruct((TOP_K, T), jnp.int32),
            jax.ShapeDtypeStruct((N_EXPERTS, LANES), jnp.int32),
        ],
        scratch_shapes=[
            pltpu.VMEM((N_EXPERTS, LANES), jnp.float32),
        ],
        compiler_params=pltpu.CompilerParams(
            dimension_semantics=("arbitrary",), vmem_limit_bytes=VMEM_LIMIT),
        name="out_route",
    )(x2, ym, p, p, pool_w, pool_s, w_out, g2, wr_t, br)


def _plan(dest_flat, fill):
    n_assign = dest_flat.shape[0]
    n_table = fill.shape[0]
    mesh = plsc.VectorSubcoreMesh(core_axis_name="c", subcore_axis_name="s")

    @pl.kernel(out_type=jax.ShapeDtypeStruct((n_table,), jnp.int32), mesh=mesh,
               scratch_types=[pltpu.VMEM((n_table,), jnp.int32),
                              pltpu.VMEM((PLAN_CHUNK,), jnp.int32)],
               compiler_params=pltpu.CompilerParams(needs_layout_passes=False))
    def plan_kernel(dest_hbm, fill_hbm, out_hbm, table, chunk):
        first = jnp.logical_and(lax.axis_index("c") == 0, lax.axis_index("s") == 0)

        @pl.when(first)
        def _():
            pltpu.sync_copy(fill_hbm, table)

            @pl.loop(0, n_assign // PLAN_CHUNK)
            def _(ci):
                pltpu.sync_copy(dest_hbm.at[pl.ds(ci * PLAN_CHUNK, PLAN_CHUNK)], chunk)

                @pl.loop(0, PLAN_CHUNK // (SC_LANES * PLAN_UNROLL))
                def _(i):
                    for j in range(PLAN_UNROLL):
                        off = (i * PLAN_UNROLL + j) * SC_LANES
                        idx = chunk[pl.ds(off, SC_LANES)]
                        vals = (ci * PLAN_CHUNK + off
                                + lax.broadcasted_iota(jnp.int32, (SC_LANES,), 0))
                        plsc.store_scatter(table, [idx], vals)

            pltpu.sync_copy(table, out_hbm)

    return plan_kernel(dest_flat, fill)


def _expert_kernel(n_tok, bs_ref, slot_ref, h2_ref, wg_ref, bg_ref, wu_ref, bu_ref, wd_ref, bd_ref,
                   yt_ref, *scratch):
    TM = TM_EXPERT
    ROWS = TM * PSLAB
    e = pl.program_id(0)
    n_total = bs_ref[N_EXPERTS]
    xg = scratch[:NBUF]
    ys = scratch[NBUF:2 * NBUF]
    wgb_ref, wub_ref, wdb_ref, gsem, ssem = scratch[2 * NBUF:]

    def token_of(a):
        return a & (n_tok - 1) if n_tok & (n_tok - 1) == 0 else lax.rem(a, n_tok)

    def start_gather(blk, par):
        base = (blk + 1) * TM
        for r in range(TM):
            t = token_of(slot_ref[base + r])
            pltpu.make_async_copy(h2_ref.at[pl.ds(pl.multiple_of(t * PSLAB, PSLAB), PSLAB), :],
                                  xg[par].at[pl.ds(r * PSLAB, PSLAB), :], gsem.at[par]
                                  ).start(priority=ROW_DMA_PRIORITY)

    def wait_gather(par):
        pltpu.make_async_copy(h2_ref.at[pl.ds(0, ROWS), :], xg[0], gsem.at[par]).wait()

    def start_scatter(blk, par):
        base = (blk + 1) * TM
        for r in range(TM):
            a = slot_ref[base + r]
            pltpu.make_async_copy(ys[par].at[pl.ds(r * PSLAB, PSLAB), :],
                                  yt_ref.at[pl.ds(pl.multiple_of(a * PSLAB, PSLAB), PSLAB), :],
                                  ssem.at[par]).start(priority=ROW_DMA_PRIORITY)

    def wait_scatter(par):
        pltpu.make_async_copy(ys[0], yt_ref.at[pl.ds(0, ROWS), :], ssem.at[par]).wait()

    @pl.when(e == 0)
    def _():
        for blk in range(NBUF - 1):
            start_gather(blk, blk)
        for par in range(NBUF):
            ys[par][...] = jnp.zeros_like(ys[par])
            dump = yt_ref.at[pl.ds((n_tok * TOP_K + par * TM) * PSLAB, ROWS), :]
            cp = pltpu.make_async_copy(ys[par], dump, ssem.at[par])
            cp.start()
            cp.wait()

    wgb_ref[...] = wg_ref[0].astype(jnp.bfloat16)
    wub_ref[...] = wu_ref[0].astype(jnp.bfloat16)
    wdb_ref[...] = wd_ref[0].astype(jnp.bfloat16)

    def block_step(g, par):
        prv = (par + NBUF - 1) % NBUF
        wait_gather(par)

        @pl.when(g >= NBUF - 1)
        def _():
            wait_scatter(par)

        start_gather(g + NBUF - 1, prv)
        start_scatter(g - 1, prv)
        words = [xg[par][pl.ds(s, TM, stride=PSLAB), :] for s in range(PSLAB)]
        x = jnp.concatenate([_unpack_lo(w).astype(jnp.bfloat16) for w in words]
                            + [_unpack_hi(w).astype(jnp.bfloat16) for w in words], axis=1)
        gate = jnp.dot(x, wgb_ref[...], preferred_element_type=jnp.float32) + bg_ref[0]
        up = jnp.dot(x, wub_ref[...], preferred_element_type=jnp.float32) + bu_ref[0]
        gate = jnp.minimum(gate, SWIGLU_LIMIT)
        up = jnp.clip(up, -SWIGLU_LIMIT, SWIGLU_LIMIT)
        glu = gate * _sigmoid(SWIGLU_ALPHA * gate)
        act = (glu * (up + 1.0)).astype(jnp.bfloat16)
        y = jnp.dot(act, wdb_ref[...], preferred_element_type=jnp.float32) + bd_ref[0]
        packed = _pack_bf16_pairs(y)
        for s in range(PSLAB):
            ys[par][pl.ds(s, TM, stride=PSLAB), :] = packed[:, s * LANES:(s + 1) * LANES]

    def body(g, carry):
        for par in range(NBUF):
            pl.when(g % NBUF == par)(functools.partial(block_step, g, par))
        return carry

    lax.fori_loop(bs_ref[e], bs_ref[e + 1], body, 0)

    @pl.when(e == N_EXPERTS - 1)
    def _():
        g = n_total
        for par in range(NBUF):
            @pl.when((g - 1) % NBUF == par)
            def _():
                start_scatter(g - 1, par)
        for j in range(NBUF - 1):
            wait_gather((g + j) % NBUF)
        wait_scatter((g - 1) % NBUF)
        for j in range(2, NBUF + 1):
            @pl.when(g >= j - 1)
            def _():
                wait_scatter((g + NBUF - j) % NBUF)


def _experts(block_start, slot_buf, h2_slab, w_gate, b_gate, w_up, b_up, w_down, b_down, n_tok):
    TM = TM_EXPERT
    n_assign = n_tok * TOP_K
    w_spec = pl.BlockSpec((1, D_MODEL, D_FF), lambda e, bs, sl: (e, 0, 0))
    bias_spec = pl.BlockSpec((1, 1, D_FF), lambda e, bs, sl: (e, 0, 0))
    buf = pltpu.VMEM((TM * PSLAB, LANES), jnp.uint32)
    grid_spec = pltpu.PrefetchScalarGridSpec(
        num_scalar_prefetch=2,
        grid=(N_EXPERTS,),
        in_specs=[
            pl.BlockSpec(memory_space=pl.ANY),
            w_spec, bias_spec, w_spec, bias_spec, w_spec, bias_spec,
        ],
        out_specs=pl.BlockSpec(memory_space=pl.ANY),
        scratch_shapes=[
            *([buf] * (2 * NBUF)),
            pltpu.VMEM((D_MODEL, D_FF), jnp.bfloat16),
            pltpu.VMEM((D_MODEL, D_FF), jnp.bfloat16),
            pltpu.VMEM((D_FF, D_MODEL), jnp.bfloat16),
            pltpu.SemaphoreType.DMA((NBUF,)),
            pltpu.SemaphoreType.DMA((NBUF,)),
        ],
    )
    return pl.pallas_call(
        functools.partial(_expert_kernel, n_tok),
        grid_spec=grid_spec,
        out_shape=jax.ShapeDtypeStruct(((n_assign + NBUF * TM) * PSLAB, LANES), jnp.uint32),
        compiler_params=pltpu.CompilerParams(
            dimension_semantics=("arbitrary",), vmem_limit_bytes=VMEM_LIMIT),
        name="experts",
    )(block_start, slot_buf, h2_slab, w_gate, b_gate, w_up, b_up, w_down, b_down)


def _combine_kernel(normalize, x1_ref, y0_ref, y1_ref, y2_ref, y3_ref, gate_ref, g_ref, o_ref):
    TM = TM_COMBINE
    gates = jnp.concatenate([gate_ref[...], jnp.zeros((8 - TOP_K, TM), jnp.float32)], axis=0)
    g_cols = jnp.transpose(gates)
    g_bc = [jnp.broadcast_to(g_cols[:, k:k + 1], (TM, LANES)) for k in range(TOP_K)]
    ssq = jnp.zeros((TM, LANES), jnp.float32)
    parts = [x1_ref[:, s * LANES:(s + 1) * LANES] for s in range(SLAB)]
    for s in range(PSLAB):
        for k, y_ref in enumerate((y0_ref, y1_ref, y2_ref, y3_ref)):
            w = y_ref[pl.ds(s, TM, stride=PSLAB), :]
            parts[s] = parts[s] + g_bc[k] * _unpack_lo(w)
            parts[PSLAB + s] = parts[PSLAB + s] + g_bc[k] * _unpack_hi(w)
    for acc in parts:
        ssq = ssq + acc * acc
    if normalize:
        inv = lax.rsqrt(jnp.sum(ssq, axis=-1, keepdims=True) * (1.0 / D_MODEL) + EPS)
        for s in range(SLAB):
            o_ref[:, s * LANES:(s + 1) * LANES] = parts[s] * inv * g_ref[:, s * LANES:(s + 1) * LANES]
    else:
        for s in range(SLAB):
            o_ref[:, s * LANES:(s + 1) * LANES] = parts[s]


def _combine(x1, y_tok, gate_t, gf, normalize):
    T = x1.shape[0]
    TM = TM_COMBINE
    nt = T // TM

    def y_spec(k):
        return pl.BlockSpec((TM * PSLAB, LANES), lambda i: (k * nt + i, 0))

    return pl.pallas_call(
        functools.partial(_combine_kernel, normalize),
        grid=(nt,),
        in_specs=[
            pl.BlockSpec((TM, D_MODEL), lambda i: (i, 0)),
            y_spec(0), y_spec(1), y_spec(2), y_spec(3),
            pl.BlockSpec((TOP_K, TM), lambda i: (0, i)),
            pl.BlockSpec((1, D_MODEL), lambda i: (0, 0)),
        ],
        out_specs=pl.BlockSpec((TM, D_MODEL), lambda i: (i, 0)),
        out_shape=jax.ShapeDtypeStruct((T, D_MODEL), jnp.float32),
        compiler_params=pltpu.CompilerParams(
            dimension_semantics=("parallel",), vmem_limit_bytes=VMEM_LIMIT),
        name="combine",
    )(x1, y_tok, y_tok, y_tok, y_tok, gate_t, gf)


def kernel(x, norm1_g, w_in, ig_b, fg_b, conv_w, head_norm_g, pool_w, pool_scale, w_out, norm2_g,
           w_router, b_router, w_gate, b_gate, w_up, b_up, w_down, b_down, normf_g):
    B, S, D = x.shape
    T = B * S
    depth = norm1_g.shape[0]
    W = MLSTM_WIDTH
    f32, bf16 = jnp.float32, jnp.bfloat16

    L = CHUNK
    t_l = lax.broadcasted_iota(jnp.int32, (L, L), 0)
    t_r = lax.broadcasted_iota(jnp.int32, (L, L), 1)
    tri = (t_r <= t_l).astype(f32)
    shifts = jnp.stack([(t_l - t_r == CONV_WIDTH - 1 - j).astype(bf16)
                        for j in range(CONV_WIDTH - 1)])
    h_t = lax.broadcasted_iota(jnp.int32, (8, HALO), 0)
    h_r = lax.broadcasted_iota(jnp.int32, (8, HALO), 1)
    halo_shifts = jnp.stack([(h_r - HALO - h_t == -(CONV_WIDTH - 1 - j)).astype(bf16)
                             for j in range(CONV_WIDTH - 1)])

    n_assign = T * TOP_K
    n_blocks = -(-n_assign // TM_EXPERT) + N_EXPERTS
    n_rows = n_blocks * TM_EXPERT
    n_table = n_rows + NBUF * TM_EXPERT
    fill = n_assign + ((jnp.arange(n_table, dtype=jnp.int32) + (NBUF - 1) * TM_EXPERT)
                       % (NBUF * TM_EXPERT))
    x2 = x.reshape(T, D)
    for l in range(depth):
        w = w_in[l]
        w_a = w[:, :4 * W].astype(bf16)
        w_u = w[:, 4 * W + N_GATES:].astype(bf16)
        wg_t = jnp.zeros((BF16_SUBLANES, D), bf16).at[:N_GATES].set(
            w[:, 4 * W:4 * W + N_GATES].T.astype(bf16))
        p, gates_t = _in_proj(x2, norm1_g[l][None, :], w_a, w_u, wg_t)

        gate_b = jnp.concatenate([ig_b[l], fg_b[l]])[:, None].astype(f32)
        gates_b = gates_t.reshape(N_GATES, B, S).transpose(1, 0, 2)
        ym = _mlstm(p.reshape(B, S, N_MAIN), gates_b, conv_w[l].astype(f32), gate_b,
                    head_norm_g[l][None, :], tri, shifts, halo_shifts).reshape(T, W)

        x1, h2, idx_t, gate_t, rank_t, cnt = _out_route(
            x2, ym, p, pool_w[l].astype(bf16), pool_scale[l][None, :], w_out[l].astype(bf16),
            norm2_g[l][None, :], w_router[l].T.astype(bf16), b_router[l][:, None], S)

        counts = cnt[:, 0]
        padded = ((counts + TM_EXPERT - 1) // TM_EXPERT) * TM_EXPERT
        padded_end = jnp.cumsum(padded)
        padded_start = padded_end - padded
        expert_ids = jnp.arange(N_EXPERTS, dtype=jnp.int32)[:, None, None]
        start_of = jnp.sum(jnp.where(idx_t[None] == expert_ids, padded_start[:, None, None], 0), axis=0)
        dest = start_of + rank_t
        block_start = jnp.concatenate(
            [jnp.zeros((1,), jnp.int32), (padded_end // TM_EXPERT).astype(jnp.int32)])

        slot_buf = _plan(dest.reshape(-1) + TM_EXPERT, fill)
        y_tok = _experts(block_start, slot_buf, h2, w_gate[l], b_gate[l][:, None, :],
                         w_up[l], b_up[l][:, None, :], w_down[l], b_down[l][:, None, :], T)
        last = l + 1 == depth
        x2 = _combine(x1, y_tok, gate_t, normf_g[None, :], last)
    return x2.reshape(B, S, D)
```

```python
import functools

import jax
import jax.numpy as jnp
from jax import lax
from jax.experimental import pallas as pl
from jax.experimental.pallas import tpu as pltpu
from jax.experimental.pallas import tpu_sc as plsc

D_MODEL = 1024
MLSTM_WIDTH = 512
MLSTM_HEADS = 4
HEAD_DIM = 128
CONV_WIDTH = 4
POOL_WIDTH = 512
POOL_WINDOWS = (2, 4, 8, 16)
POOL_GROUP_DIM = 128
N_EXPERTS = 32
TOP_K = 4
D_FF = 1024
SWIGLU_LIMIT = 7.0
SWIGLU_ALPHA = 1.702
EPS = 1e-5

N_MAIN = 4 * MLSTM_WIDTH + POOL_WIDTH
N_GATES = 2 * MLSTM_HEADS

LANES = 128
BF16_SUBLANES = 16
VMEM_LIMIT = 56 * 1024 * 1024

TM_PROJ = 512
PROJ_SUB = 2
TM_COMBINE = 1024
ROUTE_SUB = 2
CHUNK = 256
MLSTM_BATCH = 4
HALO = 16
TM_EXPERT = 512
NBUF = 3
SLAB = D_MODEL // LANES
PSLAB = SLAB // 2
PLAN_CHUNK = 8192
SC_LANES = 16
PLAN_UNROLL = 8

NT_DIMS = (((1,), (1,)), ((), ()))


def _sigmoid(x):
    return 1.0 / (1.0 + jnp.exp(-x))


def _pack_bf16_pairs(v):
    half = v.shape[1] // 2
    lo = pltpu.bitcast(v[:, :half].astype(jnp.bfloat16).astype(jnp.float32), jnp.uint32)
    hi = pltpu.bitcast(v[:, half:].astype(jnp.bfloat16).astype(jnp.float32), jnp.uint32)
    return (lo >> 16) | (hi & jnp.uint32(0xFFFF0000))


def _unpack_lo(w):
    return pltpu.bitcast(w << 16, jnp.float32)


def _unpack_hi(w):
    return pltpu.bitcast(w & jnp.uint32(0xFFFF0000), jnp.float32)


def _in_proj_kernel(x_ref, g_ref, wa_ref, wu_ref, wgt_ref, p_ref, gt_ref):
    n_a = wa_ref.shape[1]
    for sub in range(PROJ_SUB):
        rows = slice(sub * TM_PROJ, (sub + 1) * TM_PROJ)
        x = x_ref[rows, :]
        h = x * lax.rsqrt(jnp.mean(x * x, axis=-1, keepdims=True) + EPS) * g_ref[...]
        hb = h.astype(jnp.bfloat16)
        p_ref[rows, :n_a] = jnp.dot(hb, wa_ref[...],
                                    preferred_element_type=jnp.float32).astype(p_ref.dtype)
        p_ref[rows, n_a:] = jnp.dot(hb, wu_ref[...],
                                    preferred_element_type=jnp.float32).astype(p_ref.dtype)
        gt = lax.dot_general(wgt_ref[...], hb, NT_DIMS, preferred_element_type=jnp.float32)
        gt_ref[:, rows] = gt[:N_GATES]


def _in_proj(x2, g1, w_a, w_u, wg_t):
    T = x2.shape[0]
    return pl.pallas_call(
        _in_proj_kernel,
        grid=(T // (PROJ_SUB * TM_PROJ),),
        in_specs=[
            pl.BlockSpec((PROJ_SUB * TM_PROJ, D_MODEL), lambda i: (i, 0)),
            pl.BlockSpec((1, D_MODEL), lambda i: (0, 0)),
            pl.BlockSpec(w_a.shape, lambda i: (0, 0)),
            pl.BlockSpec(w_u.shape, lambda i: (0, 0)),
            pl.BlockSpec((BF16_SUBLANES, D_MODEL), lambda i: (0, 0)),
        ],
        out_specs=[
            pl.BlockSpec((PROJ_SUB * TM_PROJ, N_MAIN), lambda i: (i, 0)),
            pl.BlockSpec((N_GATES, PROJ_SUB * TM_PROJ), lambda i: (0, i)),
        ],
        out_shape=[
            jax.ShapeDtypeStruct((T, N_MAIN), jnp.bfloat16),
            jax.ShapeDtypeStruct((N_GATES, T), jnp.float32),
        ],
        compiler_params=pltpu.CompilerParams(
            dimension_semantics=("parallel",), vmem_limit_bytes=VMEM_LIMIT),
        name="in_proj",
    )(x2, g1, w_a, w_u, wg_t)


def _mlstm_kernel(qk_ref, qkp_ref, v_ref, o_ref, gt_ref, convw_ref, gb_ref, hng_ref,
                  tri_ref, shift_ref, hshift_ref, y_ref, cn_ref, m_ref):
    L = CHUNK
    c = pl.program_id(1)

    @pl.when(c == 0)
    def _():
        cn_ref[...] = jnp.zeros_like(cn_ref)
        m_ref[...] = jnp.zeros_like(m_ref)

    row_id = lax.broadcasted_iota(jnp.int32, (L, L), 0)
    col_id = lax.broadcasted_iota(jnp.int32, (L, L), 1)
    causal = col_id <= row_id
    ones_blk = jnp.ones((L, HEAD_DIM), jnp.bfloat16)
    lane = lax.broadcasted_iota(jnp.int32, (MLSTM_HEADS, L), 1)

    gate_terms = []
    for bb in range(MLSTM_BATCH):
        gt = gt_ref[bb] + gb_ref[...]
        f = gt[MLSTM_HEADS:]
        lf = jnp.minimum(f, 0.0) - jnp.log(1.0 + jnp.exp(-jnp.abs(f)))
        ig = gt[:MLSTM_HEADS]
        b_rows = lax.dot_general(lf, tri_ref[...], NT_DIMS, precision=lax.Precision.HIGHEST,
                                 preferred_element_type=jnp.float32)
        c_rows = ig - b_rows
        cm_rows = c_rows
        d = 1
        while d < L:
            cm_rows = jnp.maximum(
                cm_rows, jnp.where(lane >= d, pltpu.roll(cm_rows, d, axis=1), -jnp.inf))
            d *= 2
        gate_terms.append((b_rows, c_rows, cm_rows))

    conv_terms = []
    for bb in range(MLSTM_BATCH):
        x_cur = qk_ref[bb]
        x_prev = jnp.where(c > 0, qkp_ref[bb], jnp.zeros((HALO, 2 * MLSTM_WIDTH), jnp.bfloat16))
        acc = convw_ref[CONV_WIDTH - 1:CONV_WIDTH, :] * x_cur.astype(jnp.float32)
        for j in range(CONV_WIDTH - 1):
            sh = jnp.dot(shift_ref[j], x_cur, preferred_element_type=jnp.float32)
            top = sh[:8] + jnp.dot(hshift_ref[j], x_prev, preferred_element_type=jnp.float32)
            sh = jnp.concatenate([top, sh[8:]], axis=0)
            acc = acc + convw_ref[j:j + 1, :] * sh
        qk = acc * _sigmoid(acc)
        q_all = qk[:, :MLSTM_WIDTH].astype(jnp.bfloat16)
        k_t = jnp.transpose(qk[:, MLSTM_WIDTH:] * (HEAD_DIM ** -0.5))
        conv_terms.append((q_all, k_t))

    for bb in range(MLSTM_BATCH):
        b_rows, c_rows, cm_rows = gate_terms[bb]
        q_all, k_t = conv_terms[bb]
        m_in4 = jnp.concatenate(
            [m_ref[bb * MLSTM_HEADS + h][0:1, 0:1] for h in range(MLSTM_HEADS)], axis=0)
        mx_rows = jnp.maximum(cm_rows, m_in4)
        inter_rows = jnp.exp(m_in4 - mx_rows)
        einv_rows = jnp.exp(-(b_rows + mx_rows))
        fac_t = jnp.transpose(jnp.concatenate(
            [mx_rows, inter_rows, einv_rows, jnp.zeros_like(mx_rows)], axis=0))

        for h in range(MLSTM_HEADS):
            lo = h * HEAD_DIM
            st = bb * MLSTM_HEADS + h
            q = q_all[:, lo:lo + HEAD_DIM]
            kt = k_t[lo:lo + HEAD_DIM, :]
            v_ext = jnp.concatenate([v_ref[bb, :, lo:lo + HEAD_DIM], ones_blk], axis=1)
            mx_col = fac_t[:, h:h + 1]
            inter_col = fac_t[:, MLSTM_HEADS + h:MLSTM_HEADS + h + 1]
            einv_col = fac_t[:, 2 * MLSTM_HEADS + h:2 * MLSTM_HEADS + h + 1]
            c_row = c_rows[h:h + 1, :]
            b_tot = b_rows[h:h + 1, L - 1:L]
            cm_tot = cm_rows[h:h + 1, L - 1:L]
            m_in = m_ref[st][0:1, 0:1]
            cn = cn_ref[st]

            s_qk = jnp.dot(q, kt.astype(jnp.bfloat16), preferred_element_type=jnp.float32)
            s = (s_qk * jnp.exp(jnp.where(causal, c_row - mx_col, -jnp.inf))).astype(jnp.bfloat16)
            num = (jnp.dot(s, v_ext, preferred_element_type=jnp.float32)
                   + inter_col * jnp.dot(q, cn.astype(jnp.bfloat16),
                                         preferred_element_type=jnp.float32))
            den = num[:, HEAD_DIM:]
            hh = num[:, :HEAD_DIM] / jnp.maximum(jnp.abs(den), einv_col)

            mu = jnp.mean(hh, axis=-1, keepdims=True)
            dv = hh - mu
            var = jnp.mean(dv * dv, axis=-1, keepdims=True)
            hn = dv * lax.rsqrt(var + EPS) * hng_ref[:, lo:lo + HEAD_DIM]
            og = _sigmoid(o_ref[bb, :, lo:lo + HEAD_DIM].astype(jnp.float32))
            y_ref[bb, :, lo:lo + HEAD_DIM] = (og * hn).astype(y_ref.dtype)

            m_loc = b_tot + cm_tot
            kw_t = (kt * jnp.exp(c_row - cm_tot)).astype(jnp.bfloat16)
            c_loc = jnp.dot(kw_t, v_ext, preferred_element_type=jnp.float32)
            m_new = jnp.maximum(b_tot + m_in, m_loc)
            s_old = jnp.exp(b_tot + m_in - m_new)
            s_loc = jnp.exp(m_loc - m_new)
            cn_ref[st] = s_old * cn + s_loc * c_loc
            m_ref[st] = jnp.broadcast_to(m_new, m_ref.shape[1:])


def _mlstm(p3, gates_b, conv_w, gate_b, hn_g, tri, shifts, halo_shifts):
    batch, seq, _ = p3.shape
    L = CHUNK
    BB = MLSTM_BATCH
    halo_per_chunk = L // HALO
    return pl.pallas_call(
        _mlstm_kernel,
        grid=(batch // BB, seq // L),
        in_specs=[
            pl.BlockSpec((BB, L, 2 * MLSTM_WIDTH), lambda bi, ci: (bi, ci, 0)),
            pl.BlockSpec((BB, HALO, 2 * MLSTM_WIDTH),
                         lambda bi, ci: (bi, jnp.maximum(ci * halo_per_chunk - 1, 0), 0)),
            pl.BlockSpec((BB, L, MLSTM_WIDTH), lambda bi, ci: (bi, ci, 2)),
            pl.BlockSpec((BB, L, MLSTM_WIDTH), lambda bi, ci: (bi, ci, 3)),
            pl.BlockSpec((BB, N_GATES, L), lambda bi, ci: (bi, 0, ci)),
            pl.BlockSpec((CONV_WIDTH, 2 * MLSTM_WIDTH), lambda bi, ci: (0, 0)),
            pl.BlockSpec((N_GATES, 1), lambda bi, ci: (0, 0)),
            pl.BlockSpec((1, MLSTM_WIDTH), lambda bi, ci: (0, 0)),
            pl.BlockSpec((L, L), lambda bi, ci: (0, 0)),
            pl.BlockSpec((CONV_WIDTH - 1, L, L), lambda bi, ci: (0, 0, 0)),
            pl.BlockSpec((CONV_WIDTH - 1, 8, HALO), lambda bi, ci: (0, 0, 0)),
        ],
        out_specs=pl.BlockSpec((BB, L, MLSTM_WIDTH), lambda bi, ci: (bi, ci, 0)),
        out_shape=jax.ShapeDtypeStruct((batch, seq, MLSTM_WIDTH), jnp.bfloat16),
        scratch_shapes=[
            pltpu.VMEM((BB * MLSTM_HEADS, HEAD_DIM, 2 * HEAD_DIM), jnp.float32),
            pltpu.VMEM((BB * MLSTM_HEADS, 8, LANES), jnp.float32),
        ],
        compiler_params=pltpu.CompilerParams(
            dimension_semantics=("parallel", "arbitrary"), vmem_limit_bytes=VMEM_LIMIT),
        name="mlstm",
    )(p3, p3, p3, p3, gates_b, conv_w, gate_b, hn_g, tri, shifts, halo_shifts)


def _out_route_kernel(seq, x_ref, ym_ref, u_ref, up_ref, pw_ref, ps_ref, wo_ref, g2_ref,
                      wrt_ref, br_ref, x1_ref, h2_ref, idx_ref, gate_ref, rank_ref, cnt_ref,
                      carry_ref):
    TM = TM_PROJ
    R = ROUTE_SUB * TM
    i = pl.program_id(0)

    @pl.when(i == 0)
    def _():
        carry_ref[...] = jnp.zeros_like(carry_ref)

    pos0 = (i * R) % seq
    e_id = lax.broadcasted_iota(jnp.int32, (N_EXPERTS, TM), 0).astype(jnp.float32)
    t_row = lax.broadcasted_iota(jnp.int32, (TM, TM), 0)
    t_col = lax.broadcasted_iota(jnp.int32, (TM, TM), 1)
    before = jnp.where(t_row < t_col, 1.0, 0.0).astype(jnp.bfloat16)
    carry = carry_ref[...]
    subs = [slice(sub * TM, (sub + 1) * TM) for sub in range(ROUTE_SUB)]

    halo = jnp.where(pos0 > 0, up_ref[...].astype(jnp.float32), 0.0)
    u_ext = jnp.concatenate([halo, u_ref[...].astype(jnp.float32)], axis=0)
    win_sums = []
    for gi, w in enumerate(POOL_WINDOWS):
        sw = u_ext[:, gi * POOL_GROUP_DIM:(gi + 1) * POOL_GROUP_DIM]
        span = 1
        while span < w:
            sw = sw + pltpu.roll(sw, span, axis=0)
            span *= 2
        win_sums.append(sw)
    y_cats = []
    for sub, rows in enumerate(subs):
        r0 = sub * TM
        pos = (pos0 + r0 + lax.broadcasted_iota(jnp.int32, (TM, 1), 0) + 1).astype(jnp.float32)
        mixed = []
        for gi, w in enumerate(POOL_WINDOWS):
            lo = gi * POOL_GROUP_DIM
            tok = u_ext[HALO + r0:HALO + r0 + TM, lo:lo + POOL_GROUP_DIM]
            pooled = win_sums[gi][HALO + r0:HALO + r0 + TM] / jnp.minimum(pos, float(w)) - tok
            mg = jnp.dot(pooled.astype(jnp.bfloat16), pw_ref[gi],
                         preferred_element_type=jnp.float32)
            mixed.append((mg * ps_ref[:, lo:lo + POOL_GROUP_DIM]).astype(jnp.bfloat16))
        y_cats.append(jnp.concatenate([ym_ref[rows, :]] + mixed, axis=1))

    all_logits = []
    for sub, rows in enumerate(subs):
        r0 = sub * TM
        x1 = x_ref[rows, :] + jnp.dot(y_cats[sub], wo_ref[...], preferred_element_type=jnp.float32)
        x1_ref[rows, :] = x1
        h2 = x1 * lax.rsqrt(jnp.mean(x1 * x1, axis=-1, keepdims=True) + EPS) * g2_ref[...]
        h2b = h2.astype(jnp.bfloat16)
        h2w = _pack_bf16_pairs(h2)
        for s in range(PSLAB):
            h2_ref[pl.ds(r0 * PSLAB + s, TM, stride=PSLAB), :] = h2w[:, s * LANES:(s + 1) * LANES]
        all_logits.append(lax.dot_general(wrt_ref[...], h2b, NT_DIMS,
                                          preferred_element_type=jnp.float32) + br_ref[...])

    for sub, rows in enumerate(subs):
        work = all_logits[sub]
        vals, ids, hots = [], [], []
        for _ in range(TOP_K):
            mk = jnp.max(work, axis=0, keepdims=True)
            ik = jnp.min(jnp.where(work == mk, e_id, float(N_EXPERTS)), axis=0, keepdims=True)
            hot = e_id == ik
            work = jnp.where(hot, -jnp.inf, work)
            vals.append(mk)
            ids.append(ik)
            hots.append(hot)
        ex = [jnp.exp(vk - vals[0]) for vk in vals]
        denom = ex[0] + ex[1] + ex[2] + ex[3]
        gate_ref[:, rows] = jnp.concatenate([e / denom for e in ex], axis=0)
        idx_ref[:, rows] = jnp.concatenate(ids, axis=0).astype(jnp.int32)

        sel_f = sum(jnp.where(hot, 1.0, 0.0) for hot in hots)
        prefix = jnp.dot(sel_f.astype(jnp.bfloat16), before, preferred_element_type=jnp.float32)
        rank_e = carry[:, 0:1] + prefix
        ranks = [jnp.sum(jnp.where(hot, rank_e, 0.0), axis=0, keepdims=True) for hot in hots]
        rank_ref[:, rows] = jnp.concatenate(ranks, axis=0).astype(jnp.int32)
        carry = carry + jnp.sum(sel_f, axis=1, keepdims=True)
    carry_ref[...] = carry
    cnt_ref[...] = carry.astype(jnp.int32)


def _out_route(x2, ym, p, pool_w, pool_s, w_out, g2, wr_t, br, seq):
    T = x2.shape[0]
    TM = ROUTE_SUB * TM_PROJ
    nt = T // TM
    u_blk = N_MAIN // POOL_WIDTH - 1
    halo_per_tile = TM // HALO
    tok_spec = pl.BlockSpec((TOP_K, TM), lambda i: (0, i))
    return pl.pallas_call(
        functools.partial(_out_route_kernel, seq),
        grid=(nt,),
        in_specs=[
            pl.BlockSpec((TM, D_MODEL), lambda i: (i, 0)),
            pl.BlockSpec((TM, MLSTM_WIDTH), lambda i: (i, 0)),
            pl.BlockSpec((TM, POOL_WIDTH), lambda i: (i, u_blk)),
            pl.BlockSpec((HALO, POOL_WIDTH),
                         lambda i: (jnp.maximum(i * halo_per_tile - 1, 0), u_blk)),
            pl.BlockSpec((len(POOL_WINDOWS), POOL_GROUP_DIM, POOL_GROUP_DIM), lambda i: (0, 0, 0)),
            pl.BlockSpec((1, POOL_WIDTH), lambda i: (0, 0)),
            pl.BlockSpec((D_MODEL, D_MODEL), lambda i: (0, 0)),
            pl.BlockSpec((1, D_MODEL), lambda i: (0, 0)),
            pl.BlockSpec((N_EXPERTS, D_MODEL), lambda i: (0, 0)),
            pl.BlockSpec((N_EXPERTS, 1), lambda i: (0, 0)),
        ],
        out_specs=[
            pl.BlockSpec((TM, D_MODEL), lambda i: (i, 0)),
            pl.BlockSpec((TM * PSLAB, LANES), lambda i: (i, 0)),
            tok_spec, tok_spec, tok_spec,
            pl.BlockSpec((N_EXPERTS, LANES), lambda i: (0, 0)),
        ],
        out_shape=[
            jax.ShapeDtypeStruct((T, D_MODEL), jnp.float32),
            jax.ShapeDtypeStruct((T * PSLAB, LANES), jnp.uint32),
            jax.ShapeDtypeStruct((TOP_K, T), jnp.int32),
            jax.ShapeDtypeStruct((TOP_K, T), jnp.float32),
            jax.ShapeDtypeStruct((TOP_K, T), jnp.int32),
            jax.ShapeDtypeStruct((N_EXPERTS, LANES), jnp.int32),
        ],
        scratch_shapes=[
            pltpu.VMEM((N_EXPERTS, LANES), jnp.float32),
        ],
        compiler_params=pltpu.CompilerParams(
            dimension_semantics=("arbitrary",), vmem_limit_bytes=VMEM_LIMIT),
        name="out_route",
    )(x2, ym, p, p, pool_w, pool_s, w_out, g2, wr_t, br)


def _plan(dest_flat, fill):
    n_assign = dest_flat.shape[0]
    n_table = fill.shape[0]
    mesh = plsc.VectorSubcoreMesh(core_axis_name="c", subcore_axis_name="s")

    @pl.kernel(out_type=jax.ShapeDtypeStruct((n_table,), jnp.int32), mesh=mesh,
               scratch_types=[pltpu.VMEM((n_table,), jnp.int32),
                              pltpu.VMEM((PLAN_CHUNK,), jnp.int32)],
               compiler_params=pltpu.CompilerParams(needs_layout_passes=False))
    def plan_kernel(dest_hbm, fill_hbm, out_hbm, table, chunk):
        first = jnp.logical_and(lax.axis_index("c") == 0, lax.axis_index("s") == 0)

        @pl.when(first)
        def _():
            pltpu.sync_copy(fill_hbm, table)

            @pl.loop(0, n_assign // PLAN_CHUNK)
            def _(ci):
                pltpu.sync_copy(dest_hbm.at[pl.ds(ci * PLAN_CHUNK, PLAN_CHUNK)], chunk)

                @pl.loop(0, PLAN_CHUNK // (SC_LANES * PLAN_UNROLL))
                def _(i):
                    for j in range(PLAN_UNROLL):
                        off = (i * PLAN_UNROLL + j) * SC_LANES
                        idx = chunk[pl.ds(off, SC_LANES)]
                        vals = (ci * PLAN_CHUNK + off
                                + lax.broadcasted_iota(jnp.int32, (SC_LANES,), 0))
                        plsc.store_scatter(table, [idx], vals)

            pltpu.sync_copy(table, out_hbm)

    return plan_kernel(dest_flat, fill)


def _expert_kernel(n_tok, bs_ref, slot_ref, h2_ref, wg_ref, bg_ref, wu_ref, bu_ref, wd_ref, bd_ref,
                   yt_ref, *scratch):
    TM = TM_EXPERT
    ROWS = TM * PSLAB
    e = pl.program_id(0)
    n_total = bs_ref[N_EXPERTS]
    xg = scratch[:NBUF]
    ys = scratch[NBUF:2 * NBUF]
    wgb_ref, wub_ref, wdb_ref, gsem, ssem = scratch[2 * NBUF:]

    def token_of(a):
        return a & (n_tok - 1) if n_tok & (n_tok - 1) == 0 else lax.rem(a, n_tok)

    def start_gather(blk, par):
        base = (blk + 1) * TM
        for r in range(TM):
            t = token_of(slot_ref[base + r])
            pltpu.make_async_copy(h2_ref.at[pl.ds(pl.multiple_of(t * PSLAB, PSLAB), PSLAB), :],
                                  xg[par].at[pl.ds(r * PSLAB, PSLAB), :], gsem.at[par]).start()

    def wait_gather(par):
        pltpu.make_async_copy(h2_ref.at[pl.ds(0, ROWS), :], xg[0], gsem.at[par]).wait()

    def start_scatter(blk, par):
        base = (blk + 1) * TM
        for r in range(TM):
            a = slot_ref[base + r]
            pltpu.make_async_copy(ys[par].at[pl.ds(r * PSLAB, PSLAB), :],
                                  yt_ref.at[pl.ds(pl.multiple_of(a * PSLAB, PSLAB), PSLAB), :],
                                  ssem.at[par]).start()

    def wait_scatter(par):
        pltpu.make_async_copy(ys[0], yt_ref.at[pl.ds(0, ROWS), :], ssem.at[par]).wait()

    @pl.when(e == 0)
    def _():
        for blk in range(NBUF - 1):
            start_gather(blk, blk)
        for par in range(NBUF):
            ys[par][...] = jnp.zeros_like(ys[par])
            dump = yt_ref.at[pl.ds((n_tok * TOP_K + par * TM) * PSLAB, ROWS), :]
            cp = pltpu.make_async_copy(ys[par], dump, ssem.at[par])
            cp.start()
            cp.wait()

    wgb_ref[...] = wg_ref[0].astype(jnp.bfloat16)
    wub_ref[...] = wu_ref[0].astype(jnp.bfloat16)
    wdb_ref[...] = wd_ref[0].astype(jnp.bfloat16)

    def block_step(g, par):
        prv = (par + NBUF - 1) % NBUF
        wait_gather(par)

        @pl.when(g >= NBUF - 1)
        def _():
            wait_scatter(par)

        start_gather(g + NBUF - 1, prv)
        start_scatter(g - 1, prv)
        words = [xg[par][pl.ds(s, TM, stride=PSLAB), :] for s in range(PSLAB)]
        x = jnp.concatenate([_unpack_lo(w).astype(jnp.bfloat16) for w in words]
                            + [_unpack_hi(w).astype(jnp.bfloat16) for w in words], axis=1)
        gate = jnp.dot(x, wgb_ref[...], preferred_element_type=jnp.float32) + bg_ref[0]
        up = jnp.dot(x, wub_ref[...], preferred_element_type=jnp.float32) + bu_ref[0]
        gate = jnp.minimum(gate, SWIGLU_LIMIT)
        up = jnp.clip(up, -SWIGLU_LIMIT, SWIGLU_LIMIT)
        glu = gate * _sigmoid(SWIGLU_ALPHA * gate)
        act = (glu * (up + 1.0)).astype(jnp.bfloat16)
        y = jnp.dot(act, wdb_ref[...], preferred_element_type=jnp.float32) + bd_ref[0]
        packed = _pack_bf16_pairs(y)
        for s in range(PSLAB):
            ys[par][pl.ds(s, TM, stride=PSLAB), :] = packed[:, s * LANES:(s + 1) * LANES]

    def body(g, carry):
        for par in range(NBUF):
            pl.when(g % NBUF == par)(functools.partial(block_step, g, par))
        return carry

    lax.fori_loop(bs_ref[e], bs_ref[e + 1], body, 0)

    @pl.when(e == N_EXPERTS - 1)
    def _():
        g = n_total
        for par in range(NBUF):
            @pl.when((g - 1) % NBUF == par)
            def _():
                start_scatter(g - 1, par)
        for j in range(NBUF - 1):
            wait_gather((g + j) % NBUF)
        wait_scatter((g - 1) % NBUF)
        for j in range(2, NBUF + 1):
            @pl.when(g >= j - 1)
            def _():
                wait_scatter((g + NBUF - j) % NBUF)


def _experts(block_start, slot_buf, h2_slab, w_gate, b_gate, w_up, b_up, w_down, b_down, n_tok):
    TM = TM_EXPERT
    n_assign = n_tok * TOP_K
    w_spec = pl.BlockSpec((1, D_MODEL, D_FF), lambda e, bs, sl: (e, 0, 0))
    bias_spec = pl.BlockSpec((1, 1, D_FF), lambda e, bs, sl: (e, 0, 0))
    buf = pltpu.VMEM((TM * PSLAB, LANES), jnp.uint32)
    grid_spec = pltpu.PrefetchScalarGridSpec(
        num_scalar_prefetch=2,
        grid=(N_EXPERTS,),
        in_specs=[
            pl.BlockSpec(memory_space=pl.ANY),
            w_spec, bias_spec, w_spec, bias_spec, w_spec, bias_spec,
        ],
        out_specs=pl.BlockSpec(memory_space=pl.ANY),
        scratch_shapes=[
            *([buf] * (2 * NBUF)),
            pltpu.VMEM((D_MODEL, D_FF), jnp.bfloat16),
            pltpu.VMEM((D_MODEL, D_FF), jnp.bfloat16),
            pltpu.VMEM((D_FF, D_MODEL), jnp.bfloat16),
            pltpu.SemaphoreType.DMA((NBUF,)),
            pltpu.SemaphoreType.DMA((NBUF,)),
        ],
    )
    return pl.pallas_call(
        functools.partial(_expert_kernel, n_tok),
        grid_spec=grid_spec,
        out_shape=jax.ShapeDtypeStruct(((n_assign + NBUF * TM) * PSLAB, LANES), jnp.uint32),
        compiler_params=pltpu.CompilerParams(
            dimension_semantics=("arbitrary",), vmem_limit_bytes=VMEM_LIMIT),
        name="experts",
    )(block_start, slot_buf, h2_slab, w_gate, b_gate, w_up, b_up, w_down, b_down)


def _combine_kernel(normalize, x1_ref, y0_ref, y1_ref, y2_ref, y3_ref, gate_ref, g_ref, o_ref):
    TM = TM_COMBINE
    gates = jnp.concatenate([gate_ref[...], jnp.zeros((8 - TOP_K, TM), jnp.float32)], axis=0)
    g_cols = jnp.transpose(gates)
    g_bc = [jnp.broadcast_to(g_cols[:, k:k + 1], (TM, LANES)) for k in range(TOP_K)]
    ssq = jnp.zeros((TM, LANES), jnp.float32)
    parts = [x1_ref[:, s * LANES:(s + 1) * LANES] for s in range(SLAB)]
    for s in range(PSLAB):
        for k, y_ref in enumerate((y0_ref, y1_ref, y2_ref, y3_ref)):
            w = y_ref[pl.ds(s, TM, stride=PSLAB), :]
            parts[s] = parts[s] + g_bc[k] * _unpack_lo(w)
            parts[PSLAB + s] = parts[PSLAB + s] + g_bc[k] * _unpack_hi(w)
    for acc in parts:
        ssq = ssq + acc * acc
    if normalize:
        inv = lax.rsqrt(jnp.sum(ssq, axis=-1, keepdims=True) * (1.0 / D_MODEL) + EPS)
        for s in range(SLAB):
            o_ref[:, s * LANES:(s + 1) * LANES] = parts[s] * inv * g_ref[:, s * LANES:(s + 1) * LANES]
    else:
        for s in range(SLAB):
            o_ref[:, s * LANES:(s + 1) * LANES] = parts[s]


def _combine(x1, y_tok, gate_t, gf, normalize):
    T = x1.shape[0]
    TM = TM_COMBINE
    nt = T // TM

    def y_spec(k):
        return pl.BlockSpec((TM * PSLAB, LANES), lambda i: (k * nt + i, 0))

    return pl.pallas_call(
        functools.partial(_combine_kernel, normalize),
        grid=(nt,),
        in_specs=[
            pl.BlockSpec((TM, D_MODEL), lambda i: (i, 0)),
            y_spec(0), y_spec(1), y_spec(2), y_spec(3),
            pl.BlockSpec((TOP_K, TM), lambda i: (0, i)),
            pl.BlockSpec((1, D_MODEL), lambda i: (0, 0)),
        ],
        out_specs=pl.BlockSpec((TM, D_MODEL), lambda i: (i, 0)),
        out_shape=jax.ShapeDtypeStruct((T, D_MODEL), jnp.float32),
        compiler_params=pltpu.CompilerParams(
            dimension_semantics=("parallel",), vmem_limit_bytes=VMEM_LIMIT),
        name="combine",
    )(x1, y_tok, y_tok, y_tok, y_tok, gate_t, gf)


def kernel(x, norm1_g, w_in, ig_b, fg_b, conv_w, head_norm_g, pool_w, pool_scale, w_out, norm2_g,
           w_router, b_router, w_gate, b_gate, w_up, b_up, w_down, b_down, normf_g):
    B, S, D = x.shape
    T = B * S
    depth = norm1_g.shape[0]
    W = MLSTM_WIDTH
    f32, bf16 = jnp.float32, jnp.bfloat16

    L = CHUNK
    t_l = lax.broadcasted_iota(jnp.int32, (L, L), 0)
    t_r = lax.broadcasted_iota(jnp.int32, (L, L), 1)
    tri = (t_r <= t_l).astype(f32)
    shifts = jnp.stack([(t_l - t_r == CONV_WIDTH - 1 - j).astype(bf16)
                        for j in range(CONV_WIDTH - 1)])
    h_t = lax.broadcasted_iota(jnp.int32, (8, HALO), 0)
    h_r = lax.broadcasted_iota(jnp.int32, (8, HALO), 1)
    halo_shifts = jnp.stack([(h_r - HALO - h_t == -(CONV_WIDTH - 1 - j)).astype(bf16)
                             for j in range(CONV_WIDTH - 1)])

    n_assign = T * TOP_K
    n_blocks = -(-n_assign // TM_EXPERT) + N_EXPERTS
    n_rows = n_blocks * TM_EXPERT
    n_table = n_rows + NBUF * TM_EXPERT
    fill = n_assign + ((jnp.arange(n_table, dtype=jnp.int32) + (NBUF - 1) * TM_EXPERT)
                       % (NBUF * TM_EXPERT))
    x2 = x.reshape(T, D)
    for l in range(depth):
        w = w_in[l]
        w_a = w[:, :4 * W].astype(bf16)
        w_u = w[:, 4 * W + N_GATES:].astype(bf16)
        wg_t = jnp.zeros((BF16_SUBLANES, D), bf16).at[:N_GATES].set(
            w[:, 4 * W:4 * W + N_GATES].T.astype(bf16))
        p, gates_t = _in_proj(x2, norm1_g[l][None, :], w_a, w_u, wg_t)

        gate_b = jnp.concatenate([ig_b[l], fg_b[l]])[:, None].astype(f32)
        gates_b = gates_t.reshape(N_GATES, B, S).transpose(1, 0, 2)
        ym = _mlstm(p.reshape(B, S, N_MAIN), gates_b, conv_w[l].astype(f32), gate_b,
                    head_norm_g[l][None, :], tri, shifts, halo_shifts).reshape(T, W)

        x1, h2, idx_t, gate_t, rank_t, cnt = _out_route(
            x2, ym, p, pool_w[l].astype(bf16), pool_scale[l][None, :], w_out[l].astype(bf16),
            norm2_g[l][None, :], w_router[l].T.astype(bf16), b_router[l][:, None], S)

        counts = cnt[:, 0]
        padded = ((counts + TM_EXPERT - 1) // TM_EXPERT) * TM_EXPERT
        padded_end = jnp.cumsum(padded)
        padded_start = padded_end - padded
        expert_ids = jnp.arange(N_EXPERTS, dtype=jnp.int32)[:, None, None]
        start_of = jnp.sum(jnp.where(idx_t[None] == expert_ids, padded_start[:, None, None], 0), axis=0)
        dest = start_of + rank_t
        block_start = jnp.concatenate(
            [jnp.zeros((1,), jnp.int32), (padded_end // TM_EXPERT).astype(jnp.int32)])

        slot_buf = _plan(dest.reshape(-1) + TM_EXPERT, fill)
        y_tok = _experts(block_start, slot_buf, h2, w_gate[l], b_gate[l][:, None, :],
                         w_up[l], b_up[l][:, None, :], w_down[l], b_down[l][:, None, :], T)
        last = l + 1 == depth
        x2 = _combine(x1, y_tok, gate_t, normf_g[None, :], last)
    return x2.reshape(B, S, D)
```

```python
import functools

import jax
import jax.numpy as jnp
from jax import lax
from jax.experimental import pallas as pl
from jax.experimental.pallas import tpu as pltpu
from jax.experimental.pallas import tpu_sc as plsc

D_MODEL = 1024
MLSTM_WIDTH = 512
MLSTM_HEADS = 4
HEAD_DIM = 128
CONV_WIDTH = 4
POOL_WIDTH = 512
POOL_WINDOWS = (2, 4, 8, 16)
POOL_GROUP_DIM = 128
N_EXPERTS = 32
TOP_K = 4
D_FF = 1024
SWIGLU_LIMIT = 7.0
SWIGLU_ALPHA = 1.702
EPS = 1e-5

N_MAIN = 4 * MLSTM_WIDTH + POOL_WIDTH
N_GATES = 2 * MLSTM_HEADS

LANES = 128
BF16_SUBLANES = 16
VMEM_LIMIT = 56 * 1024 * 1024

TM_PROJ = 512
TM_COMBINE = 1024
ROUTE_SUB = 2
CHUNK = 256
MLSTM_BATCH = 4
HALO = 16
TM_EXPERT = 512
NBUF = 3
ROW_DMA_PRIORITY = 1
SLAB = D_MODEL // LANES
PSLAB = SLAB // 2
PLAN_CHUNK = 32768
SC_LANES = 16
PLAN_UNROLL = 8

NT_DIMS = (((1,), (1,)), ((), ()))


def _sigmoid(x):
    return 1.0 / (1.0 + jnp.exp(-x))


def _pack_bf16_pairs(v):
    half = v.shape[1] // 2
    lo = pltpu.bitcast(v[:, :half].astype(jnp.bfloat16).astype(jnp.float32), jnp.uint32)
    hi = pltpu.bitcast(v[:, half:].astype(jnp.bfloat16).astype(jnp.float32), jnp.uint32)
    return (lo >> 16) | (hi & jnp.uint32(0xFFFF0000))


def _unpack_lo(w):
    return pltpu.bitcast(w << 16, jnp.float32)


def _unpack_hi(w):
    return pltpu.bitcast(w & jnp.uint32(0xFFFF0000), jnp.float32)


def _in_proj_kernel(x_ref, g_ref, wa_ref, wu_ref, wgt_ref, p_ref, gt_ref):
    x = x_ref[...]
    h = x * lax.rsqrt(jnp.mean(x * x, axis=-1, keepdims=True) + EPS) * g_ref[...]
    hb = h.astype(jnp.bfloat16)
    n_a = wa_ref.shape[1]
    p_ref[:, :n_a] = jnp.dot(hb, wa_ref[...], preferred_element_type=jnp.float32).astype(p_ref.dtype)
    p_ref[:, n_a:] = jnp.dot(hb, wu_ref[...], preferred_element_type=jnp.float32).astype(p_ref.dtype)
    gt = lax.dot_general(wgt_ref[...], hb, NT_DIMS, preferred_element_type=jnp.float32)
    gt_ref[...] = gt[:N_GATES]


def _in_proj(x2, g1, w_a, w_u, wg_t):
    T = x2.shape[0]
    return pl.pallas_call(
        _in_proj_kernel,
        grid=(T // TM_PROJ,),
        in_specs=[
            pl.BlockSpec((TM_PROJ, D_MODEL), lambda i: (i, 0)),
            pl.BlockSpec((1, D_MODEL), lambda i: (0, 0)),
            pl.BlockSpec(w_a.shape, lambda i: (0, 0)),
            pl.BlockSpec(w_u.shape, lambda i: (0, 0)),
            pl.BlockSpec((BF16_SUBLANES, D_MODEL), lambda i: (0, 0)),
        ],
        out_specs=[
            pl.BlockSpec((TM_PROJ, N_MAIN), lambda i: (i, 0)),
            pl.BlockSpec((N_GATES, TM_PROJ), lambda i: (0, i)),
        ],
        out_shape=[
            jax.ShapeDtypeStruct((T, N_MAIN), jnp.bfloat16),
            jax.ShapeDtypeStruct((N_GATES, T), jnp.float32),
        ],
        compiler_params=pltpu.CompilerParams(
            dimension_semantics=("parallel",), vmem_limit_bytes=VMEM_LIMIT),
        name="in_proj",
    )(x2, g1, w_a, w_u, wg_t)


def _mlstm_kernel(qk_ref, qkp_ref, v_ref, o_ref, gt_ref, convw_ref, gb_ref, hng_ref,
                  tri_ref, shift_ref, hshift_ref, y_ref, cn_ref, m_ref):
    L = CHUNK
    c = pl.program_id(1)

    @pl.when(c == 0)
    def _():
        cn_ref[...] = jnp.zeros_like(cn_ref)
        m_ref[...] = jnp.zeros_like(m_ref)

    row_id = lax.broadcasted_iota(jnp.int32, (L, L), 0)
    col_id = lax.broadcasted_iota(jnp.int32, (L, L), 1)
    causal = col_id <= row_id
    ones_blk = jnp.ones((L, HEAD_DIM), jnp.bfloat16)
    lane = lax.broadcasted_iota(jnp.int32, (MLSTM_HEADS, L), 1)

    gate_terms = []
    for bb in range(MLSTM_BATCH):
        gt = gt_ref[bb] + gb_ref[...]
        f = gt[MLSTM_HEADS:]
        lf = jnp.minimum(f, 0.0) - jnp.log(1.0 + jnp.exp(-jnp.abs(f)))
        ig = gt[:MLSTM_HEADS]
        b_rows = lax.dot_general(lf, tri_ref[...], NT_DIMS, precision=lax.Precision.HIGHEST,
                                 preferred_element_type=jnp.float32)
        c_rows = ig - b_rows
        cm_rows = c_rows
        d = 1
        while d < L:
            cm_rows = jnp.maximum(
                cm_rows, jnp.where(lane >= d, pltpu.roll(cm_rows, d, axis=1), -jnp.inf))
            d *= 2
        gate_terms.append((b_rows, c_rows, cm_rows))

    conv_terms = []
    for bb in range(MLSTM_BATCH):
        x_cur = qk_ref[bb]
        x_prev = jnp.where(c > 0, qkp_ref[bb], jnp.zeros((HALO, 2 * MLSTM_WIDTH), jnp.bfloat16))
        acc = convw_ref[CONV_WIDTH - 1:CONV_WIDTH, :] * x_cur.astype(jnp.float32)
        for j in range(CONV_WIDTH - 1):
            sh = jnp.dot(shift_ref[j], x_cur, preferred_element_type=jnp.float32)
            top = sh[:8] + jnp.dot(hshift_ref[j], x_prev, preferred_element_type=jnp.float32)
            sh = jnp.concatenate([top, sh[8:]], axis=0)
            acc = acc + convw_ref[j:j + 1, :] * sh
        qk = acc * _sigmoid(acc)
        q_all = qk[:, :MLSTM_WIDTH].astype(jnp.bfloat16)
        k_t = jnp.transpose(qk[:, MLSTM_WIDTH:] * (HEAD_DIM ** -0.5))
        conv_terms.append((q_all, k_t))

    for bb in range(MLSTM_BATCH):
        b_rows, c_rows, cm_rows = gate_terms[bb]
        q_all, k_t = conv_terms[bb]
        m_in4 = jnp.concatenate(
            [m_ref[bb * MLSTM_HEADS + h][0:1, 0:1] for h in range(MLSTM_HEADS)], axis=0)
        mx_rows = jnp.maximum(cm_rows, m_in4)
        inter_rows = jnp.exp(m_in4 - mx_rows)
        einv_rows = jnp.exp(-(b_rows + mx_rows))
        fac_t = jnp.transpose(jnp.concatenate(
            [mx_rows, inter_rows, einv_rows, jnp.zeros_like(mx_rows)], axis=0))

        for h in range(MLSTM_HEADS):
            lo = h * HEAD_DIM
            st = bb * MLSTM_HEADS + h
            q = q_all[:, lo:lo + HEAD_DIM]
            kt = k_t[lo:lo + HEAD_DIM, :]
            v_ext = jnp.concatenate([v_ref[bb, :, lo:lo + HEAD_DIM], ones_blk], axis=1)
            mx_col = fac_t[:, h:h + 1]
            inter_col = fac_t[:, MLSTM_HEADS + h:MLSTM_HEADS + h + 1]
            einv_col = fac_t[:, 2 * MLSTM_HEADS + h:2 * MLSTM_HEADS + h + 1]
            c_row = c_rows[h:h + 1, :]
            b_tot = b_rows[h:h + 1, L - 1:L]
            cm_tot = cm_rows[h:h + 1, L - 1:L]
            m_in = m_ref[st][0:1, 0:1]
            cn = cn_ref[st]

            s_qk = jnp.dot(q, kt.astype(jnp.bfloat16), preferred_element_type=jnp.float32)
            s = (s_qk * jnp.exp(jnp.where(causal, c_row - mx_col, -jnp.inf))).astype(jnp.bfloat16)
            num = (jnp.dot(s, v_ext, preferred_element_type=jnp.float32)
                   + inter_col * jnp.dot(q, cn.astype(jnp.bfloat16),
                                         preferred_element_type=jnp.float32))
            den = num[:, HEAD_DIM:]
            hh = num[:, :HEAD_DIM] / jnp.maximum(jnp.abs(den), einv_col)

            mu = jnp.mean(hh, axis=-1, keepdims=True)
            dv = hh - mu
            var = jnp.mean(dv * dv, axis=-1, keepdims=True)
            hn = dv * lax.rsqrt(var + EPS) * hng_ref[:, lo:lo + HEAD_DIM]
            og = _sigmoid(o_ref[bb, :, lo:lo + HEAD_DIM].astype(jnp.float32))
            y_ref[bb, :, lo:lo + HEAD_DIM] = (og * hn).astype(y_ref.dtype)

            m_loc = b_tot + cm_tot
            kw_t = (kt * jnp.exp(c_row - cm_tot)).astype(jnp.bfloat16)
            c_loc = jnp.dot(kw_t, v_ext, preferred_element_type=jnp.float32)
            m_new = jnp.maximum(b_tot + m_in, m_loc)
            s_old = jnp.exp(b_tot + m_in - m_new)
            s_loc = jnp.exp(m_loc - m_new)
            cn_ref[st] = s_old * cn + s_loc * c_loc
            m_ref[st] = jnp.broadcast_to(m_new, m_ref.shape[1:])


def _mlstm(p3, gates_b, conv_w, gate_b, hn_g, tri, shifts, halo_shifts):
    batch, seq, _ = p3.shape
    L = CHUNK
    BB = MLSTM_BATCH
    halo_per_chunk = L // HALO
    return pl.pallas_call(
        _mlstm_kernel,
        grid=(batch // BB, seq // L),
        in_specs=[
            pl.BlockSpec((BB, L, 2 * MLSTM_WIDTH), lambda bi, ci: (bi, ci, 0)),
            pl.BlockSpec((BB, HALO, 2 * MLSTM_WIDTH),
                         lambda bi, ci: (bi, jnp.maximum(ci * halo_per_chunk - 1, 0), 0)),
            pl.BlockSpec((BB, L, MLSTM_WIDTH), lambda bi, ci: (bi, ci, 2)),
            pl.BlockSpec((BB, L, MLSTM_WIDTH), lambda bi, ci: (bi, ci, 3)),
            pl.BlockSpec((BB, N_GATES, L), lambda bi, ci: (bi, 0, ci)),
            pl.BlockSpec((CONV_WIDTH, 2 * MLSTM_WIDTH), lambda bi, ci: (0, 0)),
            pl.BlockSpec((N_GATES, 1), lambda bi, ci: (0, 0)),
            pl.BlockSpec((1, MLSTM_WIDTH), lambda bi, ci: (0, 0)),
            pl.BlockSpec((L, L), lambda bi, ci: (0, 0)),
            pl.BlockSpec((CONV_WIDTH - 1, L, L), lambda bi, ci: (0, 0, 0)),
            pl.BlockSpec((CONV_WIDTH - 1, 8, HALO), lambda bi, ci: (0, 0, 0)),
        ],
        out_specs=pl.BlockSpec((BB, L, MLSTM_WIDTH), lambda bi, ci: (bi, ci, 0)),
        out_shape=jax.ShapeDtypeStruct((batch, seq, MLSTM_WIDTH), jnp.bfloat16),
        scratch_shapes=[
            pltpu.VMEM((BB * MLSTM_HEADS, HEAD_DIM, 2 * HEAD_DIM), jnp.float32),
            pltpu.VMEM((BB * MLSTM_HEADS, 8, LANES), jnp.float32),
        ],
        compiler_params=pltpu.CompilerParams(
            dimension_semantics=("parallel", "arbitrary"), vmem_limit_bytes=VMEM_LIMIT),
        name="mlstm",
    )(p3, p3, p3, p3, gates_b, conv_w, gate_b, hn_g, tri, shifts, halo_shifts)


def _out_route_kernel(seq, x_ref, ym_ref, u_ref, up_ref, pw_ref, ps_ref, wo_ref, g2_ref,
                      wrt_ref, br_ref, x1_ref, h2_ref, idx_ref, gate_ref, rank_ref, cnt_ref,
                      carry_ref):
    TM = TM_PROJ
    R = ROUTE_SUB * TM
    i = pl.program_id(0)

    @pl.when(i == 0)
    def _():
        carry_ref[...] = jnp.zeros_like(carry_ref)

    pos0 = (i * R) % seq
    e_id = lax.broadcasted_iota(jnp.int32, (N_EXPERTS, TM), 0).astype(jnp.float32)
    t_row = lax.broadcasted_iota(jnp.int32, (TM, TM), 0)
    t_col = lax.broadcasted_iota(jnp.int32, (TM, TM), 1)
    before = jnp.where(t_row < t_col, 1.0, 0.0).astype(jnp.bfloat16)
    carry = carry_ref[...]
    subs = [slice(sub * TM, (sub + 1) * TM) for sub in range(ROUTE_SUB)]

    halo = jnp.where(pos0 > 0, up_ref[...].astype(jnp.float32), 0.0)
    u_ext = jnp.concatenate([halo, u_ref[...].astype(jnp.float32)], axis=0)
    win_sums = []
    for gi, w in enumerate(POOL_WINDOWS):
        sw = u_ext[:, gi * POOL_GROUP_DIM:(gi + 1) * POOL_GROUP_DIM]
        span = 1
        while span < w:
            sw = sw + pltpu.roll(sw, span, axis=0)
            span *= 2
        win_sums.append(sw)
    y_cats = []
    for sub, rows in enumerate(subs):
        r0 = sub * TM
        pos = (pos0 + r0 + lax.broadcasted_iota(jnp.int32, (TM, 1), 0) + 1).astype(jnp.float32)
        mixed = []
        for gi, w in enumerate(POOL_WINDOWS):
            lo = gi * POOL_GROUP_DIM
            tok = u_ext[HALO + r0:HALO + r0 + TM, lo:lo + POOL_GROUP_DIM]
            pooled = win_sums[gi][HALO + r0:HALO + r0 + TM] / jnp.minimum(pos, float(w)) - tok
            mg = jnp.dot(pooled.astype(jnp.bfloat16), pw_ref[gi],
                         preferred_element_type=jnp.float32)
            mixed.append((mg * ps_ref[:, lo:lo + POOL_GROUP_DIM]).astype(jnp.bfloat16))
        y_cats.append(jnp.concatenate([ym_ref[rows, :]] + mixed, axis=1))

    all_logits = []
    for sub, rows in enumerate(subs):
        r0 = sub * TM
        x1 = x_ref[rows, :] + jnp.dot(y_cats[sub], wo_ref[...], preferred_element_type=jnp.float32)
        x1_ref[rows, :] = x1
        h2 = x1 * lax.rsqrt(jnp.mean(x1 * x1, axis=-1, keepdims=True) + EPS) * g2_ref[...]
        h2b = h2.astype(jnp.bfloat16)
        h2w = _pack_bf16_pairs(h2)
        for s in range(PSLAB):
            h2_ref[pl.ds(r0 * PSLAB + s, TM, stride=PSLAB), :] = h2w[:, s * LANES:(s + 1) * LANES]
        all_logits.append(lax.dot_general(wrt_ref[...], h2b, NT_DIMS,
                                          preferred_element_type=jnp.float32) + br_ref[...])

    for sub, rows in enumerate(subs):
        work = all_logits[sub]
        vals, ids, hots = [], [], []
        for _ in range(TOP_K):
            mk = jnp.max(work, axis=0, keepdims=True)
            ik = jnp.min(jnp.where(work == mk, e_id, float(N_EXPERTS)), axis=0, keepdims=True)
            hot = e_id == ik
            work = jnp.where(hot, -jnp.inf, work)
            vals.append(mk)
            ids.append(ik)
            hots.append(hot)
        ex = [jnp.exp(vk - vals[0]) for vk in vals]
        denom = ex[0] + ex[1] + ex[2] + ex[3]
        gate_ref[:, rows] = jnp.concatenate([e / denom for e in ex], axis=0)
        idx_ref[:, rows] = jnp.concatenate(ids, axis=0).astype(jnp.int32)

        sel_f = sum(jnp.where(hot, 1.0, 0.0) for hot in hots)
        prefix = jnp.dot(sel_f.astype(jnp.bfloat16), before, preferred_element_type=jnp.float32)
        rank_e = carry[:, 0:1] + prefix
        ranks = [jnp.sum(jnp.where(hot, rank_e, 0.0), axis=0, keepdims=True) for hot in hots]
        rank_ref[:, rows] = jnp.concatenate(ranks, axis=0).astype(jnp.int32)
        carry = carry + jnp.sum(sel_f, axis=1, keepdims=True)
    carry_ref[...] = carry
    cnt_ref[...] = carry.astype(jnp.int32)


def _out_route(x2, ym, p, pool_w, pool_s, w_out, g2, wr_t, br, seq):
    T = x2.shape[0]
    TM = ROUTE_SUB * TM_PROJ
    nt = T // TM
    u_blk = N_MAIN // POOL_WIDTH - 1
    halo_per_tile = TM // HALO
    tok_spec = pl.BlockSpec((TOP_K, TM), lambda i: (0, i))
    return pl.pallas_call(
        functools.partial(_out_route_kernel, seq),
        grid=(nt,),
        in_specs=[
            pl.BlockSpec((TM, D_MODEL), lambda i: (i, 0)),
            pl.BlockSpec((TM, MLSTM_WIDTH), lambda i: (i, 0)),
            pl.BlockSpec((TM, POOL_WIDTH), lambda i: (i, u_blk)),
            pl.BlockSpec((HALO, POOL_WIDTH),
                         lambda i: (jnp.maximum(i * halo_per_tile - 1, 0), u_blk)),
            pl.BlockSpec((len(POOL_WINDOWS), POOL_GROUP_DIM, POOL_GROUP_DIM), lambda i: (0, 0, 0)),
            pl.BlockSpec((1, POOL_WIDTH), lambda i: (0, 0)),
            pl.BlockSpec((D_MODEL, D_MODEL), lambda i: (0, 0)),
            pl.BlockSpec((1, D_MODEL), lambda i: (0, 0)),
            pl.BlockSpec((N_EXPERTS, D_MODEL), lambda i: (0, 0)),
            pl.BlockSpec((N_EXPERTS, 1), lambda i: (0, 0)),
        ],
        out_specs=[
            pl.BlockSpec((TM, D_MODEL), lambda i: (i, 0)),
            pl.BlockSpec((TM * PSLAB, LANES), lambda i: (i, 0)),
            tok_spec, tok_spec, tok_spec,
            pl.BlockSpec((N_EXPERTS, LANES), lambda i: (0, 0)),
        ],
        out_shape=[
            jax.ShapeDtypeStruct((T, D_MODEL), jnp.float32),
            jax.ShapeDtypeStruct((T * PSLAB, LANES), jnp.uint32),
            jax.ShapeDtypeStruct((TOP_K, T), jnp.int32),
            jax.ShapeDtypeStruct((TOP_K, T), jnp.float32),
            jax.ShapeDtypeStruct((TOP_K, T), jnp.int32),
            jax.ShapeDtypeStruct((N_EXPERTS, LANES), jnp.int32),
        ],
        scratch_shapes=[
            pltpu.VMEM((N_EXPERTS, LANES), jnp.float32),
        ],
        compiler_params=pltpu.CompilerParams(
            dimension_semantics=("arbitrary",), vmem_limit_bytes=VMEM_LIMIT),
        name="out_route",
    )(x2, ym, p, p, pool_w, pool_s, w_out, g2, wr_t, br)


def _plan(dest_flat, fill):
    n_assign = dest_flat.shape[0]
    n_table = fill.shape[0]
    mesh = plsc.VectorSubcoreMesh(core_axis_name="c", subcore_axis_name="s")

    @pl.kernel(out_type=jax.ShapeDtypeStruct((n_table,), jnp.int32), mesh=mesh,
               scratch_types=[pltpu.VMEM((n_table,), jnp.int32),
                              pltpu.VMEM((PLAN_CHUNK,), jnp.int32)],
               compiler_params=pltpu.CompilerParams(needs_layout_passes=False))
    def plan_kernel(dest_hbm, fill_hbm, out_hbm, table, chunk):
        first = jnp.logical_and(lax.axis_index("c") == 0, lax.axis_index("s") == 0)

        @pl.when(first)
        def _():
            pltpu.sync_copy(fill_hbm, table)

            @pl.loop(0, n_assign // PLAN_CHUNK)
            def _(ci):
                pltpu.sync_copy(dest_hbm.at[pl.ds(ci * PLAN_CHUNK, PLAN_CHUNK)], chunk)

                @pl.loop(0, PLAN_CHUNK // (SC_LANES * PLAN_UNROLL))
                def _(i):
                    for j in range(PLAN_UNROLL):
                        off = (i * PLAN_UNROLL + j) * SC_LANES
                        idx = chunk[pl.ds(off, SC_LANES)]
                        vals = (ci * PLAN_CHUNK + off
                                + lax.broadcasted_iota(jnp.int32, (SC_LANES,), 0))
                        plsc.store_scatter(table, [idx], vals)

            pltpu.sync_copy(table, out_hbm)

    return plan_kernel(dest_flat, fill)


def _expert_kernel(n_tok, bs_ref, slot_ref, h2_ref, wg_ref, bg_ref, wu_ref, bu_ref, wd_ref, bd_ref,
                   yt_ref, *scratch):
    TM = TM_EXPERT
    ROWS = TM * PSLAB
    e = pl.program_id(0)
    n_total = bs_ref[N_EXPERTS]
    xg = scratch[:NBUF]
    ys = scratch[NBUF:2 * NBUF]
    wgb_ref, wub_ref, wdb_ref, gsem, ssem = scratch[2 * NBUF:]

    def token_of(a):
        return a & (n_tok - 1) if n_tok & (n_tok - 1) == 0 else lax.rem(a, n_tok)

    def start_gather(blk, par):
        base = (blk + 1) * TM
        for r in range(TM):
            t = token_of(slot_ref[base + r])
            pltpu.make_async_copy(h2_ref.at[pl.ds(pl.multiple_of(t * PSLAB, PSLAB), PSLAB), :],
                                  xg[par].at[pl.ds(r * PSLAB, PSLAB), :], gsem.at[par]
                                  ).start(priority=ROW_DMA_PRIORITY)

    def wait_gather(par):
        pltpu.make_async_copy(h2_ref.at[pl.ds(0, ROWS), :], xg[0], gsem.at[par]).wait()

    def start_scatter(blk, par):
        base = (blk + 1) * TM
        for r in range(TM):
            a = slot_ref[base + r]
            pltpu.make_async_copy(ys[par].at[pl.ds(r * PSLAB, PSLAB), :],
                                  yt_ref.at[pl.ds(pl.multiple_of(a * PSLAB, PSLAB), PSLAB), :],
                                  ssem.at[par]).start(priority=ROW_DMA_PRIORITY)

    def wait_scatter(par):
        pltpu.make_async_copy(ys[0], yt_ref.at[pl.ds(0, ROWS), :], ssem.at[par]).wait()

    @pl.when(e == 0)
    def _():
        for blk in range(NBUF - 1):
            start_gather(blk, blk)
        for par in range(NBUF):
            ys[par][...] = jnp.zeros_like(ys[par])
            dump = yt_ref.at[pl.ds((n_tok * TOP_K + par * TM) * PSLAB, ROWS), :]
            cp = pltpu.make_async_copy(ys[par], dump, ssem.at[par])
            cp.start()
            cp.wait()

    wgb_ref[...] = wg_ref[0].astype(jnp.bfloat16)
    wub_ref[...] = wu_ref[0].astype(jnp.bfloat16)
    wdb_ref[...] = wd_ref[0].astype(jnp.bfloat16)

    def block_step(g, par):
        prv = (par + NBUF - 1) % NBUF
        wait_gather(par)

        @pl.when(g >= NBUF - 1)
        def _():
            wait_scatter(par)

        start_gather(g + NBUF - 1, prv)
        start_scatter(g - 1, prv)
        words = [xg[par][pl.ds(s, TM, stride=PSLAB), :] for s in range(PSLAB)]
        x = jnp.concatenate([_unpack_lo(w).astype(jnp.bfloat16) for w in words]
                            + [_unpack_hi(w).astype(jnp.bfloat16) for w in words], axis=1)
        gate = jnp.dot(x, wgb_ref[...], preferred_element_type=jnp.float32) + bg_ref[0]
        up = jnp.dot(x, wub_ref[...], preferred_element_type=jnp.float32) + bu_ref[0]
        gate = jnp.minimum(gate, SWIGLU_LIMIT)
        up = jnp.clip(up, -SWIGLU_LIMIT, SWIGLU_LIMIT)
        glu = gate * _sigmoid(SWIGLU_ALPHA * gate)
        act = (glu * (up + 1.0)).astype(jnp.bfloat16)
        y = jnp.dot(act, wdb_ref[...], preferred_element_type=jnp.float32) + bd_ref[0]
        packed = _pack_bf16_pairs(y)
        for s in range(PSLAB):
            ys[par][pl.ds(s, TM, stride=PSLAB), :] = packed[:, s * LANES:(s + 1) * LANES]

    def body(g, carry):
        for par in range(NBUF):
            pl.when(g % NBUF == par)(functools.partial(block_step, g, par))
        return carry

    lax.fori_loop(bs_ref[e], bs_ref[e + 1], body, 0)

    @pl.when(e == N_EXPERTS - 1)
    def _():
        g = n_total
        for par in range(NBUF):
            @pl.when((g - 1) % NBUF == par)
            def _():
                start_scatter(g - 1, par)
        for j in range(NBUF - 1):
            wait_gather((g + j) % NBUF)
        wait_scatter((g - 1) % NBUF)
        for j in range(2, NBUF + 1):
            @pl.when(g >= j - 1)
            def _():
                wait_scatter((g + NBUF - j) % NBUF)


def _experts(block_start, slot_buf, h2_slab, w_gate, b_gate, w_up, b_up, w_down, b_down, n_tok):
    TM = TM_EXPERT
    n_assign = n_tok * TOP_K
    w_spec = pl.BlockSpec((1, D_MODEL, D_FF), lambda e, bs, sl: (e, 0, 0))
    bias_spec = pl.BlockSpec((1, 1, D_FF), lambda e, bs, sl: (e, 0, 0))
    buf = pltpu.VMEM((TM * PSLAB, LANES), jnp.uint32)
    grid_spec = pltpu.PrefetchScalarGridSpec(
        num_scalar_prefetch=2,
        grid=(N_EXPERTS,),
        in_specs=[
            pl.BlockSpec(memory_space=pl.ANY),
            w_spec, bias_spec, w_spec, bias_spec, w_spec, bias_spec,
        ],
        out_specs=pl.BlockSpec(memory_space=pl.ANY),
        scratch_shapes=[
            *([buf] * (2 * NBUF)),
            pltpu.VMEM((D_MODEL, D_FF), jnp.bfloat16),
            pltpu.VMEM((D_MODEL, D_FF), jnp.bfloat16),
            pltpu.VMEM((D_FF, D_MODEL), jnp.bfloat16),
            pltpu.SemaphoreType.DMA((NBUF,)),
            pltpu.SemaphoreType.DMA((NBUF,)),
        ],
    )
    return pl.pallas_call(
        functools.partial(_expert_kernel, n_tok),
        grid_spec=grid_spec,
        out_shape=jax.ShapeDtypeStruct(((n_assign + NBUF * TM) * PSLAB, LANES), jnp.uint32),
        compiler_params=pltpu.CompilerParams(
            dimension_semantics=("arbitrary",), vmem_limit_bytes=VMEM_LIMIT),
        name="experts",
    )(block_start, slot_buf, h2_slab, w_gate, b_gate, w_up, b_up, w_down, b_down)


def _combine_kernel(normalize, x1_ref, y0_ref, y1_ref, y2_ref, y3_ref, gate_ref, g_ref, o_ref):
    TM = TM_COMBINE
    gates = jnp.concatenate([gate_ref[...], jnp.zeros((8 - TOP_K, TM), jnp.float32)], axis=0)
    g_cols = jnp.transpose(gates)
    g_bc = [jnp.broadcast_to(g_cols[:, k:k + 1], (TM, LANES)) for k in range(TOP_K)]
    ssq = jnp.zeros((TM, LANES), jnp.float32)
    parts = [x1_ref[:, s * LANES:(s + 1) * LANES] for s in range(SLAB)]
    for s in range(PSLAB):
        for k, y_ref in enumerate((y0_ref, y1_ref, y2_ref, y3_ref)):
            w = y_ref[pl.ds(s, TM, stride=PSLAB), :]
            parts[s] = parts[s] + g_bc[k] * _unpack_lo(w)
            parts[PSLAB + s] = parts[PSLAB + s] + g_bc[k] * _unpack_hi(w)
    for acc in parts:
        ssq = ssq + acc * acc
    if normalize:
        inv = lax.rsqrt(jnp.sum(ssq, axis=-1, keepdims=True) * (1.0 / D_MODEL) + EPS)
        for s in range(SLAB):
            o_ref[:, s * LANES:(s + 1) * LANES] = parts[s] * inv * g_ref[:, s * LANES:(s + 1) * LANES]
    else:
        for s in range(SLAB):
            o_ref[:, s * LANES:(s + 1) * LANES] = parts[s]


def _combine(x1, y_tok, gate_t, gf, normalize):
    T = x1.shape[0]
    TM = TM_COMBINE
    nt = T // TM

    def y_spec(k):
        return pl.BlockSpec((TM * PSLAB, LANES), lambda i: (k * nt + i, 0))

    return pl.pallas_call(
        functools.partial(_combine_kernel, normalize),
        grid=(nt,),
        in_specs=[
            pl.BlockSpec((TM, D_MODEL), lambda i: (i, 0)),
            y_spec(0), y_spec(1), y_spec(2), y_spec(3),
            pl.BlockSpec((TOP_K, TM), lambda i: (0, i)),
            pl.BlockSpec((1, D_MODEL), lambda i: (0, 0)),
        ],
        out_specs=pl.BlockSpec((TM, D_MODEL), lambda i: (i, 0)),
        out_shape=jax.ShapeDtypeStruct((T, D_MODEL), jnp.float32),
        compiler_params=pltpu.CompilerParams(
            dimension_semantics=("parallel",), vmem_limit_bytes=VMEM_LIMIT),
        name="combine",
    )(x1, y_tok, y_tok, y_tok, y_tok, gate_t, gf)


def kernel(x, norm1_g, w_in, ig_b, fg_b, conv_w, head_norm_g, pool_w, pool_scale, w_out, norm2_g,
           w_router, b_router, w_gate, b_gate, w_up, b_up, w_down, b_down, normf_g):
    B, S, D = x.shape
    T = B * S
    depth = norm1_g.shape[0]
    W = MLSTM_WIDTH
    f32, bf16 = jnp.float32, jnp.bfloat16

    L = CHUNK
    t_l = lax.broadcasted_iota(jnp.int32, (L, L), 0)
    t_r = lax.broadcasted_iota(jnp.int32, (L, L), 1)
    tri = (t_r <= t_l).astype(f32)
    shifts = jnp.stack([(t_l - t_r == CONV_WIDTH - 1 - j).astype(bf16)
                        for j in range(CONV_WIDTH - 1)])
    h_t = lax.broadcasted_iota(jnp.int32, (8, HALO), 0)
    h_r = lax.broadcasted_iota(jnp.int32, (8, HALO), 1)
    halo_shifts = jnp.stack([(h_r - HALO - h_t == -(CONV_WIDTH - 1 - j)).astype(bf16)
                             for j in range(CONV_WIDTH - 1)])

    n_assign = T * TOP_K
    n_blocks = -(-n_assign // TM_EXPERT) + N_EXPERTS
    n_rows = n_blocks * TM_EXPERT
    n_table = n_rows + NBUF * TM_EXPERT
    fill = n_assign + ((jnp.arange(n_table, dtype=jnp.int32) + (NBUF - 1) * TM_EXPERT)
                       % (NBUF * TM_EXPERT))
    x2 = x.reshape(T, D)
    for l in range(depth):
        w = w_in[l]
        w_a = w[:, :4 * W].astype(bf16)
        w_u = w[:, 4 * W + N_GATES:].astype(bf16)
        wg_t = jnp.zeros((BF16_SUBLANES, D), bf16).at[:N_GATES].set(
            w[:, 4 * W:4 * W + N_GATES].T.astype(bf16))
        p, gates_t = _in_proj(x2, norm1_g[l][None, :], w_a, w_u, wg_t)

        gate_b = jnp.concatenate([ig_b[l], fg_b[l]])[:, None].astype(f32)
        gates_b = gates_t.reshape(N_GATES, B, S).transpose(1, 0, 2)
        ym = _mlstm(p.reshape(B, S, N_MAIN), gates_b, conv_w[l].astype(f32), gate_b,
                    head_norm_g[l][None, :], tri, shifts, halo_shifts).reshape(T, W)

        x1, h2, idx_t, gate_t, rank_t, cnt = _out_route(
            x2, ym, p, pool_w[l].astype(bf16), pool_scale[l][None, :], w_out[l].astype(bf16),
            norm2_g[l][None, :], w_router[l].T.astype(bf16), b_router[l][:, None], S)

        counts = cnt[:, 0]
        padded = ((counts + TM_EXPERT - 1) // TM_EXPERT) * TM_EXPERT
        padded_end = jnp.cumsum(padded)
        padded_start = padded_end - padded
        expert_ids = jnp.arange(N_EXPERTS, dtype=jnp.int32)[:, None, None]
        start_of = jnp.sum(jnp.where(idx_t[None] == expert_ids, padded_start[:, None, None], 0), axis=0)
        dest = start_of + rank_t
        block_start = jnp.concatenate(
            [jnp.zeros((1,), jnp.int32), (padded_end // TM_EXPERT).astype(jnp.int32)])

        slot_buf = _plan(dest.reshape(-1) + TM_EXPERT, fill)
        y_tok = _experts(block_start, slot_buf, h2, w_gate[l], b_gate[l][:, None, :],
                         w_up[l], b_up[l][:, None, :], w_down[l], b_down[l][:, None, :], T)
        last = l + 1 == depth
        x2 = _combine(x1, y_tok, gate_t, normf_g[None, :], last)
    return x2.reshape(B, S, D)
```

```python
import functools

import jax
import jax.numpy as jnp
from jax import lax
from jax.experimental import pallas as pl
from jax.experimental.pallas import tpu as pltpu
from jax.experimental.pallas import tpu_sc as plsc

D_MODEL = 1024
MLSTM_WIDTH = 512
MLSTM_HEADS = 4
HEAD_DIM = 128
CONV_WIDTH = 4
POOL_WIDTH = 512
POOL_WINDOWS = (2, 4, 8, 16)
POOL_GROUP_DIM = 128
N_EXPERTS = 32
TOP_K = 4
D_FF = 1024
SWIGLU_LIMIT = 7.0
SWIGLU_ALPHA = 1.702
EPS = 1e-5

N_MAIN = 4 * MLSTM_WIDTH + POOL_WIDTH
N_GATES = 2 * MLSTM_HEADS

LANES = 128
BF16_SUBLANES = 16
VMEM_LIMIT = 56 * 1024 * 1024

TM_PROJ = 512
PROJ_SUB = 2
TM_COMBINE = 1024
ROUTE_SUB = 2
CHUNK = 256
MLSTM_BATCH = 4
HALO = 16
TM_EXPERT = 512
NBUF = 3
SLAB = D_MODEL // LANES
PSLAB = SLAB // 2
PLAN_CHUNK = 32768
SC_LANES = 16
PLAN_UNROLL = 8

NT_DIMS = (((1,), (1,)), ((), ()))


def _sigmoid(x):
    return 1.0 / (1.0 + jnp.exp(-x))


def _pack_bf16_pairs(v):
    half = v.shape[1] // 2
    lo = pltpu.bitcast(v[:, :half].astype(jnp.bfloat16).astype(jnp.float32), jnp.uint32)
    hi = pltpu.bitcast(v[:, half:].astype(jnp.bfloat16).astype(jnp.float32), jnp.uint32)
    return (lo >> 16) | (hi & jnp.uint32(0xFFFF0000))


def _unpack_lo(w):
    return pltpu.bitcast(w << 16, jnp.float32)


def _unpack_hi(w):
    return pltpu.bitcast(w & jnp.uint32(0xFFFF0000), jnp.float32)


def _in_proj_kernel(x_ref, g_ref, wa_ref, wu_ref, wgt_ref, p_ref, gt_ref):
    n_a = wa_ref.shape[1]
    for sub in range(PROJ_SUB):
        rows = slice(sub * TM_PROJ, (sub + 1) * TM_PROJ)
        x = x_ref[rows, :]
        h = x * lax.rsqrt(jnp.mean(x * x, axis=-1, keepdims=True) + EPS) * g_ref[...]
        hb = h.astype(jnp.bfloat16)
        p_ref[rows, :n_a] = jnp.dot(hb, wa_ref[...],
                                    preferred_element_type=jnp.float32).astype(p_ref.dtype)
        p_ref[rows, n_a:] = jnp.dot(hb, wu_ref[...],
                                    preferred_element_type=jnp.float32).astype(p_ref.dtype)
        gt = lax.dot_general(wgt_ref[...], hb, NT_DIMS, preferred_element_type=jnp.float32)
        gt_ref[:, rows] = gt[:N_GATES]


def _in_proj(x2, g1, w_a, w_u, wg_t):
    T = x2.shape[0]
    return pl.pallas_call(
        _in_proj_kernel,
        grid=(T // (PROJ_SUB * TM_PROJ),),
        in_specs=[
            pl.BlockSpec((PROJ_SUB * TM_PROJ, D_MODEL), lambda i: (i, 0)),
            pl.BlockSpec((1, D_MODEL), lambda i: (0, 0)),
            pl.BlockSpec(w_a.shape, lambda i: (0, 0)),
            pl.BlockSpec(w_u.shape, lambda i: (0, 0)),
            pl.BlockSpec((BF16_SUBLANES, D_MODEL), lambda i: (0, 0)),
        ],
        out_specs=[
            pl.BlockSpec((PROJ_SUB * TM_PROJ, N_MAIN), lambda i: (i, 0)),
            pl.BlockSpec((N_GATES, PROJ_SUB * TM_PROJ), lambda i: (0, i)),
        ],
        out_shape=[
            jax.ShapeDtypeStruct((T, N_MAIN), jnp.bfloat16),
            jax.ShapeDtypeStruct((N_GATES, T), jnp.float32),
        ],
        compiler_params=pltpu.CompilerParams(
            dimension_semantics=("parallel",), vmem_limit_bytes=VMEM_LIMIT),
        name="in_proj",
    )(x2, g1, w_a, w_u, wg_t)


def _mlstm_kernel(qk_ref, qkp_ref, v_ref, o_ref, gt_ref, convw_ref, gb_ref, hng_ref,
                  tri_ref, shift_ref, hshift_ref, y_ref, cn_ref, m_ref):
    L = CHUNK
    c = pl.program_id(1)

    @pl.when(c == 0)
    def _():
        cn_ref[...] = jnp.zeros_like(cn_ref)
        m_ref[...] = jnp.zeros_like(m_ref)

    row_id = lax.broadcasted_iota(jnp.int32, (L, L), 0)
    col_id = lax.broadcasted_iota(jnp.int32, (L, L), 1)
    causal = col_id <= row_id
    ones_blk = jnp.ones((L, HEAD_DIM), jnp.bfloat16)
    lane = lax.broadcasted_iota(jnp.int32, (MLSTM_HEADS, L), 1)

    gate_terms = []
    for bb in range(MLSTM_BATCH):
        gt = gt_ref[bb] + gb_ref[...]
        f = gt[MLSTM_HEADS:]
        lf = jnp.minimum(f, 0.0) - jnp.log(1.0 + jnp.exp(-jnp.abs(f)))
        ig = gt[:MLSTM_HEADS]
        b_rows = lax.dot_general(lf, tri_ref[...], NT_DIMS, precision=lax.Precision.HIGHEST,
                                 preferred_element_type=jnp.float32)
        c_rows = ig - b_rows
        cm_rows = c_rows
        d = 1
        while d < L:
            cm_rows = jnp.maximum(
                cm_rows, jnp.where(lane >= d, pltpu.roll(cm_rows, d, axis=1), -jnp.inf))
            d *= 2
        gate_terms.append((b_rows, c_rows, cm_rows))

    conv_terms = []
    for bb in range(MLSTM_BATCH):
        x_cur = qk_ref[bb]
        x_prev = jnp.where(c > 0, qkp_ref[bb], jnp.zeros((HALO, 2 * MLSTM_WIDTH), jnp.bfloat16))
        acc = convw_ref[CONV_WIDTH - 1:CONV_WIDTH, :] * x_cur.astype(jnp.float32)
        for j in range(CONV_WIDTH - 1):
            sh = jnp.dot(shift_ref[j], x_cur, preferred_element_type=jnp.float32)
            top = sh[:8] + jnp.dot(hshift_ref[j], x_prev, preferred_element_type=jnp.float32)
            sh = jnp.concatenate([top, sh[8:]], axis=0)
            acc = acc + convw_ref[j:j + 1, :] * sh
        qk = acc * _sigmoid(acc)
        q_all = qk[:, :MLSTM_WIDTH].astype(jnp.bfloat16)
        k_t = jnp.transpose(qk[:, MLSTM_WIDTH:] * (HEAD_DIM ** -0.5))
        conv_terms.append((q_all, k_t))

    for bb in range(MLSTM_BATCH):
        b_rows, c_rows, cm_rows = gate_terms[bb]
        q_all, k_t = conv_terms[bb]
        m_in4 = jnp.concatenate(
            [m_ref[bb * MLSTM_HEADS + h][0:1, 0:1] for h in range(MLSTM_HEADS)], axis=0)
        mx_rows = jnp.maximum(cm_rows, m_in4)
        inter_rows = jnp.exp(m_in4 - mx_rows)
        einv_rows = jnp.exp(-(b_rows + mx_rows))
        fac_t = jnp.transpose(jnp.concatenate(
            [mx_rows, inter_rows, einv_rows, jnp.zeros_like(mx_rows)], axis=0))

        for h in range(MLSTM_HEADS):
            lo = h * HEAD_DIM
            st = bb * MLSTM_HEADS + h
            q = q_all[:, lo:lo + HEAD_DIM]
            kt = k_t[lo:lo + HEAD_DIM, :]
            v_ext = jnp.concatenate([v_ref[bb, :, lo:lo + HEAD_DIM], ones_blk], axis=1)
            mx_col = fac_t[:, h:h + 1]
            inter_col = fac_t[:, MLSTM_HEADS + h:MLSTM_HEADS + h + 1]
            einv_col = fac_t[:, 2 * MLSTM_HEADS + h:2 * MLSTM_HEADS + h + 1]
            c_row = c_rows[h:h + 1, :]
            b_tot = b_rows[h:h + 1, L - 1:L]
            cm_tot = cm_rows[h:h + 1, L - 1:L]
            m_in = m_ref[st][0:1, 0:1]
            cn = cn_ref[st]

            s_qk = jnp.dot(q, kt.astype(jnp.bfloat16), preferred_element_type=jnp.float32)
            s = (s_qk * jnp.exp(jnp.where(causal, c_row - mx_col, -jnp.inf))).astype(jnp.bfloat16)
            num = (jnp.dot(s, v_ext, preferred_element_type=jnp.float32)
                   + inter_col * jnp.dot(q, cn.astype(jnp.bfloat16),
                                         preferred_element_type=jnp.float32))
            den = num[:, HEAD_DIM:]
            hh = num[:, :HEAD_DIM] / jnp.maximum(jnp.abs(den), einv_col)

            mu = jnp.mean(hh, axis=-1, keepdims=True)
            dv = hh - mu
            var = jnp.mean(dv * dv, axis=-1, keepdims=True)
            hn = dv * lax.rsqrt(var + EPS) * hng_ref[:, lo:lo + HEAD_DIM]
            og = _sigmoid(o_ref[bb, :, lo:lo + HEAD_DIM].astype(jnp.float32))
            y_ref[bb, :, lo:lo + HEAD_DIM] = (og * hn).astype(y_ref.dtype)

            m_loc = b_tot + cm_tot
            kw_t = (kt * jnp.exp(c_row - cm_tot)).astype(jnp.bfloat16)
            c_loc = jnp.dot(kw_t, v_ext, preferred_element_type=jnp.float32)
            m_new = jnp.maximum(b_tot + m_in, m_loc)
            s_old = jnp.exp(b_tot + m_in - m_new)
            s_loc = jnp.exp(m_loc - m_new)
            cn_ref[st] = s_old * cn + s_loc * c_loc
            m_ref[st] = jnp.broadcast_to(m_new, m_ref.shape[1:])


def _mlstm(p3, gates_b, conv_w, gate_b, hn_g, tri, shifts, halo_shifts):
    batch, seq, _ = p3.shape
    L = CHUNK
    BB = MLSTM_BATCH
    halo_per_chunk = L // HALO
    return pl.pallas_call(
        _mlstm_kernel,
        grid=(batch // BB, seq // L),
        in_specs=[
            pl.BlockSpec((BB, L, 2 * MLSTM_WIDTH), lambda bi, ci: (bi, ci, 0)),
            pl.BlockSpec((BB, HALO, 2 * MLSTM_WIDTH),
                         lambda bi, ci: (bi, jnp.maximum(ci * halo_per_chunk - 1, 0), 0)),
            pl.BlockSpec((BB, L, MLSTM_WIDTH), lambda bi, ci: (bi, ci, 2)),
            pl.BlockSpec((BB, L, MLSTM_WIDTH), lambda bi, ci: (bi, ci, 3)),
            pl.BlockSpec((BB, N_GATES, L), lambda bi, ci: (bi, 0, ci)),
            pl.BlockSpec((CONV_WIDTH, 2 * MLSTM_WIDTH), lambda bi, ci: (0, 0)),
            pl.BlockSpec((N_GATES, 1), lambda bi, ci: (0, 0)),
            pl.BlockSpec((1, MLSTM_WIDTH), lambda bi, ci: (0, 0)),
            pl.BlockSpec((L, L), lambda bi, ci: (0, 0)),
            pl.BlockSpec((CONV_WIDTH - 1, L, L), lambda bi, ci: (0, 0, 0)),
            pl.BlockSpec((CONV_WIDTH - 1, 8, HALO), lambda bi, ci: (0, 0, 0)),
        ],
        out_specs=pl.BlockSpec((BB, L, MLSTM_WIDTH), lambda bi, ci: (bi, ci, 0)),
        out_shape=jax.ShapeDtypeStruct((batch, seq, MLSTM_WIDTH), jnp.bfloat16),
        scratch_shapes=[
            pltpu.VMEM((BB * MLSTM_HEADS, HEAD_DIM, 2 * HEAD_DIM), jnp.float32),
            pltpu.VMEM((BB * MLSTM_HEADS, 8, LANES), jnp.float32),
        ],
        compiler_params=pltpu.CompilerParams(
            dimension_semantics=("parallel", "arbitrary"), vmem_limit_bytes=VMEM_LIMIT),
        name="mlstm",
    )(p3, p3, p3, p3, gates_b, conv_w, gate_b, hn_g, tri, shifts, halo_shifts)


def _out_route_kernel(seq, x_ref, ym_ref, u_ref, up_ref, pw_ref, ps_ref, wo_ref, g2_ref,
                      wrt_ref, br_ref, x1_ref, h2_ref, idx_ref, gate_ref, rank_ref, cnt_ref,
                      carry_ref):
    TM = TM_PROJ
    R = ROUTE_SUB * TM
    i = pl.program_id(0)

    @pl.when(i == 0)
    def _():
        carry_ref[...] = jnp.zeros_like(carry_ref)

    pos0 = (i * R) % seq
    e_id = lax.broadcasted_iota(jnp.int32, (N_EXPERTS, TM), 0).astype(jnp.float32)
    t_row = lax.broadcasted_iota(jnp.int32, (TM, TM), 0)
    t_col = lax.broadcasted_iota(jnp.int32, (TM, TM), 1)
    before = jnp.where(t_row < t_col, 1.0, 0.0).astype(jnp.bfloat16)
    carry = carry_ref[...]
    subs = [slice(sub * TM, (sub + 1) * TM) for sub in range(ROUTE_SUB)]

    halo = jnp.where(pos0 > 0, up_ref[...].astype(jnp.float32), 0.0)
    u_ext = jnp.concatenate([halo, u_ref[...].astype(jnp.float32)], axis=0)
    win_sums = []
    for gi, w in enumerate(POOL_WINDOWS):
        sw = u_ext[:, gi * POOL_GROUP_DIM:(gi + 1) * POOL_GROUP_DIM]
        span = 1
        while span < w:
            sw = sw + pltpu.roll(sw, span, axis=0)
            span *= 2
        win_sums.append(sw)
    y_cats = []
    for sub, rows in enumerate(subs):
        r0 = sub * TM
        pos = (pos0 + r0 + lax.broadcasted_iota(jnp.int32, (TM, 1), 0) + 1).astype(jnp.float32)
        mixed = []
        for gi, w in enumerate(POOL_WINDOWS):
            lo = gi * POOL_GROUP_DIM
            tok = u_ext[HALO + r0:HALO + r0 + TM, lo:lo + POOL_GROUP_DIM]
            pooled = win_sums[gi][HALO + r0:HALO + r0 + TM] / jnp.minimum(pos, float(w)) - tok
            mg = jnp.dot(pooled.astype(jnp.bfloat16), pw_ref[gi],
                         preferred_element_type=jnp.float32)
            mixed.append((mg * ps_ref[:, lo:lo + POOL_GROUP_DIM]).astype(jnp.bfloat16))
        y_cats.append(jnp.concatenate([ym_ref[rows, :]] + mixed, axis=1))

    all_logits = []
    for sub, rows in enumerate(subs):
        r0 = sub * TM
        x1 = x_ref[rows, :] + jnp.dot(y_cats[sub], wo_ref[...], preferred_element_type=jnp.float32)
        x1_ref[rows, :] = x1
        h2 = x1 * lax.rsqrt(jnp.mean(x1 * x1, axis=-1, keepdims=True) + EPS) * g2_ref[...]
        h2b = h2.astype(jnp.bfloat16)
        h2w = _pack_bf16_pairs(h2)
        for s in range(PSLAB):
            h2_ref[pl.ds(r0 * PSLAB + s, TM, stride=PSLAB), :] = h2w[:, s * LANES:(s + 1) * LANES]
        all_logits.append(lax.dot_general(wrt_ref[...], h2b, NT_DIMS,
                                          preferred_element_type=jnp.float32) + br_ref[...])

    for sub, rows in enumerate(subs):
        work = all_logits[sub]
        vals, ids, hots = [], [], []
        for _ in range(TOP_K):
            mk = jnp.max(work, axis=0, keepdims=True)
            ik = jnp.min(jnp.where(work == mk, e_id, float(N_EXPERTS)), axis=0, keepdims=True)
            hot = e_id == ik
            work = jnp.where(hot, -jnp.inf, work)
            vals.append(mk)
            ids.append(ik)
            hots.append(hot)
        ex = [jnp.exp(vk - vals[0]) for vk in vals]
        denom = ex[0] + ex[1] + ex[2] + ex[3]
        gate_ref[:, rows] = jnp.concatenate([e / denom for e in ex], axis=0)
        idx_ref[:, rows] = jnp.concatenate(ids, axis=0).astype(jnp.int32)

        sel_f = sum(jnp.where(hot, 1.0, 0.0) for hot in hots)
        prefix = jnp.dot(sel_f.astype(jnp.bfloat16), before, preferred_element_type=jnp.float32)
        rank_e = carry[:, 0:1] + prefix
        ranks = [jnp.sum(jnp.where(hot, rank_e, 0.0), axis=0, keepdims=True) for hot in hots]
        rank_ref[:, rows] = jnp.concatenate(ranks, axis=0).astype(jnp.int32)
        carry = carry + jnp.sum(sel_f, axis=1, keepdims=True)
    carry_ref[...] = carry
    cnt_ref[...] = carry.astype(jnp.int32)


def _out_route(x2, ym, p, pool_w, pool_s, w_out, g2, wr_t, br, seq):
    T = x2.shape[0]
    TM = ROUTE_SUB * TM_PROJ
    nt = T // TM
    u_blk = N_MAIN // POOL_WIDTH - 1
    halo_per_tile = TM // HALO
    tok_spec = pl.BlockSpec((TOP_K, TM), lambda i: (0, i))
    return pl.pallas_call(
        functools.partial(_out_route_kernel, seq),
        grid=(nt,),
        in_specs=[
            pl.BlockSpec((TM, D_MODEL), lambda i: (i, 0)),
            pl.BlockSpec((TM, MLSTM_WIDTH), lambda i: (i, 0)),
            pl.BlockSpec((TM, POOL_WIDTH), lambda i: (i, u_blk)),
            pl.BlockSpec((HALO, POOL_WIDTH),
                         lambda i: (jnp.maximum(i * halo_per_tile - 1, 0), u_blk)),
            pl.BlockSpec((len(POOL_WINDOWS), POOL_GROUP_DIM, POOL_GROUP_DIM), lambda i: (0, 0, 0)),
            pl.BlockSpec((1, POOL_WIDTH), lambda i: (0, 0)),
            pl.BlockSpec((D_MODEL, D_MODEL), lambda i: (0, 0)),
            pl.BlockSpec((1, D_MODEL), lambda i: (0, 0)),
            pl.BlockSpec((N_EXPERTS, D_MODEL), lambda i: (0, 0)),
            pl.BlockSpec((N_EXPERTS, 1), lambda i: (0, 0)),
        ],
        out_specs=[
            pl.BlockSpec((TM, D_MODEL), lambda i: (i, 0)),
            pl.BlockSpec((TM * PSLAB, LANES), lambda i: (i, 0)),
            tok_spec, tok_spec, tok_spec,
            pl.BlockSpec((N_EXPERTS, LANES), lambda i: (0, 0)),
        ],
        out_shape=[
            jax.ShapeDtypeStruct((T, D_MODEL), jnp.float32),
            jax.ShapeDtypeStruct((T * PSLAB, LANES), jnp.uint32),
            jax.ShapeDtypeStruct((TOP_K, T), jnp.int32),
            jax.ShapeDtypeStruct((TOP_K, T), jnp.float32),
            jax.ShapeDtypeStruct((TOP_K, T), jnp.int32),
            jax.ShapeDtypeStruct((N_EXPERTS, LANES), jnp.int32),
        ],
        scratch_shapes=[
            pltpu.VMEM((N_EXPERTS, LANES), jnp.float32),
        ],
        compiler_params=pltpu.CompilerParams(
            dimension_semantics=("arbitrary",), vmem_limit_bytes=VMEM_LIMIT),
        name="out_route",
    )(x2, ym, p, p, pool_w, pool_s, w_out, g2, wr_t, br)


def _plan(dest_flat, fill):
    n_assign = dest_flat.shape[0]
    n_table = fill.shape[0]
    mesh = plsc.VectorSubcoreMesh(core_axis_name="c", subcore_axis_name="s")

    @pl.kernel(out_type=jax.ShapeDtypeStruct((n_table,), jnp.int32), mesh=mesh,
               scratch_types=[pltpu.VMEM((n_table,), jnp.int32),
                              pltpu.VMEM((PLAN_CHUNK,), jnp.int32)],
               compiler_params=pltpu.CompilerParams(needs_layout_passes=False))
    def plan_kernel(dest_hbm, fill_hbm, out_hbm, table, chunk):
        first = jnp.logical_and(lax.axis_index("c") == 0, lax.axis_index("s") == 0)

        @pl.when(first)
        def _():
            pltpu.sync_copy(fill_hbm, table)

            @pl.loop(0, n_assign // PLAN_CHUNK)
            def _(ci):
                pltpu.sync_copy(dest_hbm.at[pl.ds(ci * PLAN_CHUNK, PLAN_CHUNK)], chunk)

                @pl.loop(0, PLAN_CHUNK // (SC_LANES * PLAN_UNROLL))
                def _(i):
                    for j in range(PLAN_UNROLL):
                        off = (i * PLAN_UNROLL + j) * SC_LANES
                        idx = chunk[pl.ds(off, SC_LANES)]
                        vals = (ci * PLAN_CHUNK + off
                                + lax.broadcasted_iota(jnp.int32, (SC_LANES,), 0))
                        plsc.store_scatter(table, [idx], vals)

            pltpu.sync_copy(table, out_hbm)

    return plan_kernel(dest_flat, fill)


def _expert_kernel(n_tok, bs_ref, slot_ref, h2_ref, wg_ref, bg_ref, wu_ref, bu_ref, wd_ref, bd_ref,
                   yt_ref, *scratch):
    TM = TM_EXPERT
    ROWS = TM * PSLAB
    e = pl.program_id(0)
    n_total = bs_ref[N_EXPERTS]
    xg = scratch[:NBUF]
    ys = scratch[NBUF:2 * NBUF]
    wgb_ref, wub_ref, wdb_ref, gsem, ssem = scratch[2 * NBUF:]

    def token_of(a):
        return a & (n_tok - 1) if n_tok & (n_tok - 1) == 0 else lax.rem(a, n_tok)

    def start_gather(blk, par):
        base = (blk + 1) * TM
        for r in range(TM):
            t = token_of(slot_ref[base + r])
            pltpu.make_async_copy(h2_ref.at[pl.ds(pl.multiple_of(t * PSLAB, PSLAB), PSLAB), :],
                                  xg[par].at[pl.ds(r * PSLAB, PSLAB), :], gsem.at[par]).start()

    def wait_gather(par):
        pltpu.make_async_copy(h2_ref.at[pl.ds(0, ROWS), :], xg[0], gsem.at[par]).wait()

    def start_scatter(blk, par):
        base = (blk + 1) * TM
        for r in range(TM):
            a = slot_ref[base + r]
            pltpu.make_async_copy(ys[par].at[pl.ds(r * PSLAB, PSLAB), :],
                                  yt_ref.at[pl.ds(pl.multiple_of(a * PSLAB, PSLAB), PSLAB), :],
                                  ssem.at[par]).start()

    def wait_scatter(par):
        pltpu.make_async_copy(ys[0], yt_ref.at[pl.ds(0, ROWS), :], ssem.at[par]).wait()

    @pl.when(e == 0)
    def _():
        for blk in range(NBUF - 1):
            start_gather(blk, blk)
        for par in range(NBUF):
            ys[par][...] = jnp.zeros_like(ys[par])
            dump = yt_ref.at[pl.ds((n_tok * TOP_K + par * TM) * PSLAB, ROWS), :]
            cp = pltpu.make_async_copy(ys[par], dump, ssem.at[par])
            cp.start()
            cp.wait()

    wgb_ref[...] = wg_ref[0].astype(jnp.bfloat16)
    wub_ref[...] = wu_ref[0].astype(jnp.bfloat16)
    wdb_ref[...] = wd_ref[0].astype(jnp.bfloat16)

    def block_step(g, par):
        prv = (par + NBUF - 1) % NBUF
        wait_gather(par)

        @pl.when(g >= NBUF - 1)
        def _():
            wait_scatter(par)

        start_gather(g + NBUF - 1, prv)
        start_scatter(g - 1, prv)
        words = [xg[par][pl.ds(s, TM, stride=PSLAB), :] for s in range(PSLAB)]
        x = jnp.concatenate([_unpack_lo(w).astype(jnp.bfloat16) for w in words]
                            + [_unpack_hi(w).astype(jnp.bfloat16) for w in words], axis=1)
        gate = jnp.dot(x, wgb_ref[...], preferred_element_type=jnp.float32) + bg_ref[0]
        up = jnp.dot(x, wub_ref[...], preferred_element_type=jnp.float32) + bu_ref[0]
        gate = jnp.minimum(gate, SWIGLU_LIMIT)
        up = jnp.clip(up, -SWIGLU_LIMIT, SWIGLU_LIMIT)
        glu = gate * _sigmoid(SWIGLU_ALPHA * gate)
        act = (glu * (up + 1.0)).astype(jnp.bfloat16)
        y = jnp.dot(act, wdb_ref[...], preferred_element_type=jnp.float32) + bd_ref[0]
        packed = _pack_bf16_pairs(y)
        for s in range(PSLAB):
            ys[par][pl.ds(s, TM, stride=PSLAB), :] = packed[:, s * LANES:(s + 1) * LANES]

    def body(g, carry):
        for par in range(NBUF):
            pl.when(g % NBUF == par)(functools.partial(block_step, g, par))
        return carry

    lax.fori_loop(bs_ref[e], bs_ref[e + 1], body, 0)

    @pl.when(e == N_EXPERTS - 1)
    def _():
        g = n_total
        for par in range(NBUF):
            @pl.when((g - 1) % NBUF == par)
            def _():
                start_scatter(g - 1, par)
        for j in range(NBUF - 1):
            wait_gather((g + j) % NBUF)
        wait_scatter((g - 1) % NBUF)
        for j in range(2, NBUF + 1):
            @pl.when(g >= j - 1)
            def _():
                wait_scatter((g + NBUF - j) % NBUF)


def _experts(block_start, slot_buf, h2_slab, w_gate, b_gate, w_up, b_up, w_down, b_down, n_tok):
    TM = TM_EXPERT
    n_assign = n_tok * TOP_K
    w_spec = pl.BlockSpec((1, D_MODEL, D_FF), lambda e, bs, sl: (e, 0, 0))
    bias_spec = pl.BlockSpec((1, 1, D_FF), lambda e, bs, sl: (e, 0, 0))
    buf = pltpu.VMEM((TM * PSLAB, LANES), jnp.uint32)
    grid_spec = pltpu.PrefetchScalarGridSpec(
        num_scalar_prefetch=2,
        grid=(N_EXPERTS,),
        in_specs=[
            pl.BlockSpec(memory_space=pl.ANY),
            w_spec, bias_spec, w_spec, bias_spec, w_spec, bias_spec,
        ],
        out_specs=pl.BlockSpec(memory_space=pl.ANY),
        scratch_shapes=[
            *([buf] * (2 * NBUF)),
            pltpu.VMEM((D_MODEL, D_FF), jnp.bfloat16),
            pltpu.VMEM((D_MODEL, D_FF), jnp.bfloat16),
            pltpu.VMEM((D_FF, D_MODEL), jnp.bfloat16),
            pltpu.SemaphoreType.DMA((NBUF,)),
            pltpu.SemaphoreType.DMA((NBUF,)),
        ],
    )
    return pl.pallas_call(
        functools.partial(_expert_kernel, n_tok),
        grid_spec=grid_spec,
        out_shape=jax.ShapeDtypeStruct(((n_assign + NBUF * TM) * PSLAB, LANES), jnp.uint32),
        compiler_params=pltpu.CompilerParams(
            dimension_semantics=("arbitrary",), vmem_limit_bytes=VMEM_LIMIT),
        name="experts",
    )(block_start, slot_buf, h2_slab, w_gate, b_gate, w_up, b_up, w_down, b_down)


def _combine_kernel(normalize, x1_ref, y0_ref, y1_ref, y2_ref, y3_ref, gate_ref, g_ref, o_ref):
    TM = TM_COMBINE
    gates = jnp.concatenate([gate_ref[...], jnp.zeros((8 - TOP_K, TM), jnp.float32)], axis=0)
    g_cols = jnp.transpose(gates)
    g_bc = [jnp.broadcast_to(g_cols[:, k:k + 1], (TM, LANES)) for k in range(TOP_K)]
    ssq = jnp.zeros((TM, LANES), jnp.float32)
    parts = [x1_ref[:, s * LANES:(s + 1) * LANES] for s in range(SLAB)]
    for s in range(PSLAB):
        for k, y_ref in enumerate((y0_ref, y1_ref, y2_ref, y3_ref)):
            w = y_ref[pl.ds(s, TM, stride=PSLAB), :]
            parts[s] = parts[s] + g_bc[k] * _unpack_lo(w)
            parts[PSLAB + s] = parts[PSLAB + s] + g_bc[k] * _unpack_hi(w)
    for acc in parts:
        ssq = ssq + acc * acc
    if normalize:
        inv = lax.rsqrt(jnp.sum(ssq, axis=-1, keepdims=True) * (1.0 / D_MODEL) + EPS)
        for s in range(SLAB):
            o_ref[:, s * LANES:(s + 1) * LANES] = parts[s] * inv * g_ref[:, s * LANES:(s + 1) * LANES]
    else:
        for s in range(SLAB):
            o_ref[:, s * LANES:(s + 1) * LANES] = parts[s]


def _combine(x1, y_tok, gate_t, gf, normalize):
    T = x1.shape[0]
    TM = TM_COMBINE
    nt = T // TM

    def y_spec(k):
        return pl.BlockSpec((TM * PSLAB, LANES), lambda i: (k * nt + i, 0))

    return pl.pallas_call(
        functools.partial(_combine_kernel, normalize),
        grid=(nt,),
        in_specs=[
            pl.BlockSpec((TM, D_MODEL), lambda i: (i, 0)),
            y_spec(0), y_spec(1), y_spec(2), y_spec(3),
            pl.BlockSpec((TOP_K, TM), lambda i: (0, i)),
            pl.BlockSpec((1, D_MODEL), lambda i: (0, 0)),
        ],
        out_specs=pl.BlockSpec((TM, D_MODEL), lambda i: (i, 0)),
        out_shape=jax.ShapeDtypeStruct((T, D_MODEL), jnp.float32),
        compiler_params=pltpu.CompilerParams(
            dimension_semantics=("parallel",), vmem_limit_bytes=VMEM_LIMIT),
        name="combine",
    )(x1, y_tok, y_tok, y_tok, y_tok, gate_t, gf)


def kernel(x, norm1_g, w_in, ig_b, fg_b, conv_w, head_norm_g, pool_w, pool_scale, w_out, norm2_g,
           w_router, b_router, w_gate, b_gate, w_up, b_up, w_down, b_down, normf_g):
    B, S, D = x.shape
    T = B * S
    depth = norm1_g.shape[0]
    W = MLSTM_WIDTH
    f32, bf16 = jnp.float32, jnp.bfloat16

    L = CHUNK
    t_l = lax.broadcasted_iota(jnp.int32, (L, L), 0)
    t_r = lax.broadcasted_iota(jnp.int32, (L, L), 1)
    tri = (t_r <= t_l).astype(f32)
    shifts = jnp.stack([(t_l - t_r == CONV_WIDTH - 1 - j).astype(bf16)
                        for j in range(CONV_WIDTH - 1)])
    h_t = lax.broadcasted_iota(jnp.int32, (8, HALO), 0)
    h_r = lax.broadcasted_iota(jnp.int32, (8, HALO), 1)
    halo_shifts = jnp.stack([(h_r - HALO - h_t == -(CONV_WIDTH - 1 - j)).astype(bf16)
                             for j in range(CONV_WIDTH - 1)])

    n_assign = T * TOP_K
    n_blocks = -(-n_assign // TM_EXPERT) + N_EXPERTS
    n_rows = n_blocks * TM_EXPERT
    n_table = n_rows + NBUF * TM_EXPERT
    fill = n_assign + ((jnp.arange(n_table, dtype=jnp.int32) + (NBUF - 1) * TM_EXPERT)
                       % (NBUF * TM_EXPERT))
    x2 = x.reshape(T, D)
    for l in range(depth):
        w = w_in[l]
        w_a = w[:, :4 * W].astype(bf16)
        w_u = w[:, 4 * W + N_GATES:].astype(bf16)
        wg_t = jnp.zeros((BF16_SUBLANES, D), bf16).at[:N_GATES].set(
            w[:, 4 * W:4 * W + N_GATES].T.astype(bf16))
        p, gates_t = _in_proj(x2, norm1_g[l][None, :], w_a, w_u, wg_t)

        gate_b = jnp.concatenate([ig_b[l], fg_b[l]])[:, None].astype(f32)
        gates_b = gates_t.reshape(N_GATES, B, S).transpose(1, 0, 2)
        ym = _mlstm(p.reshape(B, S, N_MAIN), gates_b, conv_w[l].astype(f32), gate_b,
                    head_norm_g[l][None, :], tri, shifts, halo_shifts).reshape(T, W)

        x1, h2, idx_t, gate_t, rank_t, cnt = _out_route(
            x2, ym, p, pool_w[l].astype(bf16), pool_scale[l][None, :], w_out[l].astype(bf16),
            norm2_g[l][None, :], w_router[l].T.astype(bf16), b_router[l][:, None], S)

        counts = cnt[:, 0]
        padded = ((counts + TM_EXPERT - 1) // TM_EXPERT) * TM_EXPERT
        padded_end = jnp.cumsum(padded)
        padded_start = padded_end - padded
        expert_ids = jnp.arange(N_EXPERTS, dtype=jnp.int32)[:, None, None]
        start_of = jnp.sum(jnp.where(idx_t[None] == expert_ids, padded_start[:, None, None], 0), axis=0)
        dest = start_of + rank_t
        block_start = jnp.concatenate(
            [jnp.zeros((1,), jnp.int32), (padded_end // TM_EXPERT).astype(jnp.int32)])

        slot_buf = _plan(dest.reshape(-1) + TM_EXPERT, fill)
        y_tok = _experts(block_start, slot_buf, h2, w_gate[l], b_gate[l][:, None, :],
                         w_up[l], b_up[l][:, None, :], w_down[l], b_down[l][:, None, :], T)
        last = l + 1 == depth
        x2 = _combine(x1, y_tok, gate_t, normf_g[None, :], last)
    return x2.reshape(B, S, D)
```

```python
import functools

import jax
import jax.numpy as jnp
from jax import lax
from jax.experimental import pallas as pl
from jax.experimental.pallas import tpu as pltpu
from jax.experimental.pallas import tpu_sc as plsc

D_MODEL = 1024
MLSTM_WIDTH = 512
MLSTM_HEADS = 4
HEAD_DIM = 128
CONV_WIDTH = 4
POOL_WIDTH = 512
POOL_WINDOWS = (2, 4, 8, 16)
POOL_GROUP_DIM = 128
N_EXPERTS = 32
TOP_K = 4
D_FF = 1024
SWIGLU_LIMIT = 7.0
SWIGLU_ALPHA = 1.702
EPS = 1e-5

N_MAIN = 4 * MLSTM_WIDTH + POOL_WIDTH
N_GATES = 2 * MLSTM_HEADS

LANES = 128
BF16_SUBLANES = 16
VMEM_LIMIT = 56 * 1024 * 1024

TM_PROJ = 512
TM_COMBINE = 1024
ROUTE_SUB = 2
CHUNK = 256
MLSTM_BATCH = 4
HALO = 16
TM_EXPERT = 512
NBUF = 3
ROW_DMA_PRIORITY = 1
SLAB = D_MODEL // LANES
PSLAB = SLAB // 2
PLAN_CHUNK = 32768
SC_LANES = 16
PLAN_UNROLL = 8

NT_DIMS = (((1,), (1,)), ((), ()))


def _sigmoid(x):
    return 1.0 / (1.0 + jnp.exp(-x))


def _pack_bf16_pairs(v):
    half = v.shape[1] // 2
    lo = pltpu.bitcast(v[:, :half].astype(jnp.bfloat16).astype(jnp.float32), jnp.uint32)
    hi = pltpu.bitcast(v[:, half:].astype(jnp.bfloat16).astype(jnp.float32), jnp.uint32)
    return (lo >> 16) | (hi & jnp.uint32(0xFFFF0000))


def _unpack_lo(w):
    return pltpu.bitcast(w << 16, jnp.float32)


def _unpack_hi(w):
    return pltpu.bitcast(w & jnp.uint32(0xFFFF0000), jnp.float32)


def _in_proj_kernel(x_ref, g_ref, wa_ref, wu_ref, wgt_ref, p_ref, gt_ref):
    x = x_ref[...]
    h = x * lax.rsqrt(jnp.mean(x * x, axis=-1, keepdims=True) + EPS) * g_ref[...]
    hb = h.astype(jnp.bfloat16)
    n_a = wa_ref.shape[1]
    p_ref[:, :n_a] = jnp.dot(hb, wa_ref[...], preferred_element_type=jnp.float32).astype(p_ref.dtype)
    p_ref[:, n_a:] = jnp.dot(hb, wu_ref[...], preferred_element_type=jnp.float32).astype(p_ref.dtype)
    gt = lax.dot_general(wgt_ref[...], hb, NT_DIMS, preferred_element_type=jnp.float32)
    gt_ref[...] = gt[:N_GATES]


def _in_proj(x2, g1, w_a, w_u, wg_t):
    T = x2.shape[0]
    return pl.pallas_call(
        _in_proj_kernel,
        grid=(T // TM_PROJ,),
        in_specs=[
            pl.BlockSpec((TM_PROJ, D_MODEL), lambda i: (i, 0)),
            pl.BlockSpec((1, D_MODEL), lambda i: (0, 0)),
            pl.BlockSpec(w_a.shape, lambda i: (0, 0)),
            pl.BlockSpec(w_u.shape, lambda i: (0, 0)),
            pl.BlockSpec((BF16_SUBLANES, D_MODEL), lambda i: (0, 0)),
        ],
        out_specs=[
            pl.BlockSpec((TM_PROJ, N_MAIN), lambda i: (i, 0)),
            pl.BlockSpec((N_GATES, TM_PROJ), lambda i: (0, i)),
        ],
        out_shape=[
            jax.ShapeDtypeStruct((T, N_MAIN), jnp.bfloat16),
            jax.ShapeDtypeStruct((N_GATES, T), jnp.float32),
        ],
        compiler_params=pltpu.CompilerParams(
            dimension_semantics=("parallel",), vmem_limit_bytes=VMEM_LIMIT),
        name="in_proj",
    )(x2, g1, w_a, w_u, wg_t)


def _mlstm_kernel(qk_ref, qkp_ref, v_ref, o_ref, gt_ref, convw_ref, gb_ref, hng_ref,
                  tri_ref, shift_ref, hshift_ref, y_ref, cn_ref, m_ref):
    L = CHUNK
    c = pl.program_id(1)

    @pl.when(c == 0)
    def _():
        cn_ref[...] = jnp.zeros_like(cn_ref)
        m_ref[...] = jnp.zeros_like(m_ref)

    row_id = lax.broadcasted_iota(jnp.int32, (L, L), 0)
    col_id = lax.broadcasted_iota(jnp.int32, (L, L), 1)
    causal = col_id <= row_id
    ones_blk = jnp.ones((L, HEAD_DIM), jnp.bfloat16)
    lane = lax.broadcasted_iota(jnp.int32, (MLSTM_HEADS, L), 1)

    gate_terms = []
    for bb in range(MLSTM_BATCH):
        gt = gt_ref[bb] + gb_ref[...]
        f = gt[MLSTM_HEADS:]
        lf = jnp.minimum(f, 0.0) - jnp.log(1.0 + jnp.exp(-jnp.abs(f)))
        ig = gt[:MLSTM_HEADS]
        b_rows = lax.dot_general(lf, tri_ref[...], NT_DIMS, precision=lax.Precision.HIGHEST,
                                 preferred_element_type=jnp.float32)
        c_rows = ig - b_rows
        cm_rows = c_rows
        d = 1
        while d < L:
            cm_rows = jnp.maximum(
                cm_rows, jnp.where(lane >= d, pltpu.roll(cm_rows, d, axis=1), -jnp.inf))
            d *= 2
        gate_terms.append((b_rows, c_rows, cm_rows))

    conv_terms = []
    for bb in range(MLSTM_BATCH):
        x_cur = qk_ref[bb]
        x_prev = jnp.where(c > 0, qkp_ref[bb], jnp.zeros((HALO, 2 * MLSTM_WIDTH), jnp.bfloat16))
        acc = convw_ref[CONV_WIDTH - 1:CONV_WIDTH, :] * x_cur.astype(jnp.float32)
        for j in range(CONV_WIDTH - 1):
            sh = jnp.dot(shift_ref[j], x_cur, preferred_element_type=jnp.float32)
            top = sh[:8] + jnp.dot(hshift_ref[j], x_prev, preferred_element_type=jnp.float32)
            sh = jnp.concatenate([top, sh[8:]], axis=0)
            acc = acc + convw_ref[j:j + 1, :] * sh
        qk = acc * _sigmoid(acc)
        q_all = qk[:, :MLSTM_WIDTH].astype(jnp.bfloat16)
        k_t = jnp.transpose(qk[:, MLSTM_WIDTH:] * (HEAD_DIM ** -0.5))
        conv_terms.append((q_all, k_t))

    for bb in range(MLSTM_BATCH):
        b_rows, c_rows, cm_rows = gate_terms[bb]
        q_all, k_t = conv_terms[bb]
        m_in4 = jnp.concatenate(
            [m_ref[bb * MLSTM_HEADS + h][0:1, 0:1] for h in range(MLSTM_HEADS)], axis=0)
        mx_rows = jnp.maximum(cm_rows, m_in4)
        inter_rows = jnp.exp(m_in4 - mx_rows)
        einv_rows = jnp.exp(-(b_rows + mx_rows))
        fac_t = jnp.transpose(jnp.concatenate(
            [mx_rows, inter_rows, einv_rows, jnp.zeros_like(mx_rows)], axis=0))

        for h in range(MLSTM_HEADS):
            lo = h * HEAD_DIM
            st = bb * MLSTM_HEADS + h
            q = q_all[:, lo:lo + HEAD_DIM]
            kt = k_t[lo:lo + HEAD_DIM, :]
            v_ext = jnp.concatenate([v_ref[bb, :, lo:lo + HEAD_DIM], ones_blk], axis=1)
            mx_col = fac_t[:, h:h + 1]
            inter_col = fac_t[:, MLSTM_HEADS + h:MLSTM_HEADS + h + 1]
            einv_col = fac_t[:, 2 * MLSTM_HEADS + h:2 * MLSTM_HEADS + h + 1]
            c_row = c_rows[h:h + 1, :]
            b_tot = b_rows[h:h + 1, L - 1:L]
            cm_tot = cm_rows[h:h + 1, L - 1:L]
            m_in = m_ref[st][0:1, 0:1]
            cn = cn_ref[st]

            s_qk = jnp.dot(q, kt.astype(jnp.bfloat16), preferred_element_type=jnp.float32)
            s = (s_qk * jnp.exp(jnp.where(causal, c_row - mx_col, -jnp.inf))).astype(jnp.bfloat16)
            num = (jnp.dot(s, v_ext, preferred_element_type=jnp.float32)
                   + inter_col * jnp.dot(q, cn.astype(jnp.bfloat16),
                                         preferred_element_type=jnp.float32))
            den = num[:, HEAD_DIM:]
            hh = num[:, :HEAD_DIM] / jnp.maximum(jnp.abs(den), einv_col)

            mu = jnp.mean(hh, axis=-1, keepdims=True)
            dv = hh - mu
            var = jnp.mean(dv * dv, axis=-1, keepdims=True)
            hn = dv * lax.rsqrt(var + EPS) * hng_ref[:, lo:lo + HEAD_DIM]
            og = _sigmoid(o_ref[bb, :, lo:lo + HEAD_DIM].astype(jnp.float32))
            y_ref[bb, :, lo:lo + HEAD_DIM] = (og * hn).astype(y_ref.dtype)

            m_loc = b_tot + cm_tot
            kw_t = (kt * jnp.exp(c_row - cm_tot)).astype(jnp.bfloat16)
            c_loc = jnp.dot(kw_t, v_ext, preferred_element_type=jnp.float32)
            m_new = jnp.maximum(b_tot + m_in, m_loc)
            s_old = jnp.exp(b_tot + m_in - m_new)
            s_loc = jnp.exp(m_loc - m_new)
            cn_ref[st] = s_old * cn + s_loc * c_loc
            m_ref[st] = jnp.broadcast_to(m_new, m_ref.shape[1:])


def _mlstm(p3, gates_b, conv_w, gate_b, hn_g, tri, shifts, halo_shifts):
    batch, seq, _ = p3.shape
    L = CHUNK
    BB = MLSTM_BATCH
    halo_per_chunk = L // HALO
    return pl.pallas_call(
        _mlstm_kernel,
        grid=(batch // BB, seq // L),
        in_specs=[
            pl.BlockSpec((BB, L, 2 * MLSTM_WIDTH), lambda bi, ci: (bi, ci, 0)),
            pl.BlockSpec((BB, HALO, 2 * MLSTM_WIDTH),
                         lambda bi, ci: (bi, jnp.maximum(ci * halo_per_chunk - 1, 0), 0)),
            pl.BlockSpec((BB, L, MLSTM_WIDTH), lambda bi, ci: (bi, ci, 2)),
            pl.BlockSpec((BB, L, MLSTM_WIDTH), lambda bi, ci: (bi, ci, 3)),
            pl.BlockSpec((BB, N_GATES, L), lambda bi, ci: (bi, 0, ci)),
            pl.BlockSpec((CONV_WIDTH, 2 * MLSTM_WIDTH), lambda bi, ci: (0, 0)),
            pl.BlockSpec((N_GATES, 1), lambda bi, ci: (0, 0)),
            pl.BlockSpec((1, MLSTM_WIDTH), lambda bi, ci: (0, 0)),
            pl.BlockSpec((L, L), lambda bi, ci: (0, 0)),
            pl.BlockSpec((CONV_WIDTH - 1, L, L), lambda bi, ci: (0, 0, 0)),
            pl.BlockSpec((CONV_WIDTH - 1, 8, HALO), lambda bi, ci: (0, 0, 0)),
        ],
        out_specs=pl.BlockSpec((BB, L, MLSTM_WIDTH), lambda bi, ci: (bi, ci, 0)),
        out_shape=jax.ShapeDtypeStruct((batch, seq, MLSTM_WIDTH), jnp.bfloat16),
        scratch_shapes=[
            pltpu.VMEM((BB * MLSTM_HEADS, HEAD_DIM, 2 * HEAD_DIM), jnp.float32),
            pltpu.VMEM((BB * MLSTM_HEADS, 8, LANES), jnp.float32),
        ],
        compiler_params=pltpu.CompilerParams(
            dimension_semantics=("parallel", "arbitrary"), vmem_limit_bytes=VMEM_LIMIT),
        name="mlstm",
    )(p3, p3, p3, p3, gates_b, conv_w, gate_b, hn_g, tri, shifts, halo_shifts)


def _out_route_kernel(seq, x_ref, ym_ref, u_ref, up_ref, pw_ref, ps_ref, wo_ref, g2_ref,
                      wrt_ref, br_ref, x1_ref, h2_ref, idx_ref, gate_ref, rank_ref, cnt_ref,
                      carry_ref):
    TM = TM_PROJ
    R = ROUTE_SUB * TM
    i = pl.program_id(0)

    @pl.when(i == 0)
    def _():
        carry_ref[...] = jnp.zeros_like(carry_ref)

    pos0 = (i * R) % seq
    e_id = lax.broadcasted_iota(jnp.int32, (N_EXPERTS, TM), 0).astype(jnp.float32)
    t_row = lax.broadcasted_iota(jnp.int32, (TM, TM), 0)
    t_col = lax.broadcasted_iota(jnp.int32, (TM, TM), 1)
    before = jnp.where(t_row < t_col, 1.0, 0.0).astype(jnp.bfloat16)
    carry = carry_ref[...]
    subs = [slice(sub * TM, (sub + 1) * TM) for sub in range(ROUTE_SUB)]

    halo = jnp.where(pos0 > 0, up_ref[...].astype(jnp.float32), 0.0)
    u_ext = jnp.concatenate([halo, u_ref[...].astype(jnp.float32)], axis=0)
    win_sums = []
    for gi, w in enumerate(POOL_WINDOWS):
        sw = u_ext[:, gi * POOL_GROUP_DIM:(gi + 1) * POOL_GROUP_DIM]
        span = 1
        while span < w:
            sw = sw + pltpu.roll(sw, span, axis=0)
            span *= 2
        win_sums.append(sw)
    y_cats = []
    for sub, rows in enumerate(subs):
        r0 = sub * TM
        pos = (pos0 + r0 + lax.broadcasted_iota(jnp.int32, (TM, 1), 0) + 1).astype(jnp.float32)
        mixed = []
        for gi, w in enumerate(POOL_WINDOWS):
            lo = gi * POOL_GROUP_DIM
            tok = u_ext[HALO + r0:HALO + r0 + TM, lo:lo + POOL_GROUP_DIM]
            pooled = win_sums[gi][HALO + r0:HALO + r0 + TM] / jnp.minimum(pos, float(w)) - tok
            mg = jnp.dot(pooled.astype(jnp.bfloat16), pw_ref[gi],
                         preferred_element_type=jnp.float32)
            mixed.append((mg * ps_ref[:, lo:lo + POOL_GROUP_DIM]).astype(jnp.bfloat16))
        y_cats.append(jnp.concatenate([ym_ref[rows, :]] + mixed, axis=1))

    all_logits = []
    for sub, rows in enumerate(subs):
        r0 = sub * TM
        x1 = x_ref[rows, :] + jnp.dot(y_cats[sub], wo_ref[...], preferred_element_type=jnp.float32)
        x1_ref[rows, :] = x1
        h2 = x1 * lax.rsqrt(jnp.mean(x1 * x1, axis=-1, keepdims=True) + EPS) * g2_ref[...]
        h2b = h2.astype(jnp.bfloat16)
        h2w = _pack_bf16_pairs(h2)
        for s in range(PSLAB):
            h2_ref[pl.ds(r0 * PSLAB + s, TM, stride=PSLAB), :] = h2w[:, s * LANES:(s + 1) * LANES]
        all_logits.append(lax.dot_general(wrt_ref[...], h2b, NT_DIMS,
                                          preferred_element_type=jnp.float32) + br_ref[...])

    for sub, rows in enumerate(subs):
        work = all_logits[sub]
        vals, ids, hots = [], [], []
        for _ in range(TOP_K):
            mk = jnp.max(work, axis=0, keepdims=True)
            ik = jnp.min(jnp.where(work == mk, e_id, float(N_EXPERTS)), axis=0, keepdims=True)
            hot = e_id == ik
            work = jnp.where(hot, -jnp.inf, work)
            vals.append(mk)
            ids.append(ik)
            hots.append(hot)
        ex = [jnp.exp(vk - vals[0]) for vk in vals]
        denom = ex[0] + ex[1] + ex[2] + ex[3]
        gate_ref[:, rows] = jnp.concatenate([e / denom for e in ex], axis=0)
        idx_ref[:, rows] = jnp.concatenate(ids, axis=0).astype(jnp.int32)

        sel_f = sum(jnp.where(hot, 1.0, 0.0) for hot in hots)
        prefix = jnp.dot(sel_f.astype(jnp.bfloat16), before, preferred_element_type=jnp.float32)
        rank_e = carry[:, 0:1] + prefix
        ranks = [jnp.sum(jnp.where(hot, rank_e, 0.0), axis=0, keepdims=True) for hot in hots]
        rank_ref[:, rows] = jnp.concatenate(ranks, axis=0).astype(jnp.int32)
        carry = carry + jnp.sum(sel_f, axis=1, keepdims=True)
    carry_ref[...] = carry
    cnt_ref[...] = carry.astype(jnp.int32)


def _out_route(x2, ym, p, pool_w, pool_s, w_out, g2, wr_t, br, seq):
    T = x2.shape[0]
    TM = ROUTE_SUB * TM_PROJ
    nt = T // TM
    u_blk = N_MAIN // POOL_WIDTH - 1
    halo_per_tile = TM // HALO
    tok_spec = pl.BlockSpec((TOP_K, TM), lambda i: (0, i))
    return pl.pallas_call(
        functools.partial(_out_route_kernel, seq),
        grid=(nt,),
        in_specs=[
            pl.BlockSpec((TM, D_MODEL), lambda i: (i, 0)),
            pl.BlockSpec((TM, MLSTM_WIDTH), lambda i: (i, 0)),
            pl.BlockSpec((TM, POOL_WIDTH), lambda i: (i, u_blk)),
            pl.BlockSpec((HALO, POOL_WIDTH),
                         lambda i: (jnp.maximum(i * halo_per_tile - 1, 0), u_blk)),
            pl.BlockSpec((len(POOL_WINDOWS), POOL_GROUP_DIM, POOL_GROUP_DIM), lambda i: (0, 0, 0)),
            pl.BlockSpec((1, POOL_WIDTH), lambda i: (0, 0)),
            pl.BlockSpec((D_MODEL, D_MODEL), lambda i: (0, 0)),
            pl.BlockSpec((1, D_MODEL), lambda i: (0, 0)),
            pl.BlockSpec((N_EXPERTS, D_MODEL), lambda i: (0, 0)),
            pl.BlockSpec((N_EXPERTS, 1), lambda i: (0, 0)),
        ],
        out_specs=[
            pl.BlockSpec((TM, D_MODEL), lambda i: (i, 0)),
            pl.BlockSpec((TM * PSLAB, LANES), lambda i: (i, 0)),
            tok_spec, tok_spec, tok_spec,
            pl.BlockSpec((N_EXPERTS, LANES), lambda i: (0, 0)),
        ],
        out_shape=[
            jax.ShapeDtypeStruct((T, D_MODEL), jnp.float32),
            jax.ShapeDtypeStruct((T * PSLAB, LANES), jnp.uint32),
            jax.ShapeDtypeStruct((TOP_K, T), jnp.int32),
            jax.ShapeDtypeStruct((TOP_K, T), jnp.float32),
            jax.ShapeDtypeStruct((TOP_K, T), jnp.int32),
            jax.ShapeDtypeStruct((N_EXPERTS, LANES), jnp.int32),
        ],
        scratch_shapes=[
            pltpu.VMEM((N_EXPERTS, LANES), jnp.float32),
        ],
        compiler_params=pltpu.CompilerParams(
            dimension_semantics=("arbitrary",), vmem_limit_bytes=VMEM_LIMIT),
        name="out_route",
    )(x2, ym, p, p, pool_w, pool_s, w_out, g2, wr_t, br)


def _plan(dest_flat, fill):
    n_assign = dest_flat.shape[0]
    n_table = fill.shape[0]
    mesh = plsc.VectorSubcoreMesh(core_axis_name="c", subcore_axis_name="s")

    @pl.kernel(out_type=jax.ShapeDtypeStruct((n_table,), jnp.int32), mesh=mesh,
               scratch_types=[pltpu.VMEM((n_table,), jnp.int32),
                              pltpu.VMEM((PLAN_CHUNK,), jnp.int32)],
               compiler_params=pltpu.CompilerParams(needs_layout_passes=False))
    def plan_kernel(dest_hbm, fill_hbm, out_hbm, table, chunk):
        first = jnp.logical_and(lax.axis_index("c") == 0, lax.axis_index("s") == 0)

        @pl.when(first)
        def _():
            pltpu.sync_copy(fill_hbm, table)

            @pl.loop(0, n_assign // PLAN_CHUNK)
            def _(ci):
                pltpu.sync_copy(dest_hbm.at[pl.ds(ci * PLAN_CHUNK, PLAN_CHUNK)], chunk)

                @pl.loop(0, PLAN_CHUNK // (SC_LANES * PLAN_UNROLL))
                def _(i):
                    for j in range(PLAN_UNROLL):
                        off = (i * PLAN_UNROLL + j) * SC_LANES
                        idx = chunk[pl.ds(off, SC_LANES)]
                        vals = (ci * PLAN_CHUNK + off
                                + lax.broadcasted_iota(jnp.int32, (SC_LANES,), 0))
                        plsc.store_scatter(table, [idx], vals)

            pltpu.sync_copy(table, out_hbm)

    return plan_kernel(dest_flat, fill)


def _expert_kernel(n_tok, bs_ref, slot_ref, h2_ref, wg_ref, bg_ref, wu_ref, bu_ref, wd_ref, bd_ref,
                   yt_ref, *scratch):
    TM = TM_EXPERT
    ROWS = TM * PSLAB
    e = pl.program_id(0)
    n_total = bs_ref[N_EXPERTS]
    xg = scratch[:NBUF]
    ys = scratch[NBUF:2 * NBUF]
    wgb_ref, wub_ref, wdb_ref, gsem, ssem = scratch[2 * NBUF:]

    def token_of(a):
        return a & (n_tok - 1) if n_tok & (n_tok - 1) == 0 else lax.rem(a, n_tok)

    def gather_row(base, par, r):
        t = token_of(slot_ref[base + r])
        pltpu.make_async_copy(h2_ref.at[pl.ds(pl.multiple_of(t * PSLAB, PSLAB), PSLAB), :],
                              xg[par].at[pl.ds(pl.multiple_of(r * PSLAB, PSLAB), PSLAB), :],
                              gsem.at[par]).start(priority=ROW_DMA_PRIORITY)

    def start_gather(blk, par, unrolled=True):
        base = (blk + 1) * TM
        if unrolled:
            for r in range(TM):
                gather_row(base, par, r)
        else:
            lax.fori_loop(0, TM, lambda r, c: (gather_row(base, par, r), c)[1], 0)

    def wait_gather(par):
        pltpu.make_async_copy(h2_ref.at[pl.ds(0, ROWS), :], xg[0], gsem.at[par]).wait()

    def scatter_row(base, par, r):
        a = slot_ref[base + r]
        pltpu.make_async_copy(ys[par].at[pl.ds(pl.multiple_of(r * PSLAB, PSLAB), PSLAB), :],
                              yt_ref.at[pl.ds(pl.multiple_of(a * PSLAB, PSLAB), PSLAB), :],
                              ssem.at[par]).start(priority=ROW_DMA_PRIORITY)

    def start_scatter(blk, par, unrolled=True):
        base = (blk + 1) * TM
        if unrolled:
            for r in range(TM):
                scatter_row(base, par, r)
        else:
            lax.fori_loop(0, TM, lambda r, c: (scatter_row(base, par, r), c)[1], 0)

    def wait_scatter(par):
        pltpu.make_async_copy(ys[0], yt_ref.at[pl.ds(0, ROWS), :], ssem.at[par]).wait()

    @pl.when(e == 0)
    def _():
        for blk in range(NBUF - 1):
            start_gather(blk, blk, unrolled=False)
        for par in range(NBUF):
            ys[par][...] = jnp.zeros_like(ys[par])
            dump = yt_ref.at[pl.ds((n_tok * TOP_K + par * TM) * PSLAB, ROWS), :]
            cp = pltpu.make_async_copy(ys[par], dump, ssem.at[par])
            cp.start()
            cp.wait()

    wgb_ref[...] = wg_ref[0].astype(jnp.bfloat16)
    wub_ref[...] = wu_ref[0].astype(jnp.bfloat16)
    wdb_ref[...] = wd_ref[0].astype(jnp.bfloat16)

    def block_step(g, par):
        prv = (par + NBUF - 1) % NBUF
        wait_gather(par)

        @pl.when(g >= NBUF - 1)
        def _():
            wait_scatter(par)

        start_gather(g + NBUF - 1, prv)
        start_scatter(g - 1, prv)
        words = [xg[par][pl.ds(s, TM, stride=PSLAB), :] for s in range(PSLAB)]
        x = jnp.concatenate([_unpack_lo(w).astype(jnp.bfloat16) for w in words]
                            + [_unpack_hi(w).astype(jnp.bfloat16) for w in words], axis=1)
        gate = jnp.dot(x, wgb_ref[...], preferred_element_type=jnp.float32) + bg_ref[0]
        up = jnp.dot(x, wub_ref[...], preferred_element_type=jnp.float32) + bu_ref[0]
        gate = jnp.minimum(gate, SWIGLU_LIMIT)
        up = jnp.clip(up, -SWIGLU_LIMIT, SWIGLU_LIMIT)
        glu = gate * _sigmoid(SWIGLU_ALPHA * gate)
        act = (glu * (up + 1.0)).astype(jnp.bfloat16)
        y = jnp.dot(act, wdb_ref[...], preferred_element_type=jnp.float32) + bd_ref[0]
        packed = _pack_bf16_pairs(y)
        for s in range(PSLAB):
            ys[par][pl.ds(s, TM, stride=PSLAB), :] = packed[:, s * LANES:(s + 1) * LANES]

    def body(g, carry):
        for par in range(NBUF):
            pl.when(g % NBUF == par)(functools.partial(block_step, g, par))
        return carry

    lax.fori_loop(bs_ref[e], bs_ref[e + 1], body, 0)

    @pl.when(e == N_EXPERTS - 1)
    def _():
        g = n_total
        for par in range(NBUF):
            @pl.when((g - 1) % NBUF == par)
            def _():
                start_scatter(g - 1, par, unrolled=False)
        for j in range(NBUF - 1):
            wait_gather((g + j) % NBUF)
        wait_scatter((g - 1) % NBUF)
        for j in range(2, NBUF + 1):
            @pl.when(g >= j - 1)
            def _():
                wait_scatter((g + NBUF - j) % NBUF)


def _experts(block_start, slot_buf, h2_slab, w_gate, b_gate, w_up, b_up, w_down, b_down, n_tok):
    TM = TM_EXPERT
    n_assign = n_tok * TOP_K
    w_spec = pl.BlockSpec((1, D_MODEL, D_FF), lambda e, bs, sl: (e, 0, 0))
    bias_spec = pl.BlockSpec((1, 1, D_FF), lambda e, bs, sl: (e, 0, 0))
    buf = pltpu.VMEM((TM * PSLAB, LANES), jnp.uint32)
    grid_spec = pltpu.PrefetchScalarGridSpec(
        num_scalar_prefetch=2,
        grid=(N_EXPERTS,),
        in_specs=[
            pl.BlockSpec(memory_space=pl.ANY),
            w_spec, bias_spec, w_spec, bias_spec, w_spec, bias_spec,
        ],
        out_specs=pl.BlockSpec(memory_space=pl.ANY),
        scratch_shapes=[
            *([buf] * (2 * NBUF)),
            pltpu.VMEM((D_MODEL, D_FF), jnp.bfloat16),
            pltpu.VMEM((D_MODEL, D_FF), jnp.bfloat16),
            pltpu.VMEM((D_FF, D_MODEL), jnp.bfloat16),
            pltpu.SemaphoreType.DMA((NBUF,)),
            pltpu.SemaphoreType.DMA((NBUF,)),
        ],
    )
    return pl.pallas_call(
        functools.partial(_expert_kernel, n_tok),
        grid_spec=grid_spec,
        out_shape=jax.ShapeDtypeStruct(((n_assign + NBUF * TM) * PSLAB, LANES), jnp.uint32),
        compiler_params=pltpu.CompilerParams(
            dimension_semantics=("arbitrary",), vmem_limit_bytes=VMEM_LIMIT),
        name="experts",
    )(block_start, slot_buf, h2_slab, w_gate, b_gate, w_up, b_up, w_down, b_down)


def _combine_kernel(normalize, x1_ref, y0_ref, y1_ref, y2_ref, y3_ref, gate_ref, g_ref, o_ref):
    TM = TM_COMBINE
    gates = jnp.concatenate([gate_ref[...], jnp.zeros((8 - TOP_K, TM), jnp.float32)], axis=0)
    g_cols = jnp.transpose(gates)
    g_bc = [jnp.broadcast_to(g_cols[:, k:k + 1], (TM, LANES)) for k in range(TOP_K)]
    ssq = jnp.zeros((TM, LANES), jnp.float32)
    parts = [x1_ref[:, s * LANES:(s + 1) * LANES] for s in range(SLAB)]
    for s in range(PSLAB):
        for k, y_ref in enumerate((y0_ref, y1_ref, y2_ref, y3_ref)):
            w = y_ref[pl.ds(s, TM, stride=PSLAB), :]
            parts[s] = parts[s] + g_bc[k] * _unpack_lo(w)
            parts[PSLAB + s] = parts[PSLAB + s] + g_bc[k] * _unpack_hi(w)
    for acc in parts:
        ssq = ssq + acc * acc
    if normalize:
        inv = lax.rsqrt(jnp.sum(ssq, axis=-1, keepdims=True) * (1.0 / D_MODEL) + EPS)
        for s in range(SLAB):
            o_ref[:, s * LANES:(s + 1) * LANES] = parts[s] * inv * g_ref[:, s * LANES:(s + 1) * LANES]
    else:
        for s in range(SLAB):
            o_ref[:, s * LANES:(s + 1) * LANES] = parts[s]


def _combine(x1, y_tok, gate_t, gf, normalize):
    T = x1.shape[0]
    TM = TM_COMBINE
    nt = T // TM

    def y_spec(k):
        return pl.BlockSpec((TM * PSLAB, LANES), lambda i: (k * nt + i, 0))

    return pl.pallas_call(
        functools.partial(_combine_kernel, normalize),
        grid=(nt,),
        in_specs=[
            pl.BlockSpec((TM, D_MODEL), lambda i: (i, 0)),
            y_spec(0), y_spec(1), y_spec(2), y_spec(3),
            pl.BlockSpec((TOP_K, TM), lambda i: (0, i)),
            pl.BlockSpec((1, D_MODEL), lambda i: (0, 0)),
        ],
        out_specs=pl.BlockSpec((TM, D_MODEL), lambda i: (i, 0)),
        out_shape=jax.ShapeDtypeStruct((T, D_MODEL), jnp.float32),
        compiler_params=pltpu.CompilerParams(
            dimension_semantics=("parallel",), vmem_limit_bytes=VMEM_LIMIT),
        name="combine",
    )(x1, y_tok, y_tok, y_tok, y_tok, gate_t, gf)


def kernel(x, norm1_g, w_in, ig_b, fg_b, conv_w, head_norm_g, pool_w, pool_scale, w_out, norm2_g,
           w_router, b_router, w_gate, b_gate, w_up, b_up, w_down, b_down, normf_g):
    B, S, D = x.shape
    T = B * S
    depth = norm1_g.shape[0]
    W = MLSTM_WIDTH
    f32, bf16 = jnp.float32, jnp.bfloat16

    L = CHUNK
    t_l = lax.broadcasted_iota(jnp.int32, (L, L), 0)
    t_r = lax.broadcasted_iota(jnp.int32, (L, L), 1)
    tri = (t_r <= t_l).astype(f32)
    shifts = jnp.stack([(t_l - t_r == CONV_WIDTH - 1 - j).astype(bf16)
                        for j in range(CONV_WIDTH - 1)])
    h_t = lax.broadcasted_iota(jnp.int32, (8, HALO), 0)
    h_r = lax.broadcasted_iota(jnp.int32, (8, HALO), 1)
    halo_shifts = jnp.stack([(h_r - HALO - h_t == -(CONV_WIDTH - 1 - j)).astype(bf16)
                             for j in range(CONV_WIDTH - 1)])

    n_assign = T * TOP_K
    n_blocks = -(-n_assign // TM_EXPERT) + N_EXPERTS
    n_rows = n_blocks * TM_EXPERT
    n_table = n_rows + NBUF * TM_EXPERT
    fill = n_assign + ((jnp.arange(n_table, dtype=jnp.int32) + (NBUF - 1) * TM_EXPERT)
                       % (NBUF * TM_EXPERT))
    x2 = x.reshape(T, D)
    for l in range(depth):
        w = w_in[l]
        w_a = w[:, :4 * W].astype(bf16)
        w_u = w[:, 4 * W + N_GATES:].astype(bf16)
        wg_t = jnp.zeros((BF16_SUBLANES, D), bf16).at[:N_GATES].set(
            w[:, 4 * W:4 * W + N_GATES].T.astype(bf16))
        p, gates_t = _in_proj(x2, norm1_g[l][None, :], w_a, w_u, wg_t)

        gate_b = jnp.concatenate([ig_b[l], fg_b[l]])[:, None].astype(f32)
        gates_b = gates_t.reshape(N_GATES, B, S).transpose(1, 0, 2)
        ym = _mlstm(p.reshape(B, S, N_MAIN), gates_b, conv_w[l].astype(f32), gate_b,
                    head_norm_g[l][None, :], tri, shifts, halo_shifts).reshape(T, W)

        x1, h2, idx_t, gate_t, rank_t, cnt = _out_route(
            x2, ym, p, pool_w[l].astype(bf16), pool_scale[l][None, :], w_out[l].astype(bf16),
            norm2_g[l][None, :], w_router[l].T.astype(bf16), b_router[l][:, None], S)

        counts = cnt[:, 0]
        padded = ((counts + TM_EXPERT - 1) // TM_EXPERT) * TM_EXPERT
        padded_end = jnp.cumsum(padded)
        padded_start = padded_end - padded
        expert_ids = jnp.arange(N_EXPERTS, dtype=jnp.int32)[:, None, None]
        start_of = jnp.sum(jnp.where(idx_t[None] == expert_ids, padded_start[:, None, None], 0), axis=0)
        dest = start_of + rank_t
        block_start = jnp.concatenate(
            [jnp.zeros((1,), jnp.int32), (padded_end // TM_EXPERT).astype(jnp.int32)])

        slot_buf = _plan(dest.reshape(-1) + TM_EXPERT, fill)
        y_tok = _experts(block_start, slot_buf, h2, w_gate[l], b_gate[l][:, None, :],
                         w_up[l], b_up[l][:, None, :], w_down[l], b_down[l][:, None, :], T)
        last = l + 1 == depth
        x2 = _combine(x1, y_tok, gate_t, normf_g[None, :], last)
    return x2.reshape(B, S, D)
```

```python
import functools

import jax
import jax.numpy as jnp
from jax import lax
from jax.experimental import pallas as pl
from jax.experimental.pallas import tpu as pltpu
from jax.experimental.pallas import tpu_sc as plsc

D_MODEL = 1024
MLSTM_WIDTH = 512
MLSTM_HEADS = 4
HEAD_DIM = 128
CONV_WIDTH = 4
POOL_WIDTH = 512
POOL_WINDOWS = (2, 4, 8, 16)
POOL_GROUP_DIM = 128
N_EXPERTS = 32
TOP_K = 4
D_FF = 1024
SWIGLU_LIMIT = 7.0
SWIGLU_ALPHA = 1.702
EPS = 1e-5

N_MAIN = 4 * MLSTM_WIDTH + POOL_WIDTH
N_GATES = 2 * MLSTM_HEADS

LANES = 128
BF16_SUBLANES = 16
VMEM_LIMIT = 56 * 1024 * 1024

TM_PROJ = 512
PROJ_SUB = 2
TM_COMBINE = 1024
ROUTE_SUB = 2
CHUNK = 256
MLSTM_BATCH = 4
HALO = 16
TM_EXPERT = 512
NBUF = 3
SLAB = D_MODEL // LANES
PSLAB = SLAB // 2
PLAN_CHUNK = 32768
SC_LANES = 16
PLAN_UNROLL = 8

NT_DIMS = (((1,), (1,)), ((), ()))


def _sigmoid(x):
    return 1.0 / (1.0 + jnp.exp(-x))


def _pack_bf16_pairs(v):
    half = v.shape[1] // 2
    lo = pltpu.bitcast(v[:, :half].astype(jnp.bfloat16).astype(jnp.float32), jnp.uint32)
    hi = pltpu.bitcast(v[:, half:].astype(jnp.bfloat16).astype(jnp.float32), jnp.uint32)
    return (lo >> 16) | (hi & jnp.uint32(0xFFFF0000))


def _unpack_lo(w):
    return pltpu.bitcast(w << 16, jnp.float32)


def _unpack_hi(w):
    return pltpu.bitcast(w & jnp.uint32(0xFFFF0000), jnp.float32)


def _in_proj_kernel(x_ref, g_ref, wa_ref, wu_ref, wgt_ref, p_ref, gt_ref):
    n_a = wa_ref.shape[1]
    for sub in range(PROJ_SUB):
        rows = slice(sub * TM_PROJ, (sub + 1) * TM_PROJ)
        x = x_ref[rows, :]
        h = x * lax.rsqrt(jnp.mean(x * x, axis=-1, keepdims=True) + EPS) * g_ref[...]
        hb = h.astype(jnp.bfloat16)
        p_ref[rows, :n_a] = jnp.dot(hb, wa_ref[...],
                                    preferred_element_type=jnp.float32).astype(p_ref.dtype)
        p_ref[rows, n_a:] = jnp.dot(hb, wu_ref[...],
                                    preferred_element_type=jnp.float32).astype(p_ref.dtype)
        gt = lax.dot_general(wgt_ref[...], hb, NT_DIMS, preferred_element_type=jnp.float32)
        gt_ref[:, rows] = gt[:N_GATES]


def _in_proj(x2, g1, w_a, w_u, wg_t):
    T = x2.shape[0]
    return pl.pallas_call(
        _in_proj_kernel,
        grid=(T // (PROJ_SUB * TM_PROJ),),
        in_specs=[
            pl.BlockSpec((PROJ_SUB * TM_PROJ, D_MODEL), lambda i: (i, 0)),
            pl.BlockSpec((1, D_MODEL), lambda i: (0, 0)),
            pl.BlockSpec(w_a.shape, lambda i: (0, 0)),
            pl.BlockSpec(w_u.shape, lambda i: (0, 0)),
            pl.BlockSpec((BF16_SUBLANES, D_MODEL), lambda i: (0, 0)),
        ],
        out_specs=[
            pl.BlockSpec((PROJ_SUB * TM_PROJ, N_MAIN), lambda i: (i, 0)),
            pl.BlockSpec((N_GATES, PROJ_SUB * TM_PROJ), lambda i: (0, i)),
        ],
        out_shape=[
            jax.ShapeDtypeStruct((T, N_MAIN), jnp.bfloat16),
            jax.ShapeDtypeStruct((N_GATES, T), jnp.float32),
        ],
        compiler_params=pltpu.CompilerParams(
            dimension_semantics=("parallel",), vmem_limit_bytes=VMEM_LIMIT),
        name="in_proj",
    )(x2, g1, w_a, w_u, wg_t)


def _mlstm_kernel(qk_ref, qkp_ref, v_ref, o_ref, gt_ref, convw_ref, gb_ref, hng_ref,
                  tri_ref, shift_ref, hshift_ref, y_ref, cn_ref, m_ref):
    L = CHUNK
    c = pl.program_id(1)

    @pl.when(c == 0)
    def _():
        cn_ref[...] = jnp.zeros_like(cn_ref)
        m_ref[...] = jnp.zeros_like(m_ref)

    row_id = lax.broadcasted_iota(jnp.int32, (L, L), 0)
    col_id = lax.broadcasted_iota(jnp.int32, (L, L), 1)
    causal = col_id <= row_id
    ones_blk = jnp.ones((L, HEAD_DIM), jnp.bfloat16)
    lane = lax.broadcasted_iota(jnp.int32, (MLSTM_HEADS, L), 1)

    gate_terms = []
    for bb in range(MLSTM_BATCH):
        gt = gt_ref[bb] + gb_ref[...]
        f = gt[MLSTM_HEADS:]
        lf = jnp.minimum(f, 0.0) - jnp.log(1.0 + jnp.exp(-jnp.abs(f)))
        ig = gt[:MLSTM_HEADS]
        b_rows = lax.dot_general(lf, tri_ref[...], NT_DIMS, precision=lax.Precision.HIGHEST,
                                 preferred_element_type=jnp.float32)
        c_rows = ig - b_rows
        cm_rows = c_rows
        d = 1
        while d < L:
            cm_rows = jnp.maximum(
                cm_rows, jnp.where(lane >= d, pltpu.roll(cm_rows, d, axis=1), -jnp.inf))
            d *= 2
        gate_terms.append((b_rows, c_rows, cm_rows))

    conv_terms = []
    for bb in range(MLSTM_BATCH):
        x_cur = qk_ref[bb]
        x_prev = jnp.where(c > 0, qkp_ref[bb], jnp.zeros((HALO, 2 * MLSTM_WIDTH), jnp.bfloat16))
        acc = convw_ref[CONV_WIDTH - 1:CONV_WIDTH, :] * x_cur.astype(jnp.float32)
        for j in range(CONV_WIDTH - 1):
            sh = jnp.dot(shift_ref[j], x_cur, preferred_element_type=jnp.float32)
            top = sh[:8] + jnp.dot(hshift_ref[j], x_prev, preferred_element_type=jnp.float32)
            sh = jnp.concatenate([top, sh[8:]], axis=0)
            acc = acc + convw_ref[j:j + 1, :] * sh
        qk = acc * _sigmoid(acc)
        q_all = qk[:, :MLSTM_WIDTH].astype(jnp.bfloat16)
        k_t = jnp.transpose(qk[:, MLSTM_WIDTH:] * (HEAD_DIM ** -0.5))
        conv_terms.append((q_all, k_t))

    for bb in range(MLSTM_BATCH):
        b_rows, c_rows, cm_rows = gate_terms[bb]
        q_all, k_t = conv_terms[bb]
        m_in4 = jnp.concatenate(
            [m_ref[bb * MLSTM_HEADS + h][0:1, 0:1] for h in range(MLSTM_HEADS)], axis=0)
        mx_rows = jnp.maximum(cm_rows, m_in4)
        inter_rows = jnp.exp(m_in4 - mx_rows)
        einv_rows = jnp.exp(-(b_rows + mx_rows))
        fac_t = jnp.transpose(jnp.concatenate(
            [mx_rows, inter_rows, einv_rows, jnp.zeros_like(mx_rows)], axis=0))

        for h in range(MLSTM_HEADS):
            lo = h * HEAD_DIM
            st = bb * MLSTM_HEADS + h
            q = q_all[:, lo:lo + HEAD_DIM]
            kt = k_t[lo:lo + HEAD_DIM, :]
            v_ext = jnp.concatenate([v_ref[bb, :, lo:lo + HEAD_DIM], ones_blk], axis=1)
            mx_col = fac_t[:, h:h + 1]
            inter_col = fac_t[:, MLSTM_HEADS + h:MLSTM_HEADS + h + 1]
            einv_col = fac_t[:, 2 * MLSTM_HEADS + h:2 * MLSTM_HEADS + h + 1]
            c_row = c_rows[h:h + 1, :]
            b_tot = b_rows[h:h + 1, L - 1:L]
            cm_tot = cm_rows[h:h + 1, L - 1:L]
            m_in = m_ref[st][0:1, 0:1]
            cn = cn_ref[st]

            s_qk = jnp.dot(q, kt.astype(jnp.bfloat16), preferred_element_type=jnp.float32)
            s = (s_qk * jnp.exp(jnp.where(causal, c_row - mx_col, -jnp.inf))).astype(jnp.bfloat16)
            num = (jnp.dot(s, v_ext, preferred_element_type=jnp.float32)
                   + inter_col * jnp.dot(q, cn.astype(jnp.bfloat16),
                                         preferred_element_type=jnp.float32))
            den = num[:, HEAD_DIM:]
            hh = num[:, :HEAD_DIM] / jnp.maximum(jnp.abs(den), einv_col)

            mu = jnp.mean(hh, axis=-1, keepdims=True)
            dv = hh - mu
            var = jnp.mean(dv * dv, axis=-1, keepdims=True)
            hn = dv * lax.rsqrt(var + EPS) * hng_ref[:, lo:lo + HEAD_DIM]
            og = _sigmoid(o_ref[bb, :, lo:lo + HEAD_DIM].astype(jnp.float32))
            y_ref[bb, :, lo:lo + HEAD_DIM] = (og * hn).astype(y_ref.dtype)

            m_loc = b_tot + cm_tot
            kw_t = (kt * jnp.exp(c_row - cm_tot)).astype(jnp.bfloat16)
            c_loc = jnp.dot(kw_t, v_ext, preferred_element_type=jnp.float32)
            m_new = jnp.maximum(b_tot + m_in, m_loc)
            s_old = jnp.exp(b_tot + m_in - m_new)
            s_loc = jnp.exp(m_loc - m_new)
            cn_ref[st] = s_old * cn + s_loc * c_loc
            m_ref[st] = jnp.broadcast_to(m_new, m_ref.shape[1:])


def _mlstm(p3, gates_b, conv_w, gate_b, hn_g, tri, shifts, halo_shifts):
    batch, seq, _ = p3.shape
    L = CHUNK
    BB = MLSTM_BATCH
    halo_per_chunk = L // HALO
    return pl.pallas_call(
        _mlstm_kernel,
        grid=(batch // BB, seq // L),
        in_specs=[
            pl.BlockSpec((BB, L, 2 * MLSTM_WIDTH), lambda bi, ci: (bi, ci, 0)),
            pl.BlockSpec((BB, HALO, 2 * MLSTM_WIDTH),
                         lambda bi, ci: (bi, jnp.maximum(ci * halo_per_chunk - 1, 0), 0)),
            pl.BlockSpec((BB, L, MLSTM_WIDTH), lambda bi, ci: (bi, ci, 2)),
            pl.BlockSpec((BB, L, MLSTM_WIDTH), lambda bi, ci: (bi, ci, 3)),
            pl.BlockSpec((BB, N_GATES, L), lambda bi, ci: (bi, 0, ci)),
            pl.BlockSpec((CONV_WIDTH, 2 * MLSTM_WIDTH), lambda bi, ci: (0, 0)),
            pl.BlockSpec((N_GATES, 1), lambda bi, ci: (0, 0)),
            pl.BlockSpec((1, MLSTM_WIDTH), lambda bi, ci: (0, 0)),
            pl.BlockSpec((L, L), lambda bi, ci: (0, 0)),
            pl.BlockSpec((CONV_WIDTH - 1, L, L), lambda bi, ci: (0, 0, 0)),
            pl.BlockSpec((CONV_WIDTH - 1, 8, HALO), lambda bi, ci: (0, 0, 0)),
        ],
        out_specs=pl.BlockSpec((BB, L, MLSTM_WIDTH), lambda bi, ci: (bi, ci, 0)),
        out_shape=jax.ShapeDtypeStruct((batch, seq, MLSTM_WIDTH), jnp.bfloat16),
        scratch_shapes=[
            pltpu.VMEM((BB * MLSTM_HEADS, HEAD_DIM, 2 * HEAD_DIM), jnp.float32),
            pltpu.VMEM((BB * MLSTM_HEADS, 8, LANES), jnp.float32),
        ],
        compiler_params=pltpu.CompilerParams(
            dimension_semantics=("parallel", "arbitrary"), vmem_limit_bytes=VMEM_LIMIT),
        name="mlstm",
    )(p3, p3, p3, p3, gates_b, conv_w, gate_b, hn_g, tri, shifts, halo_shifts)


def _out_route_kernel(seq, x_ref, ym_ref, u_ref, up_ref, pw_ref, ps_ref, wo_ref, g2_ref,
                      wrt_ref, br_ref, before_ref, x1_ref, h2_ref, idx_ref, gate_ref, rank_ref, cnt_ref,
                      carry_ref):
    TM = TM_PROJ
    R = ROUTE_SUB * TM
    i = pl.program_id(0)

    @pl.when(i == 0)
    def _():
        carry_ref[...] = jnp.zeros_like(carry_ref)

    pos0 = (i * R) % seq
    e_id = lax.broadcasted_iota(jnp.int32, (N_EXPERTS, TM), 0).astype(jnp.float32)
    carry = carry_ref[...]
    subs = [slice(sub * TM, (sub + 1) * TM) for sub in range(ROUTE_SUB)]

    halo = jnp.where(pos0 > 0, up_ref[...].astype(jnp.float32), 0.0)
    u_ext = jnp.concatenate([halo, u_ref[...].astype(jnp.float32)], axis=0)
    win_sums = []
    for gi, w in enumerate(POOL_WINDOWS):
        sw = u_ext[:, gi * POOL_GROUP_DIM:(gi + 1) * POOL_GROUP_DIM]
        span = 1
        while span < w:
            sw = sw + pltpu.roll(sw, span, axis=0)
            span *= 2
        win_sums.append(sw)
    y_cats = []
    for sub, rows in enumerate(subs):
        r0 = sub * TM
        pos = (pos0 + r0 + lax.broadcasted_iota(jnp.int32, (TM, 1), 0) + 1).astype(jnp.float32)
        mixed = []
        for gi, w in enumerate(POOL_WINDOWS):
            lo = gi * POOL_GROUP_DIM
            tok = u_ext[HALO + r0:HALO + r0 + TM, lo:lo + POOL_GROUP_DIM]
            pooled = win_sums[gi][HALO + r0:HALO + r0 + TM] / jnp.minimum(pos, float(w)) - tok
            mg = jnp.dot(pooled.astype(jnp.bfloat16), pw_ref[gi],
                         preferred_element_type=jnp.float32)
            mixed.append((mg * ps_ref[:, lo:lo + POOL_GROUP_DIM]).astype(jnp.bfloat16))
        y_cats.append(jnp.concatenate([ym_ref[rows, :]] + mixed, axis=1))

    all_logits = []
    for sub, rows in enumerate(subs):
        r0 = sub * TM
        x1 = x_ref[rows, :] + jnp.dot(y_cats[sub], wo_ref[...], preferred_element_type=jnp.float32)
        x1_ref[rows, :] = x1
        h2 = x1 * lax.rsqrt(jnp.mean(x1 * x1, axis=-1, keepdims=True) + EPS) * g2_ref[...]
        h2b = h2.astype(jnp.bfloat16)
        h2w = _pack_bf16_pairs(h2)
        for s in range(PSLAB):
            h2_ref[pl.ds(r0 * PSLAB + s, TM, stride=PSLAB), :] = h2w[:, s * LANES:(s + 1) * LANES]
        all_logits.append(lax.dot_general(wrt_ref[...], h2b, NT_DIMS,
                                          preferred_element_type=jnp.float32) + br_ref[...])

    for sub, rows in enumerate(subs):
        work = all_logits[sub]
        vals, ids, hots = [], [], []
        for _ in range(TOP_K):
            mk = jnp.max(work, axis=0, keepdims=True)
            ik = jnp.min(jnp.where(work == mk, e_id, float(N_EXPERTS)), axis=0, keepdims=True)
            hot = e_id == ik
            work = jnp.where(hot, -jnp.inf, work)
            vals.append(mk)
            ids.append(ik)
            hots.append(hot)
        ex = [jnp.exp(vk - vals[0]) for vk in vals]
        denom = ex[0] + ex[1] + ex[2] + ex[3]
        gate_ref[:, rows] = jnp.concatenate([e / denom for e in ex], axis=0)
        idx_ref[:, rows] = jnp.concatenate(ids, axis=0).astype(jnp.int32)

        sel_f = sum(jnp.where(hot, 1.0, 0.0) for hot in hots)
        prefix = jnp.dot(sel_f.astype(jnp.bfloat16), before_ref[...],
                         preferred_element_type=jnp.float32)
        rank_e = carry[:, 0:1] + prefix
        ranks = [jnp.sum(jnp.where(hot, rank_e, 0.0), axis=0, keepdims=True) for hot in hots]
        rank_ref[:, rows] = jnp.concatenate(ranks, axis=0).astype(jnp.int32)
        carry = carry + jnp.sum(sel_f, axis=1, keepdims=True)
    carry_ref[...] = carry
    cnt_ref[...] = carry.astype(jnp.int32)


def _out_route(x2, ym, p, pool_w, pool_s, w_out, g2, wr_t, br, before, seq):
    T = x2.shape[0]
    TM = ROUTE_SUB * TM_PROJ
    nt = T // TM
    u_blk = N_MAIN // POOL_WIDTH - 1
    halo_per_tile = TM // HALO
    tok_spec = pl.BlockSpec((TOP_K, TM), lambda i: (0, i))
    return pl.pallas_call(
        functools.partial(_out_route_kernel, seq),
        grid=(nt,),
        in_specs=[
            pl.BlockSpec((TM, D_MODEL), lambda i: (i, 0)),
            pl.BlockSpec((TM, MLSTM_WIDTH), lambda i: (i, 0)),
            pl.BlockSpec((TM, POOL_WIDTH), lambda i: (i, u_blk)),
            pl.BlockSpec((HALO, POOL_WIDTH),
                         lambda i: (jnp.maximum(i * halo_per_tile - 1, 0), u_blk)),
            pl.BlockSpec((len(POOL_WINDOWS), POOL_GROUP_DIM, POOL_GROUP_DIM), lambda i: (0, 0, 0)),
            pl.BlockSpec((1, POOL_WIDTH), lambda i: (0, 0)),
            pl.BlockSpec((D_MODEL, D_MODEL), lambda i: (0, 0)),
            pl.BlockSpec((1, D_MODEL), lambda i: (0, 0)),
            pl.BlockSpec((N_EXPERTS, D_MODEL), lambda i: (0, 0)),
            pl.BlockSpec((N_EXPERTS, 1), lambda i: (0, 0)),
            pl.BlockSpec((TM_PROJ, TM_PROJ), lambda i: (0, 0)),
        ],
        out_specs=[
            pl.BlockSpec((TM, D_MODEL), lambda i: (i, 0)),
            pl.BlockSpec((TM * PSLAB, LANES), lambda i: (i, 0)),
            tok_spec, tok_spec, tok_spec,
            pl.BlockSpec((N_EXPERTS, LANES), lambda i: (0, 0)),
        ],
        out_shape=[
            jax.ShapeDtypeStruct((T, D_MODEL), jnp.float32),
            jax.ShapeDtypeStruct((T * PSLAB, LANES), jnp.uint32),
            jax.ShapeDtypeStruct((TOP_K, T), jnp.int32),
            jax.ShapeDtypeStruct((TOP_K, T), jnp.float32),
            jax.ShapeDtypeStruct((TOP_K, T), jnp.int32),
            jax.ShapeDtypeStruct((N_EXPERTS, LANES), jnp.int32),
        ],
        scratch_shapes=[
            pltpu.VMEM((N_EXPERTS, LANES), jnp.float32),
        ],
        compiler_params=pltpu.CompilerParams(
            dimension_semantics=("arbitrary",), vmem_limit_bytes=VMEM_LIMIT),
        name="out_route",
    )(x2, ym, p, p, pool_w, pool_s, w_out, g2, wr_t, br, before)


def _plan(dest_flat, fill):
    n_assign = dest_flat.shape[0]
    n_table = fill.shape[0]
    mesh = plsc.VectorSubcoreMesh(core_axis_name="c", subcore_axis_name="s")

    @pl.kernel(out_type=jax.ShapeDtypeStruct((n_table,), jnp.int32), mesh=mesh,
               scratch_types=[pltpu.VMEM((n_table,), jnp.int32),
                              pltpu.VMEM((PLAN_CHUNK,), jnp.int32)],
               compiler_params=pltpu.CompilerParams(needs_layout_passes=False))
    def plan_kernel(dest_hbm, fill_hbm, out_hbm, table, chunk):
        first = jnp.logical_and(lax.axis_index("c") == 0, lax.axis_index("s") == 0)

        @pl.when(first)
        def _():
            pltpu.sync_copy(fill_hbm, table)

            @pl.loop(0, n_assign // PLAN_CHUNK)
            def _(ci):
                pltpu.sync_copy(dest_hbm.at[pl.ds(ci * PLAN_CHUNK, PLAN_CHUNK)], chunk)

                @pl.loop(0, PLAN_CHUNK // (SC_LANES * PLAN_UNROLL))
                def _(i):
                    for j in range(PLAN_UNROLL):
                        off = (i * PLAN_UNROLL + j) * SC_LANES
                        idx = chunk[pl.ds(off, SC_LANES)]
                        vals = (ci * PLAN_CHUNK + off
                                + lax.broadcasted_iota(jnp.int32, (SC_LANES,), 0))
                        plsc.store_scatter(table, [idx], vals)

            pltpu.sync_copy(table, out_hbm)

    return plan_kernel(dest_flat, fill)


def _expert_kernel(n_tok, bs_ref, slot_ref, h2_ref, wg_ref, bg_ref, wu_ref, bu_ref, wd_ref, bd_ref,
                   yt_ref, *scratch):
    TM = TM_EXPERT
    ROWS = TM * PSLAB
    e = pl.program_id(0)
    n_total = bs_ref[N_EXPERTS]
    xg = scratch[:NBUF]
    ys = scratch[NBUF:2 * NBUF]
    wgb_ref, wub_ref, wdb_ref, gsem, ssem = scratch[2 * NBUF:]

    def token_of(a):
        return a & (n_tok - 1) if n_tok & (n_tok - 1) == 0 else lax.rem(a, n_tok)

    def gather_row(base, par, r):
        t = token_of(slot_ref[base + r])
        pltpu.make_async_copy(h2_ref.at[pl.ds(pl.multiple_of(t * PSLAB, PSLAB), PSLAB), :],
                              xg[par].at[pl.ds(pl.multiple_of(r * PSLAB, PSLAB), PSLAB), :],
                              gsem.at[par]).start()

    def start_gather(blk, par, unrolled=True):
        base = (blk + 1) * TM
        if unrolled:
            for r in range(TM):
                gather_row(base, par, r)
        else:
            lax.fori_loop(0, TM, lambda r, c: (gather_row(base, par, r), c)[1], 0)

    def wait_gather(par):
        pltpu.make_async_copy(h2_ref.at[pl.ds(0, ROWS), :], xg[0], gsem.at[par]).wait()

    def scatter_row(base, par, r):
        a = slot_ref[base + r]
        pltpu.make_async_copy(ys[par].at[pl.ds(pl.multiple_of(r * PSLAB, PSLAB), PSLAB), :],
                              yt_ref.at[pl.ds(pl.multiple_of(a * PSLAB, PSLAB), PSLAB), :],
                              ssem.at[par]).start()

    def start_scatter(blk, par, unrolled=True):
        base = (blk + 1) * TM
        if unrolled:
            for r in range(TM):
                scatter_row(base, par, r)
        else:
            lax.fori_loop(0, TM, lambda r, c: (scatter_row(base, par, r), c)[1], 0)

    def wait_scatter(par):
        pltpu.make_async_copy(ys[0], yt_ref.at[pl.ds(0, ROWS), :], ssem.at[par]).wait()

    @pl.when(e == 0)
    def _():
        for blk in range(NBUF - 1):
            start_gather(blk, blk, unrolled=False)
        for par in range(NBUF):
            ys[par][...] = jnp.zeros_like(ys[par])
            dump = yt_ref.at[pl.ds((n_tok * TOP_K + par * TM) * PSLAB, ROWS), :]
            cp = pltpu.make_async_copy(ys[par], dump, ssem.at[par])
            cp.start()
            cp.wait()

    wgb_ref[...] = wg_ref[0].astype(jnp.bfloat16)
    wub_ref[...] = wu_ref[0].astype(jnp.bfloat16)
    wdb_ref[...] = wd_ref[0].astype(jnp.bfloat16)

    def block_step(g, par):
        prv = (par + NBUF - 1) % NBUF
        wait_gather(par)

        @pl.when(g >= NBUF - 1)
        def _():
            wait_scatter(par)

        start_gather(g + NBUF - 1, prv)
        start_scatter(g - 1, prv)
        words = [xg[par][pl.ds(s, TM, stride=PSLAB), :] for s in range(PSLAB)]
        x = jnp.concatenate([_unpack_lo(w).astype(jnp.bfloat16) for w in words]
                            + [_unpack_hi(w).astype(jnp.bfloat16) for w in words], axis=1)
        gate = jnp.dot(x, wgb_ref[...], preferred_element_type=jnp.float32) + bg_ref[0]
        up = jnp.dot(x, wub_ref[...], preferred_element_type=jnp.float32) + bu_ref[0]
        gate = jnp.minimum(gate, SWIGLU_LIMIT)
        up = jnp.clip(up, -SWIGLU_LIMIT, SWIGLU_LIMIT)
        glu = gate * _sigmoid(SWIGLU_ALPHA * gate)
        act = (glu * (up + 1.0)).astype(jnp.bfloat16)
        y = jnp.dot(act, wdb_ref[...], preferred_element_type=jnp.float32) + bd_ref[0]
        packed = _pack_bf16_pairs(y)
        for s in range(PSLAB):
            ys[par][pl.ds(s, TM, stride=PSLAB), :] = packed[:, s * LANES:(s + 1) * LANES]

    def body(g, carry):
        for par in range(NBUF):
            pl.when(g % NBUF == par)(functools.partial(block_step, g, par))
        return carry

    lax.fori_loop(bs_ref[e], bs_ref[e + 1], body, 0)

    @pl.when(e == N_EXPERTS - 1)
    def _():
        g = n_total
        for par in range(NBUF):
            @pl.when((g - 1) % NBUF == par)
            def _():
                start_scatter(g - 1, par, unrolled=False)
        for j in range(NBUF - 1):
            wait_gather((g + j) % NBUF)
        wait_scatter((g - 1) % NBUF)
        for j in range(2, NBUF + 1):
            @pl.when(g >= j - 1)
            def _():
                wait_scatter((g + NBUF - j) % NBUF)


def _experts(block_start, slot_buf, h2_slab, w_gate, b_gate, w_up, b_up, w_down, b_down, n_tok):
    TM = TM_EXPERT
    n_assign = n_tok * TOP_K
    w_spec = pl.BlockSpec((1, D_MODEL, D_FF), lambda e, bs, sl: (e, 0, 0))
    bias_spec = pl.BlockSpec((1, 1, D_FF), lambda e, bs, sl: (e, 0, 0))
    buf = pltpu.VMEM((TM * PSLAB, LANES), jnp.uint32)
    grid_spec = pltpu.PrefetchScalarGridSpec(
        num_scalar_prefetch=2,
        grid=(N_EXPERTS,),
        in_specs=[
            pl.BlockSpec(memory_space=pl.ANY),
            w_spec, bias_spec, w_spec, bias_spec, w_spec, bias_spec,
        ],
        out_specs=pl.BlockSpec(memory_space=pl.ANY),
        scratch_shapes=[
            *([buf] * (2 * NBUF)),
            pltpu.VMEM((D_MODEL, D_FF), jnp.bfloat16),
            pltpu.VMEM((D_MODEL, D_FF), jnp.bfloat16),
            pltpu.VMEM((D_FF, D_MODEL), jnp.bfloat16),
            pltpu.SemaphoreType.DMA((NBUF,)),
            pltpu.SemaphoreType.DMA((NBUF,)),
        ],
    )
    return pl.pallas_call(
        functools.partial(_expert_kernel, n_tok),
        grid_spec=grid_spec,
        out_shape=jax.ShapeDtypeStruct(((n_assign + NBUF * TM) * PSLAB, LANES), jnp.uint32),
        compiler_params=pltpu.CompilerParams(
            dimension_semantics=("arbitrary",), vmem_limit_bytes=VMEM_LIMIT),
        name="experts",
    )(block_start, slot_buf, h2_slab, w_gate, b_gate, w_up, b_up, w_down, b_down)


def _combine_kernel(normalize, x1_ref, y0_ref, y1_ref, y2_ref, y3_ref, gate_ref, g_ref, o_ref):
    TM = TM_COMBINE
    gates = jnp.concatenate([gate_ref[...], jnp.zeros((8 - TOP_K, TM), jnp.float32)], axis=0)
    g_cols = jnp.transpose(gates)
    g_bc = [jnp.broadcast_to(g_cols[:, k:k + 1], (TM, LANES)) for k in range(TOP_K)]
    ssq = jnp.zeros((TM, LANES), jnp.float32)
    parts = [x1_ref[:, s * LANES:(s + 1) * LANES] for s in range(SLAB)]
    for s in range(PSLAB):
        for k, y_ref in enumerate((y0_ref, y1_ref, y2_ref, y3_ref)):
            w = y_ref[pl.ds(s, TM, stride=PSLAB), :]
            parts[s] = parts[s] + g_bc[k] * _unpack_lo(w)
            parts[PSLAB + s] = parts[PSLAB + s] + g_bc[k] * _unpack_hi(w)
    for acc in parts:
        ssq = ssq + acc * acc
    if normalize:
        inv = lax.rsqrt(jnp.sum(ssq, axis=-1, keepdims=True) * (1.0 / D_MODEL) + EPS)
        for s in range(SLAB):
            o_ref[:, s * LANES:(s + 1) * LANES] = parts[s] * inv * g_ref[:, s * LANES:(s + 1) * LANES]
    else:
        for s in range(SLAB):
            o_ref[:, s * LANES:(s + 1) * LANES] = parts[s]


def _combine(x1, y_tok, gate_t, gf, normalize):
    T = x1.shape[0]
    TM = TM_COMBINE
    nt = T // TM

    def y_spec(k):
        return pl.BlockSpec((TM * PSLAB, LANES), lambda i: (k * nt + i, 0))

    return pl.pallas_call(
        functools.partial(_combine_kernel, normalize),
        grid=(nt,),
        in_specs=[
            pl.BlockSpec((TM, D_MODEL), lambda i: (i, 0)),
            y_spec(0), y_spec(1), y_spec(2), y_spec(3),
            pl.BlockSpec((TOP_K, TM), lambda i: (0, i)),
            pl.BlockSpec((1, D_MODEL), lambda i: (0, 0)),
        ],
        out_specs=pl.BlockSpec((TM, D_MODEL), lambda i: (i, 0)),
        out_shape=jax.ShapeDtypeStruct((T, D_MODEL), jnp.float32),
        compiler_params=pltpu.CompilerParams(
            dimension_semantics=("parallel",), vmem_limit_bytes=VMEM_LIMIT),
        name="combine",
    )(x1, y_tok, y_tok, y_tok, y_tok, gate_t, gf)


def kernel(x, norm1_g, w_in, ig_b, fg_b, conv_w, head_norm_g, pool_w, pool_scale, w_out, norm2_g,
           w_router, b_router, w_gate, b_gate, w_up, b_up, w_down, b_down, normf_g):
    B, S, D = x.shape
    T = B * S
    depth = norm1_g.shape[0]
    W = MLSTM_WIDTH
    f32, bf16 = jnp.float32, jnp.bfloat16

    L = CHUNK
    t_l = lax.broadcasted_iota(jnp.int32, (L, L), 0)
    t_r = lax.broadcasted_iota(jnp.int32, (L, L), 1)
    tri = (t_r <= t_l).astype(f32)
    shifts = jnp.stack([(t_l - t_r == CONV_WIDTH - 1 - j).astype(bf16)
                        for j in range(CONV_WIDTH - 1)])
    h_t = lax.broadcasted_iota(jnp.int32, (8, HALO), 0)
    h_r = lax.broadcasted_iota(jnp.int32, (8, HALO), 1)
    halo_shifts = jnp.stack([(h_r - HALO - h_t == -(CONV_WIDTH - 1 - j)).astype(bf16)
                             for j in range(CONV_WIDTH - 1)])

    t_a = lax.broadcasted_iota(jnp.int32, (TM_PROJ, TM_PROJ), 0)
    t_b = lax.broadcasted_iota(jnp.int32, (TM_PROJ, TM_PROJ), 1)
    before = (t_a < t_b).astype(bf16)

    n_assign = T * TOP_K
    n_blocks = -(-n_assign // TM_EXPERT) + N_EXPERTS
    n_rows = n_blocks * TM_EXPERT
    n_table = n_rows + NBUF * TM_EXPERT
    fill = n_assign + ((jnp.arange(n_table, dtype=jnp.int32) + (NBUF - 1) * TM_EXPERT)
                       % (NBUF * TM_EXPERT))
    x2 = x.reshape(T, D)
    for l in range(depth):
        w = w_in[l]
        w_a = w[:, :4 * W].astype(bf16)
        w_u = w[:, 4 * W + N_GATES:].astype(bf16)
        wg_t = jnp.zeros((BF16_SUBLANES, D), bf16).at[:N_GATES].set(
            w[:, 4 * W:4 * W + N_GATES].T.astype(bf16))
        p, gates_t = _in_proj(x2, norm1_g[l][None, :], w_a, w_u, wg_t)

        gate_b = jnp.concatenate([ig_b[l], fg_b[l]])[:, None].astype(f32)
        gates_b = gates_t.reshape(N_GATES, B, S).transpose(1, 0, 2)
        ym = _mlstm(p.reshape(B, S, N_MAIN), gates_b, conv_w[l].astype(f32), gate_b,
                    head_norm_g[l][None, :], tri, shifts, halo_shifts).reshape(T, W)

        x1, h2, idx_t, gate_t, rank_t, cnt = _out_route(
            x2, ym, p, pool_w[l].astype(bf16), pool_scale[l][None, :], w_out[l].astype(bf16),
            norm2_g[l][None, :], w_router[l].T.astype(bf16), b_router[l][:, None], before, S)

        counts = cnt[:, 0]
        padded = ((counts + TM_EXPERT - 1) // TM_EXPERT) * TM_EXPERT
        padded_end = jnp.cumsum(padded)
        padded_start = padded_end - padded
        expert_ids = jnp.arange(N_EXPERTS, dtype=jnp.int32)[:, None, None]
        start_of = jnp.sum(jnp.where(idx_t[None] == expert_ids, padded_start[:, None, None], 0), axis=0)
        dest = start_of + rank_t
        block_start = jnp.concatenate(
            [jnp.zeros((1,), jnp.int32), (padded_end // TM_EXPERT).astype(jnp.int32)])

        slot_buf = _plan(dest.reshape(-1) + TM_EXPERT, fill)
        y_tok = _experts(block_start, slot_buf, h2, w_gate[l], b_gate[l][:, None, :],
                         w_up[l], b_up[l][:, None, :], w_down[l], b_down[l][:, None, :], T)
        last = l + 1 == depth
        x2 = _combine(x1, y_tok, gate_t, normf_g[None, :], last)
    return x2.reshape(B, S, D)
```

```python
import functools

import jax
import jax.numpy as jnp
from jax import lax
from jax.experimental import pallas as pl
from jax.experimental.pallas import tpu as pltpu
from jax.experimental.pallas import tpu_sc as plsc

D_MODEL = 1024
MLSTM_WIDTH = 512
MLSTM_HEADS = 4
HEAD_DIM = 128
CONV_WIDTH = 4
POOL_WIDTH = 512
POOL_WINDOWS = (2, 4, 8, 16)
POOL_GROUP_DIM = 128
N_EXPERTS = 32
TOP_K = 4
D_FF = 1024
SWIGLU_LIMIT = 7.0
SWIGLU_ALPHA = 1.702
EPS = 1e-5

N_MAIN = 4 * MLSTM_WIDTH + POOL_WIDTH
N_GATES = 2 * MLSTM_HEADS

LANES = 128
BF16_SUBLANES = 16
VMEM_LIMIT = 56 * 1024 * 1024

TM_PROJ = 512
PROJ_SUB = 4
TM_COMBINE = 1024
ROUTE_SUB = 4
CHUNK = 256
MLSTM_BATCH = 4
HALO = 16
TM_EXPERT = 512
NBUF = 3
SLAB = D_MODEL // LANES
PSLAB = SLAB // 2
PLAN_CHUNK = 32768
SC_LANES = 16
PLAN_UNROLL = 8

NT_DIMS = (((1,), (1,)), ((), ()))


def _sigmoid(x):
    return 1.0 / (1.0 + jnp.exp(-x))


def _pack_bf16_pairs(v):
    half = v.shape[1] // 2
    lo = pltpu.bitcast(v[:, :half].astype(jnp.bfloat16).astype(jnp.float32), jnp.uint32)
    hi = pltpu.bitcast(v[:, half:].astype(jnp.bfloat16).astype(jnp.float32), jnp.uint32)
    return (lo >> 16) | (hi & jnp.uint32(0xFFFF0000))


def _unpack_lo(w):
    return pltpu.bitcast(w << 16, jnp.float32)


def _unpack_hi(w):
    return pltpu.bitcast(w & jnp.uint32(0xFFFF0000), jnp.float32)


def _in_proj_kernel(x_ref, g_ref, wa_ref, wu_ref, wgt_ref, p_ref, gt_ref):
    n_a = wa_ref.shape[1]
    for sub in range(PROJ_SUB):
        rows = slice(sub * TM_PROJ, (sub + 1) * TM_PROJ)
        x = x_ref[rows, :]
        h = x * lax.rsqrt(jnp.mean(x * x, axis=-1, keepdims=True) + EPS) * g_ref[...]
        hb = h.astype(jnp.bfloat16)
        p_ref[rows, :n_a] = jnp.dot(hb, wa_ref[...],
                                    preferred_element_type=jnp.float32).astype(p_ref.dtype)
        p_ref[rows, n_a:] = jnp.dot(hb, wu_ref[...],
                                    preferred_element_type=jnp.float32).astype(p_ref.dtype)
        gt = lax.dot_general(wgt_ref[...], hb, NT_DIMS, preferred_element_type=jnp.float32)
        gt_ref[:, rows] = gt[:N_GATES]


def _in_proj(x2, g1, w_a, w_u, wg_t):
    T = x2.shape[0]
    return pl.pallas_call(
        _in_proj_kernel,
        grid=(T // (PROJ_SUB * TM_PROJ),),
        in_specs=[
            pl.BlockSpec((PROJ_SUB * TM_PROJ, D_MODEL), lambda i: (i, 0)),
            pl.BlockSpec((1, D_MODEL), lambda i: (0, 0)),
            pl.BlockSpec(w_a.shape, lambda i: (0, 0)),
            pl.BlockSpec(w_u.shape, lambda i: (0, 0)),
            pl.BlockSpec((BF16_SUBLANES, D_MODEL), lambda i: (0, 0)),
        ],
        out_specs=[
            pl.BlockSpec((PROJ_SUB * TM_PROJ, N_MAIN), lambda i: (i, 0)),
            pl.BlockSpec((N_GATES, PROJ_SUB * TM_PROJ), lambda i: (0, i)),
        ],
        out_shape=[
            jax.ShapeDtypeStruct((T, N_MAIN), jnp.bfloat16),
            jax.ShapeDtypeStruct((N_GATES, T), jnp.float32),
        ],
        compiler_params=pltpu.CompilerParams(
            dimension_semantics=("parallel",), vmem_limit_bytes=VMEM_LIMIT),
        name="in_proj",
    )(x2, g1, w_a, w_u, wg_t)


def _mlstm_kernel(qk_ref, qkp_ref, v_ref, o_ref, gt_ref, convw_ref, gb_ref, hng_ref,
                  tri_ref, shift_ref, hshift_ref, y_ref, cn_ref, m_ref):
    L = CHUNK
    c = pl.program_id(1)

    @pl.when(c == 0)
    def _():
        cn_ref[...] = jnp.zeros_like(cn_ref)
        m_ref[...] = jnp.zeros_like(m_ref)

    row_id = lax.broadcasted_iota(jnp.int32, (L, L), 0)
    col_id = lax.broadcasted_iota(jnp.int32, (L, L), 1)
    causal = col_id <= row_id
    ones_blk = jnp.ones((L, HEAD_DIM), jnp.bfloat16)
    lane = lax.broadcasted_iota(jnp.int32, (MLSTM_HEADS, L), 1)

    gate_terms = []
    for bb in range(MLSTM_BATCH):
        gt = gt_ref[bb] + gb_ref[...]
        f = gt[MLSTM_HEADS:]
        lf = jnp.minimum(f, 0.0) - jnp.log(1.0 + jnp.exp(-jnp.abs(f)))
        ig = gt[:MLSTM_HEADS]
        b_rows = lax.dot_general(lf, tri_ref[...], NT_DIMS, precision=lax.Precision.HIGHEST,
                                 preferred_element_type=jnp.float32)
        c_rows = ig - b_rows
        cm_rows = c_rows
        d = 1
        while d < L:
            cm_rows = jnp.maximum(
                cm_rows, jnp.where(lane >= d, pltpu.roll(cm_rows, d, axis=1), -jnp.inf))
            d *= 2
        gate_terms.append((b_rows, c_rows, cm_rows))

    conv_terms = []
    for bb in range(MLSTM_BATCH):
        x_cur = qk_ref[bb]
        x_prev = jnp.where(c > 0, qkp_ref[bb], jnp.zeros((HALO, 2 * MLSTM_WIDTH), jnp.bfloat16))
        acc = convw_ref[CONV_WIDTH - 1:CONV_WIDTH, :] * x_cur.astype(jnp.float32)
        for j in range(CONV_WIDTH - 1):
            sh = jnp.dot(shift_ref[j], x_cur, preferred_element_type=jnp.float32)
            top = sh[:8] + jnp.dot(hshift_ref[j], x_prev, preferred_element_type=jnp.float32)
            sh = jnp.concatenate([top, sh[8:]], axis=0)
            acc = acc + convw_ref[j:j + 1, :] * sh
        qk = acc * _sigmoid(acc)
        q_all = qk[:, :MLSTM_WIDTH].astype(jnp.bfloat16)
        k_t = jnp.transpose(qk[:, MLSTM_WIDTH:] * (HEAD_DIM ** -0.5))
        conv_terms.append((q_all, k_t))

    for bb in range(MLSTM_BATCH):
        b_rows, c_rows, cm_rows = gate_terms[bb]
        q_all, k_t = conv_terms[bb]
        m_in4 = jnp.concatenate(
            [m_ref[bb * MLSTM_HEADS + h][0:1, 0:1] for h in range(MLSTM_HEADS)], axis=0)
        mx_rows = jnp.maximum(cm_rows, m_in4)
        inter_rows = jnp.exp(m_in4 - mx_rows)
        einv_rows = jnp.exp(-(b_rows + mx_rows))
        fac_t = jnp.transpose(jnp.concatenate(
            [mx_rows, inter_rows, einv_rows, jnp.zeros_like(mx_rows)], axis=0))

        for h in range(MLSTM_HEADS):
            lo = h * HEAD_DIM
            st = bb * MLSTM_HEADS + h
            q = q_all[:, lo:lo + HEAD_DIM]
            kt = k_t[lo:lo + HEAD_DIM, :]
            v_ext = jnp.concatenate([v_ref[bb, :, lo:lo + HEAD_DIM], ones_blk], axis=1)
            mx_col = fac_t[:, h:h + 1]
            inter_col = fac_t[:, MLSTM_HEADS + h:MLSTM_HEADS + h + 1]
            einv_col = fac_t[:, 2 * MLSTM_HEADS + h:2 * MLSTM_HEADS + h + 1]
            c_row = c_rows[h:h + 1, :]
            b_tot = b_rows[h:h + 1, L - 1:L]
            cm_tot = cm_rows[h:h + 1, L - 1:L]
            m_in = m_ref[st][0:1, 0:1]
            cn = cn_ref[st]

            s_qk = jnp.dot(q, kt.astype(jnp.bfloat16), preferred_element_type=jnp.float32)
            s = (s_qk * jnp.exp(jnp.where(causal, c_row - mx_col, -jnp.inf))).astype(jnp.bfloat16)
            num = (jnp.dot(s, v_ext, preferred_element_type=jnp.float32)
                   + inter_col * jnp.dot(q, cn.astype(jnp.bfloat16),
                                         preferred_element_type=jnp.float32))
            den = num[:, HEAD_DIM:]
            hh = num[:, :HEAD_DIM] / jnp.maximum(jnp.abs(den), einv_col)

            mu = jnp.mean(hh, axis=-1, keepdims=True)
            dv = hh - mu
            var = jnp.mean(dv * dv, axis=-1, keepdims=True)
            hn = dv * lax.rsqrt(var + EPS) * hng_ref[:, lo:lo + HEAD_DIM]
            og = _sigmoid(o_ref[bb, :, lo:lo + HEAD_DIM].astype(jnp.float32))
            y_ref[bb, :, lo:lo + HEAD_DIM] = (og * hn).astype(y_ref.dtype)

            m_loc = b_tot + cm_tot
            kw_t = (kt * jnp.exp(c_row - cm_tot)).astype(jnp.bfloat16)
            c_loc = jnp.dot(kw_t, v_ext, preferred_element_type=jnp.float32)
            m_new = jnp.maximum(b_tot + m_in, m_loc)
            s_old = jnp.exp(b_tot + m_in - m_new)
            s_loc = jnp.exp(m_loc - m_new)
            cn_ref[st] = s_old * cn + s_loc * c_loc
            m_ref[st] = jnp.broadcast_to(m_new, m_ref.shape[1:])


def _mlstm(p3, gates_b, conv_w, gate_b, hn_g, tri, shifts, halo_shifts):
    batch, seq, _ = p3.shape
    L = CHUNK
    BB = MLSTM_BATCH
    halo_per_chunk = L // HALO
    return pl.pallas_call(
        _mlstm_kernel,
        grid=(batch // BB, seq // L),
        in_specs=[
            pl.BlockSpec((BB, L, 2 * MLSTM_WIDTH), lambda bi, ci: (bi, ci, 0)),
            pl.BlockSpec((BB, HALO, 2 * MLSTM_WIDTH),
                         lambda bi, ci: (bi, jnp.maximum(ci * halo_per_chunk - 1, 0), 0)),
            pl.BlockSpec((BB, L, MLSTM_WIDTH), lambda bi, ci: (bi, ci, 2)),
            pl.BlockSpec((BB, L, MLSTM_WIDTH), lambda bi, ci: (bi, ci, 3)),
            pl.BlockSpec((BB, N_GATES, L), lambda bi, ci: (bi, 0, ci)),
            pl.BlockSpec((CONV_WIDTH, 2 * MLSTM_WIDTH), lambda bi, ci: (0, 0)),
            pl.BlockSpec((N_GATES, 1), lambda bi, ci: (0, 0)),
            pl.BlockSpec((1, MLSTM_WIDTH), lambda bi, ci: (0, 0)),
            pl.BlockSpec((L, L), lambda bi, ci: (0, 0)),
            pl.BlockSpec((CONV_WIDTH - 1, L, L), lambda bi, ci: (0, 0, 0)),
            pl.BlockSpec((CONV_WIDTH - 1, 8, HALO), lambda bi, ci: (0, 0, 0)),
        ],
        out_specs=pl.BlockSpec((BB, L, MLSTM_WIDTH), lambda bi, ci: (bi, ci, 0)),
        out_shape=jax.ShapeDtypeStruct((batch, seq, MLSTM_WIDTH), jnp.bfloat16),
        scratch_shapes=[
            pltpu.VMEM((BB * MLSTM_HEADS, HEAD_DIM, 2 * HEAD_DIM), jnp.float32),
            pltpu.VMEM((BB * MLSTM_HEADS, 8, LANES), jnp.float32),
        ],
        compiler_params=pltpu.CompilerParams(
            dimension_semantics=("parallel", "arbitrary"), vmem_limit_bytes=VMEM_LIMIT),
        name="mlstm",
    )(p3, p3, p3, p3, gates_b, conv_w, gate_b, hn_g, tri, shifts, halo_shifts)


def _out_route_kernel(seq, x_ref, ym_ref, u_ref, up_ref, pw_ref, ps_ref, wo_ref, g2_ref,
                      wrt_ref, br_ref, before_ref, x1_ref, h2_ref, idx_ref, gate_ref, rank_ref, cnt_ref,
                      carry_ref):
    TM = TM_PROJ
    R = ROUTE_SUB * TM
    i = pl.program_id(0)

    @pl.when(i == 0)
    def _():
        carry_ref[...] = jnp.zeros_like(carry_ref)

    pos0 = (i * R) % seq
    e_id = lax.broadcasted_iota(jnp.int32, (N_EXPERTS, TM), 0).astype(jnp.float32)
    carry = carry_ref[...]
    subs = [slice(sub * TM, (sub + 1) * TM) for sub in range(ROUTE_SUB)]

    halo = jnp.where(pos0 > 0, up_ref[...].astype(jnp.float32), 0.0)
    u_ext = jnp.concatenate([halo, u_ref[...].astype(jnp.float32)], axis=0)
    win_sums = []
    for gi, w in enumerate(POOL_WINDOWS):
        sw = u_ext[:, gi * POOL_GROUP_DIM:(gi + 1) * POOL_GROUP_DIM]
        span = 1
        while span < w:
            sw = sw + pltpu.roll(sw, span, axis=0)
            span *= 2
        win_sums.append(sw)
    y_cats = []
    for sub, rows in enumerate(subs):
        r0 = sub * TM
        pos = (pos0 + r0 + lax.broadcasted_iota(jnp.int32, (TM, 1), 0) + 1).astype(jnp.float32)
        mixed = []
        for gi, w in enumerate(POOL_WINDOWS):
            lo = gi * POOL_GROUP_DIM
            tok = u_ext[HALO + r0:HALO + r0 + TM, lo:lo + POOL_GROUP_DIM]
            pooled = win_sums[gi][HALO + r0:HALO + r0 + TM] / jnp.minimum(pos, float(w)) - tok
            mg = jnp.dot(pooled.astype(jnp.bfloat16), pw_ref[gi],
                         preferred_element_type=jnp.float32)
            mixed.append((mg * ps_ref[:, lo:lo + POOL_GROUP_DIM]).astype(jnp.bfloat16))
        y_cats.append(jnp.concatenate([ym_ref[rows, :]] + mixed, axis=1))

    all_logits = []
    for sub, rows in enumerate(subs):
        r0 = sub * TM
        x1 = x_ref[rows, :] + jnp.dot(y_cats[sub], wo_ref[...], preferred_element_type=jnp.float32)
        x1_ref[rows, :] = x1
        h2 = x1 * lax.rsqrt(jnp.mean(x1 * x1, axis=-1, keepdims=True) + EPS) * g2_ref[...]
        h2b = h2.astype(jnp.bfloat16)
        h2w = _pack_bf16_pairs(h2)
        for s in range(PSLAB):
            h2_ref[pl.ds(r0 * PSLAB + s, TM, stride=PSLAB), :] = h2w[:, s * LANES:(s + 1) * LANES]
        all_logits.append(lax.dot_general(wrt_ref[...], h2b, NT_DIMS,
                                          preferred_element_type=jnp.float32) + br_ref[...])

    for sub, rows in enumerate(subs):
        work = all_logits[sub]
        vals, ids, hots = [], [], []
        for _ in range(TOP_K):
            mk = jnp.max(work, axis=0, keepdims=True)
            ik = jnp.min(jnp.where(work == mk, e_id, float(N_EXPERTS)), axis=0, keepdims=True)
            hot = e_id == ik
            work = jnp.where(hot, -jnp.inf, work)
            vals.append(mk)
            ids.append(ik)
            hots.append(hot)
        ex = [jnp.exp(vk - vals[0]) for vk in vals]
        denom = ex[0] + ex[1] + ex[2] + ex[3]
        gate_ref[:, rows] = jnp.concatenate([e / denom for e in ex], axis=0)
        idx_ref[:, rows] = jnp.concatenate(ids, axis=0).astype(jnp.int32)

        sel_f = sum(jnp.where(hot, 1.0, 0.0) for hot in hots)
        prefix = jnp.dot(sel_f.astype(jnp.bfloat16), before_ref[...],
                         preferred_element_type=jnp.float32)
        rank_e = carry[:, 0:1] + prefix
        ranks = [jnp.sum(jnp.where(hot, rank_e, 0.0), axis=0, keepdims=True) for hot in hots]
        rank_ref[:, rows] = jnp.concatenate(ranks, axis=0).astype(jnp.int32)
        carry = carry + jnp.sum(sel_f, axis=1, keepdims=True)
    carry_ref[...] = carry
    cnt_ref[...] = carry.astype(jnp.int32)


def _out_route(x2, ym, p, pool_w, pool_s, w_out, g2, wr_t, br, before, seq):
    T = x2.shape[0]
    TM = ROUTE_SUB * TM_PROJ
    nt = T // TM
    u_blk = N_MAIN // POOL_WIDTH - 1
    halo_per_tile = TM // HALO
    tok_spec = pl.BlockSpec((TOP_K, TM), lambda i: (0, i))
    return pl.pallas_call(
        functools.partial(_out_route_kernel, seq),
        grid=(nt,),
        in_specs=[
            pl.BlockSpec((TM, D_MODEL), lambda i: (i, 0)),
            pl.BlockSpec((TM, MLSTM_WIDTH), lambda i: (i, 0)),
            pl.BlockSpec((TM, POOL_WIDTH), lambda i: (i, u_blk)),
            pl.BlockSpec((HALO, POOL_WIDTH),
                         lambda i: (jnp.maximum(i * halo_per_tile - 1, 0), u_blk)),
            pl.BlockSpec((len(POOL_WINDOWS), POOL_GROUP_DIM, POOL_GROUP_DIM), lambda i: (0, 0, 0)),
            pl.BlockSpec((1, POOL_WIDTH), lambda i: (0, 0)),
            pl.BlockSpec((D_MODEL, D_MODEL), lambda i: (0, 0)),
            pl.BlockSpec((1, D_MODEL), lambda i: (0, 0)),
            pl.BlockSpec((N_EXPERTS, D_MODEL), lambda i: (0, 0)),
            pl.BlockSpec((N_EXPERTS, 1), lambda i: (0, 0)),
            pl.BlockSpec((TM_PROJ, TM_PROJ), lambda i: (0, 0)),
        ],
        out_specs=[
            pl.BlockSpec((TM, D_MODEL), lambda i: (i, 0)),
            pl.BlockSpec((TM * PSLAB, LANES), lambda i: (i, 0)),
            tok_spec, tok_spec, tok_spec,
            pl.BlockSpec((N_EXPERTS, LANES), lambda i: (0, 0)),
        ],
        out_shape=[
            jax.ShapeDtypeStruct((T, D_MODEL), jnp.float32),
            jax.ShapeDtypeStruct((T * PSLAB, LANES), jnp.uint32),
            jax.ShapeDtypeStruct((TOP_K, T), jnp.int32),
            jax.ShapeDtypeStruct((TOP_K, T), jnp.float32),
            jax.ShapeDtypeStruct((TOP_K, T), jnp.int32),
            jax.ShapeDtypeStruct((N_EXPERTS, LANES), jnp.int32),
        ],
        scratch_shapes=[
            pltpu.VMEM((N_EXPERTS, LANES), jnp.float32),
        ],
        compiler_params=pltpu.CompilerParams(
            dimension_semantics=("arbitrary",), vmem_limit_bytes=VMEM_LIMIT),
        name="out_route",
    )(x2, ym, p, p, pool_w, pool_s, w_out, g2, wr_t, br, before)


def _plan(dest_flat, fill):
    n_assign = dest_flat.shape[0]
    n_table = fill.shape[0]
    mesh = plsc.VectorSubcoreMesh(core_axis_name="c", subcore_axis_name="s")

    @pl.kernel(out_type=jax.ShapeDtypeStruct((n_table,), jnp.int32), mesh=mesh,
               scratch_types=[pltpu.VMEM((n_table,), jnp.int32),
                              pltpu.VMEM((PLAN_CHUNK,), jnp.int32)],
               compiler_params=pltpu.CompilerParams(needs_layout_passes=False))
    def plan_kernel(dest_hbm, fill_hbm, out_hbm, table, chunk):
        first = jnp.logical_and(lax.axis_index("c") == 0, lax.axis_index("s") == 0)

        @pl.when(first)
        def _():
            pltpu.sync_copy(fill_hbm, table)

            @pl.loop(0, n_assign // PLAN_CHUNK)
            def _(ci):
                pltpu.sync_copy(dest_hbm.at[pl.ds(ci * PLAN_CHUNK, PLAN_CHUNK)], chunk)

                @pl.loop(0, PLAN_CHUNK // (SC_LANES * PLAN_UNROLL))
                def _(i):
                    for j in range(PLAN_UNROLL):
                        off = (i * PLAN_UNROLL + j) * SC_LANES
                        idx = chunk[pl.ds(off, SC_LANES)]
                        vals = (ci * PLAN_CHUNK + off
                                + lax.broadcasted_iota(jnp.int32, (SC_LANES,), 0))
                        plsc.store_scatter(table, [idx], vals)

            pltpu.sync_copy(table, out_hbm)

    return plan_kernel(dest_flat, fill)


def _expert_kernel(n_tok, bs_ref, slot_ref, h2_ref, wg_ref, bg_ref, wu_ref, bu_ref, wd_ref, bd_ref,
                   yt_ref, *scratch):
    TM = TM_EXPERT
    ROWS = TM * PSLAB
    e = pl.program_id(0)
    n_total = bs_ref[N_EXPERTS]
    xg = scratch[:NBUF]
    ys = scratch[NBUF:2 * NBUF]
    wgb_ref, wub_ref, wdb_ref, gsem, ssem = scratch[2 * NBUF:]

    def token_of(a):
        return a & (n_tok - 1) if n_tok & (n_tok - 1) == 0 else lax.rem(a, n_tok)

    def gather_row(base, par, r):
        t = token_of(slot_ref[base + r])
        pltpu.make_async_copy(h2_ref.at[pl.ds(pl.multiple_of(t * PSLAB, PSLAB), PSLAB), :],
                              xg[par].at[pl.ds(pl.multiple_of(r * PSLAB, PSLAB), PSLAB), :],
                              gsem.at[par]).start()

    def start_gather(blk, par, unrolled=True):
        base = (blk + 1) * TM
        if unrolled:
            for r in range(TM):
                gather_row(base, par, r)
        else:
            lax.fori_loop(0, TM, lambda r, c: (gather_row(base, par, r), c)[1], 0)

    def wait_gather(par):
        pltpu.make_async_copy(h2_ref.at[pl.ds(0, ROWS), :], xg[0], gsem.at[par]).wait()

    def scatter_row(base, par, r):
        a = slot_ref[base + r]
        pltpu.make_async_copy(ys[par].at[pl.ds(pl.multiple_of(r * PSLAB, PSLAB), PSLAB), :],
                              yt_ref.at[pl.ds(pl.multiple_of(a * PSLAB, PSLAB), PSLAB), :],
                              ssem.at[par]).start()

    def start_scatter(blk, par, unrolled=True):
        base = (blk + 1) * TM
        if unrolled:
            for r in range(TM):
                scatter_row(base, par, r)
        else:
            lax.fori_loop(0, TM, lambda r, c: (scatter_row(base, par, r), c)[1], 0)

    def wait_scatter(par):
        pltpu.make_async_copy(ys[0], yt_ref.at[pl.ds(0, ROWS), :], ssem.at[par]).wait()

    @pl.when(e == 0)
    def _():
        for blk in range(NBUF - 1):
            start_gather(blk, blk, unrolled=False)
        for par in range(NBUF):
            ys[par][...] = jnp.zeros_like(ys[par])
            dump = yt_ref.at[pl.ds((n_tok * TOP_K + par * TM) * PSLAB, ROWS), :]
            cp = pltpu.make_async_copy(ys[par], dump, ssem.at[par])
            cp.start()
            cp.wait()

    wgb_ref[...] = wg_ref[0].astype(jnp.bfloat16)
    wub_ref[...] = wu_ref[0].astype(jnp.bfloat16)
    wdb_ref[...] = wd_ref[0].astype(jnp.bfloat16)

    def block_step(g, par):
        prv = (par + NBUF - 1) % NBUF
        wait_gather(par)

        @pl.when(g >= NBUF - 1)
        def _():
            wait_scatter(par)

        start_gather(g + NBUF - 1, prv)
        start_scatter(g - 1, prv)
        words = [xg[par][pl.ds(s, TM, stride=PSLAB), :] for s in range(PSLAB)]
        x = jnp.concatenate([_unpack_lo(w).astype(jnp.bfloat16) for w in words]
                            + [_unpack_hi(w).astype(jnp.bfloat16) for w in words], axis=1)
        gate = jnp.dot(x, wgb_ref[...], preferred_element_type=jnp.float32) + bg_ref[0]
        up = jnp.dot(x, wub_ref[...], preferred_element_type=jnp.float32) + bu_ref[0]
        gate = jnp.minimum(gate, SWIGLU_LIMIT)
        up = jnp.clip(up, -SWIGLU_LIMIT, SWIGLU_LIMIT)
        glu = gate * _sigmoid(SWIGLU_ALPHA * gate)
        act = (glu * (up + 1.0)).astype(jnp.bfloat16)
        y = jnp.dot(act, wdb_ref[...], preferred_element_type=jnp.float32) + bd_ref[0]
        packed = _pack_bf16_pairs(y)
        for s in range(PSLAB):
            ys[par][pl.ds(s, TM, stride=PSLAB), :] = packed[:, s * LANES:(s + 1) * LANES]

    def body(g, carry):
        for par in range(NBUF):
            pl.when(g % NBUF == par)(functools.partial(block_step, g, par))
        return carry

    lax.fori_loop(bs_ref[e], bs_ref[e + 1], body, 0)

    @pl.when(e == N_EXPERTS - 1)
    def _():
        g = n_total
        for par in range(NBUF):
            @pl.when((g - 1) % NBUF == par)
            def _():
                start_scatter(g - 1, par, unrolled=False)
        for j in range(NBUF - 1):
            wait_gather((g + j) % NBUF)
        wait_scatter((g - 1) % NBUF)
        for j in range(2, NBUF + 1):
            @pl.when(g >= j - 1)
            def _():
                wait_scatter((g + NBUF - j) % NBUF)


def _experts(block_start, slot_buf, h2_slab, w_gate, b_gate, w_up, b_up, w_down, b_down, n_tok):
    TM = TM_EXPERT
    n_assign = n_tok * TOP_K
    w_spec = pl.BlockSpec((1, D_MODEL, D_FF), lambda e, bs, sl: (e, 0, 0))
    bias_spec = pl.BlockSpec((1, 1, D_FF), lambda e, bs, sl: (e, 0, 0))
    buf = pltpu.VMEM((TM * PSLAB, LANES), jnp.uint32)
    grid_spec = pltpu.PrefetchScalarGridSpec(
        num_scalar_prefetch=2,
        grid=(N_EXPERTS,),
        in_specs=[
            pl.BlockSpec(memory_space=pl.ANY),
            w_spec, bias_spec, w_spec, bias_spec, w_spec, bias_spec,
        ],
        out_specs=pl.BlockSpec(memory_space=pl.ANY),
        scratch_shapes=[
            *([buf] * (2 * NBUF)),
            pltpu.VMEM((D_MODEL, D_FF), jnp.bfloat16),
            pltpu.VMEM((D_MODEL, D_FF), jnp.bfloat16),
            pltpu.VMEM((D_FF, D_MODEL), jnp.bfloat16),
            pltpu.SemaphoreType.DMA((NBUF,)),
            pltpu.SemaphoreType.DMA((NBUF,)),
        ],
    )
    return pl.pallas_call(
        functools.partial(_expert_kernel, n_tok),
        grid_spec=grid_spec,
        out_shape=jax.ShapeDtypeStruct(((n_assign + NBUF * TM) * PSLAB, LANES), jnp.uint32),
        compiler_params=pltpu.CompilerParams(
            dimension_semantics=("arbitrary",), vmem_limit_bytes=VMEM_LIMIT),
        name="experts",
    )(block_start, slot_buf, h2_slab, w_gate, b_gate, w_up, b_up, w_down, b_down)


def _combine_kernel(normalize, x1_ref, y0_ref, y1_ref, y2_ref, y3_ref, gate_ref, g_ref, o_ref):
    TM = TM_COMBINE
    gates = jnp.concatenate([gate_ref[...], jnp.zeros((8 - TOP_K, TM), jnp.float32)], axis=0)
    g_cols = jnp.transpose(gates)
    g_bc = [jnp.broadcast_to(g_cols[:, k:k + 1], (TM, LANES)) for k in range(TOP_K)]
    ssq = jnp.zeros((TM, LANES), jnp.float32)
    parts = [x1_ref[:, s * LANES:(s + 1) * LANES] for s in range(SLAB)]
    for s in range(PSLAB):
        for k, y_ref in enumerate((y0_ref, y1_ref, y2_ref, y3_ref)):
            w = y_ref[pl.ds(s, TM, stride=PSLAB), :]
            parts[s] = parts[s] + g_bc[k] * _unpack_lo(w)
            parts[PSLAB + s] = parts[PSLAB + s] + g_bc[k] * _unpack_hi(w)
    for acc in parts:
        ssq = ssq + acc * acc
    if normalize:
        inv = lax.rsqrt(jnp.sum(ssq, axis=-1, keepdims=True) * (1.0 / D_MODEL) + EPS)
        for s in range(SLAB):
            o_ref[:, s * LANES:(s + 1) * LANES] = parts[s] * inv * g_ref[:, s * LANES:(s + 1) * LANES]
    else:
        for s in range(SLAB):
            o_ref[:, s * LANES:(s + 1) * LANES] = parts[s]


def _combine(x1, y_tok, gate_t, gf, normalize):
    T = x1.shape[0]
    TM = TM_COMBINE
    nt = T // TM

    def y_spec(k):
        return pl.BlockSpec((TM * PSLAB, LANES), lambda i: (k * nt + i, 0))

    return pl.pallas_call(
        functools.partial(_combine_kernel, normalize),
        grid=(nt,),
        in_specs=[
            pl.BlockSpec((TM, D_MODEL), lambda i: (i, 0)),
            y_spec(0), y_spec(1), y_spec(2), y_spec(3),
            pl.BlockSpec((TOP_K, TM), lambda i: (0, i)),
            pl.BlockSpec((1, D_MODEL), lambda i: (0, 0)),
        ],
        out_specs=pl.BlockSpec((TM, D_MODEL), lambda i: (i, 0)),
        out_shape=jax.ShapeDtypeStruct((T, D_MODEL), jnp.float32),
        compiler_params=pltpu.CompilerParams(
            dimension_semantics=("parallel",), vmem_limit_bytes=VMEM_LIMIT),
        name="combine",
    )(x1, y_tok, y_tok, y_tok, y_tok, gate_t, gf)


def kernel(x, norm1_g, w_in, ig_b, fg_b, conv_w, head_norm_g, pool_w, pool_scale, w_out, norm2_g,
           w_router, b_router, w_gate, b_gate, w_up, b_up, w_down, b_down, normf_g):
    B, S, D = x.shape
    T = B * S
    depth = norm1_g.shape[0]
    W = MLSTM_WIDTH
    f32, bf16 = jnp.float32, jnp.bfloat16

    L = CHUNK
    t_l = lax.broadcasted_iota(jnp.int32, (L, L), 0)
    t_r = lax.broadcasted_iota(jnp.int32, (L, L), 1)
    tri = (t_r <= t_l).astype(f32)
    shifts = jnp.stack([(t_l - t_r == CONV_WIDTH - 1 - j).astype(bf16)
                        for j in range(CONV_WIDTH - 1)])
    h_t = lax.broadcasted_iota(jnp.int32, (8, HALO), 0)
    h_r = lax.broadcasted_iota(jnp.int32, (8, HALO), 1)
    halo_shifts = jnp.stack([(h_r - HALO - h_t == -(CONV_WIDTH - 1 - j)).astype(bf16)
                             for j in range(CONV_WIDTH - 1)])

    t_a = lax.broadcasted_iota(jnp.int32, (TM_PROJ, TM_PROJ), 0)
    t_b = lax.broadcasted_iota(jnp.int32, (TM_PROJ, TM_PROJ), 1)
    before = (t_a < t_b).astype(bf16)

    n_assign = T * TOP_K
    n_blocks = -(-n_assign // TM_EXPERT) + N_EXPERTS
    n_rows = n_blocks * TM_EXPERT
    n_table = n_rows + NBUF * TM_EXPERT
    fill = n_assign + ((jnp.arange(n_table, dtype=jnp.int32) + (NBUF - 1) * TM_EXPERT)
                       % (NBUF * TM_EXPERT))
    x2 = x.reshape(T, D)
    for l in range(depth):
        w = w_in[l]
        w_a = w[:, :4 * W].astype(bf16)
        w_u = w[:, 4 * W + N_GATES:].astype(bf16)
        wg_t = jnp.zeros((BF16_SUBLANES, D), bf16).at[:N_GATES].set(
            w[:, 4 * W:4 * W + N_GATES].T.astype(bf16))
        p, gates_t = _in_proj(x2, norm1_g[l][None, :], w_a, w_u, wg_t)

        gate_b = jnp.concatenate([ig_b[l], fg_b[l]])[:, None].astype(f32)
        gates_b = gates_t.reshape(N_GATES, B, S).transpose(1, 0, 2)
        ym = _mlstm(p.reshape(B, S, N_MAIN), gates_b, conv_w[l].astype(f32), gate_b,
                    head_norm_g[l][None, :], tri, shifts, halo_shifts).reshape(T, W)

        x1, h2, idx_t, gate_t, rank_t, cnt = _out_route(
            x2, ym, p, pool_w[l].astype(bf16), pool_scale[l][None, :], w_out[l].astype(bf16),
            norm2_g[l][None, :], w_router[l].T.astype(bf16), b_router[l][:, None], before, S)

        counts = cnt[:, 0]
        padded = ((counts + TM_EXPERT - 1) // TM_EXPERT) * TM_EXPERT
        padded_end = jnp.cumsum(padded)
        padded_start = padded_end - padded
        expert_ids = jnp.arange(N_EXPERTS, dtype=jnp.int32)[:, None, None]
        start_of = jnp.sum(jnp.where(idx_t[None] == expert_ids, padded_start[:, None, None], 0), axis=0)
        dest = start_of + rank_t
        block_start = jnp.concatenate(
            [jnp.zeros((1,), jnp.int32), (padded_end // TM_EXPERT).astype(jnp.int32)])

        slot_buf = _plan(dest.reshape(-1) + TM_EXPERT, fill)
        y_tok = _experts(block_start, slot_buf, h2, w_gate[l], b_gate[l][:, None, :],
                         w_up[l], b_up[l][:, None, :], w_down[l], b_down[l][:, None, :], T)
        last = l + 1 == depth
        x2 = _combine(x1, y_tok, gate_t, normf_g[None, :], last)
    return x2.reshape(B, S, D)
```

```python
import functools

import jax
import jax.numpy as jnp
from jax import lax
from jax.experimental import pallas as pl
from jax.experimental.pallas import tpu as pltpu
from jax.experimental.pallas import tpu_sc as plsc

D_MODEL = 1024
MLSTM_WIDTH = 512
MLSTM_HEADS = 4
HEAD_DIM = 128
CONV_WIDTH = 4
POOL_WIDTH = 512
POOL_WINDOWS = (2, 4, 8, 16)
POOL_GROUP_DIM = 128
N_EXPERTS = 32
TOP_K = 4
D_FF = 1024
SWIGLU_LIMIT = 7.0
SWIGLU_ALPHA = 1.702
EPS = 1e-5

N_MAIN = 4 * MLSTM_WIDTH + POOL_WIDTH
N_GATES = 2 * MLSTM_HEADS

LANES = 128
BF16_SUBLANES = 16
VMEM_LIMIT = 56 * 1024 * 1024

TM_PROJ = 512
PROJ_SUB = 2
TM_COMBINE = 1024
ROUTE_SUB = 2
CHUNK = 256
MLSTM_BATCH = 4
HALO = 16
TM_EXPERT = 384
NBUF = 3
SLAB = D_MODEL // LANES
PSLAB = SLAB // 2
PLAN_CHUNK = 32768
SC_LANES = 16
PLAN_UNROLL = 8

NT_DIMS = (((1,), (1,)), ((), ()))


def _sigmoid(x):
    return 1.0 / (1.0 + jnp.exp(-x))


def _pack_bf16_pairs(v):
    half = v.shape[1] // 2
    lo = pltpu.bitcast(v[:, :half].astype(jnp.bfloat16).astype(jnp.float32), jnp.uint32)
    hi = pltpu.bitcast(v[:, half:].astype(jnp.bfloat16).astype(jnp.float32), jnp.uint32)
    return (lo >> 16) | (hi & jnp.uint32(0xFFFF0000))


def _unpack_lo(w):
    return pltpu.bitcast(w << 16, jnp.float32)


def _unpack_hi(w):
    return pltpu.bitcast(w & jnp.uint32(0xFFFF0000), jnp.float32)


def _in_proj_kernel(x_ref, g_ref, wa_ref, wu_ref, wgt_ref, p_ref, gt_ref):
    n_a = wa_ref.shape[1]
    for sub in range(PROJ_SUB):
        rows = slice(sub * TM_PROJ, (sub + 1) * TM_PROJ)
        x = x_ref[rows, :]
        h = x * lax.rsqrt(jnp.mean(x * x, axis=-1, keepdims=True) + EPS) * g_ref[...]
        hb = h.astype(jnp.bfloat16)
        p_ref[rows, :n_a] = jnp.dot(hb, wa_ref[...],
                                    preferred_element_type=jnp.float32).astype(p_ref.dtype)
        p_ref[rows, n_a:] = jnp.dot(hb, wu_ref[...],
                                    preferred_element_type=jnp.float32).astype(p_ref.dtype)
        gt = lax.dot_general(wgt_ref[...], hb, NT_DIMS, preferred_element_type=jnp.float32)
        gt_ref[:, rows] = gt[:N_GATES]


def _in_proj(x2, g1, w_a, w_u, wg_t):
    T = x2.shape[0]
    return pl.pallas_call(
        _in_proj_kernel,
        grid=(T // (PROJ_SUB * TM_PROJ),),
        in_specs=[
            pl.BlockSpec((PROJ_SUB * TM_PROJ, D_MODEL), lambda i: (i, 0)),
            pl.BlockSpec((1, D_MODEL), lambda i: (0, 0)),
            pl.BlockSpec(w_a.shape, lambda i: (0, 0)),
            pl.BlockSpec(w_u.shape, lambda i: (0, 0)),
            pl.BlockSpec((BF16_SUBLANES, D_MODEL), lambda i: (0, 0)),
        ],
        out_specs=[
            pl.BlockSpec((PROJ_SUB * TM_PROJ, N_MAIN), lambda i: (i, 0)),
            pl.BlockSpec((N_GATES, PROJ_SUB * TM_PROJ), lambda i: (0, i)),
        ],
        out_shape=[
            jax.ShapeDtypeStruct((T, N_MAIN), jnp.bfloat16),
            jax.ShapeDtypeStruct((N_GATES, T), jnp.float32),
        ],
        compiler_params=pltpu.CompilerParams(
            dimension_semantics=("parallel",), vmem_limit_bytes=VMEM_LIMIT),
        name="in_proj",
    )(x2, g1, w_a, w_u, wg_t)


def _mlstm_kernel(qk_ref, qkp_ref, v_ref, o_ref, gt_ref, convw_ref, gb_ref, hng_ref,
                  tri_ref, shift_ref, hshift_ref, y_ref, cn_ref, m_ref):
    L = CHUNK
    c = pl.program_id(1)

    @pl.when(c == 0)
    def _():
        cn_ref[...] = jnp.zeros_like(cn_ref)
        m_ref[...] = jnp.zeros_like(m_ref)

    row_id = lax.broadcasted_iota(jnp.int32, (L, L), 0)
    col_id = lax.broadcasted_iota(jnp.int32, (L, L), 1)
    causal = col_id <= row_id
    ones_blk = jnp.ones((L, HEAD_DIM), jnp.bfloat16)
    lane = lax.broadcasted_iota(jnp.int32, (MLSTM_HEADS, L), 1)

    gate_terms = []
    for bb in range(MLSTM_BATCH):
        gt = gt_ref[bb] + gb_ref[...]
        f = gt[MLSTM_HEADS:]
        lf = jnp.minimum(f, 0.0) - jnp.log(1.0 + jnp.exp(-jnp.abs(f)))
        ig = gt[:MLSTM_HEADS]
        b_rows = lax.dot_general(lf, tri_ref[...], NT_DIMS, precision=lax.Precision.HIGHEST,
                                 preferred_element_type=jnp.float32)
        c_rows = ig - b_rows
        cm_rows = c_rows
        d = 1
        while d < L:
            cm_rows = jnp.maximum(
                cm_rows, jnp.where(lane >= d, pltpu.roll(cm_rows, d, axis=1), -jnp.inf))
            d *= 2
        gate_terms.append((b_rows, c_rows, cm_rows))

    conv_terms = []
    for bb in range(MLSTM_BATCH):
        x_cur = qk_ref[bb]
        x_prev = jnp.where(c > 0, qkp_ref[bb], jnp.zeros((HALO, 2 * MLSTM_WIDTH), jnp.bfloat16))
        acc = convw_ref[CONV_WIDTH - 1:CONV_WIDTH, :] * x_cur.astype(jnp.float32)
        for j in range(CONV_WIDTH - 1):
            sh = jnp.dot(shift_ref[j], x_cur, preferred_element_type=jnp.float32)
            top = sh[:8] + jnp.dot(hshift_ref[j], x_prev, preferred_element_type=jnp.float32)
            sh = jnp.concatenate([top, sh[8:]], axis=0)
            acc = acc + convw_ref[j:j + 1, :] * sh
        qk = acc * _sigmoid(acc)
        q_all = qk[:, :MLSTM_WIDTH].astype(jnp.bfloat16)
        k_t = jnp.transpose(qk[:, MLSTM_WIDTH:] * (HEAD_DIM ** -0.5))
        conv_terms.append((q_all, k_t))

    for bb in range(MLSTM_BATCH):
        b_rows, c_rows, cm_rows = gate_terms[bb]
        q_all, k_t = conv_terms[bb]
        m_in4 = jnp.concatenate(
            [m_ref[bb * MLSTM_HEADS + h][0:1, 0:1] for h in range(MLSTM_HEADS)], axis=0)
        mx_rows = jnp.maximum(cm_rows, m_in4)
        inter_rows = jnp.exp(m_in4 - mx_rows)
        einv_rows = jnp.exp(-(b_rows + mx_rows))
        fac_t = jnp.transpose(jnp.concatenate(
            [mx_rows, inter_rows, einv_rows, jnp.zeros_like(mx_rows)], axis=0))

        for h in range(MLSTM_HEADS):
            lo = h * HEAD_DIM
            st = bb * MLSTM_HEADS + h
            q = q_all[:, lo:lo + HEAD_DIM]
            kt = k_t[lo:lo + HEAD_DIM, :]
            v_ext = jnp.concatenate([v_ref[bb, :, lo:lo + HEAD_DIM], ones_blk], axis=1)
            mx_col = fac_t[:, h:h + 1]
            inter_col = fac_t[:, MLSTM_HEADS + h:MLSTM_HEADS + h + 1]
            einv_col = fac_t[:, 2 * MLSTM_HEADS + h:2 * MLSTM_HEADS + h + 1]
            c_row = c_rows[h:h + 1, :]
            b_tot = b_rows[h:h + 1, L - 1:L]
            cm_tot = cm_rows[h:h + 1, L - 1:L]
            m_in = m_ref[st][0:1, 0:1]
            cn = cn_ref[st]

            s_qk = jnp.dot(q, kt.astype(jnp.bfloat16), preferred_element_type=jnp.float32)
            s = (s_qk * jnp.exp(jnp.where(causal, c_row - mx_col, -jnp.inf))).astype(jnp.bfloat16)
            num = (jnp.dot(s, v_ext, preferred_element_type=jnp.float32)
                   + inter_col * jnp.dot(q, cn.astype(jnp.bfloat16),
                                         preferred_element_type=jnp.float32))
            den = num[:, HEAD_DIM:]
            hh = num[:, :HEAD_DIM] / jnp.maximum(jnp.abs(den), einv_col)

            mu = jnp.mean(hh, axis=-1, keepdims=True)
            dv = hh - mu
            var = jnp.mean(dv * dv, axis=-1, keepdims=True)
            hn = dv * lax.rsqrt(var + EPS) * hng_ref[:, lo:lo + HEAD_DIM]
            og = _sigmoid(o_ref[bb, :, lo:lo + HEAD_DIM].astype(jnp.float32))
            y_ref[bb, :, lo:lo + HEAD_DIM] = (og * hn).astype(y_ref.dtype)

            m_loc = b_tot + cm_tot
            kw_t = (kt * jnp.exp(c_row - cm_tot)).astype(jnp.bfloat16)
            c_loc = jnp.dot(kw_t, v_ext, preferred_element_type=jnp.float32)
            m_new = jnp.maximum(b_tot + m_in, m_loc)
            s_old = jnp.exp(b_tot + m_in - m_new)
            s_loc = jnp.exp(m_loc - m_new)
            cn_ref[st] = s_old * cn + s_loc * c_loc
            m_ref[st] = jnp.broadcast_to(m_new, m_ref.shape[1:])


def _mlstm(p3, gates_b, conv_w, gate_b, hn_g, tri, shifts, halo_shifts):
    batch, seq, _ = p3.shape
    L = CHUNK
    BB = MLSTM_BATCH
    halo_per_chunk = L // HALO
    return pl.pallas_call(
        _mlstm_kernel,
        grid=(batch // BB, seq // L),
        in_specs=[
            pl.BlockSpec((BB, L, 2 * MLSTM_WIDTH), lambda bi, ci: (bi, ci, 0)),
            pl.BlockSpec((BB, HALO, 2 * MLSTM_WIDTH),
                         lambda bi, ci: (bi, jnp.maximum(ci * halo_per_chunk - 1, 0), 0)),
            pl.BlockSpec((BB, L, MLSTM_WIDTH), lambda bi, ci: (bi, ci, 2)),
            pl.BlockSpec((BB, L, MLSTM_WIDTH), lambda bi, ci: (bi, ci, 3)),
            pl.BlockSpec((BB, N_GATES, L), lambda bi, ci: (bi, 0, ci)),
            pl.BlockSpec((CONV_WIDTH, 2 * MLSTM_WIDTH), lambda bi, ci: (0, 0)),
            pl.BlockSpec((N_GATES, 1), lambda bi, ci: (0, 0)),
            pl.BlockSpec((1, MLSTM_WIDTH), lambda bi, ci: (0, 0)),
            pl.BlockSpec((L, L), lambda bi, ci: (0, 0)),
            pl.BlockSpec((CONV_WIDTH - 1, L, L), lambda bi, ci: (0, 0, 0)),
            pl.BlockSpec((CONV_WIDTH - 1, 8, HALO), lambda bi, ci: (0, 0, 0)),
        ],
        out_specs=pl.BlockSpec((BB, L, MLSTM_WIDTH), lambda bi, ci: (bi, ci, 0)),
        out_shape=jax.ShapeDtypeStruct((batch, seq, MLSTM_WIDTH), jnp.bfloat16),
        scratch_shapes=[
            pltpu.VMEM((BB * MLSTM_HEADS, HEAD_DIM, 2 * HEAD_DIM), jnp.float32),
            pltpu.VMEM((BB * MLSTM_HEADS, 8, LANES), jnp.float32),
        ],
        compiler_params=pltpu.CompilerParams(
            dimension_semantics=("parallel", "arbitrary"), vmem_limit_bytes=VMEM_LIMIT),
        name="mlstm",
    )(p3, p3, p3, p3, gates_b, conv_w, gate_b, hn_g, tri, shifts, halo_shifts)


def _out_route_kernel(seq, x_ref, ym_ref, u_ref, up_ref, pw_ref, ps_ref, wo_ref, g2_ref,
                      wrt_ref, br_ref, before_ref, x1_ref, h2_ref, idx_ref, gate_ref, rank_ref, cnt_ref,
                      carry_ref):
    TM = TM_PROJ
    R = ROUTE_SUB * TM
    i = pl.program_id(0)

    @pl.when(i == 0)
    def _():
        carry_ref[...] = jnp.zeros_like(carry_ref)

    pos0 = (i * R) % seq
    e_id = lax.broadcasted_iota(jnp.int32, (N_EXPERTS, TM), 0).astype(jnp.float32)
    carry = carry_ref[...]
    subs = [slice(sub * TM, (sub + 1) * TM) for sub in range(ROUTE_SUB)]

    halo = jnp.where(pos0 > 0, up_ref[...].astype(jnp.float32), 0.0)
    u_ext = jnp.concatenate([halo, u_ref[...].astype(jnp.float32)], axis=0)
    win_sums = []
    for gi, w in enumerate(POOL_WINDOWS):
        sw = u_ext[:, gi * POOL_GROUP_DIM:(gi + 1) * POOL_GROUP_DIM]
        span = 1
        while span < w:
            sw = sw + pltpu.roll(sw, span, axis=0)
            span *= 2
        win_sums.append(sw)
    y_cats = []
    for sub, rows in enumerate(subs):
        r0 = sub * TM
        pos = (pos0 + r0 + lax.broadcasted_iota(jnp.int32, (TM, 1), 0) + 1).astype(jnp.float32)
        mixed = []
        for gi, w in enumerate(POOL_WINDOWS):
            lo = gi * POOL_GROUP_DIM
            tok = u_ext[HALO + r0:HALO + r0 + TM, lo:lo + POOL_GROUP_DIM]
            pooled = win_sums[gi][HALO + r0:HALO + r0 + TM] / jnp.minimum(pos, float(w)) - tok
            mg = jnp.dot(pooled.astype(jnp.bfloat16), pw_ref[gi],
                         preferred_element_type=jnp.float32)
            mixed.append((mg * ps_ref[:, lo:lo + POOL_GROUP_DIM]).astype(jnp.bfloat16))
        y_cats.append(jnp.concatenate([ym_ref[rows, :]] + mixed, axis=1))

    all_logits = []
    for sub, rows in enumerate(subs):
        r0 = sub * TM
        x1 = x_ref[rows, :] + jnp.dot(y_cats[sub], wo_ref[...], preferred_element_type=jnp.float32)
        x1_ref[rows, :] = x1
        h2 = x1 * lax.rsqrt(jnp.mean(x1 * x1, axis=-1, keepdims=True) + EPS) * g2_ref[...]
        h2b = h2.astype(jnp.bfloat16)
        h2w = _pack_bf16_pairs(h2)
        for s in range(PSLAB):
            h2_ref[pl.ds(r0 * PSLAB + s, TM, stride=PSLAB), :] = h2w[:, s * LANES:(s + 1) * LANES]
        all_logits.append(lax.dot_general(wrt_ref[...], h2b, NT_DIMS,
                                          preferred_element_type=jnp.float32) + br_ref[...])

    for sub, rows in enumerate(subs):
        work = all_logits[sub]
        vals, ids, hots = [], [], []
        for _ in range(TOP_K):
            mk = jnp.max(work, axis=0, keepdims=True)
            ik = jnp.min(jnp.where(work == mk, e_id, float(N_EXPERTS)), axis=0, keepdims=True)
            hot = e_id == ik
            work = jnp.where(hot, -jnp.inf, work)
            vals.append(mk)
            ids.append(ik)
            hots.append(hot)
        ex = [jnp.exp(vk - vals[0]) for vk in vals]
        denom = ex[0] + ex[1] + ex[2] + ex[3]
        gate_ref[:, rows] = jnp.concatenate([e / denom for e in ex], axis=0)
        idx_ref[:, rows] = jnp.concatenate(ids, axis=0).astype(jnp.int32)

        sel_f = sum(jnp.where(hot, 1.0, 0.0) for hot in hots)
        prefix = jnp.dot(sel_f.astype(jnp.bfloat16), before_ref[...],
                         preferred_element_type=jnp.float32)
        rank_e = carry[:, 0:1] + prefix
        ranks = [jnp.sum(jnp.where(hot, rank_e, 0.0), axis=0, keepdims=True) for hot in hots]
        rank_ref[:, rows] = jnp.concatenate(ranks, axis=0).astype(jnp.int32)
        carry = carry + jnp.sum(sel_f, axis=1, keepdims=True)
    carry_ref[...] = carry
    cnt_ref[...] = carry.astype(jnp.int32)


def _out_route(x2, ym, p, pool_w, pool_s, w_out, g2, wr_t, br, before, seq):
    T = x2.shape[0]
    TM = ROUTE_SUB * TM_PROJ
    nt = T // TM
    u_blk = N_MAIN // POOL_WIDTH - 1
    halo_per_tile = TM // HALO
    tok_spec = pl.BlockSpec((TOP_K, TM), lambda i: (0, i))
    return pl.pallas_call(
        functools.partial(_out_route_kernel, seq),
        grid=(nt,),
        in_specs=[
            pl.BlockSpec((TM, D_MODEL), lambda i: (i, 0)),
            pl.BlockSpec((TM, MLSTM_WIDTH), lambda i: (i, 0)),
            pl.BlockSpec((TM, POOL_WIDTH), lambda i: (i, u_blk)),
            pl.BlockSpec((HALO, POOL_WIDTH),
                         lambda i: (jnp.maximum(i * halo_per_tile - 1, 0), u_blk)),
            pl.BlockSpec((len(POOL_WINDOWS), POOL_GROUP_DIM, POOL_GROUP_DIM), lambda i: (0, 0, 0)),
            pl.BlockSpec((1, POOL_WIDTH), lambda i: (0, 0)),
            pl.BlockSpec((D_MODEL, D_MODEL), lambda i: (0, 0)),
            pl.BlockSpec((1, D_MODEL), lambda i: (0, 0)),
            pl.BlockSpec((N_EXPERTS, D_MODEL), lambda i: (0, 0)),
            pl.BlockSpec((N_EXPERTS, 1), lambda i: (0, 0)),
            pl.BlockSpec((TM_PROJ, TM_PROJ), lambda i: (0, 0)),
        ],
        out_specs=[
            pl.BlockSpec((TM, D_MODEL), lambda i: (i, 0)),
            pl.BlockSpec((TM * PSLAB, LANES), lambda i: (i, 0)),
            tok_spec, tok_spec, tok_spec,
            pl.BlockSpec((N_EXPERTS, LANES), lambda i: (0, 0)),
        ],
        out_shape=[
            jax.ShapeDtypeStruct((T, D_MODEL), jnp.float32),
            jax.ShapeDtypeStruct((T * PSLAB, LANES), jnp.uint32),
            jax.ShapeDtypeStruct((TOP_K, T), jnp.int32),
            jax.ShapeDtypeStruct((TOP_K, T), jnp.float32),
            jax.ShapeDtypeStruct((TOP_K, T), jnp.int32),
            jax.ShapeDtypeStruct((N_EXPERTS, LANES), jnp.int32),
        ],
        scratch_shapes=[
            pltpu.VMEM((N_EXPERTS, LANES), jnp.float32),
        ],
        compiler_params=pltpu.CompilerParams(
            dimension_semantics=("arbitrary",), vmem_limit_bytes=VMEM_LIMIT),
        name="out_route",
    )(x2, ym, p, p, pool_w, pool_s, w_out, g2, wr_t, br, before)


def _plan(dest_flat, fill):
    n_assign = dest_flat.shape[0]
    n_table = fill.shape[0]
    mesh = plsc.VectorSubcoreMesh(core_axis_name="c", subcore_axis_name="s")

    @pl.kernel(out_type=jax.ShapeDtypeStruct((n_table,), jnp.int32), mesh=mesh,
               scratch_types=[pltpu.VMEM((n_table,), jnp.int32),
                              pltpu.VMEM((PLAN_CHUNK,), jnp.int32)],
               compiler_params=pltpu.CompilerParams(needs_layout_passes=False))
    def plan_kernel(dest_hbm, fill_hbm, out_hbm, table, chunk):
        first = jnp.logical_and(lax.axis_index("c") == 0, lax.axis_index("s") == 0)

        @pl.when(first)
        def _():
            pltpu.sync_copy(fill_hbm, table)

            @pl.loop(0, n_assign // PLAN_CHUNK)
            def _(ci):
                pltpu.sync_copy(dest_hbm.at[pl.ds(ci * PLAN_CHUNK, PLAN_CHUNK)], chunk)

                @pl.loop(0, PLAN_CHUNK // (SC_LANES * PLAN_UNROLL))
                def _(i):
                    for j in range(PLAN_UNROLL):
                        off = (i * PLAN_UNROLL + j) * SC_LANES
                        idx = chunk[pl.ds(off, SC_LANES)]
                        vals = (ci * PLAN_CHUNK + off
                                + lax.broadcasted_iota(jnp.int32, (SC_LANES,), 0))
                        plsc.store_scatter(table, [idx], vals)

            pltpu.sync_copy(table, out_hbm)

    return plan_kernel(dest_flat, fill)


def _expert_kernel(n_tok, bs_ref, slot_ref, h2_ref, wg_ref, bg_ref, wu_ref, bu_ref, wd_ref, bd_ref,
                   yt_ref, *scratch):
    TM = TM_EXPERT
    ROWS = TM * PSLAB
    e = pl.program_id(0)
    n_total = bs_ref[N_EXPERTS]
    xg = scratch[:NBUF]
    ys = scratch[NBUF:2 * NBUF]
    wgb_ref, wub_ref, wdb_ref, gsem, ssem = scratch[2 * NBUF:]

    def token_of(a):
        return a & (n_tok - 1) if n_tok & (n_tok - 1) == 0 else lax.rem(a, n_tok)

    def gather_row(base, par, r):
        t = token_of(slot_ref[base + r])
        pltpu.make_async_copy(h2_ref.at[pl.ds(pl.multiple_of(t * PSLAB, PSLAB), PSLAB), :],
                              xg[par].at[pl.ds(pl.multiple_of(r * PSLAB, PSLAB), PSLAB), :],
                              gsem.at[par]).start()

    def start_gather(blk, par, unrolled=True):
        base = (blk + 1) * TM
        if unrolled:
            for r in range(TM):
                gather_row(base, par, r)
        else:
            lax.fori_loop(0, TM, lambda r, c: (gather_row(base, par, r), c)[1], 0)

    def wait_gather(par):
        pltpu.make_async_copy(h2_ref.at[pl.ds(0, ROWS), :], xg[0], gsem.at[par]).wait()

    def scatter_row(base, par, r):
        a = slot_ref[base + r]
        pltpu.make_async_copy(ys[par].at[pl.ds(pl.multiple_of(r * PSLAB, PSLAB), PSLAB), :],
                              yt_ref.at[pl.ds(pl.multiple_of(a * PSLAB, PSLAB), PSLAB), :],
                              ssem.at[par]).start()

    def start_scatter(blk, par, unrolled=True):
        base = (blk + 1) * TM
        if unrolled:
            for r in range(TM):
                scatter_row(base, par, r)
        else:
            lax.fori_loop(0, TM, lambda r, c: (scatter_row(base, par, r), c)[1], 0)

    def wait_scatter(par):
        pltpu.make_async_copy(ys[0], yt_ref.at[pl.ds(0, ROWS), :], ssem.at[par]).wait()

    @pl.when(e == 0)
    def _():
        for blk in range(NBUF - 1):
            start_gather(blk, blk, unrolled=False)
        for par in range(NBUF):
            ys[par][...] = jnp.zeros_like(ys[par])
            dump = yt_ref.at[pl.ds((n_tok * TOP_K + par * TM) * PSLAB, ROWS), :]
            cp = pltpu.make_async_copy(ys[par], dump, ssem.at[par])
            cp.start()
            cp.wait()

    wgb_ref[...] = wg_ref[0].astype(jnp.bfloat16)
    wub_ref[...] = wu_ref[0].astype(jnp.bfloat16)
    wdb_ref[...] = wd_ref[0].astype(jnp.bfloat16)

    def block_step(g, par):
        prv = (par + NBUF - 1) % NBUF
        wait_gather(par)

        @pl.when(g >= NBUF - 1)
        def _():
            wait_scatter(par)

        start_gather(g + NBUF - 1, prv)
        start_scatter(g - 1, prv)
        words = [xg[par][pl.ds(s, TM, stride=PSLAB), :] for s in range(PSLAB)]
        x = jnp.concatenate([_unpack_lo(w).astype(jnp.bfloat16) for w in words]
                            + [_unpack_hi(w).astype(jnp.bfloat16) for w in words], axis=1)
        gate = jnp.dot(x, wgb_ref[...], preferred_element_type=jnp.float32) + bg_ref[0]
        up = jnp.dot(x, wub_ref[...], preferred_element_type=jnp.float32) + bu_ref[0]
        gate = jnp.minimum(gate, SWIGLU_LIMIT)
        up = jnp.clip(up, -SWIGLU_LIMIT, SWIGLU_LIMIT)
        glu = gate * _sigmoid(SWIGLU_ALPHA * gate)
        act = (glu * (up + 1.0)).astype(jnp.bfloat16)
        y = jnp.dot(act, wdb_ref[...], preferred_element_type=jnp.float32) + bd_ref[0]
        packed = _pack_bf16_pairs(y)
        for s in range(PSLAB):
            ys[par][pl.ds(s, TM, stride=PSLAB), :] = packed[:, s * LANES:(s + 1) * LANES]

    def body(g, carry):
        for par in range(NBUF):
            pl.when(g % NBUF == par)(functools.partial(block_step, g, par))
        return carry

    lax.fori_loop(bs_ref[e], bs_ref[e + 1], body, 0)

    @pl.when(e == N_EXPERTS - 1)
    def _():
        g = n_total
        for par in range(NBUF):
            @pl.when((g - 1) % NBUF == par)
            def _():
                start_scatter(g - 1, par, unrolled=False)
        for j in range(NBUF - 1):
            wait_gather((g + j) % NBUF)
        wait_scatter((g - 1) % NBUF)
        for j in range(2, NBUF + 1):
            @pl.when(g >= j - 1)
            def _():
                wait_scatter((g + NBUF - j) % NBUF)


def _experts(block_start, slot_buf, h2_slab, w_gate, b_gate, w_up, b_up, w_down, b_down, n_tok):
    TM = TM_EXPERT
    n_assign = n_tok * TOP_K
    w_spec = pl.BlockSpec((1, D_MODEL, D_FF), lambda e, bs, sl: (e, 0, 0))
    bias_spec = pl.BlockSpec((1, 1, D_FF), lambda e, bs, sl: (e, 0, 0))
    buf = pltpu.VMEM((TM * PSLAB, LANES), jnp.uint32)
    grid_spec = pltpu.PrefetchScalarGridSpec(
        num_scalar_prefetch=2,
        grid=(N_EXPERTS,),
        in_specs=[
            pl.BlockSpec(memory_space=pl.ANY),
            w_spec, bias_spec, w_spec, bias_spec, w_spec, bias_spec,
        ],
        out_specs=pl.BlockSpec(memory_space=pl.ANY),
        scratch_shapes=[
            *([buf] * (2 * NBUF)),
            pltpu.VMEM((D_MODEL, D_FF), jnp.bfloat16),
            pltpu.VMEM((D_MODEL, D_FF), jnp.bfloat16),
            pltpu.VMEM((D_FF, D_MODEL), jnp.bfloat16),
            pltpu.SemaphoreType.DMA((NBUF,)),
            pltpu.SemaphoreType.DMA((NBUF,)),
        ],
    )
    return pl.pallas_call(
        functools.partial(_expert_kernel, n_tok),
        grid_spec=grid_spec,
        out_shape=jax.ShapeDtypeStruct(((n_assign + NBUF * TM) * PSLAB, LANES), jnp.uint32),
        compiler_params=pltpu.CompilerParams(
            dimension_semantics=("arbitrary",), vmem_limit_bytes=VMEM_LIMIT),
        name="experts",
    )(block_start, slot_buf, h2_slab, w_gate, b_gate, w_up, b_up, w_down, b_down)


def _combine_kernel(normalize, x1_ref, y0_ref, y1_ref, y2_ref, y3_ref, gate_ref, g_ref, o_ref):
    TM = TM_COMBINE
    gates = jnp.concatenate([gate_ref[...], jnp.zeros((8 - TOP_K, TM), jnp.float32)], axis=0)
    g_cols = jnp.transpose(gates)
    g_bc = [jnp.broadcast_to(g_cols[:, k:k + 1], (TM, LANES)) for k in range(TOP_K)]
    ssq = jnp.zeros((TM, LANES), jnp.float32)
    parts = [x1_ref[:, s * LANES:(s + 1) * LANES] for s in range(SLAB)]
    for s in range(PSLAB):
        for k, y_ref in enumerate((y0_ref, y1_ref, y2_ref, y3_ref)):
            w = y_ref[pl.ds(s, TM, stride=PSLAB), :]
            parts[s] = parts[s] + g_bc[k] * _unpack_lo(w)
            parts[PSLAB + s] = parts[PSLAB + s] + g_bc[k] * _unpack_hi(w)
    for acc in parts:
        ssq = ssq + acc * acc
    if normalize:
        inv = lax.rsqrt(jnp.sum(ssq, axis=-1, keepdims=True) * (1.0 / D_MODEL) + EPS)
        for s in range(SLAB):
            o_ref[:, s * LANES:(s + 1) * LANES] = parts[s] * inv * g_ref[:, s * LANES:(s + 1) * LANES]
    else:
        for s in range(SLAB):
            o_ref[:, s * LANES:(s + 1) * LANES] = parts[s]


def _combine(x1, y_tok, gate_t, gf, normalize):
    T = x1.shape[0]
    TM = TM_COMBINE
    nt = T // TM

    def y_spec(k):
        return pl.BlockSpec((TM * PSLAB, LANES), lambda i: (k * nt + i, 0))

    return pl.pallas_call(
        functools.partial(_combine_kernel, normalize),
        grid=(nt,),
        in_specs=[
            pl.BlockSpec((TM, D_MODEL), lambda i: (i, 0)),
            y_spec(0), y_spec(1), y_spec(2), y_spec(3),
            pl.BlockSpec((TOP_K, TM), lambda i: (0, i)),
            pl.BlockSpec((1, D_MODEL), lambda i: (0, 0)),
        ],
        out_specs=pl.BlockSpec((TM, D_MODEL), lambda i: (i, 0)),
        out_shape=jax.ShapeDtypeStruct((T, D_MODEL), jnp.float32),
        compiler_params=pltpu.CompilerParams(
            dimension_semantics=("parallel",), vmem_limit_bytes=VMEM_LIMIT),
        name="combine",
    )(x1, y_tok, y_tok, y_tok, y_tok, gate_t, gf)


def kernel(x, norm1_g, w_in, ig_b, fg_b, conv_w, head_norm_g, pool_w, pool_scale, w_out, norm2_g,
           w_router, b_router, w_gate, b_gate, w_up, b_up, w_down, b_down, normf_g):
    B, S, D = x.shape
    T = B * S
    depth = norm1_g.shape[0]
    W = MLSTM_WIDTH
    f32, bf16 = jnp.float32, jnp.bfloat16

    L = CHUNK
    t_l = lax.broadcasted_iota(jnp.int32, (L, L), 0)
    t_r = lax.broadcasted_iota(jnp.int32, (L, L), 1)
    tri = (t_r <= t_l).astype(f32)
    shifts = jnp.stack([(t_l - t_r == CONV_WIDTH - 1 - j).astype(bf16)
                        for j in range(CONV_WIDTH - 1)])
    h_t = lax.broadcasted_iota(jnp.int32, (8, HALO), 0)
    h_r = lax.broadcasted_iota(jnp.int32, (8, HALO), 1)
    halo_shifts = jnp.stack([(h_r - HALO - h_t == -(CONV_WIDTH - 1 - j)).astype(bf16)
                             for j in range(CONV_WIDTH - 1)])

    t_a = lax.broadcasted_iota(jnp.int32, (TM_PROJ, TM_PROJ), 0)
    t_b = lax.broadcasted_iota(jnp.int32, (TM_PROJ, TM_PROJ), 1)
    before = (t_a < t_b).astype(bf16)

    n_assign = T * TOP_K
    n_blocks = -(-n_assign // TM_EXPERT) + N_EXPERTS
    n_rows = n_blocks * TM_EXPERT
    n_table = n_rows + NBUF * TM_EXPERT
    fill = n_assign + ((jnp.arange(n_table, dtype=jnp.int32) + (NBUF - 1) * TM_EXPERT)
                       % (NBUF * TM_EXPERT))
    x2 = x.reshape(T, D)
    for l in range(depth):
        w = w_in[l]
        w_a = w[:, :4 * W].astype(bf16)
        w_u = w[:, 4 * W + N_GATES:].astype(bf16)
        wg_t = jnp.zeros((BF16_SUBLANES, D), bf16).at[:N_GATES].set(
            w[:, 4 * W:4 * W + N_GATES].T.astype(bf16))
        p, gates_t = _in_proj(x2, norm1_g[l][None, :], w_a, w_u, wg_t)

        gate_b = jnp.concatenate([ig_b[l], fg_b[l]])[:, None].astype(f32)
        gates_b = gates_t.reshape(N_GATES, B, S).transpose(1, 0, 2)
        ym = _mlstm(p.reshape(B, S, N_MAIN), gates_b, conv_w[l].astype(f32), gate_b,
                    head_norm_g[l][None, :], tri, shifts, halo_shifts).reshape(T, W)

        x1, h2, idx_t, gate_t, rank_t, cnt = _out_route(
            x2, ym, p, pool_w[l].astype(bf16), pool_scale[l][None, :], w_out[l].astype(bf16),
            norm2_g[l][None, :], w_router[l].T.astype(bf16), b_router[l][:, None], before, S)

        counts = cnt[:, 0]
        padded = ((counts + TM_EXPERT - 1) // TM_EXPERT) * TM_EXPERT
        padded_end = jnp.cumsum(padded)
        padded_start = padded_end - padded
        expert_ids = jnp.arange(N_EXPERTS, dtype=jnp.int32)[:, None, None]
        start_of = jnp.sum(jnp.where(idx_t[None] == expert_ids, padded_start[:, None, None], 0), axis=0)
        dest = start_of + rank_t
        block_start = jnp.concatenate(
            [jnp.zeros((1,), jnp.int32), (padded_end // TM_EXPERT).astype(jnp.int32)])

        slot_buf = _plan(dest.reshape(-1) + TM_EXPERT, fill)
        y_tok = _experts(block_start, slot_buf, h2, w_gate[l], b_gate[l][:, None, :],
                         w_up[l], b_up[l][:, None, :], w_down[l], b_down[l][:, None, :], T)
        last = l + 1 == depth
        x2 = _combine(x1, y_tok, gate_t, normf_g[None, :], last)
    return x2.reshape(B, S, D)
```

```python
import functools

import jax
import jax.numpy as jnp
from jax import lax
from jax.experimental import pallas as pl
from jax.experimental.pallas import tpu as pltpu
from jax.experimental.pallas import tpu_sc as plsc

D_MODEL = 1024
MLSTM_WIDTH = 512
MLSTM_HEADS = 4
HEAD_DIM = 128
CONV_WIDTH = 4
POOL_WIDTH = 512
POOL_WINDOWS = (2, 4, 8, 16)
POOL_GROUP_DIM = 128
N_EXPERTS = 32
TOP_K = 4
D_FF = 1024
SWIGLU_LIMIT = 7.0
SWIGLU_ALPHA = 1.702
EPS = 1e-5

N_MAIN = 4 * MLSTM_WIDTH + POOL_WIDTH
N_GATES = 2 * MLSTM_HEADS

LANES = 128
BF16_SUBLANES = 16
VMEM_LIMIT = 56 * 1024 * 1024

TM_PROJ = 512
PROJ_SUB = 2
TM_COMBINE = 1024
ROUTE_SUB = 2
CHUNK = 256
MLSTM_BATCH = 4
HALO = 16
TM_EXPERT = 384
NBUF = 4
SLAB = D_MODEL // LANES
PSLAB = SLAB // 2
PLAN_CHUNK = 32768
SC_LANES = 16
PLAN_UNROLL = 8

NT_DIMS = (((1,), (1,)), ((), ()))


def _sigmoid(x):
    return 1.0 / (1.0 + jnp.exp(-x))


def _pack_bf16_pairs(v):
    half = v.shape[1] // 2
    lo = pltpu.bitcast(v[:, :half].astype(jnp.bfloat16).astype(jnp.float32), jnp.uint32)
    hi = pltpu.bitcast(v[:, half:].astype(jnp.bfloat16).astype(jnp.float32), jnp.uint32)
    return (lo >> 16) | (hi & jnp.uint32(0xFFFF0000))


def _unpack_lo(w):
    return pltpu.bitcast(w << 16, jnp.float32)


def _unpack_hi(w):
    return pltpu.bitcast(w & jnp.uint32(0xFFFF0000), jnp.float32)


def _in_proj_kernel(x_ref, g_ref, wa_ref, wu_ref, wgt_ref, p_ref, gt_ref):
    n_a = wa_ref.shape[1]
    for sub in range(PROJ_SUB):
        rows = slice(sub * TM_PROJ, (sub + 1) * TM_PROJ)
        x = x_ref[rows, :]
        h = x * lax.rsqrt(jnp.mean(x * x, axis=-1, keepdims=True) + EPS) * g_ref[...]
        hb = h.astype(jnp.bfloat16)
        p_ref[rows, :n_a] = jnp.dot(hb, wa_ref[...],
                                    preferred_element_type=jnp.float32).astype(p_ref.dtype)
        p_ref[rows, n_a:] = jnp.dot(hb, wu_ref[...],
                                    preferred_element_type=jnp.float32).astype(p_ref.dtype)
        gt = lax.dot_general(wgt_ref[...], hb, NT_DIMS, preferred_element_type=jnp.float32)
        gt_ref[:, rows] = gt[:N_GATES]


def _in_proj(x2, g1, w_a, w_u, wg_t):
    T = x2.shape[0]
    return pl.pallas_call(
        _in_proj_kernel,
        grid=(T // (PROJ_SUB * TM_PROJ),),
        in_specs=[
            pl.BlockSpec((PROJ_SUB * TM_PROJ, D_MODEL), lambda i: (i, 0)),
            pl.BlockSpec((1, D_MODEL), lambda i: (0, 0)),
            pl.BlockSpec(w_a.shape, lambda i: (0, 0)),
            pl.BlockSpec(w_u.shape, lambda i: (0, 0)),
            pl.BlockSpec((BF16_SUBLANES, D_MODEL), lambda i: (0, 0)),
        ],
        out_specs=[
            pl.BlockSpec((PROJ_SUB * TM_PROJ, N_MAIN), lambda i: (i, 0)),
            pl.BlockSpec((N_GATES, PROJ_SUB * TM_PROJ), lambda i: (0, i)),
        ],
        out_shape=[
            jax.ShapeDtypeStruct((T, N_MAIN), jnp.bfloat16),
            jax.ShapeDtypeStruct((N_GATES, T), jnp.float32),
        ],
        compiler_params=pltpu.CompilerParams(
            dimension_semantics=("parallel",), vmem_limit_bytes=VMEM_LIMIT),
        name="in_proj",
    )(x2, g1, w_a, w_u, wg_t)


def _mlstm_kernel(qk_ref, qkp_ref, v_ref, o_ref, gt_ref, convw_ref, gb_ref, hng_ref,
                  tri_ref, shift_ref, hshift_ref, y_ref, cn_ref, m_ref):
    L = CHUNK
    c = pl.program_id(1)

    @pl.when(c == 0)
    def _():
        cn_ref[...] = jnp.zeros_like(cn_ref)
        m_ref[...] = jnp.zeros_like(m_ref)

    row_id = lax.broadcasted_iota(jnp.int32, (L, L), 0)
    col_id = lax.broadcasted_iota(jnp.int32, (L, L), 1)
    causal = col_id <= row_id
    ones_blk = jnp.ones((L, HEAD_DIM), jnp.bfloat16)
    lane = lax.broadcasted_iota(jnp.int32, (MLSTM_HEADS, L), 1)

    gate_terms = []
    for bb in range(MLSTM_BATCH):
        gt = gt_ref[bb] + gb_ref[...]
        f = gt[MLSTM_HEADS:]
        lf = jnp.minimum(f, 0.0) - jnp.log(1.0 + jnp.exp(-jnp.abs(f)))
        ig = gt[:MLSTM_HEADS]
        b_rows = lax.dot_general(lf, tri_ref[...], NT_DIMS, precision=lax.Precision.HIGHEST,
                                 preferred_element_type=jnp.float32)
        c_rows = ig - b_rows
        cm_rows = c_rows
        d = 1
        while d < L:
            cm_rows = jnp.maximum(
                cm_rows, jnp.where(lane >= d, pltpu.roll(cm_rows, d, axis=1), -jnp.inf))
            d *= 2
        gate_terms.append((b_rows, c_rows, cm_rows))

    conv_terms = []
    for bb in range(MLSTM_BATCH):
        x_cur = qk_ref[bb]
        x_prev = jnp.where(c > 0, qkp_ref[bb], jnp.zeros((HALO, 2 * MLSTM_WIDTH), jnp.bfloat16))
        acc = convw_ref[CONV_WIDTH - 1:CONV_WIDTH, :] * x_cur.astype(jnp.float32)
        for j in range(CONV_WIDTH - 1):
            sh = jnp.dot(shift_ref[j], x_cur, preferred_element_type=jnp.float32)
            top = sh[:8] + jnp.dot(hshift_ref[j], x_prev, preferred_element_type=jnp.float32)
            sh = jnp.concatenate([top, sh[8:]], axis=0)
            acc = acc + convw_ref[j:j + 1, :] * sh
        qk = acc * _sigmoid(acc)
        q_all = qk[:, :MLSTM_WIDTH].astype(jnp.bfloat16)
        k_t = jnp.transpose(qk[:, MLSTM_WIDTH:] * (HEAD_DIM ** -0.5))
        conv_terms.append((q_all, k_t))

    for bb in range(MLSTM_BATCH):
        b_rows, c_rows, cm_rows = gate_terms[bb]
        q_all, k_t = conv_terms[bb]
        m_in4 = jnp.concatenate(
            [m_ref[bb * MLSTM_HEADS + h][0:1, 0:1] for h in range(MLSTM_HEADS)], axis=0)
        mx_rows = jnp.maximum(cm_rows, m_in4)
        inter_rows = jnp.exp(m_in4 - mx_rows)
        einv_rows = jnp.exp(-(b_rows + mx_rows))
        fac_t = jnp.transpose(jnp.concatenate(
            [mx_rows, inter_rows, einv_rows, jnp.zeros_like(mx_rows)], axis=0))

        for h in range(MLSTM_HEADS):
            lo = h * HEAD_DIM
            st = bb * MLSTM_HEADS + h
            q = q_all[:, lo:lo + HEAD_DIM]
            kt = k_t[lo:lo + HEAD_DIM, :]
            v_ext = jnp.concatenate([v_ref[bb, :, lo:lo + HEAD_DIM], ones_blk], axis=1)
            mx_col = fac_t[:, h:h + 1]
            inter_col = fac_t[:, MLSTM_HEADS + h:MLSTM_HEADS + h + 1]
            einv_col = fac_t[:, 2 * MLSTM_HEADS + h:2 * MLSTM_HEADS + h + 1]
            c_row = c_rows[h:h + 1, :]
            b_tot = b_rows[h:h + 1, L - 1:L]
            cm_tot = cm_rows[h:h + 1, L - 1:L]
            m_in = m_ref[st][0:1, 0:1]
            cn = cn_ref[st]

            s_qk = jnp.dot(q, kt.astype(jnp.bfloat16), preferred_element_type=jnp.float32)
            s = (s_qk * jnp.exp(jnp.where(causal, c_row - mx_col, -jnp.inf))).astype(jnp.bfloat16)
            num = (jnp.dot(s, v_ext, preferred_element_type=jnp.float32)
                   + inter_col * jnp.dot(q, cn.astype(jnp.bfloat16),
                                         preferred_element_type=jnp.float32))
            den = num[:, HEAD_DIM:]
            hh = num[:, :HEAD_DIM] / jnp.maximum(jnp.abs(den), einv_col)

            mu = jnp.mean(hh, axis=-1, keepdims=True)
            dv = hh - mu
            var = jnp.mean(dv * dv, axis=-1, keepdims=True)
            hn = dv * lax.rsqrt(var + EPS) * hng_ref[:, lo:lo + HEAD_DIM]
            og = _sigmoid(o_ref[bb, :, lo:lo + HEAD_DIM].astype(jnp.float32))
            y_ref[bb, :, lo:lo + HEAD_DIM] = (og * hn).astype(y_ref.dtype)

            m_loc = b_tot + cm_tot
            kw_t = (kt * jnp.exp(c_row - cm_tot)).astype(jnp.bfloat16)
            c_loc = jnp.dot(kw_t, v_ext, preferred_element_type=jnp.float32)
            m_new = jnp.maximum(b_tot + m_in, m_loc)
            s_old = jnp.exp(b_tot + m_in - m_new)
            s_loc = jnp.exp(m_loc - m_new)
            cn_ref[st] = s_old * cn + s_loc * c_loc
            m_ref[st] = jnp.broadcast_to(m_new, m_ref.shape[1:])


def _mlstm(p3, gates_b, conv_w, gate_b, hn_g, tri, shifts, halo_shifts):
    batch, seq, _ = p3.shape
    L = CHUNK
    BB = MLSTM_BATCH
    halo_per_chunk = L // HALO
    return pl.pallas_call(
        _mlstm_kernel,
        grid=(batch // BB, seq // L),
        in_specs=[
            pl.BlockSpec((BB, L, 2 * MLSTM_WIDTH), lambda bi, ci: (bi, ci, 0)),
            pl.BlockSpec((BB, HALO, 2 * MLSTM_WIDTH),
                         lambda bi, ci: (bi, jnp.maximum(ci * halo_per_chunk - 1, 0), 0)),
            pl.BlockSpec((BB, L, MLSTM_WIDTH), lambda bi, ci: (bi, ci, 2)),
            pl.BlockSpec((BB, L, MLSTM_WIDTH), lambda bi, ci: (bi, ci, 3)),
            pl.BlockSpec((BB, N_GATES, L), lambda bi, ci: (bi, 0, ci)),
            pl.BlockSpec((CONV_WIDTH, 2 * MLSTM_WIDTH), lambda bi, ci: (0, 0)),
            pl.BlockSpec((N_GATES, 1), lambda bi, ci: (0, 0)),
            pl.BlockSpec((1, MLSTM_WIDTH), lambda bi, ci: (0, 0)),
            pl.BlockSpec((L, L), lambda bi, ci: (0, 0)),
            pl.BlockSpec((CONV_WIDTH - 1, L, L), lambda bi, ci: (0, 0, 0)),
            pl.BlockSpec((CONV_WIDTH - 1, 8, HALO), lambda bi, ci: (0, 0, 0)),
        ],
        out_specs=pl.BlockSpec((BB, L, MLSTM_WIDTH), lambda bi, ci: (bi, ci, 0)),
        out_shape=jax.ShapeDtypeStruct((batch, seq, MLSTM_WIDTH), jnp.bfloat16),
        scratch_shapes=[
            pltpu.VMEM((BB * MLSTM_HEADS, HEAD_DIM, 2 * HEAD_DIM), jnp.float32),
            pltpu.VMEM((BB * MLSTM_HEADS, 8, LANES), jnp.float32),
        ],
        compiler_params=pltpu.CompilerParams(
            dimension_semantics=("parallel", "arbitrary"), vmem_limit_bytes=VMEM_LIMIT),
        name="mlstm",
    )(p3, p3, p3, p3, gates_b, conv_w, gate_b, hn_g, tri, shifts, halo_shifts)


def _out_route_kernel(seq, x_ref, ym_ref, u_ref, up_ref, pw_ref, ps_ref, wo_ref, g2_ref,
                      wrt_ref, br_ref, before_ref, x1_ref, h2_ref, idx_ref, gate_ref, rank_ref, cnt_ref,
                      carry_ref):
    TM = TM_PROJ
    R = ROUTE_SUB * TM
    i = pl.program_id(0)

    @pl.when(i == 0)
    def _():
        carry_ref[...] = jnp.zeros_like(carry_ref)

    pos0 = (i * R) % seq
    e_id = lax.broadcasted_iota(jnp.int32, (N_EXPERTS, TM), 0).astype(jnp.float32)
    carry = carry_ref[...]
    subs = [slice(sub * TM, (sub + 1) * TM) for sub in range(ROUTE_SUB)]

    halo = jnp.where(pos0 > 0, up_ref[...].astype(jnp.float32), 0.0)
    u_ext = jnp.concatenate([halo, u_ref[...].astype(jnp.float32)], axis=0)
    win_sums = []
    for gi, w in enumerate(POOL_WINDOWS):
        sw = u_ext[:, gi * POOL_GROUP_DIM:(gi + 1) * POOL_GROUP_DIM]
        span = 1
        while span < w:
            sw = sw + pltpu.roll(sw, span, axis=0)
            span *= 2
        win_sums.append(sw)
    y_cats = []
    for sub, rows in enumerate(subs):
        r0 = sub * TM
        pos = (pos0 + r0 + lax.broadcasted_iota(jnp.int32, (TM, 1), 0) + 1).astype(jnp.float32)
        mixed = []
        for gi, w in enumerate(POOL_WINDOWS):
            lo = gi * POOL_GROUP_DIM
            tok = u_ext[HALO + r0:HALO + r0 + TM, lo:lo + POOL_GROUP_DIM]
            pooled = win_sums[gi][HALO + r0:HALO + r0 + TM] / jnp.minimum(pos, float(w)) - tok
            mg = jnp.dot(pooled.astype(jnp.bfloat16), pw_ref[gi],
                         preferred_element_type=jnp.float32)
            mixed.append((mg * ps_ref[:, lo:lo + POOL_GROUP_DIM]).astype(jnp.bfloat16))
        y_cats.append(jnp.concatenate([ym_ref[rows, :]] + mixed, axis=1))

    all_logits = []
    for sub, rows in enumerate(subs):
        r0 = sub * TM
        x1 = x_ref[rows, :] + jnp.dot(y_cats[sub], wo_ref[...], preferred_element_type=jnp.float32)
        x1_ref[rows, :] = x1
        h2 = x1 * lax.rsqrt(jnp.mean(x1 * x1, axis=-1, keepdims=True) + EPS) * g2_ref[...]
        h2b = h2.astype(jnp.bfloat16)
        h2w = _pack_bf16_pairs(h2)
        for s in range(PSLAB):
            h2_ref[pl.ds(r0 * PSLAB + s, TM, stride=PSLAB), :] = h2w[:, s * LANES:(s + 1) * LANES]
        all_logits.append(lax.dot_general(wrt_ref[...], h2b, NT_DIMS,
                                          preferred_element_type=jnp.float32) + br_ref[...])

    for sub, rows in enumerate(subs):
        work = all_logits[sub]
        vals, ids, hots = [], [], []
        for _ in range(TOP_K):
            mk = jnp.max(work, axis=0, keepdims=True)
            ik = jnp.min(jnp.where(work == mk, e_id, float(N_EXPERTS)), axis=0, keepdims=True)
            hot = e_id == ik
            work = jnp.where(hot, -jnp.inf, work)
            vals.append(mk)
            ids.append(ik)
            hots.append(hot)
        ex = [jnp.exp(vk - vals[0]) for vk in vals]
        denom = ex[0] + ex[1] + ex[2] + ex[3]
        gate_ref[:, rows] = jnp.concatenate([e / denom for e in ex], axis=0)
        idx_ref[:, rows] = jnp.concatenate(ids, axis=0).astype(jnp.int32)

        sel_f = sum(jnp.where(hot, 1.0, 0.0) for hot in hots)
        prefix = jnp.dot(sel_f.astype(jnp.bfloat16), before_ref[...],
                         preferred_element_type=jnp.float32)
        rank_e = carry[:, 0:1] + prefix
        ranks = [jnp.sum(jnp.where(hot, rank_e, 0.0), axis=0, keepdims=True) for hot in hots]
        rank_ref[:, rows] = jnp.concatenate(ranks, axis=0).astype(jnp.int32)
        carry = carry + jnp.sum(sel_f, axis=1, keepdims=True)
    carry_ref[...] = carry
    cnt_ref[...] = carry.astype(jnp.int32)


def _out_route(x2, ym, p, pool_w, pool_s, w_out, g2, wr_t, br, before, seq):
    T = x2.shape[0]
    TM = ROUTE_SUB * TM_PROJ
    nt = T // TM
    u_blk = N_MAIN // POOL_WIDTH - 1
    halo_per_tile = TM // HALO
    tok_spec = pl.BlockSpec((TOP_K, TM), lambda i: (0, i))
    return pl.pallas_call(
        functools.partial(_out_route_kernel, seq),
        grid=(nt,),
        in_specs=[
            pl.BlockSpec((TM, D_MODEL), lambda i: (i, 0)),
            pl.BlockSpec((TM, MLSTM_WIDTH), lambda i: (i, 0)),
            pl.BlockSpec((TM, POOL_WIDTH), lambda i: (i, u_blk)),
            pl.BlockSpec((HALO, POOL_WIDTH),
                         lambda i: (jnp.maximum(i * halo_per_tile - 1, 0), u_blk)),
            pl.BlockSpec((len(POOL_WINDOWS), POOL_GROUP_DIM, POOL_GROUP_DIM), lambda i: (0, 0, 0)),
            pl.BlockSpec((1, POOL_WIDTH), lambda i: (0, 0)),
            pl.BlockSpec((D_MODEL, D_MODEL), lambda i: (0, 0)),
            pl.BlockSpec((1, D_MODEL), lambda i: (0, 0)),
            pl.BlockSpec((N_EXPERTS, D_MODEL), lambda i: (0, 0)),
            pl.BlockSpec((N_EXPERTS, 1), lambda i: (0, 0)),
            pl.BlockSpec((TM_PROJ, TM_PROJ), lambda i: (0, 0)),
        ],
        out_specs=[
            pl.BlockSpec((TM, D_MODEL), lambda i: (i, 0)),
            pl.BlockSpec((TM * PSLAB, LANES), lambda i: (i, 0)),
            tok_spec, tok_spec, tok_spec,
            pl.BlockSpec((N_EXPERTS, LANES), lambda i: (0, 0)),
        ],
        out_shape=[
            jax.ShapeDtypeStruct((T, D_MODEL), jnp.float32),
            jax.ShapeDtypeStruct((T * PSLAB, LANES), jnp.uint32),
            jax.ShapeDtypeStruct((TOP_K, T), jnp.int32),
            jax.ShapeDtypeStruct((TOP_K, T), jnp.float32),
            jax.ShapeDtypeStruct((TOP_K, T), jnp.int32),
            jax.ShapeDtypeStruct((N_EXPERTS, LANES), jnp.int32),
        ],
        scratch_shapes=[
            pltpu.VMEM((N_EXPERTS, LANES), jnp.float32),
        ],
        compiler_params=pltpu.CompilerParams(
            dimension_semantics=("arbitrary",), vmem_limit_bytes=VMEM_LIMIT),
        name="out_route",
    )(x2, ym, p, p, pool_w, pool_s, w_out, g2, wr_t, br, before)


def _plan(dest_flat, fill):
    n_assign = dest_flat.shape[0]
    n_table = fill.shape[0]
    mesh = plsc.VectorSubcoreMesh(core_axis_name="c", subcore_axis_name="s")

    @pl.kernel(out_type=jax.ShapeDtypeStruct((n_table,), jnp.int32), mesh=mesh,
               scratch_types=[pltpu.VMEM((n_table,), jnp.int32),
                              pltpu.VMEM((PLAN_CHUNK,), jnp.int32)],
               compiler_params=pltpu.CompilerParams(needs_layout_passes=False))
    def plan_kernel(dest_hbm, fill_hbm, out_hbm, table, chunk):
        first = jnp.logical_and(lax.axis_index("c") == 0, lax.axis_index("s") == 0)

        @pl.when(first)
        def _():
            pltpu.sync_copy(fill_hbm, table)

            @pl.loop(0, n_assign // PLAN_CHUNK)
            def _(ci):
                pltpu.sync_copy(dest_hbm.at[pl.ds(ci * PLAN_CHUNK, PLAN_CHUNK)], chunk)

                @pl.loop(0, PLAN_CHUNK // (SC_LANES * PLAN_UNROLL))
                def _(i):
                    for j in range(PLAN_UNROLL):
                        off = (i * PLAN_UNROLL + j) * SC_LANES
                        idx = chunk[pl.ds(off, SC_LANES)]
                        vals = (ci * PLAN_CHUNK + off
                                + lax.broadcasted_iota(jnp.int32, (SC_LANES,), 0))
                        plsc.store_scatter(table, [idx], vals)

            pltpu.sync_copy(table, out_hbm)

    return plan_kernel(dest_flat, fill)


def _expert_kernel(n_tok, bs_ref, slot_ref, h2_ref, wg_ref, bg_ref, wu_ref, bu_ref, wd_ref, bd_ref,
                   yt_ref, *scratch):
    TM = TM_EXPERT
    ROWS = TM * PSLAB
    e = pl.program_id(0)
    n_total = bs_ref[N_EXPERTS]
    xg = scratch[:NBUF]
    ys = scratch[NBUF:2 * NBUF]
    wgb_ref, wub_ref, wdb_ref, gsem, ssem = scratch[2 * NBUF:]

    def token_of(a):
        return a & (n_tok - 1) if n_tok & (n_tok - 1) == 0 else lax.rem(a, n_tok)

    def gather_row(base, par, r):
        t = token_of(slot_ref[base + r])
        pltpu.make_async_copy(h2_ref.at[pl.ds(pl.multiple_of(t * PSLAB, PSLAB), PSLAB), :],
                              xg[par].at[pl.ds(pl.multiple_of(r * PSLAB, PSLAB), PSLAB), :],
                              gsem.at[par]).start()

    def start_gather(blk, par, unrolled=True):
        base = (blk + 1) * TM
        if unrolled:
            for r in range(TM):
                gather_row(base, par, r)
        else:
            lax.fori_loop(0, TM, lambda r, c: (gather_row(base, par, r), c)[1], 0)

    def wait_gather(par):
        pltpu.make_async_copy(h2_ref.at[pl.ds(0, ROWS), :], xg[0], gsem.at[par]).wait()

    def scatter_row(base, par, r):
        a = slot_ref[base + r]
        pltpu.make_async_copy(ys[par].at[pl.ds(pl.multiple_of(r * PSLAB, PSLAB), PSLAB), :],
                              yt_ref.at[pl.ds(pl.multiple_of(a * PSLAB, PSLAB), PSLAB), :],
                              ssem.at[par]).start()

    def start_scatter(blk, par, unrolled=True):
        base = (blk + 1) * TM
        if unrolled:
            for r in range(TM):
                scatter_row(base, par, r)
        else:
            lax.fori_loop(0, TM, lambda r, c: (scatter_row(base, par, r), c)[1], 0)

    def wait_scatter(par):
        pltpu.make_async_copy(ys[0], yt_ref.at[pl.ds(0, ROWS), :], ssem.at[par]).wait()

    @pl.when(e == 0)
    def _():
        for blk in range(NBUF - 1):
            start_gather(blk, blk, unrolled=False)
        for par in range(NBUF):
            ys[par][...] = jnp.zeros_like(ys[par])
            dump = yt_ref.at[pl.ds((n_tok * TOP_K + par * TM) * PSLAB, ROWS), :]
            cp = pltpu.make_async_copy(ys[par], dump, ssem.at[par])
            cp.start()
            cp.wait()

    wgb_ref[...] = wg_ref[0].astype(jnp.bfloat16)
    wub_ref[...] = wu_ref[0].astype(jnp.bfloat16)
    wdb_ref[...] = wd_ref[0].astype(jnp.bfloat16)

    def block_step(g, par):
        prv = (par + NBUF - 1) % NBUF
        wait_gather(par)

        @pl.when(g >= NBUF - 1)
        def _():
            wait_scatter(par)

        start_gather(g + NBUF - 1, prv)
        start_scatter(g - 1, prv)
        words = [xg[par][pl.ds(s, TM, stride=PSLAB), :] for s in range(PSLAB)]
        x = jnp.concatenate([_unpack_lo(w).astype(jnp.bfloat16) for w in words]
                            + [_unpack_hi(w).astype(jnp.bfloat16) for w in words], axis=1)
        gate = jnp.dot(x, wgb_ref[...], preferred_element_type=jnp.float32) + bg_ref[0]
        up = jnp.dot(x, wub_ref[...], preferred_element_type=jnp.float32) + bu_ref[0]
        gate = jnp.minimum(gate, SWIGLU_LIMIT)
        up = jnp.clip(up, -SWIGLU_LIMIT, SWIGLU_LIMIT)
        glu = gate * _sigmoid(SWIGLU_ALPHA * gate)
        act = (glu * (up + 1.0)).astype(jnp.bfloat16)
        y = jnp.dot(act, wdb_ref[...], preferred_element_type=jnp.float32) + bd_ref[0]
        packed = _pack_bf16_pairs(y)
        for s in range(PSLAB):
            ys[par][pl.ds(s, TM, stride=PSLAB), :] = packed[:, s * LANES:(s + 1) * LANES]

    def body(g, carry):
        for par in range(NBUF):
            pl.when(g % NBUF == par)(functools.partial(block_step, g, par))
        return carry

    lax.fori_loop(bs_ref[e], bs_ref[e + 1], body, 0)

    @pl.when(e == N_EXPERTS - 1)
    def _():
        g = n_total
        for par in range(NBUF):
            @pl.when((g - 1) % NBUF == par)
            def _():
                start_scatter(g - 1, par, unrolled=False)
        for j in range(NBUF - 1):
            wait_gather((g + j) % NBUF)
        wait_scatter((g - 1) % NBUF)
        for j in range(2, NBUF + 1):
            @pl.when(g >= j - 1)
            def _():
                wait_scatter((g + NBUF - j) % NBUF)


def _experts(block_start, slot_buf, h2_slab, w_gate, b_gate, w_up, b_up, w_down, b_down, n_tok):
    TM = TM_EXPERT
    n_assign = n_tok * TOP_K
    w_spec = pl.BlockSpec((1, D_MODEL, D_FF), lambda e, bs, sl: (e, 0, 0))
    bias_spec = pl.BlockSpec((1, 1, D_FF), lambda e, bs, sl: (e, 0, 0))
    buf = pltpu.VMEM((TM * PSLAB, LANES), jnp.uint32)
    grid_spec = pltpu.PrefetchScalarGridSpec(
        num_scalar_prefetch=2,
        grid=(N_EXPERTS,),
        in_specs=[
            pl.BlockSpec(memory_space=pl.ANY),
            w_spec, bias_spec, w_spec, bias_spec, w_spec, bias_spec,
        ],
        out_specs=pl.BlockSpec(memory_space=pl.ANY),
        scratch_shapes=[
            *([buf] * (2 * NBUF)),
            pltpu.VMEM((D_MODEL, D_FF), jnp.bfloat16),
            pltpu.VMEM((D_MODEL, D_FF), jnp.bfloat16),
            pltpu.VMEM((D_FF, D_MODEL), jnp.bfloat16),
            pltpu.SemaphoreType.DMA((NBUF,)),
            pltpu.SemaphoreType.DMA((NBUF,)),
        ],
    )
    return pl.pallas_call(
        functools.partial(_expert_kernel, n_tok),
        grid_spec=grid_spec,
        out_shape=jax.ShapeDtypeStruct(((n_assign + NBUF * TM) * PSLAB, LANES), jnp.uint32),
        compiler_params=pltpu.CompilerParams(
            dimension_semantics=("arbitrary",), vmem_limit_bytes=VMEM_LIMIT),
        name="experts",
    )(block_start, slot_buf, h2_slab, w_gate, b_gate, w_up, b_up, w_down, b_down)


def _combine_kernel(normalize, x1_ref, y0_ref, y1_ref, y2_ref, y3_ref, gate_ref, g_ref, o_ref):
    TM = TM_COMBINE
    gates = jnp.concatenate([gate_ref[...], jnp.zeros((8 - TOP_K, TM), jnp.float32)], axis=0)
    g_cols = jnp.transpose(gates)
    g_bc = [jnp.broadcast_to(g_cols[:, k:k + 1], (TM, LANES)) for k in range(TOP_K)]
    ssq = jnp.zeros((TM, LANES), jnp.float32)
    parts = [x1_ref[:, s * LANES:(s + 1) * LANES] for s in range(SLAB)]
    for s in range(PSLAB):
        for k, y_ref in enumerate((y0_ref, y1_ref, y2_ref, y3_ref)):
            w = y_ref[pl.ds(s, TM, stride=PSLAB), :]
            parts[s] = parts[s] + g_bc[k] * _unpack_lo(w)
            parts[PSLAB + s] = parts[PSLAB + s] + g_bc[k] * _unpack_hi(w)
    for acc in parts:
        ssq = ssq + acc * acc
    if normalize:
        inv = lax.rsqrt(jnp.sum(ssq, axis=-1, keepdims=True) * (1.0 / D_MODEL) + EPS)
        for s in range(SLAB):
            o_ref[:, s * LANES:(s + 1) * LANES] = parts[s] * inv * g_ref[:, s * LANES:(s + 1) * LANES]
    else:
        for s in range(SLAB):
            o_ref[:, s * LANES:(s + 1) * LANES] = parts[s]


def _combine(x1, y_tok, gate_t, gf, normalize):
    T = x1.shape[0]
    TM = TM_COMBINE
    nt = T // TM

    def y_spec(k):
        return pl.BlockSpec((TM * PSLAB, LANES), lambda i: (k * nt + i, 0))

    return pl.pallas_call(
        functools.partial(_combine_kernel, normalize),
        grid=(nt,),
        in_specs=[
            pl.BlockSpec((TM, D_MODEL), lambda i: (i, 0)),
            y_spec(0), y_spec(1), y_spec(2), y_spec(3),
            pl.BlockSpec((TOP_K, TM), lambda i: (0, i)),
            pl.BlockSpec((1, D_MODEL), lambda i: (0, 0)),
        ],
        out_specs=pl.BlockSpec((TM, D_MODEL), lambda i: (i, 0)),
        out_shape=jax.ShapeDtypeStruct((T, D_MODEL), jnp.float32),
        compiler_params=pltpu.CompilerParams(
            dimension_semantics=("parallel",), vmem_limit_bytes=VMEM_LIMIT),
        name="combine",
    )(x1, y_tok, y_tok, y_tok, y_tok, gate_t, gf)


def kernel(x, norm1_g, w_in, ig_b, fg_b, conv_w, head_norm_g, pool_w, pool_scale, w_out, norm2_g,
           w_router, b_router, w_gate, b_gate, w_up, b_up, w_down, b_down, normf_g):
    B, S, D = x.shape
    T = B * S
    depth = norm1_g.shape[0]
    W = MLSTM_WIDTH
    f32, bf16 = jnp.float32, jnp.bfloat16

    L = CHUNK
    t_l = lax.broadcasted_iota(jnp.int32, (L, L), 0)
    t_r = lax.broadcasted_iota(jnp.int32, (L, L), 1)
    tri = (t_r <= t_l).astype(f32)
    shifts = jnp.stack([(t_l - t_r == CONV_WIDTH - 1 - j).astype(bf16)
                        for j in range(CONV_WIDTH - 1)])
    h_t = lax.broadcasted_iota(jnp.int32, (8, HALO), 0)
    h_r = lax.broadcasted_iota(jnp.int32, (8, HALO), 1)
    halo_shifts = jnp.stack([(h_r - HALO - h_t == -(CONV_WIDTH - 1 - j)).astype(bf16)
                             for j in range(CONV_WIDTH - 1)])

    t_a = lax.broadcasted_iota(jnp.int32, (TM_PROJ, TM_PROJ), 0)
    t_b = lax.broadcasted_iota(jnp.int32, (TM_PROJ, TM_PROJ), 1)
    before = (t_a < t_b).astype(bf16)

    n_assign = T * TOP_K
    n_blocks = -(-n_assign // TM_EXPERT) + N_EXPERTS
    n_rows = n_blocks * TM_EXPERT
    n_table = n_rows + NBUF * TM_EXPERT
    fill = n_assign + ((jnp.arange(n_table, dtype=jnp.int32) + (NBUF - 1) * TM_EXPERT)
                       % (NBUF * TM_EXPERT))
    x2 = x.reshape(T, D)
    for l in range(depth):
        w = w_in[l]
        w_a = w[:, :4 * W].astype(bf16)
        w_u = w[:, 4 * W + N_GATES:].astype(bf16)
        wg_t = jnp.zeros((BF16_SUBLANES, D), bf16).at[:N_GATES].set(
            w[:, 4 * W:4 * W + N_GATES].T.astype(bf16))
        p, gates_t = _in_proj(x2, norm1_g[l][None, :], w_a, w_u, wg_t)

        gate_b = jnp.concatenate([ig_b[l], fg_b[l]])[:, None].astype(f32)
        gates_b = gates_t.reshape(N_GATES, B, S).transpose(1, 0, 2)
        ym = _mlstm(p.reshape(B, S, N_MAIN), gates_b, conv_w[l].astype(f32), gate_b,
                    head_norm_g[l][None, :], tri, shifts, halo_shifts).reshape(T, W)

        x1, h2, idx_t, gate_t, rank_t, cnt = _out_route(
            x2, ym, p, pool_w[l].astype(bf16), pool_scale[l][None, :], w_out[l].astype(bf16),
            norm2_g[l][None, :], w_router[l].T.astype(bf16), b_router[l][:, None], before, S)

        counts = cnt[:, 0]
        padded = ((counts + TM_EXPERT - 1) // TM_EXPERT) * TM_EXPERT
        padded_end = jnp.cumsum(padded)
        padded_start = padded_end - padded
        expert_ids = jnp.arange(N_EXPERTS, dtype=jnp.int32)[:, None, None]
        start_of = jnp.sum(jnp.where(idx_t[None] == expert_ids, padded_start[:, None, None], 0), axis=0)
        dest = start_of + rank_t
        block_start = jnp.concatenate(
            [jnp.zeros((1,), jnp.int32), (padded_end // TM_EXPERT).astype(jnp.int32)])

        slot_buf = _plan(dest.reshape(-1) + TM_EXPERT, fill)
        y_tok = _experts(block_start, slot_buf, h2, w_gate[l], b_gate[l][:, None, :],
                         w_up[l], b_up[l][:, None, :], w_down[l], b_down[l][:, None, :], T)
        last = l + 1 == depth
        x2 = _combine(x1, y_tok, gate_t, normf_g[None, :], last)
    return x2.reshape(B, S, D)
```
